```python
import jax, jax.numpy as jnp
from jax import lax
import numpy as np

D_MODEL = 1024
BATCH = 8
SEQ = 4096
DEPTH = 2

CHUNK = 64
D_MIX = D_MODEL

GLA_DV = 96
GLA_DK = 48
GLA_WIDTH = 3 * D_MODEL // 8
GLA_HEADS = GLA_WIDTH // GLA_DV
GLA_KEY_WIDTH = GLA_HEADS * GLA_DK
GLA_GATE_RANK = 16
GLA_GATE_TAU = 16.0

CONV_WIDTH = D_MODEL // 4
CONV_KERNEL = 31

ATT_HEAD_DIM = 64
ATT_WIDTH = 3 * D_MODEL // 8
ATT_HEADS = ATT_WIDTH // ATT_HEAD_DIM
ATT_LEFT_CHUNKS = 8
ATT_BAND_CHUNKS = ATT_LEFT_CHUNKS + 1
ATT_BAND = ATT_BAND_CHUNKS * CHUNK
MAX_REL_DIST = 128
N_REL = 2 * MAX_REL_DIST + 1

D_FF = 4 * D_MODEL

EPS = 1e-6
NEG_INF = -1e30

IN_SIZES = (
    GLA_KEY_WIDTH,
    GLA_KEY_WIDTH,
    GLA_WIDTH,
    GLA_WIDTH,
    GLA_GATE_RANK,
    2 * CONV_WIDTH,
    ATT_WIDTH,
    ATT_WIDTH,
    ATT_WIDTH,
)
D_IN = int(sum(IN_SIZES))
IN_SPLITS = [int(s) for s in np.cumsum(IN_SIZES)[:-1]]

kernel_name = "hybrid_gla_conformer_chunkattn_encoder"


def rmsnorm(x, g):
    xf = x.astype(jnp.float32)
    y = xf * lax.rsqrt(jnp.mean(xf * xf, axis=-1, keepdims=True) + EPS)
    return (y * g.astype(jnp.float32)).astype(x.dtype)


def gla_mixer(q, k, v, g, gate_lr, w_gate, b_gate, out_norm):
    dtype = v.dtype
    B, S = q.shape[:2]
    nc = S // CHUNK
    f32 = jnp.float32
    shp_k = (B, nc, CHUNK, GLA_HEADS, GLA_DK)
    qf = q.astype(f32).reshape(shp_k) * (GLA_DK ** -0.5)
    kf = k.astype(f32).reshape(shp_k)
    vf = v.astype(f32).reshape(B, nc, CHUNK, GLA_HEADS, GLA_DV)
    z = gate_lr.astype(f32) @ w_gate.astype(f32) + b_gate.astype(f32)
    log_a = (jax.nn.log_sigmoid(z) / GLA_GATE_TAU).reshape(shp_k)
    cum = jnp.cumsum(log_a, axis=2)
    end = cum[:, :, -1:]
    k_dec = kf * jnp.exp(end - cum)
    chunk_decay = jnp.exp(end[:, :, 0])
    kv = jnp.einsum('bnchk,bnchv->bnhkv', k_dec, vf)

    def step(state, inp):
        a, kv_c = inp
        state = a[..., None] * state + kv_c
        return state, state

    init = jnp.zeros((B, GLA_HEADS, GLA_DK, GLA_DV), f32)
    _, states = lax.scan(step, init, (jnp.moveaxis(chunk_decay, 1, 0), jnp.moveaxis(kv, 1, 0)))
    o = jnp.einsum('bnchk,nbhkv->bnchv', qf, states)
    o = o.reshape(B, S, GLA_HEADS, GLA_DV)
    o = o * lax.rsqrt(jnp.mean(o * o, axis=-1, keepdims=True) + EPS)
    o = o.reshape(B, S, GLA_WIDTH) * out_norm.astype(f32)
    o = o * jax.nn.silu(g.astype(f32))
    return o.astype(dtype)


def conv_mixer(u, w_dw, b_dw, ln_g, ln_b):
    dtype = u.dtype
    a, b = jnp.split(u, 2, axis=-1)
    h = a * jax.nn.sigmoid(b)
    h = lax.conv_general_dilated(
        h, w_dw.reshape(CONV_KERNEL, 1, CONV_WIDTH).astype(h.dtype),
        window_strides=(1,), padding=((CONV_KERNEL - 1, 0),),
        dimension_numbers=('NWC', 'WIO', 'NWC'), feature_group_count=CONV_WIDTH)
    hf = h.astype(jnp.float32) + b_dw.astype(jnp.float32)
    mu = jnp.mean(hf, axis=-1, keepdims=True)
    var = jnp.mean(jnp.square(hf - mu), axis=-1, keepdims=True)
    hf = (hf - mu) * lax.rsqrt(var + EPS) * ln_g.astype(jnp.float32) + ln_b.astype(jnp.float32)
    return jax.nn.silu(hf).astype(dtype)


def chunk_attention(q, k, v, rel_bias):
    dtype = v.dtype
    B, S = q.shape[:2]
    nc = S // CHUNK
    shp = (B, nc, CHUNK, ATT_HEADS, ATT_HEAD_DIM)
    qc, kc, vc = q.reshape(shp), k.reshape(shp), v.reshape(shp)
    pad = ((0, 0), (ATT_LEFT_CHUNKS, 0), (0, 0), (0, 0), (0, 0))
    kp, vp = jnp.pad(kc, pad), jnp.pad(vc, pad)
    k_band = jnp.concatenate([kp[:, w:w + nc] for w in range(ATT_BAND_CHUNKS)], axis=2)
    v_band = jnp.concatenate([vp[:, w:w + nc] for w in range(ATT_BAND_CHUNKS)], axis=2)
    scores = jnp.einsum('bnqhd,bnkhd->bnhqk', qc, k_band).astype(jnp.float32) * (ATT_HEAD_DIM ** -0.5)
    q_pos = np.arange(CHUNK)[:, None]
    k_pos = np.arange(ATT_BAND)[None, :] - ATT_LEFT_CHUNKS * CHUNK
    rel_idx = np.clip(q_pos - k_pos, -MAX_REL_DIST, MAX_REL_DIST) + MAX_REL_DIST
    bias = rel_bias.astype(jnp.float32)[:, rel_idx]
    key_chunk = np.arange(nc)[:, None] - ATT_LEFT_CHUNKS + np.repeat(np.arange(ATT_BAND_CHUNKS), CHUNK)[None, :]
    valid = key_chunk >= 0
    scores = jnp.where(valid[None, :, None, None, :], scores + bias[None, None], NEG_INF)
    p = jax.nn.softmax(scores, axis=-1).astype(dtype)
    o = jnp.einsum('bnhqk,bnkhd->bnqhd', p, v_band)
    return o.reshape(B, S, ATT_WIDTH)


def sq_relu_mlp(x, w_up, w_down):
    h = jax.nn.relu(x @ w_up)
    return (h * h) @ w_down


def _fwd_setup_inputs(seed: int = 0) -> dict:
    key = jax.random.key(seed)
    ks = jax.random.split(key, 20)
    f32 = jnp.float32
    nrm = lambda k, shape, s: (jax.random.normal(k, shape, f32) * s)
    return {
        "x": nrm(ks[0], (BATCH, SEQ, D_MODEL), 1.0),
        "norm_mix": 1.0 + nrm(ks[1], (DEPTH, D_MODEL), 0.01),
        "w_in": nrm(ks[2], (DEPTH, D_MODEL, D_IN), D_MODEL ** -0.5),
        "w_gla_gate": nrm(ks[3], (DEPTH, GLA_GATE_RANK, GLA_KEY_WIDTH), GLA_GATE_RANK ** -0.5),
        "b_gla_gate": nrm(ks[4], (DEPTH, GLA_KEY_WIDTH), 0.1),
        "gla_norm": 1.0 + nrm(ks[5], (DEPTH, GLA_WIDTH), 0.01),
        "w_dw": nrm(ks[6], (DEPTH, CONV_KERNEL, CONV_WIDTH), CONV_KERNEL ** -0.5),
        "b_dw": nrm(ks[7], (DEPTH, CONV_WIDTH), 0.01),
        "conv_ln_g": 1.0 + nrm(ks[8], (DEPTH, CONV_WIDTH), 0.01),
        "conv_ln_b": nrm(ks[9], (DEPTH, CONV_WIDTH), 0.01),
        "rel_bias": nrm(ks[10], (DEPTH, ATT_HEADS, N_REL), 0.1),
        "w_out": nrm(ks[11], (DEPTH, D_MIX, D_MODEL), D_MIX ** -0.5),
        "norm_ffn": 1.0 + nrm(ks[12], (DEPTH, D_MODEL), 0.01),
        "w_up": nrm(ks[13], (DEPTH, D_MODEL, D_FF), D_MODEL ** -0.5),
        "w_down": nrm(ks[14], (DEPTH, D_FF, D_MODEL), D_FF ** -0.5),
        "norm_final": 1.0 + nrm(ks[15], (D_MODEL,), 0.01),
    }


def _fwd_reference(x, norm_mix, w_in, w_gla_gate, b_gla_gate, gla_norm, w_dw, b_dw,
              conv_ln_g, conv_ln_b, rel_bias, w_out, norm_ffn, w_up, w_down, norm_final):
    h = x
    for l in range(DEPTH):
        xn = rmsnorm(h, norm_mix[l])
        proj = xn @ w_in[l]
        (g_q, g_k, g_v, g_g, g_lr, c_u, a_q, a_k, a_v) = jnp.split(proj, IN_SPLITS, axis=-1)
        o_gla = gla_mixer(g_q, g_k, g_v, g_g, g_lr, w_gla_gate[l], b_gla_gate[l], gla_norm[l])
        o_conv = conv_mixer(c_u, w_dw[l], b_dw[l], conv_ln_g[l], conv_ln_b[l])
        o_att = chunk_attention(a_q, a_k, a_v, rel_bias[l])
        mixed = jnp.concatenate([o_gla, o_conv, o_att], axis=-1)
        h = h + mixed @ w_out[l]
        h = h + sq_relu_mlp(rmsnorm(h, norm_ffn[l]), w_up[l], w_down[l])
    return rmsnorm(h, norm_final)


import jax as _jax
import jax.numpy as _jnp

TWIN_FORMAT = 'train_step'
FWD_PARAMS = ['x', 'norm_mix', 'w_in', 'w_gla_gate', 'b_gla_gate', 'gla_norm', 'w_dw', 'b_dw', 'conv_ln_g', 'conv_ln_b', 'rel_bias', 'w_out', 'norm_ffn', 'w_up', 'w_down', 'norm_final']
TWIN_WEIGHTS = ['norm_mix', 'w_in', 'w_gla_gate', 'b_gla_gate', 'gla_norm', 'w_dw', 'b_dw', 'conv_ln_g', 'conv_ln_b', 'rel_bias', 'w_out', 'norm_ffn', 'w_up', 'w_down', 'norm_final']
TWIN_DIFF_INPUT = 'x'
TWIN_INPUTS = ['x', 'norm_mix', 'w_in', 'w_gla_gate', 'b_gla_gate', 'gla_norm', 'w_dw', 'b_dw', 'conv_ln_g', 'conv_ln_b', 'rel_bias', 'w_out', 'norm_ffn', 'w_up', 'w_down', 'norm_final', 'loss_target', 'm_norm_mix', 'm_w_in', 'm_w_gla_gate', 'm_b_gla_gate', 'm_gla_norm', 'm_w_dw', 'm_b_dw', 'm_conv_ln_g', 'm_conv_ln_b', 'm_rel_bias', 'm_w_out', 'm_norm_ffn', 'm_w_up', 'm_w_down', 'm_norm_final', 'v_norm_mix', 'v_w_in', 'v_w_gla_gate', 'v_b_gla_gate', 'v_gla_norm', 'v_w_dw', 'v_b_dw', 'v_conv_ln_g', 'v_conv_ln_b', 'v_rel_bias', 'v_w_out', 'v_norm_ffn', 'v_w_up', 'v_w_down', 'v_norm_final']
TWIN_OUTPUTS = ['loss', 'grad_x', 'grad_norm_mix', 'grad_w_in', 'grad_w_gla_gate', 'grad_b_gla_gate', 'grad_gla_norm', 'grad_w_dw', 'grad_b_dw', 'grad_conv_ln_g', 'grad_conv_ln_b', 'grad_rel_bias', 'grad_w_out', 'grad_norm_ffn', 'grad_w_up', 'grad_w_down', 'grad_norm_final', 'delta_norm_mix', 'delta_w_in', 'delta_w_gla_gate', 'delta_b_gla_gate', 'delta_gla_norm', 'delta_w_dw', 'delta_b_dw', 'delta_conv_ln_g', 'delta_conv_ln_b', 'delta_rel_bias', 'delta_w_out', 'delta_norm_ffn', 'delta_w_up', 'delta_w_down', 'delta_norm_final', 'new_m_norm_mix', 'new_m_w_in', 'new_m_w_gla_gate', 'new_m_b_gla_gate', 'new_m_gla_norm', 'new_m_w_dw', 'new_m_b_dw', 'new_m_conv_ln_g', 'new_m_conv_ln_b', 'new_m_rel_bias', 'new_m_w_out', 'new_m_norm_ffn', 'new_m_w_up', 'new_m_w_down', 'new_m_norm_final', 'new_v_norm_mix', 'new_v_w_in', 'new_v_w_gla_gate', 'new_v_b_gla_gate', 'new_v_gla_norm', 'new_v_w_dw', 'new_v_b_dw', 'new_v_conv_ln_g', 'new_v_conv_ln_b', 'new_v_rel_bias', 'new_v_w_out', 'new_v_norm_ffn', 'new_v_w_up', 'new_v_w_down', 'new_v_norm_final']
TWIN_LEAF_KINDS = {'loss': 'loss', 'grad_x': 'grad_x', 'grad_norm_mix': 'grad_w', 'grad_w_in': 'grad_w', 'grad_w_gla_gate': 'grad_w', 'grad_b_gla_gate': 'grad_w', 'grad_gla_norm': 'grad_w', 'grad_w_dw': 'grad_w', 'grad_b_dw': 'grad_w', 'grad_conv_ln_g': 'grad_w', 'grad_conv_ln_b': 'grad_w', 'grad_rel_bias': 'grad_w', 'grad_w_out': 'grad_w', 'grad_norm_ffn': 'grad_w', 'grad_w_up': 'grad_w', 'grad_w_down': 'grad_w', 'grad_norm_final': 'grad_w', 'delta_norm_mix': 'delta_w', 'delta_w_in': 'delta_w', 'delta_w_gla_gate': 'delta_w', 'delta_b_gla_gate': 'delta_w', 'delta_gla_norm': 'delta_w', 'delta_w_dw': 'delta_w', 'delta_b_dw': 'delta_w', 'delta_conv_ln_g': 'delta_w', 'delta_conv_ln_b': 'delta_w', 'delta_rel_bias': 'delta_w', 'delta_w_out': 'delta_w', 'delta_norm_ffn': 'delta_w', 'delta_w_up': 'delta_w', 'delta_w_down': 'delta_w', 'delta_norm_final': 'delta_w', 'new_m_norm_mix': 'new_m', 'new_m_w_in': 'new_m', 'new_m_w_gla_gate': 'new_m', 'new_m_b_gla_gate': 'new_m', 'new_m_gla_norm': 'new_m', 'new_m_w_dw': 'new_m', 'new_m_b_dw': 'new_m', 'new_m_conv_ln_g': 'new_m', 'new_m_conv_ln_b': 'new_m', 'new_m_rel_bias': 'new_m', 'new_m_w_out': 'new_m', 'new_m_norm_ffn': 'new_m', 'new_m_w_up': 'new_m', 'new_m_w_down': 'new_m', 'new_m_norm_final': 'new_m', 'new_v_norm_mix': 'new_v', 'new_v_w_in': 'new_v', 'new_v_w_gla_gate': 'new_v', 'new_v_b_gla_gate': 'new_v', 'new_v_gla_norm': 'new_v', 'new_v_w_dw': 'new_v', 'new_v_b_dw': 'new_v', 'new_v_conv_ln_g': 'new_v', 'new_v_conv_ln_b': 'new_v', 'new_v_rel_bias': 'new_v', 'new_v_w_out': 'new_v', 'new_v_norm_ffn': 'new_v', 'new_v_w_up': 'new_v', 'new_v_w_down': 'new_v', 'new_v_norm_final': 'new_v'}


def _forward(args):
    return _fwd_reference(*[args[k] for k in FWD_PARAMS])


def _output_shape():
    out = _jax.eval_shape(lambda: _forward(_fwd_setup_inputs(0)))
    return out.shape, out.dtype

N_MICROBATCH = 1
ADAM_LR = 0.001
ADAM_B1 = 0.9
ADAM_B2 = 0.999
ADAM_EPS = 1e-08
ADAM_WD = 0.01
ADAM_STEP = 10
PER_EXAMPLE_BATCH_AXIS = {'x': 0, 'loss_target': 0}
SHARED_INPUTS = []
_WEIGHT_DTYPES = {'norm_mix': _jnp.float32, 'w_in': _jnp.float32, 'w_gla_gate': _jnp.float32, 'b_gla_gate': _jnp.float32, 'gla_norm': _jnp.float32, 'w_dw': _jnp.float32, 'b_dw': _jnp.float32, 'conv_ln_g': _jnp.float32, 'conv_ln_b': _jnp.float32, 'rel_bias': _jnp.float32, 'w_out': _jnp.float32, 'norm_ffn': _jnp.float32, 'w_up': _jnp.float32, 'w_down': _jnp.float32, 'norm_final': _jnp.float32}
MOMENT_SCALE = {'norm_mix': 1.323788e-01, 'w_in': 8.044618e-02, 'w_gla_gate': 2.123812e-02, 'b_gla_gate': 6.403332e-02, 'gla_norm': 9.231870e-02, 'w_dw': 1.008734e-01, 'b_dw': 2.077283e-01, 'conv_ln_g': 1.449916e-01, 'conv_ln_b': 1.165888e-01, 'rel_bias': 1.054847e-02, 'w_out': 7.785678e-02, 'norm_ffn': 1.575493e-01, 'w_up': 7.358919e-02, 'w_down': 1.332760e-01, 'norm_final': 3.257524e+01}


def _to_microbatches(a, axis):
    t = _jnp.moveaxis(a, axis, 0)
    t = t.reshape((N_MICROBATCH, t.shape[0] // N_MICROBATCH) + t.shape[1:])
    return _jnp.moveaxis(t, 1, axis + 1)


def setup_inputs(seed: int = 0) -> dict:
    inp = _fwd_setup_inputs(seed)
    key = _jax.random.fold_in(_jax.random.key(seed), 7919)
    shape, _ = _output_shape()
    out = dict(inp)
    out["loss_target"] = _jax.random.normal(_jax.random.fold_in(key, 0), shape, _jnp.float32)
    for i, name in enumerate(TWIN_WEIGHTS):
        w = inp[name].astype(_jnp.float32)
        if MOMENT_SCALE is None:
            s = _jnp.sqrt(_jnp.mean(_jnp.square(w)) + 1e-30)
        else:
            s = MOMENT_SCALE[name]
        km, kv = _jax.random.split(_jax.random.fold_in(key, i + 1))
        out[name] = w
        out["m_" + name] = s * _jax.random.normal(km, w.shape, _jnp.float32)
        out["v_" + name] = (s * s) * _jax.random.uniform(kv, w.shape, _jnp.float32, 0.5, 1.5)
    if N_MICROBATCH > 1:
        for name, axis in PER_EXAMPLE_BATCH_AXIS.items():
            out[name] = _to_microbatches(out[name], axis)
    return {'x': out['x'], 'norm_mix': out['norm_mix'], 'w_in': out['w_in'], 'w_gla_gate': out['w_gla_gate'], 'b_gla_gate': out['b_gla_gate'], 'gla_norm': out['gla_norm'], 'w_dw': out['w_dw'], 'b_dw': out['b_dw'], 'conv_ln_g': out['conv_ln_g'], 'conv_ln_b': out['conv_ln_b'], 'rel_bias': out['rel_bias'], 'w_out': out['w_out'], 'norm_ffn': out['norm_ffn'], 'w_up': out['w_up'], 'w_down': out['w_down'], 'norm_final': out['norm_final'], 'loss_target': out['loss_target'], 'm_norm_mix': out['m_norm_mix'], 'm_w_in': out['m_w_in'], 'm_w_gla_gate': out['m_w_gla_gate'], 'm_b_gla_gate': out['m_b_gla_gate'], 'm_gla_norm': out['m_gla_norm'], 'm_w_dw': out['m_w_dw'], 'm_b_dw': out['m_b_dw'], 'm_conv_ln_g': out['m_conv_ln_g'], 'm_conv_ln_b': out['m_conv_ln_b'], 'm_rel_bias': out['m_rel_bias'], 'm_w_out': out['m_w_out'], 'm_norm_ffn': out['m_norm_ffn'], 'm_w_up': out['m_w_up'], 'm_w_down': out['m_w_down'], 'm_norm_final': out['m_norm_final'], 'v_norm_mix': out['v_norm_mix'], 'v_w_in': out['v_w_in'], 'v_w_gla_gate': out['v_w_gla_gate'], 'v_b_gla_gate': out['v_b_gla_gate'], 'v_gla_norm': out['v_gla_norm'], 'v_w_dw': out['v_w_dw'], 'v_b_dw': out['v_b_dw'], 'v_conv_ln_g': out['v_conv_ln_g'], 'v_conv_ln_b': out['v_conv_ln_b'], 'v_rel_bias': out['v_rel_bias'], 'v_w_out': out['v_w_out'], 'v_norm_ffn': out['v_norm_ffn'], 'v_w_up': out['v_w_up'], 'v_w_down': out['v_w_down'], 'v_norm_final': out['v_norm_final']}


def _loss(weights, diff, rest, loss_target):
    with _jax.named_scope("forward"):
        args = {**rest, TWIN_DIFF_INPUT: diff, **{k: w.astype(_WEIGHT_DTYPES[k]) for k, w in weights.items()}}
        y = _forward(args)
    with _jax.named_scope("loss_head"):
        err = _jnp.square(y.astype(_jnp.float32) - loss_target)
        return 0.5 * _jnp.sum(_jnp.mean(err, axis=-1)) if err.ndim else 0.5 * err


def _adamw(w, g, m, v):
    m = ADAM_B1 * m + (1.0 - ADAM_B1) * g
    v = ADAM_B2 * v + (1.0 - ADAM_B2) * _jnp.square(g)
    m_hat = m / (1.0 - ADAM_B1 ** ADAM_STEP)
    v_hat = v / (1.0 - ADAM_B2 ** ADAM_STEP)
    delta = -ADAM_LR * (m_hat / (_jnp.sqrt(v_hat) + ADAM_EPS) + ADAM_WD * w)
    return delta, m, v


def reference(x, norm_mix, w_in, w_gla_gate, b_gla_gate, gla_norm, w_dw, b_dw, conv_ln_g, conv_ln_b, rel_bias, w_out, norm_ffn, w_up, w_down, norm_final, loss_target, m_norm_mix, m_w_in, m_w_gla_gate, m_b_gla_gate, m_gla_norm, m_w_dw, m_b_dw, m_conv_ln_g, m_conv_ln_b, m_rel_bias, m_w_out, m_norm_ffn, m_w_up, m_w_down, m_norm_final, v_norm_mix, v_w_in, v_w_gla_gate, v_b_gla_gate, v_gla_norm, v_w_dw, v_b_dw, v_conv_ln_g, v_conv_ln_b, v_rel_bias, v_w_out, v_norm_ffn, v_w_up, v_w_down, v_norm_final):
    given = dict(x=x, norm_mix=norm_mix, w_in=w_in, w_gla_gate=w_gla_gate, b_gla_gate=b_gla_gate, gla_norm=gla_norm, w_dw=w_dw, b_dw=b_dw, conv_ln_g=conv_ln_g, conv_ln_b=conv_ln_b, rel_bias=rel_bias, w_out=w_out, norm_ffn=norm_ffn, w_up=w_up, w_down=w_down, norm_final=norm_final, loss_target=loss_target, m_norm_mix=m_norm_mix, m_w_in=m_w_in, m_w_gla_gate=m_w_gla_gate, m_b_gla_gate=m_b_gla_gate, m_gla_norm=m_gla_norm, m_w_dw=m_w_dw, m_b_dw=m_b_dw, m_conv_ln_g=m_conv_ln_g, m_conv_ln_b=m_conv_ln_b, m_rel_bias=m_rel_bias, m_w_out=m_w_out, m_norm_ffn=m_norm_ffn, m_w_up=m_w_up, m_w_down=m_w_down, m_norm_final=m_norm_final, v_norm_mix=v_norm_mix, v_w_in=v_w_in, v_w_gla_gate=v_w_gla_gate, v_b_gla_gate=v_b_gla_gate, v_gla_norm=v_gla_norm, v_w_dw=v_w_dw, v_b_dw=v_b_dw, v_conv_ln_g=v_conv_ln_g, v_conv_ln_b=v_conv_ln_b, v_rel_bias=v_rel_bias, v_w_out=v_w_out, v_norm_ffn=v_norm_ffn, v_w_up=v_w_up, v_w_down=v_w_down, v_norm_final=v_norm_final)
    weights = {n: given[n] for n in TWIN_WEIGHTS}
    shared = {n: given[n] for n in SHARED_INPUTS}
    per_example = {n: given[n] for n in ['x']}
    grad_fn = _jax.value_and_grad(_loss, argnums=(0, 1))

    def one_microbatch(ex, loss_target):
        ex = dict(ex)
        diff = ex.pop(TWIN_DIFF_INPUT)
        return grad_fn(weights, diff, {**shared, **ex}, loss_target)

    if N_MICROBATCH == 1:
        loss, (grad_w, grad_x) = one_microbatch(per_example, given["loss_target"])
    else:
        def body(carry, xs):
            loss_sum, grad_sum = carry
            l_k, (gw_k, gx_k) = one_microbatch(xs[0], xs[1])
            with _jax.named_scope("update"):
                return (loss_sum + l_k, _jax.tree.map(_jnp.add, grad_sum, gw_k)), gx_k

        init = (_jnp.zeros((), _jnp.float32), _jax.tree.map(_jnp.zeros_like, weights))
        (loss, grad_w), grad_x = _jax.lax.scan(body, init, (per_example, given["loss_target"]))
    with _jax.named_scope("update"):
        delta_w, new_m, new_v = {}, {}, {}
        for n in TWIN_WEIGHTS:
            delta_w[n], new_m[n], new_v[n] = _adamw(weights[n], grad_w[n], given["m_" + n], given["v_" + n])
    return (loss, grad_x, *[grad_w[n] for n in TWIN_WEIGHTS], *[delta_w[n] for n in TWIN_WEIGHTS],
            *[new_m[n] for n in TWIN_WEIGHTS], *[new_v[n] for n in TWIN_WEIGHTS])
```

```python
import functools

import numpy as np
import jax
import jax.numpy as jnp
from jax import lax
from jax.experimental import pallas as pl
from jax.experimental.pallas import tpu as pltpu

F32 = jnp.float32
BF16 = jnp.bfloat16
HI = lax.Precision.HIGHEST

D = 1024
CHUNK = 64
GLA_DK, GLA_DV, GLA_H = 48, 96, 4
KW = 256
VW = 384
LRW = 128
GLA_TAU = 16.0
CW = 256
CK = 31
AW = 384
AH = 6
BAND = 576
LEFT = 512
D_FF = 4096
EPS = 1e-6
NEG = -1e30
N_REL = 257

GLA_COLS = 2 * KW + 2 * VW + LRW
CONV_COLS = 2 * CW
ATT_COLS = 3 * AW

ADAM_LR, ADAM_B1, ADAM_B2, ADAM_EPS, ADAM_WD, ADAM_STEP = 0.001, 0.9, 0.999, 1e-08, 0.01, 10

VMEM_CAP = 56 * 1024 * 1024
MESH = pl.DeviceIdType.MESH


def _cparams(sem, vmem_bytes):
    limit = int(min(VMEM_CAP, max(vmem_bytes * 5 // 4 + (4 << 20), 16 << 20)))
    return pltpu.CompilerParams(dimension_semantics=sem, vmem_limit_bytes=limit)


def _nbytes(shape, dtype):
    return int(np.prod(shape)) * jnp.dtype(dtype).itemsize


def _sigmoid(x):
    return 1.0 / (1.0 + jnp.exp(-x))


_DIMS = {"nn": (((1,), (0,)), ((), ())), "nt": (((1,), (1,)), ((), ())), "tn": (((0,), (0,)), ((), ()))}


def _mm(a, b, *, mode, out_dtype, name, tm=512, tn=None, tk=None, a_pro=None, epi=None, extra=None):
    if mode == "nn":
        (M, K), (K2, N) = a.shape, b.shape
    elif mode == "nt":
        (M, K), (N, K2) = a.shape, b.shape
    else:
        (K, M), (K2, N) = a.shape, b.shape
    assert K == K2, (a.shape, b.shape, mode)
    tm = min(tm, M)
    tn = N if tn is None else min(tn, N)
    tk = K if tk is None else min(tk, K)
    assert M % tm == 0 and N % tn == 0 and K % tk == 0, (M, N, K, tm, tn, tk)
    nk = K // tk
    a_blk = (tk, tm) if mode == "tn" else (tm, tk)
    a_map = (lambda i, j, k: (k, i)) if mode == "tn" else (lambda i, j, k: (i, k))
    b_blk = (tn, tk) if mode == "nt" else (tk, tn)
    b_map = (lambda i, j, k: (j, k)) if mode == "nt" else (lambda i, j, k: (k, j))
    in_specs = [pl.BlockSpec(a_blk, a_map), pl.BlockSpec(b_blk, b_map)]
    args = [a, b]
    if epi is not None:
        in_specs.append(pl.BlockSpec((tm, tn), lambda i, j, k: (i, j)))
        args.append(extra)

    def body(*refs):
        a_ref, b_ref = refs[0], refs[1]
        e_ref = refs[2] if epi is not None else None
        o_ref = refs[3] if epi is not None else refs[2]
        av = a_ref[...]
        if a_pro == "relu2":
            af = jnp.maximum(av.astype(F32), 0.0)
            av = af * af
        p = lax.dot_general(av.astype(BF16), b_ref[...].astype(BF16), _DIMS[mode], preferred_element_type=F32)

        def finish(acc):
            if epi == "add":
                acc = acc + e_ref[...].astype(F32)
            elif epi == "relu2grad":
                acc = acc * (2.0 * jnp.maximum(e_ref[...].astype(F32), 0.0))
            o_ref[...] = acc.astype(o_ref.dtype)

        if nk == 1:
            finish(p)
        else:
            acc_ref = refs[-1]
            k = pl.program_id(2)

            @pl.when(k == 0)
            def _():
                acc_ref[...] = p

            @pl.when(k > 0)
            def _():
                acc_ref[...] += p

            @pl.when(k == nk - 1)
            def _():
                finish(acc_ref[...])

    vm = 2 * (_nbytes(a_blk, a.dtype) + _nbytes(b_blk, b.dtype) + _nbytes((tm, tn), out_dtype))
    vm += 3 * _nbytes((tm, tn), F32)
    if epi is not None:
        vm += 2 * _nbytes((tm, tn), extra.dtype)
    return pl.pallas_call(
        body,
        out_shape=jax.ShapeDtypeStruct((M, N), out_dtype),
        grid=(M // tm, N // tn, nk),
        in_specs=in_specs,
        out_specs=pl.BlockSpec((tm, tn), lambda i, j, k: (i, j)),
        scratch_shapes=[pltpu.VMEM((tm, tn), F32)] if nk > 1 else [],
        compiler_params=_cparams(("parallel", "parallel", "arbitrary"), vm),
        name=name,
    )(*args)


def _rmsnorm_fwd(h, g, name, tm=512):
    T = h.shape[0]

    def body(h_ref, g_ref, o_ref):
        x = h_ref[...]
        r = lax.rsqrt(jnp.mean(x * x, axis=-1, keepdims=True) + EPS)
        o_ref[...] = (x * r * g_ref[...]).astype(o_ref.dtype)

    return pl.pallas_call(
        body,
        out_shape=jax.ShapeDtypeStruct((T, D), BF16),
        grid=(T // tm,),
        in_specs=[pl.BlockSpec((tm, D), lambda i: (i, 0)), pl.BlockSpec((1, D), lambda i: (0, 0))],
        out_specs=pl.BlockSpec((tm, D), lambda i: (i, 0)),
        compiler_params=_cparams(("parallel",), 8 * _nbytes((tm, D), F32)),
        name=name,
    )(h, g)


def _rmsnorm_bwd(dxn, h, g, dres, name, tm=512):
    T = h.shape[0]

    def body(dxn_ref, h_ref, g_ref, dres_ref, dh_ref, dg_ref):
        @pl.when(pl.program_id(0) == 0)
        def _():
            dg_ref[...] = jnp.zeros_like(dg_ref)

        x = h_ref[...]
        dy = dxn_ref[...].astype(F32)
        r = lax.rsqrt(jnp.mean(x * x, axis=-1, keepdims=True) + EPS)
        gy = dy * g_ref[...]
        dot = jnp.mean(x * gy, axis=-1, keepdims=True)
        dh_ref[...] = dres_ref[...] + r * gy - x * (r * r * r * dot)
        dg_ref[...] += jnp.sum(dy * x * r, axis=0, keepdims=True)

    row = pl.BlockSpec((tm, D), lambda i: (i, 0))
    vec = pl.BlockSpec((1, D), lambda i: (0, 0))
    return pl.pallas_call(
        body,
        out_shape=(jax.ShapeDtypeStruct((T, D), F32), jax.ShapeDtypeStruct((1, D), F32)),
        grid=(T // tm,),
        in_specs=[row, row, vec, row],
        out_specs=(row, vec),
        compiler_params=_cparams(("arbitrary",), 12 * _nbytes((tm, D), F32)),
        name=name,
    )(dxn, h, g, dres)


def _final_loss(h, g, target, name, tm=512):
    T = h.shape[0]

    def body(h_ref, g_ref, t_ref, loss_ref, dh_ref, dg_ref):
        @pl.when(pl.program_id(0) == 0)
        def _():
            dg_ref[...] = jnp.zeros_like(dg_ref)
            loss_ref[...] = jnp.zeros_like(loss_ref)

        x = h_ref[...]
        gg = g_ref[...]
        r = lax.rsqrt(jnp.mean(x * x, axis=-1, keepdims=True) + EPS)
        y = x * r * gg
        e = y - t_ref[...]
        loss_ref[...] += 0.5 * jnp.sum(jnp.mean(e * e, axis=-1, keepdims=True), axis=0, keepdims=True)
        dy = e * (1.0 / D)
        gy = dy * gg
        dot = jnp.mean(x * gy, axis=-1, keepdims=True)
        dh_ref[...] = r * gy - x * (r * r * r * dot)
        dg_ref[...] += jnp.sum(dy * x * r, axis=0, keepdims=True)

    row = pl.BlockSpec((tm, D), lambda i: (i, 0))
    vec = pl.BlockSpec((1, D), lambda i: (0, 0))
    one = pl.BlockSpec((1, 1), lambda i: (0, 0))
    return pl.pallas_call(
        body,
        out_shape=(jax.ShapeDtypeStruct((1, 1), F32), jax.ShapeDtypeStruct((T, D), F32), jax.ShapeDtypeStruct((1, D), F32)),
        grid=(T // tm,),
        in_specs=[row, vec, row],
        out_specs=(one, row, vec),
        compiler_params=_cparams(("arbitrary",), 12 * _nbytes((tm, D), F32)),
        name=name,
    )(h, g, target)


GLA_G = 8


def _gla_consts():
    i = np.arange(KW)[:, None]
    j = np.arange(VW)[None, :]
    mask = ((i // GLA_DK) == (j // GLA_DV)) & (i < GLA_H * GLA_DK)
    a = np.arange(VW)
    hm = ((a[:, None] // GLA_DV) == (a[None, :] // GLA_DV)).astype(np.float32) / GLA_DV
    c = np.arange(CHUNK)
    low = (c[:, None] >= c[None, :]).astype(np.float32)
    return jnp.asarray(mask.astype(np.float32)), jnp.asarray(hm), jnp.asarray(low)


def _gla_chunk_common(p_ref, rows, wg, bg, low, ones_v):
    q = p_ref[rows, 0:KW]
    k = p_ref[rows, KW:2 * KW]
    v = p_ref[rows, 2 * KW:2 * KW + VW]
    g = p_ref[rows, 2 * KW + VW:2 * KW + 2 * VW]
    lr = p_ref[rows, 2 * KW + 2 * VW:GLA_COLS]
    z = jnp.dot(lr, wg, precision=HI, preferred_element_type=F32) + bg
    la = (jnp.minimum(z, 0.0) - jnp.log(1.0 + jnp.exp(-jnp.abs(z)))) * (1.0 / GLA_TAU)
    cum = jnp.dot(low, la, precision=HI, preferred_element_type=F32)
    endb = cum[CHUNK - 1:CHUNK, :]
    w = jnp.exp(endb - cum)
    a_full = jnp.exp(lax.dot_general(la, ones_v, _DIMS["tn"], precision=HI, preferred_element_type=F32))
    return q, k, v, g, lr, z, w, endb, a_full


def _gla_fwd(p, wg, bg, gn, consts, name):
    T = p.shape[0]
    rb = CHUNK * GLA_G
    ng = T // rb
    mask, hm, low = consts
    scale = GLA_DK ** -0.5

    def body(p_ref, wg_ref, bg_ref, gn_ref, m_ref, hm_ref, l_ref, o_ref, st_ref, s_ref):
        @pl.when(pl.program_id(0) == 0)
        def _():
            s_ref[...] = jnp.zeros_like(s_ref)

        wg_v, bg_v, gn_v = wg_ref[...], bg_ref[...], gn_ref[...]
        ones_v = jnp.ones((CHUNK, VW), F32)
        for c in range(GLA_G):
            rows = slice(c * CHUNK, (c + 1) * CHUNK)
            q, k, v, g, _, _, w, _, a_full = _gla_chunk_common(p_ref, rows, wg_v, bg_v, l_ref[...], ones_v)
            kd = (k * w).astype(BF16)
            kv = lax.dot_general(kd, v.astype(BF16), _DIMS["tn"], preferred_element_type=F32) * m_ref[...]
            s_new = a_full * s_ref[...] + kv
            s_ref[...] = s_new
            st_ref[c] = s_new
            o = jnp.dot((q * scale).astype(BF16), s_new.astype(BF16), preferred_element_type=F32)
            ms = jnp.dot(o * o, hm_ref[...], precision=HI, preferred_element_type=F32)
            r = lax.rsqrt(ms + EPS)
            y = o * r * gn_v * (g * _sigmoid(g))
            o_ref[rows, :] = y.astype(o_ref.dtype)

    full = lambda shape: pl.BlockSpec(shape, lambda i: tuple(0 for _ in shape))
    vm = 2 * _nbytes((rb, GLA_COLS), F32) + 2 * _nbytes((GLA_G, KW, VW), F32) + 12 * _nbytes((KW, VW), F32)
    return pl.pallas_call(
        body,
        out_shape=(jax.ShapeDtypeStruct((T, VW), BF16), jax.ShapeDtypeStruct((T // CHUNK, KW, VW), F32)),
        grid=(ng,),
        in_specs=[pl.BlockSpec((rb, GLA_COLS), lambda i: (i, 0)), full((LRW, KW)), full((1, KW)), full((1, VW)),
                  full((KW, VW)), full((VW, VW)), full((CHUNK, CHUNK))],
        out_specs=(pl.BlockSpec((rb, VW), lambda i: (i, 0)), pl.BlockSpec((GLA_G, KW, VW), lambda i: (i, 0, 0))),
        scratch_shapes=[pltpu.VMEM((KW, VW), F32)],
        compiler_params=_cparams(("arbitrary",), vm),
        name=name,
    )(p, wg, bg, gn, mask, hm, low)


def _gla_bwd(p, dy, states, wg, bg, gn, consts, name):
    T = p.shape[0]
    rb = CHUNK * GLA_G
    ng = T // rb
    mask, hm, low = consts
    scale = GLA_DK ** -0.5

    def body(p_ref, dy_ref, st_ref, sp_ref, wg_ref, bg_ref, gn_ref, m_ref, hm_ref, l_ref,
             dp_ref, dwg_ref, dbg_ref, dgn_ref, ga_ref):
        step = pl.program_id(0)

        @pl.when(step == 0)
        def _():
            ga_ref[...] = jnp.zeros_like(ga_ref)
            dwg_ref[...] = jnp.zeros_like(dwg_ref)
            dbg_ref[...] = jnp.zeros_like(dbg_ref)
            dgn_ref[...] = jnp.zeros_like(dgn_ref)

        first_group = step == ng - 1
        wg_v, bg_v, gn_v = wg_ref[...], bg_ref[...], gn_ref[...]
        ones_v = jnp.ones((CHUNK, VW), F32)
        ones_8 = jnp.ones((8, VW), F32)
        for c in reversed(range(GLA_G)):
            rows = slice(c * CHUNK, (c + 1) * CHUNK)
            q, k, v, g, lr, z, w, endb, a_full = _gla_chunk_common(p_ref, rows, wg_v, bg_v, l_ref[...], ones_v)
            s_n = st_ref[c]
            if c > 0:
                s_prev = st_ref[c - 1]
            else:
                s_prev = jnp.where(first_group, 0.0, sp_ref[0])
            kd = k * w
            qs = (q * scale).astype(BF16)
            o = jnp.dot(qs, s_n.astype(BF16), preferred_element_type=F32)
            ms = jnp.dot(o * o, hm_ref[...], precision=HI, preferred_element_type=F32)
            r = lax.rsqrt(ms + EPS)
            on = o * r
            sg = _sigmoid(g)
            silu = g * sg
            dyv = dy_ref[rows, :].astype(F32)
            d_on = dyv * gn_v * silu
            dg = dyv * on * gn_v * (sg * (1.0 + g * (1.0 - sg)))
            dgn_ref[...] += jnp.sum(dyv * on * silu, axis=0, keepdims=True)
            mo = jnp.dot(o * d_on, hm_ref[...], precision=HI, preferred_element_type=F32)
            do = r * d_on - o * (r * r * r) * mo
            dob = do.astype(BF16)
            dq = lax.dot_general(dob, s_n.astype(BF16), _DIMS["nt"], preferred_element_type=F32) * scale
            g_n = lax.dot_general(qs, dob, _DIMS["tn"], preferred_element_type=F32) * m_ref[...] + ga_ref[...]
            d_a = lax.dot_general(ones_8, g_n * s_prev, _DIMS["nt"], precision=HI, preferred_element_type=F32)[0:1, :]
            g_nb = g_n.astype(BF16)
            dkd = lax.dot_general(v.astype(BF16), g_nb, _DIMS["nt"], preferred_element_type=F32)
            dv = jnp.dot(kd.astype(BF16), g_nb, preferred_element_type=F32)
            dk = dkd * w
            e = dkd * kd
            d_end = jnp.sum(e, axis=0, keepdims=True) + d_a * jnp.exp(endb)
            dla = lax.dot_general(l_ref[...], -e, _DIMS["tn"], precision=HI, preferred_element_type=F32) + d_end
            dz = dla * (1.0 - _sigmoid(z)) * (1.0 / GLA_TAU)
            dlr = lax.dot_general(dz, wg_v, _DIMS["nt"], precision=HI, preferred_element_type=F32)
            dwg_ref[...] += lax.dot_general(lr, dz, _DIMS["tn"], precision=HI, preferred_element_type=F32)
            dbg_ref[...] += jnp.sum(dz, axis=0, keepdims=True)
            ga_ref[...] = a_full * g_n
            dp_ref[rows, 0:KW] = dq
            dp_ref[rows, KW:2 * KW] = dk
            dp_ref[rows, 2 * KW:2 * KW + VW] = dv
            dp_ref[rows, 2 * KW + VW:2 * KW + 2 * VW] = dg
            dp_ref[rows, 2 * KW + 2 * VW:GLA_COLS] = dlr

    full = lambda shape: pl.BlockSpec(shape, lambda i: tuple(0 for _ in shape))
    rev = lambda i: (ng - 1 - i, 0)
    vm = 4 * _nbytes((rb, GLA_COLS), F32) + 2 * _nbytes((rb, VW), F32) + 2 * _nbytes((GLA_G + 1, KW, VW), F32)
    vm += 16 * _nbytes((KW, VW), F32)
    return pl.pallas_call(
        body,
        out_shape=(jax.ShapeDtypeStruct((T, GLA_COLS), F32), jax.ShapeDtypeStruct((LRW, KW), F32),
                   jax.ShapeDtypeStruct((1, KW), F32), jax.ShapeDtypeStruct((1, VW), F32)),
        grid=(ng,),
        in_specs=[pl.BlockSpec((rb, GLA_COLS), rev), pl.BlockSpec((rb, VW), rev),
                  pl.BlockSpec((GLA_G, KW, VW), lambda i: (ng - 1 - i, 0, 0)),
                  pl.BlockSpec((1, KW, VW), lambda i: (jnp.maximum((ng - 1 - i) * GLA_G - 1, 0), 0, 0)),
                  full((LRW, KW)), full((1, KW)), full((1, VW)), full((KW, VW)), full((VW, VW)), full((CHUNK, CHUNK))],
        out_specs=(pl.BlockSpec((rb, GLA_COLS), rev), full((LRW, KW)), full((1, KW)), full((1, VW))),
        scratch_shapes=[pltpu.VMEM((KW, VW), F32)],
        compiler_params=_cparams(("arbitrary",), vm),
        name=name,
    )(p, dy, states, states, wg, bg, gn, mask, hm, low)


CONV_TM = 512
HALO = 32
CONV_RB = 64


def _glu(u):
    a = u[:, 0:CW]
    b = u[:, CW:2 * CW]
    return a * _sigmoid(b)


def _conv_taps(buf_ref, w_ref, rb0, first_tap):
    acc = jnp.zeros((CONV_RB, CW), F32)
    for j in range(CK):
        s = rb0 + first_tap(j)
        acc = acc + w_ref[j:j + 1, :] * buf_ref[s:s + CONV_RB, :]
    return acc


def _ln_fwd(c, lg, lb):
    mu = jnp.mean(c, axis=-1, keepdims=True)
    xc = c - mu
    rstd = lax.rsqrt(jnp.mean(xc * xc, axis=-1, keepdims=True) + EPS)
    n = xc * rstd
    return n, rstd, n * lg + lb


def _conv_fwd(u, w, b, lg, lb, name):
    T = u.shape[0]
    tm = CONV_TM

    def body(u_ref, uh_ref, w_ref, b_ref, lg_ref, lb_ref, o_ref, hbuf):
        i = pl.program_id(0)
        hbuf[0:HALO, :] = jnp.where(i > 0, _glu(uh_ref[...]), 0.0)
        hbuf[HALO:HALO + tm, :] = _glu(u_ref[...])
        for r in range(tm // CONV_RB):
            acc = _conv_taps(hbuf, w_ref, r * CONV_RB, lambda j: HALO - (CK - 1) + j)
            _, _, zz = _ln_fwd(acc + b_ref[...], lg_ref[...], lb_ref[...])
            o_ref[r * CONV_RB:(r + 1) * CONV_RB, :] = (zz * _sigmoid(zz)).astype(o_ref.dtype)

    vec = pl.BlockSpec((1, CW), lambda i: (0, 0))
    return pl.pallas_call(
        body,
        out_shape=jax.ShapeDtypeStruct((T, CW), BF16),
        grid=(T // tm,),
        in_specs=[pl.BlockSpec((tm, CONV_COLS), lambda i: (i, 0)),
                  pl.BlockSpec((HALO, CONV_COLS), lambda i: (jnp.maximum(i * (tm // HALO) - 1, 0), 0)),
                  pl.BlockSpec((HALO, CW), lambda i: (0, 0)), vec, vec, vec],
        out_specs=pl.BlockSpec((tm, CW), lambda i: (i, 0)),
        scratch_shapes=[pltpu.VMEM((tm + HALO, CW), F32)],
        compiler_params=_cparams(("arbitrary",), 8 * _nbytes((tm, CONV_COLS), F32)),
        name=name,
    )(u, u, w, b, lg, lb)


def _conv_bwd_dc(u, dout, w, b, lg, lb, name):
    T = u.shape[0]
    tm = CONV_TM
    nsteps = T // tm

    def body(u_ref, uh_ref, do_ref, w_ref, b_ref, lg_ref, lb_ref, dc_ref, dw_ref, db_ref, dlg_ref, dlb_ref, hbuf, dwacc):
        i = pl.program_id(0)

        @pl.when(i == 0)
        def _():
            dwacc[...] = jnp.zeros_like(dwacc)
            db_ref[...] = jnp.zeros_like(db_ref)
            dlg_ref[...] = jnp.zeros_like(dlg_ref)
            dlb_ref[...] = jnp.zeros_like(dlb_ref)

        hbuf[0:HALO, :] = jnp.where(i > 0, _glu(uh_ref[...]), 0.0)
        hbuf[HALO:HALO + tm, :] = _glu(u_ref[...])
        for r in range(tm // CONV_RB):
            rows = slice(r * CONV_RB, (r + 1) * CONV_RB)
            acc = _conv_taps(hbuf, w_ref, r * CONV_RB, lambda j: HALO - (CK - 1) + j)
            n, rstd, zz = _ln_fwd(acc + b_ref[...], lg_ref[...], lb_ref[...])
            sg = _sigmoid(zz)
            dz = do_ref[rows, :].astype(F32) * (sg * (1.0 + zz * (1.0 - sg)))
            dlg_ref[...] += jnp.sum(dz * n, axis=0, keepdims=True)
            dlb_ref[...] += jnp.sum(dz, axis=0, keepdims=True)
            dn = dz * lg_ref[...]
            dc = rstd * (dn - jnp.mean(dn, axis=-1, keepdims=True) - n * jnp.mean(dn * n, axis=-1, keepdims=True))
            dc_ref[rows, :] = dc
            db_ref[...] += jnp.sum(dc, axis=0, keepdims=True)
            for j in range(CK):
                s = r * CONV_RB + HALO - (CK - 1) + j
                prod = dc * hbuf[s:s + CONV_RB, :]
                dwacc[j] += jnp.sum(prod.reshape(CONV_RB // 8, 8, CW), axis=0)

        @pl.when(i == nsteps - 1)
        def _():
            dw_ref[...] = jnp.sum(dwacc[...], axis=1)

    vec = pl.BlockSpec((1, CW), lambda i: (0, 0))
    return pl.pallas_call(
        body,
        out_shape=(jax.ShapeDtypeStruct((T, CW), F32), jax.ShapeDtypeStruct((HALO, CW), F32),
                   jax.ShapeDtypeStruct((1, CW), F32), jax.ShapeDtypeStruct((1, CW), F32), jax.ShapeDtypeStruct((1, CW), F32)),
        grid=(nsteps,),
        in_specs=[pl.BlockSpec((tm, CONV_COLS), lambda i: (i, 0)),
                  pl.BlockSpec((HALO, CONV_COLS), lambda i: (jnp.maximum(i * (tm // HALO) - 1, 0), 0)),
                  pl.BlockSpec((tm, CW), lambda i: (i, 0)),
                  pl.BlockSpec((HALO, CW), lambda i: (0, 0)), vec, vec, vec],
        out_specs=(pl.BlockSpec((tm, CW), lambda i: (i, 0)), pl.BlockSpec((HALO, CW), lambda i: (0, 0)), vec, vec, vec),
        scratch_shapes=[pltpu.VMEM((tm + HALO, CW), F32), pltpu.VMEM((HALO, 8, CW), F32)],
        compiler_params=_cparams(("arbitrary",), 10 * _nbytes((tm, CONV_COLS), F32)),
        name=name,
    )(u, u, dout, w, b, lg, lb)


def _conv_bwd_du(u, dc, w, name):
    T = u.shape[0]
    tm = CONV_TM
    nsteps = T // tm

    def body(u_ref, dc_ref, dch_ref, w_ref, du_ref, dcbuf):
        i = pl.program_id(0)
        dcbuf[0:tm, :] = dc_ref[...]
        dcbuf[tm:tm + HALO, :] = jnp.where(i < nsteps - 1, dch_ref[...], 0.0)
        for r in range(tm // CONV_RB):
            rows = slice(r * CONV_RB, (r + 1) * CONV_RB)
            dh = _conv_taps(dcbuf, w_ref, r * CONV_RB, lambda j: (CK - 1) - j)
            a = u_ref[rows, 0:CW]
            sb = _sigmoid(u_ref[rows, CW:2 * CW])
            du_ref[rows, 0:CW] = dh * sb
            du_ref[rows, CW:2 * CW] = dh * a * sb * (1.0 - sb)

    return pl.pallas_call(
        body,
        out_shape=jax.ShapeDtypeStruct((T, CONV_COLS), F32),
        grid=(nsteps,),
        in_specs=[pl.BlockSpec((tm, CONV_COLS), lambda i: (i, 0)),
                  pl.BlockSpec((tm, CW), lambda i: (i, 0)),
                  pl.BlockSpec((HALO, CW), lambda i: (jnp.minimum((i + 1) * (tm // HALO), T // HALO - 1), 0)),
                  pl.BlockSpec((HALO, CW), lambda i: (0, 0))],
        out_specs=pl.BlockSpec((tm, CONV_COLS), lambda i: (i, 0)),
        scratch_shapes=[pltpu.VMEM((tm + HALO, CW), F32)],
        compiler_params=_cparams(("arbitrary",), 8 * _nbytes((tm, CONV_COLS), F32)),
        name=name,
    )(u, dc, dc, w)


ATT_G = 4


def _att_load_kv(p_any, kbuf, vbuf, sems, T):
    kc = pltpu.make_async_copy(p_any.at[:, pl.ds(AW, AW)], kbuf.at[pl.ds(LEFT, T), :], sems.at[0])
    vc = pltpu.make_async_copy(p_any.at[:, pl.ds(2 * AW, AW)], vbuf.at[pl.ds(LEFT, T), :], sems.at[1])
    kc.start()
    vc.start()
    kbuf[0:LEFT, :] = jnp.zeros((LEFT, AW), BF16)
    vbuf[0:LEFT, :] = jnp.zeros((LEFT, AW), BF16)
    kc.wait()
    vc.wait()


def _att_probs(qm, kb, bias_h, n):
    sc = lax.dot_general(qm, kb, _DIMS["nt"], preferred_element_type=F32) * (64 ** -0.5) + bias_h
    pos = lax.broadcasted_iota(jnp.int32, (CHUNK, BAND), 1)
    sc = jnp.where(pos >= CHUNK * (8 - n), sc, NEG)
    mx = jnp.max(sc, axis=-1, keepdims=True)
    ex = jnp.exp(sc - mx)
    return ex / jnp.sum(ex, axis=-1, keepdims=True)


def _att_fwd(p, bias, name):
    T = p.shape[0]
    qb = CHUNK * ATT_G

    def body(q_ref, p_any, bias_ref, o_ref, kbuf, vbuf, sems):
        i = pl.program_id(0)

        @pl.when(i == 0)
        def _():
            _att_load_kv(p_any, kbuf, vbuf, sems, T)

        lane = lax.broadcasted_iota(jnp.int32, (CHUNK, 128), 1)
        for c in range(ATT_G):
            n = i * ATT_G + c
            start = pl.multiple_of(n * CHUNK, CHUNK)
            rows = slice(c * CHUNK, (c + 1) * CHUNK)
            for hp in range(AH // 2):
                cols = slice(hp * 128, (hp + 1) * 128)
                kb = kbuf[pl.ds(start, BAND), cols]
                vb = vbuf[pl.ds(start, BAND), cols]
                q2 = q_ref[rows, cols]
                o2 = jnp.zeros((CHUNK, 128), F32)
                for s in range(2):
                    mine = (lane >= 64) == (s == 1)
                    qm = jnp.where(mine, q2, jnp.zeros_like(q2))
                    pr = _att_probs(qm, kb, bias_ref[2 * hp + s], n)
                    pv = jnp.dot(pr.astype(BF16), vb, preferred_element_type=F32)
                    o2 = jnp.where(mine, pv, o2)
                o_ref[rows, cols] = o2.astype(o_ref.dtype)

    vm = 2 * _nbytes((T + LEFT, AW), BF16) + 4 * _nbytes((AH, CHUNK, BAND), F32) + (8 << 20)
    return pl.pallas_call(
        body,
        out_shape=jax.ShapeDtypeStruct((T, AW), BF16),
        grid=(T // qb,),
        in_specs=[pl.BlockSpec((qb, AW), lambda i: (i, 0)), pl.BlockSpec(memory_space=pl.ANY),
                  pl.BlockSpec((AH, CHUNK, BAND), lambda i: (0, 0, 0))],
        out_specs=pl.BlockSpec((qb, AW), lambda i: (i, 0)),
        scratch_shapes=[pltpu.VMEM((T + LEFT, AW), BF16), pltpu.VMEM((T + LEFT, AW), BF16), pltpu.SemaphoreType.DMA((2,))],
        compiler_params=_cparams(("arbitrary",), vm),
        name=name,
    )(p, p, bias)


def _att_bwd(p, do, bias, name):
    T = p.shape[0]
    qb = CHUNK * ATT_G
    nsteps = T // qb

    def body(q_ref, p_any, do_ref, bias_ref, dq_ref, dk_any, dv_any, dbias_ref, kbuf, vbuf, dkbuf, dvbuf, sems):
        i = pl.program_id(0)

        @pl.when(i == 0)
        def _():
            _att_load_kv(p_any, kbuf, vbuf, sems, T)
            dkbuf[...] = jnp.zeros_like(dkbuf)
            dvbuf[...] = jnp.zeros_like(dvbuf)
            dbias_ref[...] = jnp.zeros_like(dbias_ref)

        lane = lax.broadcasted_iota(jnp.int32, (CHUNK, 128), 1)
        for c in range(ATT_G):
            n = i * ATT_G + c
            start = pl.multiple_of(n * CHUNK, CHUNK)
            rows = slice(c * CHUNK, (c + 1) * CHUNK)
            for hp in range(AH // 2):
                cols = slice(hp * 128, (hp + 1) * 128)
                kb = kbuf[pl.ds(start, BAND), cols]
                vb = vbuf[pl.ds(start, BAND), cols]
                q2 = q_ref[rows, cols]
                do2 = do_ref[rows, cols].astype(BF16)
                dq2 = jnp.zeros((CHUNK, 128), F32)
                dk2 = jnp.zeros((BAND, 128), F32)
                dv2 = jnp.zeros((BAND, 128), F32)
                for s in range(2):
                    mine = (lane >= 64) == (s == 1)
                    qm = jnp.where(mine, q2, jnp.zeros_like(q2))
                    dom = jnp.where(mine, do2, jnp.zeros_like(do2))
                    pr = _att_probs(qm, kb, bias_ref[2 * hp + s], n)
                    dp = lax.dot_general(dom, vb, _DIMS["nt"], preferred_element_type=F32)
                    ds = pr * (dp - jnp.sum(dp * pr, axis=-1, keepdims=True))
                    dbias_ref[2 * hp + s] += ds
                    dsb = (ds * (64 ** -0.5)).astype(BF16)
                    dq2 = jnp.where(mine, jnp.dot(dsb, kb, preferred_element_type=F32), dq2)
                    dk2 = dk2 + lax.dot_general(dsb, qm, _DIMS["tn"], preferred_element_type=F32)
                    dv2 = dv2 + lax.dot_general(pr.astype(BF16), dom, _DIMS["tn"], preferred_element_type=F32)
                dq_ref[rows, cols] = dq2
                dkbuf[pl.ds(start, BAND), cols] += dk2
                dvbuf[pl.ds(start, BAND), cols] += dv2

        @pl.when(i == nsteps - 1)
        def _():
            kc = pltpu.make_async_copy(dkbuf.at[pl.ds(LEFT, T), :], dk_any, sems.at[0])
            vc = pltpu.make_async_copy(dvbuf.at[pl.ds(LEFT, T), :], dv_any, sems.at[1])
            kc.start()
            vc.start()
            kc.wait()
            vc.wait()

    vm = 2 * _nbytes((T + LEFT, AW), BF16) + 2 * _nbytes((T + LEFT, AW), F32) + 6 * _nbytes((AH, CHUNK, BAND), F32) + (8 << 20)
    out = jax.ShapeDtypeStruct((T, AW), F32)
    return pl.pallas_call(
        body,
        out_shape=(out, out, out, jax.ShapeDtypeStruct((AH, CHUNK, BAND), F32)),
        grid=(nsteps,),
        in_specs=[pl.BlockSpec((qb, AW), lambda i: (i, 0)), pl.BlockSpec(memory_space=pl.ANY),
                  pl.BlockSpec((qb, AW), lambda i: (i, 0)), pl.BlockSpec((AH, CHUNK, BAND), lambda i: (0, 0, 0))],
        out_specs=(pl.BlockSpec((qb, AW), lambda i: (i, 0)), pl.BlockSpec(memory_space=pl.ANY),
                   pl.BlockSpec(memory_space=pl.ANY), pl.BlockSpec((AH, CHUNK, BAND), lambda i: (0, 0, 0))),
        scratch_shapes=[pltpu.VMEM((T + LEFT, AW), BF16), pltpu.VMEM((T + LEFT, AW), BF16),
                        pltpu.VMEM((T + LEFT, AW), F32), pltpu.VMEM((T + LEFT, AW), F32), pltpu.SemaphoreType.DMA((2,))],
        compiler_params=_cparams(("arbitrary",), vm),
        name=name,
    )(p, p, do, bias)


def _bias_from_rel(rel):
    near = rel[:, 128 - (CHUNK - 1):N_REL]
    far = jnp.broadcast_to(rel[:, N_REL - 1:N_REL], (AH, LEFT + CHUNK - 1 - 128))
    tr = jnp.concatenate([near, far], axis=1)[:, ::-1]
    return jnp.stack([tr[:, CHUNK - 1 - q:CHUNK - 1 - q + BAND] for q in range(CHUNK)], axis=1)


def _rel_from_dbias(dbias):
    width = BAND + CHUNK - 1
    dtr = sum(jnp.pad(dbias[:, q, :], ((0, 0), (CHUNK - 1 - q, q))) for q in range(CHUNK))
    dt = dtr[:, ::-1]
    n_near = 128 + CHUNK
    last = dt[:, n_near - 1:n_near] + jnp.sum(dt[:, n_near:width], axis=1, keepdims=True)
    return jnp.concatenate([jnp.zeros((AH, 128 - (CHUNK - 1)), F32), dt[:, 0:n_near - 1], last], axis=1)


def _layer_fwd(h, wl, consts, tag):
    xn = _rmsnorm_fwd(h, wl["norm_mix"], f"{tag}_norm_mix")
    p_gla = _mm(xn, wl["w_gla"], mode="nn", out_dtype=F32, name=f"{tag}_proj_gla")
    p_conv = _mm(xn, wl["w_conv"], mode="nn", out_dtype=F32, name=f"{tag}_proj_conv")
    p_att = _mm(xn, wl["w_att"], mode="nn", out_dtype=BF16, name=f"{tag}_proj_att")
    o_gla, states = _gla_fwd(p_gla, wl["wg"], wl["bg"], wl["gla_norm"], consts, f"{tag}_gla_fwd")
    o_conv = _conv_fwd(p_conv, wl["w_dw"], wl["b_dw"], wl["ln_g"], wl["ln_b"], f"{tag}_conv_fwd")
    bias = _bias_from_rel(wl["rel_bias"])
    o_att = _att_fwd(p_att, bias, f"{tag}_att_fwd")
    h1 = _mm(o_gla, wl["w_out_g"], mode="nn", out_dtype=F32, name=f"{tag}_out_gla", epi="add", extra=h)
    h1 = _mm(o_conv, wl["w_out_c"], mode="nn", out_dtype=F32, name=f"{tag}_out_conv", epi="add", extra=h1)
    h1 = _mm(o_att, wl["w_out_a"], mode="nn", out_dtype=F32, name=f"{tag}_out_att", epi="add", extra=h1)
    xn2 = _rmsnorm_fwd(h1, wl["norm_ffn"], f"{tag}_norm_ffn")
    u = _mm(xn2, wl["w_up"], mode="nn", out_dtype=BF16, tn=1024, name=f"{tag}_mlp_up")
    h2 = _mm(u, wl["w_down"], mode="nn", out_dtype=F32, tk=1024, a_pro="relu2", epi="add", extra=h1, name=f"{tag}_mlp_down")
    saved = dict(h=h, xn=xn, p_gla=p_gla, p_conv=p_conv, p_att=p_att, states=states, o_gla=o_gla, o_conv=o_conv,
                 o_att=o_att, bias=bias, h1=h1, xn2=xn2, u=u)
    return h2, saved


def _layer_bwd(dh2, sv, wl, consts, tag):
    g = {}
    du = _mm(dh2, wl["w_down"], mode="nt", out_dtype=BF16, tn=1024, epi="relu2grad", extra=sv["u"], name=f"{tag}_mlp_down_dx")
    g["w_down"] = _mm(sv["u"], dh2, mode="tn", out_dtype=F32, tm=1024, tn=512, tk=512, a_pro="relu2", name=f"{tag}_mlp_down_dw")
    dxn2 = _mm(du, wl["w_up"], mode="nt", out_dtype=F32, tk=1024, name=f"{tag}_mlp_up_dx")
    g["w_up"] = _mm(sv["xn2"], du, mode="tn", out_dtype=F32, tm=1024, tn=512, tk=512, name=f"{tag}_mlp_up_dw")
    dh1, g["norm_ffn"] = _rmsnorm_bwd(dxn2, sv["h1"], wl["norm_ffn"], dh2, f"{tag}_norm_ffn_bwd")
    d_gla = _mm(dh1, wl["w_out_g"], mode="nt", out_dtype=F32, name=f"{tag}_out_gla_dx")
    d_conv = _mm(dh1, wl["w_out_c"], mode="nt", out_dtype=F32, name=f"{tag}_out_conv_dx")
    d_att = _mm(dh1, wl["w_out_a"], mode="nt", out_dtype=F32, name=f"{tag}_out_att_dx")
    g["w_out_g"] = _mm(sv["o_gla"], dh1, mode="tn", out_dtype=F32, tm=VW, tn=512, tk=512, name=f"{tag}_out_gla_dw")
    g["w_out_c"] = _mm(sv["o_conv"], dh1, mode="tn", out_dtype=F32, tm=CW, tn=512, tk=512, name=f"{tag}_out_conv_dw")
    g["w_out_a"] = _mm(sv["o_att"], dh1, mode="tn", out_dtype=F32, tm=AW, tn=512, tk=512, name=f"{tag}_out_att_dw")
    dp_gla, g["wg"], g["bg"], g["gla_norm"] = _gla_bwd(sv["p_gla"], d_gla, sv["states"], wl["wg"], wl["bg"], wl["gla_norm"],
                                                       consts, f"{tag}_gla_bwd")
    dc, g["w_dw"], g["b_dw"], g["ln_g"], g["ln_b"] = _conv_bwd_dc(sv["p_conv"], d_conv, wl["w_dw"], wl["b_dw"], wl["ln_g"],
                                                                  wl["ln_b"], f"{tag}_conv_bwd_dc")
    dp_conv = _conv_bwd_du(sv["p_conv"], dc, wl["w_dw"], f"{tag}_conv_bwd_du")
    dq, dk, dv, dbias = _att_bwd(sv["p_att"], d_att, sv["bias"], f"{tag}_att_bwd")
    g["rel_bias"] = _rel_from_dbias(dbias)
    g["w_gla"] = _mm(sv["xn"], dp_gla, mode="tn", out_dtype=F32, tm=512, tk=512, name=f"{tag}_proj_gla_dw")
    g["w_conv"] = _mm(sv["xn"], dp_conv, mode="tn", out_dtype=F32, tm=1024, tn=512, tk=512, name=f"{tag}_proj_conv_dw")
    g["w_att_q"] = _mm(sv["xn"], dq, mode="tn", out_dtype=F32, tm=1024, tn=AW, tk=512, name=f"{tag}_proj_attq_dw")
    g["w_att_k"] = _mm(sv["xn"], dk, mode="tn", out_dtype=F32, tm=1024, tn=AW, tk=512, name=f"{tag}_proj_attk_dw")
    g["w_att_v"] = _mm(sv["xn"], dv, mode="tn", out_dtype=F32, tm=1024, tn=AW, tk=512, name=f"{tag}_proj_attv_dw")
    dxn = _mm(dp_gla, wl["w_gla"], mode="nt", out_dtype=F32, name=f"{tag}_proj_gla_dx")
    dxn = _mm(dp_conv, wl["w_conv"], mode="nt", out_dtype=F32, epi="add", extra=dxn, name=f"{tag}_proj_conv_dx")
    dxn = _mm(dq, wl["w_att"][:, 0:AW], mode="nt", out_dtype=F32, epi="add", extra=dxn, name=f"{tag}_proj_attq_dx")
    dxn = _mm(dk, wl["w_att"][:, AW:2 * AW], mode="nt", out_dtype=F32, epi="add", extra=dxn, name=f"{tag}_proj_attk_dx")
    dxn = _mm(dv, wl["w_att"][:, 2 * AW:3 * AW], mode="nt", out_dtype=F32, epi="add", extra=dxn, name=f"{tag}_proj_attv_dx")
    dh, g["norm_mix"] = _rmsnorm_bwd(dxn, sv["h"], wl["norm_mix"], dh1, f"{tag}_norm_mix_bwd")
    return dh, g


def _local_step(x, target, layers, norm_final):
    consts = _gla_consts()
    h = x
    saved = []
    for l, wl in enumerate(layers):
        h, sv = _layer_fwd(h, wl, consts, f"l{l}")
        saved.append(sv)
    loss, dh, g_final = _final_loss(h, norm_final, target, "final_loss")
    grads = [None] * len(layers)
    for l in reversed(range(len(layers))):
        dh, grads[l] = _layer_bwd(dh, saved[l], layers[l], consts, f"l{l}")
    return loss, dh, grads, g_final


ANY = pl.BlockSpec(memory_space=pl.ANY)


def _place():
    x, y, c = lax.axis_index("x"), lax.axis_index("y"), lax.axis_index("c")
    chips = [(1 - x, y), (x, 1 - y), (1 - x, 1 - y)]
    return x, y, c, chips


def _gather_chips(items):
    n = len(items)

    def body(*refs):
        srcs, outs = refs[:n], refs[n:2 * n]
        send_sems, recv_sems, loc_sems = refs[2 * n:]
        x, y, c, chips = _place()
        me = 2 * x + y

        def half(t, chip, cc):
            rh = items[t].shape[1] // 2
            return outs[t].at[chip, :, pl.ds(cc * rh, rh), :]

        def copy(t, k, src, dst, to):
            return pltpu.make_async_remote_copy(src_ref=src, dst_ref=dst, send_sem=send_sems.at[t, k],
                                                recv_sem=recv_sems.at[t, k], device_id=to, device_id_type=MESH)

        local = [pltpu.make_async_copy(srcs[t], outs[t].at[me], loc_sems.at[t]) for t in range(n)]
        for cp in local:
            cp.start()
        sent = []
        for t in range(n):
            rh = items[t].shape[1] // 2
            mine = srcs[t].at[:, pl.ds(c * rh, rh), :]
            for k, (px, py) in enumerate(chips):
                sent.append(copy(t, k, mine, half(t, me, c), (px, py, c)))
                sent[-1].start()
        for k, (px, py) in enumerate(chips):
            for t in range(n):
                blk = half(t, 2 * px + py, c)
                copy(t, k, blk, blk, (px, py, c)).wait_recv()
                sent.append(copy(t, 3 + k, blk, blk, (x, y, 1 - c)))
                sent[-1].start()
        for k, (px, py) in enumerate(chips):
            for t in range(n):
                blk = half(t, 2 * px + py, 1 - c)
                copy(t, 3 + k, blk, blk, (x, y, 1 - c)).wait_recv()
        for cp in sent:
            cp.wait_send()
        for cp in local:
            cp.wait()

    return pl.pallas_call(
        body,
        out_shape=[jax.ShapeDtypeStruct((4,) + a.shape, a.dtype) for a in items],
        in_specs=[ANY] * n,
        out_specs=[ANY] * n,
        scratch_shapes=[pltpu.SemaphoreType.DMA((n, 6)), pltpu.SemaphoreType.DMA((n, 6)), pltpu.SemaphoreType.DMA((n,))],
        name="gather_weights",
    )(*items)


def _pair_exchange(items):
    n = len(items)

    def body(*refs):
        srcs, outs = refs[:n], refs[n:2 * n]
        send_sems, recv_sems = refs[2 * n:]
        x, y, c, _ = _place()
        cps = []
        for t in range(n):
            rh = items[t].shape[2] // 2
            cps.append(pltpu.make_async_remote_copy(
                src_ref=srcs[t].at[:, :, pl.ds((1 - c) * rh, rh), :], dst_ref=outs[t], send_sem=send_sems.at[t],
                recv_sem=recv_sems.at[t], device_id=(x, y, 1 - c), device_id_type=MESH))
            cps[-1].start()
        for cp in cps:
            cp.wait()

    return pl.pallas_call(
        body,
        out_shape=[jax.ShapeDtypeStruct(a.shape[:2] + (a.shape[2] // 2, a.shape[3]), a.dtype) for a in items],
        in_specs=[ANY] * n,
        out_specs=[ANY] * n,
        scratch_shapes=[pltpu.SemaphoreType.DMA((n,)), pltpu.SemaphoreType.DMA((n,))],
        name="reduce_pair_exchange",
    )(*items)


def _row_tile(rows, cols, itemsize=4, budget=1 << 20):
    t = rows
    while t % 2 == 0 and t // 2 >= 8 and (t // 2) % 8 == 0 and t * cols * itemsize > budget:
        t //= 2
    return t


def _pair_add(full, got, c, name):
    _, L, R, C = full.shape
    rh = R // 2
    tr = _row_tile(rh, C)
    nb = rh // tr

    def body(c_ref, a_ref, b_ref, o_ref):
        o_ref[...] = (a_ref[...] + b_ref[...]).astype(o_ref.dtype)

    grid_spec = pltpu.PrefetchScalarGridSpec(
        num_scalar_prefetch=1,
        grid=(4, L, nb),
        in_specs=[pl.BlockSpec((1, 1, tr, C), lambda j, l, i, c_ref: (j, l, c_ref[0] * nb + i, 0)),
                  pl.BlockSpec((1, 1, tr, C), lambda j, l, i, c_ref: (j, l, i, 0))],
        out_specs=pl.BlockSpec((1, 1, tr, C), lambda j, l, i, c_ref: (j, l, i, 0)),
    )
    return pl.pallas_call(
        body,
        out_shape=jax.ShapeDtypeStruct((4, L, rh, C), BF16),
        grid_spec=grid_spec,
        compiler_params=_cparams(("parallel", "parallel", "parallel"), 8 * tr * C * 4),
        name=name,
    )(jnp.reshape(c, (1,)).astype(jnp.int32), full, got)


def _chip_scatter(items):
    n = len(items)

    def body(*refs):
        srcs, outs = refs[:n], refs[n:2 * n]
        send_sems, recv_sems, loc_sems = refs[2 * n:]
        x, y, c, chips = _place()
        me = 2 * x + y
        local = [pltpu.make_async_copy(srcs[t].at[me], outs[t].at[me], loc_sems.at[t]) for t in range(n)]
        for cp in local:
            cp.start()
        cps = []
        for t in range(n):
            for k, (px, py) in enumerate(chips):
                cps.append(pltpu.make_async_remote_copy(
                    src_ref=srcs[t].at[2 * px + py], dst_ref=outs[t].at[me], send_sem=send_sems.at[t, k],
                    recv_sem=recv_sems.at[t, k], device_id=(px, py, c), device_id_type=MESH))
                cps[-1].start()
        for t in range(n):
            for k, (px, py) in enumerate(chips):
                blk = outs[t].at[2 * px + py]
                pltpu.make_async_remote_copy(src_ref=blk, dst_ref=blk, send_sem=send_sems.at[t, k], recv_sem=recv_sems.at[t, k],
                                             device_id=(px, py, c), device_id_type=MESH).wait_recv()
        for cp in cps:
            cp.wait_send()
        for cp in local:
            cp.wait()

    return pl.pallas_call(
        body,
        out_shape=[jax.ShapeDtypeStruct(a.shape, a.dtype) for a in items],
        in_specs=[ANY] * n,
        out_specs=[ANY] * n,
        scratch_shapes=[pltpu.SemaphoreType.DMA((n, 3)), pltpu.SemaphoreType.DMA((n, 3)), pltpu.SemaphoreType.DMA((n,))],
        name="reduce_chip_scatter",
    )(*items)


def _sum_chips(parts, name):
    _, L, rh, C = parts.shape
    tr = _row_tile(rh, C)

    def body(p_ref, o_ref):
        acc = p_ref[0].astype(F32)
        for j in range(1, 4):
            acc = acc + p_ref[j].astype(F32)
        o_ref[...] = acc

    return pl.pallas_call(
        body,
        out_shape=jax.ShapeDtypeStruct((L, rh, C), F32),
        grid=(L, rh // tr),
        in_specs=[pl.BlockSpec((4, 1, tr, C), lambda l, i: (0, l, i, 0))],
        out_specs=pl.BlockSpec((1, tr, C), lambda l, i: (l, i, 0)),
        compiler_params=_cparams(("parallel", "parallel"), 16 * tr * C * 4),
        name=name,
    )(parts)


def _pair_allgather(items):
    n = len(items)

    def body(*refs):
        srcs, outs = refs[:n], refs[n:2 * n]
        send_sems, recv_sems, loc_sems = refs[2 * n:]
        x, y, c, _ = _place()
        cps, local = [], []
        for t in range(n):
            rh = items[t].shape[1]
            mine = outs[t].at[:, pl.ds(c * rh, rh), :]
            local.append(pltpu.make_async_copy(srcs[t], mine, loc_sems.at[t]))
            local[-1].start()
            cps.append(pltpu.make_async_remote_copy(src_ref=srcs[t], dst_ref=mine, send_sem=send_sems.at[t],
                                                    recv_sem=recv_sems.at[t], device_id=(x, y, 1 - c), device_id_type=MESH))
            cps[-1].start()
        for t in range(n):
            rh = items[t].shape[1]
            theirs = outs[t].at[:, pl.ds((1 - c) * rh, rh), :]
            pltpu.make_async_remote_copy(src_ref=theirs, dst_ref=theirs, send_sem=send_sems.at[t], recv_sem=recv_sems.at[t],
                                         device_id=(x, y, 1 - c), device_id_type=MESH).wait_recv()
        for cp in cps:
            cp.wait_send()
        for cp in local:
            cp.wait()

    return pl.pallas_call(
        body,
        out_shape=[jax.ShapeDtypeStruct((a.shape[0], 2 * a.shape[1], a.shape[2]), a.dtype) for a in items],
        in_specs=[ANY] * n,
        out_specs=[ANY] * n,
        scratch_shapes=[pltpu.SemaphoreType.DMA((n,)), pltpu.SemaphoreType.DMA((n,)), pltpu.SemaphoreType.DMA((n,))],
        name="reduce_pair_allgather",
    )(*items)


def _allreduce_small(v):
    R = v.shape[0]

    def body(v_ref, o_ref, slots, send_sems, recv_sems):
        x, y, c, _ = _place()
        me = 4 * x + 2 * y + c
        slots[me] = v_ref[...]
        cps = []
        for r in range(1, 8):
            px, py, pc = x ^ (r >> 2), y ^ ((r >> 1) & 1), c ^ (r & 1)
            cps.append(pltpu.make_async_remote_copy(src_ref=v_ref, dst_ref=slots.at[me], send_sem=send_sems.at[r - 1],
                                                    recv_sem=recv_sems.at[r - 1], device_id=(px, py, pc), device_id_type=MESH))
            cps[-1].start()
        for r in range(1, 8):
            px, py, pc = x ^ (r >> 2), y ^ ((r >> 1) & 1), c ^ (r & 1)
            theirs = slots.at[4 * px + 2 * py + pc]
            pltpu.make_async_remote_copy(src_ref=theirs, dst_ref=theirs, send_sem=send_sems.at[r - 1], recv_sem=recv_sems.at[r - 1],
                                         device_id=(px, py, pc), device_id_type=MESH).wait_recv()
        acc = slots[0]
        for j in range(1, 8):
            acc = acc + slots[j]
        o_ref[...] = acc
        for cp in cps:
            cp.wait_send()

    return pl.pallas_call(
        body,
        out_shape=jax.ShapeDtypeStruct(v.shape, F32),
        in_specs=[pl.BlockSpec(memory_space=pltpu.VMEM)],
        out_specs=pl.BlockSpec(memory_space=pltpu.VMEM),
        scratch_shapes=[pltpu.VMEM((8, R, 128), F32), pltpu.SemaphoreType.DMA((7,)), pltpu.SemaphoreType.DMA((7,))],
        name="allreduce_small",
    )(v)


def _adamw_math(w, g, m, v):
    m = ADAM_B1 * m + (1.0 - ADAM_B1) * g
    v = ADAM_B2 * v + (1.0 - ADAM_B2) * (g * g)
    m_hat = m / (1.0 - ADAM_B1 ** ADAM_STEP)
    v_hat = v / (1.0 - ADAM_B2 ** ADAM_STEP)
    delta = -ADAM_LR * (m_hat / (jnp.sqrt(v_hat) + ADAM_EPS) + ADAM_WD * w)
    return delta, m, v


def _adamw(w, g, m, v, name):
    rows, cols = w.shape
    tr = _row_tile(rows, cols, budget=1 << 19)

    def body(w_ref, g_ref, m_ref, v_ref, d_ref, nm_ref, nv_ref):
        d_ref[...], nm_ref[...], nv_ref[...] = _adamw_math(w_ref[...], g_ref[...], m_ref[...], v_ref[...])

    blk = pl.BlockSpec((tr, cols), lambda i: (i, 0))
    out = jax.ShapeDtypeStruct(w.shape, F32)
    return pl.pallas_call(
        body,
        out_shape=(out, out, out),
        grid=(rows // tr,),
        in_specs=[blk] * 4,
        out_specs=(blk, blk, blk),
        compiler_params=_cparams(("parallel",), 16 * tr * cols * 4),
        name=name,
    )(w, g, m, v)


def _adamw_small(ws, gs, ms, vs):
    n = len(ws)

    def body(*refs):
        for t in range(n):
            w_ref, g_ref, m_ref, v_ref = (refs[k * n + t] for k in range(4))
            d_ref, nm_ref, nv_ref = (refs[(4 + k) * n + t] for k in range(3))
            d_ref[...], nm_ref[...], nv_ref[...] = _adamw_math(w_ref[...], g_ref[...], m_ref[...], v_ref[...])

    vmem = pl.BlockSpec(memory_space=pltpu.VMEM)
    outs = [jax.ShapeDtypeStruct(w.shape, F32) for w in ws]
    res = pl.pallas_call(
        body,
        out_shape=outs * 3,
        in_specs=[vmem] * (4 * n),
        out_specs=[vmem] * (3 * n),
        name="adamw_small",
    )(*ws, *gs, *ms, *vs)
    return res[:n], res[n:2 * n], res[2 * n:]


IN_SIZES = (192, 192, 384, 384, 16, 512, 384, 384, 384)
IN_OFFS = tuple(int(v) for v in np.cumsum((0,) + IN_SIZES))
SMALL = ("norm_mix", "w_gla_gate", "b_gla_gate", "gla_norm", "b_dw", "conv_ln_g", "conv_ln_b", "rel_bias", "norm_ffn")


def _pad_cols(a, n):
    return jnp.pad(a, ((0, 0), (0, n - a.shape[1])))


def _split_w_in(w):
    s = [w[:, IN_OFFS[i]:IN_OFFS[i + 1]] for i in range(9)]
    w_gla = jnp.concatenate([_pad_cols(s[0], KW), _pad_cols(s[1], KW), s[2], s[3], _pad_cols(s[4], LRW)], axis=1)
    return w_gla, s[5], jnp.concatenate(s[6:9], axis=1)


def _join_w_in(g):
    gg = g["w_gla"]
    return jnp.concatenate([gg[:, 0:192], gg[:, KW:KW + 192], gg[:, 2 * KW:2 * KW + VW], gg[:, 2 * KW + VW:2 * KW + 2 * VW],
                            gg[:, 2 * KW + 2 * VW:2 * KW + 2 * VW + 16], g["w_conv"], g["w_att_q"], g["w_att_k"], g["w_att_v"]],
                           axis=1)


def _pack(arrs, rows):
    flat = jnp.concatenate([a.reshape(-1) for a in arrs])
    return jnp.pad(flat, (0, rows * 128 - flat.shape[0])).reshape(rows, 128)


def _unpack(packed, shapes):
    flat = packed.reshape(-1)
    out, off = [], 0
    for s in shapes:
        n = int(np.prod(s))
        out.append(flat[off:off + n].reshape(s))
        off += n
    return out


def kernel(x, norm_mix, w_in, w_gla_gate, b_gla_gate, gla_norm, w_dw, b_dw, conv_ln_g, conv_ln_b, rel_bias, w_out, norm_ffn, w_up, w_down, norm_final, loss_target, m_norm_mix, m_w_in, m_w_gla_gate, m_b_gla_gate, m_gla_norm, m_w_dw, m_b_dw, m_conv_ln_g, m_conv_ln_b, m_rel_bias, m_w_out, m_norm_ffn, m_w_up, m_w_down, m_norm_final, v_norm_mix, v_w_in, v_w_gla_gate, v_b_gla_gate, v_gla_norm, v_w_dw, v_b_dw, v_conv_ln_g, v_conv_ln_b, v_rel_bias, v_w_out, v_norm_ffn, v_w_up, v_w_down, v_norm_final):
    P = dict(norm_mix=norm_mix, w_in=w_in, w_gla_gate=w_gla_gate, b_gla_gate=b_gla_gate, gla_norm=gla_norm, w_dw=w_dw, b_dw=b_dw,
             conv_ln_g=conv_ln_g, conv_ln_b=conv_ln_b, rel_bias=rel_bias, w_out=w_out, norm_ffn=norm_ffn, w_up=w_up,
             w_down=w_down, norm_final=norm_final)
    Mo = dict(norm_mix=m_norm_mix, w_in=m_w_in, w_gla_gate=m_w_gla_gate, b_gla_gate=m_b_gla_gate, gla_norm=m_gla_norm, w_dw=m_w_dw,
              b_dw=m_b_dw, conv_ln_g=m_conv_ln_g, conv_ln_b=m_conv_ln_b, rel_bias=m_rel_bias, w_out=m_w_out, norm_ffn=m_norm_ffn,
              w_up=m_w_up, w_down=m_w_down, norm_final=m_norm_final)
    Vo = dict(norm_mix=v_norm_mix, w_in=v_w_in, w_gla_gate=v_w_gla_gate, b_gla_gate=v_b_gla_gate, gla_norm=v_gla_norm, w_dw=v_w_dw,
              b_dw=v_b_dw, conv_ln_g=v_conv_ln_g, conv_ln_b=v_conv_ln_b, rel_bias=v_rel_bias, w_out=v_w_out, norm_ffn=v_norm_ffn,
              w_up=v_w_up, w_down=v_w_down, norm_final=v_norm_final)
    depth = w_in.shape[0]
    xi, yi, ci = lax.axis_index("x"), lax.axis_index("y"), lax.axis_index("c")
    chip = 2 * xi + yi

    w_dw_pad = jnp.pad(w_dw, ((0, 0), (0, HALO - CK), (0, 0)))
    g_in, g_out, g_up, g_down, g_dw = _gather_chips(
        [w_in.astype(BF16), w_out.astype(BF16), w_up.astype(BF16), w_down.astype(BF16), w_dw_pad])
    layers = []
    for l in range(depth):
        w_in_full = jnp.transpose(g_in[:, l], (1, 0, 2)).reshape(D, -1)
        w_gla, w_conv, w_att = _split_w_in(w_in_full)
        w_out_full = g_out[:, l].reshape(D, D)
        layers.append(dict(
            norm_mix=norm_mix[l][None], w_gla=w_gla, w_conv=w_conv, w_att=w_att,
            wg=jnp.pad(w_gla_gate[l], ((0, LRW - 16), (0, KW - 192))), bg=_pad_cols(b_gla_gate[l][None], KW),
            gla_norm=gla_norm[l][None], w_dw=jnp.transpose(g_dw[:, l], (1, 0, 2)).reshape(HALO, CW), b_dw=b_dw[l][None],
            ln_g=conv_ln_g[l][None], ln_b=conv_ln_b[l][None], rel_bias=rel_bias[l],
            w_out_g=w_out_full[0:VW], w_out_c=w_out_full[VW:VW + CW], w_out_a=w_out_full[VW + CW:],
            norm_ffn=norm_ffn[l][None], w_up=jnp.transpose(g_up[:, l], (1, 0, 2)).reshape(D, D_FF),
            w_down=g_down[:, l].reshape(D_FF, D)))

    loss_part, grad_x, grads, g_final = _local_step(x[0], loss_target[0], layers, norm_final[None])
    loss = lax.psum(loss_part[0, 0], ("x", "y", "c"))

    def stack(fn):
        return jnp.stack([fn(grads[l]) for l in range(depth)], axis=1)

    full = [
        stack(lambda g: jnp.transpose(_join_w_in(g).reshape(D, 4, -1), (1, 0, 2))),
        stack(lambda g: jnp.concatenate([g["w_out_g"], g["w_out_c"], g["w_out_a"]], axis=0).reshape(4, D // 4, D)),
        stack(lambda g: jnp.transpose(g["w_up"].reshape(D, 4, D_FF // 4), (1, 0, 2))),
        stack(lambda g: g["w_down"].reshape(4, D_FF // 4, D)),
    ]
    got = _pair_exchange(full)
    pair = [_pair_add(f, r, ci, f"reduce_pair_add_{t}") for t, (f, r) in enumerate(zip(full, got))]
    parts = _chip_scatter(pair)
    halves = [_sum_chips(p, f"reduce_sum_chips_{t}") for t, p in enumerate(parts)]
    gw_in, gw_out, gw_up, gw_down = _pair_allgather(halves)

    small_g = []
    for l in range(depth):
        g = grads[l]
        small_g += [g["norm_mix"], g["wg"][0:16, 0:192], g["bg"][:, 0:192], g["gla_norm"], g["b_dw"], g["ln_g"], g["ln_b"],
                    g["rel_bias"], g["norm_ffn"], g["w_dw"][0:CK]]
    small_g.append(g_final)
    small_shapes = [a.shape for a in small_g]
    n_small = sum(int(np.prod(s)) for s in small_shapes)
    rows = -(-n_small // 1024) * 8
    red = _unpack(_allreduce_small(_pack(small_g, rows)), small_shapes)
    G = {}
    per = len(SMALL) + 1
    for i, name in enumerate(SMALL):
        G[name] = jnp.stack([red[l * per + i].reshape(P[name].shape[1:]) for l in range(depth)])
    gw_dw_all = jnp.stack([red[l * per + len(SMALL)] for l in range(depth)])
    G["w_dw"] = lax.dynamic_slice_in_dim(gw_dw_all, chip * (CW // 4), CW // 4, axis=2)
    G["norm_final"] = red[-1].reshape(norm_final.shape)
    G["w_in"], G["w_out"], G["w_up"], G["w_down"] = gw_in, gw_out, gw_up, gw_down

    delta, new_m, new_v = {}, {}, {}
    for name in ("w_in", "w_out", "w_up", "w_down"):
        shp = P[name].shape
        two_d = lambda a: a.reshape(-1, shp[-1])
        d, nm, nv = _adamw(two_d(P[name]), two_d(G[name]), two_d(Mo[name]), two_d(Vo[name]), f"adamw_{name}")
        delta[name], new_m[name], new_v[name] = d.reshape(shp), nm.reshape(shp), nv.reshape(shp)
    small_names = list(SMALL) + ["w_dw", "norm_final"]
    two_d = lambda a: a.reshape(-1, a.shape[-1])
    ds, nms, nvs = _adamw_small([two_d(P[k]) for k in small_names], [two_d(G[k]) for k in small_names],
                                [two_d(Mo[k]) for k in small_names], [two_d(Vo[k]) for k in small_names])
    for i, name in enumerate(small_names):
        shp = P[name].shape
        delta[name], new_m[name], new_v[name] = ds[i].reshape(shp), nms[i].reshape(shp), nvs[i].reshape(shp)

    order = ["norm_mix", "w_in", "w_gla_gate", "b_gla_gate", "gla_norm", "w_dw", "b_dw", "conv_ln_g", "conv_ln_b", "rel_bias",
             "w_out", "norm_ffn", "w_up", "w_down", "norm_final"]
    return (loss, grad_x[None], *[G[k] for k in order], *[delta[k] for k in order], *[new_m[k] for k in order],
            *[new_v[k] for k in order])
```

```python
import functools

import numpy as np
import jax
import jax.numpy as jnp
from jax import lax
from jax.experimental import pallas as pl
from jax.experimental.pallas import tpu as pltpu

F32 = jnp.float32
BF16 = jnp.bfloat16
HI = lax.Precision.HIGHEST

D = 1024
CHUNK = 64
GLA_DK, GLA_DV, GLA_H = 48, 96, 4
KW = 256
VW = 384
LRW = 128
GLA_TAU = 16.0
CW = 256
CK = 31
AW = 384
AH = 6
BAND = 576
LEFT = 512
D_FF = 4096
EPS = 1e-6
NEG = -1e30
N_REL = 257

GLA_COLS = 2 * KW + 2 * VW + LRW
CONV_COLS = 2 * CW
ATT_COLS = 3 * AW

ADAM_LR, ADAM_B1, ADAM_B2, ADAM_EPS, ADAM_WD, ADAM_STEP = 0.001, 0.9, 0.999, 1e-08, 0.01, 10

VMEM_CAP = 56 * 1024 * 1024
MESH = pl.DeviceIdType.MESH


def _cparams(sem, vmem_bytes):
    limit = int(min(VMEM_CAP, max(vmem_bytes * 5 // 4 + (4 << 20), 16 << 20)))
    return pltpu.CompilerParams(dimension_semantics=sem, vmem_limit_bytes=limit)


def _nbytes(shape, dtype):
    return int(np.prod(shape)) * jnp.dtype(dtype).itemsize


def _sigmoid(x):
    return 1.0 / (1.0 + jnp.exp(-x))


_DIMS = {"nn": (((1,), (0,)), ((), ())), "nt": (((1,), (1,)), ((), ())), "tn": (((0,), (0,)), ((), ()))}


def _mm(a, b, *, mode, out_dtype, name, tm=512, tn=None, tk=None, a_pro=None, epi=None, extra=None,
        b_chips=False, out_chips=False):
    b2 = (b.shape[1], 4 * b.shape[2]) if b_chips else b.shape
    if mode == "nn":
        (M, K), (K2, N) = a.shape, b2
    elif mode == "nt":
        (M, K), (N, K2) = a.shape, b2
    else:
        (K, M), (K2, N) = a.shape, b2
    assert K == K2, (a.shape, b.shape, mode)
    tm = min(tm, M)
    tn = N if tn is None else min(tn, N)
    tk = K if tk is None else min(tk, K)
    assert M % tm == 0 and N % tn == 0 and K % tk == 0, (M, N, K, tm, tn, tk)
    nk = K // tk
    a_blk = (tk, tm) if mode == "tn" else (tm, tk)
    a_map = (lambda i, j, k: (k, i)) if mode == "tn" else (lambda i, j, k: (i, k))
    b_blk = (tn, tk) if mode == "nt" else (tk, tn)
    b_map = (lambda i, j, k: (j, k)) if mode == "nt" else (lambda i, j, k: (k, j))
    if b_chips:
        per = b.shape[2] // b_blk[1]
        assert b.shape[2] % b_blk[1] == 0 and mode != "tn"
        flat_map = b_map
        b_map = lambda i, j, k: (flat_map(i, j, k)[1] // per, flat_map(i, j, k)[0], flat_map(i, j, k)[1] % per)
        b_blk = (None,) + b_blk
    in_specs = [pl.BlockSpec(a_blk, a_map), pl.BlockSpec(b_blk, b_map)]
    args = [a, b]
    if epi is not None:
        in_specs.append(pl.BlockSpec((tm, tn), lambda i, j, k: (i, j)))
        args.append(extra)

    def body(*refs):
        a_ref, b_ref = refs[0], refs[1]
        e_ref = refs[2] if epi is not None else None
        o_ref = refs[3] if epi is not None else refs[2]
        av = a_ref[...]
        if a_pro == "relu2":
            af = jnp.maximum(av.astype(F32), 0.0)
            av = af * af
        p = lax.dot_general(av.astype(BF16), b_ref[...].astype(BF16), _DIMS[mode], preferred_element_type=F32)

        def finish(acc):
            if epi == "add":
                acc = acc + e_ref[...].astype(F32)
            elif epi == "relu2grad":
                acc = acc * (2.0 * jnp.maximum(e_ref[...].astype(F32), 0.0))
            o_ref[...] = acc.astype(o_ref.dtype)

        if nk == 1:
            finish(p)
        else:
            acc_ref = refs[-1]
            k = pl.program_id(2)

            @pl.when(k == 0)
            def _():
                acc_ref[...] = p

            @pl.when(k > 0)
            def _():
                acc_ref[...] += p

            @pl.when(k == nk - 1)
            def _():
                finish(acc_ref[...])

    vm = 2 * (_nbytes(a_blk, a.dtype) + _nbytes((tk, tn), b.dtype) + _nbytes((tm, tn), out_dtype))
    vm += 3 * _nbytes((tm, tn), F32)
    if epi is not None:
        vm += 2 * _nbytes((tm, tn), extra.dtype)
    if out_chips:
        per_out = N // 4 // tn
        assert N % (4 * tn) == 0
        out_shape = jax.ShapeDtypeStruct((4, M, N // 4), out_dtype)
        out_spec = pl.BlockSpec((None, tm, tn), lambda i, j, k: (j // per_out, i, j % per_out))
    else:
        out_shape = jax.ShapeDtypeStruct((M, N), out_dtype)
        out_spec = pl.BlockSpec((tm, tn), lambda i, j, k: (i, j))
    return pl.pallas_call(
        body,
        out_shape=out_shape,
        grid=(M // tm, N // tn, nk),
        in_specs=in_specs,
        out_specs=out_spec,
        scratch_shapes=[pltpu.VMEM((tm, tn), F32)] if nk > 1 else [],
        compiler_params=_cparams(("parallel", "parallel", "arbitrary"), vm),
        name=name,
    )(*args)


def _rmsnorm_fwd(h, g, name, tm=512):
    T = h.shape[0]

    def body(h_ref, g_ref, o_ref):
        x = h_ref[...]
        r = lax.rsqrt(jnp.mean(x * x, axis=-1, keepdims=True) + EPS)
        o_ref[...] = (x * r * g_ref[...]).astype(o_ref.dtype)

    return pl.pallas_call(
        body,
        out_shape=jax.ShapeDtypeStruct((T, D), BF16),
        grid=(T // tm,),
        in_specs=[pl.BlockSpec((tm, D), lambda i: (i, 0)), pl.BlockSpec((1, D), lambda i: (0, 0))],
        out_specs=pl.BlockSpec((tm, D), lambda i: (i, 0)),
        compiler_params=_cparams(("parallel",), 8 * _nbytes((tm, D), F32)),
        name=name,
    )(h, g)


def _rmsnorm_bwd(dxn, h, g, dres, name, tm=512):
    T = h.shape[0]

    def body(dxn_ref, h_ref, g_ref, dres_ref, dh_ref, dg_ref):
        @pl.when(pl.program_id(0) == 0)
        def _():
            dg_ref[...] = jnp.zeros_like(dg_ref)

        x = h_ref[...]
        dy = dxn_ref[...].astype(F32)
        r = lax.rsqrt(jnp.mean(x * x, axis=-1, keepdims=True) + EPS)
        gy = dy * g_ref[...]
        dot = jnp.mean(x * gy, axis=-1, keepdims=True)
        dh_ref[...] = dres_ref[...] + r * gy - x * (r * r * r * dot)
        dg_ref[...] += jnp.sum(dy * x * r, axis=0, keepdims=True)

    row = pl.BlockSpec((tm, D), lambda i: (i, 0))
    vec = pl.BlockSpec((1, D), lambda i: (0, 0))
    return pl.pallas_call(
        body,
        out_shape=(jax.ShapeDtypeStruct((T, D), F32), jax.ShapeDtypeStruct((1, D), F32)),
        grid=(T // tm,),
        in_specs=[row, row, vec, row],
        out_specs=(row, vec),
        compiler_params=_cparams(("arbitrary",), 12 * _nbytes((tm, D), F32)),
        name=name,
    )(dxn, h, g, dres)


def _final_loss(h, g, target, name, tm=512):
    T = h.shape[0]

    def body(h_ref, g_ref, t_ref, loss_ref, dh_ref, dg_ref):
        @pl.when(pl.program_id(0) == 0)
        def _():
            dg_ref[...] = jnp.zeros_like(dg_ref)
            loss_ref[...] = jnp.zeros_like(loss_ref)

        x = h_ref[...]
        gg = g_ref[...]
        r = lax.rsqrt(jnp.mean(x * x, axis=-1, keepdims=True) + EPS)
        y = x * r * gg
        e = y - t_ref[...]
        loss_ref[...] += 0.5 * jnp.sum(jnp.mean(e * e, axis=-1, keepdims=True), axis=0, keepdims=True)
        dy = e * (1.0 / D)
        gy = dy * gg
        dot = jnp.mean(x * gy, axis=-1, keepdims=True)
        dh_ref[...] = r * gy - x * (r * r * r * dot)
        dg_ref[...] += jnp.sum(dy * x * r, axis=0, keepdims=True)

    row = pl.BlockSpec((tm, D), lambda i: (i, 0))
    vec = pl.BlockSpec((1, D), lambda i: (0, 0))
    one = pl.BlockSpec((1, 1), lambda i: (0, 0))
    return pl.pallas_call(
        body,
        out_shape=(jax.ShapeDtypeStruct((1, 1), F32), jax.ShapeDtypeStruct((T, D), F32), jax.ShapeDtypeStruct((1, D), F32)),
        grid=(T // tm,),
        in_specs=[row, vec, row],
        out_specs=(one, row, vec),
        compiler_params=_cparams(("arbitrary",), 12 * _nbytes((tm, D), F32)),
        name=name,
    )(h, g, target)


GLA_G = 8


def _gla_consts():
    i = np.arange(KW)[:, None]
    j = np.arange(VW)[None, :]
    mask = ((i // GLA_DK) == (j // GLA_DV)) & (i < GLA_H * GLA_DK)
    a = np.arange(VW)
    hm = ((a[:, None] // GLA_DV) == (a[None, :] // GLA_DV)).astype(np.float32) / GLA_DV
    c = np.arange(CHUNK)
    low = (c[:, None] >= c[None, :]).astype(np.float32)
    return jnp.asarray(mask.astype(np.float32)), jnp.asarray(hm), jnp.asarray(low)


def _gla_chunk_common(p_ref, rows, wg, bg, low, ones_v):
    q = p_ref[rows, 0:KW]
    k = p_ref[rows, KW:2 * KW]
    v = p_ref[rows, 2 * KW:2 * KW + VW]
    g = p_ref[rows, 2 * KW + VW:2 * KW + 2 * VW]
    lr = p_ref[rows, 2 * KW + 2 * VW:GLA_COLS]
    z = jnp.dot(lr, wg, precision=HI, preferred_element_type=F32) + bg
    la = (jnp.minimum(z, 0.0) - jnp.log(1.0 + jnp.exp(-jnp.abs(z)))) * (1.0 / GLA_TAU)
    cum = jnp.dot(low, la, precision=HI, preferred_element_type=F32)
    endb = cum[CHUNK - 1:CHUNK, :]
    w = jnp.exp(endb - cum)
    a_full = jnp.exp(lax.dot_general(la, ones_v, _DIMS["tn"], precision=HI, preferred_element_type=F32))
    return q, k, v, g, lr, z, w, endb, a_full


def _gla_fwd(p, wg, bg, gn, consts, name):
    T = p.shape[0]
    rb = CHUNK * GLA_G
    ng = T // rb
    mask, hm, low = consts[:3]
    scale = GLA_DK ** -0.5

    def body(p_ref, wg_ref, bg_ref, gn_ref, m_ref, hm_ref, l_ref, o_ref, st_ref, s_ref):
        @pl.when(pl.program_id(0) == 0)
        def _():
            s_ref[...] = jnp.zeros_like(s_ref)

        wg_v, bg_v, gn_v = wg_ref[...], bg_ref[...], gn_ref[...]
        ones_v = jnp.ones((CHUNK, VW), F32)
        for c in range(GLA_G):
            rows = slice(c * CHUNK, (c + 1) * CHUNK)
            q, k, v, g, _, _, w, _, a_full = _gla_chunk_common(p_ref, rows, wg_v, bg_v, l_ref[...], ones_v)
            kd = (k * w).astype(BF16)
            kv = lax.dot_general(kd, v.astype(BF16), _DIMS["tn"], preferred_element_type=F32) * m_ref[...]
            s_new = a_full * s_ref[...] + kv
            s_ref[...] = s_new
            st_ref[c] = s_new
            o = jnp.dot((q * scale).astype(BF16), s_new.astype(BF16), preferred_element_type=F32)
            ms = jnp.dot(o * o, hm_ref[...], precision=HI, preferred_element_type=F32)
            r = lax.rsqrt(ms + EPS)
            y = o * r * gn_v * (g * _sigmoid(g))
            o_ref[rows, :] = y.astype(o_ref.dtype)

    full = lambda shape: pl.BlockSpec(shape, lambda i: tuple(0 for _ in shape))
    vm = 2 * _nbytes((rb, GLA_COLS), F32) + 2 * _nbytes((GLA_G, KW, VW), F32) + 12 * _nbytes((KW, VW), F32)
    return pl.pallas_call(
        body,
        out_shape=(jax.ShapeDtypeStruct((T, VW), BF16), jax.ShapeDtypeStruct((T // CHUNK, KW, VW), F32)),
        grid=(ng,),
        in_specs=[pl.BlockSpec((rb, GLA_COLS), lambda i: (i, 0)), full((LRW, KW)), full((1, KW)), full((1, VW)),
                  full((KW, VW)), full((VW, VW)), full((CHUNK, CHUNK))],
        out_specs=(pl.BlockSpec((rb, VW), lambda i: (i, 0)), pl.BlockSpec((GLA_G, KW, VW), lambda i: (i, 0, 0))),
        scratch_shapes=[pltpu.VMEM((KW, VW), F32)],
        compiler_params=_cparams(("arbitrary",), vm),
        name=name,
    )(p, wg, bg, gn, mask, hm, low)


def _gla_bwd(p, dy, states, wg, bg, gn, consts, name):
    T = p.shape[0]
    rb = CHUNK * GLA_G
    ng = T // rb
    mask, hm, low = consts[:3]
    scale = GLA_DK ** -0.5

    def body(p_ref, dy_ref, st_ref, sp_ref, wg_ref, bg_ref, gn_ref, m_ref, hm_ref, l_ref,
             dp_ref, dwg_ref, dbg_ref, dgn_ref, ga_ref):
        step = pl.program_id(0)

        @pl.when(step == 0)
        def _():
            ga_ref[...] = jnp.zeros_like(ga_ref)
            dwg_ref[...] = jnp.zeros_like(dwg_ref)
            dbg_ref[...] = jnp.zeros_like(dbg_ref)
            dgn_ref[...] = jnp.zeros_like(dgn_ref)

        first_group = step == ng - 1
        wg_v, bg_v, gn_v = wg_ref[...], bg_ref[...], gn_ref[...]
        ones_v = jnp.ones((CHUNK, VW), F32)
        ones_8 = jnp.ones((8, VW), F32)
        for c in reversed(range(GLA_G)):
            rows = slice(c * CHUNK, (c + 1) * CHUNK)
            q, k, v, g, lr, z, w, endb, a_full = _gla_chunk_common(p_ref, rows, wg_v, bg_v, l_ref[...], ones_v)
            s_n = st_ref[c]
            if c > 0:
                s_prev = st_ref[c - 1]
            else:
                s_prev = jnp.where(first_group, 0.0, sp_ref[0])
            kd = k * w
            qs = (q * scale).astype(BF16)
            o = jnp.dot(qs, s_n.astype(BF16), preferred_element_type=F32)
            ms = jnp.dot(o * o, hm_ref[...], precision=HI, preferred_element_type=F32)
            r = lax.rsqrt(ms + EPS)
            on = o * r
            sg = _sigmoid(g)
            silu = g * sg
            dyv = dy_ref[rows, :].astype(F32)
            d_on = dyv * gn_v * silu
            dg = dyv * on * gn_v * (sg * (1.0 + g * (1.0 - sg)))
            dgn_ref[...] += jnp.sum(dyv * on * silu, axis=0, keepdims=True)
            mo = jnp.dot(o * d_on, hm_ref[...], precision=HI, preferred_element_type=F32)
            do = r * d_on - o * (r * r * r) * mo
            dob = do.astype(BF16)
            dq = lax.dot_general(dob, s_n.astype(BF16), _DIMS["nt"], preferred_element_type=F32) * scale
            g_n = lax.dot_general(qs, dob, _DIMS["tn"], preferred_element_type=F32) * m_ref[...] + ga_ref[...]
            d_a = lax.dot_general(ones_8, g_n * s_prev, _DIMS["nt"], precision=HI, preferred_element_type=F32)[0:1, :]
            g_nb = g_n.astype(BF16)
            dkd = lax.dot_general(v.astype(BF16), g_nb, _DIMS["nt"], preferred_element_type=F32)
            dv = jnp.dot(kd.astype(BF16), g_nb, preferred_element_type=F32)
            dk = dkd * w
            e = dkd * kd
            d_end = jnp.sum(e, axis=0, keepdims=True) + d_a * jnp.exp(endb)
            dla = lax.dot_general(l_ref[...], -e, _DIMS["tn"], precision=HI, preferred_element_type=F32) + d_end
            dz = dla * (1.0 - _sigmoid(z)) * (1.0 / GLA_TAU)
            dlr = lax.dot_general(dz, wg_v, _DIMS["nt"], precision=HI, preferred_element_type=F32)
            dwg_ref[...] += lax.dot_general(lr, dz, _DIMS["tn"], precision=HI, preferred_element_type=F32)
            dbg_ref[...] += jnp.sum(dz, axis=0, keepdims=True)
            ga_ref[...] = a_full * g_n
            dp_ref[rows, 0:KW] = dq.astype(dp_ref.dtype)
            dp_ref[rows, KW:2 * KW] = dk.astype(dp_ref.dtype)
            dp_ref[rows, 2 * KW:2 * KW + VW] = dv.astype(dp_ref.dtype)
            dp_ref[rows, 2 * KW + VW:2 * KW + 2 * VW] = dg.astype(dp_ref.dtype)
            dp_ref[rows, 2 * KW + 2 * VW:GLA_COLS] = dlr.astype(dp_ref.dtype)

    full = lambda shape: pl.BlockSpec(shape, lambda i: tuple(0 for _ in shape))
    rev = lambda i: (ng - 1 - i, 0)
    vm = 4 * _nbytes((rb, GLA_COLS), F32) + 2 * _nbytes((rb, VW), F32) + 2 * _nbytes((GLA_G + 1, KW, VW), F32)
    vm += 16 * _nbytes((KW, VW), F32)
    return pl.pallas_call(
        body,
        out_shape=(jax.ShapeDtypeStruct((T, GLA_COLS), BF16), jax.ShapeDtypeStruct((LRW, KW), F32),
                   jax.ShapeDtypeStruct((1, KW), F32), jax.ShapeDtypeStruct((1, VW), F32)),
        grid=(ng,),
        in_specs=[pl.BlockSpec((rb, GLA_COLS), rev), pl.BlockSpec((rb, VW), rev),
                  pl.BlockSpec((GLA_G, KW, VW), lambda i: (ng - 1 - i, 0, 0)),
                  pl.BlockSpec((1, KW, VW), lambda i: (jnp.maximum((ng - 1 - i) * GLA_G - 1, 0), 0, 0)),
                  full((LRW, KW)), full((1, KW)), full((1, VW)), full((KW, VW)), full((VW, VW)), full((CHUNK, CHUNK))],
        out_specs=(pl.BlockSpec((rb, GLA_COLS), rev), full((LRW, KW)), full((1, KW)), full((1, VW))),
        scratch_shapes=[pltpu.VMEM((KW, VW), F32)],
        compiler_params=_cparams(("arbitrary",), vm),
        name=name,
    )(p, dy, states, states, wg, bg, gn, mask, hm, low)


CONV_TM = 512
HALO = 32
CONV_RB = 64


def _glu(u):
    a = u[:, 0:CW]
    b = u[:, CW:2 * CW]
    return a * _sigmoid(b)


def _conv_taps(buf_ref, w_ref, rb0, first_tap):
    acc = jnp.zeros((CONV_RB, CW), F32)
    for j in range(CK):
        s = rb0 + first_tap(j)
        acc = acc + w_ref[j:j + 1, :] * buf_ref[s:s + CONV_RB, :]
    return acc


def _ln_fwd(c, lg, lb):
    mu = jnp.mean(c, axis=-1, keepdims=True)
    xc = c - mu
    rstd = lax.rsqrt(jnp.mean(xc * xc, axis=-1, keepdims=True) + EPS)
    n = xc * rstd
    return n, rstd, n * lg + lb


def _conv_fwd(u, w, b, lg, lb, name):
    T = u.shape[0]
    tm = CONV_TM

    def body(u_ref, uh_ref, w_ref, b_ref, lg_ref, lb_ref, o_ref, hbuf):
        i = pl.program_id(0)
        hbuf[0:HALO, :] = jnp.where(i > 0, _glu(uh_ref[...]), 0.0)
        hbuf[HALO:HALO + tm, :] = _glu(u_ref[...])
        for r in range(tm // CONV_RB):
            acc = _conv_taps(hbuf, w_ref, r * CONV_RB, lambda j: HALO - (CK - 1) + j)
            _, _, zz = _ln_fwd(acc + b_ref[...], lg_ref[...], lb_ref[...])
            o_ref[r * CONV_RB:(r + 1) * CONV_RB, :] = (zz * _sigmoid(zz)).astype(o_ref.dtype)

    vec = pl.BlockSpec((1, CW), lambda i: (0, 0))
    return pl.pallas_call(
        body,
        out_shape=jax.ShapeDtypeStruct((T, CW), BF16),
        grid=(T // tm,),
        in_specs=[pl.BlockSpec((tm, CONV_COLS), lambda i: (i, 0)),
                  pl.BlockSpec((HALO, CONV_COLS), lambda i: (jnp.maximum(i * (tm // HALO) - 1, 0), 0)),
                  pl.BlockSpec((HALO, CW), lambda i: (0, 0)), vec, vec, vec],
        out_specs=pl.BlockSpec((tm, CW), lambda i: (i, 0)),
        scratch_shapes=[pltpu.VMEM((tm + HALO, CW), F32)],
        compiler_params=_cparams(("arbitrary",), 8 * _nbytes((tm, CONV_COLS), F32)),
        name=name,
    )(u, u, w, b, lg, lb)


def _conv_bwd_dc(u, dout, w, b, lg, lb, name):
    T = u.shape[0]
    tm = CONV_TM
    nsteps = T // tm

    def body(u_ref, uh_ref, do_ref, w_ref, b_ref, lg_ref, lb_ref, dc_ref, dw_ref, db_ref, dlg_ref, dlb_ref, hbuf, dwacc):
        i = pl.program_id(0)

        @pl.when(i == 0)
        def _():
            dwacc[...] = jnp.zeros_like(dwacc)
            db_ref[...] = jnp.zeros_like(db_ref)
            dlg_ref[...] = jnp.zeros_like(dlg_ref)
            dlb_ref[...] = jnp.zeros_like(dlb_ref)

        hbuf[0:HALO, :] = jnp.where(i > 0, _glu(uh_ref[...]), 0.0)
        hbuf[HALO:HALO + tm, :] = _glu(u_ref[...])
        for r in range(tm // CONV_RB):
            rows = slice(r * CONV_RB, (r + 1) * CONV_RB)
            acc = _conv_taps(hbuf, w_ref, r * CONV_RB, lambda j: HALO - (CK - 1) + j)
            n, rstd, zz = _ln_fwd(acc + b_ref[...], lg_ref[...], lb_ref[...])
            sg = _sigmoid(zz)
            dz = do_ref[rows, :].astype(F32) * (sg * (1.0 + zz * (1.0 - sg)))
            dlg_ref[...] += jnp.sum(dz * n, axis=0, keepdims=True)
            dlb_ref[...] += jnp.sum(dz, axis=0, keepdims=True)
            dn = dz * lg_ref[...]
            dc = rstd * (dn - jnp.mean(dn, axis=-1, keepdims=True) - n * jnp.mean(dn * n, axis=-1, keepdims=True))
            dc_ref[rows, :] = dc
            db_ref[...] += jnp.sum(dc, axis=0, keepdims=True)
            for j in range(CK):
                s = r * CONV_RB + HALO - (CK - 1) + j
                prod = dc * hbuf[s:s + CONV_RB, :]
                dwacc[j] += jnp.sum(prod.reshape(CONV_RB // 8, 8, CW), axis=0)

        @pl.when(i == nsteps - 1)
        def _():
            dw_ref[...] = jnp.sum(dwacc[...], axis=1)

    vec = pl.BlockSpec((1, CW), lambda i: (0, 0))
    return pl.pallas_call(
        body,
        out_shape=(jax.ShapeDtypeStruct((T, CW), F32), jax.ShapeDtypeStruct((HALO, CW), F32),
                   jax.ShapeDtypeStruct((1, CW), F32), jax.ShapeDtypeStruct((1, CW), F32), jax.ShapeDtypeStruct((1, CW), F32)),
        grid=(nsteps,),
        in_specs=[pl.BlockSpec((tm, CONV_COLS), lambda i: (i, 0)),
                  pl.BlockSpec((HALO, CONV_COLS), lambda i: (jnp.maximum(i * (tm // HALO) - 1, 0), 0)),
                  pl.BlockSpec((tm, CW), lambda i: (i, 0)),
                  pl.BlockSpec((HALO, CW), lambda i: (0, 0)), vec, vec, vec],
        out_specs=(pl.BlockSpec((tm, CW), lambda i: (i, 0)), pl.BlockSpec((HALO, CW), lambda i: (0, 0)), vec, vec, vec),
        scratch_shapes=[pltpu.VMEM((tm + HALO, CW), F32), pltpu.VMEM((HALO, 8, CW), F32)],
        compiler_params=_cparams(("arbitrary",), 10 * _nbytes((tm, CONV_COLS), F32)),
        name=name,
    )(u, u, dout, w, b, lg, lb)


def _conv_bwd_du(u, dc, w, name):
    T = u.shape[0]
    tm = CONV_TM
    nsteps = T // tm

    def body(u_ref, dc_ref, dch_ref, w_ref, du_ref, dcbuf):
        i = pl.program_id(0)
        dcbuf[0:tm, :] = dc_ref[...]
        dcbuf[tm:tm + HALO, :] = jnp.where(i < nsteps - 1, dch_ref[...], 0.0)
        for r in range(tm // CONV_RB):
            rows = slice(r * CONV_RB, (r + 1) * CONV_RB)
            dh = _conv_taps(dcbuf, w_ref, r * CONV_RB, lambda j: (CK - 1) - j)
            a = u_ref[rows, 0:CW]
            sb = _sigmoid(u_ref[rows, CW:2 * CW])
            du_ref[rows, 0:CW] = (dh * sb).astype(du_ref.dtype)
            du_ref[rows, CW:2 * CW] = (dh * a * sb * (1.0 - sb)).astype(du_ref.dtype)

    return pl.pallas_call(
        body,
        out_shape=jax.ShapeDtypeStruct((T, CONV_COLS), BF16),
        grid=(nsteps,),
        in_specs=[pl.BlockSpec((tm, CONV_COLS), lambda i: (i, 0)),
                  pl.BlockSpec((tm, CW), lambda i: (i, 0)),
                  pl.BlockSpec((HALO, CW), lambda i: (jnp.minimum((i + 1) * (tm // HALO), T // HALO - 1), 0)),
                  pl.BlockSpec((HALO, CW), lambda i: (0, 0))],
        out_specs=pl.BlockSpec((tm, CONV_COLS), lambda i: (i, 0)),
        scratch_shapes=[pltpu.VMEM((tm + HALO, CW), F32)],
        compiler_params=_cparams(("arbitrary",), 8 * _nbytes((tm, CONV_COLS), F32)),
        name=name,
    )(u, dc, dc, w)


ATT_G = 4


def _att_load_kv(p_any, kbuf, vbuf, sems, T):
    kc = pltpu.make_async_copy(p_any.at[:, pl.ds(AW, AW)], kbuf.at[pl.ds(LEFT, T), :], sems.at[0])
    vc = pltpu.make_async_copy(p_any.at[:, pl.ds(2 * AW, AW)], vbuf.at[pl.ds(LEFT, T), :], sems.at[1])
    kc.start()
    vc.start()
    kbuf[0:LEFT, :] = jnp.zeros((LEFT, AW), BF16)
    vbuf[0:LEFT, :] = jnp.zeros((LEFT, AW), BF16)
    kc.wait()
    vc.wait()


ATT_QB = CHUNK * ATT_G
ATT_KB = LEFT + ATT_QB
REL_PAD = 384
TOEP = 1024


def _att_consts():
    m = np.arange(TOEP)
    d = ATT_KB - 1 - m
    idx = np.clip(d, -128, 128) + 128
    sel = (np.arange(REL_PAD)[:, None] == idx[None, :]) & (m[None, :] < ATT_QB + ATT_KB - 1)
    return jnp.asarray(sel.astype(np.float32))


def _att_build_bias(rel_ref, sel_ref, bias_scr):
    tr = jnp.dot(rel_ref[...], sel_ref[...], precision=HI, preferred_element_type=F32)
    qc = lax.broadcasted_iota(jnp.int32, (ATT_QB, ATT_KB), 0) // CHUNK
    kc = lax.broadcasted_iota(jnp.int32, (ATT_QB, ATT_KB), 1) // CHUNK
    band = (kc >= qc) & (kc <= qc + 8)
    for h in range(AH):
        rows = jnp.broadcast_to(tr[h:h + 1, :], (ATT_QB, TOEP))
        toep = pltpu.roll(rows, TOEP - (ATT_QB - 1), 1, stride=1, stride_axis=0)[:, 0:ATT_KB]
        bias_scr[h // 2, (h % 2) * ATT_QB:(h % 2 + 1) * ATT_QB, :] = jnp.where(band, toep, NEG)


def _att_probs(qst, kb, bias_p, n0):
    sc = lax.dot_general(qst, kb, _DIMS["nt"], preferred_element_type=F32) * (64 ** -0.5) + bias_p
    pos = lax.broadcasted_iota(jnp.int32, (2 * ATT_QB, ATT_KB), 1)
    sc = jnp.where(pos >= CHUNK * (8 - n0), sc, NEG)
    mx = jnp.max(sc, axis=-1, keepdims=True)
    ex = jnp.exp(sc - mx)
    return ex / jnp.sum(ex, axis=-1, keepdims=True)


def _head_stack(a2, lo):
    zero = jnp.zeros_like(a2)
    return jnp.concatenate([jnp.where(lo, a2, zero), jnp.where(lo, zero, a2)], axis=0)


def _att_fwd(p, rel, sel, name):
    T = p.shape[0]

    def body(q_ref, p_any, rel_ref, sel_ref, o_ref, kbuf, vbuf, bias_scr, sems):
        i = pl.program_id(0)

        @pl.when(i == 0)
        def _():
            _att_load_kv(p_any, kbuf, vbuf, sems, T)
            _att_build_bias(rel_ref, sel_ref, bias_scr)

        lo = lax.broadcasted_iota(jnp.int32, (ATT_QB, 128), 1) < 64
        n0 = i * ATT_G
        start = pl.multiple_of(i * ATT_QB, ATT_QB)
        for hp in range(AH // 2):
            cols = slice(hp * 128, (hp + 1) * 128)
            kb = kbuf[pl.ds(start, ATT_KB), cols]
            vb = vbuf[pl.ds(start, ATT_KB), cols]
            pr = _att_probs(_head_stack(q_ref[:, cols], lo), kb, bias_scr[hp], n0)
            pv = jnp.dot(pr.astype(BF16), vb, preferred_element_type=F32)
            o_ref[:, cols] = jnp.where(lo, pv[0:ATT_QB], pv[ATT_QB:2 * ATT_QB]).astype(o_ref.dtype)

    vm = 2 * _nbytes((T + LEFT, AW), BF16) + 8 * _nbytes((2 * ATT_QB, ATT_KB), F32) + (8 << 20)
    return pl.pallas_call(
        body,
        out_shape=jax.ShapeDtypeStruct((T, AW), BF16),
        grid=(T // ATT_QB,),
        in_specs=[pl.BlockSpec((ATT_QB, AW), lambda i: (i, 0)), pl.BlockSpec(memory_space=pl.ANY),
                  pl.BlockSpec((8, REL_PAD), lambda i: (0, 0)), pl.BlockSpec((REL_PAD, TOEP), lambda i: (0, 0))],
        out_specs=pl.BlockSpec((ATT_QB, AW), lambda i: (i, 0)),
        scratch_shapes=[pltpu.VMEM((T + LEFT, AW), BF16), pltpu.VMEM((T + LEFT, AW), BF16),
                        pltpu.VMEM((AH // 2, 2 * ATT_QB, ATT_KB), F32), pltpu.SemaphoreType.DMA((2,))],
        compiler_params=_cparams(("arbitrary",), vm),
        name=name,
    )(p, p, rel, sel)


def _att_bwd(p, do, rel, sel, name):
    T = p.shape[0]
    nsteps = T // ATT_QB

    def body(q_ref, p_any, do_ref, rel_ref, sel_ref, dp_any, drel_ref,
             kbuf, vbuf, dqbuf, dkbuf, dvbuf, bias_scr, dbias_scr, dtr_scr, sems):
        i = pl.program_id(0)

        @pl.when(i == 0)
        def _():
            _att_load_kv(p_any, kbuf, vbuf, sems, T)
            _att_build_bias(rel_ref, sel_ref, bias_scr)
            dkbuf[...] = jnp.zeros_like(dkbuf)
            dvbuf[...] = jnp.zeros_like(dvbuf)
            dbias_scr[...] = jnp.zeros_like(dbias_scr)

        lo = lax.broadcasted_iota(jnp.int32, (ATT_QB, 128), 1) < 64
        n0 = i * ATT_G
        start = pl.multiple_of(i * ATT_QB, ATT_QB)
        for hp in range(AH // 2):
            cols = slice(hp * 128, (hp + 1) * 128)
            kb = kbuf[pl.ds(start, ATT_KB), cols]
            vb = vbuf[pl.ds(start, ATT_KB), cols]
            qst = _head_stack(q_ref[:, cols], lo)
            dost = _head_stack(do_ref[:, cols].astype(BF16), lo)
            pr = _att_probs(qst, kb, bias_scr[hp], n0)
            dpr = lax.dot_general(dost, vb, _DIMS["nt"], preferred_element_type=F32)
            ds = pr * (dpr - jnp.sum(dpr * pr, axis=-1, keepdims=True))
            dbias_scr[hp] += ds
            dsb = (ds * (64 ** -0.5)).astype(BF16)
            dq = jnp.dot(dsb, kb, preferred_element_type=F32)
            dqbuf[pl.ds(start, ATT_QB), cols] = jnp.where(lo, dq[0:ATT_QB], dq[ATT_QB:2 * ATT_QB]).astype(BF16)
            dkbuf[pl.ds(start, ATT_KB), cols] += lax.dot_general(dsb, qst, _DIMS["tn"], preferred_element_type=F32)
            dvbuf[pl.ds(start, ATT_KB), cols] += lax.dot_general(pr.astype(BF16), dost, _DIMS["tn"], preferred_element_type=F32)

        @pl.when(i == nsteps - 1)
        def _():
            kbuf[pl.ds(LEFT, T), :] = dkbuf[pl.ds(LEFT, T), :].astype(BF16)
            vbuf[pl.ds(LEFT, T), :] = dvbuf[pl.ds(LEFT, T), :].astype(BF16)
            cps = [pltpu.make_async_copy(dqbuf, dp_any.at[:, pl.ds(0, AW)], sems.at[0]),
                   pltpu.make_async_copy(kbuf.at[pl.ds(LEFT, T), :], dp_any.at[:, pl.ds(AW, AW)], sems.at[1]),
                   pltpu.make_async_copy(vbuf.at[pl.ds(LEFT, T), :], dp_any.at[:, pl.ds(2 * AW, AW)], sems.at[2])]
            for cp in cps:
                cp.start()
            dtr_scr[...] = jnp.zeros_like(dtr_scr)
            ri = lax.broadcasted_iota(jnp.int32, (ATT_QB, ATT_QB), 0)
            ci = lax.broadcasted_iota(jnp.int32, (ATT_QB, ATT_QB), 1)
            flip = jnp.where(ri + ci == ATT_QB - 1, 1.0, 0.0)
            for h in range(AH):
                db = dbias_scr[h // 2, (h % 2) * ATT_QB:(h % 2 + 1) * ATT_QB, :]
                db = jnp.dot(flip, db, precision=HI, preferred_element_type=F32)
                wide = jnp.concatenate([db, jnp.zeros((ATT_QB, TOEP - ATT_KB), F32)], axis=1)
                diag = pltpu.roll(wide, 0, 1, stride=1, stride_axis=0)
                dtr_scr[h:h + 1, :] = jnp.sum(diag, axis=0, keepdims=True)
            drel_ref[...] = lax.dot_general(dtr_scr[...], sel_ref[...], _DIMS["nt"], precision=HI, preferred_element_type=F32)
            for cp in cps:
                cp.wait()

    vm = 3 * _nbytes((T + LEFT, AW), BF16) + 2 * _nbytes((T + LEFT, AW), F32) + 12 * _nbytes((2 * ATT_QB, ATT_KB), F32) + (8 << 20)
    return pl.pallas_call(
        body,
        out_shape=(jax.ShapeDtypeStruct((T, ATT_COLS), BF16), jax.ShapeDtypeStruct((8, REL_PAD), F32)),
        grid=(nsteps,),
        in_specs=[pl.BlockSpec((ATT_QB, AW), lambda i: (i, 0)), pl.BlockSpec(memory_space=pl.ANY),
                  pl.BlockSpec((ATT_QB, AW), lambda i: (i, 0)),
                  pl.BlockSpec((8, REL_PAD), lambda i: (0, 0)), pl.BlockSpec((REL_PAD, TOEP), lambda i: (0, 0))],
        out_specs=(pl.BlockSpec(memory_space=pl.ANY), pl.BlockSpec((8, REL_PAD), lambda i: (0, 0))),
        scratch_shapes=[pltpu.VMEM((T + LEFT, AW), BF16), pltpu.VMEM((T + LEFT, AW), BF16), pltpu.VMEM((T, AW), BF16),
                        pltpu.VMEM((T + LEFT, AW), F32), pltpu.VMEM((T + LEFT, AW), F32),
                        pltpu.VMEM((AH // 2, 2 * ATT_QB, ATT_KB), F32), pltpu.VMEM((AH // 2, 2 * ATT_QB, ATT_KB), F32),
                        pltpu.VMEM((8, TOEP), F32), pltpu.SemaphoreType.DMA((3,))],
        compiler_params=_cparams(("arbitrary",), vm),
        name=name,
    )(p, p, do, rel, sel)


def _layer_fwd(h, wl, consts, tag):
    xn = _rmsnorm_fwd(h, wl["norm_mix"], f"{tag}_norm_mix")
    p_gla = _mm(xn, wl["w_gla"], mode="nn", out_dtype=F32, name=f"{tag}_proj_gla")
    p_conv = _mm(xn, wl["w_conv"], mode="nn", out_dtype=F32, name=f"{tag}_proj_conv")
    p_att = _mm(xn, wl["w_att"], mode="nn", out_dtype=BF16, name=f"{tag}_proj_att")
    o_gla, states = _gla_fwd(p_gla, wl["wg"], wl["bg"], wl["gla_norm"], consts, f"{tag}_gla_fwd")
    o_conv = _conv_fwd(p_conv, wl["w_dw"], wl["b_dw"], wl["ln_g"], wl["ln_b"], f"{tag}_conv_fwd")
    rel = jnp.pad(wl["rel_bias"], ((0, 8 - AH), (0, REL_PAD - N_REL)))
    o_att = _att_fwd(p_att, rel, consts[3], f"{tag}_att_fwd")
    h1 = _mm(o_gla, wl["w_out_g"], mode="nn", out_dtype=F32, name=f"{tag}_out_gla", epi="add", extra=h)
    h1 = _mm(o_conv, wl["w_out_c"], mode="nn", out_dtype=F32, name=f"{tag}_out_conv", epi="add", extra=h1)
    h1 = _mm(o_att, wl["w_out_a"], mode="nn", out_dtype=F32, name=f"{tag}_out_att", epi="add", extra=h1)
    xn2 = _rmsnorm_fwd(h1, wl["norm_ffn"], f"{tag}_norm_ffn")
    u = _mm(xn2, wl["w_up"], mode="nn", out_dtype=BF16, tn=1024, b_chips=True, name=f"{tag}_mlp_up")
    h2 = _mm(u, wl["w_down"], mode="nn", out_dtype=F32, tk=1024, a_pro="relu2", epi="add", extra=h1, name=f"{tag}_mlp_down")
    saved = dict(h=h, xn=xn, p_gla=p_gla, p_conv=p_conv, p_att=p_att, states=states, o_gla=o_gla, o_conv=o_conv,
                 o_att=o_att, rel=rel, h1=h1, xn2=xn2, u=u)
    return h2, saved


def _layer_bwd(dh2, sv, wl, consts, tag):
    g = {}
    du = _mm(dh2, wl["w_down"], mode="nt", out_dtype=BF16, tn=1024, epi="relu2grad", extra=sv["u"], name=f"{tag}_mlp_down_dx")
    g["w_down"] = _mm(sv["u"], dh2, mode="tn", out_dtype=F32, tm=1024, tn=512, tk=512, a_pro="relu2", name=f"{tag}_mlp_down_dw")
    dxn2 = _mm(du, wl["w_up"], mode="nt", out_dtype=F32, tk=1024, b_chips=True, name=f"{tag}_mlp_up_dx")
    g["w_up"] = _mm(sv["xn2"], du, mode="tn", out_dtype=F32, tm=1024, tn=512, tk=512, out_chips=True, name=f"{tag}_mlp_up_dw")
    dh1, g["norm_ffn"] = _rmsnorm_bwd(dxn2, sv["h1"], wl["norm_ffn"], dh2, f"{tag}_norm_ffn_bwd")
    d_gla = _mm(dh1, wl["w_out_g"], mode="nt", out_dtype=F32, name=f"{tag}_out_gla_dx")
    d_conv = _mm(dh1, wl["w_out_c"], mode="nt", out_dtype=F32, name=f"{tag}_out_conv_dx")
    d_att = _mm(dh1, wl["w_out_a"], mode="nt", out_dtype=F32, name=f"{tag}_out_att_dx")
    g["w_out_g"] = _mm(sv["o_gla"], dh1, mode="tn", out_dtype=F32, tm=VW, tn=512, tk=512, name=f"{tag}_out_gla_dw")
    g["w_out_c"] = _mm(sv["o_conv"], dh1, mode="tn", out_dtype=F32, tm=CW, tn=512, tk=512, name=f"{tag}_out_conv_dw")
    g["w_out_a"] = _mm(sv["o_att"], dh1, mode="tn", out_dtype=F32, tm=AW, tn=512, tk=512, name=f"{tag}_out_att_dw")
    dp_gla, g["wg"], g["bg"], g["gla_norm"] = _gla_bwd(sv["p_gla"], d_gla, sv["states"], wl["wg"], wl["bg"], wl["gla_norm"],
                                                       consts, f"{tag}_gla_bwd")
    dc, g["w_dw"], g["b_dw"], g["ln_g"], g["ln_b"] = _conv_bwd_dc(sv["p_conv"], d_conv, wl["w_dw"], wl["b_dw"], wl["ln_g"],
                                                                  wl["ln_b"], f"{tag}_conv_bwd_dc")
    dp_conv = _conv_bwd_du(sv["p_conv"], dc, wl["w_dw"], f"{tag}_conv_bwd_du")
    dp_att, drel = _att_bwd(sv["p_att"], d_att, sv["rel"], consts[3], f"{tag}_att_bwd")
    g["rel_bias"] = drel[0:AH, 0:N_REL]
    g["w_gla"] = _mm(sv["xn"], dp_gla, mode="tn", out_dtype=F32, tm=512, tk=512, name=f"{tag}_proj_gla_dw")
    g["w_conv"] = _mm(sv["xn"], dp_conv, mode="tn", out_dtype=F32, tm=1024, tn=512, tk=512, name=f"{tag}_proj_conv_dw")
    g["w_att"] = _mm(sv["xn"], dp_att, mode="tn", out_dtype=F32, tm=512, tk=512, name=f"{tag}_proj_att_dw")
    dxn = _mm(dp_gla, wl["w_gla"], mode="nt", out_dtype=F32, name=f"{tag}_proj_gla_dx")
    dxn = _mm(dp_conv, wl["w_conv"], mode="nt", out_dtype=F32, epi="add", extra=dxn, name=f"{tag}_proj_conv_dx")
    dxn = _mm(dp_att, wl["w_att"], mode="nt", out_dtype=F32, epi="add", extra=dxn, name=f"{tag}_proj_att_dx")
    dh, g["norm_mix"] = _rmsnorm_bwd(dxn, sv["h"], wl["norm_mix"], dh1, f"{tag}_norm_mix_bwd")
    return dh, g


def _local_step(x, target, layers, norm_final):
    consts = _gla_consts() + (_att_consts(),)
    h = x
    saved = []
    for l, wl in enumerate(layers):
        h, sv = _layer_fwd(h, wl, consts, f"l{l}")
        saved.append(sv)
    loss, dh, g_final = _final_loss(h, norm_final, target, "final_loss")
    grads = [None] * len(layers)
    for l in reversed(range(len(layers))):
        dh, grads[l] = _layer_bwd(dh, saved[l], layers[l], consts, f"l{l}")
    return loss, dh, grads, g_final


ANY = pl.BlockSpec(memory_space=pl.ANY)


def _place():
    x, y, c = lax.axis_index("x"), lax.axis_index("y"), lax.axis_index("c")
    chips = [(1 - x, y), (x, 1 - y), (1 - x, 1 - y)]
    return x, y, c, chips


def _gather_chips(items):
    n = len(items)

    def body(*refs):
        srcs, outs = refs[:n], refs[n:2 * n]
        send_sems, recv_sems, loc_sems = refs[2 * n:]
        x, y, c, chips = _place()
        me = 2 * x + y

        def half(t, chip, cc):
            rh = items[t].shape[1] // 2
            return outs[t].at[chip, :, pl.ds(cc * rh, rh), :]

        def copy(t, k, src, dst, to):
            return pltpu.make_async_remote_copy(src_ref=src, dst_ref=dst, send_sem=send_sems.at[t, k],
                                                recv_sem=recv_sems.at[t, k], device_id=to, device_id_type=MESH)

        local = [pltpu.make_async_copy(srcs[t], outs[t].at[me], loc_sems.at[t]) for t in range(n)]
        for cp in local:
            cp.start()
        sent = []
        for t in range(n):
            rh = items[t].shape[1] // 2
            mine = srcs[t].at[:, pl.ds(c * rh, rh), :]
            for k, (px, py) in enumerate(chips):
                sent.append(copy(t, k, mine, half(t, me, c), (px, py, c)))
                sent[-1].start()
        for k, (px, py) in enumerate(chips):
            for t in range(n):
                blk = half(t, 2 * px + py, c)
                copy(t, k, blk, blk, (px, py, c)).wait_recv()
                sent.append(copy(t, 3 + k, blk, blk, (x, y, 1 - c)))
                sent[-1].start()
        for k, (px, py) in enumerate(chips):
            for t in range(n):
                blk = half(t, 2 * px + py, 1 - c)
                copy(t, 3 + k, blk, blk, (x, y, 1 - c)).wait_recv()
        for cp in sent:
            cp.wait_send()
        for cp in local:
            cp.wait()

    return pl.pallas_call(
        body,
        out_shape=[jax.ShapeDtypeStruct((4,) + a.shape, a.dtype) for a in items],
        in_specs=[ANY] * n,
        out_specs=[ANY] * n,
        scratch_shapes=[pltpu.SemaphoreType.DMA((n, 6)), pltpu.SemaphoreType.DMA((n, 6)), pltpu.SemaphoreType.DMA((n,))],
        name="gather_weights",
    )(*items)


def _pair_exchange(items):
    n = len(items)

    def body(*refs):
        srcs, outs = refs[:n], refs[n:2 * n]
        send_sems, recv_sems = refs[2 * n:]
        x, y, c, _ = _place()
        cps = []
        for t in range(n):
            rh = items[t].shape[2] // 2
            cps.append(pltpu.make_async_remote_copy(
                src_ref=srcs[t].at[:, :, pl.ds((1 - c) * rh, rh), :], dst_ref=outs[t], send_sem=send_sems.at[t],
                recv_sem=recv_sems.at[t], device_id=(x, y, 1 - c), device_id_type=MESH))
            cps[-1].start()
        for cp in cps:
            cp.wait()

    return pl.pallas_call(
        body,
        out_shape=[jax.ShapeDtypeStruct(a.shape[:2] + (a.shape[2] // 2, a.shape[3]), a.dtype) for a in items],
        in_specs=[ANY] * n,
        out_specs=[ANY] * n,
        scratch_shapes=[pltpu.SemaphoreType.DMA((n,)), pltpu.SemaphoreType.DMA((n,))],
        name="reduce_pair_exchange",
    )(*items)


def _row_tile(rows, cols, itemsize=4, budget=1 << 20):
    t = rows
    while t % 2 == 0 and t // 2 >= 8 and (t // 2) % 8 == 0 and t * cols * itemsize > budget:
        t //= 2
    return t


def _pair_add(full, got, c, name):
    _, L, R, C = full.shape
    rh = R // 2
    tr = _row_tile(rh, C)
    nb = rh // tr

    def body(c_ref, a_ref, b_ref, o_ref):
        o_ref[...] = (a_ref[...] + b_ref[...]).astype(o_ref.dtype)

    grid_spec = pltpu.PrefetchScalarGridSpec(
        num_scalar_prefetch=1,
        grid=(4, L, nb),
        in_specs=[pl.BlockSpec((1, 1, tr, C), lambda j, l, i, c_ref: (j, l, c_ref[0] * nb + i, 0)),
                  pl.BlockSpec((1, 1, tr, C), lambda j, l, i, c_ref: (j, l, i, 0))],
        out_specs=pl.BlockSpec((1, 1, tr, C), lambda j, l, i, c_ref: (j, l, i, 0)),
    )
    return pl.pallas_call(
        body,
        out_shape=jax.ShapeDtypeStruct((4, L, rh, C), BF16),
        grid_spec=grid_spec,
        compiler_params=_cparams(("parallel", "parallel", "parallel"), 8 * tr * C * 4),
        name=name,
    )(jnp.reshape(c, (1,)).astype(jnp.int32), full, got)


def _chip_scatter(items):
    n = len(items)

    def body(*refs):
        srcs, outs = refs[:n], refs[n:2 * n]
        send_sems, recv_sems, loc_sems = refs[2 * n:]
        x, y, c, chips = _place()
        me = 2 * x + y
        local = [pltpu.make_async_copy(srcs[t].at[me], outs[t].at[me], loc_sems.at[t]) for t in range(n)]
        for cp in local:
            cp.start()
        cps = []
        for t in range(n):
            for k, (px, py) in enumerate(chips):
                cps.append(pltpu.make_async_remote_copy(
                    src_ref=srcs[t].at[2 * px + py], dst_ref=outs[t].at[me], send_sem=send_sems.at[t, k],
                    recv_sem=recv_sems.at[t, k], device_id=(px, py, c), device_id_type=MESH))
                cps[-1].start()
        for t in range(n):
            for k, (px, py) in enumerate(chips):
                blk = outs[t].at[2 * px + py]
                pltpu.make_async_remote_copy(src_ref=blk, dst_ref=blk, send_sem=send_sems.at[t, k], recv_sem=recv_sems.at[t, k],
                                             device_id=(px, py, c), device_id_type=MESH).wait_recv()
        for cp in cps:
            cp.wait_send()
        for cp in local:
            cp.wait()

    return pl.pallas_call(
        body,
        out_shape=[jax.ShapeDtypeStruct(a.shape, a.dtype) for a in items],
        in_specs=[ANY] * n,
        out_specs=[ANY] * n,
        scratch_shapes=[pltpu.SemaphoreType.DMA((n, 3)), pltpu.SemaphoreType.DMA((n, 3)), pltpu.SemaphoreType.DMA((n,))],
        name="reduce_chip_scatter",
    )(*items)


def _sum_chips(parts, name):
    _, L, rh, C = parts.shape
    tr = _row_tile(rh, C)

    def body(p_ref, o_ref):
        acc = p_ref[0].astype(F32)
        for j in range(1, 4):
            acc = acc + p_ref[j].astype(F32)
        o_ref[...] = acc

    return pl.pallas_call(
        body,
        out_shape=jax.ShapeDtypeStruct((L, rh, C), F32),
        grid=(L, rh // tr),
        in_specs=[pl.BlockSpec((4, 1, tr, C), lambda l, i: (0, l, i, 0))],
        out_specs=pl.BlockSpec((1, tr, C), lambda l, i: (l, i, 0)),
        compiler_params=_cparams(("parallel", "parallel"), 16 * tr * C * 4),
        name=name,
    )(parts)


def _pair_allgather(groups):
    flat = [(w, l) for w, grp in enumerate(groups) for l in range(len(grp))]
    items = [groups[w][l] for w, l in flat]
    n, nw = len(items), len(groups)

    def body(*refs):
        srcs, outs = refs[:n], refs[n:n + nw]
        send_sems, recv_sems, loc_sems = refs[n + nw:]
        x, y, c, _ = _place()
        cps, local = [], []
        for t, (w, l) in enumerate(flat):
            rh = items[t].shape[1]
            mine = outs[w].at[pl.ds(l, 1), pl.ds(c * rh, rh), :]
            local.append(pltpu.make_async_copy(srcs[t], mine, loc_sems.at[t]))
            local[-1].start()
            cps.append(pltpu.make_async_remote_copy(src_ref=srcs[t], dst_ref=mine, send_sem=send_sems.at[t],
                                                    recv_sem=recv_sems.at[t], device_id=(x, y, 1 - c), device_id_type=MESH))
            cps[-1].start()
        for t, (w, l) in enumerate(flat):
            rh = items[t].shape[1]
            theirs = outs[w].at[pl.ds(l, 1), pl.ds((1 - c) * rh, rh), :]
            pltpu.make_async_remote_copy(src_ref=theirs, dst_ref=theirs, send_sem=send_sems.at[t], recv_sem=recv_sems.at[t],
                                         device_id=(x, y, 1 - c), device_id_type=MESH).wait_recv()
        for cp in cps:
            cp.wait_send()
        for cp in local:
            cp.wait()

    return pl.pallas_call(
        body,
        out_shape=[jax.ShapeDtypeStruct((len(grp), 2 * grp[0].shape[1], grp[0].shape[2]), grp[0].dtype) for grp in groups],
        in_specs=[ANY] * n,
        out_specs=[ANY] * nw,
        scratch_shapes=[pltpu.SemaphoreType.DMA((n,)), pltpu.SemaphoreType.DMA((n,)), pltpu.SemaphoreType.DMA((n,))],
        name="reduce_pair_allgather",
    )(*items)


def _allreduce_small(v):
    R = v.shape[0]

    def body(v_ref, o_ref, slots, send_sems, recv_sems):
        x, y, c, _ = _place()
        me = 4 * x + 2 * y + c
        slots[me] = v_ref[...]
        cps = []
        for r in range(1, 8):
            px, py, pc = x ^ (r >> 2), y ^ ((r >> 1) & 1), c ^ (r & 1)
            cps.append(pltpu.make_async_remote_copy(src_ref=v_ref, dst_ref=slots.at[me], send_sem=send_sems.at[r - 1],
                                                    recv_sem=recv_sems.at[r - 1], device_id=(px, py, pc), device_id_type=MESH))
            cps[-1].start()
        for r in range(1, 8):
            px, py, pc = x ^ (r >> 2), y ^ ((r >> 1) & 1), c ^ (r & 1)
            theirs = slots.at[4 * px + 2 * py + pc]
            pltpu.make_async_remote_copy(src_ref=theirs, dst_ref=theirs, send_sem=send_sems.at[r - 1], recv_sem=recv_sems.at[r - 1],
                                         device_id=(px, py, pc), device_id_type=MESH).wait_recv()
        acc = slots[0]
        for j in range(1, 8):
            acc = acc + slots[j]
        o_ref[...] = acc
        for cp in cps:
            cp.wait_send()

    return pl.pallas_call(
        body,
        out_shape=jax.ShapeDtypeStruct(v.shape, F32),
        in_specs=[pl.BlockSpec(memory_space=pltpu.VMEM)],
        out_specs=pl.BlockSpec(memory_space=pltpu.VMEM),
        scratch_shapes=[pltpu.VMEM((8, R, 128), F32), pltpu.SemaphoreType.DMA((7,)), pltpu.SemaphoreType.DMA((7,))],
        name="allreduce_small",
    )(v)


def _adamw_math(w, g, m, v):
    m = ADAM_B1 * m + (1.0 - ADAM_B1) * g
    v = ADAM_B2 * v + (1.0 - ADAM_B2) * (g * g)
    m_hat = m / (1.0 - ADAM_B1 ** ADAM_STEP)
    v_hat = v / (1.0 - ADAM_B2 ** ADAM_STEP)
    delta = -ADAM_LR * (m_hat / (jnp.sqrt(v_hat) + ADAM_EPS) + ADAM_WD * w)
    return delta, m, v


def _adamw(w, g, m, v, name):
    rows, cols = w.shape
    tr = _row_tile(rows, cols, budget=1 << 19)

    def body(w_ref, g_ref, m_ref, v_ref, d_ref, nm_ref, nv_ref):
        d_ref[...], nm_ref[...], nv_ref[...] = _adamw_math(w_ref[...], g_ref[...], m_ref[...], v_ref[...])

    blk = pl.BlockSpec((tr, cols), lambda i: (i, 0))
    out = jax.ShapeDtypeStruct(w.shape, F32)
    return pl.pallas_call(
        body,
        out_shape=(out, out, out),
        grid=(rows // tr,),
        in_specs=[blk] * 4,
        out_specs=(blk, blk, blk),
        compiler_params=_cparams(("parallel",), 16 * tr * cols * 4),
        name=name,
    )(w, g, m, v)


def _adamw_small(ws, gs, ms, vs):
    n = len(ws)

    def body(*refs):
        for t in range(n):
            w_ref, g_ref, m_ref, v_ref = (refs[k * n + t] for k in range(4))
            d_ref, nm_ref, nv_ref = (refs[(4 + k) * n + t] for k in range(3))
            d_ref[...], nm_ref[...], nv_ref[...] = _adamw_math(w_ref[...], g_ref[...], m_ref[...], v_ref[...])

    vmem = pl.BlockSpec(memory_space=pltpu.VMEM)
    outs = [jax.ShapeDtypeStruct(w.shape, F32) for w in ws]
    res = pl.pallas_call(
        body,
        out_shape=outs * 3,
        in_specs=[vmem] * (4 * n),
        out_specs=[vmem] * (3 * n),
        name="adamw_small",
    )(*ws, *gs, *ms, *vs)
    return res[:n], res[n:2 * n], res[2 * n:]


IN_SIZES = (192, 192, 384, 384, 16, 512, 384, 384, 384)
IN_OFFS = tuple(int(v) for v in np.cumsum((0,) + IN_SIZES))
SMALL = ("norm_mix", "w_gla_gate", "b_gla_gate", "gla_norm", "b_dw", "conv_ln_g", "conv_ln_b", "rel_bias", "norm_ffn")


def _pad_cols(a, n):
    return jnp.pad(a, ((0, 0), (0, n - a.shape[1])))


def _split_w_in(w):
    s = [w[:, IN_OFFS[i]:IN_OFFS[i + 1]] for i in range(9)]
    w_gla = jnp.concatenate([_pad_cols(s[0], KW), _pad_cols(s[1], KW), s[2], s[3], _pad_cols(s[4], LRW)], axis=1)
    return w_gla, s[5], jnp.concatenate(s[6:9], axis=1)


def _join_w_in(g):
    gg = g["w_gla"]
    return jnp.concatenate([gg[:, 0:192], gg[:, KW:KW + 192], gg[:, 2 * KW:2 * KW + VW], gg[:, 2 * KW + VW:2 * KW + 2 * VW],
                            gg[:, 2 * KW + 2 * VW:2 * KW + 2 * VW + 16], g["w_conv"], g["w_att"]],
                           axis=1)


def _pack(arrs, rows):
    flat = jnp.concatenate([a.reshape(-1) for a in arrs])
    return jnp.pad(flat, (0, rows * 128 - flat.shape[0])).reshape(rows, 128)


def _unpack(packed, shapes):
    flat = packed.reshape(-1)
    out, off = [], 0
    for s in shapes:
        n = int(np.prod(s))
        out.append(flat[off:off + n].reshape(s))
        off += n
    return out


def kernel(x, norm_mix, w_in, w_gla_gate, b_gla_gate, gla_norm, w_dw, b_dw, conv_ln_g, conv_ln_b, rel_bias, w_out, norm_ffn, w_up, w_down, norm_final, loss_target, m_norm_mix, m_w_in, m_w_gla_gate, m_b_gla_gate, m_gla_norm, m_w_dw, m_b_dw, m_conv_ln_g, m_conv_ln_b, m_rel_bias, m_w_out, m_norm_ffn, m_w_up, m_w_down, m_norm_final, v_norm_mix, v_w_in, v_w_gla_gate, v_b_gla_gate, v_gla_norm, v_w_dw, v_b_dw, v_conv_ln_g, v_conv_ln_b, v_rel_bias, v_w_out, v_norm_ffn, v_w_up, v_w_down, v_norm_final):
    P = dict(norm_mix=norm_mix, w_in=w_in, w_gla_gate=w_gla_gate, b_gla_gate=b_gla_gate, gla_norm=gla_norm, w_dw=w_dw, b_dw=b_dw,
             conv_ln_g=conv_ln_g, conv_ln_b=conv_ln_b, rel_bias=rel_bias, w_out=w_out, norm_ffn=norm_ffn, w_up=w_up,
             w_down=w_down, norm_final=norm_final)
    Mo = dict(norm_mix=m_norm_mix, w_in=m_w_in, w_gla_gate=m_w_gla_gate, b_gla_gate=m_b_gla_gate, gla_norm=m_gla_norm, w_dw=m_w_dw,
              b_dw=m_b_dw, conv_ln_g=m_conv_ln_g, conv_ln_b=m_conv_ln_b, rel_bias=m_rel_bias, w_out=m_w_out, norm_ffn=m_norm_ffn,
              w_up=m_w_up, w_down=m_w_down, norm_final=m_norm_final)
    Vo = dict(norm_mix=v_norm_mix, w_in=v_w_in, w_gla_gate=v_w_gla_gate, b_gla_gate=v_b_gla_gate, gla_norm=v_gla_norm, w_dw=v_w_dw,
              b_dw=v_b_dw, conv_ln_g=v_conv_ln_g, conv_ln_b=v_conv_ln_b, rel_bias=v_rel_bias, w_out=v_w_out, norm_ffn=v_norm_ffn,
              w_up=v_w_up, w_down=v_w_down, norm_final=v_norm_final)
    depth = w_in.shape[0]
    xi, yi, ci = lax.axis_index("x"), lax.axis_index("y"), lax.axis_index("c")
    chip = 2 * xi + yi

    w_dw_pad = jnp.pad(w_dw, ((0, 0), (0, HALO - CK), (0, 0)))
    g_in, g_out, g_up, g_down, g_dw = _gather_chips(
        [w_in.astype(BF16), w_out.astype(BF16), w_up.astype(BF16), w_down.astype(BF16), w_dw_pad])
    layers = []
    for l in range(depth):
        w_in_full = jnp.transpose(g_in[:, l], (1, 0, 2)).reshape(D, -1)
        w_gla, w_conv, w_att = _split_w_in(w_in_full)
        w_out_full = g_out[:, l].reshape(D, D)
        layers.append(dict(
            norm_mix=norm_mix[l][None], w_gla=w_gla, w_conv=w_conv, w_att=w_att,
            wg=jnp.pad(w_gla_gate[l], ((0, LRW - 16), (0, KW - 192))), bg=_pad_cols(b_gla_gate[l][None], KW),
            gla_norm=gla_norm[l][None], w_dw=jnp.transpose(g_dw[:, l], (1, 0, 2)).reshape(HALO, CW), b_dw=b_dw[l][None],
            ln_g=conv_ln_g[l][None], ln_b=conv_ln_b[l][None], rel_bias=rel_bias[l],
            w_out_g=w_out_full[0:VW], w_out_c=w_out_full[VW:VW + CW], w_out_a=w_out_full[VW + CW:],
            norm_ffn=norm_ffn[l][None], w_up=g_up[:, l],
            w_down=g_down[:, l].reshape(D_FF, D)))

    loss_part, grad_x, grads, g_final = _local_step(x[0], loss_target[0], layers, norm_final[None])
    loss = lax.psum(loss_part[0, 0], ("x", "y", "c"))

    per_weight = [
        lambda g: jnp.transpose(_join_w_in(g).reshape(D, 4, -1), (1, 0, 2)),
        lambda g: jnp.concatenate([g["w_out_g"], g["w_out_c"], g["w_out_a"]], axis=0).reshape(4, D // 4, D),
        lambda g: g["w_up"],
        lambda g: g["w_down"].reshape(4, D_FF // 4, D),
    ]
    full = [fn(grads[l])[:, None] for fn in per_weight for l in range(depth)]
    got = _pair_exchange(full)
    pair = [_pair_add(f, r, ci, f"reduce_pair_add_{t}") for t, (f, r) in enumerate(zip(full, got))]
    parts = _chip_scatter(pair)
    halves = [_sum_chips(p, f"reduce_sum_chips_{t}") for t, p in enumerate(parts)]
    gw_in, gw_out, gw_up, gw_down = _pair_allgather([halves[w * depth:(w + 1) * depth] for w in range(len(per_weight))])

    small_g = []
    for l in range(depth):
        g = grads[l]
        small_g += [g["norm_mix"], g["wg"][0:16, 0:192], g["bg"][:, 0:192], g["gla_norm"], g["b_dw"], g["ln_g"], g["ln_b"],
                    g["rel_bias"], g["norm_ffn"], g["w_dw"][0:CK]]
    small_g.append(g_final)
    small_shapes = [a.shape for a in small_g]
    n_small = sum(int(np.prod(s)) for s in small_shapes)
    rows = -(-n_small // 1024) * 8
    red = _unpack(_allreduce_small(_pack(small_g, rows)), small_shapes)
    G = {}
    per = len(SMALL) + 1
    for i, name in enumerate(SMALL):
        G[name] = jnp.stack([red[l * per + i].reshape(P[name].shape[1:]) for l in range(depth)])
    gw_dw_all = jnp.stack([red[l * per + len(SMALL)] for l in range(depth)])
    G["w_dw"] = lax.dynamic_slice_in_dim(gw_dw_all, chip * (CW // 4), CW // 4, axis=2)
    G["norm_final"] = red[-1].reshape(norm_final.shape)
    G["w_in"], G["w_out"], G["w_up"], G["w_down"] = gw_in, gw_out, gw_up, gw_down

    delta, new_m, new_v = {}, {}, {}
    for name in ("w_in", "w_out", "w_up", "w_down"):
        shp = P[name].shape
        two_d = lambda a: a.reshape(-1, shp[-1])
        d, nm, nv = _adamw(two_d(P[name]), two_d(G[name]), two_d(Mo[name]), two_d(Vo[name]), f"adamw_{name}")
        delta[name], new_m[name], new_v[name] = d.reshape(shp), nm.reshape(shp), nv.reshape(shp)
    small_names = list(SMALL) + ["w_dw", "norm_final"]
    two_d = lambda a: a.reshape(-1, a.shape[-1])
    ds, nms, nvs = _adamw_small([two_d(P[k]) for k in small_names], [two_d(G[k]) for k in small_names],
                                [two_d(Mo[k]) for k in small_names], [two_d(Vo[k]) for k in small_names])
    for i, name in enumerate(small_names):
        shp = P[name].shape
        delta[name], new_m[name], new_v[name] = ds[i].reshape(shp), nms[i].reshape(shp), nvs[i].reshape(shp)

    order = ["norm_mix", "w_in", "w_gla_gate", "b_gla_gate", "gla_norm", "w_dw", "b_dw", "conv_ln_g", "conv_ln_b", "rel_bias",
             "w_out", "norm_ffn", "w_up", "w_down", "norm_final"]
    return (loss, grad_x[None], *[G[k] for k in order], *[delta[k] for k in order], *[new_m[k] for k in order],
            *[new_v[k] for k in order])
```

```python
import functools

import numpy as np
import jax
import jax.numpy as jnp
from jax import lax
from jax.experimental import pallas as pl
from jax.experimental.pallas import tpu as pltpu

F32 = jnp.float32
BF16 = jnp.bfloat16
HI = lax.Precision.HIGHEST

D = 1024
CHUNK = 64
GLA_DK, GLA_DV, GLA_H = 48, 96, 4
KW = 256
VW = 384
LRW = 128
GLA_TAU = 16.0
CW = 256
CK = 31
AW = 384
AH = 6
BAND = 576
LEFT = 512
D_FF = 4096
EPS = 1e-6
NEG = -1e30
N_REL = 257

GLA_COLS = 2 * KW + 2 * VW + LRW
CONV_COLS = 2 * CW
ATT_COLS = 3 * AW

ADAM_LR, ADAM_B1, ADAM_B2, ADAM_EPS, ADAM_WD, ADAM_STEP = 0.001, 0.9, 0.999, 1e-08, 0.01, 10

VMEM_CAP = 56 * 1024 * 1024
MESH = pl.DeviceIdType.MESH


def _cparams(sem, vmem_bytes):
    limit = int(min(VMEM_CAP, max(vmem_bytes * 5 // 4 + (4 << 20), 16 << 20)))
    return pltpu.CompilerParams(dimension_semantics=sem, vmem_limit_bytes=limit)


def _out(shape, dtype):
    return pltpu.HBM(tuple(shape), dtype)


class _Comm:
    def __init__(self, ins, outs, aliases, sems, start, finish, then=None):
        self.ins, self.outs, self.aliases, self.sems = list(ins), list(outs), dict(aliases), list(sems)
        self.start, self.finish, self.then = start, finish, then


class _Plan:
    def __init__(self):
        self.by_call = {}

    def at(self, call, comm):
        self.by_call.setdefault(call, []).append(comm)

    def take(self, call):
        return self.by_call.pop(call, [])


_PLAN = [None]


def _pin(a):
    return pltpu.with_memory_space_constraint(a, pltpu.HBM) if jnp.issubdtype(a.dtype, jnp.floating) else a


def _pallas(body, **kw):
    comms = _PLAN[0].take(kw.get("name")) if _PLAN[0] is not None else []
    if not comms:
        call = pl.pallas_call(body, **kw)
        return lambda *args: call(*[_pin(a) for a in args])

    grid = tuple(kw.get("grid", ()))
    single = not isinstance(kw["out_shape"], (tuple, list))
    out_shape = [kw["out_shape"]] if single else list(kw["out_shape"])
    out_specs = [kw["out_specs"]] if single else list(kw["out_specs"])
    in_specs = list(kw["in_specs"])
    scratch = list(kw.get("scratch_shapes", ()))
    n_in, n_out, n_scr = len(in_specs), len(out_shape), len(scratch)
    c_in = sum(len(c.ins) for c in comms)
    c_out = sum(len(c.outs) for c in comms)
    aliases = dict(kw.get("input_output_aliases", {}))
    i0, o0 = n_in, n_out
    for c in comms:
        for i, o in c.aliases.items():
            aliases[i0 + i] = o0 + o
        i0 += len(c.ins)
        o0 += len(c.outs)

    def wrapped(*refs):
        ins, c_ins = refs[:n_in], refs[n_in:n_in + c_in]
        outs, c_outs = refs[n_in + c_in:n_in + c_in + n_out], refs[n_in + c_in + n_out:n_in + c_in + n_out + c_out]
        scr, c_sems = refs[n_in + c_in + n_out + c_out:][:n_scr], refs[n_in + c_in + n_out + c_out + n_scr:]

        def each(what):
            i0 = o0 = s0 = 0
            for c in comms:
                getattr(c, what)(c_ins[i0:i0 + len(c.ins)], c_outs[o0:o0 + len(c.outs)], c_sems[s0:s0 + len(c.sems)])
                i0, o0, s0 = i0 + len(c.ins), o0 + len(c.outs), s0 + len(c.sems)

        if grid:
            first = functools.reduce(jnp.logical_and, [pl.program_id(a) == 0 for a in range(len(grid))])
            last = functools.reduce(jnp.logical_and, [pl.program_id(a) == grid[a] - 1 for a in range(len(grid))])
            pl.when(first)(lambda: each("start"))
            body(*ins, *outs, *scr)
            pl.when(last)(lambda: each("finish"))
        else:
            each("start")
            body(*ins, *outs, *scr)
            each("finish")

    kw = dict(kw)
    kw["in_specs"] = in_specs + [ANY] * c_in
    kw["out_shape"] = out_shape + [_out(s.shape, s.dtype) for c in comms for s in c.outs]
    kw["out_specs"] = out_specs + [ANY] * c_out
    kw["scratch_shapes"] = scratch + [pltpu.SemaphoreType.DMA(s) for c in comms for s in c.sems]
    kw["input_output_aliases"] = aliases
    call = pl.pallas_call(wrapped, **kw)

    def run(*args):
        res = call(*[_pin(a) for a in args], *[_pin(a) for c in comms for a in c.ins])
        o0 = n_out
        for c in comms:
            if c.then is not None:
                c.then(res[o0:o0 + len(c.outs)])
            o0 += len(c.outs)
        return res[0] if single else res[:n_out]

    return run


def _nbytes(shape, dtype):
    return int(np.prod(shape)) * jnp.dtype(dtype).itemsize


def _sigmoid(x):
    return 1.0 / (1.0 + jnp.exp(-x))


_DIMS = {"nn": (((1,), (0,)), ((), ())), "nt": (((1,), (1,)), ((), ())), "tn": (((0,), (0,)), ((), ()))}


def _mm(a, b, *, mode, out_dtype, name, tm=512, tn=None, tk=None, a_pro=None, epi=None, extra=None,
        b_chips=False, out_chips=False):
    b2 = (b.shape[1], 4 * b.shape[2]) if b_chips else b.shape
    if mode == "nn":
        (M, K), (K2, N) = a.shape, b2
    elif mode == "nt":
        (M, K), (N, K2) = a.shape, b2
    else:
        (K, M), (K2, N) = a.shape, b2
    assert K == K2, (a.shape, b.shape, mode)
    tm = min(tm, M)
    tn = N if tn is None else min(tn, N)
    tk = K if tk is None else min(tk, K)
    assert M % tm == 0 and N % tn == 0 and K % tk == 0, (M, N, K, tm, tn, tk)
    nk = K // tk
    a_blk = (tk, tm) if mode == "tn" else (tm, tk)
    a_map = (lambda i, j, k: (k, i)) if mode == "tn" else (lambda i, j, k: (i, k))
    b_blk = (tn, tk) if mode == "nt" else (tk, tn)
    b_map = (lambda i, j, k: (j, k)) if mode == "nt" else (lambda i, j, k: (k, j))
    if b_chips:
        per = b.shape[2] // b_blk[1]
        assert b.shape[2] % b_blk[1] == 0 and mode != "tn"
        flat_map = b_map
        b_map = lambda i, j, k: (flat_map(i, j, k)[1] // per, flat_map(i, j, k)[0], flat_map(i, j, k)[1] % per)
        b_blk = (None,) + b_blk
    in_specs = [pl.BlockSpec(a_blk, a_map), pl.BlockSpec(b_blk, b_map)]
    args = [a, b]
    if epi is not None:
        in_specs.append(pl.BlockSpec((tm, tn), lambda i, j, k: (i, j)))
        args.append(extra)

    def body(*refs):
        a_ref, b_ref = refs[0], refs[1]
        e_ref = refs[2] if epi is not None else None
        o_ref = refs[3] if epi is not None else refs[2]
        av = a_ref[...]
        if a_pro == "relu2":
            af = jnp.maximum(av.astype(F32), 0.0)
            av = af * af
        p = lax.dot_general(av.astype(BF16), b_ref[...].astype(BF16), _DIMS[mode], preferred_element_type=F32)

        def finish(acc):
            if epi == "add":
                acc = acc + e_ref[...].astype(F32)
            elif epi == "relu2grad":
                acc = acc * (2.0 * jnp.maximum(e_ref[...].astype(F32), 0.0))
            o_ref[...] = acc.astype(o_ref.dtype)

        if nk == 1:
            finish(p)
        else:
            acc_ref = refs[-1]
            k = pl.program_id(2)

            @pl.when(k == 0)
            def _():
                acc_ref[...] = p

            @pl.when(k > 0)
            def _():
                acc_ref[...] += p

            @pl.when(k == nk - 1)
            def _():
                finish(acc_ref[...])

    vm = 2 * (_nbytes(a_blk, a.dtype) + _nbytes((tk, tn), b.dtype) + _nbytes((tm, tn), out_dtype))
    vm += 3 * _nbytes((tm, tn), F32)
    if epi is not None:
        vm += 2 * _nbytes((tm, tn), extra.dtype)
    if out_chips:
        per_out = N // 4 // tn
        assert N % (4 * tn) == 0
        out_shape = _out((4, M, N // 4), out_dtype)
        out_spec = pl.BlockSpec((None, tm, tn), lambda i, j, k: (j // per_out, i, j % per_out))
    else:
        out_shape = _out((M, N), out_dtype)
        out_spec = pl.BlockSpec((tm, tn), lambda i, j, k: (i, j))
    return _pallas(
        body,
        out_shape=out_shape,
        grid=(M // tm, N // tn, nk),
        in_specs=in_specs,
        out_specs=out_spec,
        scratch_shapes=[pltpu.VMEM((tm, tn), F32)] if nk > 1 else [],
        compiler_params=_cparams(("parallel", "parallel", "arbitrary"), vm),
        name=name,
    )(*args)


def _rmsnorm_fwd(h, g, name, tm=512):
    T = h.shape[0]

    def body(h_ref, g_ref, o_ref):
        x = h_ref[...]
        r = lax.rsqrt(jnp.mean(x * x, axis=-1, keepdims=True) + EPS)
        o_ref[...] = (x * r * g_ref[...]).astype(o_ref.dtype)

    return _pallas(
        body,
        out_shape=_out((T, D), BF16),
        grid=(T // tm,),
        in_specs=[pl.BlockSpec((tm, D), lambda i: (i, 0)), pl.BlockSpec((1, D), lambda i: (0, 0))],
        out_specs=pl.BlockSpec((tm, D), lambda i: (i, 0)),
        compiler_params=_cparams(("parallel",), 8 * _nbytes((tm, D), F32)),
        name=name,
    )(h, g)


def _rmsnorm_bwd(dxn, h, g, dres, name, tm=512):
    T = h.shape[0]

    def body(dxn_ref, h_ref, g_ref, dres_ref, dh_ref, dg_ref):
        @pl.when(pl.program_id(0) == 0)
        def _():
            dg_ref[...] = jnp.zeros_like(dg_ref)

        x = h_ref[...]
        dy = dxn_ref[...].astype(F32)
        r = lax.rsqrt(jnp.mean(x * x, axis=-1, keepdims=True) + EPS)
        gy = dy * g_ref[...]
        dot = jnp.mean(x * gy, axis=-1, keepdims=True)
        dh_ref[...] = dres_ref[...] + r * gy - x * (r * r * r * dot)
        dg_ref[...] += jnp.sum(dy * x * r, axis=0, keepdims=True)

    row = pl.BlockSpec((tm, D), lambda i: (i, 0))
    vec = pl.BlockSpec((1, D), lambda i: (0, 0))
    return _pallas(
        body,
        out_shape=(_out((T, D), F32), _out((1, D), F32)),
        grid=(T // tm,),
        in_specs=[row, row, vec, row],
        out_specs=(row, vec),
        compiler_params=_cparams(("arbitrary",), 12 * _nbytes((tm, D), F32)),
        name=name,
    )(dxn, h, g, dres)


def _final_loss(h, g, target, name, tm=512):
    T = h.shape[0]

    def body(h_ref, g_ref, t_ref, loss_ref, dh_ref, dg_ref):
        @pl.when(pl.program_id(0) == 0)
        def _():
            dg_ref[...] = jnp.zeros_like(dg_ref)
            loss_ref[...] = jnp.zeros_like(loss_ref)

        x = h_ref[...]
        gg = g_ref[...]
        r = lax.rsqrt(jnp.mean(x * x, axis=-1, keepdims=True) + EPS)
        y = x * r * gg
        e = y - t_ref[...]
        loss_ref[...] += 0.5 * jnp.sum(jnp.mean(e * e, axis=-1, keepdims=True), axis=0, keepdims=True)
        dy = e * (1.0 / D)
        gy = dy * gg
        dot = jnp.mean(x * gy, axis=-1, keepdims=True)
        dh_ref[...] = r * gy - x * (r * r * r * dot)
        dg_ref[...] += jnp.sum(dy * x * r, axis=0, keepdims=True)

    row = pl.BlockSpec((tm, D), lambda i: (i, 0))
    vec = pl.BlockSpec((1, D), lambda i: (0, 0))
    one = pl.BlockSpec((1, 1), lambda i: (0, 0))
    return _pallas(
        body,
        out_shape=(_out((1, 1), F32), _out((T, D), F32), _out((1, D), F32)),
        grid=(T // tm,),
        in_specs=[row, vec, row],
        out_specs=(one, row, vec),
        compiler_params=_cparams(("arbitrary",), 12 * _nbytes((tm, D), F32)),
        name=name,
    )(h, g, target)


GLA_G = 8


def _gla_consts():
    i = np.arange(KW)[:, None]
    j = np.arange(VW)[None, :]
    mask = ((i // GLA_DK) == (j // GLA_DV)) & (i < GLA_H * GLA_DK)
    a = np.arange(VW)
    hm = ((a[:, None] // GLA_DV) == (a[None, :] // GLA_DV)).astype(np.float32)
    c = np.arange(CHUNK)
    low = (c[:, None] >= c[None, :]).astype(np.float32)
    return jnp.asarray(mask.astype(np.float32)), jnp.asarray(hm, BF16), jnp.asarray(low, BF16)


def _split(x):
    hi = x.astype(BF16)
    return hi, (x - hi.astype(F32)).astype(BF16)


def _dot_sel(a, b, dims, split):
    if split == "a":
        hi, lo = _split(a)
        return (lax.dot_general(hi, b, dims, preferred_element_type=F32) + lax.dot_general(lo, b, dims, preferred_element_type=F32))
    hi, lo = _split(b)
    return (lax.dot_general(a, hi, dims, preferred_element_type=F32) + lax.dot_general(a, lo, dims, preferred_element_type=F32))


def _dot3(a, b, dims):
    ah, al = _split(a)
    bh, bl = _split(b)
    return (lax.dot_general(ah, bh, dims, preferred_element_type=F32) + lax.dot_general(al, bh, dims, preferred_element_type=F32)
            + lax.dot_general(ah, bl, dims, preferred_element_type=F32))


def _gla_chunk_common(p_ref, rows, wg, bg, low, ones_v):
    q = p_ref[rows, 0:KW]
    k = p_ref[rows, KW:2 * KW]
    v = p_ref[rows, 2 * KW:2 * KW + VW]
    g = p_ref[rows, 2 * KW + VW:2 * KW + 2 * VW]
    lr = p_ref[rows, 2 * KW + 2 * VW:GLA_COLS]
    z = _dot3(lr, wg, _DIMS["nn"]) + bg
    la = (jnp.minimum(z, 0.0) - jnp.log(1.0 + jnp.exp(-jnp.abs(z)))) * (1.0 / GLA_TAU)
    cum = _dot_sel(low, la, _DIMS["nn"], "b")
    endb = cum[CHUNK - 1:CHUNK, :]
    w = jnp.exp(endb - cum)
    a_full = jnp.exp(_dot_sel(la, ones_v, _DIMS["tn"], "a"))
    return q, k, v, g, lr, z, w, endb, a_full


def _gla_fwd(p, wg, bg, gn, consts, name):
    T = p.shape[0]
    rb = CHUNK * GLA_G
    ng = T // rb
    mask, hm, low = consts[:3]
    scale = GLA_DK ** -0.5

    def body(p_ref, wg_ref, bg_ref, gn_ref, m_ref, hm_ref, l_ref, o_ref, st_ref, s_ref):
        @pl.when(pl.program_id(0) == 0)
        def _():
            s_ref[...] = jnp.zeros_like(s_ref)

        wg_v, bg_v, gn_v = wg_ref[...], bg_ref[...], gn_ref[...]
        ones_v = jnp.ones((CHUNK, VW), BF16)
        for c in range(GLA_G):
            rows = slice(c * CHUNK, (c + 1) * CHUNK)
            q, k, v, g, _, _, w, _, a_full = _gla_chunk_common(p_ref, rows, wg_v, bg_v, l_ref[...], ones_v)
            kd = (k * w).astype(BF16)
            kv = lax.dot_general(kd, v.astype(BF16), _DIMS["tn"], preferred_element_type=F32) * m_ref[...]
            s_new = a_full * s_ref[...] + kv
            s_ref[...] = s_new
            st_ref[c] = s_new
            o = jnp.dot((q * scale).astype(BF16), s_new.astype(BF16), preferred_element_type=F32)
            ms = _dot_sel(o * o, hm_ref[...], _DIMS["nn"], "a") * (1.0 / GLA_DV)
            r = lax.rsqrt(ms + EPS)
            y = o * r * gn_v * (g * _sigmoid(g))
            o_ref[rows, :] = y.astype(o_ref.dtype)

    full = lambda shape: pl.BlockSpec(shape, lambda i: tuple(0 for _ in shape))
    vm = 2 * _nbytes((rb, GLA_COLS), F32) + 2 * _nbytes((GLA_G, KW, VW), F32) + 12 * _nbytes((KW, VW), F32)
    return _pallas(
        body,
        out_shape=(_out((T, VW), BF16), _out((T // CHUNK, KW, VW), F32)),
        grid=(ng,),
        in_specs=[pl.BlockSpec((rb, GLA_COLS), lambda i: (i, 0)), full((LRW, KW)), full((1, KW)), full((1, VW)),
                  full((KW, VW)), full((VW, VW)), full((CHUNK, CHUNK))],
        out_specs=(pl.BlockSpec((rb, VW), lambda i: (i, 0)), pl.BlockSpec((GLA_G, KW, VW), lambda i: (i, 0, 0))),
        scratch_shapes=[pltpu.VMEM((KW, VW), F32)],
        compiler_params=_cparams(("arbitrary",), vm),
        name=name,
    )(p, wg, bg, gn, mask, hm, low)


def _gla_bwd(p, dy, states, wg, bg, gn, consts, name):
    T = p.shape[0]
    rb = CHUNK * GLA_G
    ng = T // rb
    mask, hm, low = consts[:3]
    scale = GLA_DK ** -0.5

    def body(p_ref, dy_ref, st_ref, sp_ref, wg_ref, bg_ref, gn_ref, m_ref, hm_ref, l_ref,
             dp_ref, dwg_ref, dbg_ref, dgn_ref, ga_ref):
        step = pl.program_id(0)

        @pl.when(step == 0)
        def _():
            ga_ref[...] = jnp.zeros_like(ga_ref)
            dwg_ref[...] = jnp.zeros_like(dwg_ref)
            dbg_ref[...] = jnp.zeros_like(dbg_ref)
            dgn_ref[...] = jnp.zeros_like(dgn_ref)

        first_group = step == ng - 1
        wg_v, bg_v, gn_v = wg_ref[...], bg_ref[...], gn_ref[...]
        ones_v = jnp.ones((CHUNK, VW), BF16)
        ones_8 = jnp.ones((8, VW), BF16)
        for c in reversed(range(GLA_G)):
            rows = slice(c * CHUNK, (c + 1) * CHUNK)
            q, k, v, g, lr, z, w, endb, a_full = _gla_chunk_common(p_ref, rows, wg_v, bg_v, l_ref[...], ones_v)
            s_n = st_ref[c]
            if c > 0:
                s_prev = st_ref[c - 1]
            else:
                s_prev = jnp.where(first_group, 0.0, sp_ref[0])
            kd = k * w
            qs = (q * scale).astype(BF16)
            o = jnp.dot(qs, s_n.astype(BF16), preferred_element_type=F32)
            ms = _dot_sel(o * o, hm_ref[...], _DIMS["nn"], "a") * (1.0 / GLA_DV)
            r = lax.rsqrt(ms + EPS)
            on = o * r
            sg = _sigmoid(g)
            silu = g * sg
            dyv = dy_ref[rows, :].astype(F32)
            d_on = dyv * gn_v * silu
            dg = dyv * on * gn_v * (sg * (1.0 + g * (1.0 - sg)))
            dgn_ref[...] += jnp.sum(dyv * on * silu, axis=0, keepdims=True)
            mo = _dot_sel(o * d_on, hm_ref[...], _DIMS["nn"], "a") * (1.0 / GLA_DV)
            do = r * d_on - o * (r * r * r) * mo
            dob = do.astype(BF16)
            dq = lax.dot_general(dob, s_n.astype(BF16), _DIMS["nt"], preferred_element_type=F32) * scale
            g_n = lax.dot_general(qs, dob, _DIMS["tn"], preferred_element_type=F32) * m_ref[...] + ga_ref[...]
            d_a = _dot_sel(ones_8, g_n * s_prev, _DIMS["nt"], "b")[0:1, :]
            g_nb = g_n.astype(BF16)
            dkd = lax.dot_general(v.astype(BF16), g_nb, _DIMS["nt"], preferred_element_type=F32)
            dv = jnp.dot(kd.astype(BF16), g_nb, preferred_element_type=F32)
            dk = dkd * w
            e = dkd * kd
            d_end = jnp.sum(e, axis=0, keepdims=True) + d_a * jnp.exp(endb)
            dla = _dot_sel(l_ref[...], -e, _DIMS["tn"], "b") + d_end
            dz = dla * (1.0 - _sigmoid(z)) * (1.0 / GLA_TAU)
            dlr = _dot3(dz, wg_v, _DIMS["nt"])
            dwg_ref[...] += _dot3(lr, dz, _DIMS["tn"])
            dbg_ref[...] += jnp.sum(dz, axis=0, keepdims=True)
            ga_ref[...] = a_full * g_n
            dp_ref[rows, 0:KW] = dq.astype(dp_ref.dtype)
            dp_ref[rows, KW:2 * KW] = dk.astype(dp_ref.dtype)
            dp_ref[rows, 2 * KW:2 * KW + VW] = dv.astype(dp_ref.dtype)
            dp_ref[rows, 2 * KW + VW:2 * KW + 2 * VW] = dg.astype(dp_ref.dtype)
            dp_ref[rows, 2 * KW + 2 * VW:GLA_COLS] = dlr.astype(dp_ref.dtype)

    full = lambda shape: pl.BlockSpec(shape, lambda i: tuple(0 for _ in shape))
    rev = lambda i: (ng - 1 - i, 0)
    vm = 4 * _nbytes((rb, GLA_COLS), F32) + 2 * _nbytes((rb, VW), F32) + 2 * _nbytes((GLA_G + 1, KW, VW), F32)
    vm += 16 * _nbytes((KW, VW), F32)
    return _pallas(
        body,
        out_shape=(_out((T, GLA_COLS), BF16), _out((LRW, KW), F32),
                   _out((1, KW), F32), _out((1, VW), F32)),
        grid=(ng,),
        in_specs=[pl.BlockSpec((rb, GLA_COLS), rev), pl.BlockSpec((rb, VW), rev),
                  pl.BlockSpec((GLA_G, KW, VW), lambda i: (ng - 1 - i, 0, 0)),
                  pl.BlockSpec((1, KW, VW), lambda i: (jnp.maximum((ng - 1 - i) * GLA_G - 1, 0), 0, 0)),
                  full((LRW, KW)), full((1, KW)), full((1, VW)), full((KW, VW)), full((VW, VW)), full((CHUNK, CHUNK))],
        out_specs=(pl.BlockSpec((rb, GLA_COLS), rev), full((LRW, KW)), full((1, KW)), full((1, VW))),
        scratch_shapes=[pltpu.VMEM((KW, VW), F32)],
        compiler_params=_cparams(("arbitrary",), vm),
        name=name,
    )(p, dy, states, states, wg, bg, gn, mask, hm, low)


CONV_TM = 512
HALO = 32
CONV_RB = 64


def _glu(u):
    a = u[:, 0:CW]
    b = u[:, CW:2 * CW]
    return a * _sigmoid(b)


def _conv_taps(buf_ref, w_ref, rb0, first_tap):
    acc = jnp.zeros((CONV_RB, CW), F32)
    for j in range(CK):
        s = rb0 + first_tap(j)
        acc = acc + w_ref[j:j + 1, :] * buf_ref[s:s + CONV_RB, :]
    return acc


def _ln_fwd(c, lg, lb):
    mu = jnp.mean(c, axis=-1, keepdims=True)
    xc = c - mu
    rstd = lax.rsqrt(jnp.mean(xc * xc, axis=-1, keepdims=True) + EPS)
    n = xc * rstd
    return n, rstd, n * lg + lb


def _conv_fwd(u, w, b, lg, lb, name):
    T = u.shape[0]
    tm = CONV_TM

    def body(u_ref, uh_ref, w_ref, b_ref, lg_ref, lb_ref, o_ref, hbuf):
        i = pl.program_id(0)
        hbuf[0:HALO, :] = jnp.where(i > 0, _glu(uh_ref[...]), 0.0)
        hbuf[HALO:HALO + tm, :] = _glu(u_ref[...])
        for r in range(tm // CONV_RB):
            acc = _conv_taps(hbuf, w_ref, r * CONV_RB, lambda j: HALO - (CK - 1) + j)
            _, _, zz = _ln_fwd(acc + b_ref[...], lg_ref[...], lb_ref[...])
            o_ref[r * CONV_RB:(r + 1) * CONV_RB, :] = (zz * _sigmoid(zz)).astype(o_ref.dtype)

    vec = pl.BlockSpec((1, CW), lambda i: (0, 0))
    return _pallas(
        body,
        out_shape=_out((T, CW), BF16),
        grid=(T // tm,),
        in_specs=[pl.BlockSpec((tm, CONV_COLS), lambda i: (i, 0)),
                  pl.BlockSpec((HALO, CONV_COLS), lambda i: (jnp.maximum(i * (tm // HALO) - 1, 0), 0)),
                  pl.BlockSpec((HALO, CW), lambda i: (0, 0)), vec, vec, vec],
        out_specs=pl.BlockSpec((tm, CW), lambda i: (i, 0)),
        scratch_shapes=[pltpu.VMEM((tm + HALO, CW), F32)],
        compiler_params=_cparams(("arbitrary",), 8 * _nbytes((tm, CONV_COLS), F32)),
        name=name,
    )(u, u, w, b, lg, lb)


def _conv_bwd_dc(u, dout, w, b, lg, lb, name):
    T = u.shape[0]
    tm = CONV_TM
    nsteps = T // tm

    def body(u_ref, uh_ref, do_ref, w_ref, b_ref, lg_ref, lb_ref, dc_ref, dw_ref, db_ref, dlg_ref, dlb_ref, hbuf, dwacc):
        i = pl.program_id(0)

        @pl.when(i == 0)
        def _():
            dwacc[...] = jnp.zeros_like(dwacc)
            db_ref[...] = jnp.zeros_like(db_ref)
            dlg_ref[...] = jnp.zeros_like(dlg_ref)
            dlb_ref[...] = jnp.zeros_like(dlb_ref)

        hbuf[0:HALO, :] = jnp.where(i > 0, _glu(uh_ref[...]), 0.0)
        hbuf[HALO:HALO + tm, :] = _glu(u_ref[...])
        for r in range(tm // CONV_RB):
            rows = slice(r * CONV_RB, (r + 1) * CONV_RB)
            acc = _conv_taps(hbuf, w_ref, r * CONV_RB, lambda j: HALO - (CK - 1) + j)
            n, rstd, zz = _ln_fwd(acc + b_ref[...], lg_ref[...], lb_ref[...])
            sg = _sigmoid(zz)
            dz = do_ref[rows, :].astype(F32) * (sg * (1.0 + zz * (1.0 - sg)))
            dlg_ref[...] += jnp.sum(dz * n, axis=0, keepdims=True)
            dlb_ref[...] += jnp.sum(dz, axis=0, keepdims=True)
            dn = dz * lg_ref[...]
            dc = rstd * (dn - jnp.mean(dn, axis=-1, keepdims=True) - n * jnp.mean(dn * n, axis=-1, keepdims=True))
            dc_ref[rows, :] = dc
            db_ref[...] += jnp.sum(dc, axis=0, keepdims=True)
            for j in range(CK):
                s = r * CONV_RB + HALO - (CK - 1) + j
                prod = dc * hbuf[s:s + CONV_RB, :]
                dwacc[j] += jnp.sum(prod.reshape(CONV_RB // 8, 8, CW), axis=0)

        @pl.when(i == nsteps - 1)
        def _():
            dw_ref[...] = jnp.sum(dwacc[...], axis=1)

    vec = pl.BlockSpec((1, CW), lambda i: (0, 0))
    return _pallas(
        body,
        out_shape=(_out((T, CW), F32), _out((HALO, CW), F32),
                   _out((1, CW), F32), _out((1, CW), F32), _out((1, CW), F32)),
        grid=(nsteps,),
        in_specs=[pl.BlockSpec((tm, CONV_COLS), lambda i: (i, 0)),
                  pl.BlockSpec((HALO, CONV_COLS), lambda i: (jnp.maximum(i * (tm // HALO) - 1, 0), 0)),
                  pl.BlockSpec((tm, CW), lambda i: (i, 0)),
                  pl.BlockSpec((HALO, CW), lambda i: (0, 0)), vec, vec, vec],
        out_specs=(pl.BlockSpec((tm, CW), lambda i: (i, 0)), pl.BlockSpec((HALO, CW), lambda i: (0, 0)), vec, vec, vec),
        scratch_shapes=[pltpu.VMEM((tm + HALO, CW), F32), pltpu.VMEM((HALO, 8, CW), F32)],
        compiler_params=_cparams(("arbitrary",), 10 * _nbytes((tm, CONV_COLS), F32)),
        name=name,
    )(u, u, dout, w, b, lg, lb)


def _conv_bwd_du(u, dc, w, name):
    T = u.shape[0]
    tm = CONV_TM
    nsteps = T // tm

    def body(u_ref, dc_ref, dch_ref, w_ref, du_ref, dcbuf):
        i = pl.program_id(0)
        dcbuf[0:tm, :] = dc_ref[...]
        dcbuf[tm:tm + HALO, :] = jnp.where(i < nsteps - 1, dch_ref[...], 0.0)
        for r in range(tm // CONV_RB):
            rows = slice(r * CONV_RB, (r + 1) * CONV_RB)
            dh = _conv_taps(dcbuf, w_ref, r * CONV_RB, lambda j: (CK - 1) - j)
            a = u_ref[rows, 0:CW]
            sb = _sigmoid(u_ref[rows, CW:2 * CW])
            du_ref[rows, 0:CW] = (dh * sb).astype(du_ref.dtype)
            du_ref[rows, CW:2 * CW] = (dh * a * sb * (1.0 - sb)).astype(du_ref.dtype)

    return _pallas(
        body,
        out_shape=_out((T, CONV_COLS), BF16),
        grid=(nsteps,),
        in_specs=[pl.BlockSpec((tm, CONV_COLS), lambda i: (i, 0)),
                  pl.BlockSpec((tm, CW), lambda i: (i, 0)),
                  pl.BlockSpec((HALO, CW), lambda i: (jnp.minimum((i + 1) * (tm // HALO), T // HALO - 1), 0)),
                  pl.BlockSpec((HALO, CW), lambda i: (0, 0))],
        out_specs=pl.BlockSpec((tm, CONV_COLS), lambda i: (i, 0)),
        scratch_shapes=[pltpu.VMEM((tm + HALO, CW), F32)],
        compiler_params=_cparams(("arbitrary",), 8 * _nbytes((tm, CONV_COLS), F32)),
        name=name,
    )(u, dc, dc, w)


ATT_G = 4


def _att_load_kv(p_any, kbuf, vbuf, sems, T):
    kc = pltpu.make_async_copy(p_any.at[:, pl.ds(AW, AW)], kbuf.at[pl.ds(LEFT, T), :], sems.at[0])
    vc = pltpu.make_async_copy(p_any.at[:, pl.ds(2 * AW, AW)], vbuf.at[pl.ds(LEFT, T), :], sems.at[1])
    kc.start()
    vc.start()
    kbuf[0:LEFT, :] = jnp.zeros((LEFT, AW), BF16)
    vbuf[0:LEFT, :] = jnp.zeros((LEFT, AW), BF16)
    kc.wait()
    vc.wait()


ATT_QB = CHUNK * ATT_G
ATT_KB = LEFT + ATT_QB
REL_PAD = 384
TOEP = 1024


def _att_consts():
    m = np.arange(TOEP)
    d = ATT_KB - 1 - m
    idx = np.clip(d, -128, 128) + 128
    sel = (np.arange(REL_PAD)[:, None] == idx[None, :]) & (m[None, :] < ATT_QB + ATT_KB - 1)
    return jnp.asarray(sel.astype(np.float32))


def _att_build_bias(rel_ref, sel_ref, bias_scr):
    tr = jnp.dot(rel_ref[...], sel_ref[...], precision=HI, preferred_element_type=F32)
    qc = lax.broadcasted_iota(jnp.int32, (ATT_QB, ATT_KB), 0) // CHUNK
    kc = lax.broadcasted_iota(jnp.int32, (ATT_QB, ATT_KB), 1) // CHUNK
    band = (kc >= qc) & (kc <= qc + 8)
    for h in range(AH):
        rows = jnp.broadcast_to(tr[h:h + 1, :], (ATT_QB, TOEP))
        toep = pltpu.roll(rows, TOEP - (ATT_QB - 1), 1, stride=1, stride_axis=0)[:, 0:ATT_KB]
        bias_scr[h // 2, (h % 2) * ATT_QB:(h % 2 + 1) * ATT_QB, :] = jnp.where(band, toep, NEG)


def _att_probs(qst, kb, bias_p, n0):
    sc = lax.dot_general(qst, kb, _DIMS["nt"], preferred_element_type=F32) * (64 ** -0.5) + bias_p
    pos = lax.broadcasted_iota(jnp.int32, (2 * ATT_QB, ATT_KB), 1)
    sc = jnp.where(pos >= CHUNK * (8 - n0), sc, NEG)
    mx = jnp.max(sc, axis=-1, keepdims=True)
    ex = jnp.exp(sc - mx)
    return ex / jnp.sum(ex, axis=-1, keepdims=True)


def _head_stack(a2, lo):
    zero = jnp.zeros_like(a2)
    return jnp.concatenate([jnp.where(lo, a2, zero), jnp.where(lo, zero, a2)], axis=0)


def _att_fwd(p, rel, sel, name):
    T = p.shape[0]

    def body(q_ref, p_any, rel_ref, sel_ref, o_ref, kbuf, vbuf, bias_scr, sems):
        i = pl.program_id(0)

        @pl.when(i == 0)
        def _():
            _att_load_kv(p_any, kbuf, vbuf, sems, T)
            _att_build_bias(rel_ref, sel_ref, bias_scr)

        lo = lax.broadcasted_iota(jnp.int32, (ATT_QB, 128), 1) < 64
        n0 = i * ATT_G
        start = pl.multiple_of(i * ATT_QB, ATT_QB)
        for hp in range(AH // 2):
            cols = slice(hp * 128, (hp + 1) * 128)
            kb = kbuf[pl.ds(start, ATT_KB), cols]
            vb = vbuf[pl.ds(start, ATT_KB), cols]
            pr = _att_probs(_head_stack(q_ref[:, cols], lo), kb, bias_scr[hp], n0)
            pv = jnp.dot(pr.astype(BF16), vb, preferred_element_type=F32)
            o_ref[:, cols] = jnp.where(lo, pv[0:ATT_QB], pv[ATT_QB:2 * ATT_QB]).astype(o_ref.dtype)

    vm = 2 * _nbytes((T + LEFT, AW), BF16) + 8 * _nbytes((2 * ATT_QB, ATT_KB), F32) + (8 << 20)
    return _pallas(
        body,
        out_shape=_out((T, AW), BF16),
        grid=(T // ATT_QB,),
        in_specs=[pl.BlockSpec((ATT_QB, AW), lambda i: (i, 0)), pl.BlockSpec(memory_space=pl.ANY),
                  pl.BlockSpec((8, REL_PAD), lambda i: (0, 0)), pl.BlockSpec((REL_PAD, TOEP), lambda i: (0, 0))],
        out_specs=pl.BlockSpec((ATT_QB, AW), lambda i: (i, 0)),
        scratch_shapes=[pltpu.VMEM((T + LEFT, AW), BF16), pltpu.VMEM((T + LEFT, AW), BF16),
                        pltpu.VMEM((AH // 2, 2 * ATT_QB, ATT_KB), F32), pltpu.SemaphoreType.DMA((2,))],
        compiler_params=_cparams(("arbitrary",), vm),
        name=name,
    )(p, p, rel, sel)


def _att_bwd(p, do, rel, sel, name):
    T = p.shape[0]
    nsteps = T // ATT_QB

    def body(q_ref, p_any, do_ref, rel_ref, sel_ref, dp_any, drel_ref,
             kbuf, vbuf, dqbuf, dkbuf, dvbuf, bias_scr, dbias_scr, dtr_scr, sems):
        i = pl.program_id(0)

        @pl.when(i == 0)
        def _():
            _att_load_kv(p_any, kbuf, vbuf, sems, T)
            _att_build_bias(rel_ref, sel_ref, bias_scr)
            dkbuf[...] = jnp.zeros_like(dkbuf)
            dvbuf[...] = jnp.zeros_like(dvbuf)
            dbias_scr[...] = jnp.zeros_like(dbias_scr)

        lo = lax.broadcasted_iota(jnp.int32, (ATT_QB, 128), 1) < 64
        n0 = i * ATT_G
        start = pl.multiple_of(i * ATT_QB, ATT_QB)
        for hp in range(AH // 2):
            cols = slice(hp * 128, (hp + 1) * 128)
            kb = kbuf[pl.ds(start, ATT_KB), cols]
            vb = vbuf[pl.ds(start, ATT_KB), cols]
            qst = _head_stack(q_ref[:, cols], lo)
            dost = _head_stack(do_ref[:, cols].astype(BF16), lo)
            pr = _att_probs(qst, kb, bias_scr[hp], n0)
            dpr = lax.dot_general(dost, vb, _DIMS["nt"], preferred_element_type=F32)
            ds = pr * (dpr - jnp.sum(dpr * pr, axis=-1, keepdims=True))
            dbias_scr[hp] += ds
            dsb = (ds * (64 ** -0.5)).astype(BF16)
            dq = jnp.dot(dsb, kb, preferred_element_type=F32)
            dqbuf[pl.ds(start, ATT_QB), cols] = jnp.where(lo, dq[0:ATT_QB], dq[ATT_QB:2 * ATT_QB]).astype(BF16)
            dkbuf[pl.ds(start, ATT_KB), cols] += lax.dot_general(dsb, qst, _DIMS["tn"], preferred_element_type=F32)
            dvbuf[pl.ds(start, ATT_KB), cols] += lax.dot_general(pr.astype(BF16), dost, _DIMS["tn"], preferred_element_type=F32)

        @pl.when(i == nsteps - 1)
        def _():
            kbuf[pl.ds(LEFT, T), :] = dkbuf[pl.ds(LEFT, T), :].astype(BF16)
            vbuf[pl.ds(LEFT, T), :] = dvbuf[pl.ds(LEFT, T), :].astype(BF16)
            cps = [pltpu.make_async_copy(dqbuf, dp_any.at[:, pl.ds(0, AW)], sems.at[0]),
                   pltpu.make_async_copy(kbuf.at[pl.ds(LEFT, T), :], dp_any.at[:, pl.ds(AW, AW)], sems.at[1]),
                   pltpu.make_async_copy(vbuf.at[pl.ds(LEFT, T), :], dp_any.at[:, pl.ds(2 * AW, AW)], sems.at[2])]
            for cp in cps:
                cp.start()
            dtr_scr[...] = jnp.zeros_like(dtr_scr)
            ri = lax.broadcasted_iota(jnp.int32, (ATT_QB, ATT_QB), 0)
            ci = lax.broadcasted_iota(jnp.int32, (ATT_QB, ATT_QB), 1)
            flip = jnp.where(ri + ci == ATT_QB - 1, 1.0, 0.0)
            for h in range(AH):
                db = dbias_scr[h // 2, (h % 2) * ATT_QB:(h % 2 + 1) * ATT_QB, :]
                db = jnp.dot(flip, db, precision=HI, preferred_element_type=F32)
                wide = jnp.concatenate([db, jnp.zeros((ATT_QB, TOEP - ATT_KB), F32)], axis=1)
                diag = pltpu.roll(wide, 0, 1, stride=1, stride_axis=0)
                dtr_scr[h:h + 1, :] = jnp.sum(diag, axis=0, keepdims=True)
            drel_ref[...] = lax.dot_general(dtr_scr[...], sel_ref[...], _DIMS["nt"], precision=HI, preferred_element_type=F32)
            for cp in cps:
                cp.wait()

    vm = 3 * _nbytes((T + LEFT, AW), BF16) + 2 * _nbytes((T + LEFT, AW), F32) + 12 * _nbytes((2 * ATT_QB, ATT_KB), F32) + (8 << 20)
    return _pallas(
        body,
        out_shape=(_out((T, ATT_COLS), BF16), _out((8, REL_PAD), F32)),
        grid=(nsteps,),
        in_specs=[pl.BlockSpec((ATT_QB, AW), lambda i: (i, 0)), pl.BlockSpec(memory_space=pl.ANY),
                  pl.BlockSpec((ATT_QB, AW), lambda i: (i, 0)),
                  pl.BlockSpec((8, REL_PAD), lambda i: (0, 0)), pl.BlockSpec((REL_PAD, TOEP), lambda i: (0, 0))],
        out_specs=(pl.BlockSpec(memory_space=pl.ANY), pl.BlockSpec((8, REL_PAD), lambda i: (0, 0))),
        scratch_shapes=[pltpu.VMEM((T + LEFT, AW), BF16), pltpu.VMEM((T + LEFT, AW), BF16), pltpu.VMEM((T, AW), BF16),
                        pltpu.VMEM((T + LEFT, AW), F32), pltpu.VMEM((T + LEFT, AW), F32),
                        pltpu.VMEM((AH // 2, 2 * ATT_QB, ATT_KB), F32), pltpu.VMEM((AH // 2, 2 * ATT_QB, ATT_KB), F32),
                        pltpu.VMEM((8, TOEP), F32), pltpu.SemaphoreType.DMA((3,))],
        compiler_params=_cparams(("arbitrary",), vm),
        name=name,
    )(p, p, do, rel, sel)


def _layer_fwd(h, wl, consts, tag):
    xn = _rmsnorm_fwd(h, wl["norm_mix"], f"{tag}_norm_mix")
    p_gla = _mm(xn, wl["w_gla"], mode="nn", out_dtype=F32, name=f"{tag}_proj_gla")
    p_conv = _mm(xn, wl["w_conv"], mode="nn", out_dtype=F32, name=f"{tag}_proj_conv")
    p_att = _mm(xn, wl["w_att"], mode="nn", out_dtype=BF16, name=f"{tag}_proj_att")
    o_gla, states = _gla_fwd(p_gla, wl["wg"], wl["bg"], wl["gla_norm"], consts, f"{tag}_gla_fwd")
    o_conv = _conv_fwd(p_conv, wl["w_dw"], wl["b_dw"], wl["ln_g"], wl["ln_b"], f"{tag}_conv_fwd")
    rel = jnp.pad(wl["rel_bias"], ((0, 8 - AH), (0, REL_PAD - N_REL)))
    o_att = _att_fwd(p_att, rel, consts[3], f"{tag}_att_fwd")
    h1 = _mm(o_gla, wl["w_out_g"], mode="nn", out_dtype=F32, name=f"{tag}_out_gla", epi="add", extra=h)
    h1 = _mm(o_conv, wl["w_out_c"], mode="nn", out_dtype=F32, name=f"{tag}_out_conv", epi="add", extra=h1)
    h1 = _mm(o_att, wl["w_out_a"], mode="nn", out_dtype=F32, name=f"{tag}_out_att", epi="add", extra=h1)
    xn2 = _rmsnorm_fwd(h1, wl["norm_ffn"], f"{tag}_norm_ffn")
    u = _mm(xn2, wl["w_up"], mode="nn", out_dtype=BF16, tn=1024, b_chips=True, name=f"{tag}_mlp_up")
    h2 = _mm(u, wl["w_down"], mode="nn", out_dtype=F32, tk=1024, a_pro="relu2", epi="add", extra=h1, name=f"{tag}_mlp_down")
    saved = dict(h=h, xn=xn, p_gla=p_gla, p_conv=p_conv, p_att=p_att, states=states, o_gla=o_gla, o_conv=o_conv,
                 o_att=o_att, rel=rel, h1=h1, xn2=xn2, u=u)
    return h2, saved


def _layer_bwd(dh2, sv, wl, consts, tag, emit=lambda name, grad: None):
    g = {}
    du = _mm(dh2, wl["w_down"], mode="nt", out_dtype=BF16, tn=1024, epi="relu2grad", extra=sv["u"], name=f"{tag}_mlp_down_dx")
    g["w_down"] = _mm(sv["u"], dh2, mode="tn", out_dtype=F32, tm=1024, tn=512, tk=512, a_pro="relu2", name=f"{tag}_mlp_down_dw")
    emit("w_down", g["w_down"].reshape(4, D_FF // 4, D))
    dxn2 = _mm(du, wl["w_up"], mode="nt", out_dtype=F32, tk=1024, b_chips=True, name=f"{tag}_mlp_up_dx")
    g["w_up"] = _mm(sv["xn2"], du, mode="tn", out_dtype=F32, tm=1024, tn=512, tk=512, out_chips=True, name=f"{tag}_mlp_up_dw")
    emit("w_up", g["w_up"])
    dh1, g["norm_ffn"] = _rmsnorm_bwd(dxn2, sv["h1"], wl["norm_ffn"], dh2, f"{tag}_norm_ffn_bwd")
    d_gla = _mm(dh1, wl["w_out_g"], mode="nt", out_dtype=F32, name=f"{tag}_out_gla_dx")
    d_conv = _mm(dh1, wl["w_out_c"], mode="nt", out_dtype=F32, name=f"{tag}_out_conv_dx")
    d_att = _mm(dh1, wl["w_out_a"], mode="nt", out_dtype=F32, name=f"{tag}_out_att_dx")
    g["w_out_g"] = _mm(sv["o_gla"], dh1, mode="tn", out_dtype=F32, tm=VW, tn=512, tk=512, name=f"{tag}_out_gla_dw")
    g["w_out_c"] = _mm(sv["o_conv"], dh1, mode="tn", out_dtype=F32, tm=CW, tn=512, tk=512, name=f"{tag}_out_conv_dw")
    g["w_out_a"] = _mm(sv["o_att"], dh1, mode="tn", out_dtype=F32, tm=AW, tn=512, tk=512, name=f"{tag}_out_att_dw")
    emit("w_out", jnp.concatenate([g["w_out_g"], g["w_out_c"], g["w_out_a"]], axis=0).reshape(4, D // 4, D))
    dp_gla, g["wg"], g["bg"], g["gla_norm"] = _gla_bwd(sv["p_gla"], d_gla, sv["states"], wl["wg"], wl["bg"], wl["gla_norm"],
                                                       consts, f"{tag}_gla_bwd")
    dc, g["w_dw"], g["b_dw"], g["ln_g"], g["ln_b"] = _conv_bwd_dc(sv["p_conv"], d_conv, wl["w_dw"], wl["b_dw"], wl["ln_g"],
                                                                  wl["ln_b"], f"{tag}_conv_bwd_dc")
    dp_conv = _conv_bwd_du(sv["p_conv"], dc, wl["w_dw"], f"{tag}_conv_bwd_du")
    dp_att, drel = _att_bwd(sv["p_att"], d_att, sv["rel"], consts[3], f"{tag}_att_bwd")
    g["rel_bias"] = drel[0:AH, 0:N_REL]
    g["w_gla"] = _mm(sv["xn"], dp_gla, mode="tn", out_dtype=F32, tm=512, tk=512, name=f"{tag}_proj_gla_dw")
    g["w_conv"] = _mm(sv["xn"], dp_conv, mode="tn", out_dtype=F32, tm=1024, tn=512, tk=512, name=f"{tag}_proj_conv_dw")
    g["w_att"] = _mm(sv["xn"], dp_att, mode="tn", out_dtype=F32, tm=512, tk=512, name=f"{tag}_proj_att_dw")
    emit("w_in", jnp.transpose(_join_w_in(g).reshape(D, 4, -1), (1, 0, 2)))
    dxn =_mm(dp_gla, wl["w_gla"], mode="nt", out_dtype=F32, name=f"{tag}_proj_gla_dx")
    dxn = _mm(dp_conv, wl["w_conv"], mode="nt", out_dtype=F32, epi="add", extra=dxn, name=f"{tag}_proj_conv_dx")
    dxn = _mm(dp_att, wl["w_att"], mode="nt", out_dtype=F32, epi="add", extra=dxn, name=f"{tag}_proj_att_dx")
    dh, g["norm_mix"] = _rmsnorm_bwd(dxn, sv["h"], wl["norm_mix"], dh1, f"{tag}_norm_mix_bwd")
    return dh, g


def _local_step(x, target, layers, norm_final, emit=lambda layer, name, grad: None):
    consts = _gla_consts() + (_att_consts(),)
    h = x
    saved = []
    for l, wl in enumerate(layers):
        h, sv = _layer_fwd(h, wl, consts, f"l{l}")
        saved.append(sv)
    loss, dh, g_final = _final_loss(h, norm_final, target, "final_loss")
    grads = [None] * len(layers)
    for l in reversed(range(len(layers))):
        dh, grads[l] = _layer_bwd(dh, saved[l], layers[l], consts, f"l{l}", functools.partial(emit, l))
    return loss, dh, grads, g_final


ANY = pl.BlockSpec(memory_space=pl.ANY)


def _place():
    x, y, c = lax.axis_index("x"), lax.axis_index("y"), lax.axis_index("c")
    chips = [(1 - x, y), (x, 1 - y), (1 - x, 1 - y)]
    return x, y, c, chips


def _shape(shape, dtype):
    return jax.ShapeDtypeStruct(tuple(shape), dtype)


def _remote(src, dst, send_sem, recv_sem, to):
    return pltpu.make_async_remote_copy(src_ref=src, dst_ref=dst, send_sem=send_sem, recv_sem=recv_sem,
                                        device_id=to, device_id_type=MESH)


def _phase_gather_ici(src, then):
    R, C = src.shape
    rh = R // 2

    def copies(ins, outs, sems):
        x, y, c, chips = _place()
        me = 2 * x + y
        local = pltpu.make_async_copy(ins[0], outs[0].at[me], sems[2].at[0])
        sends = [_remote(ins[0].at[pl.ds(c * rh, rh), :], outs[0].at[me, pl.ds(c * rh, rh), :], sems[0].at[k], sems[1].at[k],
                         (px, py, c)) for k, (px, py) in enumerate(chips)]
        recvs = [_remote(outs[0].at[2 * px + py, pl.ds(c * rh, rh), :], outs[0].at[2 * px + py, pl.ds(c * rh, rh), :],
                         sems[0].at[k], sems[1].at[k], (px, py, c)) for k, (px, py) in enumerate(chips)]
        return local, sends, recvs

    def start(ins, outs, sems):
        local, sends, _ = copies(ins, outs, sems)
        local.start()
        for cp in sends:
            cp.start()

    def finish(ins, outs, sems):
        local, sends, recvs = copies(ins, outs, sems)
        for cp in recvs:
            cp.wait_recv()
        for cp in sends:
            cp.wait_send()
        local.wait()

    return _Comm([src], [_shape((4, R, C), src.dtype)], {}, [(3,), (3,), (1,)], start, finish, then)


def _phase_gather_d2d(part, then):
    _, R, C = part.shape
    rh = R // 2

    def copies(ins, outs, sems):
        x, y, c, chips = _place()
        sends = [_remote(ins[0].at[2 * px + py, pl.ds(c * rh, rh), :], outs[0].at[2 * px + py, pl.ds(c * rh, rh), :],
                         sems[0].at[k], sems[1].at[k], (x, y, 1 - c)) for k, (px, py) in enumerate(chips)]
        recvs = [_remote(outs[0].at[2 * px + py, pl.ds((1 - c) * rh, rh), :], outs[0].at[2 * px + py, pl.ds((1 - c) * rh, rh), :],
                         sems[0].at[k], sems[1].at[k], (x, y, 1 - c)) for k, (px, py) in enumerate(chips)]
        return sends, recvs

    def start(ins, outs, sems):
        for cp in copies(ins, outs, sems)[0]:
            cp.start()

    def finish(ins, outs, sems):
        sends, recvs = copies(ins, outs, sems)
        for cp in recvs:
            cp.wait_recv()
        for cp in sends:
            cp.wait_send()

    return _Comm([part], [_shape(part.shape, part.dtype)], {0: 0}, [(3,), (3,)], start, finish, then)


def _phase_pair_exchange(full, then):
    _, R, C = full.shape
    rh = R // 2

    def copy(ins, outs, sems):
        x, y, c, _ = _place()
        return _remote(ins[0].at[:, pl.ds((1 - c) * rh, rh), :], outs[0], sems[0].at[0], sems[1].at[0], (x, y, 1 - c))

    return _Comm([full], [_shape((4, rh, C), full.dtype)], {}, [(1,), (1,)],
                 lambda ins, outs, sems: copy(ins, outs, sems).start(),
                 lambda ins, outs, sems: copy(ins, outs, sems).wait(), then)


def _phase_chip_scatter(parts, then):
    def copies(ins, outs, sems):
        x, y, c, chips = _place()
        me = 2 * x + y
        local = pltpu.make_async_copy(ins[0].at[me], outs[0].at[me], sems[2].at[0])
        sends = [_remote(ins[0].at[2 * px + py], outs[0].at[me], sems[0].at[k], sems[1].at[k], (px, py, c))
                 for k, (px, py) in enumerate(chips)]
        recvs = [_remote(outs[0].at[2 * px + py], outs[0].at[2 * px + py], sems[0].at[k], sems[1].at[k], (px, py, c))
                 for k, (px, py) in enumerate(chips)]
        return local, sends, recvs

    def start(ins, outs, sems):
        local, sends, _ = copies(ins, outs, sems)
        local.start()
        for cp in sends:
            cp.start()

    def finish(ins, outs, sems):
        local, sends, recvs = copies(ins, outs, sems)
        for cp in recvs:
            cp.wait_recv()
        for cp in sends:
            cp.wait_send()
        local.wait()

    return _Comm([parts], [_shape(parts.shape, parts.dtype)], {}, [(3,), (3,), (1,)], start, finish, then)


def _phase_pair_allgather(half, layer, depth, into, then):
    rh, C = half.shape

    def copies(ins, outs, sems):
        x, y, c, _ = _place()
        mine = outs[0].at[layer, pl.ds(c * rh, rh), :]
        theirs = outs[0].at[layer, pl.ds((1 - c) * rh, rh), :]
        return (pltpu.make_async_copy(ins[0], mine, sems[2].at[0]),
                _remote(ins[0], mine, sems[0].at[0], sems[1].at[0], (x, y, 1 - c)),
                _remote(theirs, theirs, sems[0].at[0], sems[1].at[0], (x, y, 1 - c)))

    def start(ins, outs, sems):
        local, send, _ = copies(ins, outs, sems)
        local.start()
        send.start()

    def finish(ins, outs, sems):
        local, send, recv = copies(ins, outs, sems)
        recv.wait_recv()
        send.wait_send()
        local.wait()

    ins = [half] if into is None else [half, into]
    return _Comm(ins, [_shape((depth, 2 * rh, C), half.dtype)], {} if into is None else {1: 0}, [(1,), (1,), (1,)],
                 start, finish, then)


def _comm_only(comms, name):
    plan = _Plan()
    for c in comms:
        plan.at(name, c)
    saved, _PLAN[0] = _PLAN[0], plan
    try:
        def body(o_ref):
            o_ref[...] = jnp.zeros_like(o_ref)

        _pallas(body, out_shape=[jax.ShapeDtypeStruct((8, 128), F32)], in_specs=[],
                out_specs=[pl.BlockSpec(memory_space=pltpu.VMEM)], name=name)()
    finally:
        _PLAN[0] = saved


def _gather_chips(items):
    n = len(items)

    def body(*refs):
        srcs, outs = refs[:n], refs[n:2 * n]
        send_sems, recv_sems, loc_sems = refs[2 * n:]
        x, y, c, chips = _place()
        me = 2 * x + y

        def half(t, chip, cc):
            rh = items[t].shape[1] // 2
            return outs[t].at[chip, :, pl.ds(cc * rh, rh), :]

        def copy(t, k, src, dst, to):
            return pltpu.make_async_remote_copy(src_ref=src, dst_ref=dst, send_sem=send_sems.at[t, k],
                                                recv_sem=recv_sems.at[t, k], device_id=to, device_id_type=MESH)

        local = [pltpu.make_async_copy(srcs[t], outs[t].at[me], loc_sems.at[t]) for t in range(n)]
        for cp in local:
            cp.start()
        sent = []
        for t in range(n):
            rh = items[t].shape[1] // 2
            mine = srcs[t].at[:, pl.ds(c * rh, rh), :]
            for k, (px, py) in enumerate(chips):
                sent.append(copy(t, k, mine, half(t, me, c), (px, py, c)))
                sent[-1].start()
        for k, (px, py) in enumerate(chips):
            for t in range(n):
                blk = half(t, 2 * px + py, c)
                copy(t, k, blk, blk, (px, py, c)).wait_recv()
                sent.append(copy(t, 3 + k, blk, blk, (x, y, 1 - c)))
                sent[-1].start()
        for k, (px, py) in enumerate(chips):
            for t in range(n):
                blk = half(t, 2 * px + py, 1 - c)
                copy(t, 3 + k, blk, blk, (x, y, 1 - c)).wait_recv()
        for cp in sent:
            cp.wait_send()
        for cp in local:
            cp.wait()

    return _pallas(
        body,
        out_shape=[_out((4,) + a.shape, a.dtype) for a in items],
        in_specs=[ANY] * n,
        out_specs=[ANY] * n,
        scratch_shapes=[pltpu.SemaphoreType.DMA((n, 6)), pltpu.SemaphoreType.DMA((n, 6)), pltpu.SemaphoreType.DMA((n,))],
        name="gather_weights",
    )(*items)


def _pair_exchange(items):
    n = len(items)

    def body(*refs):
        srcs, outs = refs[:n], refs[n:2 * n]
        send_sems, recv_sems = refs[2 * n:]
        x, y, c, _ = _place()
        cps = []
        for t in range(n):
            rh = items[t].shape[2] // 2
            cps.append(pltpu.make_async_remote_copy(
                src_ref=srcs[t].at[:, :, pl.ds((1 - c) * rh, rh), :], dst_ref=outs[t], send_sem=send_sems.at[t],
                recv_sem=recv_sems.at[t], device_id=(x, y, 1 - c), device_id_type=MESH))
            cps[-1].start()
        for cp in cps:
            cp.wait()

    return _pallas(
        body,
        out_shape=[_out(a.shape[:2] + (a.shape[2] // 2, a.shape[3]), a.dtype) for a in items],
        in_specs=[ANY] * n,
        out_specs=[ANY] * n,
        scratch_shapes=[pltpu.SemaphoreType.DMA((n,)), pltpu.SemaphoreType.DMA((n,))],
        name="reduce_pair_exchange",
    )(*items)


def _row_tile(rows, cols, itemsize=4, budget=1 << 20):
    t = rows
    while t % 2 == 0 and t // 2 >= 8 and (t // 2) % 8 == 0 and t * cols * itemsize > budget:
        t //= 2
    return t


def _pair_add(full, got, c, name):
    _, L, R, C = full.shape
    rh = R // 2
    tr = _row_tile(rh, C)
    nb = rh // tr

    def body(c_ref, a_ref, b_ref, o_ref):
        o_ref[...] = (a_ref[...] + b_ref[...]).astype(o_ref.dtype)

    grid_spec = pltpu.PrefetchScalarGridSpec(
        num_scalar_prefetch=1,
        grid=(4, L, nb),
        in_specs=[pl.BlockSpec((1, 1, tr, C), lambda j, l, i, c_ref: (j, l, c_ref[0] * nb + i, 0)),
                  pl.BlockSpec((1, 1, tr, C), lambda j, l, i, c_ref: (j, l, i, 0))],
        out_specs=pl.BlockSpec((1, 1, tr, C), lambda j, l, i, c_ref: (j, l, i, 0)),
    )
    return _pallas(
        body,
        out_shape=_out((4, L, rh, C), BF16),
        grid_spec=grid_spec,
        compiler_params=_cparams(("parallel", "parallel", "parallel"), 8 * tr * C * 4),
        name=name,
    )(jnp.reshape(c, (1,)).astype(jnp.int32), full, got)


def _chip_scatter(items):
    n = len(items)

    def body(*refs):
        srcs, outs = refs[:n], refs[n:2 * n]
        send_sems, recv_sems, loc_sems = refs[2 * n:]
        x, y, c, chips = _place()
        me = 2 * x + y
        local = [pltpu.make_async_copy(srcs[t].at[me], outs[t].at[me], loc_sems.at[t]) for t in range(n)]
        for cp in local:
            cp.start()
        cps = []
        for t in range(n):
            for k, (px, py) in enumerate(chips):
                cps.append(pltpu.make_async_remote_copy(
                    src_ref=srcs[t].at[2 * px + py], dst_ref=outs[t].at[me], send_sem=send_sems.at[t, k],
                    recv_sem=recv_sems.at[t, k], device_id=(px, py, c), device_id_type=MESH))
                cps[-1].start()
        for t in range(n):
            for k, (px, py) in enumerate(chips):
                blk = outs[t].at[2 * px + py]
                pltpu.make_async_remote_copy(src_ref=blk, dst_ref=blk, send_sem=send_sems.at[t, k], recv_sem=recv_sems.at[t, k],
                                             device_id=(px, py, c), device_id_type=MESH).wait_recv()
        for cp in cps:
            cp.wait_send()
        for cp in local:
            cp.wait()

    return _pallas(
        body,
        out_shape=[_out(a.shape, a.dtype) for a in items],
        in_specs=[ANY] * n,
        out_specs=[ANY] * n,
        scratch_shapes=[pltpu.SemaphoreType.DMA((n, 3)), pltpu.SemaphoreType.DMA((n, 3)), pltpu.SemaphoreType.DMA((n,))],
        name="reduce_chip_scatter",
    )(*items)


def _sum_chips(parts, name):
    _, L, rh, C = parts.shape
    tr = _row_tile(rh, C)

    def body(p_ref, o_ref):
        acc = p_ref[0].astype(F32)
        for j in range(1, 4):
            acc = acc + p_ref[j].astype(F32)
        o_ref[...] = acc

    return _pallas(
        body,
        out_shape=_out((L, rh, C), F32),
        grid=(L, rh // tr),
        in_specs=[pl.BlockSpec((4, 1, tr, C), lambda l, i: (0, l, i, 0))],
        out_specs=pl.BlockSpec((1, tr, C), lambda l, i: (l, i, 0)),
        compiler_params=_cparams(("parallel", "parallel"), 16 * tr * C * 4),
        name=name,
    )(parts)


def _pair_allgather(groups):
    flat = [(w, l) for w, grp in enumerate(groups) for l in range(len(grp))]
    items = [groups[w][l] for w, l in flat]
    n, nw = len(items), len(groups)

    def body(*refs):
        srcs, outs = refs[:n], refs[n:n + nw]
        send_sems, recv_sems, loc_sems = refs[n + nw:]
        x, y, c, _ = _place()
        cps, local = [], []
        for t, (w, l) in enumerate(flat):
            rh = items[t].shape[1]
            mine = outs[w].at[pl.ds(l, 1), pl.ds(c * rh, rh), :]
            local.append(pltpu.make_async_copy(srcs[t], mine, loc_sems.at[t]))
            local[-1].start()
            cps.append(pltpu.make_async_remote_copy(src_ref=srcs[t], dst_ref=mine, send_sem=send_sems.at[t],
                                                    recv_sem=recv_sems.at[t], device_id=(x, y, 1 - c), device_id_type=MESH))
            cps[-1].start()
        for t, (w, l) in enumerate(flat):
            rh = items[t].shape[1]
            theirs = outs[w].at[pl.ds(l, 1), pl.ds((1 - c) * rh, rh), :]
            pltpu.make_async_remote_copy(src_ref=theirs, dst_ref=theirs, send_sem=send_sems.at[t], recv_sem=recv_sems.at[t],
                                         device_id=(x, y, 1 - c), device_id_type=MESH).wait_recv()
        for cp in cps:
            cp.wait_send()
        for cp in local:
            cp.wait()

    return _pallas(
        body,
        out_shape=[_out((len(grp), 2 * grp[0].shape[1], grp[0].shape[2]), grp[0].dtype) for grp in groups],
        in_specs=[ANY] * n,
        out_specs=[ANY] * nw,
        scratch_shapes=[pltpu.SemaphoreType.DMA((n,)), pltpu.SemaphoreType.DMA((n,)), pltpu.SemaphoreType.DMA((n,))],
        name="reduce_pair_allgather",
    )(*items)


def _allreduce_small(v):
    R = v.shape[0]

    def body(v_ref, o_ref, slots, send_sems, recv_sems):
        x, y, c, _ = _place()
        me = 4 * x + 2 * y + c
        slots[me] = v_ref[...]
        cps = []
        for r in range(1, 8):
            px, py, pc = x ^ (r >> 2), y ^ ((r >> 1) & 1), c ^ (r & 1)
            cps.append(pltpu.make_async_remote_copy(src_ref=v_ref, dst_ref=slots.at[me], send_sem=send_sems.at[r - 1],
                                                    recv_sem=recv_sems.at[r - 1], device_id=(px, py, pc), device_id_type=MESH))
            cps[-1].start()
        for r in range(1, 8):
            px, py, pc = x ^ (r >> 2), y ^ ((r >> 1) & 1), c ^ (r & 1)
            theirs = slots.at[4 * px + 2 * py + pc]
            pltpu.make_async_remote_copy(src_ref=theirs, dst_ref=theirs, send_sem=send_sems.at[r - 1], recv_sem=recv_sems.at[r - 1],
                                         device_id=(px, py, pc), device_id_type=MESH).wait_recv()
        acc = slots[0]
        for j in range(1, 8):
            acc = acc + slots[j]
        o_ref[...] = acc
        for cp in cps:
            cp.wait_send()

    return pl.pallas_call(
        body,
        out_shape=jax.ShapeDtypeStruct(v.shape, F32),
        in_specs=[pl.BlockSpec(memory_space=pltpu.VMEM)],
        out_specs=pl.BlockSpec(memory_space=pltpu.VMEM),
        scratch_shapes=[pltpu.VMEM((8, R, 128), F32), pltpu.SemaphoreType.DMA((7,)), pltpu.SemaphoreType.DMA((7,))],
        name="allreduce_small",
    )(v)


def _adamw_math(w, g, m, v):
    m = ADAM_B1 * m + (1.0 - ADAM_B1) * g
    v = ADAM_B2 * v + (1.0 - ADAM_B2) * (g * g)
    m_hat = m / (1.0 - ADAM_B1 ** ADAM_STEP)
    v_hat = v / (1.0 - ADAM_B2 ** ADAM_STEP)
    delta = -ADAM_LR * (m_hat / (jnp.sqrt(v_hat) + ADAM_EPS) + ADAM_WD * w)
    return delta, m, v


def _adamw(w, g, m, v, name):
    rows, cols = w.shape
    tr = _row_tile(rows, cols, budget=1 << 19)

    def body(w_ref, g_ref, m_ref, v_ref, d_ref, nm_ref, nv_ref):
        d_ref[...], nm_ref[...], nv_ref[...] = _adamw_math(w_ref[...], g_ref[...], m_ref[...], v_ref[...])

    blk = pl.BlockSpec((tr, cols), lambda i: (i, 0))
    out = _out(w.shape, F32)
    return _pallas(
        body,
        out_shape=(out, out, out),
        grid=(rows // tr,),
        in_specs=[blk] * 4,
        out_specs=(blk, blk, blk),
        compiler_params=_cparams(("parallel",), 16 * tr * cols * 4),
        name=name,
    )(w, g, m, v)


def _adamw_small(ws, gs, ms, vs):
    n = len(ws)

    def body(*refs):
        for t in range(n):
            w_ref, g_ref, m_ref, v_ref = (refs[k * n + t] for k in range(4))
            d_ref, nm_ref, nv_ref = (refs[(4 + k) * n + t] for k in range(3))
            d_ref[...], nm_ref[...], nv_ref[...] = _adamw_math(w_ref[...], g_ref[...], m_ref[...], v_ref[...])

    vmem = pl.BlockSpec(memory_space=pltpu.VMEM)
    outs = [jax.ShapeDtypeStruct(w.shape, F32) for w in ws]
    res = pl.pallas_call(
        body,
        out_shape=outs * 3,
        in_specs=[vmem] * (4 * n),
        out_specs=[vmem] * (3 * n),
        name="adamw_small",
    )(*ws, *gs, *ms, *vs)
    return res[:n], res[n:2 * n], res[2 * n:]


IN_SIZES = (192, 192, 384, 384, 16, 512, 384, 384, 384)
IN_OFFS = tuple(int(v) for v in np.cumsum((0,) + IN_SIZES))
SMALL = ("norm_mix", "w_gla_gate", "b_gla_gate", "gla_norm", "b_dw", "conv_ln_g", "conv_ln_b", "rel_bias", "norm_ffn")


def _pad_cols(a, n):
    return jnp.pad(a, ((0, 0), (0, n - a.shape[1])))


def _split_w_in(w):
    s = [w[:, IN_OFFS[i]:IN_OFFS[i + 1]] for i in range(9)]
    w_gla = jnp.concatenate([_pad_cols(s[0], KW), _pad_cols(s[1], KW), s[2], s[3], _pad_cols(s[4], LRW)], axis=1)
    return w_gla, s[5], jnp.concatenate(s[6:9], axis=1)


def _join_w_in(g):
    gg = g["w_gla"]
    return jnp.concatenate([gg[:, 0:192], gg[:, KW:KW + 192], gg[:, 2 * KW:2 * KW + VW], gg[:, 2 * KW + VW:2 * KW + 2 * VW],
                            gg[:, 2 * KW + 2 * VW:2 * KW + 2 * VW + 16], g["w_conv"], g["w_att"]],
                           axis=1)


def _pack(arrs, rows):
    flat = jnp.concatenate([a.reshape(-1) for a in arrs])
    return jnp.pad(flat, (0, rows * 128 - flat.shape[0])).reshape(rows, 128)


def _unpack(packed, shapes):
    flat = packed.reshape(-1)
    out, off = [], 0
    for s in shapes:
        n = int(np.prod(s))
        out.append(flat[off:off + n].reshape(s))
        off += n
    return out


def kernel(x, norm_mix, w_in, w_gla_gate, b_gla_gate, gla_norm, w_dw, b_dw, conv_ln_g, conv_ln_b, rel_bias, w_out, norm_ffn, w_up, w_down, norm_final, loss_target, m_norm_mix, m_w_in, m_w_gla_gate, m_b_gla_gate, m_gla_norm, m_w_dw, m_b_dw, m_conv_ln_g, m_conv_ln_b, m_rel_bias, m_w_out, m_norm_ffn, m_w_up, m_w_down, m_norm_final, v_norm_mix, v_w_in, v_w_gla_gate, v_b_gla_gate, v_gla_norm, v_w_dw, v_b_dw, v_conv_ln_g, v_conv_ln_b, v_rel_bias, v_w_out, v_norm_ffn, v_w_up, v_w_down, v_norm_final):
    P = dict(norm_mix=norm_mix, w_in=w_in, w_gla_gate=w_gla_gate, b_gla_gate=b_gla_gate, gla_norm=gla_norm, w_dw=w_dw, b_dw=b_dw,
             conv_ln_g=conv_ln_g, conv_ln_b=conv_ln_b, rel_bias=rel_bias, w_out=w_out, norm_ffn=norm_ffn, w_up=w_up,
             w_down=w_down, norm_final=norm_final)
    Mo = dict(norm_mix=m_norm_mix, w_in=m_w_in, w_gla_gate=m_w_gla_gate, b_gla_gate=m_b_gla_gate, gla_norm=m_gla_norm, w_dw=m_w_dw,
              b_dw=m_b_dw, conv_ln_g=m_conv_ln_g, conv_ln_b=m_conv_ln_b, rel_bias=m_rel_bias, w_out=m_w_out, norm_ffn=m_norm_ffn,
              w_up=m_w_up, w_down=m_w_down, norm_final=m_norm_final)
    Vo = dict(norm_mix=v_norm_mix, w_in=v_w_in, w_gla_gate=v_w_gla_gate, b_gla_gate=v_b_gla_gate, gla_norm=v_gla_norm, w_dw=v_w_dw,
              b_dw=v_b_dw, conv_ln_g=v_conv_ln_g, conv_ln_b=v_conv_ln_b, rel_bias=v_rel_bias, w_out=v_w_out, norm_ffn=v_norm_ffn,
              w_up=v_w_up, w_down=v_w_down, norm_final=v_norm_final)
    depth = w_in.shape[0]
    xi, yi, ci = lax.axis_index("x"), lax.axis_index("y"), lax.axis_index("c")
    chip = 2 * xi + yi

    plan = _Plan()
    _PLAN[0] = plan
    layers = [dict(
        norm_mix=norm_mix[l][None], wg=jnp.pad(w_gla_gate[l], ((0, LRW - 16), (0, KW - 192))),
        bg=_pad_cols(b_gla_gate[l][None], KW), gla_norm=gla_norm[l][None], b_dw=b_dw[l][None], ln_g=conv_ln_g[l][None],
        ln_b=conv_ln_b[l][None], rel_bias=rel_bias[l], norm_ffn=norm_ffn[l][None]) for l in range(depth)]

    def have_w_in(l, full):
        layers[l]["w_gla"], layers[l]["w_conv"], layers[l]["w_att"] = _split_w_in(jnp.transpose(full, (1, 0, 2)).reshape(D, -1))

    def have_w_out(l, full):
        w = full.reshape(D, D)
        layers[l]["w_out_g"], layers[l]["w_out_c"], layers[l]["w_out_a"] = w[0:VW], w[VW:VW + CW], w[VW + CW:]

    def have_w_up(l, full):
        layers[l]["w_up"] = full

    def have_w_down(l, full):
        layers[l]["w_down"] = full.reshape(D_FF, D)

    w_dw_pad = jnp.pad(w_dw, ((0, 0), (0, HALO - CK), (0, 0)))
    first_in, g_dw = _gather_chips([w_in[0:1].astype(BF16), w_dw_pad])
    have_w_in(0, first_in[:, 0])
    for l in range(depth):
        layers[l]["w_dw"] = jnp.transpose(g_dw[:, l], (1, 0, 2)).reshape(HALO, CW)

    def gather_behind(shard, ici_call, d2d_call, have):
        plan.at(ici_call, _phase_gather_ici(
            shard, lambda outs: plan.at(d2d_call, _phase_gather_d2d(outs[0], lambda done: have(done[0])))))

    for l in range(depth):
        if l > 0:
            gather_behind(w_in[l].astype(BF16), f"l{l - 1}_mlp_up", f"l{l - 1}_mlp_down", functools.partial(have_w_in, l))
        gather_behind(w_out[l].astype(BF16), f"l{l - 1}_mlp_down" if l > 0 else "l0_proj_gla", f"l{l}_gla_fwd",
                      functools.partial(have_w_out, l))
        gather_behind(w_up[l].astype(BF16), f"l{l}_gla_fwd", f"l{l}_att_fwd", functools.partial(have_w_up, l))
        gather_behind(w_down[l].astype(BF16), f"l{l}_att_fwd", f"l{l}_mlp_up", functools.partial(have_w_down, l))

    reduced = {}

    def reduce_calls(name, l):
        if name == "w_down":
            return f"l{l}_mlp_up_dx", f"l{l}_mlp_up_dw", f"l{l}_out_gla_dx"
        if name == "w_up":
            return f"l{l}_out_conv_dx", f"l{l}_gla_bwd", f"l{l}_conv_bwd_dc"
        if name == "w_out":
            return f"l{l}_gla_bwd", f"l{l}_conv_bwd_dc", f"l{l}_att_bwd"
        if l > 0:
            return f"l{l}_proj_gla_dx", f"l{l - 1}_mlp_down_dx", f"l{l - 1}_mlp_down_dw"
        return "l0_proj_gla_dx", "l0_proj_att_dx", "l0_norm_mix_bwd"

    def reduce_behind(l, name, full):
        calls = reduce_calls(name, l)

        def swapped(outs):
            pair = _pair_add(full[:, None], outs[0][:, None], ci, f"reduce_pair_add_{name}{l}")[:, 0]
            plan.at(calls[1], _phase_chip_scatter(pair, scattered))

        def scattered(outs):
            half = _sum_chips(outs[0][:, None], f"reduce_sum_chips_{name}{l}")[0]
            plan.at(calls[2], _phase_pair_allgather(half, l, depth, reduced.get(name), gathered))

        def gathered(outs):
            reduced[name] = outs[0]

        plan.at(calls[0], _phase_pair_exchange(full, swapped))

    loss_part, grad_x, grads, g_final = _local_step(x[0], loss_target[0], layers, norm_final[None], reduce_behind)
    _PLAN[0] = None
    assert not plan.by_call, sorted(plan.by_call)
    loss = lax.psum(loss_part[0, 0], ("x", "y", "c"))
    gw_in, gw_out, gw_up, gw_down = reduced["w_in"], reduced["w_out"], reduced["w_up"], reduced["w_down"]

    small_g = []
    for l in range(depth):
        g = grads[l]
        small_g += [g["norm_mix"], g["wg"][0:16, 0:192], g["bg"][:, 0:192], g["gla_norm"], g["b_dw"], g["ln_g"], g["ln_b"],
                    g["rel_bias"], g["norm_ffn"], g["w_dw"][0:CK]]
    small_g.append(g_final)
    small_shapes = [a.shape for a in small_g]
    n_small = sum(int(np.prod(s)) for s in small_shapes)
    rows = -(-n_small // 1024) * 8
    red = _unpack(_allreduce_small(_pack(small_g, rows)), small_shapes)
    G = {}
    per = len(SMALL) + 1
    for i, name in enumerate(SMALL):
        G[name] = jnp.stack([red[l * per + i].reshape(P[name].shape[1:]) for l in range(depth)])
    gw_dw_all = jnp.stack([red[l * per + len(SMALL)] for l in range(depth)])
    G["w_dw"] = lax.dynamic_slice_in_dim(gw_dw_all, chip * (CW // 4), CW // 4, axis=2)
    G["norm_final"] = red[-1].reshape(norm_final.shape)
    G["w_in"], G["w_out"], G["w_up"], G["w_down"] = gw_in, gw_out, gw_up, gw_down

    delta, new_m, new_v = {}, {}, {}
    for name in ("w_in", "w_out", "w_up", "w_down"):
        shp = P[name].shape
        two_d = lambda a: a.reshape(-1, shp[-1])
        d, nm, nv = _adamw(two_d(P[name]), two_d(G[name]), two_d(Mo[name]), two_d(Vo[name]), f"adamw_{name}")
        delta[name], new_m[name], new_v[name] = d.reshape(shp), nm.reshape(shp), nv.reshape(shp)
    small_names = list(SMALL) + ["w_dw", "norm_final"]
    two_d = lambda a: a.reshape(-1, a.shape[-1])
    ds, nms, nvs = _adamw_small([two_d(P[k]) for k in small_names], [two_d(G[k]) for k in small_names],
                                [two_d(Mo[k]) for k in small_names], [two_d(Vo[k]) for k in small_names])
    for i, name in enumerate(small_names):
        shp = P[name].shape
        delta[name], new_m[name], new_v[name] = ds[i].reshape(shp), nms[i].reshape(shp), nvs[i].reshape(shp)

    order = ["norm_mix", "w_in", "w_gla_gate", "b_gla_gate", "gla_norm", "w_dw", "b_dw", "conv_ln_g", "conv_ln_b", "rel_bias",
             "w_out", "norm_ffn", "w_up", "w_down", "norm_final"]
    return (loss, grad_x[None], *[G[k] for k in order], *[delta[k] for k in order], *[new_m[k] for k in order],
            *[new_v[k] for k in order])
```

```python
import functools

import numpy as np
import jax
import jax.numpy as jnp
from jax import lax
from jax.experimental import pallas as pl
from jax.experimental.pallas import tpu as pltpu

F32 = jnp.float32
BF16 = jnp.bfloat16
HI = lax.Precision.HIGHEST

D = 1024
CHUNK = 64
GLA_DK, GLA_DV, GLA_H = 48, 96, 4
KW = 256
VW = 384
LRW = 128
GLA_TAU = 16.0
CW = 256
CK = 31
AW = 384
AH = 6
BAND = 576
LEFT = 512
D_FF = 4096
EPS = 1e-6
NEG = -1e30
N_REL = 257

GLA_COLS = 2 * KW + 2 * VW + LRW
CONV_COLS = 2 * CW
ATT_COLS = 3 * AW

ADAM_LR, ADAM_B1, ADAM_B2, ADAM_EPS, ADAM_WD, ADAM_STEP = 0.001, 0.9, 0.999, 1e-08, 0.01, 10

VMEM_CAP = 56 * 1024 * 1024
MESH = pl.DeviceIdType.MESH


def _cparams(sem, vmem_bytes):
    limit = int(min(VMEM_CAP, max(vmem_bytes * 5 // 4 + (4 << 20), 16 << 20)))
    return pltpu.CompilerParams(dimension_semantics=sem, vmem_limit_bytes=limit)


def _out(shape, dtype):
    return pltpu.HBM(tuple(shape), dtype)


class _Comm:
    def __init__(self, ins, outs, aliases, sems, start, finish, then=None):
        self.ins, self.outs, self.aliases, self.sems = list(ins), list(outs), dict(aliases), list(sems)
        self.start, self.finish, self.then = start, finish, then


class _Plan:
    def __init__(self):
        self.by_call = {}

    def at(self, call, comm):
        self.by_call.setdefault(call, []).append(comm)

    def take(self, call):
        return self.by_call.pop(call, [])


_PLAN = [None]


def _pin(a):
    return pltpu.with_memory_space_constraint(a, pltpu.HBM) if jnp.issubdtype(a.dtype, jnp.floating) else a


def _pallas(body, **kw):
    comms = _PLAN[0].take(kw.get("name")) if _PLAN[0] is not None else []
    if not comms:
        call = pl.pallas_call(body, **kw)
        return lambda *args: call(*[_pin(a) for a in args])

    grid = tuple(kw.get("grid", ()))
    single = not isinstance(kw["out_shape"], (tuple, list))
    out_shape = [kw["out_shape"]] if single else list(kw["out_shape"])
    out_specs = [kw["out_specs"]] if single else list(kw["out_specs"])
    in_specs = list(kw["in_specs"])
    scratch = list(kw.get("scratch_shapes", ()))
    n_in, n_out, n_scr = len(in_specs), len(out_shape), len(scratch)
    c_in = sum(len(c.ins) for c in comms)
    c_out = sum(len(c.outs) for c in comms)
    aliases = dict(kw.get("input_output_aliases", {}))
    i0, o0 = n_in, n_out
    for c in comms:
        for i, o in c.aliases.items():
            aliases[i0 + i] = o0 + o
        i0 += len(c.ins)
        o0 += len(c.outs)

    def wrapped(*refs):
        ins, c_ins = refs[:n_in], refs[n_in:n_in + c_in]
        outs, c_outs = refs[n_in + c_in:n_in + c_in + n_out], refs[n_in + c_in + n_out:n_in + c_in + n_out + c_out]
        scr, c_sems = refs[n_in + c_in + n_out + c_out:][:n_scr], refs[n_in + c_in + n_out + c_out + n_scr:]

        def each(what):
            i0 = o0 = s0 = 0
            for c in comms:
                getattr(c, what)(c_ins[i0:i0 + len(c.ins)], c_outs[o0:o0 + len(c.outs)], c_sems[s0:s0 + len(c.sems)])
                i0, o0, s0 = i0 + len(c.ins), o0 + len(c.outs), s0 + len(c.sems)

        if grid:
            first = functools.reduce(jnp.logical_and, [pl.program_id(a) == 0 for a in range(len(grid))])
            last = functools.reduce(jnp.logical_and, [pl.program_id(a) == grid[a] - 1 for a in range(len(grid))])
            pl.when(first)(lambda: each("start"))
            body(*ins, *outs, *scr)
            pl.when(last)(lambda: each("finish"))
        else:
            each("start")
            body(*ins, *outs, *scr)
            each("finish")

    kw = dict(kw)
    kw["in_specs"] = in_specs + [ANY] * c_in
    kw["out_shape"] = out_shape + [_out(s.shape, s.dtype) for c in comms for s in c.outs]
    kw["out_specs"] = out_specs + [ANY] * c_out
    kw["scratch_shapes"] = scratch + [pltpu.SemaphoreType.DMA(s) for c in comms for s in c.sems]
    kw["input_output_aliases"] = aliases
    call = pl.pallas_call(wrapped, **kw)

    def run(*args):
        res = call(*[_pin(a) for a in args], *[_pin(a) for c in comms for a in c.ins])
        o0 = n_out
        for c in comms:
            if c.then is not None:
                c.then(res[o0:o0 + len(c.outs)])
            o0 += len(c.outs)
        return res[0] if single else res[:n_out]

    return run


def _nbytes(shape, dtype):
    return int(np.prod(shape)) * jnp.dtype(dtype).itemsize


def _sigmoid(x):
    return 1.0 / (1.0 + jnp.exp(-x))


_DIMS = {"nn": (((1,), (0,)), ((), ())), "nt": (((1,), (1,)), ((), ())), "tn": (((0,), (0,)), ((), ()))}


def _mm(a, b, *, mode, out_dtype, name, tm=512, tn=None, tk=None, a_pro=None, epi=None, extra=None,
        b_chips=False, out_chips=False):
    b2 = (b.shape[1], 4 * b.shape[2]) if b_chips else b.shape
    if mode == "nn":
        (M, K), (K2, N) = a.shape, b2
    elif mode == "nt":
        (M, K), (N, K2) = a.shape, b2
    else:
        (K, M), (K2, N) = a.shape, b2
    assert K == K2, (a.shape, b.shape, mode)
    tm = min(tm, M)
    tn = N if tn is None else min(tn, N)
    tk = K if tk is None else min(tk, K)
    assert M % tm == 0 and N % tn == 0 and K % tk == 0, (M, N, K, tm, tn, tk)
    nk = K // tk
    a_blk = (tk, tm) if mode == "tn" else (tm, tk)
    a_map = (lambda i, j, k: (k, i)) if mode == "tn" else (lambda i, j, k: (i, k))
    b_blk = (tn, tk) if mode == "nt" else (tk, tn)
    b_map = (lambda i, j, k: (j, k)) if mode == "nt" else (lambda i, j, k: (k, j))
    if b_chips:
        per = b.shape[2] // b_blk[1]
        assert b.shape[2] % b_blk[1] == 0 and mode != "tn"
        flat_map = b_map
        b_map = lambda i, j, k: (flat_map(i, j, k)[1] // per, flat_map(i, j, k)[0], flat_map(i, j, k)[1] % per)
        b_blk = (None,) + b_blk
    in_specs = [pl.BlockSpec(a_blk, a_map), pl.BlockSpec(b_blk, b_map)]
    args = [a, b]
    if epi is not None:
        in_specs.append(pl.BlockSpec((tm, tn), lambda i, j, k: (i, j)))
        args.append(extra)

    def body(*refs):
        a_ref, b_ref = refs[0], refs[1]
        e_ref = refs[2] if epi is not None else None
        o_ref = refs[3] if epi is not None else refs[2]
        av = a_ref[...]
        if a_pro == "relu2":
            af = jnp.maximum(av.astype(F32), 0.0)
            av = af * af
        p = lax.dot_general(av.astype(BF16), b_ref[...].astype(BF16), _DIMS[mode], preferred_element_type=F32)

        def finish(acc):
            if epi == "add":
                acc = acc + e_ref[...].astype(F32)
            elif epi == "relu2grad":
                acc = acc * (2.0 * jnp.maximum(e_ref[...].astype(F32), 0.0))
            o_ref[...] = acc.astype(o_ref.dtype)

        if nk == 1:
            finish(p)
        else:
            acc_ref = refs[-1]
            k = pl.program_id(2)

            @pl.when(k == 0)
            def _():
                acc_ref[...] = p

            @pl.when(k > 0)
            def _():
                acc_ref[...] += p

            @pl.when(k == nk - 1)
            def _():
                finish(acc_ref[...])

    vm = 2 * (_nbytes(a_blk, a.dtype) + _nbytes((tk, tn), b.dtype) + _nbytes((tm, tn), out_dtype))
    vm += 3 * _nbytes((tm, tn), F32)
    if epi is not None:
        vm += 2 * _nbytes((tm, tn), extra.dtype)
    if out_chips:
        per_out = N // 4 // tn
        assert N % (4 * tn) == 0
        out_shape = _out((4, M, N // 4), out_dtype)
        out_spec = pl.BlockSpec((None, tm, tn), lambda i, j, k: (j // per_out, i, j % per_out))
    else:
        out_shape = _out((M, N), out_dtype)
        out_spec = pl.BlockSpec((tm, tn), lambda i, j, k: (i, j))
    return _pallas(
        body,
        out_shape=out_shape,
        grid=(M // tm, N // tn, nk),
        in_specs=in_specs,
        out_specs=out_spec,
        scratch_shapes=[pltpu.VMEM((tm, tn), F32)] if nk > 1 else [],
        compiler_params=_cparams(("parallel", "parallel", "arbitrary"), vm),
        name=name,
    )(*args)


def _mm_fan(a, bs, *, mode, out_dtypes, name, tm=512):
    M, K = a.shape
    ns = [b.shape[1] if mode == "nn" else b.shape[0] for b in bs]
    n = len(bs)

    def body(*refs):
        av = refs[0][...].astype(BF16)
        for i in range(n):
            refs[1 + n + i][...] = lax.dot_general(av, refs[1 + i][...], _DIMS[mode],
                                                   preferred_element_type=F32).astype(refs[1 + n + i].dtype)

    vm = 2 * _nbytes((tm, K), a.dtype) + sum(2 * _nbytes(b.shape, b.dtype) + 3 * _nbytes((tm, nn), F32) for b, nn in zip(bs, ns))
    return _pallas(
        body,
        out_shape=tuple(_out((M, nn), dt) for nn, dt in zip(ns, out_dtypes)),
        grid=(M // tm,),
        in_specs=[pl.BlockSpec((tm, K), lambda i: (i, 0))] + [pl.BlockSpec(b.shape, lambda i: (0, 0)) for b in bs],
        out_specs=tuple(pl.BlockSpec((tm, nn), lambda i: (i, 0)) for nn in ns),
        compiler_params=_cparams(("parallel",), vm),
        name=name,
    )(a, *bs)


def _mm_sum(as_, bs, *, mode, out_dtype, name, extra=None, tm=512):
    M = as_[0].shape[0]
    N = bs[0].shape[1] if mode == "nn" else bs[0].shape[0]
    n = len(as_)

    def body(*refs):
        acc = None
        for i in range(n):
            p = lax.dot_general(refs[i][...].astype(BF16), refs[n + i][...], _DIMS[mode], preferred_element_type=F32)
            acc = p if acc is None else acc + p
        if extra is not None:
            acc = acc + refs[2 * n][...].astype(F32)
        refs[-1][...] = acc.astype(refs[-1].dtype)

    in_specs = [pl.BlockSpec((tm, a.shape[1]), lambda i: (i, 0)) for a in as_]
    in_specs += [pl.BlockSpec(b.shape, lambda i: (0, 0)) for b in bs]
    args = list(as_) + list(bs)
    if extra is not None:
        in_specs.append(pl.BlockSpec((tm, N), lambda i: (i, 0)))
        args.append(extra)
    vm = sum(2 * _nbytes((tm, a.shape[1]), a.dtype) for a in as_) + sum(2 * _nbytes(b.shape, b.dtype) for b in bs)
    vm += 8 * _nbytes((tm, N), F32)
    return _pallas(
        body,
        out_shape=_out((M, N), out_dtype),
        grid=(M // tm,),
        in_specs=in_specs,
        out_specs=pl.BlockSpec((tm, N), lambda i: (i, 0)),
        compiler_params=_cparams(("parallel",), vm),
        name=name,
    )(*args)


def _mm_tn_multi(ops, pairs, *, name, tk=512):
    T = ops[0].shape[0]
    n, m = len(ops), len(pairs)
    shapes = [(ops[a].shape[1], ops[b].shape[1]) for a, b in pairs]

    def body(*refs):
        vals = [refs[i][...].astype(BF16) for i in range(n)]
        first = pl.program_id(0) == 0
        for j, (a, b) in enumerate(pairs):
            p = lax.dot_general(vals[a], vals[b], _DIMS["tn"], preferred_element_type=F32)
            o_ref = refs[n + j]

            @pl.when(first)
            def _(o_ref=o_ref, p=p):
                o_ref[...] = p

            @pl.when(jnp.logical_not(first))
            def _(o_ref=o_ref, p=p):
                o_ref[...] += p

    vm = sum(2 * _nbytes((tk, o.shape[1]), o.dtype) for o in ops) + sum(3 * _nbytes(s, F32) for s in shapes)
    return _pallas(
        body,
        out_shape=tuple(_out(s, F32) for s in shapes),
        grid=(T // tk,),
        in_specs=[pl.BlockSpec((tk, o.shape[1]), lambda k: (k, 0)) for o in ops],
        out_specs=tuple(pl.BlockSpec(s, lambda k: (0, 0)) for s in shapes),
        compiler_params=_cparams(("arbitrary",), vm),
        name=name,
    )(*ops)


def _rmsnorm_fwd(h, g, name, tm=512):
    T = h.shape[0]

    def body(h_ref, g_ref, o_ref):
        x = h_ref[...]
        r = lax.rsqrt(jnp.mean(x * x, axis=-1, keepdims=True) + EPS)
        o_ref[...] = (x * r * g_ref[...]).astype(o_ref.dtype)

    return _pallas(
        body,
        out_shape=_out((T, D), BF16),
        grid=(T // tm,),
        in_specs=[pl.BlockSpec((tm, D), lambda i: (i, 0)), pl.BlockSpec((1, D), lambda i: (0, 0))],
        out_specs=pl.BlockSpec((tm, D), lambda i: (i, 0)),
        compiler_params=_cparams(("parallel",), 8 * _nbytes((tm, D), F32)),
        name=name,
    )(h, g)


def _rmsnorm_bwd(dxn, h, g, dres, name, tm=512):
    T = h.shape[0]

    def body(dxn_ref, h_ref, g_ref, dres_ref, dh_ref, dg_ref):
        @pl.when(pl.program_id(0) == 0)
        def _():
            dg_ref[...] = jnp.zeros_like(dg_ref)

        x = h_ref[...]
        dy = dxn_ref[...].astype(F32)
        r = lax.rsqrt(jnp.mean(x * x, axis=-1, keepdims=True) + EPS)
        gy = dy * g_ref[...]
        dot = jnp.mean(x * gy, axis=-1, keepdims=True)
        dh_ref[...] = dres_ref[...] + r * gy - x * (r * r * r * dot)
        dg_ref[...] += jnp.sum(dy * x * r, axis=0, keepdims=True)

    row = pl.BlockSpec((tm, D), lambda i: (i, 0))
    vec = pl.BlockSpec((1, D), lambda i: (0, 0))
    return _pallas(
        body,
        out_shape=(_out((T, D), F32), _out((1, D), F32)),
        grid=(T // tm,),
        in_specs=[row, row, vec, row],
        out_specs=(row, vec),
        compiler_params=_cparams(("arbitrary",), 12 * _nbytes((tm, D), F32)),
        name=name,
    )(dxn, h, g, dres)


def _final_loss(h, g, target, name, tm=512):
    T = h.shape[0]

    def body(h_ref, g_ref, t_ref, loss_ref, dh_ref, dg_ref):
        @pl.when(pl.program_id(0) == 0)
        def _():
            dg_ref[...] = jnp.zeros_like(dg_ref)
            loss_ref[...] = jnp.zeros_like(loss_ref)

        x = h_ref[...]
        gg = g_ref[...]
        r = lax.rsqrt(jnp.mean(x * x, axis=-1, keepdims=True) + EPS)
        y = x * r * gg
        e = y - t_ref[...]
        loss_ref[...] += 0.5 * jnp.sum(jnp.mean(e * e, axis=-1, keepdims=True), axis=0, keepdims=True)
        dy = e * (1.0 / D)
        gy = dy * gg
        dot = jnp.mean(x * gy, axis=-1, keepdims=True)
        dh_ref[...] = r * gy - x * (r * r * r * dot)
        dg_ref[...] += jnp.sum(dy * x * r, axis=0, keepdims=True)

    row = pl.BlockSpec((tm, D), lambda i: (i, 0))
    vec = pl.BlockSpec((1, D), lambda i: (0, 0))
    one = pl.BlockSpec((1, 1), lambda i: (0, 0))
    return _pallas(
        body,
        out_shape=(_out((1, 1), F32), _out((T, D), F32), _out((1, D), F32)),
        grid=(T // tm,),
        in_specs=[row, vec, row],
        out_specs=(one, row, vec),
        compiler_params=_cparams(("arbitrary",), 12 * _nbytes((tm, D), F32)),
        name=name,
    )(h, g, target)


GLA_G = 8


def _gla_consts():
    i = np.arange(KW)[:, None]
    j = np.arange(VW)[None, :]
    mask = ((i // GLA_DK) == (j // GLA_DV)) & (i < GLA_H * GLA_DK)
    a = np.arange(VW)
    hm = ((a[:, None] // GLA_DV) == (a[None, :] // GLA_DV)).astype(np.float32)
    c = np.arange(CHUNK)
    low = (c[:, None] >= c[None, :]).astype(np.float32)
    return jnp.asarray(mask.astype(np.float32)), jnp.asarray(hm, BF16), jnp.asarray(low, BF16)


def _split(x):
    hi = x.astype(BF16)
    return hi, (x - hi.astype(F32)).astype(BF16)


def _dot_sel(a, b, dims, split):
    if split == "a":
        hi, lo = _split(a)
        return (lax.dot_general(hi, b, dims, preferred_element_type=F32) + lax.dot_general(lo, b, dims, preferred_element_type=F32))
    hi, lo = _split(b)
    return (lax.dot_general(a, hi, dims, preferred_element_type=F32) + lax.dot_general(a, lo, dims, preferred_element_type=F32))


def _dot3(a, b, dims):
    ah, al = _split(a)
    bh, bl = _split(b)
    return (lax.dot_general(ah, bh, dims, preferred_element_type=F32) + lax.dot_general(al, bh, dims, preferred_element_type=F32)
            + lax.dot_general(ah, bl, dims, preferred_element_type=F32))


def _gla_chunk_common(p_ref, rows, wg, bg, low, ones_v):
    q = p_ref[rows, 0:KW]
    k = p_ref[rows, KW:2 * KW]
    v = p_ref[rows, 2 * KW:2 * KW + VW]
    g = p_ref[rows, 2 * KW + VW:2 * KW + 2 * VW]
    lr = p_ref[rows, 2 * KW + 2 * VW:GLA_COLS]
    z = _dot3(lr, wg, _DIMS["nn"]) + bg
    la = (jnp.minimum(z, 0.0) - jnp.log(1.0 + jnp.exp(-jnp.abs(z)))) * (1.0 / GLA_TAU)
    cum = _dot_sel(low, la, _DIMS["nn"], "b")
    endb = cum[CHUNK - 1:CHUNK, :]
    w = jnp.exp(endb - cum)
    a_full = jnp.exp(_dot_sel(la, ones_v, _DIMS["tn"], "a"))
    return q, k, v, g, lr, z, w, endb, a_full


def _gla_fwd(p, wg, bg, gn, consts, name):
    T = p.shape[0]
    rb = CHUNK * GLA_G
    ng = T // rb
    mask, hm, low = consts[:3]
    scale = GLA_DK ** -0.5

    def body(p_ref, wg_ref, bg_ref, gn_ref, m_ref, hm_ref, l_ref, o_ref, st_ref, s_ref):
        @pl.when(pl.program_id(0) == 0)
        def _():
            s_ref[...] = jnp.zeros_like(s_ref)

        wg_v, bg_v, gn_v = wg_ref[...], bg_ref[...], gn_ref[...]
        ones_v = jnp.ones((CHUNK, VW), BF16)
        for c in range(GLA_G):
            rows = slice(c * CHUNK, (c + 1) * CHUNK)
            q, k, v, g, _, _, w, _, a_full = _gla_chunk_common(p_ref, rows, wg_v, bg_v, l_ref[...], ones_v)
            kd = (k * w).astype(BF16)
            kv = lax.dot_general(kd, v.astype(BF16), _DIMS["tn"], preferred_element_type=F32) * m_ref[...]
            s_new = a_full * s_ref[...] + kv
            s_ref[...] = s_new
            st_ref[c] = s_new
            o = jnp.dot((q * scale).astype(BF16), s_new.astype(BF16), preferred_element_type=F32)
            ms = _dot_sel(o * o, hm_ref[...], _DIMS["nn"], "a") * (1.0 / GLA_DV)
            r = lax.rsqrt(ms + EPS)
            y = o * r * gn_v * (g * _sigmoid(g))
            o_ref[rows, :] = y.astype(o_ref.dtype)

    full = lambda shape: pl.BlockSpec(shape, lambda i: tuple(0 for _ in shape))
    vm = 2 * _nbytes((rb, GLA_COLS), F32) + 2 * _nbytes((GLA_G, KW, VW), F32) + 12 * _nbytes((KW, VW), F32)
    return _pallas(
        body,
        out_shape=(_out((T, VW), BF16), _out((T // CHUNK, KW, VW), F32)),
        grid=(ng,),
        in_specs=[pl.BlockSpec((rb, GLA_COLS), lambda i: (i, 0)), full((LRW, KW)), full((1, KW)), full((1, VW)),
                  full((KW, VW)), full((VW, VW)), full((CHUNK, CHUNK))],
        out_specs=(pl.BlockSpec((rb, VW), lambda i: (i, 0)), pl.BlockSpec((GLA_G, KW, VW), lambda i: (i, 0, 0))),
        scratch_shapes=[pltpu.VMEM((KW, VW), F32)],
        compiler_params=_cparams(("arbitrary",), vm),
        name=name,
    )(p, wg, bg, gn, mask, hm, low)


def _gla_bwd(p, dy, states, wg, bg, gn, consts, name):
    T = p.shape[0]
    rb = CHUNK * GLA_G
    ng = T // rb
    mask, hm, low = consts[:3]
    scale = GLA_DK ** -0.5

    def body(p_ref, dy_ref, st_ref, sp_ref, wg_ref, bg_ref, gn_ref, m_ref, hm_ref, l_ref,
             dp_ref, dwg_ref, dbg_ref, dgn_ref, ga_ref):
        step = pl.program_id(0)

        @pl.when(step == 0)
        def _():
            ga_ref[...] = jnp.zeros_like(ga_ref)
            dwg_ref[...] = jnp.zeros_like(dwg_ref)
            dbg_ref[...] = jnp.zeros_like(dbg_ref)
            dgn_ref[...] = jnp.zeros_like(dgn_ref)

        first_group = step == ng - 1
        wg_v, bg_v, gn_v = wg_ref[...], bg_ref[...], gn_ref[...]
        ones_v = jnp.ones((CHUNK, VW), BF16)
        ones_8 = jnp.ones((8, VW), BF16)
        for c in reversed(range(GLA_G)):
            rows = slice(c * CHUNK, (c + 1) * CHUNK)
            q, k, v, g, lr, z, w, endb, a_full = _gla_chunk_common(p_ref, rows, wg_v, bg_v, l_ref[...], ones_v)
            s_n = st_ref[c]
            if c > 0:
                s_prev = st_ref[c - 1]
            else:
                s_prev = jnp.where(first_group, 0.0, sp_ref[0])
            kd = k * w
            qs = (q * scale).astype(BF16)
            o = jnp.dot(qs, s_n.astype(BF16), preferred_element_type=F32)
            ms = _dot_sel(o * o, hm_ref[...], _DIMS["nn"], "a") * (1.0 / GLA_DV)
            r = lax.rsqrt(ms + EPS)
            on = o * r
            sg = _sigmoid(g)
            silu = g * sg
            dyv = dy_ref[rows, :].astype(F32)
            d_on = dyv * gn_v * silu
            dg = dyv * on * gn_v * (sg * (1.0 + g * (1.0 - sg)))
            dgn_ref[...] += jnp.sum(dyv * on * silu, axis=0, keepdims=True)
            mo = _dot_sel(o * d_on, hm_ref[...], _DIMS["nn"], "a") * (1.0 / GLA_DV)
            do = r * d_on - o * (r * r * r) * mo
            dob = do.astype(BF16)
            dq = lax.dot_general(dob, s_n.astype(BF16), _DIMS["nt"], preferred_element_type=F32) * scale
            g_n = lax.dot_general(qs, dob, _DIMS["tn"], preferred_element_type=F32) * m_ref[...] + ga_ref[...]
            d_a = _dot_sel(ones_8, g_n * s_prev, _DIMS["nt"], "b")[0:1, :]
            g_nb = g_n.astype(BF16)
            dkd = lax.dot_general(v.astype(BF16), g_nb, _DIMS["nt"], preferred_element_type=F32)
            dv = jnp.dot(kd.astype(BF16), g_nb, preferred_element_type=F32)
            dk = dkd * w
            e = dkd * kd
            d_end = jnp.sum(e, axis=0, keepdims=True) + d_a * jnp.exp(endb)
            dla = _dot_sel(l_ref[...], -e, _DIMS["tn"], "b") + d_end
            dz = dla * (1.0 - _sigmoid(z)) * (1.0 / GLA_TAU)
            dlr = _dot3(dz, wg_v, _DIMS["nt"])
            dwg_ref[...] += _dot3(lr, dz, _DIMS["tn"])
            dbg_ref[...] += jnp.sum(dz, axis=0, keepdims=True)
            ga_ref[...] = a_full * g_n
            dp_ref[rows, 0:KW] = dq.astype(dp_ref.dtype)
            dp_ref[rows, KW:2 * KW] = dk.astype(dp_ref.dtype)
            dp_ref[rows, 2 * KW:2 * KW + VW] = dv.astype(dp_ref.dtype)
            dp_ref[rows, 2 * KW + VW:2 * KW + 2 * VW] = dg.astype(dp_ref.dtype)
            dp_ref[rows, 2 * KW + 2 * VW:GLA_COLS] = dlr.astype(dp_ref.dtype)

    full = lambda shape: pl.BlockSpec(shape, lambda i: tuple(0 for _ in shape))
    rev = lambda i: (ng - 1 - i, 0)
    vm = 4 * _nbytes((rb, GLA_COLS), F32) + 2 * _nbytes((rb, VW), F32) + 2 * _nbytes((GLA_G + 1, KW, VW), F32)
    vm += 16 * _nbytes((KW, VW), F32)
    return _pallas(
        body,
        out_shape=(_out((T, GLA_COLS), BF16), _out((LRW, KW), F32),
                   _out((1, KW), F32), _out((1, VW), F32)),
        grid=(ng,),
        in_specs=[pl.BlockSpec((rb, GLA_COLS), rev), pl.BlockSpec((rb, VW), rev),
                  pl.BlockSpec((GLA_G, KW, VW), lambda i: (ng - 1 - i, 0, 0)),
                  pl.BlockSpec((1, KW, VW), lambda i: (jnp.maximum((ng - 1 - i) * GLA_G - 1, 0), 0, 0)),
                  full((LRW, KW)), full((1, KW)), full((1, VW)), full((KW, VW)), full((VW, VW)), full((CHUNK, CHUNK))],
        out_specs=(pl.BlockSpec((rb, GLA_COLS), rev), full((LRW, KW)), full((1, KW)), full((1, VW))),
        scratch_shapes=[pltpu.VMEM((KW, VW), F32)],
        compiler_params=_cparams(("arbitrary",), vm),
        name=name,
    )(p, dy, states, states, wg, bg, gn, mask, hm, low)


CONV_TM = 512
HALO = 32
CONV_RB = 64


def _glu(u):
    a = u[:, 0:CW]
    b = u[:, CW:2 * CW]
    return a * _sigmoid(b)


def _conv_taps(buf_ref, w_ref, rb0, first_tap):
    acc = jnp.zeros((CONV_RB, CW), F32)
    for j in range(CK):
        s = rb0 + first_tap(j)
        acc = acc + w_ref[j:j + 1, :] * buf_ref[s:s + CONV_RB, :]
    return acc


def _ln_fwd(c, lg, lb):
    mu = jnp.mean(c, axis=-1, keepdims=True)
    xc = c - mu
    rstd = lax.rsqrt(jnp.mean(xc * xc, axis=-1, keepdims=True) + EPS)
    n = xc * rstd
    return n, rstd, n * lg + lb


def _conv_fwd(u, w, b, lg, lb, name):
    T = u.shape[0]
    tm = CONV_TM

    def body(u_ref, uh_ref, w_ref, b_ref, lg_ref, lb_ref, o_ref, hbuf):
        i = pl.program_id(0)
        hbuf[0:HALO, :] = jnp.where(i > 0, _glu(uh_ref[...]), 0.0)
        hbuf[HALO:HALO + tm, :] = _glu(u_ref[...])
        for r in range(tm // CONV_RB):
            acc = _conv_taps(hbuf, w_ref, r * CONV_RB, lambda j: HALO - (CK - 1) + j)
            _, _, zz = _ln_fwd(acc + b_ref[...], lg_ref[...], lb_ref[...])
            o_ref[r * CONV_RB:(r + 1) * CONV_RB, :] = (zz * _sigmoid(zz)).astype(o_ref.dtype)

    vec = pl.BlockSpec((1, CW), lambda i: (0, 0))
    return _pallas(
        body,
        out_shape=_out((T, CW), BF16),
        grid=(T // tm,),
        in_specs=[pl.BlockSpec((tm, CONV_COLS), lambda i: (i, 0)),
                  pl.BlockSpec((HALO, CONV_COLS), lambda i: (jnp.maximum(i * (tm // HALO) - 1, 0), 0)),
                  pl.BlockSpec((HALO, CW), lambda i: (0, 0)), vec, vec, vec],
        out_specs=pl.BlockSpec((tm, CW), lambda i: (i, 0)),
        scratch_shapes=[pltpu.VMEM((tm + HALO, CW), F32)],
        compiler_params=_cparams(("arbitrary",), 8 * _nbytes((tm, CONV_COLS), F32)),
        name=name,
    )(u, u, w, b, lg, lb)


def _conv_bwd_dc(u, dout, w, b, lg, lb, name):
    T = u.shape[0]
    tm = CONV_TM
    nsteps = T // tm

    def body(u_ref, uh_ref, do_ref, w_ref, b_ref, lg_ref, lb_ref, dc_ref, dw_ref, db_ref, dlg_ref, dlb_ref, hbuf, dwacc):
        i = pl.program_id(0)

        @pl.when(i == 0)
        def _():
            dwacc[...] = jnp.zeros_like(dwacc)
            db_ref[...] = jnp.zeros_like(db_ref)
            dlg_ref[...] = jnp.zeros_like(dlg_ref)
            dlb_ref[...] = jnp.zeros_like(dlb_ref)

        hbuf[0:HALO, :] = jnp.where(i > 0, _glu(uh_ref[...]), 0.0)
        hbuf[HALO:HALO + tm, :] = _glu(u_ref[...])
        for r in range(tm // CONV_RB):
            rows = slice(r * CONV_RB, (r + 1) * CONV_RB)
            acc = _conv_taps(hbuf, w_ref, r * CONV_RB, lambda j: HALO - (CK - 1) + j)
            n, rstd, zz = _ln_fwd(acc + b_ref[...], lg_ref[...], lb_ref[...])
            sg = _sigmoid(zz)
            dz = do_ref[rows, :].astype(F32) * (sg * (1.0 + zz * (1.0 - sg)))
            dlg_ref[...] += jnp.sum(dz * n, axis=0, keepdims=True)
            dlb_ref[...] += jnp.sum(dz, axis=0, keepdims=True)
            dn = dz * lg_ref[...]
            dc = rstd * (dn - jnp.mean(dn, axis=-1, keepdims=True) - n * jnp.mean(dn * n, axis=-1, keepdims=True))
            dc_ref[rows, :] = dc
            db_ref[...] += jnp.sum(dc, axis=0, keepdims=True)
            for j in range(CK):
                s = r * CONV_RB + HALO - (CK - 1) + j
                prod = dc * hbuf[s:s + CONV_RB, :]
                dwacc[j] += jnp.sum(prod.reshape(CONV_RB // 8, 8, CW), axis=0)

        @pl.when(i == nsteps - 1)
        def _():
            dw_ref[...] = jnp.sum(dwacc[...], axis=1)

    vec = pl.BlockSpec((1, CW), lambda i: (0, 0))
    return _pallas(
        body,
        out_shape=(_out((T, CW), F32), _out((HALO, CW), F32),
                   _out((1, CW), F32), _out((1, CW), F32), _out((1, CW), F32)),
        grid=(nsteps,),
        in_specs=[pl.BlockSpec((tm, CONV_COLS), lambda i: (i, 0)),
                  pl.BlockSpec((HALO, CONV_COLS), lambda i: (jnp.maximum(i * (tm // HALO) - 1, 0), 0)),
                  pl.BlockSpec((tm, CW), lambda i: (i, 0)),
                  pl.BlockSpec((HALO, CW), lambda i: (0, 0)), vec, vec, vec],
        out_specs=(pl.BlockSpec((tm, CW), lambda i: (i, 0)), pl.BlockSpec((HALO, CW), lambda i: (0, 0)), vec, vec, vec),
        scratch_shapes=[pltpu.VMEM((tm + HALO, CW), F32), pltpu.VMEM((HALO, 8, CW), F32)],
        compiler_params=_cparams(("arbitrary",), 10 * _nbytes((tm, CONV_COLS), F32)),
        name=name,
    )(u, u, dout, w, b, lg, lb)


def _conv_bwd_du(u, dc, w, name):
    T = u.shape[0]
    tm = CONV_TM
    nsteps = T // tm

    def body(u_ref, dc_ref, dch_ref, w_ref, du_ref, dcbuf):
        i = pl.program_id(0)
        dcbuf[0:tm, :] = dc_ref[...]
        dcbuf[tm:tm + HALO, :] = jnp.where(i < nsteps - 1, dch_ref[...], 0.0)
        for r in range(tm // CONV_RB):
            rows = slice(r * CONV_RB, (r + 1) * CONV_RB)
            dh = _conv_taps(dcbuf, w_ref, r * CONV_RB, lambda j: (CK - 1) - j)
            a = u_ref[rows, 0:CW]
            sb = _sigmoid(u_ref[rows, CW:2 * CW])
            du_ref[rows, 0:CW] = (dh * sb).astype(du_ref.dtype)
            du_ref[rows, CW:2 * CW] = (dh * a * sb * (1.0 - sb)).astype(du_ref.dtype)

    return _pallas(
        body,
        out_shape=_out((T, CONV_COLS), BF16),
        grid=(nsteps,),
        in_specs=[pl.BlockSpec((tm, CONV_COLS), lambda i: (i, 0)),
                  pl.BlockSpec((tm, CW), lambda i: (i, 0)),
                  pl.BlockSpec((HALO, CW), lambda i: (jnp.minimum((i + 1) * (tm // HALO), T // HALO - 1), 0)),
                  pl.BlockSpec((HALO, CW), lambda i: (0, 0))],
        out_specs=pl.BlockSpec((tm, CONV_COLS), lambda i: (i, 0)),
        scratch_shapes=[pltpu.VMEM((tm + HALO, CW), F32)],
        compiler_params=_cparams(("arbitrary",), 8 * _nbytes((tm, CONV_COLS), F32)),
        name=name,
    )(u, dc, dc, w)


ATT_G = 4


def _att_load_kv(p_any, kbuf, vbuf, sems, T):
    kc = pltpu.make_async_copy(p_any.at[:, pl.ds(AW, AW)], kbuf.at[pl.ds(LEFT, T), :], sems.at[0])
    vc = pltpu.make_async_copy(p_any.at[:, pl.ds(2 * AW, AW)], vbuf.at[pl.ds(LEFT, T), :], sems.at[1])
    kc.start()
    vc.start()
    kbuf[0:LEFT, :] = jnp.zeros((LEFT, AW), BF16)
    vbuf[0:LEFT, :] = jnp.zeros((LEFT, AW), BF16)
    kc.wait()
    vc.wait()


ATT_QB = CHUNK * ATT_G
ATT_KB = LEFT + ATT_QB
REL_PAD = 384
TOEP = 1024


def _att_consts():
    m = np.arange(TOEP)
    d = ATT_KB - 1 - m
    idx = np.clip(d, -128, 128) + 128
    sel = (np.arange(REL_PAD)[:, None] == idx[None, :]) & (m[None, :] < ATT_QB + ATT_KB - 1)
    return jnp.asarray(sel.astype(np.float32))


def _att_build_bias(rel_ref, sel_ref, bias_scr):
    tr = jnp.dot(rel_ref[...], sel_ref[...], precision=HI, preferred_element_type=F32)
    qc = lax.broadcasted_iota(jnp.int32, (ATT_QB, ATT_KB), 0) // CHUNK
    kc = lax.broadcasted_iota(jnp.int32, (ATT_QB, ATT_KB), 1) // CHUNK
    band = (kc >= qc) & (kc <= qc + 8)
    for h in range(AH):
        rows = jnp.broadcast_to(tr[h:h + 1, :], (ATT_QB, TOEP))
        toep = pltpu.roll(rows, TOEP - (ATT_QB - 1), 1, stride=1, stride_axis=0)[:, 0:ATT_KB]
        bias_scr[h // 2, (h % 2) * ATT_QB:(h % 2 + 1) * ATT_QB, :] = jnp.where(band, toep, NEG)


def _att_probs(qst, kb, bias_p, n0):
    sc = lax.dot_general(qst, kb, _DIMS["nt"], preferred_element_type=F32) * (64 ** -0.5) + bias_p
    pos = lax.broadcasted_iota(jnp.int32, (2 * ATT_QB, ATT_KB), 1)
    sc = jnp.where(pos >= CHUNK * (8 - n0), sc, NEG)
    mx = jnp.max(sc, axis=-1, keepdims=True)
    ex = jnp.exp(sc - mx)
    return ex / jnp.sum(ex, axis=-1, keepdims=True)


def _head_stack(a2, lo):
    zero = jnp.zeros_like(a2)
    return jnp.concatenate([jnp.where(lo, a2, zero), jnp.where(lo, zero, a2)], axis=0)


def _att_fwd(p, rel, sel, name):
    T = p.shape[0]

    def body(q_ref, p_any, rel_ref, sel_ref, o_ref, kbuf, vbuf, bias_scr, sems):
        i = pl.program_id(0)

        @pl.when(i == 0)
        def _():
            _att_load_kv(p_any, kbuf, vbuf, sems, T)
            _att_build_bias(rel_ref, sel_ref, bias_scr)

        lo = lax.broadcasted_iota(jnp.int32, (ATT_QB, 128), 1) < 64
        n0 = i * ATT_G
        start = pl.multiple_of(i * ATT_QB, ATT_QB)
        for hp in range(AH // 2):
            cols = slice(hp * 128, (hp + 1) * 128)
            kb = kbuf[pl.ds(start, ATT_KB), cols]
            vb = vbuf[pl.ds(start, ATT_KB), cols]
            pr = _att_probs(_head_stack(q_ref[:, cols], lo), kb, bias_scr[hp], n0)
            pv = jnp.dot(pr.astype(BF16), vb, preferred_element_type=F32)
            o_ref[:, cols] = jnp.where(lo, pv[0:ATT_QB], pv[ATT_QB:2 * ATT_QB]).astype(o_ref.dtype)

    vm = 2 * _nbytes((T + LEFT, AW), BF16) + 8 * _nbytes((2 * ATT_QB, ATT_KB), F32) + (8 << 20)
    return _pallas(
        body,
        out_shape=_out((T, AW), BF16),
        grid=(T // ATT_QB,),
        in_specs=[pl.BlockSpec((ATT_QB, AW), lambda i: (i, 0)), pl.BlockSpec(memory_space=pl.ANY),
                  pl.BlockSpec((8, REL_PAD), lambda i: (0, 0)), pl.BlockSpec((REL_PAD, TOEP), lambda i: (0, 0))],
        out_specs=pl.BlockSpec((ATT_QB, AW), lambda i: (i, 0)),
        scratch_shapes=[pltpu.VMEM((T + LEFT, AW), BF16), pltpu.VMEM((T + LEFT, AW), BF16),
                        pltpu.VMEM((AH // 2, 2 * ATT_QB, ATT_KB), F32), pltpu.SemaphoreType.DMA((2,))],
        compiler_params=_cparams(("arbitrary",), vm),
        name=name,
    )(p, p, rel, sel)


def _att_bwd(p, do, rel, sel, name):
    T = p.shape[0]
    nsteps = T // ATT_QB

    def body(q_ref, p_any, do_ref, rel_ref, sel_ref, dp_any, drel_ref,
             kbuf, vbuf, dqbuf, dkbuf, dvbuf, bias_scr, dbias_scr, dtr_scr, sems):
        i = pl.program_id(0)

        @pl.when(i == 0)
        def _():
            _att_load_kv(p_any, kbuf, vbuf, sems, T)
            _att_build_bias(rel_ref, sel_ref, bias_scr)
            dkbuf[...] = jnp.zeros_like(dkbuf)
            dvbuf[...] = jnp.zeros_like(dvbuf)
            dbias_scr[...] = jnp.zeros_like(dbias_scr)

        lo = lax.broadcasted_iota(jnp.int32, (ATT_QB, 128), 1) < 64
        n0 = i * ATT_G
        start = pl.multiple_of(i * ATT_QB, ATT_QB)
        for hp in range(AH // 2):
            cols = slice(hp * 128, (hp + 1) * 128)
            kb = kbuf[pl.ds(start, ATT_KB), cols]
            vb = vbuf[pl.ds(start, ATT_KB), cols]
            qst = _head_stack(q_ref[:, cols], lo)
            dost = _head_stack(do_ref[:, cols].astype(BF16), lo)
            pr = _att_probs(qst, kb, bias_scr[hp], n0)
            dpr = lax.dot_general(dost, vb, _DIMS["nt"], preferred_element_type=F32)
            ds = pr * (dpr - jnp.sum(dpr * pr, axis=-1, keepdims=True))
            dbias_scr[hp] += ds
            dsb = (ds * (64 ** -0.5)).astype(BF16)
            dq = jnp.dot(dsb, kb, preferred_element_type=F32)
            dqbuf[pl.ds(start, ATT_QB), cols] = jnp.where(lo, dq[0:ATT_QB], dq[ATT_QB:2 * ATT_QB]).astype(BF16)
            dkbuf[pl.ds(start, ATT_KB), cols] += lax.dot_general(dsb, qst, _DIMS["tn"], preferred_element_type=F32)
            dvbuf[pl.ds(start, ATT_KB), cols] += lax.dot_general(pr.astype(BF16), dost, _DIMS["tn"], preferred_element_type=F32)

        @pl.when(i == nsteps - 1)
        def _():
            kbuf[pl.ds(LEFT, T), :] = dkbuf[pl.ds(LEFT, T), :].astype(BF16)
            vbuf[pl.ds(LEFT, T), :] = dvbuf[pl.ds(LEFT, T), :].astype(BF16)
            cps = [pltpu.make_async_copy(dqbuf, dp_any.at[:, pl.ds(0, AW)], sems.at[0]),
                   pltpu.make_async_copy(kbuf.at[pl.ds(LEFT, T), :], dp_any.at[:, pl.ds(AW, AW)], sems.at[1]),
                   pltpu.make_async_copy(vbuf.at[pl.ds(LEFT, T), :], dp_any.at[:, pl.ds(2 * AW, AW)], sems.at[2])]
            for cp in cps:
                cp.start()
            dtr_scr[...] = jnp.zeros_like(dtr_scr)
            ri = lax.broadcasted_iota(jnp.int32, (ATT_QB, ATT_QB), 0)
            ci = lax.broadcasted_iota(jnp.int32, (ATT_QB, ATT_QB), 1)
            flip = jnp.where(ri + ci == ATT_QB - 1, 1.0, 0.0)
            for h in range(AH):
                db = dbias_scr[h // 2, (h % 2) * ATT_QB:(h % 2 + 1) * ATT_QB, :]
                db = jnp.dot(flip, db, precision=HI, preferred_element_type=F32)
                wide = jnp.concatenate([db, jnp.zeros((ATT_QB, TOEP - ATT_KB), F32)], axis=1)
                diag = pltpu.roll(wide, 0, 1, stride=1, stride_axis=0)
                dtr_scr[h:h + 1, :] = jnp.sum(diag, axis=0, keepdims=True)
            drel_ref[...] = lax.dot_general(dtr_scr[...], sel_ref[...], _DIMS["nt"], precision=HI, preferred_element_type=F32)
            for cp in cps:
                cp.wait()

    vm = 3 * _nbytes((T + LEFT, AW), BF16) + 2 * _nbytes((T + LEFT, AW), F32) + 12 * _nbytes((2 * ATT_QB, ATT_KB), F32) + (8 << 20)
    return _pallas(
        body,
        out_shape=(_out((T, ATT_COLS), BF16), _out((8, REL_PAD), F32)),
        grid=(nsteps,),
        in_specs=[pl.BlockSpec((ATT_QB, AW), lambda i: (i, 0)), pl.BlockSpec(memory_space=pl.ANY),
                  pl.BlockSpec((ATT_QB, AW), lambda i: (i, 0)),
                  pl.BlockSpec((8, REL_PAD), lambda i: (0, 0)), pl.BlockSpec((REL_PAD, TOEP), lambda i: (0, 0))],
        out_specs=(pl.BlockSpec(memory_space=pl.ANY), pl.BlockSpec((8, REL_PAD), lambda i: (0, 0))),
        scratch_shapes=[pltpu.VMEM((T + LEFT, AW), BF16), pltpu.VMEM((T + LEFT, AW), BF16), pltpu.VMEM((T, AW), BF16),
                        pltpu.VMEM((T + LEFT, AW), F32), pltpu.VMEM((T + LEFT, AW), F32),
                        pltpu.VMEM((AH // 2, 2 * ATT_QB, ATT_KB), F32), pltpu.VMEM((AH // 2, 2 * ATT_QB, ATT_KB), F32),
                        pltpu.VMEM((8, TOEP), F32), pltpu.SemaphoreType.DMA((3,))],
        compiler_params=_cparams(("arbitrary",), vm),
        name=name,
    )(p, p, do, rel, sel)


def _layer_fwd(h, wl, consts, tag):
    xn = _rmsnorm_fwd(h, wl["norm_mix"], f"{tag}_norm_mix")
    p_gla, p_conv, p_att = _mm_fan(xn, [wl["w_gla"], wl["w_conv"], wl["w_att"]], mode="nn", out_dtypes=(F32, F32, BF16),
                                   name=f"{tag}_proj")
    o_gla, states = _gla_fwd(p_gla, wl["wg"], wl["bg"], wl["gla_norm"], consts, f"{tag}_gla_fwd")
    o_conv = _conv_fwd(p_conv, wl["w_dw"], wl["b_dw"], wl["ln_g"], wl["ln_b"], f"{tag}_conv_fwd")
    rel = jnp.pad(wl["rel_bias"], ((0, 8 - AH), (0, REL_PAD - N_REL)))
    o_att = _att_fwd(p_att, rel, consts[3], f"{tag}_att_fwd")
    h1 = _mm_sum([o_gla, o_conv, o_att], [wl["w_out_g"], wl["w_out_c"], wl["w_out_a"]], mode="nn", out_dtype=F32, extra=h,
                 name=f"{tag}_out")
    xn2 = _rmsnorm_fwd(h1, wl["norm_ffn"], f"{tag}_norm_ffn")
    u = _mm(xn2, wl["w_up"], mode="nn", out_dtype=BF16, tn=1024, b_chips=True, name=f"{tag}_mlp_up")
    h2 = _mm(u, wl["w_down"], mode="nn", out_dtype=F32, tk=1024, a_pro="relu2", epi="add", extra=h1, name=f"{tag}_mlp_down")
    saved = dict(h=h, xn=xn, p_gla=p_gla, p_conv=p_conv, p_att=p_att, states=states, o_gla=o_gla, o_conv=o_conv,
                 o_att=o_att, rel=rel, h1=h1, xn2=xn2, u=u)
    return h2, saved


def _layer_bwd(dh2, sv, wl, consts, tag, emit=lambda name, grad: None):
    g = {}
    du = _mm(dh2, wl["w_down"], mode="nt", out_dtype=BF16, tn=1024, epi="relu2grad", extra=sv["u"], name=f"{tag}_mlp_down_dx")
    g["w_down"] = _mm(sv["u"], dh2, mode="tn", out_dtype=F32, tm=1024, tn=512, tk=512, a_pro="relu2", name=f"{tag}_mlp_down_dw")
    emit("w_down", g["w_down"].reshape(4, D_FF // 4, D))
    dxn2 = _mm(du, wl["w_up"], mode="nt", out_dtype=F32, tk=1024, b_chips=True, name=f"{tag}_mlp_up_dx")
    g["w_up"] = _mm(sv["xn2"], du, mode="tn", out_dtype=F32, tm=1024, tn=512, tk=512, out_chips=True, name=f"{tag}_mlp_up_dw")
    emit("w_up", g["w_up"])
    dh1, g["norm_ffn"] = _rmsnorm_bwd(dxn2, sv["h1"], wl["norm_ffn"], dh2, f"{tag}_norm_ffn_bwd")
    d_gla, d_conv, d_att = _mm_fan(dh1, [wl["w_out_g"], wl["w_out_c"], wl["w_out_a"]], mode="nt", out_dtypes=(F32, F32, F32),
                                   name=f"{tag}_out_dx")
    g["w_out_g"], g["w_out_c"], g["w_out_a"] = _mm_tn_multi([sv["o_gla"], sv["o_conv"], sv["o_att"], dh1],
                                                            [(0, 3), (1, 3), (2, 3)], name=f"{tag}_out_dw")
    emit("w_out", jnp.concatenate([g["w_out_g"], g["w_out_c"], g["w_out_a"]], axis=0).reshape(4, D // 4, D))
    dp_gla, g["wg"], g["bg"], g["gla_norm"] = _gla_bwd(sv["p_gla"], d_gla, sv["states"], wl["wg"], wl["bg"], wl["gla_norm"],
                                                       consts, f"{tag}_gla_bwd")
    dc, g["w_dw"], g["b_dw"], g["ln_g"], g["ln_b"] = _conv_bwd_dc(sv["p_conv"], d_conv, wl["w_dw"], wl["b_dw"], wl["ln_g"],
                                                                  wl["ln_b"], f"{tag}_conv_bwd_dc")
    dp_conv = _conv_bwd_du(sv["p_conv"], dc, wl["w_dw"], f"{tag}_conv_bwd_du")
    dp_att, drel = _att_bwd(sv["p_att"], d_att, sv["rel"], consts[3], f"{tag}_att_bwd")
    g["rel_bias"] = drel[0:AH, 0:N_REL]
    g["w_gla"], g["w_conv"], g["w_att"] = _mm_tn_multi([sv["xn"], dp_gla, dp_conv, dp_att], [(0, 1), (0, 2), (0, 3)],
                                                       name=f"{tag}_proj_dw")
    emit("w_in", jnp.transpose(_join_w_in(g).reshape(D, 4, -1), (1, 0, 2)))
    dxn = _mm_sum([dp_gla, dp_conv, dp_att], [wl["w_gla"], wl["w_conv"], wl["w_att"]], mode="nt", out_dtype=F32,
                  name=f"{tag}_proj_dx")
    dh, g["norm_mix"] = _rmsnorm_bwd(dxn, sv["h"], wl["norm_mix"], dh1, f"{tag}_norm_mix_bwd")
    return dh, g


def _local_step(x, target, layers, norm_final, emit=lambda layer, name, grad: None):
    consts = _gla_consts() + (_att_consts(),)
    h = x
    saved = []
    for l, wl in enumerate(layers):
        h, sv = _layer_fwd(h, wl, consts, f"l{l}")
        saved.append(sv)
    loss, dh, g_final = _final_loss(h, norm_final, target, "final_loss")
    grads = [None] * len(layers)
    for l in reversed(range(len(layers))):
        dh, grads[l] = _layer_bwd(dh, saved[l], layers[l], consts, f"l{l}", functools.partial(emit, l))
    return loss, dh, grads, g_final


ANY = pl.BlockSpec(memory_space=pl.ANY)


def _place():
    x, y, c = lax.axis_index("x"), lax.axis_index("y"), lax.axis_index("c")
    chips = [(1 - x, y), (x, 1 - y), (1 - x, 1 - y)]
    return x, y, c, chips


def _shape(shape, dtype):
    return jax.ShapeDtypeStruct(tuple(shape), dtype)


def _remote(src, dst, send_sem, recv_sem, to):
    return pltpu.make_async_remote_copy(src_ref=src, dst_ref=dst, send_sem=send_sem, recv_sem=recv_sem,
                                        device_id=to, device_id_type=MESH)


def _phase_gather_ici(src, then):
    R, C = src.shape
    rh = R // 2

    def copies(ins, outs, sems):
        x, y, c, chips = _place()
        me = 2 * x + y
        local = pltpu.make_async_copy(ins[0], outs[0].at[me], sems[2].at[0])
        sends = [_remote(ins[0].at[pl.ds(c * rh, rh), :], outs[0].at[me, pl.ds(c * rh, rh), :], sems[0].at[k], sems[1].at[k],
                         (px, py, c)) for k, (px, py) in enumerate(chips)]
        recvs = [_remote(outs[0].at[2 * px + py, pl.ds(c * rh, rh), :], outs[0].at[2 * px + py, pl.ds(c * rh, rh), :],
                         sems[0].at[k], sems[1].at[k], (px, py, c)) for k, (px, py) in enumerate(chips)]
        return local, sends, recvs

    def start(ins, outs, sems):
        local, sends, _ = copies(ins, outs, sems)
        local.start()
        for cp in sends:
            cp.start()

    def finish(ins, outs, sems):
        local, sends, recvs = copies(ins, outs, sems)
        for cp in recvs:
            cp.wait_recv()
        for cp in sends:
            cp.wait_send()
        local.wait()

    return _Comm([src], [_shape((4, R, C), src.dtype)], {}, [(3,), (3,), (1,)], start, finish, then)


def _phase_gather_d2d(part, then):
    _, R, C = part.shape
    rh = R // 2

    def copies(ins, outs, sems):
        x, y, c, chips = _place()
        sends = [_remote(ins[0].at[2 * px + py, pl.ds(c * rh, rh), :], outs[0].at[2 * px + py, pl.ds(c * rh, rh), :],
                         sems[0].at[k], sems[1].at[k], (x, y, 1 - c)) for k, (px, py) in enumerate(chips)]
        recvs = [_remote(outs[0].at[2 * px + py, pl.ds((1 - c) * rh, rh), :], outs[0].at[2 * px + py, pl.ds((1 - c) * rh, rh), :],
                         sems[0].at[k], sems[1].at[k], (x, y, 1 - c)) for k, (px, py) in enumerate(chips)]
        return sends, recvs

    def start(ins, outs, sems):
        for cp in copies(ins, outs, sems)[0]:
            cp.start()

    def finish(ins, outs, sems):
        sends, recvs = copies(ins, outs, sems)
        for cp in recvs:
            cp.wait_recv()
        for cp in sends:
            cp.wait_send()

    return _Comm([part], [_shape(part.shape, part.dtype)], {0: 0}, [(3,), (3,)], start, finish, then)


def _phase_pair_exchange(full, then):
    _, R, C = full.shape
    rh = R // 2

    def copy(ins, outs, sems):
        x, y, c, _ = _place()
        return _remote(ins[0].at[:, pl.ds((1 - c) * rh, rh), :], outs[0], sems[0].at[0], sems[1].at[0], (x, y, 1 - c))

    return _Comm([full], [_shape((4, rh, C), full.dtype)], {}, [(1,), (1,)],
                 lambda ins, outs, sems: copy(ins, outs, sems).start(),
                 lambda ins, outs, sems: copy(ins, outs, sems).wait(), then)


def _phase_chip_scatter(parts, then):
    def copies(ins, outs, sems):
        x, y, c, chips = _place()
        me = 2 * x + y
        local = pltpu.make_async_copy(ins[0].at[me], outs[0].at[me], sems[2].at[0])
        sends = [_remote(ins[0].at[2 * px + py], outs[0].at[me], sems[0].at[k], sems[1].at[k], (px, py, c))
                 for k, (px, py) in enumerate(chips)]
        recvs = [_remote(outs[0].at[2 * px + py], outs[0].at[2 * px + py], sems[0].at[k], sems[1].at[k], (px, py, c))
                 for k, (px, py) in enumerate(chips)]
        return local, sends, recvs

    def start(ins, outs, sems):
        local, sends, _ = copies(ins, outs, sems)
        local.start()
        for cp in sends:
            cp.start()

    def finish(ins, outs, sems):
        local, sends, recvs = copies(ins, outs, sems)
        for cp in recvs:
            cp.wait_recv()
        for cp in sends:
            cp.wait_send()
        local.wait()

    return _Comm([parts], [_shape(parts.shape, parts.dtype)], {}, [(3,), (3,), (1,)], start, finish, then)


def _phase_pair_allgather(half, layer, depth, into, then):
    rh, C = half.shape

    def copies(ins, outs, sems):
        x, y, c, _ = _place()
        mine = outs[0].at[layer, pl.ds(c * rh, rh), :]
        theirs = outs[0].at[layer, pl.ds((1 - c) * rh, rh), :]
        return (pltpu.make_async_copy(ins[0], mine, sems[2].at[0]),
                _remote(ins[0], mine, sems[0].at[0], sems[1].at[0], (x, y, 1 - c)),
                _remote(theirs, theirs, sems[0].at[0], sems[1].at[0], (x, y, 1 - c)))

    def start(ins, outs, sems):
        local, send, _ = copies(ins, outs, sems)
        local.start()
        send.start()

    def finish(ins, outs, sems):
        local, send, recv = copies(ins, outs, sems)
        recv.wait_recv()
        send.wait_send()
        local.wait()

    ins = [half] if into is None else [half, into]
    return _Comm(ins, [_shape((depth, 2 * rh, C), half.dtype)], {} if into is None else {1: 0}, [(1,), (1,), (1,)],
                 start, finish, then)


def _comm_only(comms, name):
    plan = _Plan()
    for c in comms:
        plan.at(name, c)
    saved, _PLAN[0] = _PLAN[0], plan
    try:
        def body(o_ref):
            o_ref[...] = jnp.zeros_like(o_ref)

        _pallas(body, out_shape=[jax.ShapeDtypeStruct((8, 128), F32)], in_specs=[],
                out_specs=[pl.BlockSpec(memory_space=pltpu.VMEM)], name=name)()
    finally:
        _PLAN[0] = saved


def _gather_chips(items):
    n = len(items)

    def body(*refs):
        srcs, outs = refs[:n], refs[n:2 * n]
        send_sems, recv_sems, loc_sems = refs[2 * n:]
        x, y, c, chips = _place()
        me = 2 * x + y

        def half(t, chip, cc):
            rh = items[t].shape[1] // 2
            return outs[t].at[chip, :, pl.ds(cc * rh, rh), :]

        def copy(t, k, src, dst, to):
            return pltpu.make_async_remote_copy(src_ref=src, dst_ref=dst, send_sem=send_sems.at[t, k],
                                                recv_sem=recv_sems.at[t, k], device_id=to, device_id_type=MESH)

        local = [pltpu.make_async_copy(srcs[t], outs[t].at[me], loc_sems.at[t]) for t in range(n)]
        for cp in local:
            cp.start()
        sent = []
        for t in range(n):
            rh = items[t].shape[1] // 2
            mine = srcs[t].at[:, pl.ds(c * rh, rh), :]
            for k, (px, py) in enumerate(chips):
                sent.append(copy(t, k, mine, half(t, me, c), (px, py, c)))
                sent[-1].start()
        for k, (px, py) in enumerate(chips):
            for t in range(n):
                blk = half(t, 2 * px + py, c)
                copy(t, k, blk, blk, (px, py, c)).wait_recv()
                sent.append(copy(t, 3 + k, blk, blk, (x, y, 1 - c)))
                sent[-1].start()
        for k, (px, py) in enumerate(chips):
            for t in range(n):
                blk = half(t, 2 * px + py, 1 - c)
                copy(t, 3 + k, blk, blk, (x, y, 1 - c)).wait_recv()
        for cp in sent:
            cp.wait_send()
        for cp in local:
            cp.wait()

    return _pallas(
        body,
        out_shape=[_out((4,) + a.shape, a.dtype) for a in items],
        in_specs=[ANY] * n,
        out_specs=[ANY] * n,
        scratch_shapes=[pltpu.SemaphoreType.DMA((n, 6)), pltpu.SemaphoreType.DMA((n, 6)), pltpu.SemaphoreType.DMA((n,))],
        name="gather_weights",
    )(*items)


def _pair_exchange(items):
    n = len(items)

    def body(*refs):
        srcs, outs = refs[:n], refs[n:2 * n]
        send_sems, recv_sems = refs[2 * n:]
        x, y, c, _ = _place()
        cps = []
        for t in range(n):
            rh = items[t].shape[2] // 2
            cps.append(pltpu.make_async_remote_copy(
                src_ref=srcs[t].at[:, :, pl.ds((1 - c) * rh, rh), :], dst_ref=outs[t], send_sem=send_sems.at[t],
                recv_sem=recv_sems.at[t], device_id=(x, y, 1 - c), device_id_type=MESH))
            cps[-1].start()
        for cp in cps:
            cp.wait()

    return _pallas(
        body,
        out_shape=[_out(a.shape[:2] + (a.shape[2] // 2, a.shape[3]), a.dtype) for a in items],
        in_specs=[ANY] * n,
        out_specs=[ANY] * n,
        scratch_shapes=[pltpu.SemaphoreType.DMA((n,)), pltpu.SemaphoreType.DMA((n,))],
        name="reduce_pair_exchange",
    )(*items)


def _row_tile(rows, cols, itemsize=4, budget=1 << 20):
    t = rows
    while t % 2 == 0 and t // 2 >= 8 and (t // 2) % 8 == 0 and t * cols * itemsize > budget:
        t //= 2
    return t


def _pair_add(full, got, c, name):
    _, L, R, C = full.shape
    rh = R // 2
    tr = _row_tile(rh, C)
    nb = rh // tr

    def body(c_ref, a_ref, b_ref, o_ref):
        o_ref[...] = (a_ref[...] + b_ref[...]).astype(o_ref.dtype)

    grid_spec = pltpu.PrefetchScalarGridSpec(
        num_scalar_prefetch=1,
        grid=(4, L, nb),
        in_specs=[pl.BlockSpec((1, 1, tr, C), lambda j, l, i, c_ref: (j, l, c_ref[0] * nb + i, 0)),
                  pl.BlockSpec((1, 1, tr, C), lambda j, l, i, c_ref: (j, l, i, 0))],
        out_specs=pl.BlockSpec((1, 1, tr, C), lambda j, l, i, c_ref: (j, l, i, 0)),
    )
    return _pallas(
        body,
        out_shape=_out((4, L, rh, C), BF16),
        grid_spec=grid_spec,
        compiler_params=_cparams(("parallel", "parallel", "parallel"), 8 * tr * C * 4),
        name=name,
    )(jnp.reshape(c, (1,)).astype(jnp.int32), full, got)


def _chip_scatter(items):
    n = len(items)

    def body(*refs):
        srcs, outs = refs[:n], refs[n:2 * n]
        send_sems, recv_sems, loc_sems = refs[2 * n:]
        x, y, c, chips = _place()
        me = 2 * x + y
        local = [pltpu.make_async_copy(srcs[t].at[me], outs[t].at[me], loc_sems.at[t]) for t in range(n)]
        for cp in local:
            cp.start()
        cps = []
        for t in range(n):
            for k, (px, py) in enumerate(chips):
                cps.append(pltpu.make_async_remote_copy(
                    src_ref=srcs[t].at[2 * px + py], dst_ref=outs[t].at[me], send_sem=send_sems.at[t, k],
                    recv_sem=recv_sems.at[t, k], device_id=(px, py, c), device_id_type=MESH))
                cps[-1].start()
        for t in range(n):
            for k, (px, py) in enumerate(chips):
                blk = outs[t].at[2 * px + py]
                pltpu.make_async_remote_copy(src_ref=blk, dst_ref=blk, send_sem=send_sems.at[t, k], recv_sem=recv_sems.at[t, k],
                                             device_id=(px, py, c), device_id_type=MESH).wait_recv()
        for cp in cps:
            cp.wait_send()
        for cp in local:
            cp.wait()

    return _pallas(
        body,
        out_shape=[_out(a.shape, a.dtype) for a in items],
        in_specs=[ANY] * n,
        out_specs=[ANY] * n,
        scratch_shapes=[pltpu.SemaphoreType.DMA((n, 3)), pltpu.SemaphoreType.DMA((n, 3)), pltpu.SemaphoreType.DMA((n,))],
        name="reduce_chip_scatter",
    )(*items)


def _sum_chips(parts, name):
    _, L, rh, C = parts.shape
    tr = _row_tile(rh, C)

    def body(p_ref, o_ref):
        acc = p_ref[0].astype(F32)
        for j in range(1, 4):
            acc = acc + p_ref[j].astype(F32)
        o_ref[...] = acc

    return _pallas(
        body,
        out_shape=_out((L, rh, C), F32),
        grid=(L, rh // tr),
        in_specs=[pl.BlockSpec((4, 1, tr, C), lambda l, i: (0, l, i, 0))],
        out_specs=pl.BlockSpec((1, tr, C), lambda l, i: (l, i, 0)),
        compiler_params=_cparams(("parallel", "parallel"), 16 * tr * C * 4),
        name=name,
    )(parts)


def _pair_allgather(groups):
    flat = [(w, l) for w, grp in enumerate(groups) for l in range(len(grp))]
    items = [groups[w][l] for w, l in flat]
    n, nw = len(items), len(groups)

    def body(*refs):
        srcs, outs = refs[:n], refs[n:n + nw]
        send_sems, recv_sems, loc_sems = refs[n + nw:]
        x, y, c, _ = _place()
        cps, local = [], []
        for t, (w, l) in enumerate(flat):
            rh = items[t].shape[1]
            mine = outs[w].at[pl.ds(l, 1), pl.ds(c * rh, rh), :]
            local.append(pltpu.make_async_copy(srcs[t], mine, loc_sems.at[t]))
            local[-1].start()
            cps.append(pltpu.make_async_remote_copy(src_ref=srcs[t], dst_ref=mine, send_sem=send_sems.at[t],
                                                    recv_sem=recv_sems.at[t], device_id=(x, y, 1 - c), device_id_type=MESH))
            cps[-1].start()
        for t, (w, l) in enumerate(flat):
            rh = items[t].shape[1]
            theirs = outs[w].at[pl.ds(l, 1), pl.ds((1 - c) * rh, rh), :]
            pltpu.make_async_remote_copy(src_ref=theirs, dst_ref=theirs, send_sem=send_sems.at[t], recv_sem=recv_sems.at[t],
                                         device_id=(x, y, 1 - c), device_id_type=MESH).wait_recv()
        for cp in cps:
            cp.wait_send()
        for cp in local:
            cp.wait()

    return _pallas(
        body,
        out_shape=[_out((len(grp), 2 * grp[0].shape[1], grp[0].shape[2]), grp[0].dtype) for grp in groups],
        in_specs=[ANY] * n,
        out_specs=[ANY] * nw,
        scratch_shapes=[pltpu.SemaphoreType.DMA((n,)), pltpu.SemaphoreType.DMA((n,)), pltpu.SemaphoreType.DMA((n,))],
        name="reduce_pair_allgather",
    )(*items)


def _allreduce_small(v):
    R = v.shape[0]

    def body(v_ref, o_ref, slots, send_sems, recv_sems):
        x, y, c, _ = _place()
        me = 4 * x + 2 * y + c
        slots[me] = v_ref[...]
        cps = []
        for r in range(1, 8):
            px, py, pc = x ^ (r >> 2), y ^ ((r >> 1) & 1), c ^ (r & 1)
            cps.append(pltpu.make_async_remote_copy(src_ref=v_ref, dst_ref=slots.at[me], send_sem=send_sems.at[r - 1],
                                                    recv_sem=recv_sems.at[r - 1], device_id=(px, py, pc), device_id_type=MESH))
            cps[-1].start()
        for r in range(1, 8):
            px, py, pc = x ^ (r >> 2), y ^ ((r >> 1) & 1), c ^ (r & 1)
            theirs = slots.at[4 * px + 2 * py + pc]
            pltpu.make_async_remote_copy(src_ref=theirs, dst_ref=theirs, send_sem=send_sems.at[r - 1], recv_sem=recv_sems.at[r - 1],
                                         device_id=(px, py, pc), device_id_type=MESH).wait_recv()
        acc = slots[0]
        for j in range(1, 8):
            acc = acc + slots[j]
        o_ref[...] = acc
        for cp in cps:
            cp.wait_send()

    return pl.pallas_call(
        body,
        out_shape=jax.ShapeDtypeStruct(v.shape, F32),
        in_specs=[pl.BlockSpec(memory_space=pltpu.VMEM)],
        out_specs=pl.BlockSpec(memory_space=pltpu.VMEM),
        scratch_shapes=[pltpu.VMEM((8, R, 128), F32), pltpu.SemaphoreType.DMA((7,)), pltpu.SemaphoreType.DMA((7,))],
        name="allreduce_small",
    )(v)


def _adamw_math(w, g, m, v):
    m = ADAM_B1 * m + (1.0 - ADAM_B1) * g
    v = ADAM_B2 * v + (1.0 - ADAM_B2) * (g * g)
    m_hat = m / (1.0 - ADAM_B1 ** ADAM_STEP)
    v_hat = v / (1.0 - ADAM_B2 ** ADAM_STEP)
    delta = -ADAM_LR * (m_hat / (jnp.sqrt(v_hat) + ADAM_EPS) + ADAM_WD * w)
    return delta, m, v


def _adamw(w, g, m, v, name):
    rows, cols = w.shape
    tr = _row_tile(rows, cols, budget=1 << 19)

    def body(w_ref, g_ref, m_ref, v_ref, d_ref, nm_ref, nv_ref):
        d_ref[...], nm_ref[...], nv_ref[...] = _adamw_math(w_ref[...], g_ref[...], m_ref[...], v_ref[...])

    blk = pl.BlockSpec((tr, cols), lambda i: (i, 0))
    out = _out(w.shape, F32)
    return _pallas(
        body,
        out_shape=(out, out, out),
        grid=(rows // tr,),
        in_specs=[blk] * 4,
        out_specs=(blk, blk, blk),
        compiler_params=_cparams(("parallel",), 16 * tr * cols * 4),
        name=name,
    )(w, g, m, v)


def _adamw_small(ws, gs, ms, vs):
    n = len(ws)

    def body(*refs):
        for t in range(n):
            w_ref, g_ref, m_ref, v_ref = (refs[k * n + t] for k in range(4))
            d_ref, nm_ref, nv_ref = (refs[(4 + k) * n + t] for k in range(3))
            d_ref[...], nm_ref[...], nv_ref[...] = _adamw_math(w_ref[...], g_ref[...], m_ref[...], v_ref[...])

    vmem = pl.BlockSpec(memory_space=pltpu.VMEM)
    outs = [jax.ShapeDtypeStruct(w.shape, F32) for w in ws]
    res = pl.pallas_call(
        body,
        out_shape=outs * 3,
        in_specs=[vmem] * (4 * n),
        out_specs=[vmem] * (3 * n),
        name="adamw_small",
    )(*ws, *gs, *ms, *vs)
    return res[:n], res[n:2 * n], res[2 * n:]


IN_SIZES = (192, 192, 384, 384, 16, 512, 384, 384, 384)
IN_OFFS = tuple(int(v) for v in np.cumsum((0,) + IN_SIZES))
SMALL = ("norm_mix", "w_gla_gate", "b_gla_gate", "gla_norm", "b_dw", "conv_ln_g", "conv_ln_b", "rel_bias", "norm_ffn")


def _pad_cols(a, n):
    return jnp.pad(a, ((0, 0), (0, n - a.shape[1])))


def _split_w_in(w):
    s = [w[:, IN_OFFS[i]:IN_OFFS[i + 1]] for i in range(9)]
    w_gla = jnp.concatenate([_pad_cols(s[0], KW), _pad_cols(s[1], KW), s[2], s[3], _pad_cols(s[4], LRW)], axis=1)
    return w_gla, s[5], jnp.concatenate(s[6:9], axis=1)


def _join_w_in(g):
    gg = g["w_gla"]
    return jnp.concatenate([gg[:, 0:192], gg[:, KW:KW + 192], gg[:, 2 * KW:2 * KW + VW], gg[:, 2 * KW + VW:2 * KW + 2 * VW],
                            gg[:, 2 * KW + 2 * VW:2 * KW + 2 * VW + 16], g["w_conv"], g["w_att"]],
                           axis=1)


def _pack(arrs, rows):
    flat = jnp.concatenate([a.reshape(-1) for a in arrs])
    return jnp.pad(flat, (0, rows * 128 - flat.shape[0])).reshape(rows, 128)


def _unpack(packed, shapes):
    flat = packed.reshape(-1)
    out, off = [], 0
    for s in shapes:
        n = int(np.prod(s))
        out.append(flat[off:off + n].reshape(s))
        off += n
    return out


def kernel(x, norm_mix, w_in, w_gla_gate, b_gla_gate, gla_norm, w_dw, b_dw, conv_ln_g, conv_ln_b, rel_bias, w_out, norm_ffn, w_up, w_down, norm_final, loss_target, m_norm_mix, m_w_in, m_w_gla_gate, m_b_gla_gate, m_gla_norm, m_w_dw, m_b_dw, m_conv_ln_g, m_conv_ln_b, m_rel_bias, m_w_out, m_norm_ffn, m_w_up, m_w_down, m_norm_final, v_norm_mix, v_w_in, v_w_gla_gate, v_b_gla_gate, v_gla_norm, v_w_dw, v_b_dw, v_conv_ln_g, v_conv_ln_b, v_rel_bias, v_w_out, v_norm_ffn, v_w_up, v_w_down, v_norm_final):
    P = dict(norm_mix=norm_mix, w_in=w_in, w_gla_gate=w_gla_gate, b_gla_gate=b_gla_gate, gla_norm=gla_norm, w_dw=w_dw, b_dw=b_dw,
             conv_ln_g=conv_ln_g, conv_ln_b=conv_ln_b, rel_bias=rel_bias, w_out=w_out, norm_ffn=norm_ffn, w_up=w_up,
             w_down=w_down, norm_final=norm_final)
    Mo = dict(norm_mix=m_norm_mix, w_in=m_w_in, w_gla_gate=m_w_gla_gate, b_gla_gate=m_b_gla_gate, gla_norm=m_gla_norm, w_dw=m_w_dw,
              b_dw=m_b_dw, conv_ln_g=m_conv_ln_g, conv_ln_b=m_conv_ln_b, rel_bias=m_rel_bias, w_out=m_w_out, norm_ffn=m_norm_ffn,
              w_up=m_w_up, w_down=m_w_down, norm_final=m_norm_final)
    Vo = dict(norm_mix=v_norm_mix, w_in=v_w_in, w_gla_gate=v_w_gla_gate, b_gla_gate=v_b_gla_gate, gla_norm=v_gla_norm, w_dw=v_w_dw,
              b_dw=v_b_dw, conv_ln_g=v_conv_ln_g, conv_ln_b=v_conv_ln_b, rel_bias=v_rel_bias, w_out=v_w_out, norm_ffn=v_norm_ffn,
              w_up=v_w_up, w_down=v_w_down, norm_final=v_norm_final)
    depth = w_in.shape[0]
    xi, yi, ci = lax.axis_index("x"), lax.axis_index("y"), lax.axis_index("c")
    chip = 2 * xi + yi

    plan = _Plan()
    _PLAN[0] = plan
    layers = [dict(
        norm_mix=norm_mix[l][None], wg=jnp.pad(w_gla_gate[l], ((0, LRW - 16), (0, KW - 192))),
        bg=_pad_cols(b_gla_gate[l][None], KW), gla_norm=gla_norm[l][None], b_dw=b_dw[l][None], ln_g=conv_ln_g[l][None],
        ln_b=conv_ln_b[l][None], rel_bias=rel_bias[l], norm_ffn=norm_ffn[l][None]) for l in range(depth)]

    def have_w_in(l, full):
        layers[l]["w_gla"], layers[l]["w_conv"], layers[l]["w_att"] = _split_w_in(jnp.transpose(full, (1, 0, 2)).reshape(D, -1))

    def have_w_out(l, full):
        w = full.reshape(D, D)
        layers[l]["w_out_g"], layers[l]["w_out_c"], layers[l]["w_out_a"] = w[0:VW], w[VW:VW + CW], w[VW + CW:]

    def have_w_up(l, full):
        layers[l]["w_up"] = full

    def have_w_down(l, full):
        layers[l]["w_down"] = full.reshape(D_FF, D)

    w_dw_pad = jnp.pad(w_dw, ((0, 0), (0, HALO - CK), (0, 0)))
    first_in, g_dw = _gather_chips([w_in[0:1].astype(BF16), w_dw_pad])
    have_w_in(0, first_in[:, 0])
    for l in range(depth):
        layers[l]["w_dw"] = jnp.transpose(g_dw[:, l], (1, 0, 2)).reshape(HALO, CW)

    def gather_behind(shard, ici_call, d2d_call, have):
        plan.at(ici_call, _phase_gather_ici(
            shard, lambda outs: plan.at(d2d_call, _phase_gather_d2d(outs[0], lambda done: have(done[0])))))

    for l in range(depth):
        if l > 0:
            gather_behind(w_in[l].astype(BF16), f"l{l - 1}_mlp_up", f"l{l - 1}_mlp_down", functools.partial(have_w_in, l))
        gather_behind(w_out[l].astype(BF16), f"l{l - 1}_mlp_down" if l > 0 else "l0_proj", f"l{l}_gla_fwd",
                      functools.partial(have_w_out, l))
        if l > 0:
            gather_behind(w_up[l].astype(BF16), f"l{l}_proj", f"l{l}_gla_fwd", functools.partial(have_w_up, l))
            gather_behind(w_down[l].astype(BF16), f"l{l}_gla_fwd", f"l{l}_att_fwd", functools.partial(have_w_down, l))
        else:
            gather_behind(w_up[l].astype(BF16), f"l{l}_gla_fwd", f"l{l}_att_fwd", functools.partial(have_w_up, l))
            gather_behind(w_down[l].astype(BF16), f"l{l}_att_fwd", f"l{l}_mlp_up", functools.partial(have_w_down, l))

    reduced = {}

    def reduce_calls(name, l):
        if name == "w_down":
            return f"l{l}_mlp_up_dx", f"l{l}_gla_bwd", f"l{l}_conv_bwd_dc"
        if name == "w_up":
            return f"l{l}_out_dx", f"l{l}_att_bwd", f"l{l}_proj_dw"
        if name == "w_out":
            return f"l{l}_gla_bwd", f"l{l}_conv_bwd_dc", f"l{l}_att_bwd"
        if l > 0:
            return f"l{l}_proj_dx", f"l{l - 1}_mlp_down_dw", f"l{l - 1}_mlp_up_dx"
        return "l0_proj_dx", "l0_norm_mix_bwd", None

    def reduce_behind(l, name, full):
        calls = reduce_calls(name, l)

        def swapped(outs):
            pair = _pair_add(full[:, None], outs[0][:, None], ci, f"reduce_pair_add_{name}{l}")[:, 0]
            plan.at(calls[1], _phase_chip_scatter(pair, scattered))

        def scattered(outs):
            half = _sum_chips(outs[0][:, None], f"reduce_sum_chips_{name}{l}")[0]
            phase = _phase_pair_allgather(half, l, depth, reduced.get(name), gathered)
            if calls[2] is None:
                _comm_only([phase], f"reduce_pair_allgather_{name}{l}")
            else:
                plan.at(calls[2], phase)

        def gathered(outs):
            reduced[name] = outs[0]

        plan.at(calls[0], _phase_pair_exchange(full, swapped))

    loss_part, grad_x, grads, g_final = _local_step(x[0], loss_target[0], layers, norm_final[None], reduce_behind)
    _PLAN[0] = None
    assert not plan.by_call, sorted(plan.by_call)
    loss = lax.psum(loss_part[0, 0], ("x", "y", "c"))
    gw_in, gw_out, gw_up, gw_down = reduced["w_in"], reduced["w_out"], reduced["w_up"], reduced["w_down"]

    small_g = []
    for l in range(depth):
        g = grads[l]
        small_g += [g["norm_mix"], g["wg"][0:16, 0:192], g["bg"][:, 0:192], g["gla_norm"], g["b_dw"], g["ln_g"], g["ln_b"],
                    g["rel_bias"], g["norm_ffn"], g["w_dw"][0:CK]]
    small_g.append(g_final)
    small_shapes = [a.shape for a in small_g]
    n_small = sum(int(np.prod(s)) for s in small_shapes)
    rows = -(-n_small // 1024) * 8
    red = _unpack(_allreduce_small(_pack(small_g, rows)), small_shapes)
    G = {}
    per = len(SMALL) + 1
    for i, name in enumerate(SMALL):
        G[name] = jnp.stack([red[l * per + i].reshape(P[name].shape[1:]) for l in range(depth)])
    gw_dw_all = jnp.stack([red[l * per + len(SMALL)] for l in range(depth)])
    G["w_dw"] = lax.dynamic_slice_in_dim(gw_dw_all, chip * (CW // 4), CW // 4, axis=2)
    G["norm_final"] = red[-1].reshape(norm_final.shape)
    G["w_in"], G["w_out"], G["w_up"], G["w_down"] = gw_in, gw_out, gw_up, gw_down

    delta, new_m, new_v = {}, {}, {}
    for name in ("w_in", "w_out", "w_up", "w_down"):
        shp = P[name].shape
        two_d = lambda a: a.reshape(-1, shp[-1])
        d, nm, nv = _adamw(two_d(P[name]), two_d(G[name]), two_d(Mo[name]), two_d(Vo[name]), f"adamw_{name}")
        delta[name], new_m[name], new_v[name] = d.reshape(shp), nm.reshape(shp), nv.reshape(shp)
    small_names = list(SMALL) + ["w_dw", "norm_final"]
    two_d = lambda a: a.reshape(-1, a.shape[-1])
    ds, nms, nvs = _adamw_small([two_d(P[k]) for k in small_names], [two_d(G[k]) for k in small_names],
                                [two_d(Mo[k]) for k in small_names], [two_d(Vo[k]) for k in small_names])
    for i, name in enumerate(small_names):
        shp = P[name].shape
        delta[name], new_m[name], new_v[name] = ds[i].reshape(shp), nms[i].reshape(shp), nvs[i].reshape(shp)

    order = ["norm_mix", "w_in", "w_gla_gate", "b_gla_gate", "gla_norm", "w_dw", "b_dw", "conv_ln_g", "conv_ln_b", "rel_bias",
             "w_out", "norm_ffn", "w_up", "w_down", "norm_final"]
    return (loss, grad_x[None], *[G[k] for k in order], *[delta[k] for k in order], *[new_m[k] for k in order],
            *[new_v[k] for k in order])
```

```python
import functools

import numpy as np
import jax
import jax.numpy as jnp
from jax import lax
from jax.experimental import pallas as pl
from jax.experimental.pallas import tpu as pltpu

F32 = jnp.float32
BF16 = jnp.bfloat16
HI = lax.Precision.HIGHEST

D = 1024
CHUNK = 64
GLA_DK, GLA_DV, GLA_H = 48, 96, 4
KW = 256
VW = 384
LRW = 128
GLA_TAU = 16.0
CW = 256
CK = 31
AW = 384
AH = 6
BAND = 576
LEFT = 512
D_FF = 4096
EPS = 1e-6
NEG = -1e30
N_REL = 257

GLA_COLS = 2 * KW + 2 * VW + LRW
CONV_COLS = 2 * CW
ATT_COLS = 3 * AW

ADAM_LR, ADAM_B1, ADAM_B2, ADAM_EPS, ADAM_WD, ADAM_STEP = 0.001, 0.9, 0.999, 1e-08, 0.01, 10

VMEM_CAP = 56 * 1024 * 1024
MESH = pl.DeviceIdType.MESH


def _cparams(sem, vmem_bytes):
    limit = int(min(VMEM_CAP, max(vmem_bytes * 5 // 4 + (4 << 20), 16 << 20)))
    return pltpu.CompilerParams(dimension_semantics=sem, vmem_limit_bytes=limit)


def _out(shape, dtype):
    return pltpu.HBM(tuple(shape), dtype)


class _Comm:
    def __init__(self, ins, outs, aliases, sems, start, finish, then=None):
        self.ins, self.outs, self.aliases, self.sems = list(ins), list(outs), dict(aliases), list(sems)
        self.start, self.finish, self.then = start, finish, then


class _Plan:
    def __init__(self):
        self.by_call = {}

    def at(self, call, comm):
        self.by_call.setdefault(call, []).append(comm)

    def take(self, call):
        return self.by_call.pop(call, [])


_PLAN = [None]


def _pin(a):
    return pltpu.with_memory_space_constraint(a, pltpu.HBM) if jnp.issubdtype(a.dtype, jnp.floating) else a


def _pallas(body, **kw):
    comms = _PLAN[0].take(kw.get("name")) if _PLAN[0] is not None else []
    if not comms:
        call = pl.pallas_call(body, **kw)
        return lambda *args: call(*[_pin(a) for a in args])

    grid = tuple(kw.get("grid", ()))
    single = not isinstance(kw["out_shape"], (tuple, list))
    out_shape = [kw["out_shape"]] if single else list(kw["out_shape"])
    out_specs = [kw["out_specs"]] if single else list(kw["out_specs"])
    in_specs = list(kw["in_specs"])
    scratch = list(kw.get("scratch_shapes", ()))
    n_in, n_out, n_scr = len(in_specs), len(out_shape), len(scratch)
    c_in = sum(len(c.ins) for c in comms)
    c_out = sum(len(c.outs) for c in comms)
    aliases = dict(kw.get("input_output_aliases", {}))
    i0, o0 = n_in, n_out
    for c in comms:
        for i, o in c.aliases.items():
            aliases[i0 + i] = o0 + o
        i0 += len(c.ins)
        o0 += len(c.outs)

    def wrapped(*refs):
        ins, c_ins = refs[:n_in], refs[n_in:n_in + c_in]
        outs, c_outs = refs[n_in + c_in:n_in + c_in + n_out], refs[n_in + c_in + n_out:n_in + c_in + n_out + c_out]
        scr, c_sems = refs[n_in + c_in + n_out + c_out:][:n_scr], refs[n_in + c_in + n_out + c_out + n_scr:]

        def each(what):
            i0 = o0 = s0 = 0
            for c in comms:
                getattr(c, what)(c_ins[i0:i0 + len(c.ins)], c_outs[o0:o0 + len(c.outs)], c_sems[s0:s0 + len(c.sems)])
                i0, o0, s0 = i0 + len(c.ins), o0 + len(c.outs), s0 + len(c.sems)

        if grid:
            first = functools.reduce(jnp.logical_and, [pl.program_id(a) == 0 for a in range(len(grid))])
            last = functools.reduce(jnp.logical_and, [pl.program_id(a) == grid[a] - 1 for a in range(len(grid))])
            pl.when(first)(lambda: each("start"))
            body(*ins, *outs, *scr)
            pl.when(last)(lambda: each("finish"))
        else:
            each("start")
            body(*ins, *outs, *scr)
            each("finish")

    kw = dict(kw)
    kw["in_specs"] = in_specs + [ANY] * c_in
    kw["out_shape"] = out_shape + [_out(s.shape, s.dtype) for c in comms for s in c.outs]
    kw["out_specs"] = out_specs + [ANY] * c_out
    kw["scratch_shapes"] = scratch + [pltpu.SemaphoreType.DMA(s) for c in comms for s in c.sems]
    kw["input_output_aliases"] = aliases
    call = pl.pallas_call(wrapped, **kw)

    def run(*args):
        res = call(*[_pin(a) for a in args], *[_pin(a) for c in comms for a in c.ins])
        o0 = n_out
        for c in comms:
            if c.then is not None:
                c.then(res[o0:o0 + len(c.outs)])
            o0 += len(c.outs)
        return res[0] if single else res[:n_out]

    return run


def _nbytes(shape, dtype):
    return int(np.prod(shape)) * jnp.dtype(dtype).itemsize


def _sigmoid(x):
    return 1.0 / (1.0 + jnp.exp(-x))


_DIMS = {"nn": (((1,), (0,)), ((), ())), "nt": (((1,), (1,)), ((), ())), "tn": (((0,), (0,)), ((), ()))}


def _mm(a, b, *, mode, out_dtype, name, tm=512, tn=None, tk=None, a_pro=None, epi=None, extra=None,
        b_chips=False, out_chips=False):
    b2 = (b.shape[1], 4 * b.shape[2]) if b_chips else b.shape
    if mode == "nn":
        (M, K), (K2, N) = a.shape, b2
    elif mode == "nt":
        (M, K), (N, K2) = a.shape, b2
    else:
        (K, M), (K2, N) = a.shape, b2
    assert K == K2, (a.shape, b.shape, mode)
    tm = min(tm, M)
    tn = N if tn is None else min(tn, N)
    tk = K if tk is None else min(tk, K)
    assert M % tm == 0 and N % tn == 0 and K % tk == 0, (M, N, K, tm, tn, tk)
    nk = K // tk
    a_blk = (tk, tm) if mode == "tn" else (tm, tk)
    a_map = (lambda i, j, k: (k, i)) if mode == "tn" else (lambda i, j, k: (i, k))
    b_blk = (tn, tk) if mode == "nt" else (tk, tn)
    b_map = (lambda i, j, k: (j, k)) if mode == "nt" else (lambda i, j, k: (k, j))
    if b_chips:
        per = b.shape[2] // b_blk[1]
        assert b.shape[2] % b_blk[1] == 0 and mode != "tn"
        flat_map = b_map
        b_map = lambda i, j, k: (flat_map(i, j, k)[1] // per, flat_map(i, j, k)[0], flat_map(i, j, k)[1] % per)
        b_blk = (None,) + b_blk
    in_specs = [pl.BlockSpec(a_blk, a_map), pl.BlockSpec(b_blk, b_map)]
    args = [a, b]
    if epi is not None:
        in_specs.append(pl.BlockSpec((tm, tn), lambda i, j, k: (i, j)))
        args.append(extra)

    def body(*refs):
        a_ref, b_ref = refs[0], refs[1]
        e_ref = refs[2] if epi is not None else None
        o_ref = refs[3] if epi is not None else refs[2]
        av = a_ref[...]
        if a_pro == "relu2":
            af = jnp.maximum(av.astype(F32), 0.0)
            av = af * af
        p = lax.dot_general(av.astype(BF16), b_ref[...].astype(BF16), _DIMS[mode], preferred_element_type=F32)

        def finish(acc):
            if epi == "add":
                acc = acc + e_ref[...].astype(F32)
            elif epi == "relu2grad":
                acc = acc * (2.0 * jnp.maximum(e_ref[...].astype(F32), 0.0))
            o_ref[...] = acc.astype(o_ref.dtype)

        if nk == 1:
            finish(p)
        else:
            acc_ref = refs[-1]
            k = pl.program_id(2)

            @pl.when(k == 0)
            def _():
                acc_ref[...] = p

            @pl.when(k > 0)
            def _():
                acc_ref[...] += p

            @pl.when(k == nk - 1)
            def _():
                finish(acc_ref[...])

    vm = 2 * (_nbytes(a_blk, a.dtype) + _nbytes((tk, tn), b.dtype) + _nbytes((tm, tn), out_dtype))
    vm += 3 * _nbytes((tm, tn), F32)
    if epi is not None:
        vm += 2 * _nbytes((tm, tn), extra.dtype)
    if out_chips:
        per_out = N // 4 // tn
        assert N % (4 * tn) == 0
        out_shape = _out((4, M, N // 4), out_dtype)
        out_spec = pl.BlockSpec((None, tm, tn), lambda i, j, k: (j // per_out, i, j % per_out))
    else:
        out_shape = _out((M, N), out_dtype)
        out_spec = pl.BlockSpec((tm, tn), lambda i, j, k: (i, j))
    return _pallas(
        body,
        out_shape=out_shape,
        grid=(M // tm, N // tn, nk),
        in_specs=in_specs,
        out_specs=out_spec,
        scratch_shapes=[pltpu.VMEM((tm, tn), F32)] if nk > 1 else [],
        compiler_params=_cparams(("parallel", "parallel", "arbitrary"), vm),
        name=name,
    )(*args)


def _mm_fan(a, bs, *, mode, out_dtypes, name, tm=512):
    M, K = a.shape
    ns = [b.shape[1] if mode == "nn" else b.shape[0] for b in bs]
    n = len(bs)

    def body(*refs):
        av = refs[0][...].astype(BF16)
        for i in range(n):
            refs[1 + n + i][...] = lax.dot_general(av, refs[1 + i][...], _DIMS[mode],
                                                   preferred_element_type=F32).astype(refs[1 + n + i].dtype)

    vm = 2 * _nbytes((tm, K), a.dtype) + sum(2 * _nbytes(b.shape, b.dtype) + 3 * _nbytes((tm, nn), F32) for b, nn in zip(bs, ns))
    return _pallas(
        body,
        out_shape=tuple(_out((M, nn), dt) for nn, dt in zip(ns, out_dtypes)),
        grid=(M // tm,),
        in_specs=[pl.BlockSpec((tm, K), lambda i: (i, 0))] + [pl.BlockSpec(b.shape, lambda i: (0, 0)) for b in bs],
        out_specs=tuple(pl.BlockSpec((tm, nn), lambda i: (i, 0)) for nn in ns),
        compiler_params=_cparams(("parallel",), vm),
        name=name,
    )(a, *bs)


def _mm_sum(as_, bs, *, mode, out_dtype, name, extra=None, tm=512):
    M = as_[0].shape[0]
    N = bs[0].shape[1] if mode == "nn" else bs[0].shape[0]
    n = len(as_)

    def body(*refs):
        acc = None
        for i in range(n):
            p = lax.dot_general(refs[i][...].astype(BF16), refs[n + i][...], _DIMS[mode], preferred_element_type=F32)
            acc = p if acc is None else acc + p
        if extra is not None:
            acc = acc + refs[2 * n][...].astype(F32)
        refs[-1][...] = acc.astype(refs[-1].dtype)

    in_specs = [pl.BlockSpec((tm, a.shape[1]), lambda i: (i, 0)) for a in as_]
    in_specs += [pl.BlockSpec(b.shape, lambda i: (0, 0)) for b in bs]
    args = list(as_) + list(bs)
    if extra is not None:
        in_specs.append(pl.BlockSpec((tm, N), lambda i: (i, 0)))
        args.append(extra)
    vm = sum(2 * _nbytes((tm, a.shape[1]), a.dtype) for a in as_) + sum(2 * _nbytes(b.shape, b.dtype) for b in bs)
    vm += 8 * _nbytes((tm, N), F32)
    return _pallas(
        body,
        out_shape=_out((M, N), out_dtype),
        grid=(M // tm,),
        in_specs=in_specs,
        out_specs=pl.BlockSpec((tm, N), lambda i: (i, 0)),
        compiler_params=_cparams(("parallel",), vm),
        name=name,
    )(*args)


def _mm_tn_multi(ops, pairs, *, name, tk=512):
    T = ops[0].shape[0]
    n, m = len(ops), len(pairs)
    shapes = [(ops[a].shape[1], ops[b].shape[1]) for a, b in pairs]

    def body(*refs):
        vals = [refs[i][...].astype(BF16) for i in range(n)]
        first = pl.program_id(0) == 0
        for j, (a, b) in enumerate(pairs):
            p = lax.dot_general(vals[a], vals[b], _DIMS["tn"], preferred_element_type=F32)
            o_ref = refs[n + j]

            @pl.when(first)
            def _(o_ref=o_ref, p=p):
                o_ref[...] = p

            @pl.when(jnp.logical_not(first))
            def _(o_ref=o_ref, p=p):
                o_ref[...] += p

    vm = sum(2 * _nbytes((tk, o.shape[1]), o.dtype) for o in ops) + sum(3 * _nbytes(s, F32) for s in shapes)
    return _pallas(
        body,
        out_shape=tuple(_out(s, F32) for s in shapes),
        grid=(T // tk,),
        in_specs=[pl.BlockSpec((tk, o.shape[1]), lambda k: (k, 0)) for o in ops],
        out_specs=tuple(pl.BlockSpec(s, lambda k: (0, 0)) for s in shapes),
        compiler_params=_cparams(("arbitrary",), vm),
        name=name,
    )(*ops)


def _rmsnorm_fwd(h, g, name, tm=512):
    T = h.shape[0]

    def body(h_ref, g_ref, o_ref):
        x = h_ref[...]
        r = lax.rsqrt(jnp.mean(x * x, axis=-1, keepdims=True) + EPS)
        o_ref[...] = (x * r * g_ref[...]).astype(o_ref.dtype)

    return _pallas(
        body,
        out_shape=_out((T, D), BF16),
        grid=(T // tm,),
        in_specs=[pl.BlockSpec((tm, D), lambda i: (i, 0)), pl.BlockSpec((1, D), lambda i: (0, 0))],
        out_specs=pl.BlockSpec((tm, D), lambda i: (i, 0)),
        compiler_params=_cparams(("parallel",), 8 * _nbytes((tm, D), F32)),
        name=name,
    )(h, g)


def _rmsnorm_bwd(dxn, h, g, dres, name, tm=512):
    T = h.shape[0]

    def body(dxn_ref, h_ref, g_ref, dres_ref, dh_ref, dg_ref):
        @pl.when(pl.program_id(0) == 0)
        def _():
            dg_ref[...] = jnp.zeros_like(dg_ref)

        x = h_ref[...]
        dy = dxn_ref[...].astype(F32)
        r = lax.rsqrt(jnp.mean(x * x, axis=-1, keepdims=True) + EPS)
        gy = dy * g_ref[...]
        dot = jnp.mean(x * gy, axis=-1, keepdims=True)
        dh_ref[...] = dres_ref[...] + r * gy - x * (r * r * r * dot)
        dg_ref[...] += jnp.sum(dy * x * r, axis=0, keepdims=True)

    row = pl.BlockSpec((tm, D), lambda i: (i, 0))
    vec = pl.BlockSpec((1, D), lambda i: (0, 0))
    return _pallas(
        body,
        out_shape=(_out((T, D), F32), _out((1, D), F32)),
        grid=(T // tm,),
        in_specs=[row, row, vec, row],
        out_specs=(row, vec),
        compiler_params=_cparams(("arbitrary",), 12 * _nbytes((tm, D), F32)),
        name=name,
    )(dxn, h, g, dres)


def _final_loss(h, g, target, name, tm=512):
    T = h.shape[0]

    def body(h_ref, g_ref, t_ref, loss_ref, dh_ref, dg_ref):
        @pl.when(pl.program_id(0) == 0)
        def _():
            dg_ref[...] = jnp.zeros_like(dg_ref)
            loss_ref[...] = jnp.zeros_like(loss_ref)

        x = h_ref[...]
        gg = g_ref[...]
        r = lax.rsqrt(jnp.mean(x * x, axis=-1, keepdims=True) + EPS)
        y = x * r * gg
        e = y - t_ref[...]
        loss_ref[...] += 0.5 * jnp.sum(jnp.mean(e * e, axis=-1, keepdims=True), axis=0, keepdims=True)
        dy = e * (1.0 / D)
        gy = dy * gg
        dot = jnp.mean(x * gy, axis=-1, keepdims=True)
        dh_ref[...] = r * gy - x * (r * r * r * dot)
        dg_ref[...] += jnp.sum(dy * x * r, axis=0, keepdims=True)

    row = pl.BlockSpec((tm, D), lambda i: (i, 0))
    vec = pl.BlockSpec((1, D), lambda i: (0, 0))
    one = pl.BlockSpec((1, 1), lambda i: (0, 0))
    return _pallas(
        body,
        out_shape=(_out((1, 1), F32), _out((T, D), F32), _out((1, D), F32)),
        grid=(T // tm,),
        in_specs=[row, vec, row],
        out_specs=(one, row, vec),
        compiler_params=_cparams(("arbitrary",), 12 * _nbytes((tm, D), F32)),
        name=name,
    )(h, g, target)


GLA_G = 8


def _gla_consts():
    i = np.arange(KW)[:, None]
    j = np.arange(VW)[None, :]
    mask = ((i // GLA_DK) == (j // GLA_DV)) & (i < GLA_H * GLA_DK)
    a = np.arange(VW)
    hm = ((a[:, None] // GLA_DV) == (a[None, :] // GLA_DV)).astype(np.float32)
    c = np.arange(CHUNK)
    low = (c[:, None] >= c[None, :]).astype(np.float32)
    return jnp.asarray(mask.astype(np.float32)), jnp.asarray(hm, BF16), jnp.asarray(low, BF16)


def _split(x):
    hi = x.astype(BF16)
    return hi, (x - hi.astype(F32)).astype(BF16)


def _dot_sel(a, b, dims, split):
    if split == "a":
        hi, lo = _split(a)
        return (lax.dot_general(hi, b, dims, preferred_element_type=F32) + lax.dot_general(lo, b, dims, preferred_element_type=F32))
    hi, lo = _split(b)
    return (lax.dot_general(a, hi, dims, preferred_element_type=F32) + lax.dot_general(a, lo, dims, preferred_element_type=F32))


def _dot3(a, b, dims):
    ah, al = _split(a)
    bh, bl = _split(b)
    return (lax.dot_general(ah, bh, dims, preferred_element_type=F32) + lax.dot_general(al, bh, dims, preferred_element_type=F32)
            + lax.dot_general(ah, bl, dims, preferred_element_type=F32))


def _gla_chunk_common(p_ref, rows, wg, bg, low, ones_v):
    q = p_ref[rows, 0:KW]
    k = p_ref[rows, KW:2 * KW]
    v = p_ref[rows, 2 * KW:2 * KW + VW]
    g = p_ref[rows, 2 * KW + VW:2 * KW + 2 * VW]
    lr = p_ref[rows, 2 * KW + 2 * VW:GLA_COLS]
    z = _dot3(lr, wg, _DIMS["nn"]) + bg
    la = (jnp.minimum(z, 0.0) - jnp.log(1.0 + jnp.exp(-jnp.abs(z)))) * (1.0 / GLA_TAU)
    cum = _dot_sel(low, la, _DIMS["nn"], "b")
    endb = cum[CHUNK - 1:CHUNK, :]
    w = jnp.exp(endb - cum)
    a_full = jnp.exp(_dot_sel(la, ones_v, _DIMS["tn"], "a"))
    return q, k, v, g, lr, z, w, endb, a_full


def _gla_fwd(p, wg, bg, gn, consts, name):
    T = p.shape[0]
    rb = CHUNK * GLA_G
    ng = T // rb
    mask, hm, low = consts[:3]
    scale = GLA_DK ** -0.5

    def body(p_ref, wg_ref, bg_ref, gn_ref, m_ref, hm_ref, l_ref, o_ref, st_ref, s_ref):
        @pl.when(pl.program_id(0) == 0)
        def _():
            s_ref[...] = jnp.zeros_like(s_ref)

        wg_v, bg_v, gn_v = wg_ref[...], bg_ref[...], gn_ref[...]
        ones_v = jnp.ones((CHUNK, VW), BF16)
        for c in range(GLA_G):
            rows = slice(c * CHUNK, (c + 1) * CHUNK)
            q, k, v, g, _, _, w, _, a_full = _gla_chunk_common(p_ref, rows, wg_v, bg_v, l_ref[...], ones_v)
            kd = (k * w).astype(BF16)
            kv = lax.dot_general(kd, v.astype(BF16), _DIMS["tn"], preferred_element_type=F32) * m_ref[...]
            s_new = a_full * s_ref[...] + kv
            s_ref[...] = s_new
            st_ref[c] = s_new
            o = jnp.dot((q * scale).astype(BF16), s_new.astype(BF16), preferred_element_type=F32)
            ms = _dot_sel(o * o, hm_ref[...], _DIMS["nn"], "a") * (1.0 / GLA_DV)
            r = lax.rsqrt(ms + EPS)
            y = o * r * gn_v * (g * _sigmoid(g))
            o_ref[rows, :] = y.astype(o_ref.dtype)

    full = lambda shape: pl.BlockSpec(shape, lambda i: tuple(0 for _ in shape))
    vm = 2 * _nbytes((rb, GLA_COLS), F32) + 2 * _nbytes((GLA_G, KW, VW), F32) + 12 * _nbytes((KW, VW), F32)
    return _pallas(
        body,
        out_shape=(_out((T, VW), BF16), _out((T // CHUNK, KW, VW), F32)),
        grid=(ng,),
        in_specs=[pl.BlockSpec((rb, GLA_COLS), lambda i: (i, 0)), full((LRW, KW)), full((1, KW)), full((1, VW)),
                  full((KW, VW)), full((VW, VW)), full((CHUNK, CHUNK))],
        out_specs=(pl.BlockSpec((rb, VW), lambda i: (i, 0)), pl.BlockSpec((GLA_G, KW, VW), lambda i: (i, 0, 0))),
        scratch_shapes=[pltpu.VMEM((KW, VW), F32)],
        compiler_params=_cparams(("arbitrary",), vm),
        name=name,
    )(p, wg, bg, gn, mask, hm, low)


def _gla_bwd(p, dy, states, wg, bg, gn, consts, name):
    T = p.shape[0]
    rb = CHUNK * GLA_G
    ng = T // rb
    mask, hm, low = consts[:3]
    scale = GLA_DK ** -0.5

    def body(p_ref, dy_ref, st_ref, sp_ref, wg_ref, bg_ref, gn_ref, m_ref, hm_ref, l_ref,
             dp_ref, dwg_ref, dbg_ref, dgn_ref, ga_ref):
        step = pl.program_id(0)

        @pl.when(step == 0)
        def _():
            ga_ref[...] = jnp.zeros_like(ga_ref)
            dwg_ref[...] = jnp.zeros_like(dwg_ref)
            dbg_ref[...] = jnp.zeros_like(dbg_ref)
            dgn_ref[...] = jnp.zeros_like(dgn_ref)

        first_group = step == ng - 1
        wg_v, bg_v, gn_v = wg_ref[...], bg_ref[...], gn_ref[...]
        ones_v = jnp.ones((CHUNK, VW), BF16)
        ones_8 = jnp.ones((8, VW), BF16)
        for c in reversed(range(GLA_G)):
            rows = slice(c * CHUNK, (c + 1) * CHUNK)
            q, k, v, g, lr, z, w, endb, a_full = _gla_chunk_common(p_ref, rows, wg_v, bg_v, l_ref[...], ones_v)
            s_n = st_ref[c]
            if c > 0:
                s_prev = st_ref[c - 1]
            else:
                s_prev = jnp.where(first_group, 0.0, sp_ref[0])
            kd = k * w
            qs = (q * scale).astype(BF16)
            o = jnp.dot(qs, s_n.astype(BF16), preferred_element_type=F32)
            ms = _dot_sel(o * o, hm_ref[...], _DIMS["nn"], "a") * (1.0 / GLA_DV)
            r = lax.rsqrt(ms + EPS)
            on = o * r
            sg = _sigmoid(g)
            silu = g * sg
            dyv = dy_ref[rows, :].astype(F32)
            d_on = dyv * gn_v * silu
            dg = dyv * on * gn_v * (sg * (1.0 + g * (1.0 - sg)))
            dgn_ref[...] += jnp.sum(dyv * on * silu, axis=0, keepdims=True)
            mo = _dot_sel(o * d_on, hm_ref[...], _DIMS["nn"], "a") * (1.0 / GLA_DV)
            do = r * d_on - o * (r * r * r) * mo
            dob = do.astype(BF16)
            dq = lax.dot_general(dob, s_n.astype(BF16), _DIMS["nt"], preferred_element_type=F32) * scale
            g_n = lax.dot_general(qs, dob, _DIMS["tn"], preferred_element_type=F32) * m_ref[...] + ga_ref[...]
            d_a = _dot_sel(ones_8, g_n * s_prev, _DIMS["nt"], "b")[0:1, :]
            g_nb = g_n.astype(BF16)
            dkd = lax.dot_general(v.astype(BF16), g_nb, _DIMS["nt"], preferred_element_type=F32)
            dv = jnp.dot(kd.astype(BF16), g_nb, preferred_element_type=F32)
            dk = dkd * w
            e = dkd * kd
            d_end = jnp.sum(e, axis=0, keepdims=True) + d_a * jnp.exp(endb)
            dla = _dot_sel(l_ref[...], -e, _DIMS["tn"], "b") + d_end
            dz = dla * (1.0 - _sigmoid(z)) * (1.0 / GLA_TAU)
            dlr = _dot3(dz, wg_v, _DIMS["nt"])
            dwg_ref[...] += _dot3(lr, dz, _DIMS["tn"])
            dbg_ref[...] += jnp.sum(dz, axis=0, keepdims=True)
            ga_ref[...] = a_full * g_n
            dp_ref[rows, 0:KW] = dq.astype(dp_ref.dtype)
            dp_ref[rows, KW:2 * KW] = dk.astype(dp_ref.dtype)
            dp_ref[rows, 2 * KW:2 * KW + VW] = dv.astype(dp_ref.dtype)
            dp_ref[rows, 2 * KW + VW:2 * KW + 2 * VW] = dg.astype(dp_ref.dtype)
            dp_ref[rows, 2 * KW + 2 * VW:GLA_COLS] = dlr.astype(dp_ref.dtype)

    full = lambda shape: pl.BlockSpec(shape, lambda i: tuple(0 for _ in shape))
    rev = lambda i: (ng - 1 - i, 0)
    vm = 4 * _nbytes((rb, GLA_COLS), F32) + 2 * _nbytes((rb, VW), F32) + 2 * _nbytes((GLA_G + 1, KW, VW), F32)
    vm += 16 * _nbytes((KW, VW), F32)
    return _pallas(
        body,
        out_shape=(_out((T, GLA_COLS), BF16), _out((LRW, KW), F32),
                   _out((1, KW), F32), _out((1, VW), F32)),
        grid=(ng,),
        in_specs=[pl.BlockSpec((rb, GLA_COLS), rev), pl.BlockSpec((rb, VW), rev),
                  pl.BlockSpec((GLA_G, KW, VW), lambda i: (ng - 1 - i, 0, 0)),
                  pl.BlockSpec((1, KW, VW), lambda i: (jnp.maximum((ng - 1 - i) * GLA_G - 1, 0), 0, 0)),
                  full((LRW, KW)), full((1, KW)), full((1, VW)), full((KW, VW)), full((VW, VW)), full((CHUNK, CHUNK))],
        out_specs=(pl.BlockSpec((rb, GLA_COLS), rev), full((LRW, KW)), full((1, KW)), full((1, VW))),
        scratch_shapes=[pltpu.VMEM((KW, VW), F32)],
        compiler_params=_cparams(("arbitrary",), vm),
        name=name,
    )(p, dy, states, states, wg, bg, gn, mask, hm, low)


CONV_TM = 512
HALO = 32
CONV_RB = 64


def _glu(u):
    a = u[:, 0:CW]
    b = u[:, CW:2 * CW]
    return a * _sigmoid(b)


def _conv_taps(buf_ref, w_ref, rb0, first_tap):
    acc = jnp.zeros((CONV_RB, CW), F32)
    for j in range(CK):
        s = rb0 + first_tap(j)
        acc = acc + w_ref[j:j + 1, :] * buf_ref[s:s + CONV_RB, :]
    return acc


def _ln_fwd(c, lg, lb):
    mu = jnp.mean(c, axis=-1, keepdims=True)
    xc = c - mu
    rstd = lax.rsqrt(jnp.mean(xc * xc, axis=-1, keepdims=True) + EPS)
    n = xc * rstd
    return n, rstd, n * lg + lb


def _conv_fwd(u, w, b, lg, lb, name):
    T = u.shape[0]
    tm = CONV_TM

    def body(u_ref, uh_ref, w_ref, b_ref, lg_ref, lb_ref, o_ref, hbuf):
        i = pl.program_id(0)
        hbuf[0:HALO, :] = jnp.where(i > 0, _glu(uh_ref[...]), 0.0)
        hbuf[HALO:HALO + tm, :] = _glu(u_ref[...])
        for r in range(tm // CONV_RB):
            acc = _conv_taps(hbuf, w_ref, r * CONV_RB, lambda j: HALO - (CK - 1) + j)
            _, _, zz = _ln_fwd(acc + b_ref[...], lg_ref[...], lb_ref[...])
            o_ref[r * CONV_RB:(r + 1) * CONV_RB, :] = (zz * _sigmoid(zz)).astype(o_ref.dtype)

    vec = pl.BlockSpec((1, CW), lambda i: (0, 0))
    return _pallas(
        body,
        out_shape=_out((T, CW), BF16),
        grid=(T // tm,),
        in_specs=[pl.BlockSpec((tm, CONV_COLS), lambda i: (i, 0)),
                  pl.BlockSpec((HALO, CONV_COLS), lambda i: (jnp.maximum(i * (tm // HALO) - 1, 0), 0)),
                  pl.BlockSpec((HALO, CW), lambda i: (0, 0)), vec, vec, vec],
        out_specs=pl.BlockSpec((tm, CW), lambda i: (i, 0)),
        scratch_shapes=[pltpu.VMEM((tm + HALO, CW), F32)],
        compiler_params=_cparams(("arbitrary",), 8 * _nbytes((tm, CONV_COLS), F32)),
        name=name,
    )(u, u, w, b, lg, lb)


def _conv_bwd_dc(u, dout, w, b, lg, lb, name):
    T = u.shape[0]
    tm = CONV_TM
    nsteps = T // tm

    def body(u_ref, uh_ref, do_ref, w_ref, b_ref, lg_ref, lb_ref, dc_ref, dw_ref, db_ref, dlg_ref, dlb_ref, hbuf, dwacc):
        i = pl.program_id(0)

        @pl.when(i == 0)
        def _():
            dwacc[...] = jnp.zeros_like(dwacc)
            db_ref[...] = jnp.zeros_like(db_ref)
            dlg_ref[...] = jnp.zeros_like(dlg_ref)
            dlb_ref[...] = jnp.zeros_like(dlb_ref)

        hbuf[0:HALO, :] = jnp.where(i > 0, _glu(uh_ref[...]), 0.0)
        hbuf[HALO:HALO + tm, :] = _glu(u_ref[...])
        for r in range(tm // CONV_RB):
            rows = slice(r * CONV_RB, (r + 1) * CONV_RB)
            acc = _conv_taps(hbuf, w_ref, r * CONV_RB, lambda j: HALO - (CK - 1) + j)
            n, rstd, zz = _ln_fwd(acc + b_ref[...], lg_ref[...], lb_ref[...])
            sg = _sigmoid(zz)
            dz = do_ref[rows, :].astype(F32) * (sg * (1.0 + zz * (1.0 - sg)))
            dlg_ref[...] += jnp.sum(dz * n, axis=0, keepdims=True)
            dlb_ref[...] += jnp.sum(dz, axis=0, keepdims=True)
            dn = dz * lg_ref[...]
            dc = rstd * (dn - jnp.mean(dn, axis=-1, keepdims=True) - n * jnp.mean(dn * n, axis=-1, keepdims=True))
            dc_ref[rows, :] = dc
            db_ref[...] += jnp.sum(dc, axis=0, keepdims=True)
            for j in range(CK):
                s = r * CONV_RB + HALO - (CK - 1) + j
                prod = dc * hbuf[s:s + CONV_RB, :]
                dwacc[j] += jnp.sum(prod.reshape(CONV_RB // 8, 8, CW), axis=0)

        @pl.when(i == nsteps - 1)
        def _():
            dw_ref[...] = jnp.sum(dwacc[...], axis=1)

    vec = pl.BlockSpec((1, CW), lambda i: (0, 0))
    return _pallas(
        body,
        out_shape=(_out((T, CW), F32), _out((HALO, CW), F32),
                   _out((1, CW), F32), _out((1, CW), F32), _out((1, CW), F32)),
        grid=(nsteps,),
        in_specs=[pl.BlockSpec((tm, CONV_COLS), lambda i: (i, 0)),
                  pl.BlockSpec((HALO, CONV_COLS), lambda i: (jnp.maximum(i * (tm // HALO) - 1, 0), 0)),
                  pl.BlockSpec((tm, CW), lambda i: (i, 0)),
                  pl.BlockSpec((HALO, CW), lambda i: (0, 0)), vec, vec, vec],
        out_specs=(pl.BlockSpec((tm, CW), lambda i: (i, 0)), pl.BlockSpec((HALO, CW), lambda i: (0, 0)), vec, vec, vec),
        scratch_shapes=[pltpu.VMEM((tm + HALO, CW), F32), pltpu.VMEM((HALO, 8, CW), F32)],
        compiler_params=_cparams(("arbitrary",), 10 * _nbytes((tm, CONV_COLS), F32)),
        name=name,
    )(u, u, dout, w, b, lg, lb)


def _conv_bwd_du(u, dc, w, name):
    T = u.shape[0]
    tm = CONV_TM
    nsteps = T // tm

    def body(u_ref, dc_ref, dch_ref, w_ref, du_ref, dcbuf):
        i = pl.program_id(0)
        dcbuf[0:tm, :] = dc_ref[...]
        dcbuf[tm:tm + HALO, :] = jnp.where(i < nsteps - 1, dch_ref[...], 0.0)
        for r in range(tm // CONV_RB):
            rows = slice(r * CONV_RB, (r + 1) * CONV_RB)
            dh = _conv_taps(dcbuf, w_ref, r * CONV_RB, lambda j: (CK - 1) - j)
            a = u_ref[rows, 0:CW]
            sb = _sigmoid(u_ref[rows, CW:2 * CW])
            du_ref[rows, 0:CW] = (dh * sb).astype(du_ref.dtype)
            du_ref[rows, CW:2 * CW] = (dh * a * sb * (1.0 - sb)).astype(du_ref.dtype)

    return _pallas(
        body,
        out_shape=_out((T, CONV_COLS), BF16),
        grid=(nsteps,),
        in_specs=[pl.BlockSpec((tm, CONV_COLS), lambda i: (i, 0)),
                  pl.BlockSpec((tm, CW), lambda i: (i, 0)),
                  pl.BlockSpec((HALO, CW), lambda i: (jnp.minimum((i + 1) * (tm // HALO), T // HALO - 1), 0)),
                  pl.BlockSpec((HALO, CW), lambda i: (0, 0))],
        out_specs=pl.BlockSpec((tm, CONV_COLS), lambda i: (i, 0)),
        scratch_shapes=[pltpu.VMEM((tm + HALO, CW), F32)],
        compiler_params=_cparams(("arbitrary",), 8 * _nbytes((tm, CONV_COLS), F32)),
        name=name,
    )(u, dc, dc, w)


ATT_G = 4


def _att_load_kv(p_any, kbuf, vbuf, sems, T):
    kc = pltpu.make_async_copy(p_any.at[:, pl.ds(AW, AW)], kbuf.at[pl.ds(LEFT, T), :], sems.at[0])
    vc = pltpu.make_async_copy(p_any.at[:, pl.ds(2 * AW, AW)], vbuf.at[pl.ds(LEFT, T), :], sems.at[1])
    kc.start()
    vc.start()
    kbuf[0:LEFT, :] = jnp.zeros((LEFT, AW), BF16)
    vbuf[0:LEFT, :] = jnp.zeros((LEFT, AW), BF16)
    kc.wait()
    vc.wait()


ATT_QB = CHUNK * ATT_G
ATT_KB = LEFT + ATT_QB
REL_PAD = 384
TOEP = 1024


def _att_consts():
    m = np.arange(TOEP)
    d = ATT_KB - 1 - m
    idx = np.clip(d, -128, 128) + 128
    sel = (np.arange(REL_PAD)[:, None] == idx[None, :]) & (m[None, :] < ATT_QB + ATT_KB - 1)
    return jnp.asarray(sel.astype(np.float32))


def _att_build_bias(rel_ref, sel_ref, bias_scr):
    tr = jnp.dot(rel_ref[...], sel_ref[...], precision=HI, preferred_element_type=F32)
    qc = lax.broadcasted_iota(jnp.int32, (ATT_QB, ATT_KB), 0) // CHUNK
    kc = lax.broadcasted_iota(jnp.int32, (ATT_QB, ATT_KB), 1) // CHUNK
    band = (kc >= qc) & (kc <= qc + 8)
    for h in range(AH):
        rows = jnp.broadcast_to(tr[h:h + 1, :], (ATT_QB, TOEP))
        toep = pltpu.roll(rows, TOEP - (ATT_QB - 1), 1, stride=1, stride_axis=0)[:, 0:ATT_KB]
        bias_scr[h // 2, (h % 2) * ATT_QB:(h % 2 + 1) * ATT_QB, :] = jnp.where(band, toep, NEG)


def _att_probs(qst, kb, bias_p, n0):
    sc = lax.dot_general(qst, kb, _DIMS["nt"], preferred_element_type=F32) * (64 ** -0.5) + bias_p
    pos = lax.broadcasted_iota(jnp.int32, (2 * ATT_QB, ATT_KB), 1)
    sc = jnp.where(pos >= CHUNK * (8 - n0), sc, NEG)
    mx = jnp.max(sc, axis=-1, keepdims=True)
    ex = jnp.exp(sc - mx)
    return ex / jnp.sum(ex, axis=-1, keepdims=True)


def _head_stack(a2, lo):
    zero = jnp.zeros_like(a2)
    return jnp.concatenate([jnp.where(lo, a2, zero), jnp.where(lo, zero, a2)], axis=0)


def _att_fwd(p, rel, sel, name):
    T = p.shape[0]

    def body(q_ref, p_any, rel_ref, sel_ref, o_ref, kbuf, vbuf, bias_scr, sems):
        i = pl.program_id(0)

        @pl.when(i == 0)
        def _():
            _att_load_kv(p_any, kbuf, vbuf, sems, T)
            _att_build_bias(rel_ref, sel_ref, bias_scr)

        lo = lax.broadcasted_iota(jnp.int32, (ATT_QB, 128), 1) < 64
        n0 = i * ATT_G
        start = pl.multiple_of(i * ATT_QB, ATT_QB)
        for hp in range(AH // 2):
            cols = slice(hp * 128, (hp + 1) * 128)
            kb = kbuf[pl.ds(start, ATT_KB), cols]
            vb = vbuf[pl.ds(start, ATT_KB), cols]
            pr = _att_probs(_head_stack(q_ref[:, cols], lo), kb, bias_scr[hp], n0)
            pv = jnp.dot(pr.astype(BF16), vb, preferred_element_type=F32)
            o_ref[:, cols] = jnp.where(lo, pv[0:ATT_QB], pv[ATT_QB:2 * ATT_QB]).astype(o_ref.dtype)

    vm = 2 * _nbytes((T + LEFT, AW), BF16) + 8 * _nbytes((2 * ATT_QB, ATT_KB), F32) + (8 << 20)
    return _pallas(
        body,
        out_shape=_out((T, AW), BF16),
        grid=(T // ATT_QB,),
        in_specs=[pl.BlockSpec((ATT_QB, AW), lambda i: (i, 0)), pl.BlockSpec(memory_space=pl.ANY),
                  pl.BlockSpec((8, REL_PAD), lambda i: (0, 0)), pl.BlockSpec((REL_PAD, TOEP), lambda i: (0, 0))],
        out_specs=pl.BlockSpec((ATT_QB, AW), lambda i: (i, 0)),
        scratch_shapes=[pltpu.VMEM((T + LEFT, AW), BF16), pltpu.VMEM((T + LEFT, AW), BF16),
                        pltpu.VMEM((AH // 2, 2 * ATT_QB, ATT_KB), F32), pltpu.SemaphoreType.DMA((2,))],
        compiler_params=_cparams(("arbitrary",), vm),
        name=name,
    )(p, p, rel, sel)


def _att_bwd(p, do, rel, sel, name):
    T = p.shape[0]
    nsteps = T // ATT_QB

    def body(q_ref, p_any, do_ref, rel_ref, sel_ref, dp_any, drel_ref,
             kbuf, vbuf, dqbuf, dkbuf, dvbuf, bias_scr, dbias_scr, dtr_scr, sems):
        i = pl.program_id(0)

        @pl.when(i == 0)
        def _():
            _att_load_kv(p_any, kbuf, vbuf, sems, T)
            _att_build_bias(rel_ref, sel_ref, bias_scr)
            dkbuf[...] = jnp.zeros_like(dkbuf)
            dvbuf[...] = jnp.zeros_like(dvbuf)
            dbias_scr[...] = jnp.zeros_like(dbias_scr)

        lo = lax.broadcasted_iota(jnp.int32, (ATT_QB, 128), 1) < 64
        n0 = i * ATT_G
        start = pl.multiple_of(i * ATT_QB, ATT_QB)
        for hp in range(AH // 2):
            cols = slice(hp * 128, (hp + 1) * 128)
            kb = kbuf[pl.ds(start, ATT_KB), cols]
            vb = vbuf[pl.ds(start, ATT_KB), cols]
            qst = _head_stack(q_ref[:, cols], lo)
            dost = _head_stack(do_ref[:, cols].astype(BF16), lo)
            pr = _att_probs(qst, kb, bias_scr[hp], n0)
            dpr = lax.dot_general(dost, vb, _DIMS["nt"], preferred_element_type=F32)
            ds = pr * (dpr - jnp.sum(dpr * pr, axis=-1, keepdims=True))
            dbias_scr[hp] += ds
            dsb = (ds * (64 ** -0.5)).astype(BF16)
            dq = jnp.dot(dsb, kb, preferred_element_type=F32)
            dqbuf[pl.ds(start, ATT_QB), cols] = jnp.where(lo, dq[0:ATT_QB], dq[ATT_QB:2 * ATT_QB]).astype(BF16)
            dkbuf[pl.ds(start, ATT_KB), cols] += lax.dot_general(dsb, qst, _DIMS["tn"], preferred_element_type=F32)
            dvbuf[pl.ds(start, ATT_KB), cols] += lax.dot_general(pr.astype(BF16), dost, _DIMS["tn"], preferred_element_type=F32)

        @pl.when(i == nsteps - 1)
        def _():
            kbuf[pl.ds(LEFT, T), :] = dkbuf[pl.ds(LEFT, T), :].astype(BF16)
            vbuf[pl.ds(LEFT, T), :] = dvbuf[pl.ds(LEFT, T), :].astype(BF16)
            cps = [pltpu.make_async_copy(dqbuf, dp_any.at[:, pl.ds(0, AW)], sems.at[0]),
                   pltpu.make_async_copy(kbuf.at[pl.ds(LEFT, T), :], dp_any.at[:, pl.ds(AW, AW)], sems.at[1]),
                   pltpu.make_async_copy(vbuf.at[pl.ds(LEFT, T), :], dp_any.at[:, pl.ds(2 * AW, AW)], sems.at[2])]
            for cp in cps:
                cp.start()
            dtr_scr[...] = jnp.zeros_like(dtr_scr)
            ri = lax.broadcasted_iota(jnp.int32, (ATT_QB, ATT_QB), 0)
            ci = lax.broadcasted_iota(jnp.int32, (ATT_QB, ATT_QB), 1)
            flip = jnp.where(ri + ci == ATT_QB - 1, 1.0, 0.0)
            for h in range(AH):
                db = dbias_scr[h // 2, (h % 2) * ATT_QB:(h % 2 + 1) * ATT_QB, :]
                db = jnp.dot(flip, db, precision=HI, preferred_element_type=F32)
                wide = jnp.concatenate([db, jnp.zeros((ATT_QB, TOEP - ATT_KB), F32)], axis=1)
                diag = pltpu.roll(wide, 0, 1, stride=1, stride_axis=0)
                dtr_scr[h:h + 1, :] = jnp.sum(diag, axis=0, keepdims=True)
            drel_ref[...] = lax.dot_general(dtr_scr[...], sel_ref[...], _DIMS["nt"], precision=HI, preferred_element_type=F32)
            for cp in cps:
                cp.wait()

    vm = 3 * _nbytes((T + LEFT, AW), BF16) + 2 * _nbytes((T + LEFT, AW), F32) + 12 * _nbytes((2 * ATT_QB, ATT_KB), F32) + (8 << 20)
    return _pallas(
        body,
        out_shape=(_out((T, ATT_COLS), BF16), _out((8, REL_PAD), F32)),
        grid=(nsteps,),
        in_specs=[pl.BlockSpec((ATT_QB, AW), lambda i: (i, 0)), pl.BlockSpec(memory_space=pl.ANY),
                  pl.BlockSpec((ATT_QB, AW), lambda i: (i, 0)),
                  pl.BlockSpec((8, REL_PAD), lambda i: (0, 0)), pl.BlockSpec((REL_PAD, TOEP), lambda i: (0, 0))],
        out_specs=(pl.BlockSpec(memory_space=pl.ANY), pl.BlockSpec((8, REL_PAD), lambda i: (0, 0))),
        scratch_shapes=[pltpu.VMEM((T + LEFT, AW), BF16), pltpu.VMEM((T + LEFT, AW), BF16), pltpu.VMEM((T, AW), BF16),
                        pltpu.VMEM((T + LEFT, AW), F32), pltpu.VMEM((T + LEFT, AW), F32),
                        pltpu.VMEM((AH // 2, 2 * ATT_QB, ATT_KB), F32), pltpu.VMEM((AH // 2, 2 * ATT_QB, ATT_KB), F32),
                        pltpu.VMEM((8, TOEP), F32), pltpu.SemaphoreType.DMA((3,))],
        compiler_params=_cparams(("arbitrary",), vm),
        name=name,
    )(p, p, do, rel, sel)


def _layer_fwd(h, wl, consts, tag):
    xn = _rmsnorm_fwd(h, wl["norm_mix"], f"{tag}_norm_mix")
    p_gla, p_conv, p_att = _mm_fan(xn, [wl["w_gla"], wl["w_conv"], wl["w_att"]], mode="nn", out_dtypes=(F32, F32, BF16),
                                   name=f"{tag}_proj")
    o_gla, states = _gla_fwd(p_gla, wl["wg"], wl["bg"], wl["gla_norm"], consts, f"{tag}_gla_fwd")
    o_conv = _conv_fwd(p_conv, wl["w_dw"], wl["b_dw"], wl["ln_g"], wl["ln_b"], f"{tag}_conv_fwd")
    rel = jnp.pad(wl["rel_bias"], ((0, 8 - AH), (0, REL_PAD - N_REL)))
    o_att = _att_fwd(p_att, rel, consts[3], f"{tag}_att_fwd")
    h1 = _mm_sum([o_gla, o_conv, o_att], [wl["w_out_g"], wl["w_out_c"], wl["w_out_a"]], mode="nn", out_dtype=F32, extra=h,
                 name=f"{tag}_out")
    xn2 = _rmsnorm_fwd(h1, wl["norm_ffn"], f"{tag}_norm_ffn")
    u = _mm(xn2, wl["w_up"], mode="nn", out_dtype=BF16, tm=1024, tn=1024, b_chips=True, name=f"{tag}_mlp_up")
    h2 = _mm(u, wl["w_down"], mode="nn", out_dtype=F32, tm=1024, tk=1024, a_pro="relu2", epi="add", extra=h1,
             name=f"{tag}_mlp_down")
    saved = dict(h=h, xn=xn, p_gla=p_gla, p_conv=p_conv, p_att=p_att, states=states, o_gla=o_gla, o_conv=o_conv,
                 o_att=o_att, rel=rel, h1=h1, xn2=xn2, u=u)
    return h2, saved


def _layer_bwd(dh2, sv, wl, consts, tag, emit=lambda name, grad: None):
    g = {}
    du = _mm(dh2, wl["w_down"], mode="nt", out_dtype=BF16, tm=1024, tn=1024, epi="relu2grad", extra=sv["u"],
             name=f"{tag}_mlp_down_dx")
    g["w_down"] = _mm(sv["u"], dh2, mode="tn", out_dtype=F32, tm=2048, tn=1024, tk=512, a_pro="relu2", name=f"{tag}_mlp_down_dw")
    emit("w_down", g["w_down"].reshape(4, D_FF // 4, D))
    dxn2 = _mm(du, wl["w_up"], mode="nt", out_dtype=F32, tm=1024, tk=1024, b_chips=True, name=f"{tag}_mlp_up_dx")
    g["w_up"] = _mm(sv["xn2"], du, mode="tn", out_dtype=F32, tm=1024, tn=1024, tk=512, out_chips=True, name=f"{tag}_mlp_up_dw")
    emit("w_up", g["w_up"])
    dh1, g["norm_ffn"] = _rmsnorm_bwd(dxn2, sv["h1"], wl["norm_ffn"], dh2, f"{tag}_norm_ffn_bwd")
    d_gla, d_conv, d_att = _mm_fan(dh1, [wl["w_out_g"], wl["w_out_c"], wl["w_out_a"]], mode="nt", out_dtypes=(F32, F32, F32),
                                   name=f"{tag}_out_dx")
    g["w_out_g"], g["w_out_c"], g["w_out_a"] = _mm_tn_multi([sv["o_gla"], sv["o_conv"], sv["o_att"], dh1],
                                                            [(0, 3), (1, 3), (2, 3)], name=f"{tag}_out_dw")
    emit("w_out", jnp.concatenate([g["w_out_g"], g["w_out_c"], g["w_out_a"]], axis=0).reshape(4, D // 4, D))
    dp_gla, g["wg"], g["bg"], g["gla_norm"] = _gla_bwd(sv["p_gla"], d_gla, sv["states"], wl["wg"], wl["bg"], wl["gla_norm"],
                                                       consts, f"{tag}_gla_bwd")
    dc, g["w_dw"], g["b_dw"], g["ln_g"], g["ln_b"] = _conv_bwd_dc(sv["p_conv"], d_conv, wl["w_dw"], wl["b_dw"], wl["ln_g"],
                                                                  wl["ln_b"], f"{tag}_conv_bwd_dc")
    dp_conv = _conv_bwd_du(sv["p_conv"], dc, wl["w_dw"], f"{tag}_conv_bwd_du")
    dp_att, drel = _att_bwd(sv["p_att"], d_att, sv["rel"], consts[3], f"{tag}_att_bwd")
    g["rel_bias"] = drel[0:AH, 0:N_REL]
    g["w_gla"], g["w_conv"], g["w_att"] = _mm_tn_multi([sv["xn"], dp_gla, dp_conv, dp_att], [(0, 1), (0, 2), (0, 3)],
                                                       name=f"{tag}_proj_dw")
    emit("w_in", jnp.transpose(_join_w_in(g).reshape(D, 4, -1), (1, 0, 2)))
    dxn = _mm_sum([dp_gla, dp_conv, dp_att], [wl["w_gla"], wl["w_conv"], wl["w_att"]], mode="nt", out_dtype=F32,
                  name=f"{tag}_proj_dx")
    dh, g["norm_mix"] = _rmsnorm_bwd(dxn, sv["h"], wl["norm_mix"], dh1, f"{tag}_norm_mix_bwd")
    return dh, g


def _local_step(x, target, layers, norm_final, emit=lambda layer, name, grad: None):
    consts = _gla_consts() + (_att_consts(),)
    h = x
    saved = []
    for l, wl in enumerate(layers):
        h, sv = _layer_fwd(h, wl, consts, f"l{l}")
        saved.append(sv)
    loss, dh, g_final = _final_loss(h, norm_final, target, "final_loss")
    grads = [None] * len(layers)
    for l in reversed(range(len(layers))):
        dh, grads[l] = _layer_bwd(dh, saved[l], layers[l], consts, f"l{l}", functools.partial(emit, l))
    return loss, dh, grads, g_final


ANY = pl.BlockSpec(memory_space=pl.ANY)


def _place():
    x, y, c = lax.axis_index("x"), lax.axis_index("y"), lax.axis_index("c")
    chips = [(1 - x, y), (x, 1 - y), (1 - x, 1 - y)]
    return x, y, c, chips


def _shape(shape, dtype):
    return jax.ShapeDtypeStruct(tuple(shape), dtype)


def _remote(src, dst, send_sem, recv_sem, to):
    return pltpu.make_async_remote_copy(src_ref=src, dst_ref=dst, send_sem=send_sem, recv_sem=recv_sem,
                                        device_id=to, device_id_type=MESH)


def _phase_gather_ici(src, then):
    R, C = src.shape
    rh = R // 2

    def copies(ins, outs, sems):
        x, y, c, chips = _place()
        me = 2 * x + y
        local = pltpu.make_async_copy(ins[0], outs[0].at[me], sems[2].at[0])
        sends = [_remote(ins[0].at[pl.ds(c * rh, rh), :], outs[0].at[me, pl.ds(c * rh, rh), :], sems[0].at[k], sems[1].at[k],
                         (px, py, c)) for k, (px, py) in enumerate(chips)]
        recvs = [_remote(outs[0].at[2 * px + py, pl.ds(c * rh, rh), :], outs[0].at[2 * px + py, pl.ds(c * rh, rh), :],
                         sems[0].at[k], sems[1].at[k], (px, py, c)) for k, (px, py) in enumerate(chips)]
        return local, sends, recvs

    def start(ins, outs, sems):
        local, sends, _ = copies(ins, outs, sems)
        local.start()
        for cp in sends:
            cp.start()

    def finish(ins, outs, sems):
        local, sends, recvs = copies(ins, outs, sems)
        for cp in recvs:
            cp.wait_recv()
        for cp in sends:
            cp.wait_send()
        local.wait()

    return _Comm([src], [_shape((4, R, C), src.dtype)], {}, [(3,), (3,), (1,)], start, finish, then)


def _phase_gather_d2d(part, then):
    _, R, C = part.shape
    rh = R // 2

    def copies(ins, outs, sems):
        x, y, c, chips = _place()
        sends = [_remote(ins[0].at[2 * px + py, pl.ds(c * rh, rh), :], outs[0].at[2 * px + py, pl.ds(c * rh, rh), :],
                         sems[0].at[k], sems[1].at[k], (x, y, 1 - c)) for k, (px, py) in enumerate(chips)]
        recvs = [_remote(outs[0].at[2 * px + py, pl.ds((1 - c) * rh, rh), :], outs[0].at[2 * px + py, pl.ds((1 - c) * rh, rh), :],
                         sems[0].at[k], sems[1].at[k], (x, y, 1 - c)) for k, (px, py) in enumerate(chips)]
        return sends, recvs

    def start(ins, outs, sems):
        for cp in copies(ins, outs, sems)[0]:
            cp.start()

    def finish(ins, outs, sems):
        sends, recvs = copies(ins, outs, sems)
        for cp in recvs:
            cp.wait_recv()
        for cp in sends:
            cp.wait_send()

    return _Comm([part], [_shape(part.shape, part.dtype)], {0: 0}, [(3,), (3,)], start, finish, then)


def _phase_pair_exchange(full, then):
    _, R, C = full.shape
    rh = R // 2

    def copy(ins, outs, sems):
        x, y, c, _ = _place()
        return _remote(ins[0].at[:, pl.ds((1 - c) * rh, rh), :], outs[0], sems[0].at[0], sems[1].at[0], (x, y, 1 - c))

    return _Comm([full], [_shape((4, rh, C), full.dtype)], {}, [(1,), (1,)],
                 lambda ins, outs, sems: copy(ins, outs, sems).start(),
                 lambda ins, outs, sems: copy(ins, outs, sems).wait(), then)


def _phase_chip_scatter(parts, then):
    def copies(ins, outs, sems):
        x, y, c, chips = _place()
        me = 2 * x + y
        local = pltpu.make_async_copy(ins[0].at[me], outs[0].at[me], sems[2].at[0])
        sends = [_remote(ins[0].at[2 * px + py], outs[0].at[me], sems[0].at[k], sems[1].at[k], (px, py, c))
                 for k, (px, py) in enumerate(chips)]
        recvs = [_remote(outs[0].at[2 * px + py], outs[0].at[2 * px + py], sems[0].at[k], sems[1].at[k], (px, py, c))
                 for k, (px, py) in enumerate(chips)]
        return local, sends, recvs

    def start(ins, outs, sems):
        local, sends, _ = copies(ins, outs, sems)
        local.start()
        for cp in sends:
            cp.start()

    def finish(ins, outs, sems):
        local, sends, recvs = copies(ins, outs, sems)
        for cp in recvs:
            cp.wait_recv()
        for cp in sends:
            cp.wait_send()
        local.wait()

    return _Comm([parts], [_shape(parts.shape, parts.dtype)], {}, [(3,), (3,), (1,)], start, finish, then)


def _phase_pair_allgather(half, layer, depth, into, then):
    rh, C = half.shape

    def copies(ins, outs, sems):
        x, y, c, _ = _place()
        mine = outs[0].at[layer, pl.ds(c * rh, rh), :]
        theirs = outs[0].at[layer, pl.ds((1 - c) * rh, rh), :]
        return (pltpu.make_async_copy(ins[0], mine, sems[2].at[0]),
                _remote(ins[0], mine, sems[0].at[0], sems[1].at[0], (x, y, 1 - c)),
                _remote(theirs, theirs, sems[0].at[0], sems[1].at[0], (x, y, 1 - c)))

    def start(ins, outs, sems):
        local, send, _ = copies(ins, outs, sems)
        local.start()
        send.start()

    def finish(ins, outs, sems):
        local, send, recv = copies(ins, outs, sems)
        recv.wait_recv()
        send.wait_send()
        local.wait()

    ins = [half] if into is None else [half, into]
    return _Comm(ins, [_shape((depth, 2 * rh, C), half.dtype)], {} if into is None else {1: 0}, [(1,), (1,), (1,)],
                 start, finish, then)


def _comm_only(comms, name):
    plan = _Plan()
    for c in comms:
        plan.at(name, c)
    saved, _PLAN[0] = _PLAN[0], plan
    try:
        def body(o_ref):
            o_ref[...] = jnp.zeros_like(o_ref)

        _pallas(body, out_shape=[jax.ShapeDtypeStruct((8, 128), F32)], in_specs=[],
                out_specs=[pl.BlockSpec(memory_space=pltpu.VMEM)], name=name)()
    finally:
        _PLAN[0] = saved


def _gather_chips(items):
    n = len(items)

    def body(*refs):
        srcs, outs = refs[:n], refs[n:2 * n]
        send_sems, recv_sems, loc_sems = refs[2 * n:]
        x, y, c, chips = _place()
        me = 2 * x + y

        def half(t, chip, cc):
            rh = items[t].shape[1] // 2
            return outs[t].at[chip, :, pl.ds(cc * rh, rh), :]

        def copy(t, k, src, dst, to):
            return pltpu.make_async_remote_copy(src_ref=src, dst_ref=dst, send_sem=send_sems.at[t, k],
                                                recv_sem=recv_sems.at[t, k], device_id=to, device_id_type=MESH)

        local = [pltpu.make_async_copy(srcs[t], outs[t].at[me], loc_sems.at[t]) for t in range(n)]
        for cp in local:
            cp.start()
        sent = []
        for t in range(n):
            rh = items[t].shape[1] // 2
            mine = srcs[t].at[:, pl.ds(c * rh, rh), :]
            for k, (px, py) in enumerate(chips):
                sent.append(copy(t, k, mine, half(t, me, c), (px, py, c)))
                sent[-1].start()
        for k, (px, py) in enumerate(chips):
            for t in range(n):
                blk = half(t, 2 * px + py, c)
                copy(t, k, blk, blk, (px, py, c)).wait_recv()
                sent.append(copy(t, 3 + k, blk, blk, (x, y, 1 - c)))
                sent[-1].start()
        for k, (px, py) in enumerate(chips):
            for t in range(n):
                blk = half(t, 2 * px + py, 1 - c)
                copy(t, 3 + k, blk, blk, (x, y, 1 - c)).wait_recv()
        for cp in sent:
            cp.wait_send()
        for cp in local:
            cp.wait()

    return _pallas(
        body,
        out_shape=[_out((4,) + a.shape, a.dtype) for a in items],
        in_specs=[ANY] * n,
        out_specs=[ANY] * n,
        scratch_shapes=[pltpu.SemaphoreType.DMA((n, 6)), pltpu.SemaphoreType.DMA((n, 6)), pltpu.SemaphoreType.DMA((n,))],
        name="gather_weights",
    )(*items)


def _pair_exchange(items):
    n = len(items)

    def body(*refs):
        srcs, outs = refs[:n], refs[n:2 * n]
        send_sems, recv_sems = refs[2 * n:]
        x, y, c, _ = _place()
        cps = []
        for t in range(n):
            rh = items[t].shape[2] // 2
            cps.append(pltpu.make_async_remote_copy(
                src_ref=srcs[t].at[:, :, pl.ds((1 - c) * rh, rh), :], dst_ref=outs[t], send_sem=send_sems.at[t],
                recv_sem=recv_sems.at[t], device_id=(x, y, 1 - c), device_id_type=MESH))
            cps[-1].start()
        for cp in cps:
            cp.wait()

    return _pallas(
        body,
        out_shape=[_out(a.shape[:2] + (a.shape[2] // 2, a.shape[3]), a.dtype) for a in items],
        in_specs=[ANY] * n,
        out_specs=[ANY] * n,
        scratch_shapes=[pltpu.SemaphoreType.DMA((n,)), pltpu.SemaphoreType.DMA((n,))],
        name="reduce_pair_exchange",
    )(*items)


def _row_tile(rows, cols, itemsize=4, budget=1 << 20):
    t = rows
    while t % 2 == 0 and t // 2 >= 8 and (t // 2) % 8 == 0 and t * cols * itemsize > budget:
        t //= 2
    return t


def _pair_add(full, got, c, name):
    _, L, R, C = full.shape
    rh = R // 2
    tr = _row_tile(rh, C)
    nb = rh // tr

    def body(c_ref, a_ref, b_ref, o_ref):
        o_ref[...] = (a_ref[...] + b_ref[...]).astype(o_ref.dtype)

    grid_spec = pltpu.PrefetchScalarGridSpec(
        num_scalar_prefetch=1,
        grid=(4, L, nb),
        in_specs=[pl.BlockSpec((1, 1, tr, C), lambda j, l, i, c_ref: (j, l, c_ref[0] * nb + i, 0)),
                  pl.BlockSpec((1, 1, tr, C), lambda j, l, i, c_ref: (j, l, i, 0))],
        out_specs=pl.BlockSpec((1, 1, tr, C), lambda j, l, i, c_ref: (j, l, i, 0)),
    )
    return _pallas(
        body,
        out_shape=_out((4, L, rh, C), BF16),
        grid_spec=grid_spec,
        compiler_params=_cparams(("parallel", "parallel", "parallel"), 8 * tr * C * 4),
        name=name,
    )(jnp.reshape(c, (1,)).astype(jnp.int32), full, got)


def _chip_scatter(items):
    n = len(items)

    def body(*refs):
        srcs, outs = refs[:n], refs[n:2 * n]
        send_sems, recv_sems, loc_sems = refs[2 * n:]
        x, y, c, chips = _place()
        me = 2 * x + y
        local = [pltpu.make_async_copy(srcs[t].at[me], outs[t].at[me], loc_sems.at[t]) for t in range(n)]
        for cp in local:
            cp.start()
        cps = []
        for t in range(n):
            for k, (px, py) in enumerate(chips):
                cps.append(pltpu.make_async_remote_copy(
                    src_ref=srcs[t].at[2 * px + py], dst_ref=outs[t].at[me], send_sem=send_sems.at[t, k],
                    recv_sem=recv_sems.at[t, k], device_id=(px, py, c), device_id_type=MESH))
                cps[-1].start()
        for t in range(n):
            for k, (px, py) in enumerate(chips):
                blk = outs[t].at[2 * px + py]
                pltpu.make_async_remote_copy(src_ref=blk, dst_ref=blk, send_sem=send_sems.at[t, k], recv_sem=recv_sems.at[t, k],
                                             device_id=(px, py, c), device_id_type=MESH).wait_recv()
        for cp in cps:
            cp.wait_send()
        for cp in local:
            cp.wait()

    return _pallas(
        body,
        out_shape=[_out(a.shape, a.dtype) for a in items],
        in_specs=[ANY] * n,
        out_specs=[ANY] * n,
        scratch_shapes=[pltpu.SemaphoreType.DMA((n, 3)), pltpu.SemaphoreType.DMA((n, 3)), pltpu.SemaphoreType.DMA((n,))],
        name="reduce_chip_scatter",
    )(*items)


def _sum_chips(parts, name):
    _, L, rh, C = parts.shape
    tr = _row_tile(rh, C)

    def body(p_ref, o_ref):
        acc = p_ref[0].astype(F32)
        for j in range(1, 4):
            acc = acc + p_ref[j].astype(F32)
        o_ref[...] = acc

    return _pallas(
        body,
        out_shape=_out((L, rh, C), F32),
        grid=(L, rh // tr),
        in_specs=[pl.BlockSpec((4, 1, tr, C), lambda l, i: (0, l, i, 0))],
        out_specs=pl.BlockSpec((1, tr, C), lambda l, i: (l, i, 0)),
        compiler_params=_cparams(("parallel", "parallel"), 16 * tr * C * 4),
        name=name,
    )(parts)


def _pair_allgather(groups):
    flat = [(w, l) for w, grp in enumerate(groups) for l in range(len(grp))]
    items = [groups[w][l] for w, l in flat]
    n, nw = len(items), len(groups)

    def body(*refs):
        srcs, outs = refs[:n], refs[n:n + nw]
        send_sems, recv_sems, loc_sems = refs[n + nw:]
        x, y, c, _ = _place()
        cps, local = [], []
        for t, (w, l) in enumerate(flat):
            rh = items[t].shape[1]
            mine = outs[w].at[pl.ds(l, 1), pl.ds(c * rh, rh), :]
            local.append(pltpu.make_async_copy(srcs[t], mine, loc_sems.at[t]))
            local[-1].start()
            cps.append(pltpu.make_async_remote_copy(src_ref=srcs[t], dst_ref=mine, send_sem=send_sems.at[t],
                                                    recv_sem=recv_sems.at[t], device_id=(x, y, 1 - c), device_id_type=MESH))
            cps[-1].start()
        for t, (w, l) in enumerate(flat):
            rh = items[t].shape[1]
            theirs = outs[w].at[pl.ds(l, 1), pl.ds((1 - c) * rh, rh), :]
            pltpu.make_async_remote_copy(src_ref=theirs, dst_ref=theirs, send_sem=send_sems.at[t], recv_sem=recv_sems.at[t],
                                         device_id=(x, y, 1 - c), device_id_type=MESH).wait_recv()
        for cp in cps:
            cp.wait_send()
        for cp in local:
            cp.wait()

    return _pallas(
        body,
        out_shape=[_out((len(grp), 2 * grp[0].shape[1], grp[0].shape[2]), grp[0].dtype) for grp in groups],
        in_specs=[ANY] * n,
        out_specs=[ANY] * nw,
        scratch_shapes=[pltpu.SemaphoreType.DMA((n,)), pltpu.SemaphoreType.DMA((n,)), pltpu.SemaphoreType.DMA((n,))],
        name="reduce_pair_allgather",
    )(*items)


def _allreduce_small(v):
    R = v.shape[0]

    def body(v_ref, o_ref, slots, send_sems, recv_sems):
        x, y, c, _ = _place()
        me = 4 * x + 2 * y + c
        slots[me] = v_ref[...]
        cps = []
        for r in range(1, 8):
            px, py, pc = x ^ (r >> 2), y ^ ((r >> 1) & 1), c ^ (r & 1)
            cps.append(pltpu.make_async_remote_copy(src_ref=v_ref, dst_ref=slots.at[me], send_sem=send_sems.at[r - 1],
                                                    recv_sem=recv_sems.at[r - 1], device_id=(px, py, pc), device_id_type=MESH))
            cps[-1].start()
        for r in range(1, 8):
            px, py, pc = x ^ (r >> 2), y ^ ((r >> 1) & 1), c ^ (r & 1)
            theirs = slots.at[4 * px + 2 * py + pc]
            pltpu.make_async_remote_copy(src_ref=theirs, dst_ref=theirs, send_sem=send_sems.at[r - 1], recv_sem=recv_sems.at[r - 1],
                                         device_id=(px, py, pc), device_id_type=MESH).wait_recv()
        acc = slots[0]
        for j in range(1, 8):
            acc = acc + slots[j]
        o_ref[...] = acc
        for cp in cps:
            cp.wait_send()

    return pl.pallas_call(
        body,
        out_shape=jax.ShapeDtypeStruct(v.shape, F32),
        in_specs=[pl.BlockSpec(memory_space=pltpu.VMEM)],
        out_specs=pl.BlockSpec(memory_space=pltpu.VMEM),
        scratch_shapes=[pltpu.VMEM((8, R, 128), F32), pltpu.SemaphoreType.DMA((7,)), pltpu.SemaphoreType.DMA((7,))],
        name="allreduce_small",
    )(v)


def _adamw_math(w, g, m, v):
    m = ADAM_B1 * m + (1.0 - ADAM_B1) * g
    v = ADAM_B2 * v + (1.0 - ADAM_B2) * (g * g)
    m_hat = m / (1.0 - ADAM_B1 ** ADAM_STEP)
    v_hat = v / (1.0 - ADAM_B2 ** ADAM_STEP)
    delta = -ADAM_LR * (m_hat / (jnp.sqrt(v_hat) + ADAM_EPS) + ADAM_WD * w)
    return delta, m, v


def _adamw(ws, gs, ms, vs, name):
    n = len(ws)
    tiles = [_row_tile(w.shape[1], w.shape[2], budget=1 << 19) for w in ws]
    per_layer = [w.shape[1] // t for w, t in zip(ws, tiles)]
    steps = [w.shape[0] * p for w, p in zip(ws, per_layer)]
    starts = [sum(steps[:k]) for k in range(n)]

    def body(*refs):
        i = pl.program_id(0)
        for k in range(n):
            w_ref, g_ref, m_ref, v_ref = (refs[j * n + k] for j in range(4))
            outs = [refs[(4 + j) * n + k] for j in range(3)]

            @pl.when((i >= starts[k]) & (i < starts[k] + steps[k]))
            def _(w_ref=w_ref, g_ref=g_ref, m_ref=m_ref, v_ref=v_ref, outs=outs):
                outs[0][...], outs[1][...], outs[2][...] = _adamw_math(w_ref[...], g_ref[...], m_ref[...], v_ref[...])

    def spec(k):
        def index(i):
            local = jnp.clip(i - starts[k], 0, steps[k] - 1)
            return local // per_layer[k], local % per_layer[k], 0
        return pl.BlockSpec((None, tiles[k], ws[k].shape[2]), index)

    specs = [spec(k) for k in range(n)]
    outs = [_out(w.shape, F32) for w in ws]
    res = _pallas(
        body,
        out_shape=tuple(outs * 3),
        grid=(sum(steps),),
        in_specs=specs * 4,
        out_specs=tuple(specs * 3),
        compiler_params=_cparams(("arbitrary",), sum(16 * t * w.shape[2] * 4 for w, t in zip(ws, tiles))),
        name=name,
    )(*ws, *gs, *ms, *vs)
    return res[:n], res[n:2 * n], res[2 * n:]


def _adamw_small(ws, gs, ms, vs):
    n = len(ws)

    def body(*refs):
        for t in range(n):
            w_ref, g_ref, m_ref, v_ref = (refs[k * n + t] for k in range(4))
            d_ref, nm_ref, nv_ref = (refs[(4 + k) * n + t] for k in range(3))
            d_ref[...], nm_ref[...], nv_ref[...] = _adamw_math(w_ref[...], g_ref[...], m_ref[...], v_ref[...])

    vmem = pl.BlockSpec(memory_space=pltpu.VMEM)
    outs = [jax.ShapeDtypeStruct(w.shape, F32) for w in ws]
    res = pl.pallas_call(
        body,
        out_shape=outs * 3,
        in_specs=[vmem] * (4 * n),
        out_specs=[vmem] * (3 * n),
        name="adamw_small",
    )(*ws, *gs, *ms, *vs)
    return res[:n], res[n:2 * n], res[2 * n:]


IN_SIZES = (192, 192, 384, 384, 16, 512, 384, 384, 384)
IN_OFFS = tuple(int(v) for v in np.cumsum((0,) + IN_SIZES))
SMALL = ("norm_mix", "w_gla_gate", "b_gla_gate", "gla_norm", "b_dw", "conv_ln_g", "conv_ln_b", "rel_bias", "norm_ffn")


def _pad_cols(a, n):
    return jnp.pad(a, ((0, 0), (0, n - a.shape[1])))


def _split_w_in(w):
    s = [w[:, IN_OFFS[i]:IN_OFFS[i + 1]] for i in range(9)]
    w_gla = jnp.concatenate([_pad_cols(s[0], KW), _pad_cols(s[1], KW), s[2], s[3], _pad_cols(s[4], LRW)], axis=1)
    return w_gla, s[5], jnp.concatenate(s[6:9], axis=1)


def _join_w_in(g):
    gg = g["w_gla"]
    return jnp.concatenate([gg[:, 0:192], gg[:, KW:KW + 192], gg[:, 2 * KW:2 * KW + VW], gg[:, 2 * KW + VW:2 * KW + 2 * VW],
                            gg[:, 2 * KW + 2 * VW:2 * KW + 2 * VW + 16], g["w_conv"], g["w_att"]],
                           axis=1)


def _pack(arrs, rows):
    flat = jnp.concatenate([a.reshape(-1) for a in arrs])
    return jnp.pad(flat, (0, rows * 128 - flat.shape[0])).reshape(rows, 128)


def _unpack(packed, shapes):
    flat = packed.reshape(-1)
    out, off = [], 0
    for s in shapes:
        n = int(np.prod(s))
        out.append(flat[off:off + n].reshape(s))
        off += n
    return out


def kernel(x, norm_mix, w_in, w_gla_gate, b_gla_gate, gla_norm, w_dw, b_dw, conv_ln_g, conv_ln_b, rel_bias, w_out, norm_ffn, w_up, w_down, norm_final, loss_target, m_norm_mix, m_w_in, m_w_gla_gate, m_b_gla_gate, m_gla_norm, m_w_dw, m_b_dw, m_conv_ln_g, m_conv_ln_b, m_rel_bias, m_w_out, m_norm_ffn, m_w_up, m_w_down, m_norm_final, v_norm_mix, v_w_in, v_w_gla_gate, v_b_gla_gate, v_gla_norm, v_w_dw, v_b_dw, v_conv_ln_g, v_conv_ln_b, v_rel_bias, v_w_out, v_norm_ffn, v_w_up, v_w_down, v_norm_final):
    P = dict(norm_mix=norm_mix, w_in=w_in, w_gla_gate=w_gla_gate, b_gla_gate=b_gla_gate, gla_norm=gla_norm, w_dw=w_dw, b_dw=b_dw,
             conv_ln_g=conv_ln_g, conv_ln_b=conv_ln_b, rel_bias=rel_bias, w_out=w_out, norm_ffn=norm_ffn, w_up=w_up,
             w_down=w_down, norm_final=norm_final)
    Mo = dict(norm_mix=m_norm_mix, w_in=m_w_in, w_gla_gate=m_w_gla_gate, b_gla_gate=m_b_gla_gate, gla_norm=m_gla_norm, w_dw=m_w_dw,
              b_dw=m_b_dw, conv_ln_g=m_conv_ln_g, conv_ln_b=m_conv_ln_b, rel_bias=m_rel_bias, w_out=m_w_out, norm_ffn=m_norm_ffn,
              w_up=m_w_up, w_down=m_w_down, norm_final=m_norm_final)
    Vo = dict(norm_mix=v_norm_mix, w_in=v_w_in, w_gla_gate=v_w_gla_gate, b_gla_gate=v_b_gla_gate, gla_norm=v_gla_norm, w_dw=v_w_dw,
              b_dw=v_b_dw, conv_ln_g=v_conv_ln_g, conv_ln_b=v_conv_ln_b, rel_bias=v_rel_bias, w_out=v_w_out, norm_ffn=v_norm_ffn,
              w_up=v_w_up, w_down=v_w_down, norm_final=v_norm_final)
    depth = w_in.shape[0]
    xi, yi, ci = lax.axis_index("x"), lax.axis_index("y"), lax.axis_index("c")
    chip = 2 * xi + yi

    plan = _Plan()
    _PLAN[0] = plan
    layers = [dict(
        norm_mix=norm_mix[l][None], wg=jnp.pad(w_gla_gate[l], ((0, LRW - 16), (0, KW - 192))),
        bg=_pad_cols(b_gla_gate[l][None], KW), gla_norm=gla_norm[l][None], b_dw=b_dw[l][None], ln_g=conv_ln_g[l][None],
        ln_b=conv_ln_b[l][None], rel_bias=rel_bias[l], norm_ffn=norm_ffn[l][None]) for l in range(depth)]

    def have_w_in(l, full):
        layers[l]["w_gla"], layers[l]["w_conv"], layers[l]["w_att"] = _split_w_in(jnp.transpose(full, (1, 0, 2)).reshape(D, -1))

    def have_w_out(l, full):
        w = full.reshape(D, D)
        layers[l]["w_out_g"], layers[l]["w_out_c"], layers[l]["w_out_a"] = w[0:VW], w[VW:VW + CW], w[VW + CW:]

    def have_w_up(l, full):
        layers[l]["w_up"] = full

    def have_w_down(l, full):
        layers[l]["w_down"] = full.reshape(D_FF, D)

    w_dw_pad = jnp.pad(w_dw, ((0, 0), (0, HALO - CK), (0, 0)))
    first_in, g_dw = _gather_chips([w_in[0:1].astype(BF16), w_dw_pad])
    have_w_in(0, first_in[:, 0])
    for l in range(depth):
        layers[l]["w_dw"] = jnp.transpose(g_dw[:, l], (1, 0, 2)).reshape(HALO, CW)

    def gather_behind(shard, ici_call, d2d_call, have):
        plan.at(ici_call, _phase_gather_ici(
            shard, lambda outs: plan.at(d2d_call, _phase_gather_d2d(outs[0], lambda done: have(done[0])))))

    for l in range(depth):
        if l > 0:
            gather_behind(w_in[l].astype(BF16), f"l{l - 1}_mlp_up", f"l{l - 1}_mlp_down", functools.partial(have_w_in, l))
        gather_behind(w_out[l].astype(BF16), f"l{l - 1}_mlp_down" if l > 0 else "l0_proj", f"l{l}_gla_fwd",
                      functools.partial(have_w_out, l))
        if l > 0:
            gather_behind(w_up[l].astype(BF16), f"l{l}_proj", f"l{l}_gla_fwd", functools.partial(have_w_up, l))
            gather_behind(w_down[l].astype(BF16), f"l{l}_gla_fwd", f"l{l}_att_fwd", functools.partial(have_w_down, l))
        else:
            gather_behind(w_up[l].astype(BF16), f"l{l}_gla_fwd", f"l{l}_att_fwd", functools.partial(have_w_up, l))
            gather_behind(w_down[l].astype(BF16), f"l{l}_att_fwd", f"l{l}_mlp_up", functools.partial(have_w_down, l))

    reduced = {}
    last_swap = []

    def reduce_calls(name, l):
        if name == "w_down":
            return f"l{l}_mlp_up_dx", f"l{l}_gla_bwd", f"l{l}_conv_bwd_dc"
        if name == "w_up":
            return f"l{l}_out_dx", f"l{l}_att_bwd", f"l{l}_proj_dw"
        if name == "w_out":
            return f"l{l}_gla_bwd", f"l{l}_conv_bwd_dc", f"l{l}_att_bwd"
        if l > 0:
            return f"l{l}_proj_dx", f"l{l - 1}_mlp_down_dw", f"l{l - 1}_mlp_up_dx"
        return "l0_proj_dx", "adamw_early", None

    def reduce_behind(l, name, full):
        calls = reduce_calls(name, l)

        def swapped(outs):
            pair = _pair_add(full[:, None], outs[0][:, None], ci, f"reduce_pair_add_{name}{l}")[:, 0]
            plan.at(calls[1], _phase_chip_scatter(pair, scattered))

        def scattered(outs):
            half = _sum_chips(outs[0][:, None], f"reduce_sum_chips_{name}{l}")[0]
            phase = _phase_pair_allgather(half, l, depth, reduced.get(name), gathered)
            if calls[2] is None:
                last_swap.append(phase)
            else:
                plan.at(calls[2], phase)

        def gathered(outs):
            reduced[name] = outs[0]

        plan.at(calls[0], _phase_pair_exchange(full, swapped))

    loss_part, grad_x, grads, g_final = _local_step(x[0], loss_target[0], layers, norm_final[None], reduce_behind)
    loss = lax.psum(loss_part[0, 0], ("x", "y", "c"))

    G, delta, new_m, new_v = {}, {}, {}, {}
    early = ("w_down", "w_up", "w_out")
    for name in early:
        G[name] = reduced[name]
    ds, nms, nvs = _adamw([P[k] for k in early], [G[k] for k in early], [Mo[k] for k in early], [Vo[k] for k in early],
                          "adamw_early")
    for i, name in enumerate(early):
        delta[name], new_m[name], new_v[name] = ds[i], nms[i], nvs[i]
    _PLAN[0] = None
    assert not plan.by_call, sorted(plan.by_call)

    small_g = []
    for l in range(depth):
        g = grads[l]
        small_g += [g["norm_mix"], g["wg"][0:16, 0:192], g["bg"][:, 0:192], g["gla_norm"], g["b_dw"], g["ln_g"], g["ln_b"],
                    g["rel_bias"], g["norm_ffn"], g["w_dw"][0:CK]]
    small_g.append(g_final)
    small_shapes = [a.shape for a in small_g]
    n_small = sum(int(np.prod(s)) for s in small_shapes)
    rows = -(-n_small // 1024) * 8
    red = _unpack(_allreduce_small(_pack(small_g, rows)), small_shapes)
    per = len(SMALL) + 1
    for i, name in enumerate(SMALL):
        G[name] = jnp.stack([red[l * per + i].reshape(P[name].shape[1:]) for l in range(depth)])
    gw_dw_all = jnp.stack([red[l * per + len(SMALL)] for l in range(depth)])
    G["w_dw"] = lax.dynamic_slice_in_dim(gw_dw_all, chip * (CW // 4), CW // 4, axis=2)
    G["norm_final"] = red[-1].reshape(norm_final.shape)

    _comm_only(last_swap, "reduce_pair_allgather_last")
    G["w_in"] = reduced["w_in"]
    ds, nms, nvs = _adamw([w_in], [G["w_in"]], [m_w_in], [v_w_in], "adamw_w_in")
    delta["w_in"], new_m["w_in"], new_v["w_in"] = ds[0], nms[0], nvs[0]

    small_names = list(SMALL) + ["w_dw", "norm_final"]
    two_d = lambda a: a.reshape(-1, a.shape[-1])
    ds, nms, nvs = _adamw_small([two_d(P[k]) for k in small_names], [two_d(G[k]) for k in small_names],
                                [two_d(Mo[k]) for k in small_names], [two_d(Vo[k]) for k in small_names])
    for i, name in enumerate(small_names):
        shp = P[name].shape
        delta[name], new_m[name], new_v[name] = ds[i].reshape(shp), nms[i].reshape(shp), nvs[i].reshape(shp)

    order = ["norm_mix", "w_in", "w_gla_gate", "b_gla_gate", "gla_norm", "w_dw", "b_dw", "conv_ln_g", "conv_ln_b", "rel_bias",
             "w_out", "norm_ffn", "w_up", "w_down", "norm_final"]
    return (loss, grad_x[None], *[G[k] for k in order], *[delta[k] for k in order], *[new_m[k] for k in order],
            *[new_v[k] for k in order])
```

```python
import functools

import numpy as np
import jax
import jax.numpy as jnp
from jax import lax
from jax.experimental import pallas as pl
from jax.experimental.pallas import tpu as pltpu

F32 = jnp.float32
BF16 = jnp.bfloat16
HI = lax.Precision.HIGHEST

D = 1024
CHUNK = 64
GLA_DK, GLA_DV, GLA_H = 48, 96, 4
KW = 256
VW = 384
LRW = 128
GLA_TAU = 16.0
CW = 256
CK = 31
AW = 384
AH = 6
BAND = 576
LEFT = 512
D_FF = 4096
EPS = 1e-6
NEG = -1e30
N_REL = 257

GLA_COLS = 2 * KW + 2 * VW + LRW
CONV_COLS = 2 * CW
ATT_COLS = 3 * AW

ADAM_LR, ADAM_B1, ADAM_B2, ADAM_EPS, ADAM_WD, ADAM_STEP = 0.001, 0.9, 0.999, 1e-08, 0.01, 10

VMEM_CAP = 56 * 1024 * 1024
MESH = pl.DeviceIdType.MESH


def _cparams(sem, vmem_bytes):
    limit = int(min(VMEM_CAP, max(vmem_bytes * 5 // 4 + (4 << 20), 16 << 20)))
    return pltpu.CompilerParams(dimension_semantics=sem, vmem_limit_bytes=limit)


def _out(shape, dtype):
    return pltpu.HBM(tuple(shape), dtype)


class _Comm:
    def __init__(self, ins, outs, aliases, sems, start, finish, then=None):
        self.ins, self.outs, self.aliases, self.sems = list(ins), list(outs), dict(aliases), list(sems)
        self.start, self.finish, self.then = start, finish, then


class _Plan:
    def __init__(self):
        self.by_call = {}

    def at(self, call, comm):
        self.by_call.setdefault(call, []).append(comm)

    def take(self, call):
        return self.by_call.pop(call, [])


_PLAN = [None]


def _pin(a):
    return pltpu.with_memory_space_constraint(a, pltpu.HBM) if jnp.issubdtype(a.dtype, jnp.floating) else a


def _pallas(body, **kw):
    comms = _PLAN[0].take(kw.get("name")) if _PLAN[0] is not None else []
    if not comms:
        call = pl.pallas_call(body, **kw)
        return lambda *args: call(*[_pin(a) for a in args])

    grid = tuple(kw.get("grid", ()))
    single = not isinstance(kw["out_shape"], (tuple, list))
    out_shape = [kw["out_shape"]] if single else list(kw["out_shape"])
    out_specs = [kw["out_specs"]] if single else list(kw["out_specs"])
    in_specs = list(kw["in_specs"])
    scratch = list(kw.get("scratch_shapes", ()))
    n_in, n_out, n_scr = len(in_specs), len(out_shape), len(scratch)
    c_in = sum(len(c.ins) for c in comms)
    c_out = sum(len(c.outs) for c in comms)
    aliases = dict(kw.get("input_output_aliases", {}))
    i0, o0 = n_in, n_out
    for c in comms:
        for i, o in c.aliases.items():
            aliases[i0 + i] = o0 + o
        i0 += len(c.ins)
        o0 += len(c.outs)

    def wrapped(*refs):
        ins, c_ins = refs[:n_in], refs[n_in:n_in + c_in]
        outs, c_outs = refs[n_in + c_in:n_in + c_in + n_out], refs[n_in + c_in + n_out:n_in + c_in + n_out + c_out]
        scr, c_sems = refs[n_in + c_in + n_out + c_out:][:n_scr], refs[n_in + c_in + n_out + c_out + n_scr:]

        def each(what):
            i0 = o0 = s0 = 0
            for c in comms:
                getattr(c, what)(c_ins[i0:i0 + len(c.ins)], c_outs[o0:o0 + len(c.outs)], c_sems[s0:s0 + len(c.sems)])
                i0, o0, s0 = i0 + len(c.ins), o0 + len(c.outs), s0 + len(c.sems)

        if grid:
            first = functools.reduce(jnp.logical_and, [pl.program_id(a) == 0 for a in range(len(grid))])
            last = functools.reduce(jnp.logical_and, [pl.program_id(a) == grid[a] - 1 for a in range(len(grid))])
            pl.when(first)(lambda: each("start"))
            body(*ins, *outs, *scr)
            pl.when(last)(lambda: each("finish"))
        else:
            each("start")
            body(*ins, *outs, *scr)
            each("finish")

    kw = dict(kw)
    kw["in_specs"] = in_specs + [ANY] * c_in
    kw["out_shape"] = out_shape + [_out(s.shape, s.dtype) for c in comms for s in c.outs]
    kw["out_specs"] = out_specs + [ANY] * c_out
    staging = [s for c in comms for s in c.sems if len(s) == 2 and not isinstance(s[1], int)]
    kw["scratch_shapes"] = scratch + [pltpu.VMEM(*s) if s in staging else pltpu.SemaphoreType.DMA(s) for c in comms for s in c.sems]
    kw["input_output_aliases"] = aliases
    extra = sum(_nbytes(*s) for s in staging)
    old = kw.get("compiler_params")
    limit = (old.vmem_limit_bytes if old is not None else 16 << 20) + extra
    kw["compiler_params"] = pltpu.CompilerParams(
        dimension_semantics=old.dimension_semantics if old is not None else None, vmem_limit_bytes=int(min(VMEM_CAP, limit)))
    call = pl.pallas_call(wrapped, **kw)

    def run(*args):
        res = call(*[_pin(a) for a in args], *[_pin(a) for c in comms for a in c.ins])
        o0 = n_out
        for c in comms:
            if c.then is not None:
                c.then(res[o0:o0 + len(c.outs)])
            o0 += len(c.outs)
        return res[0] if single else res[:n_out]

    return run


def _nbytes(shape, dtype):
    return int(np.prod(shape)) * jnp.dtype(dtype).itemsize


def _sigmoid(x):
    return 1.0 / (1.0 + jnp.exp(-x))


_DIMS = {"nn": (((1,), (0,)), ((), ())), "nt": (((1,), (1,)), ((), ())), "tn": (((0,), (0,)), ((), ()))}


def _mm(a, b, *, mode, out_dtype, name, tm=512, tn=None, tk=None, a_pro=None, epi=None, extra=None,
        b_chips=False, out_chips=False):
    b2 = (b.shape[1], 4 * b.shape[2]) if b_chips else b.shape
    if mode == "nn":
        (M, K), (K2, N) = a.shape, b2
    elif mode == "nt":
        (M, K), (N, K2) = a.shape, b2
    else:
        (K, M), (K2, N) = a.shape, b2
    assert K == K2, (a.shape, b.shape, mode)
    tm = min(tm, M)
    tn = N if tn is None else min(tn, N)
    tk = K if tk is None else min(tk, K)
    assert M % tm == 0 and N % tn == 0 and K % tk == 0, (M, N, K, tm, tn, tk)
    nk = K // tk
    a_blk = (tk, tm) if mode == "tn" else (tm, tk)
    a_map = (lambda i, j, k: (k, i)) if mode == "tn" else (lambda i, j, k: (i, k))
    b_blk = (tn, tk) if mode == "nt" else (tk, tn)
    b_map = (lambda i, j, k: (j, k)) if mode == "nt" else (lambda i, j, k: (k, j))
    if b_chips:
        per = b.shape[2] // b_blk[1]
        assert b.shape[2] % b_blk[1] == 0 and mode != "tn"
        flat_map = b_map
        b_map = lambda i, j, k: (flat_map(i, j, k)[1] // per, flat_map(i, j, k)[0], flat_map(i, j, k)[1] % per)
        b_blk = (None,) + b_blk
    in_specs = [pl.BlockSpec(a_blk, a_map), pl.BlockSpec(b_blk, b_map)]
    args = [a, b]
    if epi is not None:
        in_specs.append(pl.BlockSpec((tm, tn), lambda i, j, k: (i, j)))
        args.append(extra)

    def body(*refs):
        a_ref, b_ref = refs[0], refs[1]
        e_ref = refs[2] if epi is not None else None
        o_ref = refs[3] if epi is not None else refs[2]
        av = a_ref[...]
        if a_pro == "relu2":
            af = jnp.maximum(av.astype(F32), 0.0)
            av = af * af
        p = lax.dot_general(av.astype(BF16), b_ref[...].astype(BF16), _DIMS[mode], preferred_element_type=F32)

        def finish(acc):
            if epi == "add":
                acc = acc + e_ref[...].astype(F32)
            elif epi == "relu2grad":
                acc = acc * (2.0 * jnp.maximum(e_ref[...].astype(F32), 0.0))
            o_ref[...] = acc.astype(o_ref.dtype)

        if nk == 1:
            finish(p)
        else:
            acc_ref = refs[-1]
            k = pl.program_id(2)

            @pl.when(k == 0)
            def _():
                acc_ref[...] = p

            @pl.when(k > 0)
            def _():
                acc_ref[...] += p

            @pl.when(k == nk - 1)
            def _():
                finish(acc_ref[...])

    vm = 2 * (_nbytes(a_blk, a.dtype) + _nbytes((tk, tn), b.dtype) + _nbytes((tm, tn), out_dtype))
    vm += 3 * _nbytes((tm, tn), F32)
    if epi is not None:
        vm += 2 * _nbytes((tm, tn), extra.dtype)
    if out_chips:
        per_out = N // 4 // tn
        assert N % (4 * tn) == 0
        out_shape = _out((4, M, N // 4), out_dtype)
        out_spec = pl.BlockSpec((None, tm, tn), lambda i, j, k: (j // per_out, i, j % per_out))
    else:
        out_shape = _out((M, N), out_dtype)
        out_spec = pl.BlockSpec((tm, tn), lambda i, j, k: (i, j))
    return _pallas(
        body,
        out_shape=out_shape,
        grid=(M // tm, N // tn, nk),
        in_specs=in_specs,
        out_specs=out_spec,
        scratch_shapes=[pltpu.VMEM((tm, tn), F32)] if nk > 1 else [],
        compiler_params=_cparams(("parallel", "parallel", "arbitrary"), vm),
        name=name,
    )(*args)


def _mm_fan(a, bs, *, mode, out_dtypes, name, tm=512):
    M, K = a.shape
    ns = [b.shape[1] if mode == "nn" else b.shape[0] for b in bs]
    n = len(bs)

    def body(*refs):
        av = refs[0][...].astype(BF16)
        for i in range(n):
            refs[1 + n + i][...] = lax.dot_general(av, refs[1 + i][...], _DIMS[mode],
                                                   preferred_element_type=F32).astype(refs[1 + n + i].dtype)

    vm = 2 * _nbytes((tm, K), a.dtype) + sum(2 * _nbytes(b.shape, b.dtype) + 3 * _nbytes((tm, nn), F32) for b, nn in zip(bs, ns))
    return _pallas(
        body,
        out_shape=tuple(_out((M, nn), dt) for nn, dt in zip(ns, out_dtypes)),
        grid=(M // tm,),
        in_specs=[pl.BlockSpec((tm, K), lambda i: (i, 0))] + [pl.BlockSpec(b.shape, lambda i: (0, 0)) for b in bs],
        out_specs=tuple(pl.BlockSpec((tm, nn), lambda i: (i, 0)) for nn in ns),
        compiler_params=_cparams(("parallel",), vm),
        name=name,
    )(a, *bs)


def _mm_sum(as_, bs, *, mode, out_dtype, name, extra=None, tm=512):
    M = as_[0].shape[0]
    N = bs[0].shape[1] if mode == "nn" else bs[0].shape[0]
    n = len(as_)

    def body(*refs):
        acc = None
        for i in range(n):
            p = lax.dot_general(refs[i][...].astype(BF16), refs[n + i][...], _DIMS[mode], preferred_element_type=F32)
            acc = p if acc is None else acc + p
        if extra is not None:
            acc = acc + refs[2 * n][...].astype(F32)
        refs[-1][...] = acc.astype(refs[-1].dtype)

    in_specs = [pl.BlockSpec((tm, a.shape[1]), lambda i: (i, 0)) for a in as_]
    in_specs += [pl.BlockSpec(b.shape, lambda i: (0, 0)) for b in bs]
    args = list(as_) + list(bs)
    if extra is not None:
        in_specs.append(pl.BlockSpec((tm, N), lambda i: (i, 0)))
        args.append(extra)
    vm = sum(2 * _nbytes((tm, a.shape[1]), a.dtype) for a in as_) + sum(2 * _nbytes(b.shape, b.dtype) for b in bs)
    vm += 8 * _nbytes((tm, N), F32)
    return _pallas(
        body,
        out_shape=_out((M, N), out_dtype),
        grid=(M // tm,),
        in_specs=in_specs,
        out_specs=pl.BlockSpec((tm, N), lambda i: (i, 0)),
        compiler_params=_cparams(("parallel",), vm),
        name=name,
    )(*args)


def _mm_tn_multi(ops, pairs, *, name, tk=512):
    T = ops[0].shape[0]
    n, m = len(ops), len(pairs)
    shapes = [(ops[a].shape[1], ops[b].shape[1]) for a, b in pairs]

    def body(*refs):
        vals = [refs[i][...].astype(BF16) for i in range(n)]
        first = pl.program_id(0) == 0
        for j, (a, b) in enumerate(pairs):
            p = lax.dot_general(vals[a], vals[b], _DIMS["tn"], preferred_element_type=F32)
            o_ref = refs[n + j]

            @pl.when(first)
            def _(o_ref=o_ref, p=p):
                o_ref[...] = p

            @pl.when(jnp.logical_not(first))
            def _(o_ref=o_ref, p=p):
                o_ref[...] += p

    vm = sum(2 * _nbytes((tk, o.shape[1]), o.dtype) for o in ops) + sum(3 * _nbytes(s, F32) for s in shapes)
    return _pallas(
        body,
        out_shape=tuple(_out(s, F32) for s in shapes),
        grid=(T // tk,),
        in_specs=[pl.BlockSpec((tk, o.shape[1]), lambda k: (k, 0)) for o in ops],
        out_specs=tuple(pl.BlockSpec(s, lambda k: (0, 0)) for s in shapes),
        compiler_params=_cparams(("arbitrary",), vm),
        name=name,
    )(*ops)


def _rmsnorm_fwd(h, g, name, tm=512):
    T = h.shape[0]

    def body(h_ref, g_ref, o_ref):
        x = h_ref[...]
        r = lax.rsqrt(jnp.mean(x * x, axis=-1, keepdims=True) + EPS)
        o_ref[...] = (x * r * g_ref[...]).astype(o_ref.dtype)

    return _pallas(
        body,
        out_shape=_out((T, D), BF16),
        grid=(T // tm,),
        in_specs=[pl.BlockSpec((tm, D), lambda i: (i, 0)), pl.BlockSpec((1, D), lambda i: (0, 0))],
        out_specs=pl.BlockSpec((tm, D), lambda i: (i, 0)),
        compiler_params=_cparams(("parallel",), 8 * _nbytes((tm, D), F32)),
        name=name,
    )(h, g)


def _rmsnorm_bwd(dxn, h, g, dres, name, tm=512):
    T = h.shape[0]

    def body(dxn_ref, h_ref, g_ref, dres_ref, dh_ref, dg_ref):
        @pl.when(pl.program_id(0) == 0)
        def _():
            dg_ref[...] = jnp.zeros_like(dg_ref)

        x = h_ref[...]
        dy = dxn_ref[...].astype(F32)
        r = lax.rsqrt(jnp.mean(x * x, axis=-1, keepdims=True) + EPS)
        gy = dy * g_ref[...]
        dot = jnp.mean(x * gy, axis=-1, keepdims=True)
        dh_ref[...] = dres_ref[...] + r * gy - x * (r * r * r * dot)
        dg_ref[...] += jnp.sum(dy * x * r, axis=0, keepdims=True)

    row = pl.BlockSpec((tm, D), lambda i: (i, 0))
    vec = pl.BlockSpec((1, D), lambda i: (0, 0))
    return _pallas(
        body,
        out_shape=(_out((T, D), F32), _out((1, D), F32)),
        grid=(T // tm,),
        in_specs=[row, row, vec, row],
        out_specs=(row, vec),
        compiler_params=_cparams(("arbitrary",), 12 * _nbytes((tm, D), F32)),
        name=name,
    )(dxn, h, g, dres)


def _final_loss(h, g, target, name, tm=512):
    T = h.shape[0]

    def body(h_ref, g_ref, t_ref, loss_ref, dh_ref, dg_ref):
        @pl.when(pl.program_id(0) == 0)
        def _():
            dg_ref[...] = jnp.zeros_like(dg_ref)
            loss_ref[...] = jnp.zeros_like(loss_ref)

        x = h_ref[...]
        gg = g_ref[...]
        r = lax.rsqrt(jnp.mean(x * x, axis=-1, keepdims=True) + EPS)
        y = x * r * gg
        e = y - t_ref[...]
        loss_ref[...] += 0.5 * jnp.sum(jnp.mean(e * e, axis=-1, keepdims=True), axis=0, keepdims=True)
        dy = e * (1.0 / D)
        gy = dy * gg
        dot = jnp.mean(x * gy, axis=-1, keepdims=True)
        dh_ref[...] = r * gy - x * (r * r * r * dot)
        dg_ref[...] += jnp.sum(dy * x * r, axis=0, keepdims=True)

    row = pl.BlockSpec((tm, D), lambda i: (i, 0))
    vec = pl.BlockSpec((1, D), lambda i: (0, 0))
    one = pl.BlockSpec((1, 1), lambda i: (0, 0))
    return _pallas(
        body,
        out_shape=(_out((1, 1), F32), _out((T, D), F32), _out((1, D), F32)),
        grid=(T // tm,),
        in_specs=[row, vec, row],
        out_specs=(one, row, vec),
        compiler_params=_cparams(("arbitrary",), 12 * _nbytes((tm, D), F32)),
        name=name,
    )(h, g, target)


GLA_G = 8


def _gla_consts():
    i = np.arange(KW)[:, None]
    j = np.arange(VW)[None, :]
    mask = ((i // GLA_DK) == (j // GLA_DV)) & (i < GLA_H * GLA_DK)
    a = np.arange(VW)
    hm = ((a[:, None] // GLA_DV) == (a[None, :] // GLA_DV)).astype(np.float32)
    c = np.arange(CHUNK)
    low = (c[:, None] >= c[None, :]).astype(np.float32)
    return jnp.asarray(mask.astype(np.float32)), jnp.asarray(hm, BF16), jnp.asarray(low, BF16)


def _split(x):
    hi = x.astype(BF16)
    return hi, (x - hi.astype(F32)).astype(BF16)


def _dot_sel(a, b, dims, split):
    if split == "a":
        hi, lo = _split(a)
        return (lax.dot_general(hi, b, dims, preferred_element_type=F32) + lax.dot_general(lo, b, dims, preferred_element_type=F32))
    hi, lo = _split(b)
    return (lax.dot_general(a, hi, dims, preferred_element_type=F32) + lax.dot_general(a, lo, dims, preferred_element_type=F32))


def _dot3(a, b, dims):
    ah, al = _split(a)
    bh, bl = _split(b)
    return (lax.dot_general(ah, bh, dims, preferred_element_type=F32) + lax.dot_general(al, bh, dims, preferred_element_type=F32)
            + lax.dot_general(ah, bl, dims, preferred_element_type=F32))


def _gla_chunk_common(p_ref, rows, wg, bg, low, ones_v):
    q = p_ref[rows, 0:KW]
    k = p_ref[rows, KW:2 * KW]
    v = p_ref[rows, 2 * KW:2 * KW + VW]
    g = p_ref[rows, 2 * KW + VW:2 * KW + 2 * VW]
    lr = p_ref[rows, 2 * KW + 2 * VW:GLA_COLS]
    z = _dot3(lr, wg, _DIMS["nn"]) + bg
    la = (jnp.minimum(z, 0.0) - jnp.log(1.0 + jnp.exp(-jnp.abs(z)))) * (1.0 / GLA_TAU)
    cum = _dot_sel(low, la, _DIMS["nn"], "b")
    endb = cum[CHUNK - 1:CHUNK, :]
    w = jnp.exp(endb - cum)
    a_full = jnp.exp(_dot_sel(la, ones_v, _DIMS["tn"], "a"))
    return q, k, v, g, lr, z, w, endb, a_full


def _gla_fwd(p, wg, bg, gn, consts, name):
    T = p.shape[0]
    rb = CHUNK * GLA_G
    ng = T // rb
    mask, hm, low = consts[:3]
    scale = GLA_DK ** -0.5

    def body(p_ref, wg_ref, bg_ref, gn_ref, m_ref, hm_ref, l_ref, o_ref, st_ref, s_ref):
        @pl.when(pl.program_id(0) == 0)
        def _():
            s_ref[...] = jnp.zeros_like(s_ref)

        wg_v, bg_v, gn_v = wg_ref[...], bg_ref[...], gn_ref[...]
        ones_v = jnp.ones((CHUNK, VW), BF16)
        for c in range(GLA_G):
            rows = slice(c * CHUNK, (c + 1) * CHUNK)
            q, k, v, g, _, _, w, _, a_full = _gla_chunk_common(p_ref, rows, wg_v, bg_v, l_ref[...], ones_v)
            kd = (k * w).astype(BF16)
            kv = lax.dot_general(kd, v.astype(BF16), _DIMS["tn"], preferred_element_type=F32) * m_ref[...]
            s_new = a_full * s_ref[...] + kv
            s_ref[...] = s_new
            st_ref[c] = s_new
            o = jnp.dot((q * scale).astype(BF16), s_new.astype(BF16), preferred_element_type=F32)
            ms = _dot_sel(o * o, hm_ref[...], _DIMS["nn"], "a") * (1.0 / GLA_DV)
            r = lax.rsqrt(ms + EPS)
            y = o * r * gn_v * (g * _sigmoid(g))
            o_ref[rows, :] = y.astype(o_ref.dtype)

    full = lambda shape: pl.BlockSpec(shape, lambda i: tuple(0 for _ in shape))
    vm = 2 * _nbytes((rb, GLA_COLS), F32) + 2 * _nbytes((GLA_G, KW, VW), F32) + 12 * _nbytes((KW, VW), F32)
    return _pallas(
        body,
        out_shape=(_out((T, VW), BF16), _out((T // CHUNK, KW, VW), F32)),
        grid=(ng,),
        in_specs=[pl.BlockSpec((rb, GLA_COLS), lambda i: (i, 0)), full((LRW, KW)), full((1, KW)), full((1, VW)),
                  full((KW, VW)), full((VW, VW)), full((CHUNK, CHUNK))],
        out_specs=(pl.BlockSpec((rb, VW), lambda i: (i, 0)), pl.BlockSpec((GLA_G, KW, VW), lambda i: (i, 0, 0))),
        scratch_shapes=[pltpu.VMEM((KW, VW), F32)],
        compiler_params=_cparams(("arbitrary",), vm),
        name=name,
    )(p, wg, bg, gn, mask, hm, low)


def _gla_bwd(p, dy, states, wg, bg, gn, consts, name):
    T = p.shape[0]
    rb = CHUNK * GLA_G
    ng = T // rb
    mask, hm, low = consts[:3]
    scale = GLA_DK ** -0.5

    def body(p_ref, dy_ref, st_ref, sp_ref, wg_ref, bg_ref, gn_ref, m_ref, hm_ref, l_ref,
             dp_ref, dwg_ref, dbg_ref, dgn_ref, ga_ref):
        step = pl.program_id(0)

        @pl.when(step == 0)
        def _():
            ga_ref[...] = jnp.zeros_like(ga_ref)
            dwg_ref[...] = jnp.zeros_like(dwg_ref)
            dbg_ref[...] = jnp.zeros_like(dbg_ref)
            dgn_ref[...] = jnp.zeros_like(dgn_ref)

        first_group = step == ng - 1
        wg_v, bg_v, gn_v = wg_ref[...], bg_ref[...], gn_ref[...]
        ones_v = jnp.ones((CHUNK, VW), BF16)
        ones_8 = jnp.ones((8, VW), BF16)
        for c in reversed(range(GLA_G)):
            rows = slice(c * CHUNK, (c + 1) * CHUNK)
            q, k, v, g, lr, z, w, endb, a_full = _gla_chunk_common(p_ref, rows, wg_v, bg_v, l_ref[...], ones_v)
            s_n = st_ref[c]
            if c > 0:
                s_prev = st_ref[c - 1]
            else:
                s_prev = jnp.where(first_group, 0.0, sp_ref[0])
            kd = k * w
            qs = (q * scale).astype(BF16)
            o = jnp.dot(qs, s_n.astype(BF16), preferred_element_type=F32)
            ms = _dot_sel(o * o, hm_ref[...], _DIMS["nn"], "a") * (1.0 / GLA_DV)
            r = lax.rsqrt(ms + EPS)
            on = o * r
            sg = _sigmoid(g)
            silu = g * sg
            dyv = dy_ref[rows, :].astype(F32)
            d_on = dyv * gn_v * silu
            dg = dyv * on * gn_v * (sg * (1.0 + g * (1.0 - sg)))
            dgn_ref[...] += jnp.sum(dyv * on * silu, axis=0, keepdims=True)
            mo = _dot_sel(o * d_on, hm_ref[...], _DIMS["nn"], "a") * (1.0 / GLA_DV)
            do = r * d_on - o * (r * r * r) * mo
            dob = do.astype(BF16)
            dq = lax.dot_general(dob, s_n.astype(BF16), _DIMS["nt"], preferred_element_type=F32) * scale
            g_n = lax.dot_general(qs, dob, _DIMS["tn"], preferred_element_type=F32) * m_ref[...] + ga_ref[...]
            d_a = _dot_sel(ones_8, g_n * s_prev, _DIMS["nt"], "b")[0:1, :]
            g_nb = g_n.astype(BF16)
            dkd = lax.dot_general(v.astype(BF16), g_nb, _DIMS["nt"], preferred_element_type=F32)
            dv = jnp.dot(kd.astype(BF16), g_nb, preferred_element_type=F32)
            dk = dkd * w
            e = dkd * kd
            d_end = jnp.sum(e, axis=0, keepdims=True) + d_a * jnp.exp(endb)
            dla = _dot_sel(l_ref[...], -e, _DIMS["tn"], "b") + d_end
            dz = dla * (1.0 - _sigmoid(z)) * (1.0 / GLA_TAU)
            dlr = _dot3(dz, wg_v, _DIMS["nt"])
            dwg_ref[...] += _dot3(lr, dz, _DIMS["tn"])
            dbg_ref[...] += jnp.sum(dz, axis=0, keepdims=True)
            ga_ref[...] = a_full * g_n
            dp_ref[rows, 0:KW] = dq.astype(dp_ref.dtype)
            dp_ref[rows, KW:2 * KW] = dk.astype(dp_ref.dtype)
            dp_ref[rows, 2 * KW:2 * KW + VW] = dv.astype(dp_ref.dtype)
            dp_ref[rows, 2 * KW + VW:2 * KW + 2 * VW] = dg.astype(dp_ref.dtype)
            dp_ref[rows, 2 * KW + 2 * VW:GLA_COLS] = dlr.astype(dp_ref.dtype)

    full = lambda shape: pl.BlockSpec(shape, lambda i: tuple(0 for _ in shape))
    rev = lambda i: (ng - 1 - i, 0)
    vm = 4 * _nbytes((rb, GLA_COLS), F32) + 2 * _nbytes((rb, VW), F32) + 2 * _nbytes((GLA_G + 1, KW, VW), F32)
    vm += 16 * _nbytes((KW, VW), F32)
    return _pallas(
        body,
        out_shape=(_out((T, GLA_COLS), BF16), _out((LRW, KW), F32),
                   _out((1, KW), F32), _out((1, VW), F32)),
        grid=(ng,),
        in_specs=[pl.BlockSpec((rb, GLA_COLS), rev), pl.BlockSpec((rb, VW), rev),
                  pl.BlockSpec((GLA_G, KW, VW), lambda i: (ng - 1 - i, 0, 0)),
                  pl.BlockSpec((1, KW, VW), lambda i: (jnp.maximum((ng - 1 - i) * GLA_G - 1, 0), 0, 0)),
                  full((LRW, KW)), full((1, KW)), full((1, VW)), full((KW, VW)), full((VW, VW)), full((CHUNK, CHUNK))],
        out_specs=(pl.BlockSpec((rb, GLA_COLS), rev), full((LRW, KW)), full((1, KW)), full((1, VW))),
        scratch_shapes=[pltpu.VMEM((KW, VW), F32)],
        compiler_params=_cparams(("arbitrary",), vm),
        name=name,
    )(p, dy, states, states, wg, bg, gn, mask, hm, low)


CONV_TM = 512
HALO = 32
CONV_RB = 64


def _glu(u):
    a = u[:, 0:CW]
    b = u[:, CW:2 * CW]
    return a * _sigmoid(b)


def _conv_taps(buf_ref, w_ref, rb0, first_tap):
    acc = jnp.zeros((CONV_RB, CW), F32)
    for j in range(CK):
        s = rb0 + first_tap(j)
        acc = acc + w_ref[j:j + 1, :] * buf_ref[s:s + CONV_RB, :]
    return acc


def _ln_fwd(c, lg, lb):
    mu = jnp.mean(c, axis=-1, keepdims=True)
    xc = c - mu
    rstd = lax.rsqrt(jnp.mean(xc * xc, axis=-1, keepdims=True) + EPS)
    n = xc * rstd
    return n, rstd, n * lg + lb


def _conv_fwd(u, w, b, lg, lb, name):
    T = u.shape[0]
    tm = CONV_TM

    def body(u_ref, uh_ref, w_ref, b_ref, lg_ref, lb_ref, o_ref, hbuf):
        i = pl.program_id(0)
        hbuf[0:HALO, :] = jnp.where(i > 0, _glu(uh_ref[...]), 0.0)
        hbuf[HALO:HALO + tm, :] = _glu(u_ref[...])
        for r in range(tm // CONV_RB):
            acc = _conv_taps(hbuf, w_ref, r * CONV_RB, lambda j: HALO - (CK - 1) + j)
            _, _, zz = _ln_fwd(acc + b_ref[...], lg_ref[...], lb_ref[...])
            o_ref[r * CONV_RB:(r + 1) * CONV_RB, :] = (zz * _sigmoid(zz)).astype(o_ref.dtype)

    vec = pl.BlockSpec((1, CW), lambda i: (0, 0))
    return _pallas(
        body,
        out_shape=_out((T, CW), BF16),
        grid=(T // tm,),
        in_specs=[pl.BlockSpec((tm, CONV_COLS), lambda i: (i, 0)),
                  pl.BlockSpec((HALO, CONV_COLS), lambda i: (jnp.maximum(i * (tm // HALO) - 1, 0), 0)),
                  pl.BlockSpec((HALO, CW), lambda i: (0, 0)), vec, vec, vec],
        out_specs=pl.BlockSpec((tm, CW), lambda i: (i, 0)),
        scratch_shapes=[pltpu.VMEM((tm + HALO, CW), F32)],
        compiler_params=_cparams(("arbitrary",), 8 * _nbytes((tm, CONV_COLS), F32)),
        name=name,
    )(u, u, w, b, lg, lb)


def _conv_bwd_dc(u, dout, w, b, lg, lb, name):
    T = u.shape[0]
    tm = CONV_TM
    nsteps = T // tm

    def body(u_ref, uh_ref, do_ref, w_ref, b_ref, lg_ref, lb_ref, dc_ref, dw_ref, db_ref, dlg_ref, dlb_ref, hbuf, dwacc):
        i = pl.program_id(0)

        @pl.when(i == 0)
        def _():
            dwacc[...] = jnp.zeros_like(dwacc)
            db_ref[...] = jnp.zeros_like(db_ref)
            dlg_ref[...] = jnp.zeros_like(dlg_ref)
            dlb_ref[...] = jnp.zeros_like(dlb_ref)

        hbuf[0:HALO, :] = jnp.where(i > 0, _glu(uh_ref[...]), 0.0)
        hbuf[HALO:HALO + tm, :] = _glu(u_ref[...])
        for r in range(tm // CONV_RB):
            rows = slice(r * CONV_RB, (r + 1) * CONV_RB)
            acc = _conv_taps(hbuf, w_ref, r * CONV_RB, lambda j: HALO - (CK - 1) + j)
            n, rstd, zz = _ln_fwd(acc + b_ref[...], lg_ref[...], lb_ref[...])
            sg = _sigmoid(zz)
            dz = do_ref[rows, :].astype(F32) * (sg * (1.0 + zz * (1.0 - sg)))
            dlg_ref[...] += jnp.sum(dz * n, axis=0, keepdims=True)
            dlb_ref[...] += jnp.sum(dz, axis=0, keepdims=True)
            dn = dz * lg_ref[...]
            dc = rstd * (dn - jnp.mean(dn, axis=-1, keepdims=True) - n * jnp.mean(dn * n, axis=-1, keepdims=True))
            dc_ref[rows, :] = dc
            db_ref[...] += jnp.sum(dc, axis=0, keepdims=True)
            for j in range(CK):
                s = r * CONV_RB + HALO - (CK - 1) + j
                prod = dc * hbuf[s:s + CONV_RB, :]
                dwacc[j] += jnp.sum(prod.reshape(CONV_RB // 8, 8, CW), axis=0)

        @pl.when(i == nsteps - 1)
        def _():
            dw_ref[...] = jnp.sum(dwacc[...], axis=1)

    vec = pl.BlockSpec((1, CW), lambda i: (0, 0))
    return _pallas(
        body,
        out_shape=(_out((T, CW), F32), _out((HALO, CW), F32),
                   _out((1, CW), F32), _out((1, CW), F32), _out((1, CW), F32)),
        grid=(nsteps,),
        in_specs=[pl.BlockSpec((tm, CONV_COLS), lambda i: (i, 0)),
                  pl.BlockSpec((HALO, CONV_COLS), lambda i: (jnp.maximum(i * (tm // HALO) - 1, 0), 0)),
                  pl.BlockSpec((tm, CW), lambda i: (i, 0)),
                  pl.BlockSpec((HALO, CW), lambda i: (0, 0)), vec, vec, vec],
        out_specs=(pl.BlockSpec((tm, CW), lambda i: (i, 0)), pl.BlockSpec((HALO, CW), lambda i: (0, 0)), vec, vec, vec),
        scratch_shapes=[pltpu.VMEM((tm + HALO, CW), F32), pltpu.VMEM((HALO, 8, CW), F32)],
        compiler_params=_cparams(("arbitrary",), 10 * _nbytes((tm, CONV_COLS), F32)),
        name=name,
    )(u, u, dout, w, b, lg, lb)


def _conv_bwd_du(u, dc, w, name):
    T = u.shape[0]
    tm = CONV_TM
    nsteps = T // tm

    def body(u_ref, dc_ref, dch_ref, w_ref, du_ref, dcbuf):
        i = pl.program_id(0)
        dcbuf[0:tm, :] = dc_ref[...]
        dcbuf[tm:tm + HALO, :] = jnp.where(i < nsteps - 1, dch_ref[...], 0.0)
        for r in range(tm // CONV_RB):
            rows = slice(r * CONV_RB, (r + 1) * CONV_RB)
            dh = _conv_taps(dcbuf, w_ref, r * CONV_RB, lambda j: (CK - 1) - j)
            a = u_ref[rows, 0:CW]
            sb = _sigmoid(u_ref[rows, CW:2 * CW])
            du_ref[rows, 0:CW] = (dh * sb).astype(du_ref.dtype)
            du_ref[rows, CW:2 * CW] = (dh * a * sb * (1.0 - sb)).astype(du_ref.dtype)

    return _pallas(
        body,
        out_shape=_out((T, CONV_COLS), BF16),
        grid=(nsteps,),
        in_specs=[pl.BlockSpec((tm, CONV_COLS), lambda i: (i, 0)),
                  pl.BlockSpec((tm, CW), lambda i: (i, 0)),
                  pl.BlockSpec((HALO, CW), lambda i: (jnp.minimum((i + 1) * (tm // HALO), T // HALO - 1), 0)),
                  pl.BlockSpec((HALO, CW), lambda i: (0, 0))],
        out_specs=pl.BlockSpec((tm, CONV_COLS), lambda i: (i, 0)),
        scratch_shapes=[pltpu.VMEM((tm + HALO, CW), F32)],
        compiler_params=_cparams(("arbitrary",), 8 * _nbytes((tm, CONV_COLS), F32)),
        name=name,
    )(u, dc, dc, w)


ATT_G = 4


def _att_load_kv(p_any, kbuf, vbuf, sems, T):
    kc = pltpu.make_async_copy(p_any.at[:, pl.ds(AW, AW)], kbuf.at[pl.ds(LEFT, T), :], sems.at[0])
    vc = pltpu.make_async_copy(p_any.at[:, pl.ds(2 * AW, AW)], vbuf.at[pl.ds(LEFT, T), :], sems.at[1])
    kc.start()
    vc.start()
    kbuf[0:LEFT, :] = jnp.zeros((LEFT, AW), BF16)
    vbuf[0:LEFT, :] = jnp.zeros((LEFT, AW), BF16)
    kc.wait()
    vc.wait()


ATT_QB = CHUNK * ATT_G
ATT_KB = LEFT + ATT_QB
REL_PAD = 384
TOEP = 1024


def _att_consts():
    m = np.arange(TOEP)
    d = ATT_KB - 1 - m
    idx = np.clip(d, -128, 128) + 128
    sel = (np.arange(REL_PAD)[:, None] == idx[None, :]) & (m[None, :] < ATT_QB + ATT_KB - 1)
    return jnp.asarray(sel.astype(np.float32))


def _att_build_bias(rel_ref, sel_ref, bias_scr):
    tr = jnp.dot(rel_ref[...], sel_ref[...], precision=HI, preferred_element_type=F32)
    qc = lax.broadcasted_iota(jnp.int32, (ATT_QB, ATT_KB), 0) // CHUNK
    kc = lax.broadcasted_iota(jnp.int32, (ATT_QB, ATT_KB), 1) // CHUNK
    band = (kc >= qc) & (kc <= qc + 8)
    for h in range(AH):
        rows = jnp.broadcast_to(tr[h:h + 1, :], (ATT_QB, TOEP))
        toep = pltpu.roll(rows, TOEP - (ATT_QB - 1), 1, stride=1, stride_axis=0)[:, 0:ATT_KB]
        bias_scr[h // 2, (h % 2) * ATT_QB:(h % 2 + 1) * ATT_QB, :] = jnp.where(band, toep, NEG)


def _att_probs(qst, kb, bias_p, n0):
    sc = lax.dot_general(qst, kb, _DIMS["nt"], preferred_element_type=F32) * (64 ** -0.5) + bias_p
    pos = lax.broadcasted_iota(jnp.int32, (2 * ATT_QB, ATT_KB), 1)
    sc = jnp.where(pos >= CHUNK * (8 - n0), sc, NEG)
    mx = jnp.max(sc, axis=-1, keepdims=True)
    ex = jnp.exp(sc - mx)
    return ex / jnp.sum(ex, axis=-1, keepdims=True)


def _head_stack(a2, lo):
    zero = jnp.zeros_like(a2)
    return jnp.concatenate([jnp.where(lo, a2, zero), jnp.where(lo, zero, a2)], axis=0)


def _att_fwd(p, rel, sel, name):
    T = p.shape[0]

    def body(q_ref, p_any, rel_ref, sel_ref, o_ref, kbuf, vbuf, bias_scr, sems):
        i = pl.program_id(0)

        @pl.when(i == 0)
        def _():
            _att_load_kv(p_any, kbuf, vbuf, sems, T)
            _att_build_bias(rel_ref, sel_ref, bias_scr)

        lo = lax.broadcasted_iota(jnp.int32, (ATT_QB, 128), 1) < 64
        n0 = i * ATT_G
        start = pl.multiple_of(i * ATT_QB, ATT_QB)
        for hp in range(AH // 2):
            cols = slice(hp * 128, (hp + 1) * 128)
            kb = kbuf[pl.ds(start, ATT_KB), cols]
            vb = vbuf[pl.ds(start, ATT_KB), cols]
            pr = _att_probs(_head_stack(q_ref[:, cols], lo), kb, bias_scr[hp], n0)
            pv = jnp.dot(pr.astype(BF16), vb, preferred_element_type=F32)
            o_ref[:, cols] = jnp.where(lo, pv[0:ATT_QB], pv[ATT_QB:2 * ATT_QB]).astype(o_ref.dtype)

    vm = 2 * _nbytes((T + LEFT, AW), BF16) + 8 * _nbytes((2 * ATT_QB, ATT_KB), F32) + (8 << 20)
    return _pallas(
        body,
        out_shape=_out((T, AW), BF16),
        grid=(T // ATT_QB,),
        in_specs=[pl.BlockSpec((ATT_QB, AW), lambda i: (i, 0)), pl.BlockSpec(memory_space=pl.ANY),
                  pl.BlockSpec((8, REL_PAD), lambda i: (0, 0)), pl.BlockSpec((REL_PAD, TOEP), lambda i: (0, 0))],
        out_specs=pl.BlockSpec((ATT_QB, AW), lambda i: (i, 0)),
        scratch_shapes=[pltpu.VMEM((T + LEFT, AW), BF16), pltpu.VMEM((T + LEFT, AW), BF16),
                        pltpu.VMEM((AH // 2, 2 * ATT_QB, ATT_KB), F32), pltpu.SemaphoreType.DMA((2,))],
        compiler_params=_cparams(("arbitrary",), vm),
        name=name,
    )(p, p, rel, sel)


def _att_bwd(p, do, rel, sel, name):
    T = p.shape[0]
    nsteps = T // ATT_QB

    def body(q_ref, p_any, do_ref, rel_ref, sel_ref, dp_any, drel_ref,
             kbuf, vbuf, dqbuf, dkbuf, dvbuf, bias_scr, dbias_scr, dtr_scr, sems):
        i = pl.program_id(0)

        @pl.when(i == 0)
        def _():
            _att_load_kv(p_any, kbuf, vbuf, sems, T)
            _att_build_bias(rel_ref, sel_ref, bias_scr)
            dkbuf[...] = jnp.zeros_like(dkbuf)
            dvbuf[...] = jnp.zeros_like(dvbuf)
            dbias_scr[...] = jnp.zeros_like(dbias_scr)

        lo = lax.broadcasted_iota(jnp.int32, (ATT_QB, 128), 1) < 64
        n0 = i * ATT_G
        start = pl.multiple_of(i * ATT_QB, ATT_QB)
        for hp in range(AH // 2):
            cols = slice(hp * 128, (hp + 1) * 128)
            kb = kbuf[pl.ds(start, ATT_KB), cols]
            vb = vbuf[pl.ds(start, ATT_KB), cols]
            qst = _head_stack(q_ref[:, cols], lo)
            dost = _head_stack(do_ref[:, cols].astype(BF16), lo)
            pr = _att_probs(qst, kb, bias_scr[hp], n0)
            dpr = lax.dot_general(dost, vb, _DIMS["nt"], preferred_element_type=F32)
            ds = pr * (dpr - jnp.sum(dpr * pr, axis=-1, keepdims=True))
            dbias_scr[hp] += ds
            dsb = (ds * (64 ** -0.5)).astype(BF16)
            dq = jnp.dot(dsb, kb, preferred_element_type=F32)
            dqbuf[pl.ds(start, ATT_QB), cols] = jnp.where(lo, dq[0:ATT_QB], dq[ATT_QB:2 * ATT_QB]).astype(BF16)
            dkbuf[pl.ds(start, ATT_KB), cols] += lax.dot_general(dsb, qst, _DIMS["tn"], preferred_element_type=F32)
            dvbuf[pl.ds(start, ATT_KB), cols] += lax.dot_general(pr.astype(BF16), dost, _DIMS["tn"], preferred_element_type=F32)

        @pl.when(i == nsteps - 1)
        def _():
            kbuf[pl.ds(LEFT, T), :] = dkbuf[pl.ds(LEFT, T), :].astype(BF16)
            vbuf[pl.ds(LEFT, T), :] = dvbuf[pl.ds(LEFT, T), :].astype(BF16)
            cps = [pltpu.make_async_copy(dqbuf, dp_any.at[:, pl.ds(0, AW)], sems.at[0]),
                   pltpu.make_async_copy(kbuf.at[pl.ds(LEFT, T), :], dp_any.at[:, pl.ds(AW, AW)], sems.at[1]),
                   pltpu.make_async_copy(vbuf.at[pl.ds(LEFT, T), :], dp_any.at[:, pl.ds(2 * AW, AW)], sems.at[2])]
            for cp in cps:
                cp.start()
            dtr_scr[...] = jnp.zeros_like(dtr_scr)
            ri = lax.broadcasted_iota(jnp.int32, (ATT_QB, ATT_QB), 0)
            ci = lax.broadcasted_iota(jnp.int32, (ATT_QB, ATT_QB), 1)
            flip = jnp.where(ri + ci == ATT_QB - 1, 1.0, 0.0)
            for h in range(AH):
                db = dbias_scr[h // 2, (h % 2) * ATT_QB:(h % 2 + 1) * ATT_QB, :]
                db = jnp.dot(flip, db, precision=HI, preferred_element_type=F32)
                wide = jnp.concatenate([db, jnp.zeros((ATT_QB, TOEP - ATT_KB), F32)], axis=1)
                diag = pltpu.roll(wide, 0, 1, stride=1, stride_axis=0)
                dtr_scr[h:h + 1, :] = jnp.sum(diag, axis=0, keepdims=True)
            drel_ref[...] = lax.dot_general(dtr_scr[...], sel_ref[...], _DIMS["nt"], precision=HI, preferred_element_type=F32)
            for cp in cps:
                cp.wait()

    vm = 3 * _nbytes((T + LEFT, AW), BF16) + 2 * _nbytes((T + LEFT, AW), F32) + 12 * _nbytes((2 * ATT_QB, ATT_KB), F32) + (8 << 20)
    return _pallas(
        body,
        out_shape=(_out((T, ATT_COLS), BF16), _out((8, REL_PAD), F32)),
        grid=(nsteps,),
        in_specs=[pl.BlockSpec((ATT_QB, AW), lambda i: (i, 0)), pl.BlockSpec(memory_space=pl.ANY),
                  pl.BlockSpec((ATT_QB, AW), lambda i: (i, 0)),
                  pl.BlockSpec((8, REL_PAD), lambda i: (0, 0)), pl.BlockSpec((REL_PAD, TOEP), lambda i: (0, 0))],
        out_specs=(pl.BlockSpec(memory_space=pl.ANY), pl.BlockSpec((8, REL_PAD), lambda i: (0, 0))),
        scratch_shapes=[pltpu.VMEM((T + LEFT, AW), BF16), pltpu.VMEM((T + LEFT, AW), BF16), pltpu.VMEM((T, AW), BF16),
                        pltpu.VMEM((T + LEFT, AW), F32), pltpu.VMEM((T + LEFT, AW), F32),
                        pltpu.VMEM((AH // 2, 2 * ATT_QB, ATT_KB), F32), pltpu.VMEM((AH // 2, 2 * ATT_QB, ATT_KB), F32),
                        pltpu.VMEM((8, TOEP), F32), pltpu.SemaphoreType.DMA((3,))],
        compiler_params=_cparams(("arbitrary",), vm),
        name=name,
    )(p, p, do, rel, sel)


def _layer_fwd(h, wl, consts, tag):
    xn = _rmsnorm_fwd(h, wl["norm_mix"], f"{tag}_norm_mix")
    p_gla, p_conv, p_att = _mm_fan(xn, [wl["w_gla"], wl["w_conv"], wl["w_att"]], mode="nn", out_dtypes=(F32, F32, BF16),
                                   name=f"{tag}_proj")
    o_gla, states = _gla_fwd(p_gla, wl["wg"], wl["bg"], wl["gla_norm"], consts, f"{tag}_gla_fwd")
    o_conv = _conv_fwd(p_conv, wl["w_dw"], wl["b_dw"], wl["ln_g"], wl["ln_b"], f"{tag}_conv_fwd")
    rel = jnp.pad(wl["rel_bias"], ((0, 8 - AH), (0, REL_PAD - N_REL)))
    o_att = _att_fwd(p_att, rel, consts[3], f"{tag}_att_fwd")
    h1 = _mm_sum([o_gla, o_conv, o_att], [wl["w_out_g"], wl["w_out_c"], wl["w_out_a"]], mode="nn", out_dtype=F32, extra=h,
                 name=f"{tag}_out")
    xn2 = _rmsnorm_fwd(h1, wl["norm_ffn"], f"{tag}_norm_ffn")
    u = _mm(xn2, wl["w_up"], mode="nn", out_dtype=BF16, tm=1024, tn=1024, b_chips=True, name=f"{tag}_mlp_up")
    h2 = _mm(u, wl["w_down"], mode="nn", out_dtype=F32, tm=1024, tk=1024, a_pro="relu2", epi="add", extra=h1,
             name=f"{tag}_mlp_down")
    saved = dict(h=h, xn=xn, p_gla=p_gla, p_conv=p_conv, p_att=p_att, states=states, o_gla=o_gla, o_conv=o_conv,
                 o_att=o_att, rel=rel, h1=h1, xn2=xn2, u=u)
    return h2, saved


def _layer_bwd(dh2, sv, wl, consts, tag, emit=lambda name, grad: None):
    g = {}
    du = _mm(dh2, wl["w_down"], mode="nt", out_dtype=BF16, tm=1024, tn=1024, epi="relu2grad", extra=sv["u"],
             name=f"{tag}_mlp_down_dx")
    g["w_down"] = _mm(sv["u"], dh2, mode="tn", out_dtype=F32, tm=2048, tn=1024, tk=512, a_pro="relu2", name=f"{tag}_mlp_down_dw")
    emit("w_down", g["w_down"].reshape(4, D_FF // 4, D))
    dxn2 = _mm(du, wl["w_up"], mode="nt", out_dtype=F32, tm=1024, tk=1024, b_chips=True, name=f"{tag}_mlp_up_dx")
    g["w_up"] = _mm(sv["xn2"], du, mode="tn", out_dtype=F32, tm=1024, tn=1024, tk=512, out_chips=True, name=f"{tag}_mlp_up_dw")
    emit("w_up", g["w_up"])
    dh1, g["norm_ffn"] = _rmsnorm_bwd(dxn2, sv["h1"], wl["norm_ffn"], dh2, f"{tag}_norm_ffn_bwd")
    d_gla, d_conv, d_att = _mm_fan(dh1, [wl["w_out_g"], wl["w_out_c"], wl["w_out_a"]], mode="nt", out_dtypes=(F32, F32, F32),
                                   name=f"{tag}_out_dx")
    g["w_out_g"], g["w_out_c"], g["w_out_a"] = _mm_tn_multi([sv["o_gla"], sv["o_conv"], sv["o_att"], dh1],
                                                            [(0, 3), (1, 3), (2, 3)], name=f"{tag}_out_dw")
    emit("w_out", jnp.concatenate([g["w_out_g"], g["w_out_c"], g["w_out_a"]], axis=0).reshape(4, D // 4, D))
    dp_gla, g["wg"], g["bg"], g["gla_norm"] = _gla_bwd(sv["p_gla"], d_gla, sv["states"], wl["wg"], wl["bg"], wl["gla_norm"],
                                                       consts, f"{tag}_gla_bwd")
    dc, g["w_dw"], g["b_dw"], g["ln_g"], g["ln_b"] = _conv_bwd_dc(sv["p_conv"], d_conv, wl["w_dw"], wl["b_dw"], wl["ln_g"],
                                                                  wl["ln_b"], f"{tag}_conv_bwd_dc")
    dp_conv = _conv_bwd_du(sv["p_conv"], dc, wl["w_dw"], f"{tag}_conv_bwd_du")
    dp_att, drel = _att_bwd(sv["p_att"], d_att, sv["rel"], consts[3], f"{tag}_att_bwd")
    g["rel_bias"] = drel[0:AH, 0:N_REL]
    g["w_gla"], g["w_conv"], g["w_att"] = _mm_tn_multi([sv["xn"], dp_gla, dp_conv, dp_att], [(0, 1), (0, 2), (0, 3)],
                                                       name=f"{tag}_proj_dw")
    emit("w_in", jnp.transpose(_join_w_in(g).reshape(D, 4, -1), (1, 0, 2)))
    dxn = _mm_sum([dp_gla, dp_conv, dp_att], [wl["w_gla"], wl["w_conv"], wl["w_att"]], mode="nt", out_dtype=F32,
                  name=f"{tag}_proj_dx")
    dh, g["norm_mix"] = _rmsnorm_bwd(dxn, sv["h"], wl["norm_mix"], dh1, f"{tag}_norm_mix_bwd")
    return dh, g


def _local_step(x, target, layers, norm_final, emit=lambda layer, name, grad: None):
    consts = _gla_consts() + (_att_consts(),)
    h = x
    saved = []
    for l, wl in enumerate(layers):
        h, sv = _layer_fwd(h, wl, consts, f"l{l}")
        saved.append(sv)
    loss, dh, g_final = _final_loss(h, norm_final, target, "final_loss")
    grads = [None] * len(layers)
    for l in reversed(range(len(layers))):
        dh, grads[l] = _layer_bwd(dh, saved[l], layers[l], consts, f"l{l}", functools.partial(emit, l))
    return loss, dh, grads, g_final


ANY = pl.BlockSpec(memory_space=pl.ANY)


def _place():
    x, y, c = lax.axis_index("x"), lax.axis_index("y"), lax.axis_index("c")
    chips = [(1 - x, y), (x, 1 - y), (1 - x, 1 - y)]
    return x, y, c, chips


def _shape(shape, dtype):
    return jax.ShapeDtypeStruct(tuple(shape), dtype)


def _remote(src, dst, send_sem, recv_sem, to):
    return pltpu.make_async_remote_copy(src_ref=src, dst_ref=dst, send_sem=send_sem, recv_sem=recv_sem,
                                        device_id=to, device_id_type=MESH)


class _Staged:
    def __init__(self, src, dst, buf, sems):
        self.load = pltpu.make_async_copy(src, buf, sems.at[0])
        self.store = pltpu.make_async_copy(buf, dst, sems.at[1])

    def start(self):
        self.load.start()

    def wait(self):
        self.load.wait()
        self.store.start()
        self.store.wait()


def _phase_gather_ici(src, then):
    R, C = src.shape
    rh = R // 2

    def copies(ins, outs, sems):
        x, y, c, chips = _place()
        me = 2 * x + y
        local = _Staged(ins[0], outs[0].at[me], sems[3], sems[2])
        sends = [_remote(ins[0].at[pl.ds(c * rh, rh), :], outs[0].at[me, pl.ds(c * rh, rh), :], sems[0].at[k], sems[1].at[k],
                         (px, py, c)) for k, (px, py) in enumerate(chips)]
        recvs = [_remote(outs[0].at[2 * px + py, pl.ds(c * rh, rh), :], outs[0].at[2 * px + py, pl.ds(c * rh, rh), :],
                         sems[0].at[k], sems[1].at[k], (px, py, c)) for k, (px, py) in enumerate(chips)]
        return local, sends, recvs

    def start(ins, outs, sems):
        local, sends, _ = copies(ins, outs, sems)
        local.start()
        for cp in sends:
            cp.start()

    def finish(ins, outs, sems):
        local, sends, recvs = copies(ins, outs, sems)
        for cp in recvs:
            cp.wait_recv()
        for cp in sends:
            cp.wait_send()
        local.wait()

    return _Comm([src], [_shape((4, R, C), src.dtype)], {}, [(3,), (3,), (2,), ((R, C), src.dtype)], start, finish, then)


def _phase_gather_d2d(part, then):
    _, R, C = part.shape
    rh = R // 2

    def copies(ins, outs, sems):
        x, y, c, chips = _place()
        sends = [_remote(ins[0].at[2 * px + py, pl.ds(c * rh, rh), :], outs[0].at[2 * px + py, pl.ds(c * rh, rh), :],
                         sems[0].at[k], sems[1].at[k], (x, y, 1 - c)) for k, (px, py) in enumerate(chips)]
        recvs = [_remote(outs[0].at[2 * px + py, pl.ds((1 - c) * rh, rh), :], outs[0].at[2 * px + py, pl.ds((1 - c) * rh, rh), :],
                         sems[0].at[k], sems[1].at[k], (x, y, 1 - c)) for k, (px, py) in enumerate(chips)]
        return sends, recvs

    def start(ins, outs, sems):
        for cp in copies(ins, outs, sems)[0]:
            cp.start()

    def finish(ins, outs, sems):
        sends, recvs = copies(ins, outs, sems)
        for cp in recvs:
            cp.wait_recv()
        for cp in sends:
            cp.wait_send()

    return _Comm([part], [_shape(part.shape, part.dtype)], {0: 0}, [(3,), (3,)], start, finish, then)


def _phase_pair_exchange(full, then):
    _, R, C = full.shape
    rh = R // 2

    def copy(ins, outs, sems):
        x, y, c, _ = _place()
        return _remote(ins[0].at[:, pl.ds((1 - c) * rh, rh), :], outs[0], sems[0].at[0], sems[1].at[0], (x, y, 1 - c))

    return _Comm([full], [_shape((4, rh, C), full.dtype)], {}, [(1,), (1,)],
                 lambda ins, outs, sems: copy(ins, outs, sems).start(),
                 lambda ins, outs, sems: copy(ins, outs, sems).wait(), then)


def _phase_chip_scatter(parts, then):
    def copies(ins, outs, sems):
        x, y, c, chips = _place()
        me = 2 * x + y
        local = _Staged(ins[0].at[me], outs[0].at[me], sems[3], sems[2])
        sends = [_remote(ins[0].at[2 * px + py], outs[0].at[me], sems[0].at[k], sems[1].at[k], (px, py, c))
                 for k, (px, py) in enumerate(chips)]
        recvs = [_remote(outs[0].at[2 * px + py], outs[0].at[2 * px + py], sems[0].at[k], sems[1].at[k], (px, py, c))
                 for k, (px, py) in enumerate(chips)]
        return local, sends, recvs

    def start(ins, outs, sems):
        local, sends, _ = copies(ins, outs, sems)
        local.start()
        for cp in sends:
            cp.start()

    def finish(ins, outs, sems):
        local, sends, recvs = copies(ins, outs, sems)
        for cp in recvs:
            cp.wait_recv()
        for cp in sends:
            cp.wait_send()
        local.wait()

    return _Comm([parts], [_shape(parts.shape, parts.dtype)], {}, [(3,), (3,), (2,), (parts.shape[1:], parts.dtype)],
                 start, finish, then)


def _phase_pair_allgather(half, layer, depth, into, then):
    rh, C = half.shape

    def copies(ins, outs, sems):
        x, y, c, _ = _place()
        mine = outs[0].at[layer, pl.ds(c * rh, rh), :]
        theirs = outs[0].at[layer, pl.ds((1 - c) * rh, rh), :]
        return (_Staged(ins[0], mine, sems[3], sems[2]),
                _remote(ins[0], mine, sems[0].at[0], sems[1].at[0], (x, y, 1 - c)),
                _remote(theirs, theirs, sems[0].at[0], sems[1].at[0], (x, y, 1 - c)))

    def start(ins, outs, sems):
        local, send, _ = copies(ins, outs, sems)
        local.start()
        send.start()

    def finish(ins, outs, sems):
        local, send, recv = copies(ins, outs, sems)
        recv.wait_recv()
        send.wait_send()
        local.wait()

    ins = [half] if into is None else [half, into]
    return _Comm(ins, [_shape((depth, 2 * rh, C), half.dtype)], {} if into is None else {1: 0},
                 [(1,), (1,), (2,), ((rh, C), half.dtype)], start, finish, then)


def _comm_only(comms, name):
    plan = _Plan()
    for c in comms:
        plan.at(name, c)
    saved, _PLAN[0] = _PLAN[0], plan
    try:
        def body(o_ref):
            o_ref[...] = jnp.zeros_like(o_ref)

        _pallas(body, out_shape=[jax.ShapeDtypeStruct((8, 128), F32)], in_specs=[],
                out_specs=[pl.BlockSpec(memory_space=pltpu.VMEM)], name=name)()
    finally:
        _PLAN[0] = saved


def _gather_chips(items):
    n = len(items)

    def body(*refs):
        srcs, outs = refs[:n], refs[n:2 * n]
        send_sems, recv_sems, loc_sems = refs[2 * n:]
        x, y, c, chips = _place()
        me = 2 * x + y

        def half(t, chip, cc):
            rh = items[t].shape[1] // 2
            return outs[t].at[chip, :, pl.ds(cc * rh, rh), :]

        def copy(t, k, src, dst, to):
            return pltpu.make_async_remote_copy(src_ref=src, dst_ref=dst, send_sem=send_sems.at[t, k],
                                                recv_sem=recv_sems.at[t, k], device_id=to, device_id_type=MESH)

        local = [pltpu.make_async_copy(srcs[t], outs[t].at[me], loc_sems.at[t]) for t in range(n)]
        for cp in local:
            cp.start()
        sent = []
        for t in range(n):
            rh = items[t].shape[1] // 2
            mine = srcs[t].at[:, pl.ds(c * rh, rh), :]
            for k, (px, py) in enumerate(chips):
                sent.append(copy(t, k, mine, half(t, me, c), (px, py, c)))
                sent[-1].start()
        for k, (px, py) in enumerate(chips):
            for t in range(n):
                blk = half(t, 2 * px + py, c)
                copy(t, k, blk, blk, (px, py, c)).wait_recv()
                sent.append(copy(t, 3 + k, blk, blk, (x, y, 1 - c)))
                sent[-1].start()
        for k, (px, py) in enumerate(chips):
            for t in range(n):
                blk = half(t, 2 * px + py, 1 - c)
                copy(t, 3 + k, blk, blk, (x, y, 1 - c)).wait_recv()
        for cp in sent:
            cp.wait_send()
        for cp in local:
            cp.wait()

    return _pallas(
        body,
        out_shape=[_out((4,) + a.shape, a.dtype) for a in items],
        in_specs=[ANY] * n,
        out_specs=[ANY] * n,
        scratch_shapes=[pltpu.SemaphoreType.DMA((n, 6)), pltpu.SemaphoreType.DMA((n, 6)), pltpu.SemaphoreType.DMA((n,))],
        name="gather_weights",
    )(*items)


def _pair_exchange(items):
    n = len(items)

    def body(*refs):
        srcs, outs = refs[:n], refs[n:2 * n]
        send_sems, recv_sems = refs[2 * n:]
        x, y, c, _ = _place()
        cps = []
        for t in range(n):
            rh = items[t].shape[2] // 2
            cps.append(pltpu.make_async_remote_copy(
                src_ref=srcs[t].at[:, :, pl.ds((1 - c) * rh, rh), :], dst_ref=outs[t], send_sem=send_sems.at[t],
                recv_sem=recv_sems.at[t], device_id=(x, y, 1 - c), device_id_type=MESH))
            cps[-1].start()
        for cp in cps:
            cp.wait()

    return _pallas(
        body,
        out_shape=[_out(a.shape[:2] + (a.shape[2] // 2, a.shape[3]), a.dtype) for a in items],
        in_specs=[ANY] * n,
        out_specs=[ANY] * n,
        scratch_shapes=[pltpu.SemaphoreType.DMA((n,)), pltpu.SemaphoreType.DMA((n,))],
        name="reduce_pair_exchange",
    )(*items)


def _row_tile(rows, cols, itemsize=4, budget=1 << 20):
    t = rows
    while t % 2 == 0 and t // 2 >= 8 and (t // 2) % 8 == 0 and t * cols * itemsize > budget:
        t //= 2
    return t


def _pair_add(full, got, c, name):
    _, L, R, C = full.shape
    rh = R // 2
    tr = _row_tile(rh, C)
    nb = rh // tr

    def body(c_ref, a_ref, b_ref, o_ref):
        o_ref[...] = (a_ref[...] + b_ref[...]).astype(o_ref.dtype)

    grid_spec = pltpu.PrefetchScalarGridSpec(
        num_scalar_prefetch=1,
        grid=(4, L, nb),
        in_specs=[pl.BlockSpec((1, 1, tr, C), lambda j, l, i, c_ref: (j, l, c_ref[0] * nb + i, 0)),
                  pl.BlockSpec((1, 1, tr, C), lambda j, l, i, c_ref: (j, l, i, 0))],
        out_specs=pl.BlockSpec((1, 1, tr, C), lambda j, l, i, c_ref: (j, l, i, 0)),
    )
    return _pallas(
        body,
        out_shape=_out((4, L, rh, C), BF16),
        grid_spec=grid_spec,
        compiler_params=_cparams(("parallel", "parallel", "parallel"), 8 * tr * C * 4),
        name=name,
    )(jnp.reshape(c, (1,)).astype(jnp.int32), full, got)


def _chip_scatter(items):
    n = len(items)

    def body(*refs):
        srcs, outs = refs[:n], refs[n:2 * n]
        send_sems, recv_sems, loc_sems = refs[2 * n:]
        x, y, c, chips = _place()
        me = 2 * x + y
        local = [pltpu.make_async_copy(srcs[t].at[me], outs[t].at[me], loc_sems.at[t]) for t in range(n)]
        for cp in local:
            cp.start()
        cps = []
        for t in range(n):
            for k, (px, py) in enumerate(chips):
                cps.append(pltpu.make_async_remote_copy(
                    src_ref=srcs[t].at[2 * px + py], dst_ref=outs[t].at[me], send_sem=send_sems.at[t, k],
                    recv_sem=recv_sems.at[t, k], device_id=(px, py, c), device_id_type=MESH))
                cps[-1].start()
        for t in range(n):
            for k, (px, py) in enumerate(chips):
                blk = outs[t].at[2 * px + py]
                pltpu.make_async_remote_copy(src_ref=blk, dst_ref=blk, send_sem=send_sems.at[t, k], recv_sem=recv_sems.at[t, k],
                                             device_id=(px, py, c), device_id_type=MESH).wait_recv()
        for cp in cps:
            cp.wait_send()
        for cp in local:
            cp.wait()

    return _pallas(
        body,
        out_shape=[_out(a.shape, a.dtype) for a in items],
        in_specs=[ANY] * n,
        out_specs=[ANY] * n,
        scratch_shapes=[pltpu.SemaphoreType.DMA((n, 3)), pltpu.SemaphoreType.DMA((n, 3)), pltpu.SemaphoreType.DMA((n,))],
        name="reduce_chip_scatter",
    )(*items)


def _sum_chips(parts, name):
    _, L, rh, C = parts.shape
    tr = _row_tile(rh, C)

    def body(p_ref, o_ref):
        acc = p_ref[0].astype(F32)
        for j in range(1, 4):
            acc = acc + p_ref[j].astype(F32)
        o_ref[...] = acc

    return _pallas(
        body,
        out_shape=_out((L, rh, C), F32),
        grid=(L, rh // tr),
        in_specs=[pl.BlockSpec((4, 1, tr, C), lambda l, i: (0, l, i, 0))],
        out_specs=pl.BlockSpec((1, tr, C), lambda l, i: (l, i, 0)),
        compiler_params=_cparams(("parallel", "parallel"), 16 * tr * C * 4),
        name=name,
    )(parts)


def _pair_allgather(groups):
    flat = [(w, l) for w, grp in enumerate(groups) for l in range(len(grp))]
    items = [groups[w][l] for w, l in flat]
    n, nw = len(items), len(groups)

    def body(*refs):
        srcs, outs = refs[:n], refs[n:n + nw]
        send_sems, recv_sems, loc_sems = refs[n + nw:]
        x, y, c, _ = _place()
        cps, local = [], []
        for t, (w, l) in enumerate(flat):
            rh = items[t].shape[1]
            mine = outs[w].at[pl.ds(l, 1), pl.ds(c * rh, rh), :]
            local.append(pltpu.make_async_copy(srcs[t], mine, loc_sems.at[t]))
            local[-1].start()
            cps.append(pltpu.make_async_remote_copy(src_ref=srcs[t], dst_ref=mine, send_sem=send_sems.at[t],
                                                    recv_sem=recv_sems.at[t], device_id=(x, y, 1 - c), device_id_type=MESH))
            cps[-1].start()
        for t, (w, l) in enumerate(flat):
            rh = items[t].shape[1]
            theirs = outs[w].at[pl.ds(l, 1), pl.ds((1 - c) * rh, rh), :]
            pltpu.make_async_remote_copy(src_ref=theirs, dst_ref=theirs, send_sem=send_sems.at[t], recv_sem=recv_sems.at[t],
                                         device_id=(x, y, 1 - c), device_id_type=MESH).wait_recv()
        for cp in cps:
            cp.wait_send()
        for cp in local:
            cp.wait()

    return _pallas(
        body,
        out_shape=[_out((len(grp), 2 * grp[0].shape[1], grp[0].shape[2]), grp[0].dtype) for grp in groups],
        in_specs=[ANY] * n,
        out_specs=[ANY] * nw,
        scratch_shapes=[pltpu.SemaphoreType.DMA((n,)), pltpu.SemaphoreType.DMA((n,)), pltpu.SemaphoreType.DMA((n,))],
        name="reduce_pair_allgather",
    )(*items)


def _allreduce_small(v):
    R = v.shape[0]

    def body(v_ref, o_ref, slots, send_sems, recv_sems):
        x, y, c, _ = _place()
        me = 4 * x + 2 * y + c
        slots[me] = v_ref[...]
        cps = []
        for r in range(1, 8):
            px, py, pc = x ^ (r >> 2), y ^ ((r >> 1) & 1), c ^ (r & 1)
            cps.append(pltpu.make_async_remote_copy(src_ref=v_ref, dst_ref=slots.at[me], send_sem=send_sems.at[r - 1],
                                                    recv_sem=recv_sems.at[r - 1], device_id=(px, py, pc), device_id_type=MESH))
            cps[-1].start()
        for r in range(1, 8):
            px, py, pc = x ^ (r >> 2), y ^ ((r >> 1) & 1), c ^ (r & 1)
            theirs = slots.at[4 * px + 2 * py + pc]
            pltpu.make_async_remote_copy(src_ref=theirs, dst_ref=theirs, send_sem=send_sems.at[r - 1], recv_sem=recv_sems.at[r - 1],
                                         device_id=(px, py, pc), device_id_type=MESH).wait_recv()
        acc = slots[0]
        for j in range(1, 8):
            acc = acc + slots[j]
        o_ref[...] = acc
        for cp in cps:
            cp.wait_send()

    return pl.pallas_call(
        body,
        out_shape=jax.ShapeDtypeStruct(v.shape, F32),
        in_specs=[pl.BlockSpec(memory_space=pltpu.VMEM)],
        out_specs=pl.BlockSpec(memory_space=pltpu.VMEM),
        scratch_shapes=[pltpu.VMEM((8, R, 128), F32), pltpu.SemaphoreType.DMA((7,)), pltpu.SemaphoreType.DMA((7,))],
        name="allreduce_small",
    )(v)


def _adamw_math(w, g, m, v):
    m = ADAM_B1 * m + (1.0 - ADAM_B1) * g
    v = ADAM_B2 * v + (1.0 - ADAM_B2) * (g * g)
    m_hat = m / (1.0 - ADAM_B1 ** ADAM_STEP)
    v_hat = v / (1.0 - ADAM_B2 ** ADAM_STEP)
    delta = -ADAM_LR * (m_hat / (jnp.sqrt(v_hat) + ADAM_EPS) + ADAM_WD * w)
    return delta, m, v


def _adamw(ws, gs, ms, vs, name):
    n = len(ws)
    tiles = [_row_tile(w.shape[1], w.shape[2], budget=1 << 19) for w in ws]
    per_layer = [w.shape[1] // t for w, t in zip(ws, tiles)]
    steps = [w.shape[0] * p for w, p in zip(ws, per_layer)]
    starts = [sum(steps[:k]) for k in range(n)]

    def body(*refs):
        i = pl.program_id(0)
        for k in range(n):
            w_ref, g_ref, m_ref, v_ref = (refs[j * n + k] for j in range(4))
            outs = [refs[(4 + j) * n + k] for j in range(3)]

            @pl.when((i >= starts[k]) & (i < starts[k] + steps[k]))
            def _(w_ref=w_ref, g_ref=g_ref, m_ref=m_ref, v_ref=v_ref, outs=outs):
                outs[0][...], outs[1][...], outs[2][...] = _adamw_math(w_ref[...], g_ref[...], m_ref[...], v_ref[...])

    def spec(k):
        def index(i):
            local = jnp.clip(i - starts[k], 0, steps[k] - 1)
            return local // per_layer[k], local % per_layer[k], 0
        return pl.BlockSpec((None, tiles[k], ws[k].shape[2]), index)

    specs = [spec(k) for k in range(n)]
    outs = [_out(w.shape, F32) for w in ws]
    res = _pallas(
        body,
        out_shape=tuple(outs * 3),
        grid=(sum(steps),),
        in_specs=specs * 4,
        out_specs=tuple(specs * 3),
        compiler_params=_cparams(("arbitrary",), sum(16 * t * w.shape[2] * 4 for w, t in zip(ws, tiles))),
        name=name,
    )(*ws, *gs, *ms, *vs)
    return res[:n], res[n:2 * n], res[2 * n:]


def _adamw_small(ws, gs, ms, vs):
    n = len(ws)

    def body(*refs):
        for t in range(n):
            w_ref, g_ref, m_ref, v_ref = (refs[k * n + t] for k in range(4))
            d_ref, nm_ref, nv_ref = (refs[(4 + k) * n + t] for k in range(3))
            d_ref[...], nm_ref[...], nv_ref[...] = _adamw_math(w_ref[...], g_ref[...], m_ref[...], v_ref[...])

    vmem = pl.BlockSpec(memory_space=pltpu.VMEM)
    outs = [jax.ShapeDtypeStruct(w.shape, F32) for w in ws]
    res = pl.pallas_call(
        body,
        out_shape=outs * 3,
        in_specs=[vmem] * (4 * n),
        out_specs=[vmem] * (3 * n),
        name="adamw_small",
    )(*ws, *gs, *ms, *vs)
    return res[:n], res[n:2 * n], res[2 * n:]


IN_SIZES = (192, 192, 384, 384, 16, 512, 384, 384, 384)
IN_OFFS = tuple(int(v) for v in np.cumsum((0,) + IN_SIZES))
SMALL = ("norm_mix", "w_gla_gate", "b_gla_gate", "gla_norm", "b_dw", "conv_ln_g", "conv_ln_b", "rel_bias", "norm_ffn")


def _pad_cols(a, n):
    return jnp.pad(a, ((0, 0), (0, n - a.shape[1])))


def _split_w_in(w):
    s = [w[:, IN_OFFS[i]:IN_OFFS[i + 1]] for i in range(9)]
    w_gla = jnp.concatenate([_pad_cols(s[0], KW), _pad_cols(s[1], KW), s[2], s[3], _pad_cols(s[4], LRW)], axis=1)
    return w_gla, s[5], jnp.concatenate(s[6:9], axis=1)


def _join_w_in(g):
    gg = g["w_gla"]
    return jnp.concatenate([gg[:, 0:192], gg[:, KW:KW + 192], gg[:, 2 * KW:2 * KW + VW], gg[:, 2 * KW + VW:2 * KW + 2 * VW],
                            gg[:, 2 * KW + 2 * VW:2 * KW + 2 * VW + 16], g["w_conv"], g["w_att"]],
                           axis=1)


def _pack(arrs, rows):
    flat = jnp.concatenate([a.reshape(-1) for a in arrs])
    return jnp.pad(flat, (0, rows * 128 - flat.shape[0])).reshape(rows, 128)


def _unpack(packed, shapes):
    flat = packed.reshape(-1)
    out, off = [], 0
    for s in shapes:
        n = int(np.prod(s))
        out.append(flat[off:off + n].reshape(s))
        off += n
    return out


def kernel(x, norm_mix, w_in, w_gla_gate, b_gla_gate, gla_norm, w_dw, b_dw, conv_ln_g, conv_ln_b, rel_bias, w_out, norm_ffn, w_up, w_down, norm_final, loss_target, m_norm_mix, m_w_in, m_w_gla_gate, m_b_gla_gate, m_gla_norm, m_w_dw, m_b_dw, m_conv_ln_g, m_conv_ln_b, m_rel_bias, m_w_out, m_norm_ffn, m_w_up, m_w_down, m_norm_final, v_norm_mix, v_w_in, v_w_gla_gate, v_b_gla_gate, v_gla_norm, v_w_dw, v_b_dw, v_conv_ln_g, v_conv_ln_b, v_rel_bias, v_w_out, v_norm_ffn, v_w_up, v_w_down, v_norm_final):
    P = dict(norm_mix=norm_mix, w_in=w_in, w_gla_gate=w_gla_gate, b_gla_gate=b_gla_gate, gla_norm=gla_norm, w_dw=w_dw, b_dw=b_dw,
             conv_ln_g=conv_ln_g, conv_ln_b=conv_ln_b, rel_bias=rel_bias, w_out=w_out, norm_ffn=norm_ffn, w_up=w_up,
             w_down=w_down, norm_final=norm_final)
    Mo = dict(norm_mix=m_norm_mix, w_in=m_w_in, w_gla_gate=m_w_gla_gate, b_gla_gate=m_b_gla_gate, gla_norm=m_gla_norm, w_dw=m_w_dw,
              b_dw=m_b_dw, conv_ln_g=m_conv_ln_g, conv_ln_b=m_conv_ln_b, rel_bias=m_rel_bias, w_out=m_w_out, norm_ffn=m_norm_ffn,
              w_up=m_w_up, w_down=m_w_down, norm_final=m_norm_final)
    Vo = dict(norm_mix=v_norm_mix, w_in=v_w_in, w_gla_gate=v_w_gla_gate, b_gla_gate=v_b_gla_gate, gla_norm=v_gla_norm, w_dw=v_w_dw,
              b_dw=v_b_dw, conv_ln_g=v_conv_ln_g, conv_ln_b=v_conv_ln_b, rel_bias=v_rel_bias, w_out=v_w_out, norm_ffn=v_norm_ffn,
              w_up=v_w_up, w_down=v_w_down, norm_final=v_norm_final)
    depth = w_in.shape[0]
    xi, yi, ci = lax.axis_index("x"), lax.axis_index("y"), lax.axis_index("c")
    chip = 2 * xi + yi

    plan = _Plan()
    _PLAN[0] = plan
    layers = [dict(
        norm_mix=norm_mix[l][None], wg=jnp.pad(w_gla_gate[l], ((0, LRW - 16), (0, KW - 192))),
        bg=_pad_cols(b_gla_gate[l][None], KW), gla_norm=gla_norm[l][None], b_dw=b_dw[l][None], ln_g=conv_ln_g[l][None],
        ln_b=conv_ln_b[l][None], rel_bias=rel_bias[l], norm_ffn=norm_ffn[l][None]) for l in range(depth)]

    def have_w_in(l, full):
        layers[l]["w_gla"], layers[l]["w_conv"], layers[l]["w_att"] = _split_w_in(jnp.transpose(full, (1, 0, 2)).reshape(D, -1))

    def have_w_out(l, full):
        w = full.reshape(D, D)
        layers[l]["w_out_g"], layers[l]["w_out_c"], layers[l]["w_out_a"] = w[0:VW], w[VW:VW + CW], w[VW + CW:]

    def have_w_up(l, full):
        layers[l]["w_up"] = full

    def have_w_down(l, full):
        layers[l]["w_down"] = full.reshape(D_FF, D)

    def have_w_dw(full):
        taps = full.reshape(4, depth, HALO, CW // 4)
        for l in range(depth):
            layers[l]["w_dw"] = jnp.transpose(taps[:, l], (1, 0, 2)).reshape(HALO, CW)

    first_d2d = []

    def first_ici(shard, have):
        return _phase_gather_ici(shard, lambda outs: first_d2d.append(_phase_gather_d2d(outs[0], lambda done: have(done[0]))))

    w_dw_pad = jnp.pad(w_dw, ((0, 0), (0, HALO - CK), (0, 0))).reshape(depth * HALO, CW // 4)
    _comm_only([first_ici(w_in[0].astype(BF16), functools.partial(have_w_in, 0)), first_ici(w_dw_pad, have_w_dw)],
               "gather_first_ici")
    _comm_only(first_d2d, "gather_first_d2d")

    def gather_behind(shard, ici_call, d2d_call, have):
        plan.at(ici_call, _phase_gather_ici(
            shard, lambda outs: plan.at(d2d_call, _phase_gather_d2d(outs[0], lambda done: have(done[0])))))

    for l in range(depth):
        if l > 0:
            gather_behind(w_in[l].astype(BF16), f"l{l - 1}_mlp_up", f"l{l - 1}_mlp_down", functools.partial(have_w_in, l))
        gather_behind(w_out[l].astype(BF16), f"l{l - 1}_mlp_down" if l > 0 else "l0_proj", f"l{l}_gla_fwd",
                      functools.partial(have_w_out, l))
        if l > 0:
            gather_behind(w_up[l].astype(BF16), f"l{l}_proj", f"l{l}_gla_fwd", functools.partial(have_w_up, l))
            gather_behind(w_down[l].astype(BF16), f"l{l}_gla_fwd", f"l{l}_att_fwd", functools.partial(have_w_down, l))
        else:
            gather_behind(w_up[l].astype(BF16), f"l{l}_gla_fwd", f"l{l}_att_fwd", functools.partial(have_w_up, l))
            gather_behind(w_down[l].astype(BF16), f"l{l}_att_fwd", f"l{l}_mlp_up", functools.partial(have_w_down, l))

    reduced = {}
    last_swap = []

    def reduce_calls(name, l):
        if name == "w_down":
            return f"l{l}_mlp_up_dx", f"l{l}_gla_bwd", f"l{l}_conv_bwd_dc"
        if name == "w_up":
            return f"l{l}_out_dx", f"l{l}_att_bwd", f"l{l}_proj_dw"
        if name == "w_out":
            return f"l{l}_gla_bwd", f"l{l}_conv_bwd_dc", f"l{l}_att_bwd"
        if l > 0:
            return f"l{l}_proj_dx", f"l{l - 1}_mlp_down_dw", f"l{l - 1}_mlp_up_dx"
        return "l0_proj_dx", "adamw_early", None

    def reduce_behind(l, name, full):
        calls = reduce_calls(name, l)

        def swapped(outs):
            pair = _pair_add(full[:, None], outs[0][:, None], ci, f"reduce_pair_add_{name}{l}")[:, 0]
            plan.at(calls[1], _phase_chip_scatter(pair, scattered))

        def scattered(outs):
            half = _sum_chips(outs[0][:, None], f"reduce_sum_chips_{name}{l}")[0]
            phase = _phase_pair_allgather(half, l, depth, reduced.get(name), gathered)
            if calls[2] is None:
                last_swap.append(phase)
            else:
                plan.at(calls[2], phase)

        def gathered(outs):
            reduced[name] = outs[0]

        plan.at(calls[0], _phase_pair_exchange(full, swapped))

    loss_part, grad_x, grads, g_final = _local_step(x[0], loss_target[0], layers, norm_final[None], reduce_behind)
    loss = lax.psum(loss_part[0, 0], ("x", "y", "c"))

    G, delta, new_m, new_v = {}, {}, {}, {}
    early = ("w_down", "w_up", "w_out")
    for name in early:
        G[name] = reduced[name]
    ds, nms, nvs = _adamw([P[k] for k in early], [G[k] for k in early], [Mo[k] for k in early], [Vo[k] for k in early],
                          "adamw_early")
    for i, name in enumerate(early):
        delta[name], new_m[name], new_v[name] = ds[i], nms[i], nvs[i]
    _PLAN[0] = None
    assert not plan.by_call, sorted(plan.by_call)

    small_g = []
    for l in range(depth):
        g = grads[l]
        small_g += [g["norm_mix"], g["wg"][0:16, 0:192], g["bg"][:, 0:192], g["gla_norm"], g["b_dw"], g["ln_g"], g["ln_b"],
                    g["rel_bias"], g["norm_ffn"], g["w_dw"][0:CK]]
    small_g.append(g_final)
    small_shapes = [a.shape for a in small_g]
    n_small = sum(int(np.prod(s)) for s in small_shapes)
    rows = -(-n_small // 1024) * 8
    red = _unpack(_allreduce_small(_pack(small_g, rows)), small_shapes)
    per = len(SMALL) + 1
    for i, name in enumerate(SMALL):
        G[name] = jnp.stack([red[l * per + i].reshape(P[name].shape[1:]) for l in range(depth)])
    gw_dw_all = jnp.stack([red[l * per + len(SMALL)] for l in range(depth)])
    G["w_dw"] = lax.dynamic_slice_in_dim(gw_dw_all, chip * (CW // 4), CW // 4, axis=2)
    G["norm_final"] = red[-1].reshape(norm_final.shape)

    _comm_only(last_swap, "reduce_pair_allgather_last")
    G["w_in"] = reduced["w_in"]
    ds, nms, nvs = _adamw([w_in], [G["w_in"]], [m_w_in], [v_w_in], "adamw_w_in")
    delta["w_in"], new_m["w_in"], new_v["w_in"] = ds[0], nms[0], nvs[0]

    small_names = list(SMALL) + ["w_dw", "norm_final"]
    two_d = lambda a: a.reshape(-1, a.shape[-1])
    ds, nms, nvs = _adamw_small([two_d(P[k]) for k in small_names], [two_d(G[k]) for k in small_names],
                                [two_d(Mo[k]) for k in small_names], [two_d(Vo[k]) for k in small_names])
    for i, name in enumerate(small_names):
        shp = P[name].shape
        delta[name], new_m[name], new_v[name] = ds[i].reshape(shp), nms[i].reshape(shp), nvs[i].reshape(shp)

    order = ["norm_mix", "w_in", "w_gla_gate", "b_gla_gate", "gla_norm", "w_dw", "b_dw", "conv_ln_g", "conv_ln_b", "rel_bias",
             "w_out", "norm_ffn", "w_up", "w_down", "norm_final"]
    return (loss, grad_x[None], *[G[k] for k in order], *[delta[k] for k in order], *[new_m[k] for k in order],
            *[new_v[k] for k in order])
```

```python
import functools

import numpy as np
import jax
import jax.numpy as jnp
from jax import lax
from jax.experimental import pallas as pl
from jax.experimental.pallas import tpu as pltpu

F32 = jnp.float32
BF16 = jnp.bfloat16
HI = lax.Precision.HIGHEST

D = 1024
CHUNK = 64
GLA_DK, GLA_DV, GLA_H = 48, 96, 4
KW = 256
VW = 384
LRW = 128
GLA_TAU = 16.0
CW = 256
CK = 31
AW = 384
AH = 6
BAND = 576
LEFT = 512
D_FF = 4096
EPS = 1e-6
NEG = -1e30
N_REL = 257

GLA_COLS = 2 * KW + 2 * VW + LRW
CONV_COLS = 2 * CW
ATT_COLS = 3 * AW

ADAM_LR, ADAM_B1, ADAM_B2, ADAM_EPS, ADAM_WD, ADAM_STEP = 0.001, 0.9, 0.999, 1e-08, 0.01, 10

VMEM_CAP = 56 * 1024 * 1024
MESH = pl.DeviceIdType.MESH


def _cparams(sem, vmem_bytes):
    limit = int(min(VMEM_CAP, max(vmem_bytes * 5 // 4 + (4 << 20), 16 << 20)))
    return pltpu.CompilerParams(dimension_semantics=sem, vmem_limit_bytes=limit)


def _out(shape, dtype):
    return pltpu.HBM(tuple(shape), dtype)


class _Comm:
    def __init__(self, ins, outs, aliases, sems, start, finish, then=None):
        self.ins, self.outs, self.aliases, self.sems = list(ins), list(outs), dict(aliases), list(sems)
        self.start, self.finish, self.then = start, finish, then


class _Plan:
    def __init__(self):
        self.by_call = {}

    def at(self, call, comm):
        self.by_call.setdefault(call, []).append(comm)

    def take(self, call):
        return self.by_call.pop(call, [])


_PLAN = [None]


def _pin(a):
    return pltpu.with_memory_space_constraint(a, pltpu.HBM) if jnp.issubdtype(a.dtype, jnp.floating) else a


def _pallas(body, **kw):
    comms = _PLAN[0].take(kw.get("name")) if _PLAN[0] is not None else []
    if not comms:
        call = pl.pallas_call(body, **kw)
        return lambda *args: call(*[_pin(a) for a in args])

    grid = tuple(kw.get("grid", ()))
    single = not isinstance(kw["out_shape"], (tuple, list))
    out_shape = [kw["out_shape"]] if single else list(kw["out_shape"])
    out_specs = [kw["out_specs"]] if single else list(kw["out_specs"])
    in_specs = list(kw["in_specs"])
    scratch = list(kw.get("scratch_shapes", ()))
    n_in, n_out, n_scr = len(in_specs), len(out_shape), len(scratch)
    c_in = sum(len(c.ins) for c in comms)
    c_out = sum(len(c.outs) for c in comms)
    aliases = dict(kw.get("input_output_aliases", {}))
    i0, o0 = n_in, n_out
    for c in comms:
        for i, o in c.aliases.items():
            aliases[i0 + i] = o0 + o
        i0 += len(c.ins)
        o0 += len(c.outs)

    def wrapped(*refs):
        ins, c_ins = refs[:n_in], refs[n_in:n_in + c_in]
        outs, c_outs = refs[n_in + c_in:n_in + c_in + n_out], refs[n_in + c_in + n_out:n_in + c_in + n_out + c_out]
        scr, c_sems = refs[n_in + c_in + n_out + c_out:][:n_scr], refs[n_in + c_in + n_out + c_out + n_scr:]

        def each(what):
            i0 = o0 = s0 = 0
            for c in comms:
                getattr(c, what)(c_ins[i0:i0 + len(c.ins)], c_outs[o0:o0 + len(c.outs)], c_sems[s0:s0 + len(c.sems)])
                i0, o0, s0 = i0 + len(c.ins), o0 + len(c.outs), s0 + len(c.sems)

        if grid:
            first = functools.reduce(jnp.logical_and, [pl.program_id(a) == 0 for a in range(len(grid))])
            last = functools.reduce(jnp.logical_and, [pl.program_id(a) == grid[a] - 1 for a in range(len(grid))])
            pl.when(first)(lambda: each("start"))
            body(*ins, *outs, *scr)
            pl.when(last)(lambda: each("finish"))
        else:
            each("start")
            body(*ins, *outs, *scr)
            each("finish")

    kw = dict(kw)
    kw["in_specs"] = in_specs + [ANY] * c_in
    kw["out_shape"] = out_shape + [_out(s.shape, s.dtype) for c in comms for s in c.outs]
    kw["out_specs"] = out_specs + [ANY] * c_out
    staging = [s for c in comms for s in c.sems if len(s) == 2 and not isinstance(s[1], int)]
    kw["scratch_shapes"] = scratch + [pltpu.VMEM(*s) if s in staging else pltpu.SemaphoreType.DMA(s) for c in comms for s in c.sems]
    kw["input_output_aliases"] = aliases
    extra = sum(_nbytes(*s) for s in staging)
    old = kw.get("compiler_params")
    limit = (old.vmem_limit_bytes if old is not None else 16 << 20) + extra
    kw["compiler_params"] = pltpu.CompilerParams(
        dimension_semantics=old.dimension_semantics if old is not None else None, vmem_limit_bytes=int(min(VMEM_CAP, limit)))
    call = pl.pallas_call(wrapped, **kw)

    def run(*args):
        res = call(*[_pin(a) for a in args], *[_pin(a) for c in comms for a in c.ins])
        o0 = n_out
        for c in comms:
            if c.then is not None:
                c.then(res[o0:o0 + len(c.outs)])
            o0 += len(c.outs)
        return res[0] if single else res[:n_out]

    return run


def _nbytes(shape, dtype):
    return int(np.prod(shape)) * jnp.dtype(dtype).itemsize


def _sigmoid(x):
    return 1.0 / (1.0 + jnp.exp(-x))


_DIMS = {"nn": (((1,), (0,)), ((), ())), "nt": (((1,), (1,)), ((), ())), "tn": (((0,), (0,)), ((), ()))}


def _mm(a, b, *, mode, out_dtype, name, tm=512, tn=None, tk=None, a_pro=None, epi=None, extra=None,
        b_chips=False, out_chips=False):
    b2 = (b.shape[1], 4 * b.shape[2]) if b_chips else b.shape
    if mode == "nn":
        (M, K), (K2, N) = a.shape, b2
    elif mode == "nt":
        (M, K), (N, K2) = a.shape, b2
    else:
        (K, M), (K2, N) = a.shape, b2
    assert K == K2, (a.shape, b.shape, mode)
    tm = min(tm, M)
    tn = N if tn is None else min(tn, N)
    tk = K if tk is None else min(tk, K)
    assert M % tm == 0 and N % tn == 0 and K % tk == 0, (M, N, K, tm, tn, tk)
    nk = K // tk
    a_blk = (tk, tm) if mode == "tn" else (tm, tk)
    a_map = (lambda i, j, k: (k, i)) if mode == "tn" else (lambda i, j, k: (i, k))
    b_blk = (tn, tk) if mode == "nt" else (tk, tn)
    b_map = (lambda i, j, k: (j, k)) if mode == "nt" else (lambda i, j, k: (k, j))
    if b_chips:
        per = b.shape[2] // b_blk[1]
        assert b.shape[2] % b_blk[1] == 0 and mode != "tn"
        flat_map = b_map
        b_map = lambda i, j, k: (flat_map(i, j, k)[1] // per, flat_map(i, j, k)[0], flat_map(i, j, k)[1] % per)
        b_blk = (None,) + b_blk
    in_specs = [pl.BlockSpec(a_blk, a_map), pl.BlockSpec(b_blk, b_map)]
    args = [a, b]
    if epi is not None:
        in_specs.append(pl.BlockSpec((tm, tn), lambda i, j, k: (i, j)))
        args.append(extra)

    def body(*refs):
        a_ref, b_ref = refs[0], refs[1]
        e_ref = refs[2] if epi is not None else None
        o_ref = refs[3] if epi is not None else refs[2]
        av = a_ref[...]
        if a_pro == "relu2":
            af = jnp.maximum(av.astype(F32), 0.0)
            av = af * af
        p = lax.dot_general(av.astype(BF16), b_ref[...].astype(BF16), _DIMS[mode], preferred_element_type=F32)

        def finish(acc):
            if epi == "add":
                acc = acc + e_ref[...].astype(F32)
            elif epi == "relu2grad":
                acc = acc * (2.0 * jnp.maximum(e_ref[...].astype(F32), 0.0))
            o_ref[...] = acc.astype(o_ref.dtype)

        if nk == 1:
            finish(p)
        else:
            acc_ref = refs[-1]
            k = pl.program_id(2)

            @pl.when(k == 0)
            def _():
                acc_ref[...] = p

            @pl.when(k > 0)
            def _():
                acc_ref[...] += p

            @pl.when(k == nk - 1)
            def _():
                finish(acc_ref[...])

    vm = 2 * (_nbytes(a_blk, a.dtype) + _nbytes((tk, tn), b.dtype) + _nbytes((tm, tn), out_dtype))
    vm += 3 * _nbytes((tm, tn), F32)
    if epi is not None:
        vm += 2 * _nbytes((tm, tn), extra.dtype)
    if out_chips:
        per_out = N // 4 // tn
        assert N % (4 * tn) == 0
        out_shape = _out((4, M, N // 4), out_dtype)
        out_spec = pl.BlockSpec((None, tm, tn), lambda i, j, k: (j // per_out, i, j % per_out))
    else:
        out_shape = _out((M, N), out_dtype)
        out_spec = pl.BlockSpec((tm, tn), lambda i, j, k: (i, j))
    return _pallas(
        body,
        out_shape=out_shape,
        grid=(M // tm, N // tn, nk),
        in_specs=in_specs,
        out_specs=out_spec,
        scratch_shapes=[pltpu.VMEM((tm, tn), F32)] if nk > 1 else [],
        compiler_params=_cparams(("parallel", "parallel", "arbitrary"), vm),
        name=name,
    )(*args)


def _mm_fan(a, bs, *, mode, out_dtypes, name, tm=512):
    M, K = a.shape
    ns = [b.shape[1] if mode == "nn" else b.shape[0] for b in bs]
    n = len(bs)

    def body(*refs):
        av = refs[0][...].astype(BF16)
        for i in range(n):
            refs[1 + n + i][...] = lax.dot_general(av, refs[1 + i][...], _DIMS[mode],
                                                   preferred_element_type=F32).astype(refs[1 + n + i].dtype)

    vm = 2 * _nbytes((tm, K), a.dtype) + sum(2 * _nbytes(b.shape, b.dtype) + 3 * _nbytes((tm, nn), F32) for b, nn in zip(bs, ns))
    return _pallas(
        body,
        out_shape=tuple(_out((M, nn), dt) for nn, dt in zip(ns, out_dtypes)),
        grid=(M // tm,),
        in_specs=[pl.BlockSpec((tm, K), lambda i: (i, 0))] + [pl.BlockSpec(b.shape, lambda i: (0, 0)) for b in bs],
        out_specs=tuple(pl.BlockSpec((tm, nn), lambda i: (i, 0)) for nn in ns),
        compiler_params=_cparams(("parallel",), vm),
        name=name,
    )(a, *bs)


def _mm_sum(as_, bs, *, mode, out_dtype, name, extra=None, tm=512):
    M = as_[0].shape[0]
    N = bs[0].shape[1] if mode == "nn" else bs[0].shape[0]
    n = len(as_)

    def body(*refs):
        acc = None
        for i in range(n):
            p = lax.dot_general(refs[i][...].astype(BF16), refs[n + i][...], _DIMS[mode], preferred_element_type=F32)
            acc = p if acc is None else acc + p
        if extra is not None:
            acc = acc + refs[2 * n][...].astype(F32)
        refs[-1][...] = acc.astype(refs[-1].dtype)

    in_specs = [pl.BlockSpec((tm, a.shape[1]), lambda i: (i, 0)) for a in as_]
    in_specs += [pl.BlockSpec(b.shape, lambda i: (0, 0)) for b in bs]
    args = list(as_) + list(bs)
    if extra is not None:
        in_specs.append(pl.BlockSpec((tm, N), lambda i: (i, 0)))
        args.append(extra)
    vm = sum(2 * _nbytes((tm, a.shape[1]), a.dtype) for a in as_) + sum(2 * _nbytes(b.shape, b.dtype) for b in bs)
    vm += 8 * _nbytes((tm, N), F32)
    return _pallas(
        body,
        out_shape=_out((M, N), out_dtype),
        grid=(M // tm,),
        in_specs=in_specs,
        out_specs=pl.BlockSpec((tm, N), lambda i: (i, 0)),
        compiler_params=_cparams(("parallel",), vm),
        name=name,
    )(*args)


def _mm_tn_multi(ops, pairs, *, name, tk=512):
    T = ops[0].shape[0]
    n, m = len(ops), len(pairs)
    shapes = [(ops[a].shape[1], ops[b].shape[1]) for a, b in pairs]

    def body(*refs):
        vals = [refs[i][...].astype(BF16) for i in range(n)]
        first = pl.program_id(0) == 0
        for j, (a, b) in enumerate(pairs):
            p = lax.dot_general(vals[a], vals[b], _DIMS["tn"], preferred_element_type=F32)
            o_ref = refs[n + j]

            @pl.when(first)
            def _(o_ref=o_ref, p=p):
                o_ref[...] = p

            @pl.when(jnp.logical_not(first))
            def _(o_ref=o_ref, p=p):
                o_ref[...] += p

    vm = sum(2 * _nbytes((tk, o.shape[1]), o.dtype) for o in ops) + sum(3 * _nbytes(s, F32) for s in shapes)
    return _pallas(
        body,
        out_shape=tuple(_out(s, F32) for s in shapes),
        grid=(T // tk,),
        in_specs=[pl.BlockSpec((tk, o.shape[1]), lambda k: (k, 0)) for o in ops],
        out_specs=tuple(pl.BlockSpec(s, lambda k: (0, 0)) for s in shapes),
        compiler_params=_cparams(("arbitrary",), vm),
        name=name,
    )(*ops)


def _rmsnorm_fwd(h, g, name, tm=512):
    T = h.shape[0]

    def body(h_ref, g_ref, o_ref):
        x = h_ref[...]
        r = lax.rsqrt(jnp.mean(x * x, axis=-1, keepdims=True) + EPS)
        o_ref[...] = (x * r * g_ref[...]).astype(o_ref.dtype)

    return _pallas(
        body,
        out_shape=_out((T, D), BF16),
        grid=(T // tm,),
        in_specs=[pl.BlockSpec((tm, D), lambda i: (i, 0)), pl.BlockSpec((1, D), lambda i: (0, 0))],
        out_specs=pl.BlockSpec((tm, D), lambda i: (i, 0)),
        compiler_params=_cparams(("parallel",), 8 * _nbytes((tm, D), F32)),
        name=name,
    )(h, g)


def _rmsnorm_bwd(dxn, h, g, dres, name, tm=512):
    T = h.shape[0]

    def body(dxn_ref, h_ref, g_ref, dres_ref, dh_ref, dg_ref):
        @pl.when(pl.program_id(0) == 0)
        def _():
            dg_ref[...] = jnp.zeros_like(dg_ref)

        x = h_ref[...]
        dy = dxn_ref[...].astype(F32)
        r = lax.rsqrt(jnp.mean(x * x, axis=-1, keepdims=True) + EPS)
        gy = dy * g_ref[...]
        dot = jnp.mean(x * gy, axis=-1, keepdims=True)
        dh_ref[...] = dres_ref[...] + r * gy - x * (r * r * r * dot)
        dg_ref[...] += jnp.sum(dy * x * r, axis=0, keepdims=True)

    row = pl.BlockSpec((tm, D), lambda i: (i, 0))
    vec = pl.BlockSpec((1, D), lambda i: (0, 0))
    return _pallas(
        body,
        out_shape=(_out((T, D), F32), _out((1, D), F32)),
        grid=(T // tm,),
        in_specs=[row, row, vec, row],
        out_specs=(row, vec),
        compiler_params=_cparams(("arbitrary",), 12 * _nbytes((tm, D), F32)),
        name=name,
    )(dxn, h, g, dres)


def _final_loss(h, g, target, name, tm=512):
    T = h.shape[0]

    def body(h_ref, g_ref, t_ref, loss_ref, dh_ref, dg_ref):
        @pl.when(pl.program_id(0) == 0)
        def _():
            dg_ref[...] = jnp.zeros_like(dg_ref)
            loss_ref[...] = jnp.zeros_like(loss_ref)

        x = h_ref[...]
        gg = g_ref[...]
        r = lax.rsqrt(jnp.mean(x * x, axis=-1, keepdims=True) + EPS)
        y = x * r * gg
        e = y - t_ref[...]
        loss_ref[...] += 0.5 * jnp.sum(jnp.mean(e * e, axis=-1, keepdims=True), axis=0, keepdims=True)
        dy = e * (1.0 / D)
        gy = dy * gg
        dot = jnp.mean(x * gy, axis=-1, keepdims=True)
        dh_ref[...] = r * gy - x * (r * r * r * dot)
        dg_ref[...] += jnp.sum(dy * x * r, axis=0, keepdims=True)

    row = pl.BlockSpec((tm, D), lambda i: (i, 0))
    vec = pl.BlockSpec((1, D), lambda i: (0, 0))
    one = pl.BlockSpec((1, 1), lambda i: (0, 0))
    return _pallas(
        body,
        out_shape=(_out((1, 1), F32), _out((T, D), F32), _out((1, D), F32)),
        grid=(T // tm,),
        in_specs=[row, vec, row],
        out_specs=(one, row, vec),
        compiler_params=_cparams(("arbitrary",), 12 * _nbytes((tm, D), F32)),
        name=name,
    )(h, g, target)


GLA_G = 8


def _gla_consts():
    i = np.arange(KW)[:, None]
    j = np.arange(VW)[None, :]
    mask = ((i // GLA_DK) == (j // GLA_DV)) & (i < GLA_H * GLA_DK)
    a = np.arange(VW)
    hm = ((a[:, None] // GLA_DV) == (a[None, :] // GLA_DV)).astype(np.float32)
    c = np.arange(CHUNK)
    low = (c[:, None] >= c[None, :]).astype(np.float32)
    return jnp.asarray(mask.astype(np.float32)), jnp.asarray(hm, BF16), jnp.asarray(low, BF16)


def _split(x):
    hi = x.astype(BF16)
    return hi, (x - hi.astype(F32)).astype(BF16)


def _dot_sel(a, b, dims, split):
    if split == "a":
        hi, lo = _split(a)
        return (lax.dot_general(hi, b, dims, preferred_element_type=F32) + lax.dot_general(lo, b, dims, preferred_element_type=F32))
    hi, lo = _split(b)
    return (lax.dot_general(a, hi, dims, preferred_element_type=F32) + lax.dot_general(a, lo, dims, preferred_element_type=F32))


def _dot3(a, b, dims):
    ah, al = _split(a)
    bh, bl = _split(b)
    return (lax.dot_general(ah, bh, dims, preferred_element_type=F32) + lax.dot_general(al, bh, dims, preferred_element_type=F32)
            + lax.dot_general(ah, bl, dims, preferred_element_type=F32))


def _dot3s(a_s, b_s, dims):
    (ah, al), (bh, bl) = a_s, b_s
    return (lax.dot_general(ah, bh, dims, preferred_element_type=F32) + lax.dot_general(al, bh, dims, preferred_element_type=F32)
            + lax.dot_general(ah, bl, dims, preferred_element_type=F32))


def _gla_group_common(p_ref, wg, bg):
    lr_s = _split(p_ref[:, 2 * KW + 2 * VW:GLA_COLS])
    wg_s = _split(wg)
    z = _dot3s(lr_s, wg_s, _DIMS["nn"]) + bg
    la = (jnp.minimum(z, 0.0) - jnp.log(1.0 + jnp.exp(-jnp.abs(z)))) * (1.0 / GLA_TAU)
    return lr_s, wg_s, z, _split(la)


def _gla_chunk_common(p_ref, rows, la_s, low, ones_v):
    q = p_ref[rows, 0:KW]
    k = p_ref[rows, KW:2 * KW]
    v = p_ref[rows, 2 * KW:2 * KW + VW]
    g = p_ref[rows, 2 * KW + VW:2 * KW + 2 * VW]
    la_h, la_l = la_s[0][rows], la_s[1][rows]
    cum = jnp.dot(low, la_h, preferred_element_type=F32) + jnp.dot(low, la_l, preferred_element_type=F32)
    endb = cum[CHUNK - 1:CHUNK, :]
    w = jnp.exp(endb - cum)
    a_full = jnp.exp(lax.dot_general(la_h, ones_v, _DIMS["tn"], preferred_element_type=F32)
                     + lax.dot_general(la_l, ones_v, _DIMS["tn"], preferred_element_type=F32))
    return q, k, v, g, w, endb, a_full


def _gla_fwd(p, wg, bg, gn, consts, name):
    T = p.shape[0]
    rb = CHUNK * GLA_G
    ng = T // rb
    mask, hm, low = consts[:3]
    scale = GLA_DK ** -0.5

    def body(p_ref, wg_ref, bg_ref, gn_ref, m_ref, hm_ref, l_ref, o_ref, st_ref, s_ref):
        @pl.when(pl.program_id(0) == 0)
        def _():
            s_ref[...] = jnp.zeros_like(s_ref)

        wg_v, bg_v, gn_v = wg_ref[...], bg_ref[...], gn_ref[...]
        ones_v = jnp.ones((CHUNK, VW), BF16)
        s_new = s_ref[...]
        _, _, _, la_s = _gla_group_common(p_ref, wg_v, bg_v)
        outs = []
        for c in range(GLA_G):
            rows = slice(c * CHUNK, (c + 1) * CHUNK)
            q, k, v, _, w, _, a_full = _gla_chunk_common(p_ref, rows, la_s, l_ref[...], ones_v)
            kd = (k * w).astype(BF16)
            kv = lax.dot_general(kd, v.astype(BF16), _DIMS["tn"], preferred_element_type=F32) * m_ref[...]
            s_new = a_full * s_new + kv
            st_ref[c] = s_new
            outs.append(jnp.dot((q * scale).astype(BF16), s_new.astype(BF16), preferred_element_type=F32))
        s_ref[...] = s_new
        o = jnp.concatenate(outs, axis=0)
        g = p_ref[:, 2 * KW + VW:2 * KW + 2 * VW]
        ms = _dot_sel(o * o, hm_ref[...], _DIMS["nn"], "a") * (1.0 / GLA_DV)
        o_ref[...] = (o * lax.rsqrt(ms + EPS) * gn_v * (g * _sigmoid(g))).astype(o_ref.dtype)

    full = lambda shape: pl.BlockSpec(shape, lambda i: tuple(0 for _ in shape))
    vm = 2 * _nbytes((rb, GLA_COLS), F32) + 2 * _nbytes((GLA_G, KW, VW), F32) + 12 * _nbytes((KW, VW), F32)
    return _pallas(
        body,
        out_shape=(_out((T, VW), BF16), _out((T // CHUNK, KW, VW), F32)),
        grid=(ng,),
        in_specs=[pl.BlockSpec((rb, GLA_COLS), lambda i: (i, 0)), full((LRW, KW)), full((1, KW)), full((1, VW)),
                  full((KW, VW)), full((VW, VW)), full((CHUNK, CHUNK))],
        out_specs=(pl.BlockSpec((rb, VW), lambda i: (i, 0)), pl.BlockSpec((GLA_G, KW, VW), lambda i: (i, 0, 0))),
        scratch_shapes=[pltpu.VMEM((KW, VW), F32)],
        compiler_params=_cparams(("arbitrary",), vm),
        name=name,
    )(p, wg, bg, gn, mask, hm, low)


def _gla_bwd(p, dy, states, wg, bg, gn, consts, name):
    T = p.shape[0]
    rb = CHUNK * GLA_G
    ng = T // rb
    mask, hm, low = consts[:3]
    scale = GLA_DK ** -0.5

    def body(p_ref, dy_ref, st_ref, sp_ref, wg_ref, bg_ref, gn_ref, m_ref, hm_ref, l_ref,
             dp_ref, dwg_ref, dbg_ref, dgn_ref, ga_ref):
        step = pl.program_id(0)

        @pl.when(step == 0)
        def _():
            ga_ref[...] = jnp.zeros_like(ga_ref)
            dwg_ref[...] = jnp.zeros_like(dwg_ref)
            dbg_ref[...] = jnp.zeros_like(dbg_ref)
            dgn_ref[...] = jnp.zeros_like(dgn_ref)

        first_group = step == ng - 1
        wg_v, bg_v, gn_v = wg_ref[...], bg_ref[...], gn_ref[...]
        ones_v = jnp.ones((CHUNK, VW), BF16)
        ones_8 = jnp.ones((8, VW), BF16)
        ga = ga_ref[...]
        lr_s, wg_s, z_all, la_s = _gla_group_common(p_ref, wg_v, bg_v)
        qss = [(p_ref[c * CHUNK:(c + 1) * CHUNK, 0:KW] * scale).astype(BF16) for c in range(GLA_G)]
        o = jnp.concatenate([jnp.dot(qss[c], st_ref[c].astype(BF16), preferred_element_type=F32) for c in range(GLA_G)], axis=0)
        g = p_ref[:, 2 * KW + VW:2 * KW + 2 * VW]
        dyv = dy_ref[...].astype(F32)
        r = lax.rsqrt(_dot_sel(o * o, hm_ref[...], _DIMS["nn"], "a") * (1.0 / GLA_DV) + EPS)
        on = o * r
        sg = _sigmoid(g)
        silu = g * sg
        d_on = dyv * gn_v * silu
        dp_ref[:, 2 * KW + VW:2 * KW + 2 * VW] = (dyv * on * gn_v * (sg * (1.0 + g * (1.0 - sg)))).astype(dp_ref.dtype)
        dgn_ref[...] += jnp.sum(dyv * on * silu, axis=0, keepdims=True)
        mo = _dot_sel(o * d_on, hm_ref[...], _DIMS["nn"], "a") * (1.0 / GLA_DV)
        dob_all = (r * d_on - o * (r * r * r) * mo).astype(BF16)
        dzs = [None] * GLA_G
        for c in reversed(range(GLA_G)):
            rows = slice(c * CHUNK, (c + 1) * CHUNK)
            _, k, v, _, w, endb, a_full = _gla_chunk_common(p_ref, rows, la_s, l_ref[...], ones_v)
            s_n = st_ref[c]
            if c > 0:
                s_prev = st_ref[c - 1]
            else:
                s_prev = jnp.where(first_group, 0.0, sp_ref[0])
            kd = k * w
            dob = dob_all[rows]
            dq = lax.dot_general(dob, s_n.astype(BF16), _DIMS["nt"], preferred_element_type=F32) * scale
            g_n = lax.dot_general(qss[c], dob, _DIMS["tn"], preferred_element_type=F32) * m_ref[...] + ga
            d_a = _dot_sel(ones_8, g_n * s_prev, _DIMS["nt"], "b")[0:1, :]
            g_nb = g_n.astype(BF16)
            dkd = lax.dot_general(v.astype(BF16), g_nb, _DIMS["nt"], preferred_element_type=F32)
            dv = jnp.dot(kd.astype(BF16), g_nb, preferred_element_type=F32)
            e = dkd * kd
            d_end = jnp.sum(e, axis=0, keepdims=True) + d_a * jnp.exp(endb)
            dla = _dot_sel(l_ref[...], -e, _DIMS["tn"], "b") + d_end
            dzs[c] = dla * (1.0 - _sigmoid(z_all[rows])) * (1.0 / GLA_TAU)
            ga = a_full * g_n
            dp_ref[rows, 0:KW] = dq.astype(dp_ref.dtype)
            dp_ref[rows, KW:2 * KW] = (dkd * w).astype(dp_ref.dtype)
            dp_ref[rows, 2 * KW:2 * KW + VW] = dv.astype(dp_ref.dtype)
        ga_ref[...] = ga
        dz = jnp.concatenate(dzs, axis=0)
        dz_s = _split(dz)
        dp_ref[:, 2 * KW + 2 * VW:GLA_COLS] = _dot3s(dz_s, wg_s, _DIMS["nt"]).astype(dp_ref.dtype)
        dwg_ref[...] += _dot3s(lr_s, dz_s, _DIMS["tn"])
        dbg_ref[...] += jnp.sum(dz, axis=0, keepdims=True)

    full = lambda shape: pl.BlockSpec(shape, lambda i: tuple(0 for _ in shape))
    rev = lambda i: (ng - 1 - i, 0)
    vm = 4 * _nbytes((rb, GLA_COLS), F32) + 2 * _nbytes((rb, VW), F32) + 2 * _nbytes((GLA_G + 1, KW, VW), F32)
    vm += 16 * _nbytes((KW, VW), F32)
    return _pallas(
        body,
        out_shape=(_out((T, GLA_COLS), BF16), _out((LRW, KW), F32),
                   _out((1, KW), F32), _out((1, VW), F32)),
        grid=(ng,),
        in_specs=[pl.BlockSpec((rb, GLA_COLS), rev), pl.BlockSpec((rb, VW), rev),
                  pl.BlockSpec((GLA_G, KW, VW), lambda i: (ng - 1 - i, 0, 0)),
                  pl.BlockSpec((1, KW, VW), lambda i: (jnp.maximum((ng - 1 - i) * GLA_G - 1, 0), 0, 0)),
                  full((LRW, KW)), full((1, KW)), full((1, VW)), full((KW, VW)), full((VW, VW)), full((CHUNK, CHUNK))],
        out_specs=(pl.BlockSpec((rb, GLA_COLS), rev), full((LRW, KW)), full((1, KW)), full((1, VW))),
        scratch_shapes=[pltpu.VMEM((KW, VW), F32)],
        compiler_params=_cparams(("arbitrary",), vm),
        name=name,
    )(p, dy, states, states, wg, bg, gn, mask, hm, low)


CONV_TM = 512
HALO = 32
CONV_RB = 64


def _glu(u):
    a = u[:, 0:CW]
    b = u[:, CW:2 * CW]
    return a * _sigmoid(b)


def _conv_taps(buf_ref, w_ref, rb0, first_tap):
    acc = jnp.zeros((CONV_RB, CW), F32)
    for j in range(CK):
        s = rb0 + first_tap(j)
        acc = acc + w_ref[j:j + 1, :] * buf_ref[s:s + CONV_RB, :]
    return acc


def _ln_fwd(c, lg, lb):
    mu = jnp.mean(c, axis=-1, keepdims=True)
    xc = c - mu
    rstd = lax.rsqrt(jnp.mean(xc * xc, axis=-1, keepdims=True) + EPS)
    n = xc * rstd
    return n, rstd, n * lg + lb


def _conv_fwd(u, w, b, lg, lb, name):
    T = u.shape[0]
    tm = CONV_TM

    def body(u_ref, uh_ref, w_ref, b_ref, lg_ref, lb_ref, o_ref, c_ref, hbuf):
        i = pl.program_id(0)
        hbuf[0:HALO, :] = jnp.where(i > 0, _glu(uh_ref[...]), 0.0)
        hbuf[HALO:HALO + tm, :] = _glu(u_ref[...])
        for r in range(tm // CONV_RB):
            rows = slice(r * CONV_RB, (r + 1) * CONV_RB)
            acc = _conv_taps(hbuf, w_ref, r * CONV_RB, lambda j: HALO - (CK - 1) + j)
            c_ref[rows, :] = acc
            _, _, zz = _ln_fwd(acc + b_ref[...], lg_ref[...], lb_ref[...])
            o_ref[rows, :] = (zz * _sigmoid(zz)).astype(o_ref.dtype)

    vec = pl.BlockSpec((1, CW), lambda i: (0, 0))
    return _pallas(
        body,
        out_shape=(_out((T, CW), BF16), _out((T, CW), F32)),
        grid=(T // tm,),
        in_specs=[pl.BlockSpec((tm, CONV_COLS), lambda i: (i, 0)),
                  pl.BlockSpec((HALO, CONV_COLS), lambda i: (jnp.maximum(i * (tm // HALO) - 1, 0), 0)),
                  pl.BlockSpec((HALO, CW), lambda i: (0, 0)), vec, vec, vec],
        out_specs=(pl.BlockSpec((tm, CW), lambda i: (i, 0)), pl.BlockSpec((tm, CW), lambda i: (i, 0))),
        scratch_shapes=[pltpu.VMEM((tm + HALO, CW), F32)],
        compiler_params=_cparams(("arbitrary",), 8 * _nbytes((tm, CONV_COLS), F32)),
        name=name,
    )(u, u, w, b, lg, lb)


def _conv_bwd_dc(u, conv, dout, b, lg, lb, name):
    T = u.shape[0]
    tm = CONV_TM
    nsteps = T // tm

    def body(u_ref, uh_ref, c_ref, do_ref, b_ref, lg_ref, lb_ref, dc_ref, dw_ref, db_ref, dlg_ref, dlb_ref, hbuf, dwacc):
        i = pl.program_id(0)

        @pl.when(i == 0)
        def _():
            dwacc[...] = jnp.zeros_like(dwacc)
            db_ref[...] = jnp.zeros_like(db_ref)
            dlg_ref[...] = jnp.zeros_like(dlg_ref)
            dlb_ref[...] = jnp.zeros_like(dlb_ref)

        hbuf[0:HALO, :] = jnp.where(i > 0, _glu(uh_ref[...]), 0.0)
        hbuf[HALO:HALO + tm, :] = _glu(u_ref[...])
        for r in range(tm // CONV_RB):
            rows = slice(r * CONV_RB, (r + 1) * CONV_RB)
            n, rstd, zz = _ln_fwd(c_ref[rows, :] + b_ref[...], lg_ref[...], lb_ref[...])
            sg = _sigmoid(zz)
            dz = do_ref[rows, :].astype(F32) * (sg * (1.0 + zz * (1.0 - sg)))
            dlg_ref[...] += jnp.sum(dz * n, axis=0, keepdims=True)
            dlb_ref[...] += jnp.sum(dz, axis=0, keepdims=True)
            dn = dz * lg_ref[...]
            dc = rstd * (dn - jnp.mean(dn, axis=-1, keepdims=True) - n * jnp.mean(dn * n, axis=-1, keepdims=True))
            dc_ref[rows, :] = dc
            db_ref[...] += jnp.sum(dc, axis=0, keepdims=True)
            for j in range(CK):
                s = r * CONV_RB + HALO - (CK - 1) + j
                prod = dc * hbuf[s:s + CONV_RB, :]
                dwacc[j] += jnp.sum(prod.reshape(CONV_RB // 8, 8, CW), axis=0)

        @pl.when(i == nsteps - 1)
        def _():
            dw_ref[...] = jnp.sum(dwacc[...], axis=1)

    vec = pl.BlockSpec((1, CW), lambda i: (0, 0))
    return _pallas(
        body,
        out_shape=(_out((T, CW), F32), _out((HALO, CW), F32),
                   _out((1, CW), F32), _out((1, CW), F32), _out((1, CW), F32)),
        grid=(nsteps,),
        in_specs=[pl.BlockSpec((tm, CONV_COLS), lambda i: (i, 0)),
                  pl.BlockSpec((HALO, CONV_COLS), lambda i: (jnp.maximum(i * (tm // HALO) - 1, 0), 0)),
                  pl.BlockSpec((tm, CW), lambda i: (i, 0)), pl.BlockSpec((tm, CW), lambda i: (i, 0)), vec, vec, vec],
        out_specs=(pl.BlockSpec((tm, CW), lambda i: (i, 0)), pl.BlockSpec((HALO, CW), lambda i: (0, 0)), vec, vec, vec),
        scratch_shapes=[pltpu.VMEM((tm + HALO, CW), F32), pltpu.VMEM((HALO, 8, CW), F32)],
        compiler_params=_cparams(("arbitrary",), 10 * _nbytes((tm, CONV_COLS), F32)),
        name=name,
    )(u, u, conv, dout, b, lg, lb)


def _conv_bwd_du(u, dc, w, name):
    T = u.shape[0]
    tm = CONV_TM
    nsteps = T // tm

    def body(u_ref, dc_ref, dch_ref, w_ref, du_ref, dcbuf):
        i = pl.program_id(0)
        dcbuf[0:tm, :] = dc_ref[...]
        dcbuf[tm:tm + HALO, :] = jnp.where(i < nsteps - 1, dch_ref[...], 0.0)
        for r in range(tm // CONV_RB):
            rows = slice(r * CONV_RB, (r + 1) * CONV_RB)
            dh = _conv_taps(dcbuf, w_ref, r * CONV_RB, lambda j: (CK - 1) - j)
            a = u_ref[rows, 0:CW]
            sb = _sigmoid(u_ref[rows, CW:2 * CW])
            du_ref[rows, 0:CW] = (dh * sb).astype(du_ref.dtype)
            du_ref[rows, CW:2 * CW] = (dh * a * sb * (1.0 - sb)).astype(du_ref.dtype)

    return _pallas(
        body,
        out_shape=_out((T, CONV_COLS), BF16),
        grid=(nsteps,),
        in_specs=[pl.BlockSpec((tm, CONV_COLS), lambda i: (i, 0)),
                  pl.BlockSpec((tm, CW), lambda i: (i, 0)),
                  pl.BlockSpec((HALO, CW), lambda i: (jnp.minimum((i + 1) * (tm // HALO), T // HALO - 1), 0)),
                  pl.BlockSpec((HALO, CW), lambda i: (0, 0))],
        out_specs=pl.BlockSpec((tm, CONV_COLS), lambda i: (i, 0)),
        scratch_shapes=[pltpu.VMEM((tm + HALO, CW), F32)],
        compiler_params=_cparams(("arbitrary",), 8 * _nbytes((tm, CONV_COLS), F32)),
        name=name,
    )(u, dc, dc, w)


ATT_G = 4


def _att_load_kv(p_any, kbuf, vbuf, sems, T):
    kc = pltpu.make_async_copy(p_any.at[:, pl.ds(AW, AW)], kbuf.at[pl.ds(LEFT, T), :], sems.at[0])
    vc = pltpu.make_async_copy(p_any.at[:, pl.ds(2 * AW, AW)], vbuf.at[pl.ds(LEFT, T), :], sems.at[1])
    kc.start()
    vc.start()
    kbuf[0:LEFT, :] = jnp.zeros((LEFT, AW), BF16)
    vbuf[0:LEFT, :] = jnp.zeros((LEFT, AW), BF16)
    kc.wait()
    vc.wait()


ATT_QB = CHUNK * ATT_G
ATT_KB = LEFT + ATT_QB
REL_PAD = 384
TOEP = 1024


def _att_consts():
    m = np.arange(TOEP)
    d = ATT_KB - 1 - m
    idx = np.clip(d, -128, 128) + 128
    sel = (np.arange(REL_PAD)[:, None] == idx[None, :]) & (m[None, :] < ATT_QB + ATT_KB - 1)
    return jnp.asarray(sel.astype(np.float32))


def _att_build_bias(rel_ref, sel_ref, bias_scr):
    tr = jnp.dot(rel_ref[...], sel_ref[...], precision=HI, preferred_element_type=F32)
    qc = lax.broadcasted_iota(jnp.int32, (ATT_QB, ATT_KB), 0) // CHUNK
    kc = lax.broadcasted_iota(jnp.int32, (ATT_QB, ATT_KB), 1) // CHUNK
    band = (kc >= qc) & (kc <= qc + 8)
    for h in range(AH):
        rows = jnp.broadcast_to(tr[h:h + 1, :], (ATT_QB, TOEP))
        toep = pltpu.roll(rows, TOEP - (ATT_QB - 1), 1, stride=1, stride_axis=0)[:, 0:ATT_KB]
        bias_scr[h // 2, (h % 2) * ATT_QB:(h % 2 + 1) * ATT_QB, :] = jnp.where(band, toep, NEG)


def _att_probs(qst, kb, bias_p, n0):
    sc = lax.dot_general(qst, kb, _DIMS["nt"], preferred_element_type=F32) * (64 ** -0.5) + bias_p
    pos = lax.broadcasted_iota(jnp.int32, (2 * ATT_QB, ATT_KB), 1)
    sc = jnp.where(pos >= CHUNK * (8 - n0), sc, NEG)
    mx = jnp.max(sc, axis=-1, keepdims=True)
    ex = jnp.exp(sc - mx)
    return ex / jnp.sum(ex, axis=-1, keepdims=True)


def _head_stack(a2, lo):
    zero = jnp.zeros_like(a2)
    return jnp.concatenate([jnp.where(lo, a2, zero), jnp.where(lo, zero, a2)], axis=0)


def _att_fwd(p, rel, sel, name):
    T = p.shape[0]

    def body(q_ref, p_any, rel_ref, sel_ref, o_ref, kbuf, vbuf, bias_scr, sems):
        i = pl.program_id(0)

        @pl.when(i == 0)
        def _():
            _att_load_kv(p_any, kbuf, vbuf, sems, T)
            _att_build_bias(rel_ref, sel_ref, bias_scr)

        lo = lax.broadcasted_iota(jnp.int32, (ATT_QB, 128), 1) < 64
        n0 = i * ATT_G
        start = pl.multiple_of(i * ATT_QB, ATT_QB)
        for hp in range(AH // 2):
            cols = slice(hp * 128, (hp + 1) * 128)
            kb = kbuf[pl.ds(start, ATT_KB), cols]
            vb = vbuf[pl.ds(start, ATT_KB), cols]
            pr = _att_probs(_head_stack(q_ref[:, cols], lo), kb, bias_scr[hp], n0)
            pv = jnp.dot(pr.astype(BF16), vb, preferred_element_type=F32)
            o_ref[:, cols] = jnp.where(lo, pv[0:ATT_QB], pv[ATT_QB:2 * ATT_QB]).astype(o_ref.dtype)

    vm = 2 * _nbytes((T + LEFT, AW), BF16) + 8 * _nbytes((2 * ATT_QB, ATT_KB), F32) + (8 << 20)
    return _pallas(
        body,
        out_shape=_out((T, AW), BF16),
        grid=(T // ATT_QB,),
        in_specs=[pl.BlockSpec((ATT_QB, AW), lambda i: (i, 0)), pl.BlockSpec(memory_space=pl.ANY),
                  pl.BlockSpec((8, REL_PAD), lambda i: (0, 0)), pl.BlockSpec((REL_PAD, TOEP), lambda i: (0, 0))],
        out_specs=pl.BlockSpec((ATT_QB, AW), lambda i: (i, 0)),
        scratch_shapes=[pltpu.VMEM((T + LEFT, AW), BF16), pltpu.VMEM((T + LEFT, AW), BF16),
                        pltpu.VMEM((AH // 2, 2 * ATT_QB, ATT_KB), F32), pltpu.SemaphoreType.DMA((2,))],
        compiler_params=_cparams(("arbitrary",), vm),
        name=name,
    )(p, p, rel, sel)


def _att_bwd(p, do, rel, sel, name):
    T = p.shape[0]
    nsteps = T // ATT_QB

    def body(q_ref, p_any, do_ref, rel_ref, sel_ref, dp_any, drel_ref,
             kbuf, vbuf, dqbuf, dkbuf, dvbuf, bias_scr, dbias_scr, dtr_scr, sems):
        i = pl.program_id(0)

        @pl.when(i == 0)
        def _():
            _att_load_kv(p_any, kbuf, vbuf, sems, T)
            _att_build_bias(rel_ref, sel_ref, bias_scr)
            dkbuf[...] = jnp.zeros_like(dkbuf)
            dvbuf[...] = jnp.zeros_like(dvbuf)
            dbias_scr[...] = jnp.zeros_like(dbias_scr)

        lo = lax.broadcasted_iota(jnp.int32, (ATT_QB, 128), 1) < 64
        n0 = i * ATT_G
        start = pl.multiple_of(i * ATT_QB, ATT_QB)
        for hp in range(AH // 2):
            cols = slice(hp * 128, (hp + 1) * 128)
            kb = kbuf[pl.ds(start, ATT_KB), cols]
            vb = vbuf[pl.ds(start, ATT_KB), cols]
            qst = _head_stack(q_ref[:, cols], lo)
            dost = _head_stack(do_ref[:, cols].astype(BF16), lo)
            pr = _att_probs(qst, kb, bias_scr[hp], n0)
            dpr = lax.dot_general(dost, vb, _DIMS["nt"], preferred_element_type=F32)
            ds = pr * (dpr - jnp.sum(dpr * pr, axis=-1, keepdims=True))
            dbias_scr[hp] += ds
            dsb = (ds * (64 ** -0.5)).astype(BF16)
            dq = jnp.dot(dsb, kb, preferred_element_type=F32)
            dqbuf[pl.ds(start, ATT_QB), cols] = jnp.where(lo, dq[0:ATT_QB], dq[ATT_QB:2 * ATT_QB]).astype(BF16)
            dkbuf[pl.ds(start, ATT_KB), cols] += lax.dot_general(dsb, qst, _DIMS["tn"], preferred_element_type=F32)
            dvbuf[pl.ds(start, ATT_KB), cols] += lax.dot_general(pr.astype(BF16), dost, _DIMS["tn"], preferred_element_type=F32)

        @pl.when(i == nsteps - 1)
        def _():
            kbuf[pl.ds(LEFT, T), :] = dkbuf[pl.ds(LEFT, T), :].astype(BF16)
            vbuf[pl.ds(LEFT, T), :] = dvbuf[pl.ds(LEFT, T), :].astype(BF16)
            cps = [pltpu.make_async_copy(dqbuf, dp_any.at[:, pl.ds(0, AW)], sems.at[0]),
                   pltpu.make_async_copy(kbuf.at[pl.ds(LEFT, T), :], dp_any.at[:, pl.ds(AW, AW)], sems.at[1]),
                   pltpu.make_async_copy(vbuf.at[pl.ds(LEFT, T), :], dp_any.at[:, pl.ds(2 * AW, AW)], sems.at[2])]
            for cp in cps:
                cp.start()
            dtr_scr[...] = jnp.zeros_like(dtr_scr)
            ri = lax.broadcasted_iota(jnp.int32, (ATT_QB, ATT_QB), 0)
            ci = lax.broadcasted_iota(jnp.int32, (ATT_QB, ATT_QB), 1)
            flip = jnp.where(ri + ci == ATT_QB - 1, 1.0, 0.0)
            for h in range(AH):
                db = dbias_scr[h // 2, (h % 2) * ATT_QB:(h % 2 + 1) * ATT_QB, :]
                db = jnp.dot(flip, db, precision=HI, preferred_element_type=F32)
                wide = jnp.concatenate([db, jnp.zeros((ATT_QB, TOEP - ATT_KB), F32)], axis=1)
                diag = pltpu.roll(wide, 0, 1, stride=1, stride_axis=0)
                dtr_scr[h:h + 1, :] = jnp.sum(diag, axis=0, keepdims=True)
            drel_ref[...] = lax.dot_general(dtr_scr[...], sel_ref[...], _DIMS["nt"], precision=HI, preferred_element_type=F32)
            for cp in cps:
                cp.wait()

    vm = 3 * _nbytes((T + LEFT, AW), BF16) + 2 * _nbytes((T + LEFT, AW), F32) + 12 * _nbytes((2 * ATT_QB, ATT_KB), F32) + (8 << 20)
    return _pallas(
        body,
        out_shape=(_out((T, ATT_COLS), BF16), _out((8, REL_PAD), F32)),
        grid=(nsteps,),
        in_specs=[pl.BlockSpec((ATT_QB, AW), lambda i: (i, 0)), pl.BlockSpec(memory_space=pl.ANY),
                  pl.BlockSpec((ATT_QB, AW), lambda i: (i, 0)),
                  pl.BlockSpec((8, REL_PAD), lambda i: (0, 0)), pl.BlockSpec((REL_PAD, TOEP), lambda i: (0, 0))],
        out_specs=(pl.BlockSpec(memory_space=pl.ANY), pl.BlockSpec((8, REL_PAD), lambda i: (0, 0))),
        scratch_shapes=[pltpu.VMEM((T + LEFT, AW), BF16), pltpu.VMEM((T + LEFT, AW), BF16), pltpu.VMEM((T, AW), BF16),
                        pltpu.VMEM((T + LEFT, AW), F32), pltpu.VMEM((T + LEFT, AW), F32),
                        pltpu.VMEM((AH // 2, 2 * ATT_QB, ATT_KB), F32), pltpu.VMEM((AH // 2, 2 * ATT_QB, ATT_KB), F32),
                        pltpu.VMEM((8, TOEP), F32), pltpu.SemaphoreType.DMA((3,))],
        compiler_params=_cparams(("arbitrary",), vm),
        name=name,
    )(p, p, do, rel, sel)


def _layer_fwd(h, wl, consts, tag):
    xn = _rmsnorm_fwd(h, wl["norm_mix"], f"{tag}_norm_mix")
    p_gla, p_conv, p_att = _mm_fan(xn, [wl["w_gla"], wl["w_conv"], wl["w_att"]], mode="nn", out_dtypes=(F32, F32, BF16),
                                   name=f"{tag}_proj")
    o_gla, states = _gla_fwd(p_gla, wl["wg"], wl["bg"], wl["gla_norm"], consts, f"{tag}_gla_fwd")
    o_conv, conv = _conv_fwd(p_conv, wl["w_dw"], wl["b_dw"], wl["ln_g"], wl["ln_b"], f"{tag}_conv_fwd")
    rel = jnp.pad(wl["rel_bias"], ((0, 8 - AH), (0, REL_PAD - N_REL)))
    o_att = _att_fwd(p_att, rel, consts[3], f"{tag}_att_fwd")
    h1 = _mm_sum([o_gla, o_conv, o_att], [wl["w_out_g"], wl["w_out_c"], wl["w_out_a"]], mode="nn", out_dtype=F32, extra=h,
                 name=f"{tag}_out")
    xn2 = _rmsnorm_fwd(h1, wl["norm_ffn"], f"{tag}_norm_ffn")
    u = _mm(xn2, wl["w_up"], mode="nn", out_dtype=BF16, tm=1024, tn=1024, b_chips=True, name=f"{tag}_mlp_up")
    h2 = _mm(u, wl["w_down"], mode="nn", out_dtype=F32, tm=1024, tk=1024, a_pro="relu2", epi="add", extra=h1,
             name=f"{tag}_mlp_down")
    saved = dict(h=h, xn=xn, p_gla=p_gla, p_conv=p_conv, p_att=p_att, states=states, o_gla=o_gla, o_conv=o_conv, conv=conv,
                 o_att=o_att, rel=rel, h1=h1, xn2=xn2, u=u)
    return h2, saved


def _layer_bwd(dh2, sv, wl, consts, tag, emit=lambda name, grad: None):
    g = {}
    du = _mm(dh2, wl["w_down"], mode="nt", out_dtype=BF16, tm=1024, tn=1024, epi="relu2grad", extra=sv["u"],
             name=f"{tag}_mlp_down_dx")
    g["w_down"] = _mm(sv["u"], dh2, mode="tn", out_dtype=F32, tm=2048, tn=1024, tk=512, a_pro="relu2", name=f"{tag}_mlp_down_dw")
    emit("w_down", g["w_down"].reshape(4, D_FF // 4, D))
    dxn2 = _mm(du, wl["w_up"], mode="nt", out_dtype=F32, tm=1024, tk=1024, b_chips=True, name=f"{tag}_mlp_up_dx")
    g["w_up"] = _mm(sv["xn2"], du, mode="tn", out_dtype=F32, tm=1024, tn=1024, tk=512, out_chips=True, name=f"{tag}_mlp_up_dw")
    emit("w_up", g["w_up"])
    dh1, g["norm_ffn"] = _rmsnorm_bwd(dxn2, sv["h1"], wl["norm_ffn"], dh2, f"{tag}_norm_ffn_bwd")
    d_gla, d_conv, d_att = _mm_fan(dh1, [wl["w_out_g"], wl["w_out_c"], wl["w_out_a"]], mode="nt", out_dtypes=(F32, F32, F32),
                                   name=f"{tag}_out_dx")
    g["w_out_g"], g["w_out_c"], g["w_out_a"] = _mm_tn_multi([sv["o_gla"], sv["o_conv"], sv["o_att"], dh1],
                                                            [(0, 3), (1, 3), (2, 3)], name=f"{tag}_out_dw")
    emit("w_out", jnp.concatenate([g["w_out_g"], g["w_out_c"], g["w_out_a"]], axis=0).reshape(4, D // 4, D))
    dp_gla, g["wg"], g["bg"], g["gla_norm"] = _gla_bwd(sv["p_gla"], d_gla, sv["states"], wl["wg"], wl["bg"], wl["gla_norm"],
                                                       consts, f"{tag}_gla_bwd")
    dc, g["w_dw"], g["b_dw"], g["ln_g"], g["ln_b"] = _conv_bwd_dc(sv["p_conv"], sv["conv"], d_conv, wl["b_dw"], wl["ln_g"],
                                                                  wl["ln_b"], f"{tag}_conv_bwd_dc")
    dp_conv = _conv_bwd_du(sv["p_conv"], dc, wl["w_dw"], f"{tag}_conv_bwd_du")
    dp_att, drel = _att_bwd(sv["p_att"], d_att, sv["rel"], consts[3], f"{tag}_att_bwd")
    g["rel_bias"] = drel[0:AH, 0:N_REL]
    g["w_gla"], g["w_conv"], g["w_att"] = _mm_tn_multi([sv["xn"], dp_gla, dp_conv, dp_att], [(0, 1), (0, 2), (0, 3)],
                                                       name=f"{tag}_proj_dw")
    emit("w_in", jnp.transpose(_join_w_in(g).reshape(D, 4, -1), (1, 0, 2)))
    dxn = _mm_sum([dp_gla, dp_conv, dp_att], [wl["w_gla"], wl["w_conv"], wl["w_att"]], mode="nt", out_dtype=F32,
                  name=f"{tag}_proj_dx")
    dh, g["norm_mix"] = _rmsnorm_bwd(dxn, sv["h"], wl["norm_mix"], dh1, f"{tag}_norm_mix_bwd")
    return dh, g


def _local_step(x, target, layers, norm_final, emit=lambda layer, name, grad: None):
    consts = _gla_consts() + (_att_consts(),)
    h = x
    saved = []
    for l, wl in enumerate(layers):
        h, sv = _layer_fwd(h, wl, consts, f"l{l}")
        saved.append(sv)
    loss, dh, g_final = _final_loss(h, norm_final, target, "final_loss")
    grads = [None] * len(layers)
    for l in reversed(range(len(layers))):
        dh, grads[l] = _layer_bwd(dh, saved[l], layers[l], consts, f"l{l}", functools.partial(emit, l))
    return loss, dh, grads, g_final


ANY = pl.BlockSpec(memory_space=pl.ANY)


def _place():
    x, y, c = lax.axis_index("x"), lax.axis_index("y"), lax.axis_index("c")
    chips = [(1 - x, y), (x, 1 - y), (1 - x, 1 - y)]
    return x, y, c, chips


def _shape(shape, dtype):
    return jax.ShapeDtypeStruct(tuple(shape), dtype)


def _remote(src, dst, send_sem, recv_sem, to):
    return pltpu.make_async_remote_copy(src_ref=src, dst_ref=dst, send_sem=send_sem, recv_sem=recv_sem,
                                        device_id=to, device_id_type=MESH)


class _Staged:
    def __init__(self, src, dst, buf, sems):
        self.load = pltpu.make_async_copy(src, buf, sems.at[0])
        self.store = pltpu.make_async_copy(buf, dst, sems.at[1])

    def start(self):
        self.load.start()

    def wait(self):
        self.load.wait()
        self.store.start()
        self.store.wait()


def _phase_gather_ici(src, then):
    R, C = src.shape
    rh = R // 2

    def copies(ins, outs, sems):
        x, y, c, chips = _place()
        me = 2 * x + y
        local = _Staged(ins[0], outs[0].at[me], sems[3], sems[2])
        sends = [_remote(ins[0].at[pl.ds(c * rh, rh), :], outs[0].at[me, pl.ds(c * rh, rh), :], sems[0].at[k], sems[1].at[k],
                         (px, py, c)) for k, (px, py) in enumerate(chips)]
        recvs = [_remote(outs[0].at[2 * px + py, pl.ds(c * rh, rh), :], outs[0].at[2 * px + py, pl.ds(c * rh, rh), :],
                         sems[0].at[k], sems[1].at[k], (px, py, c)) for k, (px, py) in enumerate(chips)]
        return local, sends, recvs

    def start(ins, outs, sems):
        local, sends, _ = copies(ins, outs, sems)
        local.start()
        for cp in sends:
            cp.start()

    def finish(ins, outs, sems):
        local, sends, recvs = copies(ins, outs, sems)
        for cp in recvs:
            cp.wait_recv()
        for cp in sends:
            cp.wait_send()
        local.wait()

    return _Comm([src], [_shape((4, R, C), src.dtype)], {}, [(3,), (3,), (2,), ((R, C), src.dtype)], start, finish, then)


def _phase_gather_d2d(part, then):
    _, R, C = part.shape
    rh = R // 2

    def copies(ins, outs, sems):
        x, y, c, chips = _place()
        sends = [_remote(ins[0].at[2 * px + py, pl.ds(c * rh, rh), :], outs[0].at[2 * px + py, pl.ds(c * rh, rh), :],
                         sems[0].at[k], sems[1].at[k], (x, y, 1 - c)) for k, (px, py) in enumerate(chips)]
        recvs = [_remote(outs[0].at[2 * px + py, pl.ds((1 - c) * rh, rh), :], outs[0].at[2 * px + py, pl.ds((1 - c) * rh, rh), :],
                         sems[0].at[k], sems[1].at[k], (x, y, 1 - c)) for k, (px, py) in enumerate(chips)]
        return sends, recvs

    def start(ins, outs, sems):
        for cp in copies(ins, outs, sems)[0]:
            cp.start()

    def finish(ins, outs, sems):
        sends, recvs = copies(ins, outs, sems)
        for cp in recvs:
            cp.wait_recv()
        for cp in sends:
            cp.wait_send()

    return _Comm([part], [_shape(part.shape, part.dtype)], {0: 0}, [(3,), (3,)], start, finish, then)


def _phase_pair_exchange(full, then):
    _, R, C = full.shape
    rh = R // 2

    def copy(ins, outs, sems):
        x, y, c, _ = _place()
        return _remote(ins[0].at[:, pl.ds((1 - c) * rh, rh), :], outs[0], sems[0].at[0], sems[1].at[0], (x, y, 1 - c))

    return _Comm([full], [_shape((4, rh, C), full.dtype)], {}, [(1,), (1,)],
                 lambda ins, outs, sems: copy(ins, outs, sems).start(),
                 lambda ins, outs, sems: copy(ins, outs, sems).wait(), then)


def _phase_chip_scatter(parts, then):
    def copies(ins, outs, sems):
        x, y, c, chips = _place()
        me = 2 * x + y
        local = _Staged(ins[0].at[me], outs[0].at[me], sems[3], sems[2])
        sends = [_remote(ins[0].at[2 * px + py], outs[0].at[me], sems[0].at[k], sems[1].at[k], (px, py, c))
                 for k, (px, py) in enumerate(chips)]
        recvs = [_remote(outs[0].at[2 * px + py], outs[0].at[2 * px + py], sems[0].at[k], sems[1].at[k], (px, py, c))
                 for k, (px, py) in enumerate(chips)]
        return local, sends, recvs

    def start(ins, outs, sems):
        local, sends, _ = copies(ins, outs, sems)
        local.start()
        for cp in sends:
            cp.start()

    def finish(ins, outs, sems):
        local, sends, recvs = copies(ins, outs, sems)
        for cp in recvs:
            cp.wait_recv()
        for cp in sends:
            cp.wait_send()
        local.wait()

    return _Comm([parts], [_shape(parts.shape, parts.dtype)], {}, [(3,), (3,), (2,), (parts.shape[1:], parts.dtype)],
                 start, finish, then)


def _phase_pair_allgather(half, layer, depth, into, then):
    rh, C = half.shape

    def copies(ins, outs, sems):
        x, y, c, _ = _place()
        mine = outs[0].at[layer, pl.ds(c * rh, rh), :]
        theirs = outs[0].at[layer, pl.ds((1 - c) * rh, rh), :]
        return (_Staged(ins[0], mine, sems[3], sems[2]),
                _remote(ins[0], mine, sems[0].at[0], sems[1].at[0], (x, y, 1 - c)),
                _remote(theirs, theirs, sems[0].at[0], sems[1].at[0], (x, y, 1 - c)))

    def start(ins, outs, sems):
        local, send, _ = copies(ins, outs, sems)
        local.start()
        send.start()

    def finish(ins, outs, sems):
        local, send, recv = copies(ins, outs, sems)
        recv.wait_recv()
        send.wait_send()
        local.wait()

    ins = [half] if into is None else [half, into]
    return _Comm(ins, [_shape((depth, 2 * rh, C), half.dtype)], {} if into is None else {1: 0},
                 [(1,), (1,), (2,), ((rh, C), half.dtype)], start, finish, then)


def _comm_only(comms, name):
    plan = _Plan()
    for c in comms:
        plan.at(name, c)
    saved, _PLAN[0] = _PLAN[0], plan
    try:
        def body(o_ref):
            o_ref[...] = jnp.zeros_like(o_ref)

        _pallas(body, out_shape=[jax.ShapeDtypeStruct((8, 128), F32)], in_specs=[],
                out_specs=[pl.BlockSpec(memory_space=pltpu.VMEM)], name=name)()
    finally:
        _PLAN[0] = saved


def _gather_chips(items):
    n = len(items)

    def body(*refs):
        srcs, outs = refs[:n], refs[n:2 * n]
        send_sems, recv_sems, loc_sems = refs[2 * n:]
        x, y, c, chips = _place()
        me = 2 * x + y

        def half(t, chip, cc):
            rh = items[t].shape[1] // 2
            return outs[t].at[chip, :, pl.ds(cc * rh, rh), :]

        def copy(t, k, src, dst, to):
            return pltpu.make_async_remote_copy(src_ref=src, dst_ref=dst, send_sem=send_sems.at[t, k],
                                                recv_sem=recv_sems.at[t, k], device_id=to, device_id_type=MESH)

        local = [pltpu.make_async_copy(srcs[t], outs[t].at[me], loc_sems.at[t]) for t in range(n)]
        for cp in local:
            cp.start()
        sent = []
        for t in range(n):
            rh = items[t].shape[1] // 2
            mine = srcs[t].at[:, pl.ds(c * rh, rh), :]
            for k, (px, py) in enumerate(chips):
                sent.append(copy(t, k, mine, half(t, me, c), (px, py, c)))
                sent[-1].start()
        for k, (px, py) in enumerate(chips):
            for t in range(n):
                blk = half(t, 2 * px + py, c)
                copy(t, k, blk, blk, (px, py, c)).wait_recv()
                sent.append(copy(t, 3 + k, blk, blk, (x, y, 1 - c)))
                sent[-1].start()
        for k, (px, py) in enumerate(chips):
            for t in range(n):
                blk = half(t, 2 * px + py, 1 - c)
                copy(t, 3 + k, blk, blk, (x, y, 1 - c)).wait_recv()
        for cp in sent:
            cp.wait_send()
        for cp in local:
            cp.wait()

    return _pallas(
        body,
        out_shape=[_out((4,) + a.shape, a.dtype) for a in items],
        in_specs=[ANY] * n,
        out_specs=[ANY] * n,
        scratch_shapes=[pltpu.SemaphoreType.DMA((n, 6)), pltpu.SemaphoreType.DMA((n, 6)), pltpu.SemaphoreType.DMA((n,))],
        name="gather_weights",
    )(*items)


def _pair_exchange(items):
    n = len(items)

    def body(*refs):
        srcs, outs = refs[:n], refs[n:2 * n]
        send_sems, recv_sems = refs[2 * n:]
        x, y, c, _ = _place()
        cps = []
        for t in range(n):
            rh = items[t].shape[2] // 2
            cps.append(pltpu.make_async_remote_copy(
                src_ref=srcs[t].at[:, :, pl.ds((1 - c) * rh, rh), :], dst_ref=outs[t], send_sem=send_sems.at[t],
                recv_sem=recv_sems.at[t], device_id=(x, y, 1 - c), device_id_type=MESH))
            cps[-1].start()
        for cp in cps:
            cp.wait()

    return _pallas(
        body,
        out_shape=[_out(a.shape[:2] + (a.shape[2] // 2, a.shape[3]), a.dtype) for a in items],
        in_specs=[ANY] * n,
        out_specs=[ANY] * n,
        scratch_shapes=[pltpu.SemaphoreType.DMA((n,)), pltpu.SemaphoreType.DMA((n,))],
        name="reduce_pair_exchange",
    )(*items)


def _row_tile(rows, cols, itemsize=4, budget=1 << 20):
    t = rows
    while t % 2 == 0 and t // 2 >= 8 and (t // 2) % 8 == 0 and t * cols * itemsize > budget:
        t //= 2
    return t


def _pair_add(full, got, c, name):
    _, L, R, C = full.shape
    rh = R // 2
    tr = _row_tile(rh, C)
    nb = rh // tr

    def body(c_ref, a_ref, b_ref, o_ref):
        o_ref[...] = (a_ref[...] + b_ref[...]).astype(o_ref.dtype)

    grid_spec = pltpu.PrefetchScalarGridSpec(
        num_scalar_prefetch=1,
        grid=(4, L, nb),
        in_specs=[pl.BlockSpec((1, 1, tr, C), lambda j, l, i, c_ref: (j, l, c_ref[0] * nb + i, 0)),
                  pl.BlockSpec((1, 1, tr, C), lambda j, l, i, c_ref: (j, l, i, 0))],
        out_specs=pl.BlockSpec((1, 1, tr, C), lambda j, l, i, c_ref: (j, l, i, 0)),
    )
    return _pallas(
        body,
        out_shape=_out((4, L, rh, C), BF16),
        grid_spec=grid_spec,
        compiler_params=_cparams(("parallel", "parallel", "parallel"), 8 * tr * C * 4),
        name=name,
    )(jnp.reshape(c, (1,)).astype(jnp.int32), full, got)


def _chip_scatter(items):
    n = len(items)

    def body(*refs):
        srcs, outs = refs[:n], refs[n:2 * n]
        send_sems, recv_sems, loc_sems = refs[2 * n:]
        x, y, c, chips = _place()
        me = 2 * x + y
        local = [pltpu.make_async_copy(srcs[t].at[me], outs[t].at[me], loc_sems.at[t]) for t in range(n)]
        for cp in local:
            cp.start()
        cps = []
        for t in range(n):
            for k, (px, py) in enumerate(chips):
                cps.append(pltpu.make_async_remote_copy(
                    src_ref=srcs[t].at[2 * px + py], dst_ref=outs[t].at[me], send_sem=send_sems.at[t, k],
                    recv_sem=recv_sems.at[t, k], device_id=(px, py, c), device_id_type=MESH))
                cps[-1].start()
        for t in range(n):
            for k, (px, py) in enumerate(chips):
                blk = outs[t].at[2 * px + py]
                pltpu.make_async_remote_copy(src_ref=blk, dst_ref=blk, send_sem=send_sems.at[t, k], recv_sem=recv_sems.at[t, k],
                                             device_id=(px, py, c), device_id_type=MESH).wait_recv()
        for cp in cps:
            cp.wait_send()
        for cp in local:
            cp.wait()

    return _pallas(
        body,
        out_shape=[_out(a.shape, a.dtype) for a in items],
        in_specs=[ANY] * n,
        out_specs=[ANY] * n,
        scratch_shapes=[pltpu.SemaphoreType.DMA((n, 3)), pltpu.SemaphoreType.DMA((n, 3)), pltpu.SemaphoreType.DMA((n,))],
        name="reduce_chip_scatter",
    )(*items)


def _sum_chips(parts, name):
    _, L, rh, C = parts.shape
    tr = _row_tile(rh, C)

    def body(p_ref, o_ref):
        acc = p_ref[0].astype(F32)
        for j in range(1, 4):
            acc = acc + p_ref[j].astype(F32)
        o_ref[...] = acc

    return _pallas(
        body,
        out_shape=_out((L, rh, C), F32),
        grid=(L, rh // tr),
        in_specs=[pl.BlockSpec((4, 1, tr, C), lambda l, i: (0, l, i, 0))],
        out_specs=pl.BlockSpec((1, tr, C), lambda l, i: (l, i, 0)),
        compiler_params=_cparams(("parallel", "parallel"), 16 * tr * C * 4),
        name=name,
    )(parts)


def _pair_allgather(groups):
    flat = [(w, l) for w, grp in enumerate(groups) for l in range(len(grp))]
    items = [groups[w][l] for w, l in flat]
    n, nw = len(items), len(groups)

    def body(*refs):
        srcs, outs = refs[:n], refs[n:n + nw]
        send_sems, recv_sems, loc_sems = refs[n + nw:]
        x, y, c, _ = _place()
        cps, local = [], []
        for t, (w, l) in enumerate(flat):
            rh = items[t].shape[1]
            mine = outs[w].at[pl.ds(l, 1), pl.ds(c * rh, rh), :]
            local.append(pltpu.make_async_copy(srcs[t], mine, loc_sems.at[t]))
            local[-1].start()
            cps.append(pltpu.make_async_remote_copy(src_ref=srcs[t], dst_ref=mine, send_sem=send_sems.at[t],
                                                    recv_sem=recv_sems.at[t], device_id=(x, y, 1 - c), device_id_type=MESH))
            cps[-1].start()
        for t, (w, l) in enumerate(flat):
            rh = items[t].shape[1]
            theirs = outs[w].at[pl.ds(l, 1), pl.ds((1 - c) * rh, rh), :]
            pltpu.make_async_remote_copy(src_ref=theirs, dst_ref=theirs, send_sem=send_sems.at[t], recv_sem=recv_sems.at[t],
                                         device_id=(x, y, 1 - c), device_id_type=MESH).wait_recv()
        for cp in cps:
            cp.wait_send()
        for cp in local:
            cp.wait()

    return _pallas(
        body,
        out_shape=[_out((len(grp), 2 * grp[0].shape[1], grp[0].shape[2]), grp[0].dtype) for grp in groups],
        in_specs=[ANY] * n,
        out_specs=[ANY] * nw,
        scratch_shapes=[pltpu.SemaphoreType.DMA((n,)), pltpu.SemaphoreType.DMA((n,)), pltpu.SemaphoreType.DMA((n,))],
        name="reduce_pair_allgather",
    )(*items)


def _allreduce_small(v):
    R = v.shape[0]

    def body(v_ref, o_ref, slots, send_sems, recv_sems):
        x, y, c, _ = _place()
        me = 4 * x + 2 * y + c
        slots[me] = v_ref[...]
        cps = []
        for r in range(1, 8):
            px, py, pc = x ^ (r >> 2), y ^ ((r >> 1) & 1), c ^ (r & 1)
            cps.append(pltpu.make_async_remote_copy(src_ref=v_ref, dst_ref=slots.at[me], send_sem=send_sems.at[r - 1],
                                                    recv_sem=recv_sems.at[r - 1], device_id=(px, py, pc), device_id_type=MESH))
            cps[-1].start()
        for r in range(1, 8):
            px, py, pc = x ^ (r >> 2), y ^ ((r >> 1) & 1), c ^ (r & 1)
            theirs = slots.at[4 * px + 2 * py + pc]
            pltpu.make_async_remote_copy(src_ref=theirs, dst_ref=theirs, send_sem=send_sems.at[r - 1], recv_sem=recv_sems.at[r - 1],
                                         device_id=(px, py, pc), device_id_type=MESH).wait_recv()
        acc = slots[0]
        for j in range(1, 8):
            acc = acc + slots[j]
        o_ref[...] = acc
        for cp in cps:
            cp.wait_send()

    return pl.pallas_call(
        body,
        out_shape=jax.ShapeDtypeStruct(v.shape, F32),
        in_specs=[pl.BlockSpec(memory_space=pltpu.VMEM)],
        out_specs=pl.BlockSpec(memory_space=pltpu.VMEM),
        scratch_shapes=[pltpu.VMEM((8, R, 128), F32), pltpu.SemaphoreType.DMA((7,)), pltpu.SemaphoreType.DMA((7,))],
        name="allreduce_small",
    )(v)


def _adamw_math(w, g, m, v):
    m = ADAM_B1 * m + (1.0 - ADAM_B1) * g
    v = ADAM_B2 * v + (1.0 - ADAM_B2) * (g * g)
    m_hat = m / (1.0 - ADAM_B1 ** ADAM_STEP)
    v_hat = v / (1.0 - ADAM_B2 ** ADAM_STEP)
    delta = -ADAM_LR * (m_hat / (jnp.sqrt(v_hat) + ADAM_EPS) + ADAM_WD * w)
    return delta, m, v


def _adamw(ws, gs, ms, vs, name):
    n = len(ws)
    tiles = [_row_tile(w.shape[1], w.shape[2], budget=1 << 19) for w in ws]
    per_layer = [w.shape[1] // t for w, t in zip(ws, tiles)]
    steps = [w.shape[0] * p for w, p in zip(ws, per_layer)]
    starts = [sum(steps[:k]) for k in range(n)]

    def body(*refs):
        i = pl.program_id(0)
        for k in range(n):
            w_ref, g_ref, m_ref, v_ref = (refs[j * n + k] for j in range(4))
            outs = [refs[(4 + j) * n + k] for j in range(3)]

            @pl.when((i >= starts[k]) & (i < starts[k] + steps[k]))
            def _(w_ref=w_ref, g_ref=g_ref, m_ref=m_ref, v_ref=v_ref, outs=outs):
                outs[0][...], outs[1][...], outs[2][...] = _adamw_math(w_ref[...], g_ref[...], m_ref[...], v_ref[...])

    def spec(k):
        def index(i):
            local = jnp.clip(i - starts[k], 0, steps[k] - 1)
            return local // per_layer[k], local % per_layer[k], 0
        return pl.BlockSpec((None, tiles[k], ws[k].shape[2]), index)

    specs = [spec(k) for k in range(n)]
    outs = [_out(w.shape, F32) for w in ws]
    res = _pallas(
        body,
        out_shape=tuple(outs * 3),
        grid=(sum(steps),),
        in_specs=specs * 4,
        out_specs=tuple(specs * 3),
        compiler_params=_cparams(("arbitrary",), sum(16 * t * w.shape[2] * 4 for w, t in zip(ws, tiles))),
        name=name,
    )(*ws, *gs, *ms, *vs)
    return res[:n], res[n:2 * n], res[2 * n:]


def _adamw_small(ws, gs, ms, vs):
    n = len(ws)

    def body(*refs):
        for t in range(n):
            w_ref, g_ref, m_ref, v_ref = (refs[k * n + t] for k in range(4))
            d_ref, nm_ref, nv_ref = (refs[(4 + k) * n + t] for k in range(3))
            d_ref[...], nm_ref[...], nv_ref[...] = _adamw_math(w_ref[...], g_ref[...], m_ref[...], v_ref[...])

    vmem = pl.BlockSpec(memory_space=pltpu.VMEM)
    outs = [jax.ShapeDtypeStruct(w.shape, F32) for w in ws]
    res = pl.pallas_call(
        body,
        out_shape=outs * 3,
        in_specs=[vmem] * (4 * n),
        out_specs=[vmem] * (3 * n),
        name="adamw_small",
    )(*ws, *gs, *ms, *vs)
    return res[:n], res[n:2 * n], res[2 * n:]


IN_SIZES = (192, 192, 384, 384, 16, 512, 384, 384, 384)
IN_OFFS = tuple(int(v) for v in np.cumsum((0,) + IN_SIZES))
SMALL = ("norm_mix", "w_gla_gate", "b_gla_gate", "gla_norm", "b_dw", "conv_ln_g", "conv_ln_b", "rel_bias", "norm_ffn")


def _pad_cols(a, n):
    return jnp.pad(a, ((0, 0), (0, n - a.shape[1])))


def _split_w_in(w):
    s = [w[:, IN_OFFS[i]:IN_OFFS[i + 1]] for i in range(9)]
    w_gla = jnp.concatenate([_pad_cols(s[0], KW), _pad_cols(s[1], KW), s[2], s[3], _pad_cols(s[4], LRW)], axis=1)
    return w_gla, s[5], jnp.concatenate(s[6:9], axis=1)


def _join_w_in(g):
    gg = g["w_gla"]
    return jnp.concatenate([gg[:, 0:192], gg[:, KW:KW + 192], gg[:, 2 * KW:2 * KW + VW], gg[:, 2 * KW + VW:2 * KW + 2 * VW],
                            gg[:, 2 * KW + 2 * VW:2 * KW + 2 * VW + 16], g["w_conv"], g["w_att"]],
                           axis=1)


def _pack(arrs, rows):
    flat = jnp.concatenate([a.reshape(-1) for a in arrs])
    return jnp.pad(flat, (0, rows * 128 - flat.shape[0])).reshape(rows, 128)


def _unpack(packed, shapes):
    flat = packed.reshape(-1)
    out, off = [], 0
    for s in shapes:
        n = int(np.prod(s))
        out.append(flat[off:off + n].reshape(s))
        off += n
    return out


def kernel(x, norm_mix, w_in, w_gla_gate, b_gla_gate, gla_norm, w_dw, b_dw, conv_ln_g, conv_ln_b, rel_bias, w_out, norm_ffn, w_up, w_down, norm_final, loss_target, m_norm_mix, m_w_in, m_w_gla_gate, m_b_gla_gate, m_gla_norm, m_w_dw, m_b_dw, m_conv_ln_g, m_conv_ln_b, m_rel_bias, m_w_out, m_norm_ffn, m_w_up, m_w_down, m_norm_final, v_norm_mix, v_w_in, v_w_gla_gate, v_b_gla_gate, v_gla_norm, v_w_dw, v_b_dw, v_conv_ln_g, v_conv_ln_b, v_rel_bias, v_w_out, v_norm_ffn, v_w_up, v_w_down, v_norm_final):
    P = dict(norm_mix=norm_mix, w_in=w_in, w_gla_gate=w_gla_gate, b_gla_gate=b_gla_gate, gla_norm=gla_norm, w_dw=w_dw, b_dw=b_dw,
             conv_ln_g=conv_ln_g, conv_ln_b=conv_ln_b, rel_bias=rel_bias, w_out=w_out, norm_ffn=norm_ffn, w_up=w_up,
             w_down=w_down, norm_final=norm_final)
    Mo = dict(norm_mix=m_norm_mix, w_in=m_w_in, w_gla_gate=m_w_gla_gate, b_gla_gate=m_b_gla_gate, gla_norm=m_gla_norm, w_dw=m_w_dw,
              b_dw=m_b_dw, conv_ln_g=m_conv_ln_g, conv_ln_b=m_conv_ln_b, rel_bias=m_rel_bias, w_out=m_w_out, norm_ffn=m_norm_ffn,
              w_up=m_w_up, w_down=m_w_down, norm_final=m_norm_final)
    Vo = dict(norm_mix=v_norm_mix, w_in=v_w_in, w_gla_gate=v_w_gla_gate, b_gla_gate=v_b_gla_gate, gla_norm=v_gla_norm, w_dw=v_w_dw,
              b_dw=v_b_dw, conv_ln_g=v_conv_ln_g, conv_ln_b=v_conv_ln_b, rel_bias=v_rel_bias, w_out=v_w_out, norm_ffn=v_norm_ffn,
              w_up=v_w_up, w_down=v_w_down, norm_final=v_norm_final)
    depth = w_in.shape[0]
    xi, yi, ci = lax.axis_index("x"), lax.axis_index("y"), lax.axis_index("c")
    chip = 2 * xi + yi

    plan = _Plan()
    _PLAN[0] = plan
    layers = [dict(
        norm_mix=norm_mix[l][None], wg=jnp.pad(w_gla_gate[l], ((0, LRW - 16), (0, KW - 192))),
        bg=_pad_cols(b_gla_gate[l][None], KW), gla_norm=gla_norm[l][None], b_dw=b_dw[l][None], ln_g=conv_ln_g[l][None],
        ln_b=conv_ln_b[l][None], rel_bias=rel_bias[l], norm_ffn=norm_ffn[l][None]) for l in range(depth)]

    def have_w_in(l, full):
        layers[l]["w_gla"], layers[l]["w_conv"], layers[l]["w_att"] = _split_w_in(jnp.transpose(full, (1, 0, 2)).reshape(D, -1))

    def have_w_out(l, full):
        w = full.reshape(D, D)
        layers[l]["w_out_g"], layers[l]["w_out_c"], layers[l]["w_out_a"] = w[0:VW], w[VW:VW + CW], w[VW + CW:]

    def have_w_up(l, full):
        layers[l]["w_up"] = full

    def have_w_down(l, full):
        layers[l]["w_down"] = full.reshape(D_FF, D)

    def have_w_dw(full):
        taps = full.reshape(4, depth, HALO, CW // 4)
        for l in range(depth):
            layers[l]["w_dw"] = jnp.transpose(taps[:, l], (1, 0, 2)).reshape(HALO, CW)

    first_d2d = []

    def first_ici(shard, have):
        return _phase_gather_ici(shard, lambda outs: first_d2d.append(_phase_gather_d2d(outs[0], lambda done: have(done[0]))))

    w_dw_pad = jnp.pad(w_dw, ((0, 0), (0, HALO - CK), (0, 0))).reshape(depth * HALO, CW // 4)
    _comm_only([first_ici(w_in[0].astype(BF16), functools.partial(have_w_in, 0)), first_ici(w_dw_pad, have_w_dw)],
               "gather_first_ici")
    _comm_only(first_d2d, "gather_first_d2d")

    def gather_behind(shard, ici_call, d2d_call, have):
        plan.at(ici_call, _phase_gather_ici(
            shard, lambda outs: plan.at(d2d_call, _phase_gather_d2d(outs[0], lambda done: have(done[0])))))

    for l in range(depth):
        if l > 0:
            gather_behind(w_in[l].astype(BF16), f"l{l - 1}_mlp_up", f"l{l - 1}_mlp_down", functools.partial(have_w_in, l))
        gather_behind(w_out[l].astype(BF16), f"l{l - 1}_mlp_down" if l > 0 else "l0_proj", f"l{l}_gla_fwd",
                      functools.partial(have_w_out, l))
        if l > 0:
            gather_behind(w_up[l].astype(BF16), f"l{l}_proj", f"l{l}_gla_fwd", functools.partial(have_w_up, l))
            gather_behind(w_down[l].astype(BF16), f"l{l}_gla_fwd", f"l{l}_att_fwd", functools.partial(have_w_down, l))
        else:
            gather_behind(w_up[l].astype(BF16), f"l{l}_gla_fwd", f"l{l}_att_fwd", functools.partial(have_w_up, l))
            gather_behind(w_down[l].astype(BF16), f"l{l}_att_fwd", f"l{l}_mlp_up", functools.partial(have_w_down, l))

    reduced = {}
    last_swap = []

    def reduce_calls(name, l):
        if name == "w_down":
            return f"l{l}_mlp_up_dx", f"l{l}_gla_bwd", f"l{l}_conv_bwd_dc"
        if name == "w_up":
            return f"l{l}_out_dx", f"l{l}_att_bwd", f"l{l}_proj_dw"
        if name == "w_out":
            return f"l{l}_gla_bwd", f"l{l}_conv_bwd_dc", f"l{l}_att_bwd"
        if l > 0:
            return f"l{l}_proj_dx", f"l{l - 1}_mlp_down_dw", f"l{l - 1}_mlp_up_dx"
        return None, "l0_proj_dx", None

    def reduce_behind(l, name, full):
        calls = reduce_calls(name, l)

        def swapped(outs):
            pair = _pair_add(full[:, None], outs[0][:, None], ci, f"reduce_pair_add_{name}{l}")[:, 0]
            plan.at(calls[1], _phase_chip_scatter(pair, scattered))

        def scattered(outs):
            half = _sum_chips(outs[0][:, None], f"reduce_sum_chips_{name}{l}")[0]
            phase = _phase_pair_allgather(half, l, depth, reduced.get(name), gathered)
            if calls[2] is None:
                last_swap.append(phase)
            else:
                plan.at(calls[2], phase)

        def gathered(outs):
            reduced[name] = outs[0]

        if calls[0] is None:
            _comm_only([_phase_pair_exchange(full, swapped)], f"reduce_pair_exchange_{name}{l}")
        else:
            plan.at(calls[0], _phase_pair_exchange(full, swapped))

    loss_part, grad_x, grads, g_final = _local_step(x[0], loss_target[0], layers, norm_final[None], reduce_behind)
    loss = lax.psum(loss_part[0, 0], ("x", "y", "c"))

    G, delta, new_m, new_v = {}, {}, {}, {}
    early = ("w_down", "w_up", "w_out")
    for name in early:
        G[name] = reduced[name]
    ds, nms, nvs = _adamw([P[k] for k in early], [G[k] for k in early], [Mo[k] for k in early], [Vo[k] for k in early],
                          "adamw_early")
    for i, name in enumerate(early):
        delta[name], new_m[name], new_v[name] = ds[i], nms[i], nvs[i]
    _PLAN[0] = None
    assert not plan.by_call, sorted(plan.by_call)

    small_g = []
    for l in range(depth):
        g = grads[l]
        small_g += [g["norm_mix"], g["wg"][0:16, 0:192], g["bg"][:, 0:192], g["gla_norm"], g["b_dw"], g["ln_g"], g["ln_b"],
                    g["rel_bias"], g["norm_ffn"], g["w_dw"][0:CK]]
    small_g.append(g_final)
    small_shapes = [a.shape for a in small_g]
    n_small = sum(int(np.prod(s)) for s in small_shapes)
    rows = -(-n_small // 1024) * 8
    red = _unpack(_allreduce_small(_pack(small_g, rows)), small_shapes)
    per = len(SMALL) + 1
    for i, name in enumerate(SMALL):
        G[name] = jnp.stack([red[l * per + i].reshape(P[name].shape[1:]) for l in range(depth)])
    gw_dw_all = jnp.stack([red[l * per + len(SMALL)] for l in range(depth)])
    G["w_dw"] = lax.dynamic_slice_in_dim(gw_dw_all, chip * (CW // 4), CW // 4, axis=2)
    G["norm_final"] = red[-1].reshape(norm_final.shape)

    _comm_only(last_swap, "reduce_pair_allgather_last")
    G["w_in"] = reduced["w_in"]
    ds, nms, nvs = _adamw([w_in], [G["w_in"]], [m_w_in], [v_w_in], "adamw_w_in")
    delta["w_in"], new_m["w_in"], new_v["w_in"] = ds[0], nms[0], nvs[0]

    small_names = list(SMALL) + ["w_dw", "norm_final"]
    two_d = lambda a: a.reshape(-1, a.shape[-1])
    ds, nms, nvs = _adamw_small([two_d(P[k]) for k in small_names], [two_d(G[k]) for k in small_names],
                                [two_d(Mo[k]) for k in small_names], [two_d(Vo[k]) for k in small_names])
    for i, name in enumerate(small_names):
        shp = P[name].shape
        delta[name], new_m[name], new_v[name] = ds[i].reshape(shp), nms[i].reshape(shp), nvs[i].reshape(shp)

    order = ["norm_mix", "w_in", "w_gla_gate", "b_gla_gate", "gla_norm", "w_dw", "b_dw", "conv_ln_g", "conv_ln_b", "rel_bias",
             "w_out", "norm_ffn", "w_up", "w_down", "norm_final"]
    return (loss, grad_x[None], *[G[k] for k in order], *[delta[k] for k in order], *[new_m[k] for k in order],
            *[new_v[k] for k in order])
```

```python
import functools

import numpy as np
import jax
import jax.numpy as jnp
from jax import lax
from jax.experimental import pallas as pl
from jax.experimental.pallas import tpu as pltpu

F32 = jnp.float32
BF16 = jnp.bfloat16
HI = lax.Precision.HIGHEST

D = 1024
CHUNK = 64
GLA_DK, GLA_DV, GLA_H = 48, 96, 4
KW = 256
VW = 384
LRW = 128
GLA_TAU = 16.0
CW = 256
CK = 31
AW = 384
AH = 6
BAND = 576
LEFT = 512
D_FF = 4096
EPS = 1e-6
NEG = -1e30
N_REL = 257

GLA_COLS = 2 * KW + 2 * VW + LRW
CONV_COLS = 2 * CW
ATT_COLS = 3 * AW

ADAM_LR, ADAM_B1, ADAM_B2, ADAM_EPS, ADAM_WD, ADAM_STEP = 0.001, 0.9, 0.999, 1e-08, 0.01, 10

VMEM_CAP = 56 * 1024 * 1024
MESH = pl.DeviceIdType.MESH


def _cparams(sem, vmem_bytes):
    limit = int(min(VMEM_CAP, max(vmem_bytes * 5 // 4 + (4 << 20), 16 << 20)))
    return pltpu.CompilerParams(dimension_semantics=sem, vmem_limit_bytes=limit)


def _out(shape, dtype):
    return pltpu.HBM(tuple(shape), dtype)


class _Comm:
    def __init__(self, ins, outs, aliases, sems, start, finish, then=None):
        self.ins, self.outs, self.aliases, self.sems = list(ins), list(outs), dict(aliases), list(sems)
        self.start, self.finish, self.then = start, finish, then


class _Plan:
    def __init__(self):
        self.by_call = {}

    def at(self, call, comm):
        self.by_call.setdefault(call, []).append(comm)

    def take(self, call):
        return self.by_call.pop(call, [])


_PLAN = [None]


def _pin(a):
    return pltpu.with_memory_space_constraint(a, pltpu.HBM) if jnp.issubdtype(a.dtype, jnp.floating) else a


def _pallas(body, **kw):
    comms = _PLAN[0].take(kw.get("name")) if _PLAN[0] is not None else []
    if not comms:
        call = pl.pallas_call(body, **kw)
        return lambda *args: call(*[_pin(a) for a in args])

    grid = tuple(kw.get("grid", ()))
    single = not isinstance(kw["out_shape"], (tuple, list))
    out_shape = [kw["out_shape"]] if single else list(kw["out_shape"])
    out_specs = [kw["out_specs"]] if single else list(kw["out_specs"])
    in_specs = list(kw["in_specs"])
    scratch = list(kw.get("scratch_shapes", ()))
    n_in, n_out, n_scr = len(in_specs), len(out_shape), len(scratch)
    c_in = sum(len(c.ins) for c in comms)
    c_out = sum(len(c.outs) for c in comms)
    aliases = dict(kw.get("input_output_aliases", {}))
    i0, o0 = n_in, n_out
    for c in comms:
        for i, o in c.aliases.items():
            aliases[i0 + i] = o0 + o
        i0 += len(c.ins)
        o0 += len(c.outs)

    def wrapped(*refs):
        ins, c_ins = refs[:n_in], refs[n_in:n_in + c_in]
        outs, c_outs = refs[n_in + c_in:n_in + c_in + n_out], refs[n_in + c_in + n_out:n_in + c_in + n_out + c_out]
        scr, c_sems = refs[n_in + c_in + n_out + c_out:][:n_scr], refs[n_in + c_in + n_out + c_out + n_scr:]

        def each(what):
            i0 = o0 = s0 = 0
            for c in comms:
                getattr(c, what)(c_ins[i0:i0 + len(c.ins)], c_outs[o0:o0 + len(c.outs)], c_sems[s0:s0 + len(c.sems)])
                i0, o0, s0 = i0 + len(c.ins), o0 + len(c.outs), s0 + len(c.sems)

        if grid:
            first = functools.reduce(jnp.logical_and, [pl.program_id(a) == 0 for a in range(len(grid))])
            last = functools.reduce(jnp.logical_and, [pl.program_id(a) == grid[a] - 1 for a in range(len(grid))])
            pl.when(first)(lambda: each("start"))
            body(*ins, *outs, *scr)
            pl.when(last)(lambda: each("finish"))
        else:
            each("start")
            body(*ins, *outs, *scr)
            each("finish")

    kw = dict(kw)
    kw["in_specs"] = in_specs + [ANY] * c_in
    kw["out_shape"] = out_shape + [_out(s.shape, s.dtype) for c in comms for s in c.outs]
    kw["out_specs"] = out_specs + [ANY] * c_out
    staging = [s for c in comms for s in c.sems if len(s) == 2 and not isinstance(s[1], int)]
    kw["scratch_shapes"] = scratch + [pltpu.VMEM(*s) if s in staging else pltpu.SemaphoreType.DMA(s) for c in comms for s in c.sems]
    kw["input_output_aliases"] = aliases
    extra = sum(_nbytes(*s) for s in staging)
    old = kw.get("compiler_params")
    limit = (old.vmem_limit_bytes if old is not None else 16 << 20) + extra
    kw["compiler_params"] = pltpu.CompilerParams(
        dimension_semantics=old.dimension_semantics if old is not None else None, vmem_limit_bytes=int(min(VMEM_CAP, limit)))
    call = pl.pallas_call(wrapped, **kw)

    def run(*args):
        res = call(*[_pin(a) for a in args], *[_pin(a) for c in comms for a in c.ins])
        o0 = n_out
        for c in comms:
            if c.then is not None:
                c.then(res[o0:o0 + len(c.outs)])
            o0 += len(c.outs)
        return res[0] if single else res[:n_out]

    return run


def _nbytes(shape, dtype):
    return int(np.prod(shape)) * jnp.dtype(dtype).itemsize


def _sigmoid(x):
    return 1.0 / (1.0 + jnp.exp(-x))


_DIMS = {"nn": (((1,), (0,)), ((), ())), "nt": (((1,), (1,)), ((), ())), "tn": (((0,), (0,)), ((), ()))}


def _mm(a, b, *, mode, out_dtype, name, tm=512, tn=None, tk=None, a_pro=None, epi=None, extra=None,
        b_chips=False, out_chips=False):
    b2 = (b.shape[1], 4 * b.shape[2]) if b_chips else b.shape
    if mode == "nn":
        (M, K), (K2, N) = a.shape, b2
    elif mode == "nt":
        (M, K), (N, K2) = a.shape, b2
    else:
        (K, M), (K2, N) = a.shape, b2
    assert K == K2, (a.shape, b.shape, mode)
    tm = min(tm, M)
    tn = N if tn is None else min(tn, N)
    tk = K if tk is None else min(tk, K)
    assert M % tm == 0 and N % tn == 0 and K % tk == 0, (M, N, K, tm, tn, tk)
    nk = K // tk
    a_blk = (tk, tm) if mode == "tn" else (tm, tk)
    a_map = (lambda i, j, k: (k, i)) if mode == "tn" else (lambda i, j, k: (i, k))
    b_blk = (tn, tk) if mode == "nt" else (tk, tn)
    b_map = (lambda i, j, k: (j, k)) if mode == "nt" else (lambda i, j, k: (k, j))
    if b_chips:
        per = b.shape[2] // b_blk[1]
        assert b.shape[2] % b_blk[1] == 0 and mode != "tn"
        flat_map = b_map
        b_map = lambda i, j, k: (flat_map(i, j, k)[1] // per, flat_map(i, j, k)[0], flat_map(i, j, k)[1] % per)
        b_blk = (None,) + b_blk
    in_specs = [pl.BlockSpec(a_blk, a_map), pl.BlockSpec(b_blk, b_map)]
    args = [a, b]
    if epi is not None:
        in_specs.append(pl.BlockSpec((tm, tn), lambda i, j, k: (i, j)))
        args.append(extra)

    def body(*refs):
        a_ref, b_ref = refs[0], refs[1]
        e_ref = refs[2] if epi is not None else None
        o_ref = refs[3] if epi is not None else refs[2]
        av = a_ref[...]
        if a_pro == "relu2":
            af = jnp.maximum(av.astype(F32), 0.0)
            av = af * af
        p = lax.dot_general(av.astype(BF16), b_ref[...].astype(BF16), _DIMS[mode], preferred_element_type=F32)

        def finish(acc):
            if epi == "add":
                acc = acc + e_ref[...].astype(F32)
            elif epi == "relu2grad":
                acc = acc * (2.0 * jnp.maximum(e_ref[...].astype(F32), 0.0))
            o_ref[...] = acc.astype(o_ref.dtype)

        if nk == 1:
            finish(p)
        else:
            acc_ref = refs[-1]
            k = pl.program_id(2)

            @pl.when(k == 0)
            def _():
                acc_ref[...] = p

            @pl.when(k > 0)
            def _():
                acc_ref[...] += p

            @pl.when(k == nk - 1)
            def _():
                finish(acc_ref[...])

    vm = 2 * (_nbytes(a_blk, a.dtype) + _nbytes((tk, tn), b.dtype) + _nbytes((tm, tn), out_dtype))
    vm += 3 * _nbytes((tm, tn), F32)
    if epi is not None:
        vm += 2 * _nbytes((tm, tn), extra.dtype)
    if out_chips:
        per_out = N // 4 // tn
        assert N % (4 * tn) == 0
        out_shape = _out((4, M, N // 4), out_dtype)
        out_spec = pl.BlockSpec((None, tm, tn), lambda i, j, k: (j // per_out, i, j % per_out))
    else:
        out_shape = _out((M, N), out_dtype)
        out_spec = pl.BlockSpec((tm, tn), lambda i, j, k: (i, j))
    return _pallas(
        body,
        out_shape=out_shape,
        grid=(M // tm, N // tn, nk),
        in_specs=in_specs,
        out_specs=out_spec,
        scratch_shapes=[pltpu.VMEM((tm, tn), F32)] if nk > 1 else [],
        compiler_params=_cparams(("parallel", "parallel", "arbitrary"), vm),
        name=name,
    )(*args)


def _mm_fan(a, bs, *, mode, out_dtypes, name, tm=512):
    M, K = a.shape
    ns = [b.shape[1] if mode == "nn" else b.shape[0] for b in bs]
    n = len(bs)

    def body(*refs):
        av = refs[0][...].astype(BF16)
        for i in range(n):
            refs[1 + n + i][...] = lax.dot_general(av, refs[1 + i][...], _DIMS[mode],
                                                   preferred_element_type=F32).astype(refs[1 + n + i].dtype)

    vm = 2 * _nbytes((tm, K), a.dtype) + sum(2 * _nbytes(b.shape, b.dtype) + 3 * _nbytes((tm, nn), F32) for b, nn in zip(bs, ns))
    return _pallas(
        body,
        out_shape=tuple(_out((M, nn), dt) for nn, dt in zip(ns, out_dtypes)),
        grid=(M // tm,),
        in_specs=[pl.BlockSpec((tm, K), lambda i: (i, 0))] + [pl.BlockSpec(b.shape, lambda i: (0, 0)) for b in bs],
        out_specs=tuple(pl.BlockSpec((tm, nn), lambda i: (i, 0)) for nn in ns),
        compiler_params=_cparams(("parallel",), vm),
        name=name,
    )(a, *bs)


def _mm_sum(as_, bs, *, mode, out_dtype, name, extra=None, tm=512):
    M = as_[0].shape[0]
    N = bs[0].shape[1] if mode == "nn" else bs[0].shape[0]
    n = len(as_)

    def body(*refs):
        acc = None
        for i in range(n):
            p = lax.dot_general(refs[i][...].astype(BF16), refs[n + i][...], _DIMS[mode], preferred_element_type=F32)
            acc = p if acc is None else acc + p
        if extra is not None:
            acc = acc + refs[2 * n][...].astype(F32)
        refs[-1][...] = acc.astype(refs[-1].dtype)

    in_specs = [pl.BlockSpec((tm, a.shape[1]), lambda i: (i, 0)) for a in as_]
    in_specs += [pl.BlockSpec(b.shape, lambda i: (0, 0)) for b in bs]
    args = list(as_) + list(bs)
    if extra is not None:
        in_specs.append(pl.BlockSpec((tm, N), lambda i: (i, 0)))
        args.append(extra)
    vm = sum(2 * _nbytes((tm, a.shape[1]), a.dtype) for a in as_) + sum(2 * _nbytes(b.shape, b.dtype) for b in bs)
    vm += 8 * _nbytes((tm, N), F32)
    return _pallas(
        body,
        out_shape=_out((M, N), out_dtype),
        grid=(M // tm,),
        in_specs=in_specs,
        out_specs=pl.BlockSpec((tm, N), lambda i: (i, 0)),
        compiler_params=_cparams(("parallel",), vm),
        name=name,
    )(*args)


def _mm_tn_multi(ops, pairs, *, name, tk=512):
    T = ops[0].shape[0]
    n, m = len(ops), len(pairs)
    shapes = [(ops[a].shape[1], ops[b].shape[1]) for a, b in pairs]

    def body(*refs):
        vals = [refs[i][...].astype(BF16) for i in range(n)]
        first = pl.program_id(0) == 0
        for j, (a, b) in enumerate(pairs):
            p = lax.dot_general(vals[a], vals[b], _DIMS["tn"], preferred_element_type=F32)
            o_ref = refs[n + j]

            @pl.when(first)
            def _(o_ref=o_ref, p=p):
                o_ref[...] = p

            @pl.when(jnp.logical_not(first))
            def _(o_ref=o_ref, p=p):
                o_ref[...] += p

    vm = sum(2 * _nbytes((tk, o.shape[1]), o.dtype) for o in ops) + sum(3 * _nbytes(s, F32) for s in shapes)
    return _pallas(
        body,
        out_shape=tuple(_out(s, F32) for s in shapes),
        grid=(T // tk,),
        in_specs=[pl.BlockSpec((tk, o.shape[1]), lambda k: (k, 0)) for o in ops],
        out_specs=tuple(pl.BlockSpec(s, lambda k: (0, 0)) for s in shapes),
        compiler_params=_cparams(("arbitrary",), vm),
        name=name,
    )(*ops)


def _rmsnorm_fwd(h, g, name, tm=512):
    T = h.shape[0]

    def body(h_ref, g_ref, o_ref):
        x = h_ref[...]
        r = lax.rsqrt(jnp.mean(x * x, axis=-1, keepdims=True) + EPS)
        o_ref[...] = (x * r * g_ref[...]).astype(o_ref.dtype)

    return _pallas(
        body,
        out_shape=_out((T, D), BF16),
        grid=(T // tm,),
        in_specs=[pl.BlockSpec((tm, D), lambda i: (i, 0)), pl.BlockSpec((1, D), lambda i: (0, 0))],
        out_specs=pl.BlockSpec((tm, D), lambda i: (i, 0)),
        compiler_params=_cparams(("parallel",), 8 * _nbytes((tm, D), F32)),
        name=name,
    )(h, g)


def _rmsnorm_bwd(dxn, h, g, dres, name, tm=512):
    T = h.shape[0]

    def body(dxn_ref, h_ref, g_ref, dres_ref, dh_ref, dg_ref):
        @pl.when(pl.program_id(0) == 0)
        def _():
            dg_ref[...] = jnp.zeros_like(dg_ref)

        x = h_ref[...]
        dy = dxn_ref[...].astype(F32)
        r = lax.rsqrt(jnp.mean(x * x, axis=-1, keepdims=True) + EPS)
        gy = dy * g_ref[...]
        dot = jnp.mean(x * gy, axis=-1, keepdims=True)
        dh_ref[...] = dres_ref[...] + r * gy - x * (r * r * r * dot)
        dg_ref[...] += jnp.sum(dy * x * r, axis=0, keepdims=True)

    row = pl.BlockSpec((tm, D), lambda i: (i, 0))
    vec = pl.BlockSpec((1, D), lambda i: (0, 0))
    return _pallas(
        body,
        out_shape=(_out((T, D), F32), _out((1, D), F32)),
        grid=(T // tm,),
        in_specs=[row, row, vec, row],
        out_specs=(row, vec),
        compiler_params=_cparams(("arbitrary",), 12 * _nbytes((tm, D), F32)),
        name=name,
    )(dxn, h, g, dres)


def _final_loss(h, g, target, name, tm=512):
    T = h.shape[0]

    def body(h_ref, g_ref, t_ref, loss_ref, dh_ref, dg_ref):
        @pl.when(pl.program_id(0) == 0)
        def _():
            dg_ref[...] = jnp.zeros_like(dg_ref)
            loss_ref[...] = jnp.zeros_like(loss_ref)

        x = h_ref[...]
        gg = g_ref[...]
        r = lax.rsqrt(jnp.mean(x * x, axis=-1, keepdims=True) + EPS)
        y = x * r * gg
        e = y - t_ref[...]
        loss_ref[...] += 0.5 * jnp.sum(jnp.mean(e * e, axis=-1, keepdims=True), axis=0, keepdims=True)
        dy = e * (1.0 / D)
        gy = dy * gg
        dot = jnp.mean(x * gy, axis=-1, keepdims=True)
        dh_ref[...] = r * gy - x * (r * r * r * dot)
        dg_ref[...] += jnp.sum(dy * x * r, axis=0, keepdims=True)

    row = pl.BlockSpec((tm, D), lambda i: (i, 0))
    vec = pl.BlockSpec((1, D), lambda i: (0, 0))
    one = pl.BlockSpec((1, 1), lambda i: (0, 0))
    return _pallas(
        body,
        out_shape=(_out((1, 1), F32), _out((T, D), F32), _out((1, D), F32)),
        grid=(T // tm,),
        in_specs=[row, vec, row],
        out_specs=(one, row, vec),
        compiler_params=_cparams(("arbitrary",), 12 * _nbytes((tm, D), F32)),
        name=name,
    )(h, g, target)


GLA_G = 8


def _gla_consts():
    i = np.arange(KW)[:, None]
    j = np.arange(VW)[None, :]
    mask = ((i // GLA_DK) == (j // GLA_DV)) & (i < GLA_H * GLA_DK)
    a = np.arange(VW)
    hm = ((a[:, None] // GLA_DV) == (a[None, :] // GLA_DV)).astype(np.float32)
    c = np.arange(CHUNK)
    low = (c[:, None] >= c[None, :]).astype(np.float32)
    return jnp.asarray(mask.astype(np.float32)), jnp.asarray(hm, BF16), jnp.asarray(low, BF16)


def _split(x):
    hi = x.astype(BF16)
    return hi, (x - hi.astype(F32)).astype(BF16)


def _dot_sel(a, b, dims, split):
    if split == "a":
        hi, lo = _split(a)
        return (lax.dot_general(hi, b, dims, preferred_element_type=F32) + lax.dot_general(lo, b, dims, preferred_element_type=F32))
    hi, lo = _split(b)
    return (lax.dot_general(a, hi, dims, preferred_element_type=F32) + lax.dot_general(a, lo, dims, preferred_element_type=F32))


def _dot3(a, b, dims):
    ah, al = _split(a)
    bh, bl = _split(b)
    return (lax.dot_general(ah, bh, dims, preferred_element_type=F32) + lax.dot_general(al, bh, dims, preferred_element_type=F32)
            + lax.dot_general(ah, bl, dims, preferred_element_type=F32))


def _dot3s(a_s, b_s, dims):
    (ah, al), (bh, bl) = a_s, b_s
    return (lax.dot_general(ah, bh, dims, preferred_element_type=F32) + lax.dot_general(al, bh, dims, preferred_element_type=F32)
            + lax.dot_general(ah, bl, dims, preferred_element_type=F32))


def _gla_group_common(p_ref, wg, bg):
    lr_s = _split(p_ref[:, 2 * KW + 2 * VW:GLA_COLS])
    wg_s = _split(wg)
    z = _dot3s(lr_s, wg_s, _DIMS["nn"]) + bg
    la = (jnp.minimum(z, 0.0) - jnp.log(1.0 + jnp.exp(-jnp.abs(z)))) * (1.0 / GLA_TAU)
    return lr_s, wg_s, z, _split(la)


def _gla_chunk_common(p_ref, rows, la_s, low, ones_v):
    q = p_ref[rows, 0:KW]
    k = p_ref[rows, KW:2 * KW]
    v = p_ref[rows, 2 * KW:2 * KW + VW]
    g = p_ref[rows, 2 * KW + VW:2 * KW + 2 * VW]
    la_h, la_l = la_s[0][rows], la_s[1][rows]
    cum = jnp.dot(low, la_h, preferred_element_type=F32) + jnp.dot(low, la_l, preferred_element_type=F32)
    endb = cum[CHUNK - 1:CHUNK, :]
    w = jnp.exp(endb - cum)
    a_full = jnp.exp(lax.dot_general(la_h, ones_v, _DIMS["tn"], preferred_element_type=F32)
                     + lax.dot_general(la_l, ones_v, _DIMS["tn"], preferred_element_type=F32))
    return q, k, v, g, w, endb, a_full


def _gla_fwd(p, wg, bg, gn, consts, name):
    T = p.shape[0]
    rb = CHUNK * GLA_G
    ng = T // rb
    mask, hm, low = consts[:3]
    scale = GLA_DK ** -0.5

    def body(p_ref, wg_ref, bg_ref, gn_ref, m_ref, hm_ref, l_ref, o_ref, st_ref, s_ref):
        @pl.when(pl.program_id(0) == 0)
        def _():
            s_ref[...] = jnp.zeros_like(s_ref)

        wg_v, bg_v, gn_v = wg_ref[...], bg_ref[...], gn_ref[...]
        ones_v = jnp.ones((CHUNK, VW), BF16)
        s_new = s_ref[...]
        _, _, _, la_s = _gla_group_common(p_ref, wg_v, bg_v)
        outs = []
        for c in range(GLA_G):
            rows = slice(c * CHUNK, (c + 1) * CHUNK)
            q, k, v, _, w, _, a_full = _gla_chunk_common(p_ref, rows, la_s, l_ref[...], ones_v)
            kd = (k * w).astype(BF16)
            kv = lax.dot_general(kd, v.astype(BF16), _DIMS["tn"], preferred_element_type=F32) * m_ref[...]
            s_new = a_full * s_new + kv
            st_ref[c] = s_new
            outs.append(jnp.dot((q * scale).astype(BF16), s_new.astype(BF16), preferred_element_type=F32))
        s_ref[...] = s_new
        o = jnp.concatenate(outs, axis=0)
        g = p_ref[:, 2 * KW + VW:2 * KW + 2 * VW]
        ms = _dot_sel(o * o, hm_ref[...], _DIMS["nn"], "a") * (1.0 / GLA_DV)
        o_ref[...] = (o * lax.rsqrt(ms + EPS) * gn_v * (g * _sigmoid(g))).astype(o_ref.dtype)

    full = lambda shape: pl.BlockSpec(shape, lambda i: tuple(0 for _ in shape))
    vm = 2 * _nbytes((rb, GLA_COLS), F32) + 2 * _nbytes((GLA_G, KW, VW), F32) + 12 * _nbytes((KW, VW), F32)
    return _pallas(
        body,
        out_shape=(_out((T, VW), BF16), _out((T // CHUNK, KW, VW), F32)),
        grid=(ng,),
        in_specs=[pl.BlockSpec((rb, GLA_COLS), lambda i: (i, 0)), full((LRW, KW)), full((1, KW)), full((1, VW)),
                  full((KW, VW)), full((VW, VW)), full((CHUNK, CHUNK))],
        out_specs=(pl.BlockSpec((rb, VW), lambda i: (i, 0)), pl.BlockSpec((GLA_G, KW, VW), lambda i: (i, 0, 0))),
        scratch_shapes=[pltpu.VMEM((KW, VW), F32)],
        compiler_params=_cparams(("arbitrary",), vm),
        name=name,
    )(p, wg, bg, gn, mask, hm, low)


def _gla_bwd(p, dy, states, wg, bg, gn, consts, name):
    T = p.shape[0]
    rb = CHUNK * GLA_G
    ng = T // rb
    mask, hm, low = consts[:3]
    scale = GLA_DK ** -0.5

    def body(p_ref, dy_ref, st_ref, sp_ref, wg_ref, bg_ref, gn_ref, m_ref, hm_ref, l_ref,
             dp_ref, dwg_ref, dbg_ref, dgn_ref, ga_ref):
        step = pl.program_id(0)

        @pl.when(step == 0)
        def _():
            ga_ref[...] = jnp.zeros_like(ga_ref)
            dwg_ref[...] = jnp.zeros_like(dwg_ref)
            dbg_ref[...] = jnp.zeros_like(dbg_ref)
            dgn_ref[...] = jnp.zeros_like(dgn_ref)

        first_group = step == ng - 1
        wg_v, bg_v, gn_v = wg_ref[...], bg_ref[...], gn_ref[...]
        ones_v = jnp.ones((CHUNK, VW), BF16)
        ones_8 = jnp.ones((8, VW), BF16)
        ga = ga_ref[...]
        lr_s, wg_s, z_all, la_s = _gla_group_common(p_ref, wg_v, bg_v)
        qss = [(p_ref[c * CHUNK:(c + 1) * CHUNK, 0:KW] * scale).astype(BF16) for c in range(GLA_G)]
        o = jnp.concatenate([jnp.dot(qss[c], st_ref[c].astype(BF16), preferred_element_type=F32) for c in range(GLA_G)], axis=0)
        g = p_ref[:, 2 * KW + VW:2 * KW + 2 * VW]
        dyv = dy_ref[...].astype(F32)
        r = lax.rsqrt(_dot_sel(o * o, hm_ref[...], _DIMS["nn"], "a") * (1.0 / GLA_DV) + EPS)
        on = o * r
        sg = _sigmoid(g)
        silu = g * sg
        d_on = dyv * gn_v * silu
        dp_ref[:, 2 * KW + VW:2 * KW + 2 * VW] = (dyv * on * gn_v * (sg * (1.0 + g * (1.0 - sg)))).astype(dp_ref.dtype)
        dgn_ref[...] += jnp.sum(dyv * on * silu, axis=0, keepdims=True)
        mo = _dot_sel(o * d_on, hm_ref[...], _DIMS["nn"], "a") * (1.0 / GLA_DV)
        dob_all = (r * d_on - o * (r * r * r) * mo).astype(BF16)
        dzs = [None] * GLA_G
        for c in reversed(range(GLA_G)):
            rows = slice(c * CHUNK, (c + 1) * CHUNK)
            _, k, v, _, w, endb, a_full = _gla_chunk_common(p_ref, rows, la_s, l_ref[...], ones_v)
            s_n = st_ref[c]
            if c > 0:
                s_prev = st_ref[c - 1]
            else:
                s_prev = jnp.where(first_group, 0.0, sp_ref[0])
            kd = k * w
            dob = dob_all[rows]
            dq = lax.dot_general(dob, s_n.astype(BF16), _DIMS["nt"], preferred_element_type=F32) * scale
            g_n = lax.dot_general(qss[c], dob, _DIMS["tn"], preferred_element_type=F32) * m_ref[...] + ga
            d_a = _dot_sel(ones_8, g_n * s_prev, _DIMS["nt"], "b")[0:1, :]
            g_nb = g_n.astype(BF16)
            dkd = lax.dot_general(v.astype(BF16), g_nb, _DIMS["nt"], preferred_element_type=F32)
            dv = jnp.dot(kd.astype(BF16), g_nb, preferred_element_type=F32)
            e = dkd * kd
            d_end = jnp.sum(e, axis=0, keepdims=True) + d_a * jnp.exp(endb)
            dla = _dot_sel(l_ref[...], -e, _DIMS["tn"], "b") + d_end
            dzs[c] = dla * (1.0 - _sigmoid(z_all[rows])) * (1.0 / GLA_TAU)
            ga = a_full * g_n
            dp_ref[rows, 0:KW] = dq.astype(dp_ref.dtype)
            dp_ref[rows, KW:2 * KW] = (dkd * w).astype(dp_ref.dtype)
            dp_ref[rows, 2 * KW:2 * KW + VW] = dv.astype(dp_ref.dtype)
        ga_ref[...] = ga
        dz = jnp.concatenate(dzs, axis=0)
        dz_s = _split(dz)
        dp_ref[:, 2 * KW + 2 * VW:GLA_COLS] = _dot3s(dz_s, wg_s, _DIMS["nt"]).astype(dp_ref.dtype)
        dwg_ref[...] += _dot3s(lr_s, dz_s, _DIMS["tn"])
        dbg_ref[...] += jnp.sum(dz, axis=0, keepdims=True)

    full = lambda shape: pl.BlockSpec(shape, lambda i: tuple(0 for _ in shape))
    rev = lambda i: (ng - 1 - i, 0)
    vm = 4 * _nbytes((rb, GLA_COLS), F32) + 2 * _nbytes((rb, VW), F32) + 2 * _nbytes((GLA_G + 1, KW, VW), F32)
    vm += 16 * _nbytes((KW, VW), F32)
    return _pallas(
        body,
        out_shape=(_out((T, GLA_COLS), BF16), _out((LRW, KW), F32),
                   _out((1, KW), F32), _out((1, VW), F32)),
        grid=(ng,),
        in_specs=[pl.BlockSpec((rb, GLA_COLS), rev), pl.BlockSpec((rb, VW), rev),
                  pl.BlockSpec((GLA_G, KW, VW), lambda i: (ng - 1 - i, 0, 0)),
                  pl.BlockSpec((1, KW, VW), lambda i: (jnp.maximum((ng - 1 - i) * GLA_G - 1, 0), 0, 0)),
                  full((LRW, KW)), full((1, KW)), full((1, VW)), full((KW, VW)), full((VW, VW)), full((CHUNK, CHUNK))],
        out_specs=(pl.BlockSpec((rb, GLA_COLS), rev), full((LRW, KW)), full((1, KW)), full((1, VW))),
        scratch_shapes=[pltpu.VMEM((KW, VW), F32)],
        compiler_params=_cparams(("arbitrary",), vm),
        name=name,
    )(p, dy, states, states, wg, bg, gn, mask, hm, low)


CONV_TM = 512
HALO = 32
CONV_RB = 64


def _glu(u):
    a = u[:, 0:CW]
    b = u[:, CW:2 * CW]
    return a * _sigmoid(b)


def _conv_taps(buf_ref, w_ref, rb0, first_tap):
    acc = jnp.zeros((CONV_RB, CW), F32)
    for j in range(CK):
        s = rb0 + first_tap(j)
        acc = acc + w_ref[j:j + 1, :] * buf_ref[s:s + CONV_RB, :]
    return acc


def _ln_fwd(c, lg, lb):
    mu = jnp.mean(c, axis=-1, keepdims=True)
    xc = c - mu
    rstd = lax.rsqrt(jnp.mean(xc * xc, axis=-1, keepdims=True) + EPS)
    n = xc * rstd
    return n, rstd, n * lg + lb


def _conv_fwd(u, w, b, lg, lb, name):
    T = u.shape[0]
    tm = CONV_TM

    def body(u_ref, uh_ref, w_ref, b_ref, lg_ref, lb_ref, o_ref, c_ref, hbuf):
        i = pl.program_id(0)
        hbuf[0:HALO, :] = jnp.where(i > 0, _glu(uh_ref[...]), 0.0)
        hbuf[HALO:HALO + tm, :] = _glu(u_ref[...])
        for r in range(tm // CONV_RB):
            rows = slice(r * CONV_RB, (r + 1) * CONV_RB)
            acc = _conv_taps(hbuf, w_ref, r * CONV_RB, lambda j: HALO - (CK - 1) + j)
            c_ref[rows, :] = acc
            _, _, zz = _ln_fwd(acc + b_ref[...], lg_ref[...], lb_ref[...])
            o_ref[rows, :] = (zz * _sigmoid(zz)).astype(o_ref.dtype)

    vec = pl.BlockSpec((1, CW), lambda i: (0, 0))
    return _pallas(
        body,
        out_shape=(_out((T, CW), BF16), _out((T, CW), F32)),
        grid=(T // tm,),
        in_specs=[pl.BlockSpec((tm, CONV_COLS), lambda i: (i, 0)),
                  pl.BlockSpec((HALO, CONV_COLS), lambda i: (jnp.maximum(i * (tm // HALO) - 1, 0), 0)),
                  pl.BlockSpec((HALO, CW), lambda i: (0, 0)), vec, vec, vec],
        out_specs=(pl.BlockSpec((tm, CW), lambda i: (i, 0)), pl.BlockSpec((tm, CW), lambda i: (i, 0))),
        scratch_shapes=[pltpu.VMEM((tm + HALO, CW), F32)],
        compiler_params=_cparams(("arbitrary",), 8 * _nbytes((tm, CONV_COLS), F32)),
        name=name,
    )(u, u, w, b, lg, lb)


def _conv_bwd_dc(u, conv, dout, b, lg, lb, name):
    T = u.shape[0]
    tm = CONV_TM
    nsteps = T // tm

    def body(u_ref, uh_ref, c_ref, do_ref, b_ref, lg_ref, lb_ref, dc_ref, dw_ref, db_ref, dlg_ref, dlb_ref, hbuf, dwacc):
        i = pl.program_id(0)

        @pl.when(i == 0)
        def _():
            dwacc[...] = jnp.zeros_like(dwacc)
            db_ref[...] = jnp.zeros_like(db_ref)
            dlg_ref[...] = jnp.zeros_like(dlg_ref)
            dlb_ref[...] = jnp.zeros_like(dlb_ref)

        hbuf[0:HALO, :] = jnp.where(i > 0, _glu(uh_ref[...]), 0.0)
        hbuf[HALO:HALO + tm, :] = _glu(u_ref[...])
        for r in range(tm // CONV_RB):
            rows = slice(r * CONV_RB, (r + 1) * CONV_RB)
            n, rstd, zz = _ln_fwd(c_ref[rows, :] + b_ref[...], lg_ref[...], lb_ref[...])
            sg = _sigmoid(zz)
            dz = do_ref[rows, :].astype(F32) * (sg * (1.0 + zz * (1.0 - sg)))
            dlg_ref[...] += jnp.sum(dz * n, axis=0, keepdims=True)
            dlb_ref[...] += jnp.sum(dz, axis=0, keepdims=True)
            dn = dz * lg_ref[...]
            dc = rstd * (dn - jnp.mean(dn, axis=-1, keepdims=True) - n * jnp.mean(dn * n, axis=-1, keepdims=True))
            dc_ref[rows, :] = dc
            db_ref[...] += jnp.sum(dc, axis=0, keepdims=True)
            for j in range(CK):
                s = r * CONV_RB + HALO - (CK - 1) + j
                prod = dc * hbuf[s:s + CONV_RB, :]
                dwacc[j] += jnp.sum(prod.reshape(CONV_RB // 8, 8, CW), axis=0)

        @pl.when(i == nsteps - 1)
        def _():
            dw_ref[...] = jnp.sum(dwacc[...], axis=1)

    vec = pl.BlockSpec((1, CW), lambda i: (0, 0))
    return _pallas(
        body,
        out_shape=(_out((T, CW), F32), _out((HALO, CW), F32),
                   _out((1, CW), F32), _out((1, CW), F32), _out((1, CW), F32)),
        grid=(nsteps,),
        in_specs=[pl.BlockSpec((tm, CONV_COLS), lambda i: (i, 0)),
                  pl.BlockSpec((HALO, CONV_COLS), lambda i: (jnp.maximum(i * (tm // HALO) - 1, 0), 0)),
                  pl.BlockSpec((tm, CW), lambda i: (i, 0)), pl.BlockSpec((tm, CW), lambda i: (i, 0)), vec, vec, vec],
        out_specs=(pl.BlockSpec((tm, CW), lambda i: (i, 0)), pl.BlockSpec((HALO, CW), lambda i: (0, 0)), vec, vec, vec),
        scratch_shapes=[pltpu.VMEM((tm + HALO, CW), F32), pltpu.VMEM((HALO, 8, CW), F32)],
        compiler_params=_cparams(("arbitrary",), 10 * _nbytes((tm, CONV_COLS), F32)),
        name=name,
    )(u, u, conv, dout, b, lg, lb)


def _conv_bwd_du(u, dc, w, name):
    T = u.shape[0]
    tm = CONV_TM
    nsteps = T // tm

    def body(u_ref, dc_ref, dch_ref, w_ref, du_ref, dcbuf):
        i = pl.program_id(0)
        dcbuf[0:tm, :] = dc_ref[...]
        dcbuf[tm:tm + HALO, :] = jnp.where(i < nsteps - 1, dch_ref[...], 0.0)
        for r in range(tm // CONV_RB):
            rows = slice(r * CONV_RB, (r + 1) * CONV_RB)
            dh = _conv_taps(dcbuf, w_ref, r * CONV_RB, lambda j: (CK - 1) - j)
            a = u_ref[rows, 0:CW]
            sb = _sigmoid(u_ref[rows, CW:2 * CW])
            du_ref[rows, 0:CW] = (dh * sb).astype(du_ref.dtype)
            du_ref[rows, CW:2 * CW] = (dh * a * sb * (1.0 - sb)).astype(du_ref.dtype)

    return _pallas(
        body,
        out_shape=_out((T, CONV_COLS), BF16),
        grid=(nsteps,),
        in_specs=[pl.BlockSpec((tm, CONV_COLS), lambda i: (i, 0)),
                  pl.BlockSpec((tm, CW), lambda i: (i, 0)),
                  pl.BlockSpec((HALO, CW), lambda i: (jnp.minimum((i + 1) * (tm // HALO), T // HALO - 1), 0)),
                  pl.BlockSpec((HALO, CW), lambda i: (0, 0))],
        out_specs=pl.BlockSpec((tm, CONV_COLS), lambda i: (i, 0)),
        scratch_shapes=[pltpu.VMEM((tm + HALO, CW), F32)],
        compiler_params=_cparams(("arbitrary",), 8 * _nbytes((tm, CONV_COLS), F32)),
        name=name,
    )(u, dc, dc, w)


ATT_G = 4


def _att_load_kv(p_any, kbuf, vbuf, sems, T):
    kc = pltpu.make_async_copy(p_any.at[:, pl.ds(AW, AW)], kbuf.at[pl.ds(LEFT, T), :], sems.at[0])
    vc = pltpu.make_async_copy(p_any.at[:, pl.ds(2 * AW, AW)], vbuf.at[pl.ds(LEFT, T), :], sems.at[1])
    kc.start()
    vc.start()
    kbuf[0:LEFT, :] = jnp.zeros((LEFT, AW), BF16)
    vbuf[0:LEFT, :] = jnp.zeros((LEFT, AW), BF16)
    kc.wait()
    vc.wait()


ATT_QB = CHUNK * ATT_G
ATT_KB = LEFT + ATT_QB
REL_PAD = 384
TOEP = 1024


def _att_consts():
    m = np.arange(TOEP)
    d = ATT_KB - 1 - m
    idx = np.clip(d, -128, 128) + 128
    sel = (np.arange(REL_PAD)[:, None] == idx[None, :]) & (m[None, :] < ATT_QB + ATT_KB - 1)
    return jnp.asarray(sel.astype(np.float32))


def _att_build_bias(rel_ref, sel_ref, bias_scr):
    tr = jnp.dot(rel_ref[...], sel_ref[...], precision=HI, preferred_element_type=F32)
    qc = lax.broadcasted_iota(jnp.int32, (ATT_QB, ATT_KB), 0) // CHUNK
    kc = lax.broadcasted_iota(jnp.int32, (ATT_QB, ATT_KB), 1) // CHUNK
    band = (kc >= qc) & (kc <= qc + 8)
    for h in range(AH):
        rows = jnp.broadcast_to(tr[h:h + 1, :], (ATT_QB, TOEP))
        toep = pltpu.roll(rows, TOEP - (ATT_QB - 1), 1, stride=1, stride_axis=0)[:, 0:ATT_KB]
        bias_scr[h // 2, (h % 2) * ATT_QB:(h % 2 + 1) * ATT_QB, :] = jnp.where(band, toep, NEG)


def _att_probs(qst, kb, bias_p, n0):
    sc = lax.dot_general(qst, kb, _DIMS["nt"], preferred_element_type=F32) * (64 ** -0.5) + bias_p
    pos = lax.broadcasted_iota(jnp.int32, (2 * ATT_QB, ATT_KB), 1)
    sc = jnp.where(pos >= CHUNK * (8 - n0), sc, NEG)
    mx = jnp.max(sc, axis=-1, keepdims=True)
    ex = jnp.exp(sc - mx)
    return ex / jnp.sum(ex, axis=-1, keepdims=True)


def _head_stack(a2, lo):
    zero = jnp.zeros_like(a2)
    return jnp.concatenate([jnp.where(lo, a2, zero), jnp.where(lo, zero, a2)], axis=0)


def _att_fwd(p, rel, sel, name):
    T = p.shape[0]

    def body(q_ref, p_any, rel_ref, sel_ref, o_ref, kbuf, vbuf, bias_scr, sems):
        i = pl.program_id(0)

        @pl.when(i == 0)
        def _():
            _att_load_kv(p_any, kbuf, vbuf, sems, T)
            _att_build_bias(rel_ref, sel_ref, bias_scr)

        lo = lax.broadcasted_iota(jnp.int32, (ATT_QB, 128), 1) < 64
        n0 = i * ATT_G
        start = pl.multiple_of(i * ATT_QB, ATT_QB)
        for hp in range(AH // 2):
            cols = slice(hp * 128, (hp + 1) * 128)
            kb = kbuf[pl.ds(start, ATT_KB), cols]
            vb = vbuf[pl.ds(start, ATT_KB), cols]
            pr = _att_probs(_head_stack(q_ref[:, cols], lo), kb, bias_scr[hp], n0)
            pv = jnp.dot(pr.astype(BF16), vb, preferred_element_type=F32)
            o_ref[:, cols] = jnp.where(lo, pv[0:ATT_QB], pv[ATT_QB:2 * ATT_QB]).astype(o_ref.dtype)

    vm = 2 * _nbytes((T + LEFT, AW), BF16) + 8 * _nbytes((2 * ATT_QB, ATT_KB), F32) + (8 << 20)
    return _pallas(
        body,
        out_shape=_out((T, AW), BF16),
        grid=(T // ATT_QB,),
        in_specs=[pl.BlockSpec((ATT_QB, AW), lambda i: (i, 0)), pl.BlockSpec(memory_space=pl.ANY),
                  pl.BlockSpec((8, REL_PAD), lambda i: (0, 0)), pl.BlockSpec((REL_PAD, TOEP), lambda i: (0, 0))],
        out_specs=pl.BlockSpec((ATT_QB, AW), lambda i: (i, 0)),
        scratch_shapes=[pltpu.VMEM((T + LEFT, AW), BF16), pltpu.VMEM((T + LEFT, AW), BF16),
                        pltpu.VMEM((AH // 2, 2 * ATT_QB, ATT_KB), F32), pltpu.SemaphoreType.DMA((2,))],
        compiler_params=_cparams(("arbitrary",), vm),
        name=name,
    )(p, p, rel, sel)


def _att_bwd(p, do, rel, sel, name):
    T = p.shape[0]
    nsteps = T // ATT_QB

    def body(q_ref, p_any, do_ref, rel_ref, sel_ref, dp_any, drel_ref,
             kbuf, vbuf, dqbuf, dkbuf, dvbuf, bias_scr, dbias_scr, dtr_scr, sems):
        i = pl.program_id(0)

        @pl.when(i == 0)
        def _():
            _att_load_kv(p_any, kbuf, vbuf, sems, T)
            _att_build_bias(rel_ref, sel_ref, bias_scr)
            dkbuf[...] = jnp.zeros_like(dkbuf)
            dvbuf[...] = jnp.zeros_like(dvbuf)
            dbias_scr[...] = jnp.zeros_like(dbias_scr)

        lo = lax.broadcasted_iota(jnp.int32, (ATT_QB, 128), 1) < 64
        n0 = i * ATT_G
        start = pl.multiple_of(i * ATT_QB, ATT_QB)
        for hp in range(AH // 2):
            cols = slice(hp * 128, (hp + 1) * 128)
            kb = kbuf[pl.ds(start, ATT_KB), cols]
            vb = vbuf[pl.ds(start, ATT_KB), cols]
            qst = _head_stack(q_ref[:, cols], lo)
            dost = _head_stack(do_ref[:, cols].astype(BF16), lo)
            pr = _att_probs(qst, kb, bias_scr[hp], n0)
            dpr = lax.dot_general(dost, vb, _DIMS["nt"], preferred_element_type=F32)
            ds = pr * (dpr - jnp.sum(dpr * pr, axis=-1, keepdims=True))
            dbias_scr[hp] += ds
            dsb = (ds * (64 ** -0.5)).astype(BF16)
            dq = jnp.dot(dsb, kb, preferred_element_type=F32)
            dqbuf[pl.ds(start, ATT_QB), cols] = jnp.where(lo, dq[0:ATT_QB], dq[ATT_QB:2 * ATT_QB]).astype(BF16)
            dkbuf[pl.ds(start, ATT_KB), cols] += lax.dot_general(dsb, qst, _DIMS["tn"], preferred_element_type=F32)
            dvbuf[pl.ds(start, ATT_KB), cols] += lax.dot_general(pr.astype(BF16), dost, _DIMS["tn"], preferred_element_type=F32)

        @pl.when(i == nsteps - 1)
        def _():
            kbuf[pl.ds(LEFT, T), :] = dkbuf[pl.ds(LEFT, T), :].astype(BF16)
            vbuf[pl.ds(LEFT, T), :] = dvbuf[pl.ds(LEFT, T), :].astype(BF16)
            cps = [pltpu.make_async_copy(dqbuf, dp_any.at[:, pl.ds(0, AW)], sems.at[0]),
                   pltpu.make_async_copy(kbuf.at[pl.ds(LEFT, T), :], dp_any.at[:, pl.ds(AW, AW)], sems.at[1]),
                   pltpu.make_async_copy(vbuf.at[pl.ds(LEFT, T), :], dp_any.at[:, pl.ds(2 * AW, AW)], sems.at[2])]
            for cp in cps:
                cp.start()
            dtr_scr[...] = jnp.zeros_like(dtr_scr)
            ri = lax.broadcasted_iota(jnp.int32, (ATT_QB, ATT_QB), 0)
            ci = lax.broadcasted_iota(jnp.int32, (ATT_QB, ATT_QB), 1)
            flip = jnp.where(ri + ci == ATT_QB - 1, 1.0, 0.0)
            for h in range(AH):
                db = dbias_scr[h // 2, (h % 2) * ATT_QB:(h % 2 + 1) * ATT_QB, :]
                db = jnp.dot(flip, db, precision=HI, preferred_element_type=F32)
                wide = jnp.concatenate([db, jnp.zeros((ATT_QB, TOEP - ATT_KB), F32)], axis=1)
                diag = pltpu.roll(wide, 0, 1, stride=1, stride_axis=0)
                dtr_scr[h:h + 1, :] = jnp.sum(diag, axis=0, keepdims=True)
            drel_ref[...] = lax.dot_general(dtr_scr[...], sel_ref[...], _DIMS["nt"], precision=HI, preferred_element_type=F32)
            for cp in cps:
                cp.wait()

    vm = 3 * _nbytes((T + LEFT, AW), BF16) + 2 * _nbytes((T + LEFT, AW), F32) + 12 * _nbytes((2 * ATT_QB, ATT_KB), F32) + (8 << 20)
    return _pallas(
        body,
        out_shape=(_out((T, ATT_COLS), BF16), _out((8, REL_PAD), F32)),
        grid=(nsteps,),
        in_specs=[pl.BlockSpec((ATT_QB, AW), lambda i: (i, 0)), pl.BlockSpec(memory_space=pl.ANY),
                  pl.BlockSpec((ATT_QB, AW), lambda i: (i, 0)),
                  pl.BlockSpec((8, REL_PAD), lambda i: (0, 0)), pl.BlockSpec((REL_PAD, TOEP), lambda i: (0, 0))],
        out_specs=(pl.BlockSpec(memory_space=pl.ANY), pl.BlockSpec((8, REL_PAD), lambda i: (0, 0))),
        scratch_shapes=[pltpu.VMEM((T + LEFT, AW), BF16), pltpu.VMEM((T + LEFT, AW), BF16), pltpu.VMEM((T, AW), BF16),
                        pltpu.VMEM((T + LEFT, AW), F32), pltpu.VMEM((T + LEFT, AW), F32),
                        pltpu.VMEM((AH // 2, 2 * ATT_QB, ATT_KB), F32), pltpu.VMEM((AH // 2, 2 * ATT_QB, ATT_KB), F32),
                        pltpu.VMEM((8, TOEP), F32), pltpu.SemaphoreType.DMA((3,))],
        compiler_params=_cparams(("arbitrary",), vm),
        name=name,
    )(p, p, do, rel, sel)


def _layer_fwd(h, wl, consts, tag):
    xn = _rmsnorm_fwd(h, wl["norm_mix"], f"{tag}_norm_mix")
    p_gla, p_conv, p_att = _mm_fan(xn, [wl["w_gla"], wl["w_conv"], wl["w_att"]], mode="nt", out_dtypes=(F32, F32, BF16),
                                   name=f"{tag}_proj")
    o_gla, states = _gla_fwd(p_gla, wl["wg"], wl["bg"], wl["gla_norm"], consts, f"{tag}_gla_fwd")
    o_conv, conv = _conv_fwd(p_conv, wl["w_dw"], wl["b_dw"], wl["ln_g"], wl["ln_b"], f"{tag}_conv_fwd")
    rel = jnp.pad(wl["rel_bias"], ((0, 8 - AH), (0, REL_PAD - N_REL)))
    o_att = _att_fwd(p_att, rel, consts[3], f"{tag}_att_fwd")
    h1 = _mm_sum([o_gla, o_conv, o_att], [wl["w_out_g"], wl["w_out_c"], wl["w_out_a"]], mode="nn", out_dtype=F32, extra=h,
                 name=f"{tag}_out")
    xn2 = _rmsnorm_fwd(h1, wl["norm_ffn"], f"{tag}_norm_ffn")
    u = _mm(xn2, wl["w_up"], mode="nn", out_dtype=BF16, tm=1024, tn=1024, b_chips=True, name=f"{tag}_mlp_up")
    h2 = _mm(u, wl["w_down"], mode="nn", out_dtype=F32, tm=1024, tk=2048, a_pro="relu2", epi="add", extra=h1,
             name=f"{tag}_mlp_down")
    saved = dict(h=h, xn=xn, p_gla=p_gla, p_conv=p_conv, p_att=p_att, states=states, o_gla=o_gla, o_conv=o_conv, conv=conv,
                 o_att=o_att, rel=rel, h1=h1, xn2=xn2, u=u)
    return h2, saved


def _layer_bwd(dh2, sv, wl, consts, tag, emit=lambda name, grad: None):
    g = {}
    du = _mm(dh2, wl["w_down"], mode="nt", out_dtype=BF16, tm=1024, tn=1024, epi="relu2grad", extra=sv["u"],
             name=f"{tag}_mlp_down_dx")
    g["w_down"] = _mm(sv["u"], dh2, mode="tn", out_dtype=F32, tm=2048, tn=1024, tk=512, a_pro="relu2", name=f"{tag}_mlp_down_dw")
    emit("w_down", g["w_down"].reshape(4, D_FF // 4, D))
    dxn2 = _mm(du, wl["w_up"], mode="nt", out_dtype=F32, tm=1024, tk=1024, b_chips=True, name=f"{tag}_mlp_up_dx")
    g["w_up"] = _mm(sv["xn2"], du, mode="tn", out_dtype=F32, tm=1024, tn=1024, tk=1024, out_chips=True, name=f"{tag}_mlp_up_dw")
    emit("w_up", g["w_up"])
    dh1, g["norm_ffn"] = _rmsnorm_bwd(dxn2, sv["h1"], wl["norm_ffn"], dh2, f"{tag}_norm_ffn_bwd")
    d_gla, d_conv, d_att = _mm_fan(dh1, [wl["w_out_g"], wl["w_out_c"], wl["w_out_a"]], mode="nt", out_dtypes=(F32, F32, F32),
                                   name=f"{tag}_out_dx")
    g["w_out_g"], g["w_out_c"], g["w_out_a"] = _mm_tn_multi([sv["o_gla"], sv["o_conv"], sv["o_att"], dh1],
                                                            [(0, 3), (1, 3), (2, 3)], name=f"{tag}_out_dw")
    emit("w_out", jnp.concatenate([g["w_out_g"], g["w_out_c"], g["w_out_a"]], axis=0).reshape(4, D // 4, D))
    dp_gla, g["wg"], g["bg"], g["gla_norm"] = _gla_bwd(sv["p_gla"], d_gla, sv["states"], wl["wg"], wl["bg"], wl["gla_norm"],
                                                       consts, f"{tag}_gla_bwd")
    dc, g["w_dw"], g["b_dw"], g["ln_g"], g["ln_b"] = _conv_bwd_dc(sv["p_conv"], sv["conv"], d_conv, wl["b_dw"], wl["ln_g"],
                                                                  wl["ln_b"], f"{tag}_conv_bwd_dc")
    dp_conv = _conv_bwd_du(sv["p_conv"], dc, wl["w_dw"], f"{tag}_conv_bwd_du")
    dp_att, drel = _att_bwd(sv["p_att"], d_att, sv["rel"], consts[3], f"{tag}_att_bwd")
    g["rel_bias"] = drel[0:AH, 0:N_REL]
    g["w_gla"], g["w_conv"], g["w_att"] = _mm_tn_multi([sv["xn"], dp_gla, dp_conv, dp_att], [(1, 0), (2, 0), (3, 0)],
                                                       name=f"{tag}_proj_dw")
    emit("w_in", _join_w_in_t(g))
    dxn = _mm_sum([dp_gla, dp_conv, dp_att], [wl["w_gla"], wl["w_conv"], wl["w_att"]], mode="nn", out_dtype=F32,
                  name=f"{tag}_proj_dx")
    dh, g["norm_mix"] = _rmsnorm_bwd(dxn, sv["h"], wl["norm_mix"], dh1, f"{tag}_norm_mix_bwd")
    return dh, g


def _local_step(x, target, layers, norm_final, emit=lambda layer, name, grad: None):
    consts = _gla_consts() + (_att_consts(),)
    h = x
    saved = []
    for l, wl in enumerate(layers):
        h, sv = _layer_fwd(h, wl, consts, f"l{l}")
        saved.append(sv)
    loss, dh, g_final = _final_loss(h, norm_final, target, "final_loss")
    grads = [None] * len(layers)
    for l in reversed(range(len(layers))):
        dh, grads[l] = _layer_bwd(dh, saved[l], layers[l], consts, f"l{l}", functools.partial(emit, l))
    return loss, dh, grads, g_final


ANY = pl.BlockSpec(memory_space=pl.ANY)


def _place():
    x, y, c = lax.axis_index("x"), lax.axis_index("y"), lax.axis_index("c")
    chips = [(1 - x, y), (x, 1 - y), (1 - x, 1 - y)]
    return x, y, c, chips


def _shape(shape, dtype):
    return jax.ShapeDtypeStruct(tuple(shape), dtype)


def _remote(src, dst, send_sem, recv_sem, to):
    return pltpu.make_async_remote_copy(src_ref=src, dst_ref=dst, send_sem=send_sem, recv_sem=recv_sem,
                                        device_id=to, device_id_type=MESH)


class _Staged:
    def __init__(self, src, dst, buf, sems):
        self.load = pltpu.make_async_copy(src, buf, sems.at[0])
        self.store = pltpu.make_async_copy(buf, dst, sems.at[1])

    def start(self):
        self.load.start()

    def wait(self):
        self.load.wait()
        self.store.start()
        self.store.wait()


def _phase_gather_ici(src, then):
    R, C = src.shape
    rh = R // 2

    def copies(ins, outs, sems):
        x, y, c, chips = _place()
        me = 2 * x + y
        local = _Staged(ins[0], outs[0].at[me], sems[3], sems[2])
        sends = [_remote(ins[0].at[pl.ds(c * rh, rh), :], outs[0].at[me, pl.ds(c * rh, rh), :], sems[0].at[k], sems[1].at[k],
                         (px, py, c)) for k, (px, py) in enumerate(chips)]
        recvs = [_remote(outs[0].at[2 * px + py, pl.ds(c * rh, rh), :], outs[0].at[2 * px + py, pl.ds(c * rh, rh), :],
                         sems[0].at[k], sems[1].at[k], (px, py, c)) for k, (px, py) in enumerate(chips)]
        return local, sends, recvs

    def start(ins, outs, sems):
        local, sends, _ = copies(ins, outs, sems)
        local.start()
        for cp in sends:
            cp.start()

    def finish(ins, outs, sems):
        local, sends, recvs = copies(ins, outs, sems)
        for cp in recvs:
            cp.wait_recv()
        for cp in sends:
            cp.wait_send()
        local.wait()

    return _Comm([src], [_shape((4, R, C), src.dtype)], {}, [(3,), (3,), (2,), ((R, C), src.dtype)], start, finish, then)


def _phase_gather_d2d(part, then):
    _, R, C = part.shape
    rh = R // 2

    def copies(ins, outs, sems):
        x, y, c, chips = _place()
        sends = [_remote(ins[0].at[2 * px + py, pl.ds(c * rh, rh), :], outs[0].at[2 * px + py, pl.ds(c * rh, rh), :],
                         sems[0].at[k], sems[1].at[k], (x, y, 1 - c)) for k, (px, py) in enumerate(chips)]
        recvs = [_remote(outs[0].at[2 * px + py, pl.ds((1 - c) * rh, rh), :], outs[0].at[2 * px + py, pl.ds((1 - c) * rh, rh), :],
                         sems[0].at[k], sems[1].at[k], (x, y, 1 - c)) for k, (px, py) in enumerate(chips)]
        return sends, recvs

    def start(ins, outs, sems):
        for cp in copies(ins, outs, sems)[0]:
            cp.start()

    def finish(ins, outs, sems):
        sends, recvs = copies(ins, outs, sems)
        for cp in recvs:
            cp.wait_recv()
        for cp in sends:
            cp.wait_send()

    return _Comm([part], [_shape(part.shape, part.dtype)], {0: 0}, [(3,), (3,)], start, finish, then)


def _phase_pair_exchange(full, then):
    _, R, C = full.shape
    rh = R // 2

    def copy(ins, outs, sems):
        x, y, c, _ = _place()
        return _remote(ins[0].at[:, pl.ds((1 - c) * rh, rh), :], outs[0], sems[0].at[0], sems[1].at[0], (x, y, 1 - c))

    return _Comm([full], [_shape((4, rh, C), full.dtype)], {}, [(1,), (1,)],
                 lambda ins, outs, sems: copy(ins, outs, sems).start(),
                 lambda ins, outs, sems: copy(ins, outs, sems).wait(), then)


def _phase_chip_scatter(parts, then):
    def copies(ins, outs, sems):
        x, y, c, chips = _place()
        me = 2 * x + y
        local = _Staged(ins[0].at[me], outs[0].at[me], sems[3], sems[2])
        sends = [_remote(ins[0].at[2 * px + py], outs[0].at[me], sems[0].at[k], sems[1].at[k], (px, py, c))
                 for k, (px, py) in enumerate(chips)]
        recvs = [_remote(outs[0].at[2 * px + py], outs[0].at[2 * px + py], sems[0].at[k], sems[1].at[k], (px, py, c))
                 for k, (px, py) in enumerate(chips)]
        return local, sends, recvs

    def start(ins, outs, sems):
        local, sends, _ = copies(ins, outs, sems)
        local.start()
        for cp in sends:
            cp.start()

    def finish(ins, outs, sems):
        local, sends, recvs = copies(ins, outs, sems)
        for cp in recvs:
            cp.wait_recv()
        for cp in sends:
            cp.wait_send()
        local.wait()

    return _Comm([parts], [_shape(parts.shape, parts.dtype)], {}, [(3,), (3,), (2,), (parts.shape[1:], parts.dtype)],
                 start, finish, then)


def _phase_pair_allgather(half, layer, depth, into, then):
    rh, C = half.shape

    def copies(ins, outs, sems):
        x, y, c, _ = _place()
        mine = outs[0].at[layer, pl.ds(c * rh, rh), :]
        theirs = outs[0].at[layer, pl.ds((1 - c) * rh, rh), :]
        return (_Staged(ins[0], mine, sems[3], sems[2]),
                _remote(ins[0], mine, sems[0].at[0], sems[1].at[0], (x, y, 1 - c)),
                _remote(theirs, theirs, sems[0].at[0], sems[1].at[0], (x, y, 1 - c)))

    def start(ins, outs, sems):
        local, send, _ = copies(ins, outs, sems)
        local.start()
        send.start()

    def finish(ins, outs, sems):
        local, send, recv = copies(ins, outs, sems)
        recv.wait_recv()
        send.wait_send()
        local.wait()

    ins = [half] if into is None else [half, into]
    return _Comm(ins, [_shape((depth, 2 * rh, C), half.dtype)], {} if into is None else {1: 0},
                 [(1,), (1,), (2,), ((rh, C), half.dtype)], start, finish, then)


def _comm_only(comms, name):
    plan = _Plan()
    for c in comms:
        plan.at(name, c)
    saved, _PLAN[0] = _PLAN[0], plan
    try:
        def body(o_ref):
            o_ref[...] = jnp.zeros_like(o_ref)

        _pallas(body, out_shape=[jax.ShapeDtypeStruct((8, 128), F32)], in_specs=[],
                out_specs=[pl.BlockSpec(memory_space=pltpu.VMEM)], name=name)()
    finally:
        _PLAN[0] = saved


def _gather_chips(items):
    n = len(items)

    def body(*refs):
        srcs, outs = refs[:n], refs[n:2 * n]
        send_sems, recv_sems, loc_sems = refs[2 * n:]
        x, y, c, chips = _place()
        me = 2 * x + y

        def half(t, chip, cc):
            rh = items[t].shape[1] // 2
            return outs[t].at[chip, :, pl.ds(cc * rh, rh), :]

        def copy(t, k, src, dst, to):
            return pltpu.make_async_remote_copy(src_ref=src, dst_ref=dst, send_sem=send_sems.at[t, k],
                                                recv_sem=recv_sems.at[t, k], device_id=to, device_id_type=MESH)

        local = [pltpu.make_async_copy(srcs[t], outs[t].at[me], loc_sems.at[t]) for t in range(n)]
        for cp in local:
            cp.start()
        sent = []
        for t in range(n):
            rh = items[t].shape[1] // 2
            mine = srcs[t].at[:, pl.ds(c * rh, rh), :]
            for k, (px, py) in enumerate(chips):
                sent.append(copy(t, k, mine, half(t, me, c), (px, py, c)))
                sent[-1].start()
        for k, (px, py) in enumerate(chips):
            for t in range(n):
                blk = half(t, 2 * px + py, c)
                copy(t, k, blk, blk, (px, py, c)).wait_recv()
                sent.append(copy(t, 3 + k, blk, blk, (x, y, 1 - c)))
                sent[-1].start()
        for k, (px, py) in enumerate(chips):
            for t in range(n):
                blk = half(t, 2 * px + py, 1 - c)
                copy(t, 3 + k, blk, blk, (x, y, 1 - c)).wait_recv()
        for cp in sent:
            cp.wait_send()
        for cp in local:
            cp.wait()

    return _pallas(
        body,
        out_shape=[_out((4,) + a.shape, a.dtype) for a in items],
        in_specs=[ANY] * n,
        out_specs=[ANY] * n,
        scratch_shapes=[pltpu.SemaphoreType.DMA((n, 6)), pltpu.SemaphoreType.DMA((n, 6)), pltpu.SemaphoreType.DMA((n,))],
        name="gather_weights",
    )(*items)


def _pair_exchange(items):
    n = len(items)

    def body(*refs):
        srcs, outs = refs[:n], refs[n:2 * n]
        send_sems, recv_sems = refs[2 * n:]
        x, y, c, _ = _place()
        cps = []
        for t in range(n):
            rh = items[t].shape[2] // 2
            cps.append(pltpu.make_async_remote_copy(
                src_ref=srcs[t].at[:, :, pl.ds((1 - c) * rh, rh), :], dst_ref=outs[t], send_sem=send_sems.at[t],
                recv_sem=recv_sems.at[t], device_id=(x, y, 1 - c), device_id_type=MESH))
            cps[-1].start()
        for cp in cps:
            cp.wait()

    return _pallas(
        body,
        out_shape=[_out(a.shape[:2] + (a.shape[2] // 2, a.shape[3]), a.dtype) for a in items],
        in_specs=[ANY] * n,
        out_specs=[ANY] * n,
        scratch_shapes=[pltpu.SemaphoreType.DMA((n,)), pltpu.SemaphoreType.DMA((n,))],
        name="reduce_pair_exchange",
    )(*items)


def _row_tile(rows, cols, itemsize=4, budget=1 << 20, mult=8):
    fits = [t for t in range(mult, rows + 1, mult) if rows % t == 0 and t * cols * itemsize <= budget]
    return max(fits) if fits else rows


def _pair_add(full, got, c, name):
    _, L, R, C = full.shape
    rh = R // 2
    tr = _row_tile(rh, C, budget=2 << 20, mult=16)
    nb = rh // tr

    def body(c_ref, a_ref, b_ref, o_ref):
        o_ref[...] = (a_ref[...] + b_ref[...]).astype(o_ref.dtype)

    grid_spec = pltpu.PrefetchScalarGridSpec(
        num_scalar_prefetch=1,
        grid=(4, L, nb),
        in_specs=[pl.BlockSpec((1, 1, tr, C), lambda j, l, i, c_ref: (j, l, c_ref[0] * nb + i, 0)),
                  pl.BlockSpec((1, 1, tr, C), lambda j, l, i, c_ref: (j, l, i, 0))],
        out_specs=pl.BlockSpec((1, 1, tr, C), lambda j, l, i, c_ref: (j, l, i, 0)),
    )
    return _pallas(
        body,
        out_shape=_out((4, L, rh, C), BF16),
        grid_spec=grid_spec,
        compiler_params=_cparams(("parallel", "parallel", "parallel"), 8 * tr * C * 4),
        name=name,
    )(jnp.reshape(c, (1,)).astype(jnp.int32), full, got)


def _chip_scatter(items):
    n = len(items)

    def body(*refs):
        srcs, outs = refs[:n], refs[n:2 * n]
        send_sems, recv_sems, loc_sems = refs[2 * n:]
        x, y, c, chips = _place()
        me = 2 * x + y
        local = [pltpu.make_async_copy(srcs[t].at[me], outs[t].at[me], loc_sems.at[t]) for t in range(n)]
        for cp in local:
            cp.start()
        cps = []
        for t in range(n):
            for k, (px, py) in enumerate(chips):
                cps.append(pltpu.make_async_remote_copy(
                    src_ref=srcs[t].at[2 * px + py], dst_ref=outs[t].at[me], send_sem=send_sems.at[t, k],
                    recv_sem=recv_sems.at[t, k], device_id=(px, py, c), device_id_type=MESH))
                cps[-1].start()
        for t in range(n):
            for k, (px, py) in enumerate(chips):
                blk = outs[t].at[2 * px + py]
                pltpu.make_async_remote_copy(src_ref=blk, dst_ref=blk, send_sem=send_sems.at[t, k], recv_sem=recv_sems.at[t, k],
                                             device_id=(px, py, c), device_id_type=MESH).wait_recv()
        for cp in cps:
            cp.wait_send()
        for cp in local:
            cp.wait()

    return _pallas(
        body,
        out_shape=[_out(a.shape, a.dtype) for a in items],
        in_specs=[ANY] * n,
        out_specs=[ANY] * n,
        scratch_shapes=[pltpu.SemaphoreType.DMA((n, 3)), pltpu.SemaphoreType.DMA((n, 3)), pltpu.SemaphoreType.DMA((n,))],
        name="reduce_chip_scatter",
    )(*items)


def _sum_chips(parts, name):
    _, L, rh, C = parts.shape
    tr = _row_tile(rh, C, budget=2 << 20, mult=16)

    def body(p_ref, o_ref):
        acc = p_ref[0].astype(F32)
        for j in range(1, 4):
            acc = acc + p_ref[j].astype(F32)
        o_ref[...] = acc

    return _pallas(
        body,
        out_shape=_out((L, rh, C), F32),
        grid=(L, rh // tr),
        in_specs=[pl.BlockSpec((4, 1, tr, C), lambda l, i: (0, l, i, 0))],
        out_specs=pl.BlockSpec((1, tr, C), lambda l, i: (l, i, 0)),
        compiler_params=_cparams(("parallel", "parallel"), 16 * tr * C * 4),
        name=name,
    )(parts)


def _pair_allgather(groups):
    flat = [(w, l) for w, grp in enumerate(groups) for l in range(len(grp))]
    items = [groups[w][l] for w, l in flat]
    n, nw = len(items), len(groups)

    def body(*refs):
        srcs, outs = refs[:n], refs[n:n + nw]
        send_sems, recv_sems, loc_sems = refs[n + nw:]
        x, y, c, _ = _place()
        cps, local = [], []
        for t, (w, l) in enumerate(flat):
            rh = items[t].shape[1]
            mine = outs[w].at[pl.ds(l, 1), pl.ds(c * rh, rh), :]
            local.append(pltpu.make_async_copy(srcs[t], mine, loc_sems.at[t]))
            local[-1].start()
            cps.append(pltpu.make_async_remote_copy(src_ref=srcs[t], dst_ref=mine, send_sem=send_sems.at[t],
                                                    recv_sem=recv_sems.at[t], device_id=(x, y, 1 - c), device_id_type=MESH))
            cps[-1].start()
        for t, (w, l) in enumerate(flat):
            rh = items[t].shape[1]
            theirs = outs[w].at[pl.ds(l, 1), pl.ds((1 - c) * rh, rh), :]
            pltpu.make_async_remote_copy(src_ref=theirs, dst_ref=theirs, send_sem=send_sems.at[t], recv_sem=recv_sems.at[t],
                                         device_id=(x, y, 1 - c), device_id_type=MESH).wait_recv()
        for cp in cps:
            cp.wait_send()
        for cp in local:
            cp.wait()

    return _pallas(
        body,
        out_shape=[_out((len(grp), 2 * grp[0].shape[1], grp[0].shape[2]), grp[0].dtype) for grp in groups],
        in_specs=[ANY] * n,
        out_specs=[ANY] * nw,
        scratch_shapes=[pltpu.SemaphoreType.DMA((n,)), pltpu.SemaphoreType.DMA((n,)), pltpu.SemaphoreType.DMA((n,))],
        name="reduce_pair_allgather",
    )(*items)


def _allreduce_small(v):
    R = v.shape[0]

    def body(v_ref, o_ref, slots, send_sems, recv_sems):
        x, y, c, _ = _place()
        me = 4 * x + 2 * y + c
        slots[me] = v_ref[...]
        cps = []
        for r in range(1, 8):
            px, py, pc = x ^ (r >> 2), y ^ ((r >> 1) & 1), c ^ (r & 1)
            cps.append(pltpu.make_async_remote_copy(src_ref=v_ref, dst_ref=slots.at[me], send_sem=send_sems.at[r - 1],
                                                    recv_sem=recv_sems.at[r - 1], device_id=(px, py, pc), device_id_type=MESH))
            cps[-1].start()
        for r in range(1, 8):
            px, py, pc = x ^ (r >> 2), y ^ ((r >> 1) & 1), c ^ (r & 1)
            theirs = slots.at[4 * px + 2 * py + pc]
            pltpu.make_async_remote_copy(src_ref=theirs, dst_ref=theirs, send_sem=send_sems.at[r - 1], recv_sem=recv_sems.at[r - 1],
                                         device_id=(px, py, pc), device_id_type=MESH).wait_recv()
        acc = slots[0]
        for j in range(1, 8):
            acc = acc + slots[j]
        o_ref[...] = acc
        for cp in cps:
            cp.wait_send()

    return pl.pallas_call(
        body,
        out_shape=jax.ShapeDtypeStruct(v.shape, F32),
        in_specs=[pl.BlockSpec(memory_space=pltpu.VMEM)],
        out_specs=pl.BlockSpec(memory_space=pltpu.VMEM),
        scratch_shapes=[pltpu.VMEM((8, R, 128), F32), pltpu.SemaphoreType.DMA((7,)), pltpu.SemaphoreType.DMA((7,))],
        name="allreduce_small",
    )(v)


def _adamw_math(w, g, m, v):
    m = ADAM_B1 * m + (1.0 - ADAM_B1) * g
    v = ADAM_B2 * v + (1.0 - ADAM_B2) * (g * g)
    m_hat = m / (1.0 - ADAM_B1 ** ADAM_STEP)
    v_hat = v / (1.0 - ADAM_B2 ** ADAM_STEP)
    delta = -ADAM_LR * (m_hat / (jnp.sqrt(v_hat) + ADAM_EPS) + ADAM_WD * w)
    return delta, m, v


def _adamw(ws, gs, ms, vs, name, budget=1 << 19):
    n = len(ws)
    tiles = [_row_tile(w.shape[1], w.shape[2], budget=budget) for w in ws]
    per_layer = [w.shape[1] // t for w, t in zip(ws, tiles)]
    steps = [w.shape[0] * p for w, p in zip(ws, per_layer)]
    starts = [sum(steps[:k]) for k in range(n)]

    def body(*refs):
        i = pl.program_id(0)
        for k in range(n):
            w_ref, g_ref, m_ref, v_ref = (refs[j * n + k] for j in range(4))
            outs = [refs[(4 + j) * n + k] for j in range(3)]

            @pl.when((i >= starts[k]) & (i < starts[k] + steps[k]))
            def _(w_ref=w_ref, g_ref=g_ref, m_ref=m_ref, v_ref=v_ref, outs=outs):
                outs[0][...], outs[1][...], outs[2][...] = _adamw_math(w_ref[...], g_ref[...], m_ref[...], v_ref[...])

    def spec(k):
        def index(i):
            local = jnp.clip(i - starts[k], 0, steps[k] - 1)
            return local // per_layer[k], local % per_layer[k], 0
        return pl.BlockSpec((None, tiles[k], ws[k].shape[2]), index)

    specs = [spec(k) for k in range(n)]
    outs = [_out(w.shape, F32) for w in ws]
    res = _pallas(
        body,
        out_shape=tuple(outs * 3),
        grid=(sum(steps),),
        in_specs=specs * 4,
        out_specs=tuple(specs * 3),
        compiler_params=_cparams(("arbitrary",), sum(16 * t * w.shape[2] * 4 for w, t in zip(ws, tiles))),
        name=name,
    )(*ws, *gs, *ms, *vs)
    return res[:n], res[n:2 * n], res[2 * n:]


def _adamw_small(ws, gs, ms, vs):
    n = len(ws)

    def body(*refs):
        for t in range(n):
            w_ref, g_ref, m_ref, v_ref = (refs[k * n + t] for k in range(4))
            d_ref, nm_ref, nv_ref = (refs[(4 + k) * n + t] for k in range(3))
            d_ref[...], nm_ref[...], nv_ref[...] = _adamw_math(w_ref[...], g_ref[...], m_ref[...], v_ref[...])

    vmem = pl.BlockSpec(memory_space=pltpu.VMEM)
    outs = [jax.ShapeDtypeStruct(w.shape, F32) for w in ws]
    res = pl.pallas_call(
        body,
        out_shape=outs * 3,
        in_specs=[vmem] * (4 * n),
        out_specs=[vmem] * (3 * n),
        name="adamw_small",
    )(*ws, *gs, *ms, *vs)
    return res[:n], res[n:2 * n], res[2 * n:]


IN_SIZES = (192, 192, 384, 384, 16, 512, 384, 384, 384)
IN_OFFS = tuple(int(v) for v in np.cumsum((0,) + IN_SIZES))
SMALL = ("norm_mix", "w_gla_gate", "b_gla_gate", "gla_norm", "b_dw", "conv_ln_g", "conv_ln_b", "rel_bias", "norm_ffn")


def _pad_cols(a, n):
    return jnp.pad(a, ((0, 0), (0, n - a.shape[1])))


W_IN_SHARD = 708
W_IN_ROWS = 736


def _pad_rows(a, n):
    return jnp.pad(a, ((0, n - a.shape[0]), (0, 0)))


def _split_w_in_t(w):
    s = [w[IN_OFFS[i]:IN_OFFS[i + 1]] for i in range(9)]
    w_gla = jnp.concatenate([_pad_rows(s[0], KW), _pad_rows(s[1], KW), s[2], s[3], _pad_rows(s[4], LRW)], axis=0)
    return w_gla, s[5], jnp.concatenate(s[6:9], axis=0)


def _join_w_in_t(g):
    gg = g["w_gla"]
    full = jnp.concatenate([gg[0:192], gg[KW:KW + 192], gg[2 * KW:2 * KW + VW], gg[2 * KW + VW:2 * KW + 2 * VW],
                            gg[2 * KW + 2 * VW:2 * KW + 2 * VW + 16], g["w_conv"], g["w_att"]], axis=0)
    return jnp.pad(full.reshape(4, W_IN_SHARD, D), ((0, 0), (0, W_IN_ROWS - W_IN_SHARD), (0, 0)))


def _pack(arrs, rows):
    flat = jnp.concatenate([a.reshape(-1) for a in arrs])
    return jnp.pad(flat, (0, rows * 128 - flat.shape[0])).reshape(rows, 128)


def _unpack(packed, shapes):
    flat = packed.reshape(-1)
    out, off = [], 0
    for s in shapes:
        n = int(np.prod(s))
        out.append(flat[off:off + n].reshape(s))
        off += n
    return out


def kernel(x, norm_mix, w_in, w_gla_gate, b_gla_gate, gla_norm, w_dw, b_dw, conv_ln_g, conv_ln_b, rel_bias, w_out, norm_ffn, w_up, w_down, norm_final, loss_target, m_norm_mix, m_w_in, m_w_gla_gate, m_b_gla_gate, m_gla_norm, m_w_dw, m_b_dw, m_conv_ln_g, m_conv_ln_b, m_rel_bias, m_w_out, m_norm_ffn, m_w_up, m_w_down, m_norm_final, v_norm_mix, v_w_in, v_w_gla_gate, v_b_gla_gate, v_gla_norm, v_w_dw, v_b_dw, v_conv_ln_g, v_conv_ln_b, v_rel_bias, v_w_out, v_norm_ffn, v_w_up, v_w_down, v_norm_final):
    P = dict(norm_mix=norm_mix, w_in=w_in, w_gla_gate=w_gla_gate, b_gla_gate=b_gla_gate, gla_norm=gla_norm, w_dw=w_dw, b_dw=b_dw,
             conv_ln_g=conv_ln_g, conv_ln_b=conv_ln_b, rel_bias=rel_bias, w_out=w_out, norm_ffn=norm_ffn, w_up=w_up,
             w_down=w_down, norm_final=norm_final)
    Mo = dict(norm_mix=m_norm_mix, w_in=m_w_in, w_gla_gate=m_w_gla_gate, b_gla_gate=m_b_gla_gate, gla_norm=m_gla_norm, w_dw=m_w_dw,
              b_dw=m_b_dw, conv_ln_g=m_conv_ln_g, conv_ln_b=m_conv_ln_b, rel_bias=m_rel_bias, w_out=m_w_out, norm_ffn=m_norm_ffn,
              w_up=m_w_up, w_down=m_w_down, norm_final=m_norm_final)
    Vo = dict(norm_mix=v_norm_mix, w_in=v_w_in, w_gla_gate=v_w_gla_gate, b_gla_gate=v_b_gla_gate, gla_norm=v_gla_norm, w_dw=v_w_dw,
              b_dw=v_b_dw, conv_ln_g=v_conv_ln_g, conv_ln_b=v_conv_ln_b, rel_bias=v_rel_bias, w_out=v_w_out, norm_ffn=v_norm_ffn,
              w_up=v_w_up, w_down=v_w_down, norm_final=v_norm_final)
    depth = w_in.shape[0]
    xi, yi, ci = lax.axis_index("x"), lax.axis_index("y"), lax.axis_index("c")
    chip = 2 * xi + yi

    plan = _Plan()
    _PLAN[0] = plan
    layers = [dict(
        norm_mix=norm_mix[l][None], wg=jnp.pad(w_gla_gate[l], ((0, LRW - 16), (0, KW - 192))),
        bg=_pad_cols(b_gla_gate[l][None], KW), gla_norm=gla_norm[l][None], b_dw=b_dw[l][None], ln_g=conv_ln_g[l][None],
        ln_b=conv_ln_b[l][None], rel_bias=rel_bias[l], norm_ffn=norm_ffn[l][None]) for l in range(depth)]

    w_in_t, m_w_in_t, v_w_in_t = (jnp.transpose(a, (2, 0, 1)) for a in (w_in, m_w_in, v_w_in))

    def w_in_shard(l):
        return _pad_rows(w_in_t[:, l, :], W_IN_ROWS).astype(BF16)

    def have_w_in(l, full):
        rows = jnp.concatenate([full[j, 0:W_IN_SHARD] for j in range(4)], axis=0)
        layers[l]["w_gla"], layers[l]["w_conv"], layers[l]["w_att"] = _split_w_in_t(rows)

    def have_w_out(l, full):
        w = full.reshape(D, D)
        layers[l]["w_out_g"], layers[l]["w_out_c"], layers[l]["w_out_a"] = w[0:VW], w[VW:VW + CW], w[VW + CW:]

    def have_w_up(l, full):
        layers[l]["w_up"] = full

    def have_w_down(l, full):
        layers[l]["w_down"] = full.reshape(D_FF, D)

    def have_w_dw(full):
        taps = full.reshape(4, depth, HALO, CW // 4)
        for l in range(depth):
            layers[l]["w_dw"] = jnp.transpose(taps[:, l], (1, 0, 2)).reshape(HALO, CW)

    first_d2d = []

    def first_ici(shard, have):
        return _phase_gather_ici(shard, lambda outs: first_d2d.append(_phase_gather_d2d(outs[0], lambda done: have(done[0]))))

    w_dw_pad = jnp.pad(w_dw, ((0, 0), (0, HALO - CK), (0, 0))).reshape(depth * HALO, CW // 4)
    _comm_only([first_ici(w_in_shard(0), functools.partial(have_w_in, 0)), first_ici(w_dw_pad, have_w_dw)],
               "gather_first_ici")
    _comm_only(first_d2d, "gather_first_d2d")

    def gather_behind(shard, ici_call, d2d_call, have):
        plan.at(ici_call, _phase_gather_ici(
            shard, lambda outs: plan.at(d2d_call, _phase_gather_d2d(outs[0], lambda done: have(done[0])))))

    for l in range(depth):
        if l > 0:
            gather_behind(w_in_shard(l), f"l{l - 1}_mlp_up", f"l{l - 1}_mlp_down", functools.partial(have_w_in, l))
        gather_behind(w_out[l].astype(BF16), f"l{l - 1}_mlp_down" if l > 0 else "l0_proj", f"l{l}_gla_fwd",
                      functools.partial(have_w_out, l))
        if l > 0:
            gather_behind(w_up[l].astype(BF16), f"l{l}_proj", f"l{l}_gla_fwd", functools.partial(have_w_up, l))
            gather_behind(w_down[l].astype(BF16), f"l{l}_gla_fwd", f"l{l}_att_fwd", functools.partial(have_w_down, l))
        else:
            gather_behind(w_up[l].astype(BF16), f"l{l}_gla_fwd", f"l{l}_att_fwd", functools.partial(have_w_up, l))
            gather_behind(w_down[l].astype(BF16), f"l{l}_att_fwd", f"l{l}_mlp_up", functools.partial(have_w_down, l))

    reduced = {}
    last_swap = []

    def reduce_calls(name, l):
        if name == "w_down":
            return f"l{l}_mlp_up_dx", f"l{l}_gla_bwd", f"l{l}_conv_bwd_dc"
        if name == "w_up":
            return f"l{l}_out_dx", f"l{l}_att_bwd", f"l{l}_proj_dw"
        if name == "w_out":
            return f"l{l}_gla_bwd", f"l{l}_conv_bwd_dc", f"l{l}_att_bwd"
        if l > 0:
            return f"l{l}_proj_dx", f"l{l - 1}_mlp_down_dw", f"l{l - 1}_mlp_up_dx"
        return None, "l0_proj_dx", None

    def reduce_behind(l, name, full):
        calls = reduce_calls(name, l)

        def swapped(outs):
            pair = _pair_add(full[:, None], outs[0][:, None], ci, f"reduce_pair_add_{name}{l}")[:, 0]
            plan.at(calls[1], _phase_chip_scatter(pair, scattered))

        def scattered(outs):
            half = _sum_chips(outs[0][:, None], f"reduce_sum_chips_{name}{l}")[0]
            phase = _phase_pair_allgather(half, l, depth, reduced.get(name), gathered)
            if calls[2] is None:
                last_swap.append(phase)
            else:
                plan.at(calls[2], phase)

        def gathered(outs):
            reduced[name] = outs[0]

        if calls[0] is None:
            _comm_only([_phase_pair_exchange(full, swapped)], f"reduce_pair_exchange_{name}{l}")
        else:
            plan.at(calls[0], _phase_pair_exchange(full, swapped))

    loss_part, grad_x, grads, g_final = _local_step(x[0], loss_target[0], layers, norm_final[None], reduce_behind)
    loss = lax.psum(loss_part[0, 0], ("x", "y", "c"))

    G, delta, new_m, new_v = {}, {}, {}, {}
    early = ("w_down", "w_up", "w_out")
    for name in early:
        G[name] = reduced[name]
    ds, nms, nvs = _adamw([P[k] for k in early], [G[k] for k in early], [Mo[k] for k in early], [Vo[k] for k in early],
                          "adamw_early")
    for i, name in enumerate(early):
        delta[name], new_m[name], new_v[name] = ds[i], nms[i], nvs[i]
    _PLAN[0] = None
    assert not plan.by_call, sorted(plan.by_call)

    small_g = []
    for l in range(depth):
        g = grads[l]
        small_g += [g["norm_mix"], g["wg"][0:16, 0:192], g["bg"][:, 0:192], g["gla_norm"], g["b_dw"], g["ln_g"], g["ln_b"],
                    g["rel_bias"], g["norm_ffn"], g["w_dw"][0:CK]]
    small_g.append(g_final)
    small_shapes = [a.shape for a in small_g]
    n_small = sum(int(np.prod(s)) for s in small_shapes)
    rows = -(-n_small // 1024) * 8
    red = _unpack(_allreduce_small(_pack(small_g, rows)), small_shapes)
    per = len(SMALL) + 1
    for i, name in enumerate(SMALL):
        G[name] = jnp.stack([red[l * per + i].reshape(P[name].shape[1:]) for l in range(depth)])
    gw_dw_all = jnp.stack([red[l * per + len(SMALL)] for l in range(depth)])
    G["w_dw"] = lax.dynamic_slice_in_dim(gw_dw_all, chip * (CW // 4), CW // 4, axis=2)
    G["norm_final"] = red[-1].reshape(norm_final.shape)

    _comm_only(last_swap, "reduce_pair_allgather_last")
    flat = lambda a: a.reshape(1, W_IN_SHARD * depth, D)
    back = lambda a: jnp.transpose(a.reshape(W_IN_SHARD, depth, D), (1, 2, 0))
    g_in_t = flat(jnp.transpose(reduced["w_in"][:, 0:W_IN_SHARD, :], (1, 0, 2)))
    ds, nms, nvs = _adamw([flat(w_in_t)], [g_in_t], [flat(m_w_in_t)], [flat(v_w_in_t)], "adamw_w_in", budget=2 << 20)
    G["w_in"], delta["w_in"], new_m["w_in"], new_v["w_in"] = back(g_in_t), back(ds[0]), back(nms[0]), back(nvs[0])

    small_names = list(SMALL) + ["w_dw", "norm_final"]
    two_d = lambda a: a.reshape(-1, a.shape[-1])
    ds, nms, nvs = _adamw_small([two_d(P[k]) for k in small_names], [two_d(G[k]) for k in small_names],
                                [two_d(Mo[k]) for k in small_names], [two_d(Vo[k]) for k in small_names])
    for i, name in enumerate(small_names):
        shp = P[name].shape
        delta[name], new_m[name], new_v[name] = ds[i].reshape(shp), nms[i].reshape(shp), nvs[i].reshape(shp)

    order = ["norm_mix", "w_in", "w_gla_gate", "b_gla_gate", "gla_norm", "w_dw", "b_dw", "conv_ln_g", "conv_ln_b", "rel_bias",
             "w_out", "norm_ffn", "w_up", "w_down", "norm_final"]
    return (loss, grad_x[None], *[G[k] for k in order], *[delta[k] for k in order], *[new_m[k] for k in order],
            *[new_v[k] for k in order])
```

```python
import functools

import numpy as np
import jax
import jax.numpy as jnp
from jax import lax
from jax.experimental import pallas as pl
from jax.experimental.pallas import tpu as pltpu

F32 = jnp.float32
BF16 = jnp.bfloat16
HI = lax.Precision.HIGHEST

D = 1024
CHUNK = 64
GLA_DK, GLA_DV, GLA_H = 48, 96, 4
KW = 256
VW = 384
LRW = 128
GLA_TAU = 16.0
CW = 256
CK = 31
AW = 384
AH = 6
BAND = 576
LEFT = 512
D_FF = 4096
EPS = 1e-6
NEG = -1e30
N_REL = 257

GLA_COLS = 2 * KW + 2 * VW + LRW
CONV_COLS = 2 * CW
ATT_COLS = 3 * AW

ADAM_LR, ADAM_B1, ADAM_B2, ADAM_EPS, ADAM_WD, ADAM_STEP = 0.001, 0.9, 0.999, 1e-08, 0.01, 10

VMEM_CAP = 56 * 1024 * 1024
MESH = pl.DeviceIdType.MESH


def _cparams(sem, vmem_bytes):
    limit = int(min(VMEM_CAP, max(vmem_bytes * 5 // 4 + (4 << 20), 16 << 20)))
    return pltpu.CompilerParams(dimension_semantics=sem, vmem_limit_bytes=limit)


def _out(shape, dtype):
    return pltpu.HBM(tuple(shape), dtype)


class _Comm:
    def __init__(self, ins, outs, aliases, sems, start, finish, then=None):
        self.ins, self.outs, self.aliases, self.sems = list(ins), list(outs), dict(aliases), list(sems)
        self.start, self.finish, self.then = start, finish, then


class _Plan:
    def __init__(self):
        self.by_call = {}

    def at(self, call, comm):
        self.by_call.setdefault(call, []).append(comm)

    def take(self, call):
        return self.by_call.pop(call, [])


_PLAN = [None]


def _pin(a):
    return pltpu.with_memory_space_constraint(a, pltpu.HBM) if jnp.issubdtype(a.dtype, jnp.floating) else a


def _pallas(body, **kw):
    comms = _PLAN[0].take(kw.get("name")) if _PLAN[0] is not None else []
    if not comms:
        call = pl.pallas_call(body, **kw)
        return lambda *args: call(*[_pin(a) for a in args])

    grid = tuple(kw.get("grid", ()))
    single = not isinstance(kw["out_shape"], (tuple, list))
    out_shape = [kw["out_shape"]] if single else list(kw["out_shape"])
    out_specs = [kw["out_specs"]] if single else list(kw["out_specs"])
    in_specs = list(kw["in_specs"])
    scratch = list(kw.get("scratch_shapes", ()))
    n_in, n_out, n_scr = len(in_specs), len(out_shape), len(scratch)
    c_in = sum(len(c.ins) for c in comms)
    c_out = sum(len(c.outs) for c in comms)
    aliases = dict(kw.get("input_output_aliases", {}))
    i0, o0 = n_in, n_out
    for c in comms:
        for i, o in c.aliases.items():
            aliases[i0 + i] = o0 + o
        i0 += len(c.ins)
        o0 += len(c.outs)

    def wrapped(*refs):
        ins, c_ins = refs[:n_in], refs[n_in:n_in + c_in]
        outs, c_outs = refs[n_in + c_in:n_in + c_in + n_out], refs[n_in + c_in + n_out:n_in + c_in + n_out + c_out]
        scr, c_sems = refs[n_in + c_in + n_out + c_out:][:n_scr], refs[n_in + c_in + n_out + c_out + n_scr:]

        def each(what):
            i0 = o0 = s0 = 0
            for c in comms:
                getattr(c, what)(c_ins[i0:i0 + len(c.ins)], c_outs[o0:o0 + len(c.outs)], c_sems[s0:s0 + len(c.sems)])
                i0, o0, s0 = i0 + len(c.ins), o0 + len(c.outs), s0 + len(c.sems)

        if grid:
            first = functools.reduce(jnp.logical_and, [pl.program_id(a) == 0 for a in range(len(grid))])
            last = functools.reduce(jnp.logical_and, [pl.program_id(a) == grid[a] - 1 for a in range(len(grid))])
            pl.when(first)(lambda: each("start"))
            body(*ins, *outs, *scr)
            pl.when(last)(lambda: each("finish"))
        else:
            each("start")
            body(*ins, *outs, *scr)
            each("finish")

    kw = dict(kw)
    kw["in_specs"] = in_specs + [ANY] * c_in
    kw["out_shape"] = out_shape + [_out(s.shape, s.dtype) for c in comms for s in c.outs]
    kw["out_specs"] = out_specs + [ANY] * c_out
    staging = [s for c in comms for s in c.sems if len(s) == 2 and not isinstance(s[1], int)]
    kw["scratch_shapes"] = scratch + [pltpu.VMEM(*s) if s in staging else pltpu.SemaphoreType.DMA(s) for c in comms for s in c.sems]
    kw["input_output_aliases"] = aliases
    extra = sum(_nbytes(*s) for s in staging)
    old = kw.get("compiler_params")
    limit = (old.vmem_limit_bytes if old is not None else 16 << 20) + extra
    kw["compiler_params"] = pltpu.CompilerParams(
        dimension_semantics=old.dimension_semantics if old is not None else None, vmem_limit_bytes=int(min(VMEM_CAP, limit)))
    call = pl.pallas_call(wrapped, **kw)

    def run(*args):
        res = call(*[_pin(a) for a in args], *[_pin(a) for c in comms for a in c.ins])
        o0 = n_out
        for c in comms:
            if c.then is not None:
                c.then(res[o0:o0 + len(c.outs)])
            o0 += len(c.outs)
        return res[0] if single else res[:n_out]

    return run


def _nbytes(shape, dtype):
    return int(np.prod(shape)) * jnp.dtype(dtype).itemsize


def _sigmoid(x):
    return 1.0 / (1.0 + jnp.exp(-x))


_DIMS = {"nn": (((1,), (0,)), ((), ())), "nt": (((1,), (1,)), ((), ())), "tn": (((0,), (0,)), ((), ()))}


def _mm(a, b, *, mode, out_dtype, name, tm=512, tn=None, tk=None, a_pro=None, epi=None, extra=None,
        b_chips=False, out_chips=False):
    b2 = (b.shape[1], 4 * b.shape[2]) if b_chips else b.shape
    if mode == "nn":
        (M, K), (K2, N) = a.shape, b2
    elif mode == "nt":
        (M, K), (N, K2) = a.shape, b2
    else:
        (K, M), (K2, N) = a.shape, b2
    assert K == K2, (a.shape, b.shape, mode)
    tm = min(tm, M)
    tn = N if tn is None else min(tn, N)
    tk = K if tk is None else min(tk, K)
    assert M % tm == 0 and N % tn == 0 and K % tk == 0, (M, N, K, tm, tn, tk)
    nk = K // tk
    a_blk = (tk, tm) if mode == "tn" else (tm, tk)
    a_map = (lambda i, j, k: (k, i)) if mode == "tn" else (lambda i, j, k: (i, k))
    b_blk = (tn, tk) if mode == "nt" else (tk, tn)
    b_map = (lambda i, j, k: (j, k)) if mode == "nt" else (lambda i, j, k: (k, j))
    if b_chips:
        per = b.shape[2] // b_blk[1]
        assert b.shape[2] % b_blk[1] == 0 and mode != "tn"
        flat_map = b_map
        b_map = lambda i, j, k: (flat_map(i, j, k)[1] // per, flat_map(i, j, k)[0], flat_map(i, j, k)[1] % per)
        b_blk = (None,) + b_blk
    in_specs = [pl.BlockSpec(a_blk, a_map), pl.BlockSpec(b_blk, b_map)]
    args = [a, b]
    if epi is not None:
        in_specs.append(pl.BlockSpec((tm, tn), lambda i, j, k: (i, j)))
        args.append(extra)

    def body(*refs):
        a_ref, b_ref = refs[0], refs[1]
        e_ref = refs[2] if epi is not None else None
        o_ref = refs[3] if epi is not None else refs[2]
        av = a_ref[...]
        if a_pro == "relu2":
            af = jnp.maximum(av.astype(F32), 0.0)
            av = af * af
        p = lax.dot_general(av.astype(BF16), b_ref[...].astype(BF16), _DIMS[mode], preferred_element_type=F32)

        def finish(acc):
            if epi == "add":
                acc = acc + e_ref[...].astype(F32)
            elif epi == "relu2grad":
                acc = acc * (2.0 * jnp.maximum(e_ref[...].astype(F32), 0.0))
            o_ref[...] = acc.astype(o_ref.dtype)

        if nk == 1:
            finish(p)
        else:
            acc_ref = refs[-1]
            k = pl.program_id(2)

            @pl.when(k == 0)
            def _():
                acc_ref[...] = p

            @pl.when(k > 0)
            def _():
                acc_ref[...] += p

            @pl.when(k == nk - 1)
            def _():
                finish(acc_ref[...])

    vm = 2 * (_nbytes(a_blk, a.dtype) + _nbytes((tk, tn), b.dtype) + _nbytes((tm, tn), out_dtype))
    vm += 3 * _nbytes((tm, tn), F32)
    if epi is not None:
        vm += 2 * _nbytes((tm, tn), extra.dtype)
    if out_chips:
        per_out = N // 4 // tn
        assert N % (4 * tn) == 0
        out_shape = _out((4, M, N // 4), out_dtype)
        out_spec = pl.BlockSpec((None, tm, tn), lambda i, j, k: (j // per_out, i, j % per_out))
    else:
        out_shape = _out((M, N), out_dtype)
        out_spec = pl.BlockSpec((tm, tn), lambda i, j, k: (i, j))
    return _pallas(
        body,
        out_shape=out_shape,
        grid=(M // tm, N // tn, nk),
        in_specs=in_specs,
        out_specs=out_spec,
        scratch_shapes=[pltpu.VMEM((tm, tn), F32)] if nk > 1 else [],
        compiler_params=_cparams(("parallel", "parallel", "arbitrary"), vm),
        name=name,
    )(*args)


def _mm_fan(a, bs, *, mode, out_dtypes, name, tm=512):
    M, K = a.shape
    ns = [b.shape[1] if mode == "nn" else b.shape[0] for b in bs]
    n = len(bs)

    def body(*refs):
        av = refs[0][...].astype(BF16)
        for i in range(n):
            refs[1 + n + i][...] = lax.dot_general(av, refs[1 + i][...], _DIMS[mode],
                                                   preferred_element_type=F32).astype(refs[1 + n + i].dtype)

    vm = 2 * _nbytes((tm, K), a.dtype) + sum(2 * _nbytes(b.shape, b.dtype) + 3 * _nbytes((tm, nn), F32) for b, nn in zip(bs, ns))
    return _pallas(
        body,
        out_shape=tuple(_out((M, nn), dt) for nn, dt in zip(ns, out_dtypes)),
        grid=(M // tm,),
        in_specs=[pl.BlockSpec((tm, K), lambda i: (i, 0))] + [pl.BlockSpec(b.shape, lambda i: (0, 0)) for b in bs],
        out_specs=tuple(pl.BlockSpec((tm, nn), lambda i: (i, 0)) for nn in ns),
        compiler_params=_cparams(("parallel",), vm),
        name=name,
    )(a, *bs)


def _mm_sum(as_, bs, *, mode, out_dtype, name, extra=None, tm=512):
    M = as_[0].shape[0]
    N = bs[0].shape[1] if mode == "nn" else bs[0].shape[0]
    n = len(as_)

    def body(*refs):
        acc = None
        for i in range(n):
            p = lax.dot_general(refs[i][...].astype(BF16), refs[n + i][...], _DIMS[mode], preferred_element_type=F32)
            acc = p if acc is None else acc + p
        if extra is not None:
            acc = acc + refs[2 * n][...].astype(F32)
        refs[-1][...] = acc.astype(refs[-1].dtype)

    in_specs = [pl.BlockSpec((tm, a.shape[1]), lambda i: (i, 0)) for a in as_]
    in_specs += [pl.BlockSpec(b.shape, lambda i: (0, 0)) for b in bs]
    args = list(as_) + list(bs)
    if extra is not None:
        in_specs.append(pl.BlockSpec((tm, N), lambda i: (i, 0)))
        args.append(extra)
    vm = sum(2 * _nbytes((tm, a.shape[1]), a.dtype) for a in as_) + sum(2 * _nbytes(b.shape, b.dtype) for b in bs)
    vm += 8 * _nbytes((tm, N), F32)
    return _pallas(
        body,
        out_shape=_out((M, N), out_dtype),
        grid=(M // tm,),
        in_specs=in_specs,
        out_specs=pl.BlockSpec((tm, N), lambda i: (i, 0)),
        compiler_params=_cparams(("parallel",), vm),
        name=name,
    )(*args)


def _mm_tn_multi(ops, pairs, *, name, tk=512):
    T = ops[0].shape[0]
    n, m = len(ops), len(pairs)
    shapes = [(ops[a].shape[1], ops[b].shape[1]) for a, b in pairs]

    def body(*refs):
        vals = [refs[i][...].astype(BF16) for i in range(n)]
        first = pl.program_id(0) == 0
        for j, (a, b) in enumerate(pairs):
            p = lax.dot_general(vals[a], vals[b], _DIMS["tn"], preferred_element_type=F32)
            o_ref = refs[n + j]

            @pl.when(first)
            def _(o_ref=o_ref, p=p):
                o_ref[...] = p

            @pl.when(jnp.logical_not(first))
            def _(o_ref=o_ref, p=p):
                o_ref[...] += p

    vm = sum(2 * _nbytes((tk, o.shape[1]), o.dtype) for o in ops) + sum(3 * _nbytes(s, F32) for s in shapes)
    return _pallas(
        body,
        out_shape=tuple(_out(s, F32) for s in shapes),
        grid=(T // tk,),
        in_specs=[pl.BlockSpec((tk, o.shape[1]), lambda k: (k, 0)) for o in ops],
        out_specs=tuple(pl.BlockSpec(s, lambda k: (0, 0)) for s in shapes),
        compiler_params=_cparams(("arbitrary",), vm),
        name=name,
    )(*ops)


def _rmsnorm_fwd(h, g, name, tm=512):
    T = h.shape[0]

    def body(h_ref, g_ref, o_ref):
        x = h_ref[...]
        r = lax.rsqrt(jnp.mean(x * x, axis=-1, keepdims=True) + EPS)
        o_ref[...] = (x * r * g_ref[...]).astype(o_ref.dtype)

    return _pallas(
        body,
        out_shape=_out((T, D), BF16),
        grid=(T // tm,),
        in_specs=[pl.BlockSpec((tm, D), lambda i: (i, 0)), pl.BlockSpec((1, D), lambda i: (0, 0))],
        out_specs=pl.BlockSpec((tm, D), lambda i: (i, 0)),
        compiler_params=_cparams(("parallel",), 8 * _nbytes((tm, D), F32)),
        name=name,
    )(h, g)


def _rmsnorm_bwd(dxn, h, g, dres, name, tm=512):
    T = h.shape[0]

    def body(dxn_ref, h_ref, g_ref, dres_ref, dh_ref, dg_ref):
        @pl.when(pl.program_id(0) == 0)
        def _():
            dg_ref[...] = jnp.zeros_like(dg_ref)

        x = h_ref[...]
        dy = dxn_ref[...].astype(F32)
        r = lax.rsqrt(jnp.mean(x * x, axis=-1, keepdims=True) + EPS)
        gy = dy * g_ref[...]
        dot = jnp.mean(x * gy, axis=-1, keepdims=True)
        dh_ref[...] = dres_ref[...] + r * gy - x * (r * r * r * dot)
        dg_ref[...] += jnp.sum(dy * x * r, axis=0, keepdims=True)

    row = pl.BlockSpec((tm, D), lambda i: (i, 0))
    vec = pl.BlockSpec((1, D), lambda i: (0, 0))
    return _pallas(
        body,
        out_shape=(_out((T, D), F32), _out((1, D), F32)),
        grid=(T // tm,),
        in_specs=[row, row, vec, row],
        out_specs=(row, vec),
        compiler_params=_cparams(("arbitrary",), 12 * _nbytes((tm, D), F32)),
        name=name,
    )(dxn, h, g, dres)


def _final_loss(h, g, target, name, tm=512):
    T = h.shape[0]

    def body(h_ref, g_ref, t_ref, loss_ref, dh_ref, dg_ref):
        @pl.when(pl.program_id(0) == 0)
        def _():
            dg_ref[...] = jnp.zeros_like(dg_ref)
            loss_ref[...] = jnp.zeros_like(loss_ref)

        x = h_ref[...]
        gg = g_ref[...]
        r = lax.rsqrt(jnp.mean(x * x, axis=-1, keepdims=True) + EPS)
        y = x * r * gg
        e = y - t_ref[...]
        loss_ref[...] += 0.5 * jnp.sum(jnp.mean(e * e, axis=-1, keepdims=True), axis=0, keepdims=True)
        dy = e * (1.0 / D)
        gy = dy * gg
        dot = jnp.mean(x * gy, axis=-1, keepdims=True)
        dh_ref[...] = r * gy - x * (r * r * r * dot)
        dg_ref[...] += jnp.sum(dy * x * r, axis=0, keepdims=True)

    row = pl.BlockSpec((tm, D), lambda i: (i, 0))
    vec = pl.BlockSpec((1, D), lambda i: (0, 0))
    one = pl.BlockSpec((1, 1), lambda i: (0, 0))
    return _pallas(
        body,
        out_shape=(_out((1, 1), F32), _out((T, D), F32), _out((1, D), F32)),
        grid=(T // tm,),
        in_specs=[row, vec, row],
        out_specs=(one, row, vec),
        compiler_params=_cparams(("arbitrary",), 12 * _nbytes((tm, D), F32)),
        name=name,
    )(h, g, target)


GLA_G = 8


def _gla_consts():
    i = np.arange(KW)[:, None]
    j = np.arange(VW)[None, :]
    mask = ((i // GLA_DK) == (j // GLA_DV)) & (i < GLA_H * GLA_DK)
    a = np.arange(VW)
    hm = ((a[:, None] // GLA_DV) == (a[None, :] // GLA_DV)).astype(np.float32)
    c = np.arange(CHUNK)
    low = (c[:, None] >= c[None, :]).astype(np.float32)
    return jnp.asarray(mask.astype(np.float32)), jnp.asarray(hm, BF16), jnp.asarray(low, BF16)


def _split(x):
    hi = x.astype(BF16)
    return hi, (x - hi.astype(F32)).astype(BF16)


def _dot_sel(a, b, dims, split):
    if split == "a":
        hi, lo = _split(a)
        return (lax.dot_general(hi, b, dims, preferred_element_type=F32) + lax.dot_general(lo, b, dims, preferred_element_type=F32))
    hi, lo = _split(b)
    return (lax.dot_general(a, hi, dims, preferred_element_type=F32) + lax.dot_general(a, lo, dims, preferred_element_type=F32))


def _dot3(a, b, dims):
    ah, al = _split(a)
    bh, bl = _split(b)
    return (lax.dot_general(ah, bh, dims, preferred_element_type=F32) + lax.dot_general(al, bh, dims, preferred_element_type=F32)
            + lax.dot_general(ah, bl, dims, preferred_element_type=F32))


def _dot3s(a_s, b_s, dims):
    (ah, al), (bh, bl) = a_s, b_s
    return (lax.dot_general(ah, bh, dims, preferred_element_type=F32) + lax.dot_general(al, bh, dims, preferred_element_type=F32)
            + lax.dot_general(ah, bl, dims, preferred_element_type=F32))


def _gla_group_common(p_ref, wg, bg):
    lr_s = _split(p_ref[:, 2 * KW + 2 * VW:GLA_COLS])
    wg_s = _split(wg)
    z = _dot3s(lr_s, wg_s, _DIMS["nn"]) + bg
    la = (jnp.minimum(z, 0.0) - jnp.log(1.0 + jnp.exp(-jnp.abs(z)))) * (1.0 / GLA_TAU)
    return lr_s, wg_s, z, _split(la)


def _gla_chunk_common(p_ref, rows, la_s, low, ones_v):
    q = p_ref[rows, 0:KW]
    k = p_ref[rows, KW:2 * KW]
    v = p_ref[rows, 2 * KW:2 * KW + VW]
    g = p_ref[rows, 2 * KW + VW:2 * KW + 2 * VW]
    la_h, la_l = la_s[0][rows], la_s[1][rows]
    cum = jnp.dot(low, la_h, preferred_element_type=F32) + jnp.dot(low, la_l, preferred_element_type=F32)
    endb = cum[CHUNK - 1:CHUNK, :]
    w = jnp.exp(endb - cum)
    a_full = jnp.exp(lax.dot_general(la_h, ones_v, _DIMS["tn"], preferred_element_type=F32)
                     + lax.dot_general(la_l, ones_v, _DIMS["tn"], preferred_element_type=F32))
    return q, k, v, g, w, endb, a_full


def _gla_fwd(p, wg, bg, gn, consts, name):
    T = p.shape[0]
    rb = CHUNK * GLA_G
    ng = T // rb
    mask, hm, low = consts[:3]
    scale = GLA_DK ** -0.5

    def body(p_ref, wg_ref, bg_ref, gn_ref, m_ref, hm_ref, l_ref, o_ref, st_ref, s_ref):
        @pl.when(pl.program_id(0) == 0)
        def _():
            s_ref[...] = jnp.zeros_like(s_ref)

        wg_v, bg_v, gn_v = wg_ref[...], bg_ref[...], gn_ref[...]
        ones_v = jnp.ones((CHUNK, VW), BF16)
        s_new = s_ref[...]
        _, _, _, la_s = _gla_group_common(p_ref, wg_v, bg_v)
        outs = []
        for c in range(GLA_G):
            rows = slice(c * CHUNK, (c + 1) * CHUNK)
            q, k, v, _, w, _, a_full = _gla_chunk_common(p_ref, rows, la_s, l_ref[...], ones_v)
            kd = (k * w).astype(BF16)
            kv = lax.dot_general(kd, v.astype(BF16), _DIMS["tn"], preferred_element_type=F32) * m_ref[...]
            s_new = a_full * s_new + kv
            st_ref[c] = s_new
            outs.append(jnp.dot((q * scale).astype(BF16), s_new.astype(BF16), preferred_element_type=F32))
        s_ref[...] = s_new
        o = jnp.concatenate(outs, axis=0)
        g = p_ref[:, 2 * KW + VW:2 * KW + 2 * VW]
        ms = _dot_sel(o * o, hm_ref[...], _DIMS["nn"], "a") * (1.0 / GLA_DV)
        o_ref[...] = (o * lax.rsqrt(ms + EPS) * gn_v * (g * _sigmoid(g))).astype(o_ref.dtype)

    full = lambda shape: pl.BlockSpec(shape, lambda i: tuple(0 for _ in shape))
    vm = 2 * _nbytes((rb, GLA_COLS), F32) + 2 * _nbytes((GLA_G, KW, VW), F32) + 12 * _nbytes((KW, VW), F32)
    return _pallas(
        body,
        out_shape=(_out((T, VW), BF16), _out((T // CHUNK, KW, VW), F32)),
        grid=(ng,),
        in_specs=[pl.BlockSpec((rb, GLA_COLS), lambda i: (i, 0)), full((LRW, KW)), full((1, KW)), full((1, VW)),
                  full((KW, VW)), full((VW, VW)), full((CHUNK, CHUNK))],
        out_specs=(pl.BlockSpec((rb, VW), lambda i: (i, 0)), pl.BlockSpec((GLA_G, KW, VW), lambda i: (i, 0, 0))),
        scratch_shapes=[pltpu.VMEM((KW, VW), F32)],
        compiler_params=_cparams(("arbitrary",), vm),
        name=name,
    )(p, wg, bg, gn, mask, hm, low)


def _gla_bwd(p, dy, states, wg, bg, gn, consts, name):
    T = p.shape[0]
    rb = CHUNK * GLA_G
    ng = T // rb
    mask, hm, low = consts[:3]
    scale = GLA_DK ** -0.5

    def body(p_ref, dy_ref, st_ref, sp_ref, wg_ref, bg_ref, gn_ref, m_ref, hm_ref, l_ref,
             dp_ref, dwg_ref, dbg_ref, dgn_ref, ga_ref):
        step = pl.program_id(0)

        @pl.when(step == 0)
        def _():
            ga_ref[...] = jnp.zeros_like(ga_ref)
            dwg_ref[...] = jnp.zeros_like(dwg_ref)
            dbg_ref[...] = jnp.zeros_like(dbg_ref)
            dgn_ref[...] = jnp.zeros_like(dgn_ref)

        first_group = step == ng - 1
        wg_v, bg_v, gn_v = wg_ref[...], bg_ref[...], gn_ref[...]
        ones_v = jnp.ones((CHUNK, VW), BF16)
        ones_8 = jnp.ones((8, VW), BF16)
        ga = ga_ref[...]
        lr_s, wg_s, z_all, la_s = _gla_group_common(p_ref, wg_v, bg_v)
        qss = [(p_ref[c * CHUNK:(c + 1) * CHUNK, 0:KW] * scale).astype(BF16) for c in range(GLA_G)]
        o = jnp.concatenate([jnp.dot(qss[c], st_ref[c].astype(BF16), preferred_element_type=F32) for c in range(GLA_G)], axis=0)
        g = p_ref[:, 2 * KW + VW:2 * KW + 2 * VW]
        dyv = dy_ref[...].astype(F32)
        r = lax.rsqrt(_dot_sel(o * o, hm_ref[...], _DIMS["nn"], "a") * (1.0 / GLA_DV) + EPS)
        on = o * r
        sg = _sigmoid(g)
        silu = g * sg
        d_on = dyv * gn_v * silu
        dp_ref[:, 2 * KW + VW:2 * KW + 2 * VW] = (dyv * on * gn_v * (sg * (1.0 + g * (1.0 - sg)))).astype(dp_ref.dtype)
        dgn_ref[...] += jnp.sum(dyv * on * silu, axis=0, keepdims=True)
        mo = _dot_sel(o * d_on, hm_ref[...], _DIMS["nn"], "a") * (1.0 / GLA_DV)
        dob_all = (r * d_on - o * (r * r * r) * mo).astype(BF16)
        dzs = [None] * GLA_G
        for c in reversed(range(GLA_G)):
            rows = slice(c * CHUNK, (c + 1) * CHUNK)
            _, k, v, _, w, endb, a_full = _gla_chunk_common(p_ref, rows, la_s, l_ref[...], ones_v)
            s_n = st_ref[c]
            if c > 0:
                s_prev = st_ref[c - 1]
            else:
                s_prev = jnp.where(first_group, 0.0, sp_ref[0])
            kd = k * w
            dob = dob_all[rows]
            dq = lax.dot_general(dob, s_n.astype(BF16), _DIMS["nt"], preferred_element_type=F32) * scale
            g_n = lax.dot_general(qss[c], dob, _DIMS["tn"], preferred_element_type=F32) * m_ref[...] + ga
            d_a = _dot_sel(ones_8, g_n * s_prev, _DIMS["nt"], "b")[0:1, :]
            g_nb = g_n.astype(BF16)
            dkd = lax.dot_general(v.astype(BF16), g_nb, _DIMS["nt"], preferred_element_type=F32)
            dv = jnp.dot(kd.astype(BF16), g_nb, preferred_element_type=F32)
            e = dkd * kd
            d_end = jnp.sum(e, axis=0, keepdims=True) + d_a * jnp.exp(endb)
            dla = _dot_sel(l_ref[...], -e, _DIMS["tn"], "b") + d_end
            dzs[c] = dla * (1.0 - _sigmoid(z_all[rows])) * (1.0 / GLA_TAU)
            ga = a_full * g_n
            dp_ref[rows, 0:KW] = dq.astype(dp_ref.dtype)
            dp_ref[rows, KW:2 * KW] = (dkd * w).astype(dp_ref.dtype)
            dp_ref[rows, 2 * KW:2 * KW + VW] = dv.astype(dp_ref.dtype)
        ga_ref[...] = ga
        dz = jnp.concatenate(dzs, axis=0)
        dz_s = _split(dz)
        dp_ref[:, 2 * KW + 2 * VW:GLA_COLS] = _dot3s(dz_s, wg_s, _DIMS["nt"]).astype(dp_ref.dtype)
        dwg_ref[...] += _dot3s(lr_s, dz_s, _DIMS["tn"])
        dbg_ref[...] += jnp.sum(dz, axis=0, keepdims=True)

    full = lambda shape: pl.BlockSpec(shape, lambda i: tuple(0 for _ in shape))
    rev = lambda i: (ng - 1 - i, 0)
    vm = 4 * _nbytes((rb, GLA_COLS), F32) + 2 * _nbytes((rb, VW), F32) + 2 * _nbytes((GLA_G + 1, KW, VW), F32)
    vm += 16 * _nbytes((KW, VW), F32)
    return _pallas(
        body,
        out_shape=(_out((T, GLA_COLS), BF16), _out((LRW, KW), F32),
                   _out((1, KW), F32), _out((1, VW), F32)),
        grid=(ng,),
        in_specs=[pl.BlockSpec((rb, GLA_COLS), rev), pl.BlockSpec((rb, VW), rev),
                  pl.BlockSpec((GLA_G, KW, VW), lambda i: (ng - 1 - i, 0, 0)),
                  pl.BlockSpec((1, KW, VW), lambda i: (jnp.maximum((ng - 1 - i) * GLA_G - 1, 0), 0, 0)),
                  full((LRW, KW)), full((1, KW)), full((1, VW)), full((KW, VW)), full((VW, VW)), full((CHUNK, CHUNK))],
        out_specs=(pl.BlockSpec((rb, GLA_COLS), rev), full((LRW, KW)), full((1, KW)), full((1, VW))),
        scratch_shapes=[pltpu.VMEM((KW, VW), F32)],
        compiler_params=_cparams(("arbitrary",), vm),
        name=name,
    )(p, dy, states, states, wg, bg, gn, mask, hm, low)


CONV_TM = 512
HALO = 32
CONV_RB = 64


def _glu(u):
    a = u[:, 0:CW]
    b = u[:, CW:2 * CW]
    return a * _sigmoid(b)


def _conv_taps(buf_ref, w_ref, rb0, first_tap):
    acc = jnp.zeros((CONV_RB, CW), F32)
    for j in range(CK):
        s = rb0 + first_tap(j)
        acc = acc + w_ref[j:j + 1, :] * buf_ref[s:s + CONV_RB, :]
    return acc


def _ln_fwd(c, lg, lb):
    mu = jnp.mean(c, axis=-1, keepdims=True)
    xc = c - mu
    rstd = lax.rsqrt(jnp.mean(xc * xc, axis=-1, keepdims=True) + EPS)
    n = xc * rstd
    return n, rstd, n * lg + lb


def _conv_fwd(u, w, b, lg, lb, name):
    T = u.shape[0]
    tm = CONV_TM

    def body(u_ref, uh_ref, w_ref, b_ref, lg_ref, lb_ref, o_ref, c_ref, hbuf):
        i = pl.program_id(0)
        hbuf[0:HALO, :] = jnp.where(i > 0, _glu(uh_ref[...]), 0.0)
        hbuf[HALO:HALO + tm, :] = _glu(u_ref[...])
        for r in range(tm // CONV_RB):
            rows = slice(r * CONV_RB, (r + 1) * CONV_RB)
            acc = _conv_taps(hbuf, w_ref, r * CONV_RB, lambda j: HALO - (CK - 1) + j)
            c_ref[rows, :] = acc
            _, _, zz = _ln_fwd(acc + b_ref[...], lg_ref[...], lb_ref[...])
            o_ref[rows, :] = (zz * _sigmoid(zz)).astype(o_ref.dtype)

    vec = pl.BlockSpec((1, CW), lambda i: (0, 0))
    return _pallas(
        body,
        out_shape=(_out((T, CW), BF16), _out((T, CW), F32)),
        grid=(T // tm,),
        in_specs=[pl.BlockSpec((tm, CONV_COLS), lambda i: (i, 0)),
                  pl.BlockSpec((HALO, CONV_COLS), lambda i: (jnp.maximum(i * (tm // HALO) - 1, 0), 0)),
                  pl.BlockSpec((HALO, CW), lambda i: (0, 0)), vec, vec, vec],
        out_specs=(pl.BlockSpec((tm, CW), lambda i: (i, 0)), pl.BlockSpec((tm, CW), lambda i: (i, 0))),
        scratch_shapes=[pltpu.VMEM((tm + HALO, CW), F32)],
        compiler_params=_cparams(("arbitrary",), 8 * _nbytes((tm, CONV_COLS), F32)),
        name=name,
    )(u, u, w, b, lg, lb)


def _conv_bwd_dc(u, conv, dout, b, lg, lb, name):
    T = u.shape[0]
    tm = CONV_TM
    nsteps = T // tm

    def body(u_ref, uh_ref, c_ref, do_ref, b_ref, lg_ref, lb_ref, dc_ref, dw_ref, db_ref, dlg_ref, dlb_ref, hbuf, dwacc):
        i = pl.program_id(0)

        @pl.when(i == 0)
        def _():
            dwacc[...] = jnp.zeros_like(dwacc)
            db_ref[...] = jnp.zeros_like(db_ref)
            dlg_ref[...] = jnp.zeros_like(dlg_ref)
            dlb_ref[...] = jnp.zeros_like(dlb_ref)

        hbuf[0:HALO, :] = jnp.where(i > 0, _glu(uh_ref[...]), 0.0)
        hbuf[HALO:HALO + tm, :] = _glu(u_ref[...])
        for r in range(tm // CONV_RB):
            rows = slice(r * CONV_RB, (r + 1) * CONV_RB)
            n, rstd, zz = _ln_fwd(c_ref[rows, :] + b_ref[...], lg_ref[...], lb_ref[...])
            sg = _sigmoid(zz)
            dz = do_ref[rows, :].astype(F32) * (sg * (1.0 + zz * (1.0 - sg)))
            dlg_ref[...] += jnp.sum(dz * n, axis=0, keepdims=True)
            dlb_ref[...] += jnp.sum(dz, axis=0, keepdims=True)
            dn = dz * lg_ref[...]
            dc = rstd * (dn - jnp.mean(dn, axis=-1, keepdims=True) - n * jnp.mean(dn * n, axis=-1, keepdims=True))
            dc_ref[rows, :] = dc
            db_ref[...] += jnp.sum(dc, axis=0, keepdims=True)
            for j in range(CK):
                s = r * CONV_RB + HALO - (CK - 1) + j
                prod = dc * hbuf[s:s + CONV_RB, :]
                dwacc[j] += jnp.sum(prod.reshape(CONV_RB // 8, 8, CW), axis=0)

        @pl.when(i == nsteps - 1)
        def _():
            dw_ref[...] = jnp.sum(dwacc[...], axis=1)

    vec = pl.BlockSpec((1, CW), lambda i: (0, 0))
    return _pallas(
        body,
        out_shape=(_out((T, CW), F32), _out((HALO, CW), F32),
                   _out((1, CW), F32), _out((1, CW), F32), _out((1, CW), F32)),
        grid=(nsteps,),
        in_specs=[pl.BlockSpec((tm, CONV_COLS), lambda i: (i, 0)),
                  pl.BlockSpec((HALO, CONV_COLS), lambda i: (jnp.maximum(i * (tm // HALO) - 1, 0), 0)),
                  pl.BlockSpec((tm, CW), lambda i: (i, 0)), pl.BlockSpec((tm, CW), lambda i: (i, 0)), vec, vec, vec],
        out_specs=(pl.BlockSpec((tm, CW), lambda i: (i, 0)), pl.BlockSpec((HALO, CW), lambda i: (0, 0)), vec, vec, vec),
        scratch_shapes=[pltpu.VMEM((tm + HALO, CW), F32), pltpu.VMEM((HALO, 8, CW), F32)],
        compiler_params=_cparams(("arbitrary",), 10 * _nbytes((tm, CONV_COLS), F32)),
        name=name,
    )(u, u, conv, dout, b, lg, lb)


def _conv_bwd_du(u, dc, w, name):
    T = u.shape[0]
    tm = CONV_TM
    nsteps = T // tm

    def body(u_ref, dc_ref, dch_ref, w_ref, du_ref, dcbuf):
        i = pl.program_id(0)
        dcbuf[0:tm, :] = dc_ref[...]
        dcbuf[tm:tm + HALO, :] = jnp.where(i < nsteps - 1, dch_ref[...], 0.0)
        for r in range(tm // CONV_RB):
            rows = slice(r * CONV_RB, (r + 1) * CONV_RB)
            dh = _conv_taps(dcbuf, w_ref, r * CONV_RB, lambda j: (CK - 1) - j)
            a = u_ref[rows, 0:CW]
            sb = _sigmoid(u_ref[rows, CW:2 * CW])
            du_ref[rows, 0:CW] = (dh * sb).astype(du_ref.dtype)
            du_ref[rows, CW:2 * CW] = (dh * a * sb * (1.0 - sb)).astype(du_ref.dtype)

    return _pallas(
        body,
        out_shape=_out((T, CONV_COLS), BF16),
        grid=(nsteps,),
        in_specs=[pl.BlockSpec((tm, CONV_COLS), lambda i: (i, 0)),
                  pl.BlockSpec((tm, CW), lambda i: (i, 0)),
                  pl.BlockSpec((HALO, CW), lambda i: (jnp.minimum((i + 1) * (tm // HALO), T // HALO - 1), 0)),
                  pl.BlockSpec((HALO, CW), lambda i: (0, 0))],
        out_specs=pl.BlockSpec((tm, CONV_COLS), lambda i: (i, 0)),
        scratch_shapes=[pltpu.VMEM((tm + HALO, CW), F32)],
        compiler_params=_cparams(("arbitrary",), 8 * _nbytes((tm, CONV_COLS), F32)),
        name=name,
    )(u, dc, dc, w)


ATT_G = 4


def _att_load_kv(p_any, kbuf, vbuf, sems, T):
    kc = pltpu.make_async_copy(p_any.at[:, pl.ds(AW, AW)], kbuf.at[pl.ds(LEFT, T), :], sems.at[0])
    vc = pltpu.make_async_copy(p_any.at[:, pl.ds(2 * AW, AW)], vbuf.at[pl.ds(LEFT, T), :], sems.at[1])
    kc.start()
    vc.start()
    kbuf[0:LEFT, :] = jnp.zeros((LEFT, AW), BF16)
    vbuf[0:LEFT, :] = jnp.zeros((LEFT, AW), BF16)
    kc.wait()
    vc.wait()


ATT_QB = CHUNK * ATT_G
ATT_KB = LEFT + ATT_QB
REL_PAD = 384
TOEP = 1024


def _att_consts():
    m = np.arange(TOEP)
    d = ATT_KB - 1 - m
    idx = np.clip(d, -128, 128) + 128
    sel = (np.arange(REL_PAD)[:, None] == idx[None, :]) & (m[None, :] < ATT_QB + ATT_KB - 1)
    return jnp.asarray(sel.astype(np.float32))


def _att_build_bias(rel_ref, sel_ref, bias_scr):
    tr = jnp.dot(rel_ref[...], sel_ref[...], precision=HI, preferred_element_type=F32)
    qc = lax.broadcasted_iota(jnp.int32, (ATT_QB, ATT_KB), 0) // CHUNK
    kc = lax.broadcasted_iota(jnp.int32, (ATT_QB, ATT_KB), 1) // CHUNK
    band = (kc >= qc) & (kc <= qc + 8)
    for h in range(AH):
        rows = jnp.broadcast_to(tr[h:h + 1, :], (ATT_QB, TOEP))
        toep = pltpu.roll(rows, TOEP - (ATT_QB - 1), 1, stride=1, stride_axis=0)[:, 0:ATT_KB]
        bias_scr[h // 2, (h % 2) * ATT_QB:(h % 2 + 1) * ATT_QB, :] = jnp.where(band, toep, NEG)


def _att_probs(qst, kb, bias_p, n0):
    sc = lax.dot_general(qst, kb, _DIMS["nt"], preferred_element_type=F32) * (64 ** -0.5) + bias_p
    pos = lax.broadcasted_iota(jnp.int32, (2 * ATT_QB, ATT_KB), 1)
    sc = jnp.where(pos >= CHUNK * (8 - n0), sc, NEG)
    mx = jnp.max(sc, axis=-1, keepdims=True)
    ex = jnp.exp(sc - mx)
    return ex / jnp.sum(ex, axis=-1, keepdims=True)


def _head_stack(a2, lo):
    zero = jnp.zeros_like(a2)
    return jnp.concatenate([jnp.where(lo, a2, zero), jnp.where(lo, zero, a2)], axis=0)


def _att_fwd(p, rel, sel, name):
    T = p.shape[0]

    def body(q_ref, p_any, rel_ref, sel_ref, o_ref, kbuf, vbuf, bias_scr, sems):
        i = pl.program_id(0)

        @pl.when(i == 0)
        def _():
            _att_load_kv(p_any, kbuf, vbuf, sems, T)
            _att_build_bias(rel_ref, sel_ref, bias_scr)

        lo = lax.broadcasted_iota(jnp.int32, (ATT_QB, 128), 1) < 64
        n0 = i * ATT_G
        start = pl.multiple_of(i * ATT_QB, ATT_QB)
        for hp in range(AH // 2):
            cols = slice(hp * 128, (hp + 1) * 128)
            kb = kbuf[pl.ds(start, ATT_KB), cols]
            vb = vbuf[pl.ds(start, ATT_KB), cols]
            pr = _att_probs(_head_stack(q_ref[:, cols], lo), kb, bias_scr[hp], n0)
            pv = jnp.dot(pr.astype(BF16), vb, preferred_element_type=F32)
            o_ref[:, cols] = jnp.where(lo, pv[0:ATT_QB], pv[ATT_QB:2 * ATT_QB]).astype(o_ref.dtype)

    vm = 2 * _nbytes((T + LEFT, AW), BF16) + 8 * _nbytes((2 * ATT_QB, ATT_KB), F32) + (8 << 20)
    return _pallas(
        body,
        out_shape=_out((T, AW), BF16),
        grid=(T // ATT_QB,),
        in_specs=[pl.BlockSpec((ATT_QB, AW), lambda i: (i, 0)), pl.BlockSpec(memory_space=pl.ANY),
                  pl.BlockSpec((8, REL_PAD), lambda i: (0, 0)), pl.BlockSpec((REL_PAD, TOEP), lambda i: (0, 0))],
        out_specs=pl.BlockSpec((ATT_QB, AW), lambda i: (i, 0)),
        scratch_shapes=[pltpu.VMEM((T + LEFT, AW), BF16), pltpu.VMEM((T + LEFT, AW), BF16),
                        pltpu.VMEM((AH // 2, 2 * ATT_QB, ATT_KB), F32), pltpu.SemaphoreType.DMA((2,))],
        compiler_params=_cparams(("arbitrary",), vm),
        name=name,
    )(p, p, rel, sel)


def _att_bwd(p, do, rel, sel, name):
    T = p.shape[0]
    nsteps = T // ATT_QB

    def body(q_ref, p_any, do_ref, rel_ref, sel_ref, dp_any, drel_ref,
             kbuf, vbuf, dqbuf, dkbuf, dvbuf, bias_scr, dbias_scr, dtr_scr, sems):
        i = pl.program_id(0)

        @pl.when(i == 0)
        def _():
            _att_load_kv(p_any, kbuf, vbuf, sems, T)
            _att_build_bias(rel_ref, sel_ref, bias_scr)
            dkbuf[...] = jnp.zeros_like(dkbuf)
            dvbuf[...] = jnp.zeros_like(dvbuf)
            dbias_scr[...] = jnp.zeros_like(dbias_scr)

        lo = lax.broadcasted_iota(jnp.int32, (ATT_QB, 128), 1) < 64
        n0 = i * ATT_G
        start = pl.multiple_of(i * ATT_QB, ATT_QB)
        for hp in range(AH // 2):
            cols = slice(hp * 128, (hp + 1) * 128)
            kb = kbuf[pl.ds(start, ATT_KB), cols]
            vb = vbuf[pl.ds(start, ATT_KB), cols]
            qst = _head_stack(q_ref[:, cols], lo)
            dost = _head_stack(do_ref[:, cols].astype(BF16), lo)
            pr = _att_probs(qst, kb, bias_scr[hp], n0)
            dpr = lax.dot_general(dost, vb, _DIMS["nt"], preferred_element_type=F32)
            ds = pr * (dpr - jnp.sum(dpr * pr, axis=-1, keepdims=True))
            dbias_scr[hp] += ds
            dsb = (ds * (64 ** -0.5)).astype(BF16)
            dq = jnp.dot(dsb, kb, preferred_element_type=F32)
            dqbuf[pl.ds(start, ATT_QB), cols] = jnp.where(lo, dq[0:ATT_QB], dq[ATT_QB:2 * ATT_QB]).astype(BF16)
            dkbuf[pl.ds(start, ATT_KB), cols] += lax.dot_general(dsb, qst, _DIMS["tn"], preferred_element_type=F32)
            dvbuf[pl.ds(start, ATT_KB), cols] += lax.dot_general(pr.astype(BF16), dost, _DIMS["tn"], preferred_element_type=F32)

        @pl.when(i == nsteps - 1)
        def _():
            kbuf[pl.ds(LEFT, T), :] = dkbuf[pl.ds(LEFT, T), :].astype(BF16)
            vbuf[pl.ds(LEFT, T), :] = dvbuf[pl.ds(LEFT, T), :].astype(BF16)
            cps = [pltpu.make_async_copy(dqbuf, dp_any.at[:, pl.ds(0, AW)], sems.at[0]),
                   pltpu.make_async_copy(kbuf.at[pl.ds(LEFT, T), :], dp_any.at[:, pl.ds(AW, AW)], sems.at[1]),
                   pltpu.make_async_copy(vbuf.at[pl.ds(LEFT, T), :], dp_any.at[:, pl.ds(2 * AW, AW)], sems.at[2])]
            for cp in cps:
                cp.start()
            dtr_scr[...] = jnp.zeros_like(dtr_scr)
            ri = lax.broadcasted_iota(jnp.int32, (ATT_QB, ATT_QB), 0)
            ci = lax.broadcasted_iota(jnp.int32, (ATT_QB, ATT_QB), 1)
            flip = jnp.where(ri + ci == ATT_QB - 1, 1.0, 0.0)
            for h in range(AH):
                db = dbias_scr[h // 2, (h % 2) * ATT_QB:(h % 2 + 1) * ATT_QB, :]
                db = jnp.dot(flip, db, precision=HI, preferred_element_type=F32)
                wide = jnp.concatenate([db, jnp.zeros((ATT_QB, TOEP - ATT_KB), F32)], axis=1)
                diag = pltpu.roll(wide, 0, 1, stride=1, stride_axis=0)
                dtr_scr[h:h + 1, :] = jnp.sum(diag, axis=0, keepdims=True)
            drel_ref[...] = lax.dot_general(dtr_scr[...], sel_ref[...], _DIMS["nt"], precision=HI, preferred_element_type=F32)
            for cp in cps:
                cp.wait()

    vm = 3 * _nbytes((T + LEFT, AW), BF16) + 2 * _nbytes((T + LEFT, AW), F32) + 12 * _nbytes((2 * ATT_QB, ATT_KB), F32) + (8 << 20)
    return _pallas(
        body,
        out_shape=(_out((T, ATT_COLS), BF16), _out((8, REL_PAD), F32)),
        grid=(nsteps,),
        in_specs=[pl.BlockSpec((ATT_QB, AW), lambda i: (i, 0)), pl.BlockSpec(memory_space=pl.ANY),
                  pl.BlockSpec((ATT_QB, AW), lambda i: (i, 0)),
                  pl.BlockSpec((8, REL_PAD), lambda i: (0, 0)), pl.BlockSpec((REL_PAD, TOEP), lambda i: (0, 0))],
        out_specs=(pl.BlockSpec(memory_space=pl.ANY), pl.BlockSpec((8, REL_PAD), lambda i: (0, 0))),
        scratch_shapes=[pltpu.VMEM((T + LEFT, AW), BF16), pltpu.VMEM((T + LEFT, AW), BF16), pltpu.VMEM((T, AW), BF16),
                        pltpu.VMEM((T + LEFT, AW), F32), pltpu.VMEM((T + LEFT, AW), F32),
                        pltpu.VMEM((AH // 2, 2 * ATT_QB, ATT_KB), F32), pltpu.VMEM((AH // 2, 2 * ATT_QB, ATT_KB), F32),
                        pltpu.VMEM((8, TOEP), F32), pltpu.SemaphoreType.DMA((3,))],
        compiler_params=_cparams(("arbitrary",), vm),
        name=name,
    )(p, p, do, rel, sel)


def _layer_fwd(h, wl, consts, tag):
    xn = _rmsnorm_fwd(h, wl["norm_mix"], f"{tag}_norm_mix")
    p_gla, p_conv, p_att = _mm_fan(xn, [wl["w_gla"], wl["w_conv"], wl["w_att"]], mode="nt", out_dtypes=(F32, F32, BF16),
                                   name=f"{tag}_proj")
    o_gla, states = _gla_fwd(p_gla, wl["wg"], wl["bg"], wl["gla_norm"], consts, f"{tag}_gla_fwd")
    o_conv, conv = _conv_fwd(p_conv, wl["w_dw"], wl["b_dw"], wl["ln_g"], wl["ln_b"], f"{tag}_conv_fwd")
    rel = jnp.pad(wl["rel_bias"], ((0, 8 - AH), (0, REL_PAD - N_REL)))
    o_att = _att_fwd(p_att, rel, consts[3], f"{tag}_att_fwd")
    h1 = _mm_sum([o_gla, o_conv, o_att], [wl["w_out_g"], wl["w_out_c"], wl["w_out_a"]], mode="nn", out_dtype=F32, extra=h,
                 name=f"{tag}_out")
    xn2 = _rmsnorm_fwd(h1, wl["norm_ffn"], f"{tag}_norm_ffn")
    u = _mm(xn2, wl["w_up"], mode="nn", out_dtype=BF16, tm=1024, tn=1024, b_chips=True, name=f"{tag}_mlp_up")
    h2 = _mm(u, wl["w_down"], mode="nn", out_dtype=F32, tm=1024, tk=2048, a_pro="relu2", epi="add", extra=h1,
             name=f"{tag}_mlp_down")
    saved = dict(h=h, xn=xn, p_gla=p_gla, p_conv=p_conv, p_att=p_att, states=states, o_gla=o_gla, o_conv=o_conv, conv=conv,
                 o_att=o_att, rel=rel, h1=h1, xn2=xn2, u=u)
    return h2, saved


def _layer_bwd(dh2, sv, wl, consts, tag, emit=lambda name, grad: None):
    g = {}
    du = _mm(dh2, wl["w_down"], mode="nt", out_dtype=BF16, tm=1024, tn=1024, epi="relu2grad", extra=sv["u"],
             name=f"{tag}_mlp_down_dx")
    g["w_down"] = _mm(sv["u"], dh2, mode="tn", out_dtype=F32, tm=2048, tn=1024, tk=512, a_pro="relu2", name=f"{tag}_mlp_down_dw")
    emit("w_down", g["w_down"].reshape(4, D_FF // 4, D))
    dxn2 = _mm(du, wl["w_up"], mode="nt", out_dtype=F32, tm=1024, tk=1024, b_chips=True, name=f"{tag}_mlp_up_dx")
    g["w_up"] = _mm(sv["xn2"], du, mode="tn", out_dtype=F32, tm=1024, tn=1024, tk=1024, out_chips=True, name=f"{tag}_mlp_up_dw")
    emit("w_up", g["w_up"])
    dh1, g["norm_ffn"] = _rmsnorm_bwd(dxn2, sv["h1"], wl["norm_ffn"], dh2, f"{tag}_norm_ffn_bwd")
    d_gla, d_conv, d_att = _mm_fan(dh1, [wl["w_out_g"], wl["w_out_c"], wl["w_out_a"]], mode="nt", out_dtypes=(F32, F32, F32),
                                   name=f"{tag}_out_dx")
    g["w_out_g"], g["w_out_c"], g["w_out_a"] = _mm_tn_multi([sv["o_gla"], sv["o_conv"], sv["o_att"], dh1],
                                                            [(0, 3), (1, 3), (2, 3)], name=f"{tag}_out_dw")
    emit("w_out", jnp.concatenate([g["w_out_g"], g["w_out_c"], g["w_out_a"]], axis=0).reshape(4, D // 4, D))
    dp_gla, g["wg"], g["bg"], g["gla_norm"] = _gla_bwd(sv["p_gla"], d_gla, sv["states"], wl["wg"], wl["bg"], wl["gla_norm"],
                                                       consts, f"{tag}_gla_bwd")
    dc, g["w_dw"], g["b_dw"], g["ln_g"], g["ln_b"] = _conv_bwd_dc(sv["p_conv"], sv["conv"], d_conv, wl["b_dw"], wl["ln_g"],
                                                                  wl["ln_b"], f"{tag}_conv_bwd_dc")
    dp_conv = _conv_bwd_du(sv["p_conv"], dc, wl["w_dw"], f"{tag}_conv_bwd_du")
    dp_att, drel = _att_bwd(sv["p_att"], d_att, sv["rel"], consts[3], f"{tag}_att_bwd")
    g["rel_bias"] = drel[0:AH, 0:N_REL]
    g["w_gla"], g["w_conv"], g["w_att"] = _mm_tn_multi([sv["xn"], dp_gla, dp_conv, dp_att], [(1, 0), (2, 0), (3, 0)],
                                                       name=f"{tag}_proj_dw")
    emit("w_in", _join_w_in_t(g))
    dxn = _mm_sum([dp_gla, dp_conv, dp_att], [wl["w_gla"], wl["w_conv"], wl["w_att"]], mode="nn", out_dtype=F32,
                  name=f"{tag}_proj_dx")
    dh, g["norm_mix"] = _rmsnorm_bwd(dxn, sv["h"], wl["norm_mix"], dh1, f"{tag}_norm_mix_bwd")
    return dh, g


def _local_step(x, target, layers, norm_final, emit=lambda layer, name, grad: None):
    consts = _gla_consts() + (_att_consts(),)
    h = x
    saved = []
    for l, wl in enumerate(layers):
        h, sv = _layer_fwd(h, wl, consts, f"l{l}")
        saved.append(sv)
    loss, dh, g_final = _final_loss(h, norm_final, target, "final_loss")
    grads = [None] * len(layers)
    for l in reversed(range(len(layers))):
        dh, grads[l] = _layer_bwd(dh, saved[l], layers[l], consts, f"l{l}", functools.partial(emit, l))
    return loss, dh, grads, g_final


ANY = pl.BlockSpec(memory_space=pl.ANY)


def _place():
    x, y, c = lax.axis_index("x"), lax.axis_index("y"), lax.axis_index("c")
    chips = [(1 - x, y), (x, 1 - y), (1 - x, 1 - y)]
    return x, y, c, chips


def _shape(shape, dtype):
    return jax.ShapeDtypeStruct(tuple(shape), dtype)


def _remote(src, dst, send_sem, recv_sem, to):
    return pltpu.make_async_remote_copy(src_ref=src, dst_ref=dst, send_sem=send_sem, recv_sem=recv_sem,
                                        device_id=to, device_id_type=MESH)


class _Staged:
    def __init__(self, src, dst, buf, sems):
        self.load = pltpu.make_async_copy(src, buf, sems.at[0])
        self.store = pltpu.make_async_copy(buf, dst, sems.at[1])

    def start(self):
        self.load.start()

    def wait(self):
        self.load.wait()
        self.store.start()
        self.store.wait()


def _phase_gather_ici(src, then):
    R, C = src.shape
    rh = R // 2

    def copies(ins, outs, sems):
        x, y, c, chips = _place()
        me = 2 * x + y
        local = _Staged(ins[0], outs[0].at[me], sems[3], sems[2])
        sends = [_remote(ins[0].at[pl.ds(c * rh, rh), :], outs[0].at[me, pl.ds(c * rh, rh), :], sems[0].at[k], sems[1].at[k],
                         (px, py, c)) for k, (px, py) in enumerate(chips)]
        recvs = [_remote(outs[0].at[2 * px + py, pl.ds(c * rh, rh), :], outs[0].at[2 * px + py, pl.ds(c * rh, rh), :],
                         sems[0].at[k], sems[1].at[k], (px, py, c)) for k, (px, py) in enumerate(chips)]
        return local, sends, recvs

    def start(ins, outs, sems):
        local, sends, _ = copies(ins, outs, sems)
        local.start()
        for cp in sends:
            cp.start()

    def finish(ins, outs, sems):
        local, sends, recvs = copies(ins, outs, sems)
        for cp in recvs:
            cp.wait_recv()
        for cp in sends:
            cp.wait_send()
        local.wait()

    return _Comm([src], [_shape((4, R, C), src.dtype)], {}, [(3,), (3,), (2,), ((R, C), src.dtype)], start, finish, then)


def _phase_gather_d2d(part, then):
    _, R, C = part.shape
    rh = R // 2

    def copies(ins, outs, sems):
        x, y, c, chips = _place()
        sends = [_remote(ins[0].at[2 * px + py, pl.ds(c * rh, rh), :], outs[0].at[2 * px + py, pl.ds(c * rh, rh), :],
                         sems[0].at[k], sems[1].at[k], (x, y, 1 - c)) for k, (px, py) in enumerate(chips)]
        recvs = [_remote(outs[0].at[2 * px + py, pl.ds((1 - c) * rh, rh), :], outs[0].at[2 * px + py, pl.ds((1 - c) * rh, rh), :],
                         sems[0].at[k], sems[1].at[k], (x, y, 1 - c)) for k, (px, py) in enumerate(chips)]
        return sends, recvs

    def start(ins, outs, sems):
        for cp in copies(ins, outs, sems)[0]:
            cp.start()

    def finish(ins, outs, sems):
        sends, recvs = copies(ins, outs, sems)
        for cp in recvs:
            cp.wait_recv()
        for cp in sends:
            cp.wait_send()

    return _Comm([part], [_shape(part.shape, part.dtype)], {0: 0}, [(3,), (3,)], start, finish, then)


def _phase_pair_exchange(full, then):
    _, R, C = full.shape
    rh = R // 2

    def copy(ins, outs, sems):
        x, y, c, _ = _place()
        return _remote(ins[0].at[:, pl.ds((1 - c) * rh, rh), :], outs[0], sems[0].at[0], sems[1].at[0], (x, y, 1 - c))

    return _Comm([full], [_shape((4, rh, C), full.dtype)], {}, [(1,), (1,)],
                 lambda ins, outs, sems: copy(ins, outs, sems).start(),
                 lambda ins, outs, sems: copy(ins, outs, sems).wait(), then)


def _phase_chip_scatter(parts, then):
    def copies(ins, outs, sems):
        x, y, c, chips = _place()
        me = 2 * x + y
        local = _Staged(ins[0].at[me], outs[0].at[me], sems[3], sems[2])
        sends = [_remote(ins[0].at[2 * px + py], outs[0].at[me], sems[0].at[k], sems[1].at[k], (px, py, c))
                 for k, (px, py) in enumerate(chips)]
        recvs = [_remote(outs[0].at[2 * px + py], outs[0].at[2 * px + py], sems[0].at[k], sems[1].at[k], (px, py, c))
                 for k, (px, py) in enumerate(chips)]
        return local, sends, recvs

    def start(ins, outs, sems):
        local, sends, _ = copies(ins, outs, sems)
        local.start()
        for cp in sends:
            cp.start()

    def finish(ins, outs, sems):
        local, sends, recvs = copies(ins, outs, sems)
        for cp in recvs:
            cp.wait_recv()
        for cp in sends:
            cp.wait_send()
        local.wait()

    return _Comm([parts], [_shape(parts.shape, parts.dtype)], {}, [(3,), (3,), (2,), (parts.shape[1:], parts.dtype)],
                 start, finish, then)


def _phase_pair_allgather(half, layer, depth, into, then):
    rh, C = half.shape

    def copies(ins, outs, sems):
        x, y, c, _ = _place()
        mine = outs[0].at[layer, pl.ds(c * rh, rh), :]
        theirs = outs[0].at[layer, pl.ds((1 - c) * rh, rh), :]
        return (_Staged(ins[0], mine, sems[3], sems[2]),
                _remote(ins[0], mine, sems[0].at[0], sems[1].at[0], (x, y, 1 - c)),
                _remote(theirs, theirs, sems[0].at[0], sems[1].at[0], (x, y, 1 - c)))

    def start(ins, outs, sems):
        local, send, _ = copies(ins, outs, sems)
        local.start()
        send.start()

    def finish(ins, outs, sems):
        local, send, recv = copies(ins, outs, sems)
        recv.wait_recv()
        send.wait_send()
        local.wait()

    ins = [half] if into is None else [half, into]
    return _Comm(ins, [_shape((depth, 2 * rh, C), half.dtype)], {} if into is None else {1: 0},
                 [(1,), (1,), (2,), ((rh, C), half.dtype)], start, finish, then)


def _comm_only(comms, name):
    plan = _Plan()
    for c in comms:
        plan.at(name, c)
    saved, _PLAN[0] = _PLAN[0], plan
    try:
        def body(o_ref):
            o_ref[...] = jnp.zeros_like(o_ref)

        _pallas(body, out_shape=[jax.ShapeDtypeStruct((8, 128), F32)], in_specs=[],
                out_specs=[pl.BlockSpec(memory_space=pltpu.VMEM)], name=name)()
    finally:
        _PLAN[0] = saved


def _row_tile(rows, cols, itemsize=4, budget=1 << 20, mult=8):
    fits = [t for t in range(mult, rows + 1, mult) if rows % t == 0 and t * cols * itemsize <= budget]
    return max(fits) if fits else rows


def _pair_add(full, got, c, name):
    _, L, R, C = full.shape
    rh = R // 2
    tr = _row_tile(rh, C, budget=2 << 20, mult=16)
    nb = rh // tr

    def body(c_ref, a_ref, b_ref, o_ref):
        o_ref[...] = (a_ref[...] + b_ref[...]).astype(o_ref.dtype)

    grid_spec = pltpu.PrefetchScalarGridSpec(
        num_scalar_prefetch=1,
        grid=(4, L, nb),
        in_specs=[pl.BlockSpec((1, 1, tr, C), lambda j, l, i, c_ref: (j, l, c_ref[0] * nb + i, 0)),
                  pl.BlockSpec((1, 1, tr, C), lambda j, l, i, c_ref: (j, l, i, 0))],
        out_specs=pl.BlockSpec((1, 1, tr, C), lambda j, l, i, c_ref: (j, l, i, 0)),
    )
    return _pallas(
        body,
        out_shape=_out((4, L, rh, C), BF16),
        grid_spec=grid_spec,
        compiler_params=_cparams(("parallel", "parallel", "parallel"), 8 * tr * C * 4),
        name=name,
    )(jnp.reshape(c, (1,)).astype(jnp.int32), full, got)


def _sum_chips(parts, name):
    _, L, rh, C = parts.shape
    tr = _row_tile(rh, C, budget=2 << 20, mult=16)

    def body(p_ref, o_ref):
        acc = p_ref[0].astype(F32)
        for j in range(1, 4):
            acc = acc + p_ref[j].astype(F32)
        o_ref[...] = acc

    return _pallas(
        body,
        out_shape=_out((L, rh, C), F32),
        grid=(L, rh // tr),
        in_specs=[pl.BlockSpec((4, 1, tr, C), lambda l, i: (0, l, i, 0))],
        out_specs=pl.BlockSpec((1, tr, C), lambda l, i: (l, i, 0)),
        compiler_params=_cparams(("parallel", "parallel"), 16 * tr * C * 4),
        name=name,
    )(parts)


def _allreduce_small(v):
    R = v.shape[0]

    def body(v_ref, o_ref, slots, send_sems, recv_sems):
        x, y, c, _ = _place()
        me = 4 * x + 2 * y + c
        slots[me] = v_ref[...]
        cps = []
        for r in range(1, 8):
            px, py, pc = x ^ (r >> 2), y ^ ((r >> 1) & 1), c ^ (r & 1)
            cps.append(pltpu.make_async_remote_copy(src_ref=v_ref, dst_ref=slots.at[me], send_sem=send_sems.at[r - 1],
                                                    recv_sem=recv_sems.at[r - 1], device_id=(px, py, pc), device_id_type=MESH))
            cps[-1].start()
        for r in range(1, 8):
            px, py, pc = x ^ (r >> 2), y ^ ((r >> 1) & 1), c ^ (r & 1)
            theirs = slots.at[4 * px + 2 * py + pc]
            pltpu.make_async_remote_copy(src_ref=theirs, dst_ref=theirs, send_sem=send_sems.at[r - 1], recv_sem=recv_sems.at[r - 1],
                                         device_id=(px, py, pc), device_id_type=MESH).wait_recv()
        acc = slots[0]
        for j in range(1, 8):
            acc = acc + slots[j]
        o_ref[...] = acc
        for cp in cps:
            cp.wait_send()

    return pl.pallas_call(
        body,
        out_shape=jax.ShapeDtypeStruct(v.shape, F32),
        in_specs=[pl.BlockSpec(memory_space=pltpu.VMEM)],
        out_specs=pl.BlockSpec(memory_space=pltpu.VMEM),
        scratch_shapes=[pltpu.VMEM((8, R, 128), F32), pltpu.SemaphoreType.DMA((7,)), pltpu.SemaphoreType.DMA((7,))],
        name="allreduce_small",
    )(v)


def _adamw_math(w, g, m, v):
    m = ADAM_B1 * m + (1.0 - ADAM_B1) * g
    v = ADAM_B2 * v + (1.0 - ADAM_B2) * (g * g)
    m_hat = m / (1.0 - ADAM_B1 ** ADAM_STEP)
    v_hat = v / (1.0 - ADAM_B2 ** ADAM_STEP)
    delta = -ADAM_LR * (m_hat / (jnp.sqrt(v_hat) + ADAM_EPS) + ADAM_WD * w)
    return delta, m, v


def _adamw(ws, gs, ms, vs, name, budget=1 << 19):
    n = len(ws)
    tiles = [_row_tile(w.shape[1], w.shape[2], budget=budget) for w in ws]
    per_layer = [w.shape[1] // t for w, t in zip(ws, tiles)]
    steps = [w.shape[0] * p for w, p in zip(ws, per_layer)]
    starts = [sum(steps[:k]) for k in range(n)]

    def body(*refs):
        i = pl.program_id(0)
        for k in range(n):
            w_ref, g_ref, m_ref, v_ref = (refs[j * n + k] for j in range(4))
            outs = [refs[(4 + j) * n + k] for j in range(3)]

            @pl.when((i >= starts[k]) & (i < starts[k] + steps[k]))
            def _(w_ref=w_ref, g_ref=g_ref, m_ref=m_ref, v_ref=v_ref, outs=outs):
                outs[0][...], outs[1][...], outs[2][...] = _adamw_math(w_ref[...], g_ref[...], m_ref[...], v_ref[...])

    def spec(k):
        def index(i):
            local = jnp.clip(i - starts[k], 0, steps[k] - 1)
            return local // per_layer[k], local % per_layer[k], 0
        return pl.BlockSpec((None, tiles[k], ws[k].shape[2]), index)

    specs = [spec(k) for k in range(n)]
    outs = [_out(w.shape, F32) for w in ws]
    res = _pallas(
        body,
        out_shape=tuple(outs * 3),
        grid=(sum(steps),),
        in_specs=specs * 4,
        out_specs=tuple(specs * 3),
        compiler_params=_cparams(("arbitrary",), sum(16 * t * w.shape[2] * 4 for w, t in zip(ws, tiles))),
        name=name,
    )(*ws, *gs, *ms, *vs)
    return res[:n], res[n:2 * n], res[2 * n:]


def _adamw_slabs(w, g, m, v, name, slabs=59):
    n, L, C = w.shape
    assert n % slabs == 0

    def body(w_ref, g_ref, m_ref, v_ref, d_ref, nm_ref, nv_ref):
        d_ref[...], nm_ref[...], nv_ref[...] = _adamw_math(w_ref[...], g_ref[...], m_ref[...], v_ref[...])

    blk = pl.BlockSpec((slabs, L, C), lambda i: (i, 0, 0))
    out = _out(w.shape, F32)
    return _pallas(
        body,
        out_shape=(out, out, out),
        grid=(n // slabs,),
        in_specs=[blk] * 4,
        out_specs=(blk, blk, blk),
        compiler_params=_cparams(("parallel",), 16 * slabs * 8 * C * 4),
        name=name,
    )(w, g, m, v)


def _adamw_small(ws, gs, ms, vs):
    n = len(ws)

    def body(*refs):
        for t in range(n):
            w_ref, g_ref, m_ref, v_ref = (refs[k * n + t] for k in range(4))
            d_ref, nm_ref, nv_ref = (refs[(4 + k) * n + t] for k in range(3))
            d_ref[...], nm_ref[...], nv_ref[...] = _adamw_math(w_ref[...], g_ref[...], m_ref[...], v_ref[...])

    vmem = pl.BlockSpec(memory_space=pltpu.VMEM)
    outs = [jax.ShapeDtypeStruct(w.shape, F32) for w in ws]
    res = pl.pallas_call(
        body,
        out_shape=outs * 3,
        in_specs=[vmem] * (4 * n),
        out_specs=[vmem] * (3 * n),
        name="adamw_small",
    )(*ws, *gs, *ms, *vs)
    return res[:n], res[n:2 * n], res[2 * n:]


IN_SIZES = (192, 192, 384, 384, 16, 512, 384, 384, 384)
IN_OFFS = tuple(int(v) for v in np.cumsum((0,) + IN_SIZES))
SMALL = ("norm_mix", "w_gla_gate", "b_gla_gate", "gla_norm", "b_dw", "conv_ln_g", "conv_ln_b", "rel_bias", "norm_ffn")


def _pad_cols(a, n):
    return jnp.pad(a, ((0, 0), (0, n - a.shape[1])))


W_IN_SHARD = 708
W_IN_ROWS = 736


def _pad_rows(a, n):
    return jnp.pad(a, ((0, n - a.shape[0]), (0, 0)))


def _split_w_in_t(w):
    s = [w[IN_OFFS[i]:IN_OFFS[i + 1]] for i in range(9)]
    w_gla = jnp.concatenate([_pad_rows(s[0], KW), _pad_rows(s[1], KW), s[2], s[3], _pad_rows(s[4], LRW)], axis=0)
    return w_gla, s[5], jnp.concatenate(s[6:9], axis=0)


def _join_w_in_t(g):
    gg = g["w_gla"]
    full = jnp.concatenate([gg[0:192], gg[KW:KW + 192], gg[2 * KW:2 * KW + VW], gg[2 * KW + VW:2 * KW + 2 * VW],
                            gg[2 * KW + 2 * VW:2 * KW + 2 * VW + 16], g["w_conv"], g["w_att"]], axis=0)
    return jnp.pad(full.reshape(4, W_IN_SHARD, D), ((0, 0), (0, W_IN_ROWS - W_IN_SHARD), (0, 0)))


def _pack(arrs, rows):
    flat = jnp.concatenate([a.reshape(-1) for a in arrs])
    return jnp.pad(flat, (0, rows * 128 - flat.shape[0])).reshape(rows, 128)


def _unpack(packed, shapes):
    flat = packed.reshape(-1)
    out, off = [], 0
    for s in shapes:
        n = int(np.prod(s))
        out.append(flat[off:off + n].reshape(s))
        off += n
    return out


def kernel(x, norm_mix, w_in, w_gla_gate, b_gla_gate, gla_norm, w_dw, b_dw, conv_ln_g, conv_ln_b, rel_bias, w_out, norm_ffn, w_up, w_down, norm_final, loss_target, m_norm_mix, m_w_in, m_w_gla_gate, m_b_gla_gate, m_gla_norm, m_w_dw, m_b_dw, m_conv_ln_g, m_conv_ln_b, m_rel_bias, m_w_out, m_norm_ffn, m_w_up, m_w_down, m_norm_final, v_norm_mix, v_w_in, v_w_gla_gate, v_b_gla_gate, v_gla_norm, v_w_dw, v_b_dw, v_conv_ln_g, v_conv_ln_b, v_rel_bias, v_w_out, v_norm_ffn, v_w_up, v_w_down, v_norm_final):
    P = dict(norm_mix=norm_mix, w_in=w_in, w_gla_gate=w_gla_gate, b_gla_gate=b_gla_gate, gla_norm=gla_norm, w_dw=w_dw, b_dw=b_dw,
             conv_ln_g=conv_ln_g, conv_ln_b=conv_ln_b, rel_bias=rel_bias, w_out=w_out, norm_ffn=norm_ffn, w_up=w_up,
             w_down=w_down, norm_final=norm_final)
    Mo = dict(norm_mix=m_norm_mix, w_in=m_w_in, w_gla_gate=m_w_gla_gate, b_gla_gate=m_b_gla_gate, gla_norm=m_gla_norm, w_dw=m_w_dw,
              b_dw=m_b_dw, conv_ln_g=m_conv_ln_g, conv_ln_b=m_conv_ln_b, rel_bias=m_rel_bias, w_out=m_w_out, norm_ffn=m_norm_ffn,
              w_up=m_w_up, w_down=m_w_down, norm_final=m_norm_final)
    Vo = dict(norm_mix=v_norm_mix, w_in=v_w_in, w_gla_gate=v_w_gla_gate, b_gla_gate=v_b_gla_gate, gla_norm=v_gla_norm, w_dw=v_w_dw,
              b_dw=v_b_dw, conv_ln_g=v_conv_ln_g, conv_ln_b=v_conv_ln_b, rel_bias=v_rel_bias, w_out=v_w_out, norm_ffn=v_norm_ffn,
              w_up=v_w_up, w_down=v_w_down, norm_final=v_norm_final)
    depth = w_in.shape[0]
    xi, yi, ci = lax.axis_index("x"), lax.axis_index("y"), lax.axis_index("c")
    chip = 2 * xi + yi

    plan = _Plan()
    _PLAN[0] = plan
    layers = [dict(
        norm_mix=norm_mix[l][None], wg=jnp.pad(w_gla_gate[l], ((0, LRW - 16), (0, KW - 192))),
        bg=_pad_cols(b_gla_gate[l][None], KW), gla_norm=gla_norm[l][None], b_dw=b_dw[l][None], ln_g=conv_ln_g[l][None],
        ln_b=conv_ln_b[l][None], rel_bias=rel_bias[l], norm_ffn=norm_ffn[l][None]) for l in range(depth)]

    w_in_t, m_w_in_t, v_w_in_t = (jnp.transpose(a, (2, 0, 1)) for a in (w_in, m_w_in, v_w_in))

    def w_in_shard(l):
        return _pad_rows(w_in_t[:, l, :], W_IN_ROWS).astype(BF16)

    def have_w_in(l, full):
        rows = jnp.concatenate([full[j, 0:W_IN_SHARD] for j in range(4)], axis=0)
        layers[l]["w_gla"], layers[l]["w_conv"], layers[l]["w_att"] = _split_w_in_t(rows)

    def have_w_out(l, full):
        w = full.reshape(D, D)
        layers[l]["w_out_g"], layers[l]["w_out_c"], layers[l]["w_out_a"] = w[0:VW], w[VW:VW + CW], w[VW + CW:]

    def have_w_up(l, full):
        layers[l]["w_up"] = full

    def have_w_down(l, full):
        layers[l]["w_down"] = full.reshape(D_FF, D)

    def have_w_dw(full):
        taps = full.reshape(4, depth, HALO, CW // 4)
        for l in range(depth):
            layers[l]["w_dw"] = jnp.transpose(taps[:, l], (1, 0, 2)).reshape(HALO, CW)

    first_d2d = []

    def first_ici(shard, have):
        return _phase_gather_ici(shard, lambda outs: first_d2d.append(_phase_gather_d2d(outs[0], lambda done: have(done[0]))))

    w_dw_pad = jnp.pad(w_dw, ((0, 0), (0, HALO - CK), (0, 0))).reshape(depth * HALO, CW // 4)
    _comm_only([first_ici(w_in_shard(0), functools.partial(have_w_in, 0)), first_ici(w_dw_pad, have_w_dw)],
               "gather_first_ici")
    _comm_only(first_d2d, "gather_first_d2d")

    def gather_behind(shard, ici_call, d2d_call, have):
        plan.at(ici_call, _phase_gather_ici(
            shard, lambda outs: plan.at(d2d_call, _phase_gather_d2d(outs[0], lambda done: have(done[0])))))

    for l in range(depth):
        if l > 0:
            gather_behind(w_in_shard(l), f"l{l - 1}_mlp_up", f"l{l - 1}_mlp_down", functools.partial(have_w_in, l))
        gather_behind(w_out[l].astype(BF16), f"l{l - 1}_mlp_down" if l > 0 else "l0_proj", f"l{l}_gla_fwd",
                      functools.partial(have_w_out, l))
        if l > 0:
            gather_behind(w_up[l].astype(BF16), f"l{l}_proj", f"l{l}_gla_fwd", functools.partial(have_w_up, l))
            gather_behind(w_down[l].astype(BF16), f"l{l}_gla_fwd", f"l{l}_att_fwd", functools.partial(have_w_down, l))
        else:
            gather_behind(w_up[l].astype(BF16), f"l{l}_gla_fwd", f"l{l}_att_fwd", functools.partial(have_w_up, l))
            gather_behind(w_down[l].astype(BF16), f"l{l}_att_fwd", f"l{l}_mlp_up", functools.partial(have_w_down, l))

    reduced = {}
    last_swap = []

    def reduce_calls(name, l):
        if name == "w_down":
            return f"l{l}_mlp_up_dx", f"l{l}_gla_bwd", f"l{l}_conv_bwd_dc"
        if name == "w_up":
            return f"l{l}_out_dx", f"l{l}_att_bwd", f"l{l}_proj_dw"
        if name == "w_out":
            return f"l{l}_gla_bwd", f"l{l}_conv_bwd_dc", f"l{l}_att_bwd"
        if l > 0:
            return f"l{l}_proj_dx", f"l{l - 1}_mlp_down_dw", f"l{l - 1}_mlp_up_dx"
        return None, "l0_proj_dx", None

    def reduce_behind(l, name, full):
        calls = reduce_calls(name, l)

        def swapped(outs):
            pair = _pair_add(full[:, None], outs[0][:, None], ci, f"reduce_pair_add_{name}{l}")[:, 0]
            plan.at(calls[1], _phase_chip_scatter(pair, scattered))

        def scattered(outs):
            half = _sum_chips(outs[0][:, None], f"reduce_sum_chips_{name}{l}")[0]
            phase = _phase_pair_allgather(half, l, depth, reduced.get(name), gathered)
            if calls[2] is None:
                last_swap.append(phase)
            else:
                plan.at(calls[2], phase)

        def gathered(outs):
            reduced[name] = outs[0]

        if calls[0] is None:
            _comm_only([_phase_pair_exchange(full, swapped)], f"reduce_pair_exchange_{name}{l}")
        else:
            plan.at(calls[0], _phase_pair_exchange(full, swapped))

    loss_part, grad_x, grads, g_final = _local_step(x[0], loss_target[0], layers, norm_final[None], reduce_behind)
    loss = lax.psum(loss_part[0, 0], ("x", "y", "c"))

    G, delta, new_m, new_v = {}, {}, {}, {}
    early = ("w_down", "w_up", "w_out")
    for name in early:
        G[name] = reduced[name]
    ds, nms, nvs = _adamw([P[k] for k in early], [G[k] for k in early], [Mo[k] for k in early], [Vo[k] for k in early],
                          "adamw_early")
    for i, name in enumerate(early):
        delta[name], new_m[name], new_v[name] = ds[i], nms[i], nvs[i]
    _PLAN[0] = None
    assert not plan.by_call, sorted(plan.by_call)

    small_g = []
    for l in range(depth):
        g = grads[l]
        small_g += [g["norm_mix"], g["wg"][0:16, 0:192], g["bg"][:, 0:192], g["gla_norm"], g["b_dw"], g["ln_g"], g["ln_b"],
                    g["rel_bias"], g["norm_ffn"], g["w_dw"][0:CK]]
    small_g.append(g_final)
    small_shapes = [a.shape for a in small_g]
    n_small = sum(int(np.prod(s)) for s in small_shapes)
    rows = -(-n_small // 1024) * 8
    red = _unpack(_allreduce_small(_pack(small_g, rows)), small_shapes)
    per = len(SMALL) + 1
    for i, name in enumerate(SMALL):
        G[name] = jnp.stack([red[l * per + i].reshape(P[name].shape[1:]) for l in range(depth)])
    gw_dw_all = jnp.stack([red[l * per + len(SMALL)] for l in range(depth)])
    G["w_dw"] = lax.dynamic_slice_in_dim(gw_dw_all, chip * (CW // 4), CW // 4, axis=2)
    G["norm_final"] = red[-1].reshape(norm_final.shape)

    _comm_only(last_swap, "reduce_pair_allgather_last")
    back = lambda a: jnp.transpose(a, (1, 2, 0))
    g_in_t = jnp.transpose(reduced["w_in"][:, 0:W_IN_SHARD, :], (1, 0, 2))
    d_in, nm_in, nv_in = _adamw_slabs(w_in_t, g_in_t, m_w_in_t, v_w_in_t, "adamw_w_in")
    G["w_in"], delta["w_in"], new_m["w_in"], new_v["w_in"] = back(g_in_t), back(d_in), back(nm_in), back(nv_in)

    small_names = list(SMALL) + ["w_dw", "norm_final"]
    two_d = lambda a: a.reshape(-1, a.shape[-1])
    ds, nms, nvs = _adamw_small([two_d(P[k]) for k in small_names], [two_d(G[k]) for k in small_names],
                                [two_d(Mo[k]) for k in small_names], [two_d(Vo[k]) for k in small_names])
    for i, name in enumerate(small_names):
        shp = P[name].shape
        delta[name], new_m[name], new_v[name] = ds[i].reshape(shp), nms[i].reshape(shp), nvs[i].reshape(shp)

    order = ["norm_mix", "w_in", "w_gla_gate", "b_gla_gate", "gla_norm", "w_dw", "b_dw", "conv_ln_g", "conv_ln_b", "rel_bias",
             "w_out", "norm_ffn", "w_up", "w_down", "norm_final"]
    return (loss, grad_x[None], *[G[k] for k in order], *[delta[k] for k in order], *[new_m[k] for k in order],
            *[new_v[k] for k in order])
```

```python
import functools

import numpy as np
import jax
import jax.numpy as jnp
from jax import lax
from jax.experimental import pallas as pl
from jax.experimental.pallas import tpu as pltpu

F32 = jnp.float32
BF16 = jnp.bfloat16
HI = lax.Precision.HIGHEST

D = 1024
CHUNK = 64
GLA_DK, GLA_DV, GLA_H = 48, 96, 4
KW = 256
VW = 384
LRW = 128
GLA_TAU = 16.0
CW = 256
CK = 31
AW = 384
AH = 6
BAND = 576
LEFT = 512
D_FF = 4096
EPS = 1e-6
NEG = -1e30
N_REL = 257

GLA_COLS = 2 * KW + 2 * VW + LRW
CONV_COLS = 2 * CW
ATT_COLS = 3 * AW

ADAM_LR, ADAM_B1, ADAM_B2, ADAM_EPS, ADAM_WD, ADAM_STEP = 0.001, 0.9, 0.999, 1e-08, 0.01, 10

VMEM_CAP = 56 * 1024 * 1024
MESH = pl.DeviceIdType.MESH


def _cparams(sem, vmem_bytes):
    limit = int(min(VMEM_CAP, max(vmem_bytes * 5 // 4 + (4 << 20), 16 << 20)))
    return pltpu.CompilerParams(dimension_semantics=sem, vmem_limit_bytes=limit)


def _out(shape, dtype):
    return pltpu.HBM(tuple(shape), dtype)


class _Comm:
    def __init__(self, ins, outs, aliases, sems, start, finish, then=None):
        self.ins, self.outs, self.aliases, self.sems = list(ins), list(outs), dict(aliases), list(sems)
        self.start, self.finish, self.then = start, finish, then


class _Plan:
    def __init__(self):
        self.by_call = {}

    def at(self, call, comm):
        self.by_call.setdefault(call, []).append(comm)

    def take(self, call):
        return self.by_call.pop(call, [])


_PLAN = [None]


def _pin(a):
    return pltpu.with_memory_space_constraint(a, pltpu.HBM) if jnp.issubdtype(a.dtype, jnp.floating) else a


def _pallas(body, **kw):
    comms = _PLAN[0].take(kw.get("name")) if _PLAN[0] is not None else []
    if not comms:
        call = pl.pallas_call(body, **kw)
        return lambda *args: call(*[_pin(a) for a in args])

    grid = tuple(kw.get("grid", ()))
    single = not isinstance(kw["out_shape"], (tuple, list))
    out_shape = [kw["out_shape"]] if single else list(kw["out_shape"])
    out_specs = [kw["out_specs"]] if single else list(kw["out_specs"])
    in_specs = list(kw["in_specs"])
    scratch = list(kw.get("scratch_shapes", ()))
    n_in, n_out, n_scr = len(in_specs), len(out_shape), len(scratch)
    c_in = sum(len(c.ins) for c in comms)
    c_out = sum(len(c.outs) for c in comms)
    aliases = dict(kw.get("input_output_aliases", {}))
    i0, o0 = n_in, n_out
    for c in comms:
        for i, o in c.aliases.items():
            aliases[i0 + i] = o0 + o
        i0 += len(c.ins)
        o0 += len(c.outs)

    def wrapped(*refs):
        ins, c_ins = refs[:n_in], refs[n_in:n_in + c_in]
        outs, c_outs = refs[n_in + c_in:n_in + c_in + n_out], refs[n_in + c_in + n_out:n_in + c_in + n_out + c_out]
        scr, c_sems = refs[n_in + c_in + n_out + c_out:][:n_scr], refs[n_in + c_in + n_out + c_out + n_scr:]

        def each(what):
            i0 = o0 = s0 = 0
            for c in comms:
                getattr(c, what)(c_ins[i0:i0 + len(c.ins)], c_outs[o0:o0 + len(c.outs)], c_sems[s0:s0 + len(c.sems)])
                i0, o0, s0 = i0 + len(c.ins), o0 + len(c.outs), s0 + len(c.sems)

        if grid:
            first = functools.reduce(jnp.logical_and, [pl.program_id(a) == 0 for a in range(len(grid))])
            last = functools.reduce(jnp.logical_and, [pl.program_id(a) == grid[a] - 1 for a in range(len(grid))])
            pl.when(first)(lambda: each("start"))
            body(*ins, *outs, *scr)
            pl.when(last)(lambda: each("finish"))
        else:
            each("start")
            body(*ins, *outs, *scr)
            each("finish")

    kw = dict(kw)
    kw["in_specs"] = in_specs + [ANY] * c_in
    kw["out_shape"] = out_shape + [_out(s.shape, s.dtype) for c in comms for s in c.outs]
    kw["out_specs"] = out_specs + [ANY] * c_out
    staging = [s for c in comms for s in c.sems if len(s) == 2 and not isinstance(s[1], int)]
    kw["scratch_shapes"] = scratch + [pltpu.VMEM(*s) if s in staging else pltpu.SemaphoreType.DMA(s) for c in comms for s in c.sems]
    kw["input_output_aliases"] = aliases
    extra = sum(_nbytes(*s) for s in staging)
    old = kw.get("compiler_params")
    limit = (old.vmem_limit_bytes if old is not None else 16 << 20) + extra
    kw["compiler_params"] = pltpu.CompilerParams(
        dimension_semantics=old.dimension_semantics if old is not None else None, vmem_limit_bytes=int(min(VMEM_CAP, limit)))
    call = pl.pallas_call(wrapped, **kw)

    def run(*args):
        res = call(*[_pin(a) for a in args], *[_pin(a) for c in comms for a in c.ins])
        o0 = n_out
        for c in comms:
            if c.then is not None:
                c.then(res[o0:o0 + len(c.outs)])
            o0 += len(c.outs)
        return res[0] if single else res[:n_out]

    return run


def _nbytes(shape, dtype):
    return int(np.prod(shape)) * jnp.dtype(dtype).itemsize


def _sigmoid(x):
    return 1.0 / (1.0 + jnp.exp(-x))


_DIMS = {"nn": (((1,), (0,)), ((), ())), "nt": (((1,), (1,)), ((), ())), "tn": (((0,), (0,)), ((), ()))}


def _mm(a, b, *, mode, out_dtype, name, tm=512, tn=None, tk=None, a_pro=None, epi=None, extra=None,
        b_chips=False, out_chips=False):
    b2 = (b.shape[1], 4 * b.shape[2]) if b_chips else b.shape
    if mode == "nn":
        (M, K), (K2, N) = a.shape, b2
    elif mode == "nt":
        (M, K), (N, K2) = a.shape, b2
    else:
        (K, M), (K2, N) = a.shape, b2
    assert K == K2, (a.shape, b.shape, mode)
    tm = min(tm, M)
    tn = N if tn is None else min(tn, N)
    tk = K if tk is None else min(tk, K)
    assert M % tm == 0 and N % tn == 0 and K % tk == 0, (M, N, K, tm, tn, tk)
    nk = K // tk
    a_blk = (tk, tm) if mode == "tn" else (tm, tk)
    a_map = (lambda i, j, k: (k, i)) if mode == "tn" else (lambda i, j, k: (i, k))
    b_blk = (tn, tk) if mode == "nt" else (tk, tn)
    b_map = (lambda i, j, k: (j, k)) if mode == "nt" else (lambda i, j, k: (k, j))
    if b_chips:
        per = b.shape[2] // b_blk[1]
        assert b.shape[2] % b_blk[1] == 0 and mode != "tn"
        flat_map = b_map
        b_map = lambda i, j, k: (flat_map(i, j, k)[1] // per, flat_map(i, j, k)[0], flat_map(i, j, k)[1] % per)
        b_blk = (None,) + b_blk
    in_specs = [pl.BlockSpec(a_blk, a_map), pl.BlockSpec(b_blk, b_map)]
    args = [a, b]
    if epi is not None:
        in_specs.append(pl.BlockSpec((tm, tn), lambda i, j, k: (i, j)))
        args.append(extra)

    def body(*refs):
        a_ref, b_ref = refs[0], refs[1]
        e_ref = refs[2] if epi is not None else None
        o_ref = refs[3] if epi is not None else refs[2]
        av = a_ref[...]
        if a_pro == "relu2":
            af = jnp.maximum(av.astype(F32), 0.0)
            av = af * af
        p = lax.dot_general(av.astype(BF16), b_ref[...].astype(BF16), _DIMS[mode], preferred_element_type=F32)

        def finish(acc):
            if epi == "add":
                acc = acc + e_ref[...].astype(F32)
            elif epi == "relu2grad":
                acc = acc * (2.0 * jnp.maximum(e_ref[...].astype(F32), 0.0))
            o_ref[...] = acc.astype(o_ref.dtype)

        if nk == 1:
            finish(p)
        else:
            acc_ref = refs[-1]
            k = pl.program_id(2)

            @pl.when(k == 0)
            def _():
                acc_ref[...] = p

            @pl.when(k > 0)
            def _():
                acc_ref[...] += p

            @pl.when(k == nk - 1)
            def _():
                finish(acc_ref[...])

    vm = 2 * (_nbytes(a_blk, a.dtype) + _nbytes((tk, tn), b.dtype) + _nbytes((tm, tn), out_dtype))
    vm += 3 * _nbytes((tm, tn), F32)
    if epi is not None:
        vm += 2 * _nbytes((tm, tn), extra.dtype)
    if out_chips:
        per_out = N // 4 // tn
        assert N % (4 * tn) == 0
        out_shape = _out((4, M, N // 4), out_dtype)
        out_spec = pl.BlockSpec((None, tm, tn), lambda i, j, k: (j // per_out, i, j % per_out))
    else:
        out_shape = _out((M, N), out_dtype)
        out_spec = pl.BlockSpec((tm, tn), lambda i, j, k: (i, j))
    return _pallas(
        body,
        out_shape=out_shape,
        grid=(M // tm, N // tn, nk),
        in_specs=in_specs,
        out_specs=out_spec,
        scratch_shapes=[pltpu.VMEM((tm, tn), F32)] if nk > 1 else [],
        compiler_params=_cparams(("parallel", "parallel", "arbitrary"), vm),
        name=name,
    )(*args)


def _mm_fan(a, bs, *, mode, out_dtypes, name, tm=512, norm_g=None):
    M, K = a.shape
    ns = [b.shape[1] if mode == "nn" else b.shape[0] for b in bs]
    n = len(bs)
    first_out = 1 + n + (norm_g is not None)

    def body(*refs):
        if norm_g is None:
            av = refs[0][...].astype(BF16)
        else:
            x = refs[0][...]
            av = (x * lax.rsqrt(jnp.mean(x * x, axis=-1, keepdims=True) + EPS) * refs[1 + n][...]).astype(BF16)
            refs[first_out + n][...] = av
        for i in range(n):
            refs[first_out + i][...] = lax.dot_general(av, refs[1 + i][...], _DIMS[mode],
                                                       preferred_element_type=F32).astype(refs[first_out + i].dtype)

    vm = 4 * _nbytes((tm, K), F32) + sum(2 * _nbytes(b.shape, b.dtype) + 3 * _nbytes((tm, nn), F32) for b, nn in zip(bs, ns))
    in_specs = [pl.BlockSpec((tm, K), lambda i: (i, 0))] + [pl.BlockSpec(b.shape, lambda i: (0, 0)) for b in bs]
    out_shape = [_out((M, nn), dt) for nn, dt in zip(ns, out_dtypes)]
    out_specs = [pl.BlockSpec((tm, nn), lambda i: (i, 0)) for nn in ns]
    args = [a, *bs]
    if norm_g is not None:
        in_specs.append(pl.BlockSpec((1, K), lambda i: (0, 0)))
        out_shape.append(_out((M, K), BF16))
        out_specs.append(pl.BlockSpec((tm, K), lambda i: (i, 0)))
        args.append(norm_g)
    return _pallas(
        body,
        out_shape=tuple(out_shape),
        grid=(M // tm,),
        in_specs=in_specs,
        out_specs=tuple(out_specs),
        compiler_params=_cparams(("parallel",), vm),
        name=name,
    )(*args)


def _mm_sum(as_, bs, *, mode, out_dtype, name, extra=None, norm_bwd=None, tm=512):
    M = as_[0].shape[0]
    N = bs[0].shape[1] if mode == "nn" else bs[0].shape[0]
    n = len(as_)

    def body(*refs):
        acc = None
        for i in range(n):
            p = lax.dot_general(refs[i][...].astype(BF16), refs[n + i][...], _DIMS[mode], preferred_element_type=F32)
            acc = p if acc is None else acc + p
        if extra is not None:
            acc = acc + refs[2 * n][...].astype(F32)
        if norm_bwd is None:
            refs[-1][...] = acc.astype(refs[-1].dtype)
        else:
            h_ref, g_ref, dres_ref, dh_ref, dg_ref = refs[-5:]

            @pl.when(pl.program_id(0) == 0)
            def _():
                dg_ref[...] = jnp.zeros_like(dg_ref)

            x = h_ref[...]
            r = lax.rsqrt(jnp.mean(x * x, axis=-1, keepdims=True) + EPS)
            gy = acc * g_ref[...]
            dot = jnp.mean(x * gy, axis=-1, keepdims=True)
            dh_ref[...] = dres_ref[...] + r * gy - x * (r * r * r * dot)
            dg_ref[...] += jnp.sum(acc * x * r, axis=0, keepdims=True)

    row = pl.BlockSpec((tm, N), lambda i: (i, 0))
    in_specs = [pl.BlockSpec((tm, a.shape[1]), lambda i: (i, 0)) for a in as_]
    in_specs += [pl.BlockSpec(b.shape, lambda i: (0, 0)) for b in bs]
    args = list(as_) + list(bs)
    if extra is not None:
        in_specs.append(row)
        args.append(extra)
    vm = sum(2 * _nbytes((tm, a.shape[1]), a.dtype) for a in as_) + sum(2 * _nbytes(b.shape, b.dtype) for b in bs)
    vm += 8 * _nbytes((tm, N), F32)
    if norm_bwd is None:
        out_shape, out_specs, sem = _out((M, N), out_dtype), row, "parallel"
    else:
        vec = pl.BlockSpec((1, N), lambda i: (0, 0))
        in_specs += [row, vec, row]
        args += list(norm_bwd)
        out_shape, out_specs, sem = (_out((M, N), F32), _out((1, N), F32)), (row, vec), "arbitrary"
        vm += 8 * _nbytes((tm, N), F32)
    return _pallas(
        body,
        out_shape=out_shape,
        grid=(M // tm,),
        in_specs=in_specs,
        out_specs=out_specs,
        compiler_params=_cparams((sem,), vm),
        name=name,
    )(*args)


def _mm_tn_multi(ops, pairs, *, name, tk=512):
    T = ops[0].shape[0]
    n, m = len(ops), len(pairs)
    shapes = [(ops[a].shape[1], ops[b].shape[1]) for a, b in pairs]

    def body(*refs):
        vals = [refs[i][...].astype(BF16) for i in range(n)]
        first = pl.program_id(0) == 0
        for j, (a, b) in enumerate(pairs):
            p = lax.dot_general(vals[a], vals[b], _DIMS["tn"], preferred_element_type=F32)
            o_ref = refs[n + j]

            @pl.when(first)
            def _(o_ref=o_ref, p=p):
                o_ref[...] = p

            @pl.when(jnp.logical_not(first))
            def _(o_ref=o_ref, p=p):
                o_ref[...] += p

    vm = sum(2 * _nbytes((tk, o.shape[1]), o.dtype) for o in ops) + sum(3 * _nbytes(s, F32) for s in shapes)
    return _pallas(
        body,
        out_shape=tuple(_out(s, F32) for s in shapes),
        grid=(T // tk,),
        in_specs=[pl.BlockSpec((tk, o.shape[1]), lambda k: (k, 0)) for o in ops],
        out_specs=tuple(pl.BlockSpec(s, lambda k: (0, 0)) for s in shapes),
        compiler_params=_cparams(("arbitrary",), vm),
        name=name,
    )(*ops)


def _rmsnorm_fwd(h, g, name, tm=512):
    T = h.shape[0]

    def body(h_ref, g_ref, o_ref):
        x = h_ref[...]
        r = lax.rsqrt(jnp.mean(x * x, axis=-1, keepdims=True) + EPS)
        o_ref[...] = (x * r * g_ref[...]).astype(o_ref.dtype)

    return _pallas(
        body,
        out_shape=_out((T, D), BF16),
        grid=(T // tm,),
        in_specs=[pl.BlockSpec((tm, D), lambda i: (i, 0)), pl.BlockSpec((1, D), lambda i: (0, 0))],
        out_specs=pl.BlockSpec((tm, D), lambda i: (i, 0)),
        compiler_params=_cparams(("parallel",), 8 * _nbytes((tm, D), F32)),
        name=name,
    )(h, g)


def _rmsnorm_bwd(dxn, h, g, dres, name, tm=512):
    T = h.shape[0]

    def body(dxn_ref, h_ref, g_ref, dres_ref, dh_ref, dg_ref):
        @pl.when(pl.program_id(0) == 0)
        def _():
            dg_ref[...] = jnp.zeros_like(dg_ref)

        x = h_ref[...]
        dy = dxn_ref[...].astype(F32)
        r = lax.rsqrt(jnp.mean(x * x, axis=-1, keepdims=True) + EPS)
        gy = dy * g_ref[...]
        dot = jnp.mean(x * gy, axis=-1, keepdims=True)
        dh_ref[...] = dres_ref[...] + r * gy - x * (r * r * r * dot)
        dg_ref[...] += jnp.sum(dy * x * r, axis=0, keepdims=True)

    row = pl.BlockSpec((tm, D), lambda i: (i, 0))
    vec = pl.BlockSpec((1, D), lambda i: (0, 0))
    return _pallas(
        body,
        out_shape=(_out((T, D), F32), _out((1, D), F32)),
        grid=(T // tm,),
        in_specs=[row, row, vec, row],
        out_specs=(row, vec),
        compiler_params=_cparams(("arbitrary",), 12 * _nbytes((tm, D), F32)),
        name=name,
    )(dxn, h, g, dres)


def _final_loss(h, g, target, name, tm=512):
    T = h.shape[0]

    def body(h_ref, g_ref, t_ref, loss_ref, dh_ref, dg_ref):
        @pl.when(pl.program_id(0) == 0)
        def _():
            dg_ref[...] = jnp.zeros_like(dg_ref)
            loss_ref[...] = jnp.zeros_like(loss_ref)

        x = h_ref[...]
        gg = g_ref[...]
        r = lax.rsqrt(jnp.mean(x * x, axis=-1, keepdims=True) + EPS)
        y = x * r * gg
        e = y - t_ref[...]
        loss_ref[...] += 0.5 * jnp.sum(jnp.mean(e * e, axis=-1, keepdims=True), axis=0, keepdims=True)
        dy = e * (1.0 / D)
        gy = dy * gg
        dot = jnp.mean(x * gy, axis=-1, keepdims=True)
        dh_ref[...] = r * gy - x * (r * r * r * dot)
        dg_ref[...] += jnp.sum(dy * x * r, axis=0, keepdims=True)

    row = pl.BlockSpec((tm, D), lambda i: (i, 0))
    vec = pl.BlockSpec((1, D), lambda i: (0, 0))
    one = pl.BlockSpec((1, 1), lambda i: (0, 0))
    return _pallas(
        body,
        out_shape=(_out((1, 1), F32), _out((T, D), F32), _out((1, D), F32)),
        grid=(T // tm,),
        in_specs=[row, vec, row],
        out_specs=(one, row, vec),
        compiler_params=_cparams(("arbitrary",), 12 * _nbytes((tm, D), F32)),
        name=name,
    )(h, g, target)


GLA_G = 8


def _gla_consts():
    i = np.arange(KW)[:, None]
    j = np.arange(VW)[None, :]
    mask = ((i // GLA_DK) == (j // GLA_DV)) & (i < GLA_H * GLA_DK)
    a = np.arange(VW)
    hm = ((a[:, None] // GLA_DV) == (a[None, :] // GLA_DV)).astype(np.float32)
    c = np.arange(CHUNK)
    low = (c[:, None] >= c[None, :]).astype(np.float32)
    return jnp.asarray(mask.astype(np.float32)), jnp.asarray(hm, BF16), jnp.asarray(low, BF16)


def _split(x):
    hi = x.astype(BF16)
    return hi, (x - hi.astype(F32)).astype(BF16)


def _dot_sel(a, b, dims, split):
    if split == "a":
        hi, lo = _split(a)
        return (lax.dot_general(hi, b, dims, preferred_element_type=F32) + lax.dot_general(lo, b, dims, preferred_element_type=F32))
    hi, lo = _split(b)
    return (lax.dot_general(a, hi, dims, preferred_element_type=F32) + lax.dot_general(a, lo, dims, preferred_element_type=F32))


def _dot3(a, b, dims):
    ah, al = _split(a)
    bh, bl = _split(b)
    return (lax.dot_general(ah, bh, dims, preferred_element_type=F32) + lax.dot_general(al, bh, dims, preferred_element_type=F32)
            + lax.dot_general(ah, bl, dims, preferred_element_type=F32))


def _dot3s(a_s, b_s, dims):
    (ah, al), (bh, bl) = a_s, b_s
    return (lax.dot_general(ah, bh, dims, preferred_element_type=F32) + lax.dot_general(al, bh, dims, preferred_element_type=F32)
            + lax.dot_general(ah, bl, dims, preferred_element_type=F32))


def _gla_group_common(p_ref, wg, bg):
    lr_s = _split(p_ref[:, 2 * KW + 2 * VW:GLA_COLS])
    wg_s = _split(wg)
    z = _dot3s(lr_s, wg_s, _DIMS["nn"]) + bg
    la = (jnp.minimum(z, 0.0) - jnp.log(1.0 + jnp.exp(-jnp.abs(z)))) * (1.0 / GLA_TAU)
    return lr_s, wg_s, z, _split(la)


def _gla_chunk_common(p_ref, rows, la_s, low, ones_v):
    q = p_ref[rows, 0:KW]
    k = p_ref[rows, KW:2 * KW]
    v = p_ref[rows, 2 * KW:2 * KW + VW]
    g = p_ref[rows, 2 * KW + VW:2 * KW + 2 * VW]
    la_h, la_l = la_s[0][rows], la_s[1][rows]
    cum = jnp.dot(low, la_h, preferred_element_type=F32) + jnp.dot(low, la_l, preferred_element_type=F32)
    endb = cum[CHUNK - 1:CHUNK, :]
    w = jnp.exp(endb - cum)
    a_full = jnp.exp(lax.dot_general(la_h, ones_v, _DIMS["tn"], preferred_element_type=F32)
                     + lax.dot_general(la_l, ones_v, _DIMS["tn"], preferred_element_type=F32))
    return q, k, v, g, w, endb, a_full


def _gla_fwd(p, wg, bg, gn, consts, name):
    T = p.shape[0]
    rb = CHUNK * GLA_G
    ng = T // rb
    mask, hm, low = consts[:3]
    scale = GLA_DK ** -0.5

    def body(p_ref, wg_ref, bg_ref, gn_ref, m_ref, hm_ref, l_ref, o_ref, st_ref, s_ref):
        @pl.when(pl.program_id(0) == 0)
        def _():
            s_ref[...] = jnp.zeros_like(s_ref)

        wg_v, bg_v, gn_v = wg_ref[...], bg_ref[...], gn_ref[...]
        ones_v = jnp.ones((CHUNK, VW), BF16)
        s_new = s_ref[...]
        _, _, _, la_s = _gla_group_common(p_ref, wg_v, bg_v)
        outs = []
        for c in range(GLA_G):
            rows = slice(c * CHUNK, (c + 1) * CHUNK)
            q, k, v, _, w, _, a_full = _gla_chunk_common(p_ref, rows, la_s, l_ref[...], ones_v)
            kd = (k * w).astype(BF16)
            kv = lax.dot_general(kd, v.astype(BF16), _DIMS["tn"], preferred_element_type=F32) * m_ref[...]
            s_new = a_full * s_new + kv
            st_ref[c] = s_new
            outs.append(jnp.dot((q * scale).astype(BF16), s_new.astype(BF16), preferred_element_type=F32))
        s_ref[...] = s_new
        o = jnp.concatenate(outs, axis=0)
        g = p_ref[:, 2 * KW + VW:2 * KW + 2 * VW]
        ms = _dot_sel(o * o, hm_ref[...], _DIMS["nn"], "a") * (1.0 / GLA_DV)
        o_ref[...] = (o * lax.rsqrt(ms + EPS) * gn_v * (g * _sigmoid(g))).astype(o_ref.dtype)

    full = lambda shape: pl.BlockSpec(shape, lambda i: tuple(0 for _ in shape))
    vm = 2 * _nbytes((rb, GLA_COLS), F32) + 2 * _nbytes((GLA_G, KW, VW), F32) + 12 * _nbytes((KW, VW), F32)
    return _pallas(
        body,
        out_shape=(_out((T, VW), BF16), _out((T // CHUNK, KW, VW), F32)),
        grid=(ng,),
        in_specs=[pl.BlockSpec((rb, GLA_COLS), lambda i: (i, 0)), full((LRW, KW)), full((1, KW)), full((1, VW)),
                  full((KW, VW)), full((VW, VW)), full((CHUNK, CHUNK))],
        out_specs=(pl.BlockSpec((rb, VW), lambda i: (i, 0)), pl.BlockSpec((GLA_G, KW, VW), lambda i: (i, 0, 0))),
        scratch_shapes=[pltpu.VMEM((KW, VW), F32)],
        compiler_params=_cparams(("arbitrary",), vm),
        name=name,
    )(p, wg, bg, gn, mask, hm, low)


def _gla_bwd(p, dy, states, wg, bg, gn, consts, name):
    T = p.shape[0]
    rb = CHUNK * GLA_G
    ng = T // rb
    mask, hm, low = consts[:3]
    scale = GLA_DK ** -0.5

    def body(p_ref, dy_ref, st_ref, sp_ref, wg_ref, bg_ref, gn_ref, m_ref, hm_ref, l_ref,
             dp_ref, dwg_ref, dbg_ref, dgn_ref, ga_ref):
        step = pl.program_id(0)

        @pl.when(step == 0)
        def _():
            ga_ref[...] = jnp.zeros_like(ga_ref)
            dwg_ref[...] = jnp.zeros_like(dwg_ref)
            dbg_ref[...] = jnp.zeros_like(dbg_ref)
            dgn_ref[...] = jnp.zeros_like(dgn_ref)

        first_group = step == ng - 1
        wg_v, bg_v, gn_v = wg_ref[...], bg_ref[...], gn_ref[...]
        ones_v = jnp.ones((CHUNK, VW), BF16)
        ones_8 = jnp.ones((8, VW), BF16)
        ga = ga_ref[...]
        lr_s, wg_s, z_all, la_s = _gla_group_common(p_ref, wg_v, bg_v)
        qss = [(p_ref[c * CHUNK:(c + 1) * CHUNK, 0:KW] * scale).astype(BF16) for c in range(GLA_G)]
        o = jnp.concatenate([jnp.dot(qss[c], st_ref[c].astype(BF16), preferred_element_type=F32) for c in range(GLA_G)], axis=0)
        g = p_ref[:, 2 * KW + VW:2 * KW + 2 * VW]
        dyv = dy_ref[...].astype(F32)
        r = lax.rsqrt(_dot_sel(o * o, hm_ref[...], _DIMS["nn"], "a") * (1.0 / GLA_DV) + EPS)
        on = o * r
        sg = _sigmoid(g)
        silu = g * sg
        d_on = dyv * gn_v * silu
        dp_ref[:, 2 * KW + VW:2 * KW + 2 * VW] = (dyv * on * gn_v * (sg * (1.0 + g * (1.0 - sg)))).astype(dp_ref.dtype)
        dgn_ref[...] += jnp.sum(dyv * on * silu, axis=0, keepdims=True)
        mo = _dot_sel(o * d_on, hm_ref[...], _DIMS["nn"], "a") * (1.0 / GLA_DV)
        dob_all = (r * d_on - o * (r * r * r) * mo).astype(BF16)
        dzs = [None] * GLA_G
        for c in reversed(range(GLA_G)):
            rows = slice(c * CHUNK, (c + 1) * CHUNK)
            _, k, v, _, w, endb, a_full = _gla_chunk_common(p_ref, rows, la_s, l_ref[...], ones_v)
            s_n = st_ref[c]
            if c > 0:
                s_prev = st_ref[c - 1]
            else:
                s_prev = jnp.where(first_group, 0.0, sp_ref[0])
            kd = k * w
            dob = dob_all[rows]
            dq = lax.dot_general(dob, s_n.astype(BF16), _DIMS["nt"], preferred_element_type=F32) * scale
            g_n = lax.dot_general(qss[c], dob, _DIMS["tn"], preferred_element_type=F32) * m_ref[...] + ga
            d_a = _dot_sel(ones_8, g_n * s_prev, _DIMS["nt"], "b")[0:1, :]
            g_nb = g_n.astype(BF16)
            dkd = lax.dot_general(v.astype(BF16), g_nb, _DIMS["nt"], preferred_element_type=F32)
            dv = jnp.dot(kd.astype(BF16), g_nb, preferred_element_type=F32)
            e = dkd * kd
            d_end = jnp.sum(e, axis=0, keepdims=True) + d_a * jnp.exp(endb)
            dla = _dot_sel(l_ref[...], -e, _DIMS["tn"], "b") + d_end
            dzs[c] = dla * (1.0 - _sigmoid(z_all[rows])) * (1.0 / GLA_TAU)
            ga = a_full * g_n
            dp_ref[rows, 0:KW] = dq.astype(dp_ref.dtype)
            dp_ref[rows, KW:2 * KW] = (dkd * w).astype(dp_ref.dtype)
            dp_ref[rows, 2 * KW:2 * KW + VW] = dv.astype(dp_ref.dtype)
        ga_ref[...] = ga
        dz = jnp.concatenate(dzs, axis=0)
        dz_s = _split(dz)
        dp_ref[:, 2 * KW + 2 * VW:GLA_COLS] = _dot3s(dz_s, wg_s, _DIMS["nt"]).astype(dp_ref.dtype)
        dwg_ref[...] += _dot3s(lr_s, dz_s, _DIMS["tn"])
        dbg_ref[...] += jnp.sum(dz, axis=0, keepdims=True)

    full = lambda shape: pl.BlockSpec(shape, lambda i: tuple(0 for _ in shape))
    rev = lambda i: (ng - 1 - i, 0)
    vm = 4 * _nbytes((rb, GLA_COLS), F32) + 2 * _nbytes((rb, VW), F32) + 2 * _nbytes((GLA_G + 1, KW, VW), F32)
    vm += 16 * _nbytes((KW, VW), F32)
    return _pallas(
        body,
        out_shape=(_out((T, GLA_COLS), BF16), _out((LRW, KW), F32),
                   _out((1, KW), F32), _out((1, VW), F32)),
        grid=(ng,),
        in_specs=[pl.BlockSpec((rb, GLA_COLS), rev), pl.BlockSpec((rb, VW), rev),
                  pl.BlockSpec((GLA_G, KW, VW), lambda i: (ng - 1 - i, 0, 0)),
                  pl.BlockSpec((1, KW, VW), lambda i: (jnp.maximum((ng - 1 - i) * GLA_G - 1, 0), 0, 0)),
                  full((LRW, KW)), full((1, KW)), full((1, VW)), full((KW, VW)), full((VW, VW)), full((CHUNK, CHUNK))],
        out_specs=(pl.BlockSpec((rb, GLA_COLS), rev), full((LRW, KW)), full((1, KW)), full((1, VW))),
        scratch_shapes=[pltpu.VMEM((KW, VW), F32)],
        compiler_params=_cparams(("arbitrary",), vm),
        name=name,
    )(p, dy, states, states, wg, bg, gn, mask, hm, low)


CONV_TM = 512
HALO = 32
CONV_RB = 64


def _glu(u):
    a = u[:, 0:CW]
    b = u[:, CW:2 * CW]
    return a * _sigmoid(b)


def _conv_taps(buf_ref, w_ref, rb0, first_tap):
    acc = jnp.zeros((CONV_RB, CW), F32)
    for j in range(CK):
        s = rb0 + first_tap(j)
        acc = acc + w_ref[j:j + 1, :] * buf_ref[s:s + CONV_RB, :]
    return acc


def _ln_fwd(c, lg, lb):
    mu = jnp.mean(c, axis=-1, keepdims=True)
    xc = c - mu
    rstd = lax.rsqrt(jnp.mean(xc * xc, axis=-1, keepdims=True) + EPS)
    n = xc * rstd
    return n, rstd, n * lg + lb


def _conv_fwd(u, w, b, lg, lb, name):
    T = u.shape[0]
    tm = CONV_TM

    def body(u_ref, uh_ref, w_ref, b_ref, lg_ref, lb_ref, o_ref, c_ref, hbuf):
        i = pl.program_id(0)
        hbuf[0:HALO, :] = jnp.where(i > 0, _glu(uh_ref[...]), 0.0)
        hbuf[HALO:HALO + tm, :] = _glu(u_ref[...])
        for r in range(tm // CONV_RB):
            rows = slice(r * CONV_RB, (r + 1) * CONV_RB)
            acc = _conv_taps(hbuf, w_ref, r * CONV_RB, lambda j: HALO - (CK - 1) + j)
            c_ref[rows, :] = acc
            _, _, zz = _ln_fwd(acc + b_ref[...], lg_ref[...], lb_ref[...])
            o_ref[rows, :] = (zz * _sigmoid(zz)).astype(o_ref.dtype)

    vec = pl.BlockSpec((1, CW), lambda i: (0, 0))
    return _pallas(
        body,
        out_shape=(_out((T, CW), BF16), _out((T, CW), F32)),
        grid=(T // tm,),
        in_specs=[pl.BlockSpec((tm, CONV_COLS), lambda i: (i, 0)),
                  pl.BlockSpec((HALO, CONV_COLS), lambda i: (jnp.maximum(i * (tm // HALO) - 1, 0), 0)),
                  pl.BlockSpec((HALO, CW), lambda i: (0, 0)), vec, vec, vec],
        out_specs=(pl.BlockSpec((tm, CW), lambda i: (i, 0)), pl.BlockSpec((tm, CW), lambda i: (i, 0))),
        scratch_shapes=[pltpu.VMEM((tm + HALO, CW), F32)],
        compiler_params=_cparams(("arbitrary",), 8 * _nbytes((tm, CONV_COLS), F32)),
        name=name,
    )(u, u, w, b, lg, lb)


def _conv_bwd_dc(u, conv, dout, b, lg, lb, name):
    T = u.shape[0]
    tm = CONV_TM
    nsteps = T // tm

    def body(u_ref, uh_ref, c_ref, do_ref, b_ref, lg_ref, lb_ref, dc_ref, dw_ref, db_ref, dlg_ref, dlb_ref, hbuf, dwacc):
        i = pl.program_id(0)

        @pl.when(i == 0)
        def _():
            dwacc[...] = jnp.zeros_like(dwacc)
            db_ref[...] = jnp.zeros_like(db_ref)
            dlg_ref[...] = jnp.zeros_like(dlg_ref)
            dlb_ref[...] = jnp.zeros_like(dlb_ref)

        hbuf[0:HALO, :] = jnp.where(i > 0, _glu(uh_ref[...]), 0.0)
        hbuf[HALO:HALO + tm, :] = _glu(u_ref[...])
        for r in range(tm // CONV_RB):
            rows = slice(r * CONV_RB, (r + 1) * CONV_RB)
            n, rstd, zz = _ln_fwd(c_ref[rows, :] + b_ref[...], lg_ref[...], lb_ref[...])
            sg = _sigmoid(zz)
            dz = do_ref[rows, :].astype(F32) * (sg * (1.0 + zz * (1.0 - sg)))
            dlg_ref[...] += jnp.sum(dz * n, axis=0, keepdims=True)
            dlb_ref[...] += jnp.sum(dz, axis=0, keepdims=True)
            dn = dz * lg_ref[...]
            dc = rstd * (dn - jnp.mean(dn, axis=-1, keepdims=True) - n * jnp.mean(dn * n, axis=-1, keepdims=True))
            dc_ref[rows, :] = dc
            db_ref[...] += jnp.sum(dc, axis=0, keepdims=True)
            for j in range(CK):
                s = r * CONV_RB + HALO - (CK - 1) + j
                prod = dc * hbuf[s:s + CONV_RB, :]
                dwacc[j] += jnp.sum(prod.reshape(CONV_RB // 8, 8, CW), axis=0)

        @pl.when(i == nsteps - 1)
        def _():
            dw_ref[...] = jnp.sum(dwacc[...], axis=1)

    vec = pl.BlockSpec((1, CW), lambda i: (0, 0))
    return _pallas(
        body,
        out_shape=(_out((T, CW), F32), _out((HALO, CW), F32),
                   _out((1, CW), F32), _out((1, CW), F32), _out((1, CW), F32)),
        grid=(nsteps,),
        in_specs=[pl.BlockSpec((tm, CONV_COLS), lambda i: (i, 0)),
                  pl.BlockSpec((HALO, CONV_COLS), lambda i: (jnp.maximum(i * (tm // HALO) - 1, 0), 0)),
                  pl.BlockSpec((tm, CW), lambda i: (i, 0)), pl.BlockSpec((tm, CW), lambda i: (i, 0)), vec, vec, vec],
        out_specs=(pl.BlockSpec((tm, CW), lambda i: (i, 0)), pl.BlockSpec((HALO, CW), lambda i: (0, 0)), vec, vec, vec),
        scratch_shapes=[pltpu.VMEM((tm + HALO, CW), F32), pltpu.VMEM((HALO, 8, CW), F32)],
        compiler_params=_cparams(("arbitrary",), 10 * _nbytes((tm, CONV_COLS), F32)),
        name=name,
    )(u, u, conv, dout, b, lg, lb)


def _conv_bwd_du(u, dc, w, name):
    T = u.shape[0]
    tm = CONV_TM
    nsteps = T // tm

    def body(u_ref, dc_ref, dch_ref, w_ref, du_ref, dcbuf):
        i = pl.program_id(0)
        dcbuf[0:tm, :] = dc_ref[...]
        dcbuf[tm:tm + HALO, :] = jnp.where(i < nsteps - 1, dch_ref[...], 0.0)
        for r in range(tm // CONV_RB):
            rows = slice(r * CONV_RB, (r + 1) * CONV_RB)
            dh = _conv_taps(dcbuf, w_ref, r * CONV_RB, lambda j: (CK - 1) - j)
            a = u_ref[rows, 0:CW]
            sb = _sigmoid(u_ref[rows, CW:2 * CW])
            du_ref[rows, 0:CW] = (dh * sb).astype(du_ref.dtype)
            du_ref[rows, CW:2 * CW] = (dh * a * sb * (1.0 - sb)).astype(du_ref.dtype)

    return _pallas(
        body,
        out_shape=_out((T, CONV_COLS), BF16),
        grid=(nsteps,),
        in_specs=[pl.BlockSpec((tm, CONV_COLS), lambda i: (i, 0)),
                  pl.BlockSpec((tm, CW), lambda i: (i, 0)),
                  pl.BlockSpec((HALO, CW), lambda i: (jnp.minimum((i + 1) * (tm // HALO), T // HALO - 1), 0)),
                  pl.BlockSpec((HALO, CW), lambda i: (0, 0))],
        out_specs=pl.BlockSpec((tm, CONV_COLS), lambda i: (i, 0)),
        scratch_shapes=[pltpu.VMEM((tm + HALO, CW), F32)],
        compiler_params=_cparams(("arbitrary",), 8 * _nbytes((tm, CONV_COLS), F32)),
        name=name,
    )(u, dc, dc, w)


ATT_G = 4


def _att_load_kv(p_any, kbuf, vbuf, sems, T):
    kc = pltpu.make_async_copy(p_any.at[:, pl.ds(AW, AW)], kbuf.at[pl.ds(LEFT, T), :], sems.at[0])
    vc = pltpu.make_async_copy(p_any.at[:, pl.ds(2 * AW, AW)], vbuf.at[pl.ds(LEFT, T), :], sems.at[1])
    kc.start()
    vc.start()
    kbuf[0:LEFT, :] = jnp.zeros((LEFT, AW), BF16)
    vbuf[0:LEFT, :] = jnp.zeros((LEFT, AW), BF16)
    kc.wait()
    vc.wait()


ATT_QB = CHUNK * ATT_G
ATT_KB = LEFT + ATT_QB
REL_PAD = 384
TOEP = 1024


def _att_consts():
    m = np.arange(TOEP)
    d = ATT_KB - 1 - m
    idx = np.clip(d, -128, 128) + 128
    sel = (np.arange(REL_PAD)[:, None] == idx[None, :]) & (m[None, :] < ATT_QB + ATT_KB - 1)
    return jnp.asarray(sel.astype(np.float32))


def _att_build_bias(rel_ref, sel_ref, bias_scr):
    tr = jnp.dot(rel_ref[...], sel_ref[...], precision=HI, preferred_element_type=F32)
    qc = lax.broadcasted_iota(jnp.int32, (ATT_QB, ATT_KB), 0) // CHUNK
    kc = lax.broadcasted_iota(jnp.int32, (ATT_QB, ATT_KB), 1) // CHUNK
    band = (kc >= qc) & (kc <= qc + 8)
    for h in range(AH):
        rows = jnp.broadcast_to(tr[h:h + 1, :], (ATT_QB, TOEP))
        toep = pltpu.roll(rows, TOEP - (ATT_QB - 1), 1, stride=1, stride_axis=0)[:, 0:ATT_KB]
        bias_scr[h // 2, (h % 2) * ATT_QB:(h % 2 + 1) * ATT_QB, :] = jnp.where(band, toep, NEG)


def _att_probs(qst, kb, bias_p, n0):
    sc = lax.dot_general(qst, kb, _DIMS["nt"], preferred_element_type=F32) * (64 ** -0.5) + bias_p
    pos = lax.broadcasted_iota(jnp.int32, (2 * ATT_QB, ATT_KB), 1)
    sc = jnp.where(pos >= CHUNK * (8 - n0), sc, NEG)
    mx = jnp.max(sc, axis=-1, keepdims=True)
    ex = jnp.exp(sc - mx)
    return ex / jnp.sum(ex, axis=-1, keepdims=True)


def _head_stack(a2, lo):
    zero = jnp.zeros_like(a2)
    return jnp.concatenate([jnp.where(lo, a2, zero), jnp.where(lo, zero, a2)], axis=0)


def _att_fwd(p, rel, sel, name):
    T = p.shape[0]

    def body(q_ref, p_any, rel_ref, sel_ref, o_ref, kbuf, vbuf, bias_scr, sems):
        i = pl.program_id(0)

        @pl.when(i == 0)
        def _():
            _att_load_kv(p_any, kbuf, vbuf, sems, T)
            _att_build_bias(rel_ref, sel_ref, bias_scr)

        lo = lax.broadcasted_iota(jnp.int32, (ATT_QB, 128), 1) < 64
        n0 = i * ATT_G
        start = pl.multiple_of(i * ATT_QB, ATT_QB)
        for hp in range(AH // 2):
            cols = slice(hp * 128, (hp + 1) * 128)
            kb = kbuf[pl.ds(start, ATT_KB), cols]
            vb = vbuf[pl.ds(start, ATT_KB), cols]
            pr = _att_probs(_head_stack(q_ref[:, cols], lo), kb, bias_scr[hp], n0)
            pv = jnp.dot(pr.astype(BF16), vb, preferred_element_type=F32)
            o_ref[:, cols] = jnp.where(lo, pv[0:ATT_QB], pv[ATT_QB:2 * ATT_QB]).astype(o_ref.dtype)

    vm = 2 * _nbytes((T + LEFT, AW), BF16) + 8 * _nbytes((2 * ATT_QB, ATT_KB), F32) + (8 << 20)
    return _pallas(
        body,
        out_shape=_out((T, AW), BF16),
        grid=(T // ATT_QB,),
        in_specs=[pl.BlockSpec((ATT_QB, AW), lambda i: (i, 0)), pl.BlockSpec(memory_space=pl.ANY),
                  pl.BlockSpec((8, REL_PAD), lambda i: (0, 0)), pl.BlockSpec((REL_PAD, TOEP), lambda i: (0, 0))],
        out_specs=pl.BlockSpec((ATT_QB, AW), lambda i: (i, 0)),
        scratch_shapes=[pltpu.VMEM((T + LEFT, AW), BF16), pltpu.VMEM((T + LEFT, AW), BF16),
                        pltpu.VMEM((AH // 2, 2 * ATT_QB, ATT_KB), F32), pltpu.SemaphoreType.DMA((2,))],
        compiler_params=_cparams(("arbitrary",), vm),
        name=name,
    )(p, p, rel, sel)


def _att_bwd(p, do, rel, sel, name):
    T = p.shape[0]
    nsteps = T // ATT_QB

    def body(q_ref, p_any, do_ref, rel_ref, sel_ref, dp_any, drel_ref,
             kbuf, vbuf, dqbuf, dkbuf, dvbuf, bias_scr, dbias_scr, dtr_scr, sems):
        i = pl.program_id(0)

        @pl.when(i == 0)
        def _():
            _att_load_kv(p_any, kbuf, vbuf, sems, T)
            _att_build_bias(rel_ref, sel_ref, bias_scr)
            dkbuf[...] = jnp.zeros_like(dkbuf)
            dvbuf[...] = jnp.zeros_like(dvbuf)
            dbias_scr[...] = jnp.zeros_like(dbias_scr)

        lo = lax.broadcasted_iota(jnp.int32, (ATT_QB, 128), 1) < 64
        n0 = i * ATT_G
        start = pl.multiple_of(i * ATT_QB, ATT_QB)
        for hp in range(AH // 2):
            cols = slice(hp * 128, (hp + 1) * 128)
            kb = kbuf[pl.ds(start, ATT_KB), cols]
            vb = vbuf[pl.ds(start, ATT_KB), cols]
            qst = _head_stack(q_ref[:, cols], lo)
            dost = _head_stack(do_ref[:, cols].astype(BF16), lo)
            pr = _att_probs(qst, kb, bias_scr[hp], n0)
            dpr = lax.dot_general(dost, vb, _DIMS["nt"], preferred_element_type=F32)
            ds = pr * (dpr - jnp.sum(dpr * pr, axis=-1, keepdims=True))
            dbias_scr[hp] += ds
            dsb = (ds * (64 ** -0.5)).astype(BF16)
            dq = jnp.dot(dsb, kb, preferred_element_type=F32)
            dqbuf[pl.ds(start, ATT_QB), cols] = jnp.where(lo, dq[0:ATT_QB], dq[ATT_QB:2 * ATT_QB]).astype(BF16)
            dkbuf[pl.ds(start, ATT_KB), cols] += lax.dot_general(dsb, qst, _DIMS["tn"], preferred_element_type=F32)
            dvbuf[pl.ds(start, ATT_KB), cols] += lax.dot_general(pr.astype(BF16), dost, _DIMS["tn"], preferred_element_type=F32)

        @pl.when(i == nsteps - 1)
        def _():
            kbuf[pl.ds(LEFT, T), :] = dkbuf[pl.ds(LEFT, T), :].astype(BF16)
            vbuf[pl.ds(LEFT, T), :] = dvbuf[pl.ds(LEFT, T), :].astype(BF16)
            cps = [pltpu.make_async_copy(dqbuf, dp_any.at[:, pl.ds(0, AW)], sems.at[0]),
                   pltpu.make_async_copy(kbuf.at[pl.ds(LEFT, T), :], dp_any.at[:, pl.ds(AW, AW)], sems.at[1]),
                   pltpu.make_async_copy(vbuf.at[pl.ds(LEFT, T), :], dp_any.at[:, pl.ds(2 * AW, AW)], sems.at[2])]
            for cp in cps:
                cp.start()
            dtr_scr[...] = jnp.zeros_like(dtr_scr)
            ri = lax.broadcasted_iota(jnp.int32, (ATT_QB, ATT_QB), 0)
            ci = lax.broadcasted_iota(jnp.int32, (ATT_QB, ATT_QB), 1)
            flip = jnp.where(ri + ci == ATT_QB - 1, 1.0, 0.0)
            for h in range(AH):
                db = dbias_scr[h // 2, (h % 2) * ATT_QB:(h % 2 + 1) * ATT_QB, :]
                db = jnp.dot(flip, db, precision=HI, preferred_element_type=F32)
                wide = jnp.concatenate([db, jnp.zeros((ATT_QB, TOEP - ATT_KB), F32)], axis=1)
                diag = pltpu.roll(wide, 0, 1, stride=1, stride_axis=0)
                dtr_scr[h:h + 1, :] = jnp.sum(diag, axis=0, keepdims=True)
            drel_ref[...] = lax.dot_general(dtr_scr[...], sel_ref[...], _DIMS["nt"], precision=HI, preferred_element_type=F32)
            for cp in cps:
                cp.wait()

    vm = 3 * _nbytes((T + LEFT, AW), BF16) + 2 * _nbytes((T + LEFT, AW), F32) + 12 * _nbytes((2 * ATT_QB, ATT_KB), F32) + (8 << 20)
    return _pallas(
        body,
        out_shape=(_out((T, ATT_COLS), BF16), _out((8, REL_PAD), F32)),
        grid=(nsteps,),
        in_specs=[pl.BlockSpec((ATT_QB, AW), lambda i: (i, 0)), pl.BlockSpec(memory_space=pl.ANY),
                  pl.BlockSpec((ATT_QB, AW), lambda i: (i, 0)),
                  pl.BlockSpec((8, REL_PAD), lambda i: (0, 0)), pl.BlockSpec((REL_PAD, TOEP), lambda i: (0, 0))],
        out_specs=(pl.BlockSpec(memory_space=pl.ANY), pl.BlockSpec((8, REL_PAD), lambda i: (0, 0))),
        scratch_shapes=[pltpu.VMEM((T + LEFT, AW), BF16), pltpu.VMEM((T + LEFT, AW), BF16), pltpu.VMEM((T, AW), BF16),
                        pltpu.VMEM((T + LEFT, AW), F32), pltpu.VMEM((T + LEFT, AW), F32),
                        pltpu.VMEM((AH // 2, 2 * ATT_QB, ATT_KB), F32), pltpu.VMEM((AH // 2, 2 * ATT_QB, ATT_KB), F32),
                        pltpu.VMEM((8, TOEP), F32), pltpu.SemaphoreType.DMA((3,))],
        compiler_params=_cparams(("arbitrary",), vm),
        name=name,
    )(p, p, do, rel, sel)


def _layer_fwd(h, wl, consts, tag):
    p_gla, p_conv, p_att, xn = _mm_fan(h, [wl["w_gla"], wl["w_conv"], wl["w_att"]], mode="nt", out_dtypes=(F32, F32, BF16),
                                       norm_g=wl["norm_mix"], name=f"{tag}_proj")
    o_gla, states = _gla_fwd(p_gla, wl["wg"], wl["bg"], wl["gla_norm"], consts, f"{tag}_gla_fwd")
    o_conv, conv = _conv_fwd(p_conv, wl["w_dw"], wl["b_dw"], wl["ln_g"], wl["ln_b"], f"{tag}_conv_fwd")
    rel = jnp.pad(wl["rel_bias"], ((0, 8 - AH), (0, REL_PAD - N_REL)))
    o_att = _att_fwd(p_att, rel, consts[3], f"{tag}_att_fwd")
    h1 = _mm_sum([o_gla, o_conv, o_att], [wl["w_out_g"], wl["w_out_c"], wl["w_out_a"]], mode="nn", out_dtype=F32, extra=h,
                 name=f"{tag}_out")
    xn2 = _rmsnorm_fwd(h1, wl["norm_ffn"], f"{tag}_norm_ffn")
    u = _mm(xn2, wl["w_up"], mode="nn", out_dtype=BF16, tm=1024, tn=1024, b_chips=True, name=f"{tag}_mlp_up")
    h2 = _mm(u, wl["w_down"], mode="nn", out_dtype=F32, tm=1024, tk=2048, a_pro="relu2", epi="add", extra=h1,
             name=f"{tag}_mlp_down")
    saved = dict(h=h, xn=xn, p_gla=p_gla, p_conv=p_conv, p_att=p_att, states=states, o_gla=o_gla, o_conv=o_conv, conv=conv,
                 o_att=o_att, rel=rel, h1=h1, xn2=xn2, u=u)
    return h2, saved


def _layer_bwd(dh2, sv, wl, consts, tag, emit=lambda name, grad: None):
    g = {}
    du = _mm(dh2, wl["w_down"], mode="nt", out_dtype=BF16, tm=1024, tn=1024, epi="relu2grad", extra=sv["u"],
             name=f"{tag}_mlp_down_dx")
    g["w_down"] = _mm(sv["u"], dh2, mode="tn", out_dtype=F32, tm=2048, tn=1024, tk=512, a_pro="relu2", name=f"{tag}_mlp_down_dw")
    emit("w_down", g["w_down"].reshape(4, D_FF // 4, D))
    dxn2 = _mm(du, wl["w_up"], mode="nt", out_dtype=F32, tm=1024, tk=1024, b_chips=True, name=f"{tag}_mlp_up_dx")
    g["w_up"] = _mm(sv["xn2"], du, mode="tn", out_dtype=F32, tm=1024, tn=1024, tk=1024, out_chips=True, name=f"{tag}_mlp_up_dw")
    emit("w_up", g["w_up"])
    dh1, g["norm_ffn"] = _rmsnorm_bwd(dxn2, sv["h1"], wl["norm_ffn"], dh2, f"{tag}_norm_ffn_bwd")
    d_gla, d_conv, d_att = _mm_fan(dh1, [wl["w_out_g"], wl["w_out_c"], wl["w_out_a"]], mode="nt", out_dtypes=(F32, F32, F32),
                                   name=f"{tag}_out_dx")
    g["w_out_g"], g["w_out_c"], g["w_out_a"] = _mm_tn_multi([sv["o_gla"], sv["o_conv"], sv["o_att"], dh1],
                                                            [(0, 3), (1, 3), (2, 3)], name=f"{tag}_out_dw")
    emit("w_out", jnp.concatenate([g["w_out_g"], g["w_out_c"], g["w_out_a"]], axis=0).reshape(4, D // 4, D))
    dp_gla, g["wg"], g["bg"], g["gla_norm"] = _gla_bwd(sv["p_gla"], d_gla, sv["states"], wl["wg"], wl["bg"], wl["gla_norm"],
                                                       consts, f"{tag}_gla_bwd")
    dc, g["w_dw"], g["b_dw"], g["ln_g"], g["ln_b"] = _conv_bwd_dc(sv["p_conv"], sv["conv"], d_conv, wl["b_dw"], wl["ln_g"],
                                                                  wl["ln_b"], f"{tag}_conv_bwd_dc")
    dp_conv = _conv_bwd_du(sv["p_conv"], dc, wl["w_dw"], f"{tag}_conv_bwd_du")
    dp_att, drel = _att_bwd(sv["p_att"], d_att, sv["rel"], consts[3], f"{tag}_att_bwd")
    g["rel_bias"] = drel[0:AH, 0:N_REL]
    g["w_gla"], g["w_conv"], g["w_att"] = _mm_tn_multi([sv["xn"], dp_gla, dp_conv, dp_att], [(1, 0), (2, 0), (3, 0)],
                                                       name=f"{tag}_proj_dw")
    emit("w_in", _join_w_in_t(g))
    dh, g["norm_mix"] = _mm_sum([dp_gla, dp_conv, dp_att], [wl["w_gla"], wl["w_conv"], wl["w_att"]], mode="nn", out_dtype=F32,
                                norm_bwd=(sv["h"], wl["norm_mix"], dh1), name=f"{tag}_proj_dx")
    return dh, g


def _local_step(x, target, layers, norm_final, emit=lambda layer, name, grad: None):
    consts = _gla_consts() + (_att_consts(),)
    h = x
    saved = []
    for l, wl in enumerate(layers):
        h, sv = _layer_fwd(h, wl, consts, f"l{l}")
        saved.append(sv)
    loss, dh, g_final = _final_loss(h, norm_final, target, "final_loss")
    grads = [None] * len(layers)
    for l in reversed(range(len(layers))):
        dh, grads[l] = _layer_bwd(dh, saved[l], layers[l], consts, f"l{l}", functools.partial(emit, l))
    return loss, dh, grads, g_final


ANY = pl.BlockSpec(memory_space=pl.ANY)


def _place():
    x, y, c = lax.axis_index("x"), lax.axis_index("y"), lax.axis_index("c")
    chips = [(1 - x, y), (x, 1 - y), (1 - x, 1 - y)]
    return x, y, c, chips


def _shape(shape, dtype):
    return jax.ShapeDtypeStruct(tuple(shape), dtype)


def _remote(src, dst, send_sem, recv_sem, to):
    return pltpu.make_async_remote_copy(src_ref=src, dst_ref=dst, send_sem=send_sem, recv_sem=recv_sem,
                                        device_id=to, device_id_type=MESH)


class _Staged:
    def __init__(self, src, dst, buf, sems):
        self.load = pltpu.make_async_copy(src, buf, sems.at[0])
        self.store = pltpu.make_async_copy(buf, dst, sems.at[1])

    def start(self):
        self.load.start()

    def wait(self):
        self.load.wait()
        self.store.start()
        self.store.wait()


def _phase_gather_ici(src, then):
    R, C = src.shape
    rh = R // 2

    def copies(ins, outs, sems):
        x, y, c, chips = _place()
        me = 2 * x + y
        local = _Staged(ins[0], outs[0].at[me], sems[3], sems[2])
        sends = [_remote(ins[0].at[pl.ds(c * rh, rh), :], outs[0].at[me, pl.ds(c * rh, rh), :], sems[0].at[k], sems[1].at[k],
                         (px, py, c)) for k, (px, py) in enumerate(chips)]
        recvs = [_remote(outs[0].at[2 * px + py, pl.ds(c * rh, rh), :], outs[0].at[2 * px + py, pl.ds(c * rh, rh), :],
                         sems[0].at[k], sems[1].at[k], (px, py, c)) for k, (px, py) in enumerate(chips)]
        return local, sends, recvs

    def start(ins, outs, sems):
        local, sends, _ = copies(ins, outs, sems)
        local.start()
        for cp in sends:
            cp.start()

    def finish(ins, outs, sems):
        local, sends, recvs = copies(ins, outs, sems)
        for cp in recvs:
            cp.wait_recv()
        for cp in sends:
            cp.wait_send()
        local.wait()

    return _Comm([src], [_shape((4, R, C), src.dtype)], {}, [(3,), (3,), (2,), ((R, C), src.dtype)], start, finish, then)


def _phase_gather_d2d(part, then):
    _, R, C = part.shape
    rh = R // 2

    def copies(ins, outs, sems):
        x, y, c, chips = _place()
        sends = [_remote(ins[0].at[2 * px + py, pl.ds(c * rh, rh), :], outs[0].at[2 * px + py, pl.ds(c * rh, rh), :],
                         sems[0].at[k], sems[1].at[k], (x, y, 1 - c)) for k, (px, py) in enumerate(chips)]
        recvs = [_remote(outs[0].at[2 * px + py, pl.ds((1 - c) * rh, rh), :], outs[0].at[2 * px + py, pl.ds((1 - c) * rh, rh), :],
                         sems[0].at[k], sems[1].at[k], (x, y, 1 - c)) for k, (px, py) in enumerate(chips)]
        return sends, recvs

    def start(ins, outs, sems):
        for cp in copies(ins, outs, sems)[0]:
            cp.start()

    def finish(ins, outs, sems):
        sends, recvs = copies(ins, outs, sems)
        for cp in recvs:
            cp.wait_recv()
        for cp in sends:
            cp.wait_send()

    return _Comm([part], [_shape(part.shape, part.dtype)], {0: 0}, [(3,), (3,)], start, finish, then)


def _phase_pair_exchange(full, then):
    _, R, C = full.shape
    rh = R // 2

    def copy(ins, outs, sems):
        x, y, c, _ = _place()
        return _remote(ins[0].at[:, pl.ds((1 - c) * rh, rh), :], outs[0], sems[0].at[0], sems[1].at[0], (x, y, 1 - c))

    return _Comm([full], [_shape((4, rh, C), full.dtype)], {}, [(1,), (1,)],
                 lambda ins, outs, sems: copy(ins, outs, sems).start(),
                 lambda ins, outs, sems: copy(ins, outs, sems).wait(), then)


def _phase_chip_scatter(parts, then):
    def copies(ins, outs, sems):
        x, y, c, chips = _place()
        me = 2 * x + y
        local = _Staged(ins[0].at[me], outs[0].at[me], sems[3], sems[2])
        sends = [_remote(ins[0].at[2 * px + py], outs[0].at[me], sems[0].at[k], sems[1].at[k], (px, py, c))
                 for k, (px, py) in enumerate(chips)]
        recvs = [_remote(outs[0].at[2 * px + py], outs[0].at[2 * px + py], sems[0].at[k], sems[1].at[k], (px, py, c))
                 for k, (px, py) in enumerate(chips)]
        return local, sends, recvs

    def start(ins, outs, sems):
        local, sends, _ = copies(ins, outs, sems)
        local.start()
        for cp in sends:
            cp.start()

    def finish(ins, outs, sems):
        local, sends, recvs = copies(ins, outs, sems)
        for cp in recvs:
            cp.wait_recv()
        for cp in sends:
            cp.wait_send()
        local.wait()

    return _Comm([parts], [_shape(parts.shape, parts.dtype)], {}, [(3,), (3,), (2,), (parts.shape[1:], parts.dtype)],
                 start, finish, then)


def _phase_pair_allgather(half, layer, depth, into, then):
    rh, C = half.shape

    def copies(ins, outs, sems):
        x, y, c, _ = _place()
        mine = outs[0].at[layer, pl.ds(c * rh, rh), :]
        theirs = outs[0].at[layer, pl.ds((1 - c) * rh, rh), :]
        return (_Staged(ins[0], mine, sems[3], sems[2]),
                _remote(ins[0], mine, sems[0].at[0], sems[1].at[0], (x, y, 1 - c)),
                _remote(theirs, theirs, sems[0].at[0], sems[1].at[0], (x, y, 1 - c)))

    def start(ins, outs, sems):
        local, send, _ = copies(ins, outs, sems)
        local.start()
        send.start()

    def finish(ins, outs, sems):
        local, send, recv = copies(ins, outs, sems)
        recv.wait_recv()
        send.wait_send()
        local.wait()

    ins = [half] if into is None else [half, into]
    return _Comm(ins, [_shape((depth, 2 * rh, C), half.dtype)], {} if into is None else {1: 0},
                 [(1,), (1,), (2,), ((rh, C), half.dtype)], start, finish, then)


def _comm_only(comms, name):
    plan = _Plan()
    for c in comms:
        plan.at(name, c)
    saved, _PLAN[0] = _PLAN[0], plan
    try:
        def body(o_ref):
            o_ref[...] = jnp.zeros_like(o_ref)

        _pallas(body, out_shape=[jax.ShapeDtypeStruct((8, 128), F32)], in_specs=[],
                out_specs=[pl.BlockSpec(memory_space=pltpu.VMEM)], name=name)()
    finally:
        _PLAN[0] = saved


def _row_tile(rows, cols, itemsize=4, budget=1 << 20, mult=8):
    fits = [t for t in range(mult, rows + 1, mult) if rows % t == 0 and t * cols * itemsize <= budget]
    return max(fits) if fits else rows


def _pair_add(full, got, c, name):
    _, L, R, C = full.shape
    rh = R // 2
    tr = _row_tile(rh, C, budget=2 << 20, mult=16)
    nb = rh // tr

    def body(c_ref, a_ref, b_ref, o_ref):
        o_ref[...] = (a_ref[...] + b_ref[...]).astype(o_ref.dtype)

    grid_spec = pltpu.PrefetchScalarGridSpec(
        num_scalar_prefetch=1,
        grid=(4, L, nb),
        in_specs=[pl.BlockSpec((1, 1, tr, C), lambda j, l, i, c_ref: (j, l, c_ref[0] * nb + i, 0)),
                  pl.BlockSpec((1, 1, tr, C), lambda j, l, i, c_ref: (j, l, i, 0))],
        out_specs=pl.BlockSpec((1, 1, tr, C), lambda j, l, i, c_ref: (j, l, i, 0)),
    )
    return _pallas(
        body,
        out_shape=_out((4, L, rh, C), BF16),
        grid_spec=grid_spec,
        compiler_params=_cparams(("parallel", "parallel", "parallel"), 8 * tr * C * 4),
        name=name,
    )(jnp.reshape(c, (1,)).astype(jnp.int32), full, got)


def _sum_chips(parts, name):
    _, L, rh, C = parts.shape
    tr = _row_tile(rh, C, budget=2 << 20, mult=16)

    def body(p_ref, o_ref):
        acc = p_ref[0].astype(F32)
        for j in range(1, 4):
            acc = acc + p_ref[j].astype(F32)
        o_ref[...] = acc

    return _pallas(
        body,
        out_shape=_out((L, rh, C), F32),
        grid=(L, rh // tr),
        in_specs=[pl.BlockSpec((4, 1, tr, C), lambda l, i: (0, l, i, 0))],
        out_specs=pl.BlockSpec((1, tr, C), lambda l, i: (l, i, 0)),
        compiler_params=_cparams(("parallel", "parallel"), 16 * tr * C * 4),
        name=name,
    )(parts)


def _allreduce_small(v):
    R = v.shape[0]

    def body(v_ref, o_ref, slots, send_sems, recv_sems):
        x, y, c, _ = _place()
        me = 4 * x + 2 * y + c
        slots[me] = v_ref[...]
        cps = []
        for r in range(1, 8):
            px, py, pc = x ^ (r >> 2), y ^ ((r >> 1) & 1), c ^ (r & 1)
            cps.append(pltpu.make_async_remote_copy(src_ref=v_ref, dst_ref=slots.at[me], send_sem=send_sems.at[r - 1],
                                                    recv_sem=recv_sems.at[r - 1], device_id=(px, py, pc), device_id_type=MESH))
            cps[-1].start()
        for r in range(1, 8):
            px, py, pc = x ^ (r >> 2), y ^ ((r >> 1) & 1), c ^ (r & 1)
            theirs = slots.at[4 * px + 2 * py + pc]
            pltpu.make_async_remote_copy(src_ref=theirs, dst_ref=theirs, send_sem=send_sems.at[r - 1], recv_sem=recv_sems.at[r - 1],
                                         device_id=(px, py, pc), device_id_type=MESH).wait_recv()
        acc = slots[0]
        for j in range(1, 8):
            acc = acc + slots[j]
        o_ref[...] = acc
        for cp in cps:
            cp.wait_send()

    return pl.pallas_call(
        body,
        out_shape=jax.ShapeDtypeStruct(v.shape, F32),
        in_specs=[pl.BlockSpec(memory_space=pltpu.VMEM)],
        out_specs=pl.BlockSpec(memory_space=pltpu.VMEM),
        scratch_shapes=[pltpu.VMEM((8, R, 128), F32), pltpu.SemaphoreType.DMA((7,)), pltpu.SemaphoreType.DMA((7,))],
        name="allreduce_small",
    )(v)


def _adamw_math(w, g, m, v):
    m = ADAM_B1 * m + (1.0 - ADAM_B1) * g
    v = ADAM_B2 * v + (1.0 - ADAM_B2) * (g * g)
    m_hat = m / (1.0 - ADAM_B1 ** ADAM_STEP)
    v_hat = v / (1.0 - ADAM_B2 ** ADAM_STEP)
    delta = -ADAM_LR * (m_hat / (jnp.sqrt(v_hat) + ADAM_EPS) + ADAM_WD * w)
    return delta, m, v


def _adamw(ws, gs, ms, vs, name, budget=1 << 19):
    n = len(ws)
    tiles = [_row_tile(w.shape[1], w.shape[2], budget=budget) for w in ws]
    per_layer = [w.shape[1] // t for w, t in zip(ws, tiles)]
    steps = [w.shape[0] * p for w, p in zip(ws, per_layer)]
    starts = [sum(steps[:k]) for k in range(n)]

    def body(*refs):
        i = pl.program_id(0)
        for k in range(n):
            w_ref, g_ref, m_ref, v_ref = (refs[j * n + k] for j in range(4))
            outs = [refs[(4 + j) * n + k] for j in range(3)]

            @pl.when((i >= starts[k]) & (i < starts[k] + steps[k]))
            def _(w_ref=w_ref, g_ref=g_ref, m_ref=m_ref, v_ref=v_ref, outs=outs):
                outs[0][...], outs[1][...], outs[2][...] = _adamw_math(w_ref[...], g_ref[...], m_ref[...], v_ref[...])

    def spec(k):
        def index(i):
            local = jnp.clip(i - starts[k], 0, steps[k] - 1)
            return local // per_layer[k], local % per_layer[k], 0
        return pl.BlockSpec((None, tiles[k], ws[k].shape[2]), index)

    specs = [spec(k) for k in range(n)]
    outs = [_out(w.shape, F32) for w in ws]
    res = _pallas(
        body,
        out_shape=tuple(outs * 3),
        grid=(sum(steps),),
        in_specs=specs * 4,
        out_specs=tuple(specs * 3),
        compiler_params=_cparams(("arbitrary",), sum(16 * t * w.shape[2] * 4 for w, t in zip(ws, tiles))),
        name=name,
    )(*ws, *gs, *ms, *vs)
    return res[:n], res[n:2 * n], res[2 * n:]


def _adamw_slabs(w, g, m, v, name, slabs=59):
    n, L, C = w.shape
    assert n % slabs == 0

    def body(w_ref, g_ref, m_ref, v_ref, d_ref, nm_ref, nv_ref):
        d_ref[...], nm_ref[...], nv_ref[...] = _adamw_math(w_ref[...], g_ref[...], m_ref[...], v_ref[...])

    blk = pl.BlockSpec((slabs, L, C), lambda i: (i, 0, 0))
    out = _out(w.shape, F32)
    return _pallas(
        body,
        out_shape=(out, out, out),
        grid=(n // slabs,),
        in_specs=[blk] * 4,
        out_specs=(blk, blk, blk),
        compiler_params=_cparams(("parallel",), 16 * slabs * 8 * C * 4),
        name=name,
    )(w, g, m, v)


def _adamw_small(ws, gs, ms, vs):
    n = len(ws)

    def body(*refs):
        for t in range(n):
            w_ref, g_ref, m_ref, v_ref = (refs[k * n + t] for k in range(4))
            d_ref, nm_ref, nv_ref = (refs[(4 + k) * n + t] for k in range(3))
            d_ref[...], nm_ref[...], nv_ref[...] = _adamw_math(w_ref[...], g_ref[...], m_ref[...], v_ref[...])

    vmem = pl.BlockSpec(memory_space=pltpu.VMEM)
    outs = [jax.ShapeDtypeStruct(w.shape, F32) for w in ws]
    res = pl.pallas_call(
        body,
        out_shape=outs * 3,
        in_specs=[vmem] * (4 * n),
        out_specs=[vmem] * (3 * n),
        name="adamw_small",
    )(*ws, *gs, *ms, *vs)
    return res[:n], res[n:2 * n], res[2 * n:]


IN_SIZES = (192, 192, 384, 384, 16, 512, 384, 384, 384)
IN_OFFS = tuple(int(v) for v in np.cumsum((0,) + IN_SIZES))
SMALL = ("norm_mix", "w_gla_gate", "b_gla_gate", "gla_norm", "b_dw", "conv_ln_g", "conv_ln_b", "rel_bias", "norm_ffn")


def _pad_cols(a, n):
    return jnp.pad(a, ((0, 0), (0, n - a.shape[1])))


W_IN_SHARD = 708
W_IN_ROWS = 736


def _pad_rows(a, n):
    return jnp.pad(a, ((0, n - a.shape[0]), (0, 0)))


def _split_w_in_t(w):
    s = [w[IN_OFFS[i]:IN_OFFS[i + 1]] for i in range(9)]
    w_gla = jnp.concatenate([_pad_rows(s[0], KW), _pad_rows(s[1], KW), s[2], s[3], _pad_rows(s[4], LRW)], axis=0)
    return w_gla, s[5], jnp.concatenate(s[6:9], axis=0)


def _join_w_in_t(g):
    gg = g["w_gla"]
    full = jnp.concatenate([gg[0:192], gg[KW:KW + 192], gg[2 * KW:2 * KW + VW], gg[2 * KW + VW:2 * KW + 2 * VW],
                            gg[2 * KW + 2 * VW:2 * KW + 2 * VW + 16], g["w_conv"], g["w_att"]], axis=0)
    return jnp.pad(full.reshape(4, W_IN_SHARD, D), ((0, 0), (0, W_IN_ROWS - W_IN_SHARD), (0, 0)))


def _pack(arrs, rows):
    flat = jnp.concatenate([a.reshape(-1) for a in arrs])
    return jnp.pad(flat, (0, rows * 128 - flat.shape[0])).reshape(rows, 128)


def _unpack(packed, shapes):
    flat = packed.reshape(-1)
    out, off = [], 0
    for s in shapes:
        n = int(np.prod(s))
        out.append(flat[off:off + n].reshape(s))
        off += n
    return out


def kernel(x, norm_mix, w_in, w_gla_gate, b_gla_gate, gla_norm, w_dw, b_dw, conv_ln_g, conv_ln_b, rel_bias, w_out, norm_ffn, w_up, w_down, norm_final, loss_target, m_norm_mix, m_w_in, m_w_gla_gate, m_b_gla_gate, m_gla_norm, m_w_dw, m_b_dw, m_conv_ln_g, m_conv_ln_b, m_rel_bias, m_w_out, m_norm_ffn, m_w_up, m_w_down, m_norm_final, v_norm_mix, v_w_in, v_w_gla_gate, v_b_gla_gate, v_gla_norm, v_w_dw, v_b_dw, v_conv_ln_g, v_conv_ln_b, v_rel_bias, v_w_out, v_norm_ffn, v_w_up, v_w_down, v_norm_final):
    P = dict(norm_mix=norm_mix, w_in=w_in, w_gla_gate=w_gla_gate, b_gla_gate=b_gla_gate, gla_norm=gla_norm, w_dw=w_dw, b_dw=b_dw,
             conv_ln_g=conv_ln_g, conv_ln_b=conv_ln_b, rel_bias=rel_bias, w_out=w_out, norm_ffn=norm_ffn, w_up=w_up,
             w_down=w_down, norm_final=norm_final)
    Mo = dict(norm_mix=m_norm_mix, w_in=m_w_in, w_gla_gate=m_w_gla_gate, b_gla_gate=m_b_gla_gate, gla_norm=m_gla_norm, w_dw=m_w_dw,
              b_dw=m_b_dw, conv_ln_g=m_conv_ln_g, conv_ln_b=m_conv_ln_b, rel_bias=m_rel_bias, w_out=m_w_out, norm_ffn=m_norm_ffn,
              w_up=m_w_up, w_down=m_w_down, norm_final=m_norm_final)
    Vo = dict(norm_mix=v_norm_mix, w_in=v_w_in, w_gla_gate=v_w_gla_gate, b_gla_gate=v_b_gla_gate, gla_norm=v_gla_norm, w_dw=v_w_dw,
              b_dw=v_b_dw, conv_ln_g=v_conv_ln_g, conv_ln_b=v_conv_ln_b, rel_bias=v_rel_bias, w_out=v_w_out, norm_ffn=v_norm_ffn,
              w_up=v_w_up, w_down=v_w_down, norm_final=v_norm_final)
    depth = w_in.shape[0]
    xi, yi, ci = lax.axis_index("x"), lax.axis_index("y"), lax.axis_index("c")
    chip = 2 * xi + yi

    plan = _Plan()
    _PLAN[0] = plan
    layers = [dict(
        norm_mix=norm_mix[l][None], wg=jnp.pad(w_gla_gate[l], ((0, LRW - 16), (0, KW - 192))),
        bg=_pad_cols(b_gla_gate[l][None], KW), gla_norm=gla_norm[l][None], b_dw=b_dw[l][None], ln_g=conv_ln_g[l][None],
        ln_b=conv_ln_b[l][None], rel_bias=rel_bias[l], norm_ffn=norm_ffn[l][None]) for l in range(depth)]

    w_in_t, m_w_in_t, v_w_in_t = (jnp.transpose(a, (2, 0, 1)) for a in (w_in, m_w_in, v_w_in))

    def w_in_shard(l):
        return _pad_rows(w_in_t[:, l, :], W_IN_ROWS).astype(BF16)

    def have_w_in(l, full):
        rows = jnp.concatenate([full[j, 0:W_IN_SHARD] for j in range(4)], axis=0)
        layers[l]["w_gla"], layers[l]["w_conv"], layers[l]["w_att"] = _split_w_in_t(rows)

    def have_w_out(l, full):
        w = full.reshape(D, D)
        layers[l]["w_out_g"], layers[l]["w_out_c"], layers[l]["w_out_a"] = w[0:VW], w[VW:VW + CW], w[VW + CW:]

    def have_w_up(l, full):
        layers[l]["w_up"] = full

    def have_w_down(l, full):
        layers[l]["w_down"] = full.reshape(D_FF, D)

    def have_w_dw(full):
        taps = full.reshape(4, depth, HALO, CW // 4)
        for l in range(depth):
            layers[l]["w_dw"] = jnp.transpose(taps[:, l], (1, 0, 2)).reshape(HALO, CW)

    first_d2d = []

    def first_ici(shard, have):
        return _phase_gather_ici(shard, lambda outs: first_d2d.append(_phase_gather_d2d(outs[0], lambda done: have(done[0]))))

    w_dw_pad = jnp.pad(w_dw, ((0, 0), (0, HALO - CK), (0, 0))).reshape(depth * HALO, CW // 4)
    _comm_only([first_ici(w_in_shard(0), functools.partial(have_w_in, 0)), first_ici(w_dw_pad, have_w_dw)],
               "gather_first_ici")
    _comm_only(first_d2d, "gather_first_d2d")

    def gather_behind(shard, ici_call, d2d_call, have):
        plan.at(ici_call, _phase_gather_ici(
            shard, lambda outs: plan.at(d2d_call, _phase_gather_d2d(outs[0], lambda done: have(done[0])))))

    for l in range(depth):
        if l > 0:
            gather_behind(w_in_shard(l), f"l{l - 1}_mlp_up", f"l{l - 1}_mlp_down", functools.partial(have_w_in, l))
        gather_behind(w_out[l].astype(BF16), f"l{l - 1}_mlp_down" if l > 0 else "l0_proj", f"l{l}_gla_fwd",
                      functools.partial(have_w_out, l))
        if l > 0:
            gather_behind(w_up[l].astype(BF16), f"l{l}_proj", f"l{l}_gla_fwd", functools.partial(have_w_up, l))
            gather_behind(w_down[l].astype(BF16), f"l{l}_gla_fwd", f"l{l}_att_fwd", functools.partial(have_w_down, l))
        else:
            gather_behind(w_up[l].astype(BF16), f"l{l}_gla_fwd", f"l{l}_att_fwd", functools.partial(have_w_up, l))
            gather_behind(w_down[l].astype(BF16), f"l{l}_att_fwd", f"l{l}_mlp_up", functools.partial(have_w_down, l))

    reduced = {}
    last_swap = []

    def reduce_calls(name, l):
        if name == "w_down":
            return f"l{l}_mlp_up_dx", f"l{l}_gla_bwd", f"l{l}_conv_bwd_dc"
        if name == "w_up":
            return f"l{l}_out_dx", f"l{l}_att_bwd", f"l{l}_proj_dw"
        if name == "w_out":
            return f"l{l}_gla_bwd", f"l{l}_conv_bwd_dc", f"l{l}_att_bwd"
        if l > 0:
            return f"l{l}_proj_dx", f"l{l - 1}_mlp_down_dw", f"l{l - 1}_mlp_up_dx"
        return None, "l0_proj_dx", None

    def reduce_behind(l, name, full):
        calls = reduce_calls(name, l)

        def swapped(outs):
            pair = _pair_add(full[:, None], outs[0][:, None], ci, f"reduce_pair_add_{name}{l}")[:, 0]
            plan.at(calls[1], _phase_chip_scatter(pair, scattered))

        def scattered(outs):
            half = _sum_chips(outs[0][:, None], f"reduce_sum_chips_{name}{l}")[0]
            phase = _phase_pair_allgather(half, l, depth, reduced.get(name), gathered)
            if calls[2] is None:
                last_swap.append(phase)
            else:
                plan.at(calls[2], phase)

        def gathered(outs):
            reduced[name] = outs[0]

        if calls[0] is None:
            _comm_only([_phase_pair_exchange(full, swapped)], f"reduce_pair_exchange_{name}{l}")
        else:
            plan.at(calls[0], _phase_pair_exchange(full, swapped))

    loss_part, grad_x, grads, g_final = _local_step(x[0], loss_target[0], layers, norm_final[None], reduce_behind)
    loss = lax.psum(loss_part[0, 0], ("x", "y", "c"))

    G, delta, new_m, new_v = {}, {}, {}, {}
    early = ("w_down", "w_up", "w_out")
    for name in early:
        G[name] = reduced[name]
    ds, nms, nvs = _adamw([P[k] for k in early], [G[k] for k in early], [Mo[k] for k in early], [Vo[k] for k in early],
                          "adamw_early")
    for i, name in enumerate(early):
        delta[name], new_m[name], new_v[name] = ds[i], nms[i], nvs[i]
    _PLAN[0] = None
    assert not plan.by_call, sorted(plan.by_call)

    small_g = []
    for l in range(depth):
        g = grads[l]
        small_g += [g["norm_mix"], g["wg"][0:16, 0:192], g["bg"][:, 0:192], g["gla_norm"], g["b_dw"], g["ln_g"], g["ln_b"],
                    g["rel_bias"], g["norm_ffn"], g["w_dw"][0:CK]]
    small_g.append(g_final)
    small_shapes = [a.shape for a in small_g]
    n_small = sum(int(np.prod(s)) for s in small_shapes)
    rows = -(-n_small // 1024) * 8
    red = _unpack(_allreduce_small(_pack(small_g, rows)), small_shapes)
    per = len(SMALL) + 1
    for i, name in enumerate(SMALL):
        G[name] = jnp.stack([red[l * per + i].reshape(P[name].shape[1:]) for l in range(depth)])
    gw_dw_all = jnp.stack([red[l * per + len(SMALL)] for l in range(depth)])
    G["w_dw"] = lax.dynamic_slice_in_dim(gw_dw_all, chip * (CW // 4), CW // 4, axis=2)
    G["norm_final"] = red[-1].reshape(norm_final.shape)

    _comm_only(last_swap, "reduce_pair_allgather_last")
    back = lambda a: jnp.transpose(a, (1, 2, 0))
    g_in_t = jnp.transpose(reduced["w_in"][:, 0:W_IN_SHARD, :], (1, 0, 2))
    d_in, nm_in, nv_in = _adamw_slabs(w_in_t, g_in_t, m_w_in_t, v_w_in_t, "adamw_w_in")
    G["w_in"], delta["w_in"], new_m["w_in"], new_v["w_in"] = back(g_in_t), back(d_in), back(nm_in), back(nv_in)

    small_names = list(SMALL) + ["w_dw", "norm_final"]
    two_d = lambda a: a.reshape(-1, a.shape[-1])
    ds, nms, nvs = _adamw_small([two_d(P[k]) for k in small_names], [two_d(G[k]) for k in small_names],
                                [two_d(Mo[k]) for k in small_names], [two_d(Vo[k]) for k in small_names])
    for i, name in enumerate(small_names):
        shp = P[name].shape
        delta[name], new_m[name], new_v[name] = ds[i].reshape(shp), nms[i].reshape(shp), nvs[i].reshape(shp)

    order = ["norm_mix", "w_in", "w_gla_gate", "b_gla_gate", "gla_norm", "w_dw", "b_dw", "conv_ln_g", "conv_ln_b", "rel_bias",
             "w_out", "norm_ffn", "w_up", "w_down", "norm_final"]
    return (loss, grad_x[None], *[G[k] for k in order], *[delta[k] for k in order], *[new_m[k] for k in order],
            *[new_v[k] for k in order])
```

```python
import functools

import numpy as np
import jax
import jax.numpy as jnp
from jax import lax
from jax.experimental import pallas as pl
from jax.experimental.pallas import tpu as pltpu

F32 = jnp.float32
BF16 = jnp.bfloat16
HI = lax.Precision.HIGHEST

D = 1024
CHUNK = 64
GLA_DK, GLA_DV, GLA_H = 48, 96, 4
KW = 256
VW = 384
LRW = 128
GLA_TAU = 16.0
CW = 256
CK = 31
AW = 384
AH = 6
BAND = 576
LEFT = 512
D_FF = 4096
EPS = 1e-6
NEG = -1e30
N_REL = 257

GLA_COLS = 2 * KW + 2 * VW + LRW
CONV_COLS = 2 * CW
ATT_COLS = 3 * AW

ADAM_LR, ADAM_B1, ADAM_B2, ADAM_EPS, ADAM_WD, ADAM_STEP = 0.001, 0.9, 0.999, 1e-08, 0.01, 10

VMEM_CAP = 56 * 1024 * 1024
MESH = pl.DeviceIdType.MESH


def _cparams(sem, vmem_bytes):
    limit = int(min(VMEM_CAP, max(vmem_bytes * 5 // 4 + (4 << 20), 16 << 20)))
    return pltpu.CompilerParams(dimension_semantics=sem, vmem_limit_bytes=limit)


def _out(shape, dtype):
    return pltpu.HBM(tuple(shape), dtype)


class _Comm:
    def __init__(self, ins, outs, aliases, sems, start, finish, then=None):
        self.ins, self.outs, self.aliases, self.sems = list(ins), list(outs), dict(aliases), list(sems)
        self.start, self.finish, self.then = start, finish, then


class _Plan:
    def __init__(self):
        self.by_call = {}

    def at(self, call, comm):
        self.by_call.setdefault(call, []).append(comm)

    def take(self, call):
        return self.by_call.pop(call, [])


_PLAN = [None]


def _pin(a):
    return pltpu.with_memory_space_constraint(a, pltpu.HBM) if jnp.issubdtype(a.dtype, jnp.floating) else a


def _pallas(body, **kw):
    comms = _PLAN[0].take(kw.get("name")) if _PLAN[0] is not None else []
    if not comms:
        call = pl.pallas_call(body, **kw)
        return lambda *args: call(*[_pin(a) for a in args])

    grid = tuple(kw.get("grid", ()))
    single = not isinstance(kw["out_shape"], (tuple, list))
    out_shape = [kw["out_shape"]] if single else list(kw["out_shape"])
    out_specs = [kw["out_specs"]] if single else list(kw["out_specs"])
    in_specs = list(kw["in_specs"])
    scratch = list(kw.get("scratch_shapes", ()))
    n_in, n_out, n_scr = len(in_specs), len(out_shape), len(scratch)
    c_in = sum(len(c.ins) for c in comms)
    c_out = sum(len(c.outs) for c in comms)
    aliases = dict(kw.get("input_output_aliases", {}))
    i0, o0 = n_in, n_out
    for c in comms:
        for i, o in c.aliases.items():
            aliases[i0 + i] = o0 + o
        i0 += len(c.ins)
        o0 += len(c.outs)

    def wrapped(*refs):
        ins, c_ins = refs[:n_in], refs[n_in:n_in + c_in]
        outs, c_outs = refs[n_in + c_in:n_in + c_in + n_out], refs[n_in + c_in + n_out:n_in + c_in + n_out + c_out]
        scr, c_sems = refs[n_in + c_in + n_out + c_out:][:n_scr], refs[n_in + c_in + n_out + c_out + n_scr:]

        def each(what):
            i0 = o0 = s0 = 0
            for c in comms:
                getattr(c, what)(c_ins[i0:i0 + len(c.ins)], c_outs[o0:o0 + len(c.outs)], c_sems[s0:s0 + len(c.sems)])
                i0, o0, s0 = i0 + len(c.ins), o0 + len(c.outs), s0 + len(c.sems)

        if grid:
            first = functools.reduce(jnp.logical_and, [pl.program_id(a) == 0 for a in range(len(grid))])
            last = functools.reduce(jnp.logical_and, [pl.program_id(a) == grid[a] - 1 for a in range(len(grid))])
            pl.when(first)(lambda: each("start"))
            body(*ins, *outs, *scr)
            pl.when(last)(lambda: each("finish"))
        else:
            each("start")
            body(*ins, *outs, *scr)
            each("finish")

    kw = dict(kw)
    kw["in_specs"] = in_specs + [ANY] * c_in
    kw["out_shape"] = out_shape + [_out(s.shape, s.dtype) for c in comms for s in c.outs]
    kw["out_specs"] = out_specs + [ANY] * c_out
    staging = [s for c in comms for s in c.sems if len(s) == 2 and not isinstance(s[1], int)]
    kw["scratch_shapes"] = scratch + [pltpu.VMEM(*s) if s in staging else pltpu.SemaphoreType.DMA(s) for c in comms for s in c.sems]
    kw["input_output_aliases"] = aliases
    extra = sum(_nbytes(*s) for s in staging)
    old = kw.get("compiler_params")
    limit = (old.vmem_limit_bytes if old is not None else 16 << 20) + extra
    kw["compiler_params"] = pltpu.CompilerParams(
        dimension_semantics=old.dimension_semantics if old is not None else None, vmem_limit_bytes=int(min(VMEM_CAP, limit)))
    call = pl.pallas_call(wrapped, **kw)

    def run(*args):
        res = call(*[_pin(a) for a in args], *[_pin(a) for c in comms for a in c.ins])
        o0 = n_out
        for c in comms:
            if c.then is not None:
                c.then(res[o0:o0 + len(c.outs)])
            o0 += len(c.outs)
        return res[0] if single else res[:n_out]

    return run


def _nbytes(shape, dtype):
    return int(np.prod(shape)) * jnp.dtype(dtype).itemsize


def _sigmoid(x):
    return 1.0 / (1.0 + jnp.exp(-x))


_DIMS = {"nn": (((1,), (0,)), ((), ())), "nt": (((1,), (1,)), ((), ())), "tn": (((0,), (0,)), ((), ()))}


def _mm(a, b, *, mode, out_dtype, name, tm=512, tn=None, tk=None, a_pro=None, epi=None, extra=None,
        b_chips=False, out_chips=False, norm_g=None, norm_bwd=None):
    b2 = (b.shape[1], 4 * b.shape[2]) if b_chips else b.shape
    if mode == "nn":
        (M, K), (K2, N) = a.shape, b2
    elif mode == "nt":
        (M, K), (N, K2) = a.shape, b2
    else:
        (K, M), (K2, N) = a.shape, b2
    assert K == K2, (a.shape, b.shape, mode)
    tm = min(tm, M)
    tn = N if tn is None else min(tn, N)
    tk = K if tk is None else min(tk, K)
    assert M % tm == 0 and N % tn == 0 and K % tk == 0, (M, N, K, tm, tn, tk)
    nk = K // tk
    a_blk = (tk, tm) if mode == "tn" else (tm, tk)
    a_map = (lambda i, j, k: (k, i)) if mode == "tn" else (lambda i, j, k: (i, k))
    b_blk = (tn, tk) if mode == "nt" else (tk, tn)
    b_map = (lambda i, j, k: (j, k)) if mode == "nt" else (lambda i, j, k: (k, j))
    if b_chips:
        per = b.shape[2] // b_blk[1]
        assert b.shape[2] % b_blk[1] == 0 and mode != "tn"
        flat_map = b_map
        b_map = lambda i, j, k: (flat_map(i, j, k)[1] // per, flat_map(i, j, k)[0], flat_map(i, j, k)[1] % per)
        b_blk = (None,) + b_blk
    in_specs = [pl.BlockSpec(a_blk, a_map), pl.BlockSpec(b_blk, b_map)]
    args = [a, b]
    if epi is not None:
        in_specs.append(pl.BlockSpec((tm, tn), lambda i, j, k: (i, j)))
        args.append(extra)
    if norm_g is not None:
        assert nk == 1 and mode != "tn"
        in_specs.append(pl.BlockSpec((1, K), lambda i, j, k: (0, 0)))
        args.append(norm_g)
    if norm_bwd is not None:
        assert tn == N
        row = pl.BlockSpec((tm, N), lambda i, j, k: (i, 0))
        in_specs += [row, pl.BlockSpec((1, N), lambda i, j, k: (0, 0)), row]
        args += list(norm_bwd)

    def body(*refs):
        it = iter(refs)
        a_ref, b_ref = next(it), next(it)
        e_ref = next(it) if epi is not None else None
        ng_ref = next(it) if norm_g is not None else None
        h_ref, g_ref, dres_ref = (next(it), next(it), next(it)) if norm_bwd is not None else (None, None, None)
        o_ref = next(it)
        xn_ref = next(it) if norm_g is not None else None
        dg_ref = next(it) if norm_bwd is not None else None
        if norm_bwd is not None:
            @pl.when((pl.program_id(0) == 0) & (pl.program_id(2) == 0))
            def _():
                dg_ref[...] = jnp.zeros_like(dg_ref)

        av = a_ref[...]
        if a_pro == "relu2":
            af = jnp.maximum(av.astype(F32), 0.0)
            av = af * af
        if norm_g is not None:
            av = (av * lax.rsqrt(jnp.mean(av * av, axis=-1, keepdims=True) + EPS) * ng_ref[...]).astype(BF16)
            xn_ref[...] = av
        p = lax.dot_general(av.astype(BF16), b_ref[...].astype(BF16), _DIMS[mode], preferred_element_type=F32)

        def finish(acc):
            if epi == "add":
                acc = acc + e_ref[...].astype(F32)
            elif epi == "relu2grad":
                acc = acc * (2.0 * jnp.maximum(e_ref[...].astype(F32), 0.0))
            if norm_bwd is None:
                o_ref[...] = acc.astype(o_ref.dtype)
                return

            x = h_ref[...]
            r = lax.rsqrt(jnp.mean(x * x, axis=-1, keepdims=True) + EPS)
            gy = acc * g_ref[...]
            dot = jnp.mean(x * gy, axis=-1, keepdims=True)
            o_ref[...] = dres_ref[...] + r * gy - x * (r * r * r * dot)
            dg_ref[...] += jnp.sum(acc * x * r, axis=0, keepdims=True)

        if nk == 1:
            finish(p)
        else:
            acc_ref = refs[-1]
            k = pl.program_id(2)

            @pl.when(k == 0)
            def _():
                acc_ref[...] = p

            @pl.when(k > 0)
            def _():
                acc_ref[...] += p

            @pl.when(k == nk - 1)
            def _():
                finish(acc_ref[...])

    vm = 2 * (_nbytes(a_blk, a.dtype) + _nbytes((tk, tn), b.dtype) + _nbytes((tm, tn), out_dtype))
    vm += 3 * _nbytes((tm, tn), F32)
    if epi is not None:
        vm += 2 * _nbytes((tm, tn), extra.dtype)
    if out_chips:
        per_out = N // 4 // tn
        assert N % (4 * tn) == 0
        out_shape = _out((4, M, N // 4), out_dtype)
        out_spec = pl.BlockSpec((None, tm, tn), lambda i, j, k: (j // per_out, i, j % per_out))
    else:
        out_shape = _out((M, N), out_dtype)
        out_spec = pl.BlockSpec((tm, tn), lambda i, j, k: (i, j))
    sem = ("parallel", "parallel", "arbitrary")
    if norm_g is not None:
        out_shape, out_spec = (out_shape, _out((M, K), BF16)), (out_spec, pl.BlockSpec((tm, K), lambda i, j, k: (i, 0)))
        vm += 4 * _nbytes((tm, K), F32)
    if norm_bwd is not None:
        out_shape, out_spec = (out_shape, _out((1, N), F32)), (out_spec, pl.BlockSpec((1, N), lambda i, j, k: (0, 0)))
        sem = ("arbitrary", "arbitrary", "arbitrary")
        vm += 8 * _nbytes((tm, N), F32)
    return _pallas(
        body,
        out_shape=out_shape,
        grid=(M // tm, N // tn, nk),
        in_specs=in_specs,
        out_specs=out_spec,
        scratch_shapes=[pltpu.VMEM((tm, tn), F32)] if nk > 1 else [],
        compiler_params=_cparams(sem, vm),
        name=name,
    )(*args)


def _mm_fan(a, bs, *, mode, out_dtypes, name, tm=512, norm_g=None):
    M, K = a.shape
    ns = [b.shape[1] if mode == "nn" else b.shape[0] for b in bs]
    n = len(bs)
    first_out = 1 + n + (norm_g is not None)

    def body(*refs):
        if norm_g is None:
            av = refs[0][...].astype(BF16)
        else:
            x = refs[0][...]
            av = (x * lax.rsqrt(jnp.mean(x * x, axis=-1, keepdims=True) + EPS) * refs[1 + n][...]).astype(BF16)
            refs[first_out + n][...] = av
        for i in range(n):
            refs[first_out + i][...] = lax.dot_general(av, refs[1 + i][...], _DIMS[mode],
                                                       preferred_element_type=F32).astype(refs[first_out + i].dtype)

    vm = 4 * _nbytes((tm, K), F32) + sum(2 * _nbytes(b.shape, b.dtype) + 3 * _nbytes((tm, nn), F32) for b, nn in zip(bs, ns))
    in_specs = [pl.BlockSpec((tm, K), lambda i: (i, 0))] + [pl.BlockSpec(b.shape, lambda i: (0, 0)) for b in bs]
    out_shape = [_out((M, nn), dt) for nn, dt in zip(ns, out_dtypes)]
    out_specs = [pl.BlockSpec((tm, nn), lambda i: (i, 0)) for nn in ns]
    args = [a, *bs]
    if norm_g is not None:
        in_specs.append(pl.BlockSpec((1, K), lambda i: (0, 0)))
        out_shape.append(_out((M, K), BF16))
        out_specs.append(pl.BlockSpec((tm, K), lambda i: (i, 0)))
        args.append(norm_g)
    return _pallas(
        body,
        out_shape=tuple(out_shape),
        grid=(M // tm,),
        in_specs=in_specs,
        out_specs=tuple(out_specs),
        compiler_params=_cparams(("parallel",), vm),
        name=name,
    )(*args)


def _mm_sum(as_, bs, *, mode, out_dtype, name, extra=None, norm_bwd=None, tm=512):
    M = as_[0].shape[0]
    N = bs[0].shape[1] if mode == "nn" else bs[0].shape[0]
    n = len(as_)

    def body(*refs):
        acc = None
        for i in range(n):
            p = lax.dot_general(refs[i][...].astype(BF16), refs[n + i][...], _DIMS[mode], preferred_element_type=F32)
            acc = p if acc is None else acc + p
        if extra is not None:
            acc = acc + refs[2 * n][...].astype(F32)
        if norm_bwd is None:
            refs[-1][...] = acc.astype(refs[-1].dtype)
        else:
            h_ref, g_ref, dres_ref, dh_ref, dg_ref = refs[-5:]

            @pl.when(pl.program_id(0) == 0)
            def _():
                dg_ref[...] = jnp.zeros_like(dg_ref)

            x = h_ref[...]
            r = lax.rsqrt(jnp.mean(x * x, axis=-1, keepdims=True) + EPS)
            gy = acc * g_ref[...]
            dot = jnp.mean(x * gy, axis=-1, keepdims=True)
            dh_ref[...] = dres_ref[...] + r * gy - x * (r * r * r * dot)
            dg_ref[...] += jnp.sum(acc * x * r, axis=0, keepdims=True)

    row = pl.BlockSpec((tm, N), lambda i: (i, 0))
    in_specs = [pl.BlockSpec((tm, a.shape[1]), lambda i: (i, 0)) for a in as_]
    in_specs += [pl.BlockSpec(b.shape, lambda i: (0, 0)) for b in bs]
    args = list(as_) + list(bs)
    if extra is not None:
        in_specs.append(row)
        args.append(extra)
    vm = sum(2 * _nbytes((tm, a.shape[1]), a.dtype) for a in as_) + sum(2 * _nbytes(b.shape, b.dtype) for b in bs)
    vm += 8 * _nbytes((tm, N), F32)
    if norm_bwd is None:
        out_shape, out_specs, sem = _out((M, N), out_dtype), row, "parallel"
    else:
        vec = pl.BlockSpec((1, N), lambda i: (0, 0))
        in_specs += [row, vec, row]
        args += list(norm_bwd)
        out_shape, out_specs, sem = (_out((M, N), F32), _out((1, N), F32)), (row, vec), "arbitrary"
        vm += 8 * _nbytes((tm, N), F32)
    return _pallas(
        body,
        out_shape=out_shape,
        grid=(M // tm,),
        in_specs=in_specs,
        out_specs=out_specs,
        compiler_params=_cparams((sem,), vm),
        name=name,
    )(*args)


def _mm_tn_multi(ops, pairs, *, name, tk=512):
    T = ops[0].shape[0]
    n, m = len(ops), len(pairs)
    shapes = [(ops[a].shape[1], ops[b].shape[1]) for a, b in pairs]

    def body(*refs):
        vals = [refs[i][...].astype(BF16) for i in range(n)]
        first = pl.program_id(0) == 0
        for j, (a, b) in enumerate(pairs):
            p = lax.dot_general(vals[a], vals[b], _DIMS["tn"], preferred_element_type=F32)
            o_ref = refs[n + j]

            @pl.when(first)
            def _(o_ref=o_ref, p=p):
                o_ref[...] = p

            @pl.when(jnp.logical_not(first))
            def _(o_ref=o_ref, p=p):
                o_ref[...] += p

    vm = sum(2 * _nbytes((tk, o.shape[1]), o.dtype) for o in ops) + sum(3 * _nbytes(s, F32) for s in shapes)
    return _pallas(
        body,
        out_shape=tuple(_out(s, F32) for s in shapes),
        grid=(T // tk,),
        in_specs=[pl.BlockSpec((tk, o.shape[1]), lambda k: (k, 0)) for o in ops],
        out_specs=tuple(pl.BlockSpec(s, lambda k: (0, 0)) for s in shapes),
        compiler_params=_cparams(("arbitrary",), vm),
        name=name,
    )(*ops)


def _rmsnorm_fwd(h, g, name, tm=512):
    T = h.shape[0]

    def body(h_ref, g_ref, o_ref):
        x = h_ref[...]
        r = lax.rsqrt(jnp.mean(x * x, axis=-1, keepdims=True) + EPS)
        o_ref[...] = (x * r * g_ref[...]).astype(o_ref.dtype)

    return _pallas(
        body,
        out_shape=_out((T, D), BF16),
        grid=(T // tm,),
        in_specs=[pl.BlockSpec((tm, D), lambda i: (i, 0)), pl.BlockSpec((1, D), lambda i: (0, 0))],
        out_specs=pl.BlockSpec((tm, D), lambda i: (i, 0)),
        compiler_params=_cparams(("parallel",), 8 * _nbytes((tm, D), F32)),
        name=name,
    )(h, g)


def _rmsnorm_bwd(dxn, h, g, dres, name, tm=512):
    T = h.shape[0]

    def body(dxn_ref, h_ref, g_ref, dres_ref, dh_ref, dg_ref):
        @pl.when(pl.program_id(0) == 0)
        def _():
            dg_ref[...] = jnp.zeros_like(dg_ref)

        x = h_ref[...]
        dy = dxn_ref[...].astype(F32)
        r = lax.rsqrt(jnp.mean(x * x, axis=-1, keepdims=True) + EPS)
        gy = dy * g_ref[...]
        dot = jnp.mean(x * gy, axis=-1, keepdims=True)
        dh_ref[...] = dres_ref[...] + r * gy - x * (r * r * r * dot)
        dg_ref[...] += jnp.sum(dy * x * r, axis=0, keepdims=True)

    row = pl.BlockSpec((tm, D), lambda i: (i, 0))
    vec = pl.BlockSpec((1, D), lambda i: (0, 0))
    return _pallas(
        body,
        out_shape=(_out((T, D), F32), _out((1, D), F32)),
        grid=(T // tm,),
        in_specs=[row, row, vec, row],
        out_specs=(row, vec),
        compiler_params=_cparams(("arbitrary",), 12 * _nbytes((tm, D), F32)),
        name=name,
    )(dxn, h, g, dres)


def _final_loss(h, g, target, name, tm=512):
    T = h.shape[0]

    def body(h_ref, g_ref, t_ref, loss_ref, dh_ref, dg_ref):
        @pl.when(pl.program_id(0) == 0)
        def _():
            dg_ref[...] = jnp.zeros_like(dg_ref)
            loss_ref[...] = jnp.zeros_like(loss_ref)

        x = h_ref[...]
        gg = g_ref[...]
        r = lax.rsqrt(jnp.mean(x * x, axis=-1, keepdims=True) + EPS)
        y = x * r * gg
        e = y - t_ref[...]
        loss_ref[...] += 0.5 * jnp.sum(jnp.mean(e * e, axis=-1, keepdims=True), axis=0, keepdims=True)
        dy = e * (1.0 / D)
        gy = dy * gg
        dot = jnp.mean(x * gy, axis=-1, keepdims=True)
        dh_ref[...] = r * gy - x * (r * r * r * dot)
        dg_ref[...] += jnp.sum(dy * x * r, axis=0, keepdims=True)

    row = pl.BlockSpec((tm, D), lambda i: (i, 0))
    vec = pl.BlockSpec((1, D), lambda i: (0, 0))
    one = pl.BlockSpec((1, 1), lambda i: (0, 0))
    return _pallas(
        body,
        out_shape=(_out((1, 1), F32), _out((T, D), F32), _out((1, D), F32)),
        grid=(T // tm,),
        in_specs=[row, vec, row],
        out_specs=(one, row, vec),
        compiler_params=_cparams(("arbitrary",), 12 * _nbytes((tm, D), F32)),
        name=name,
    )(h, g, target)


GLA_G = 8


def _gla_consts():
    i = np.arange(KW)[:, None]
    j = np.arange(VW)[None, :]
    mask = ((i // GLA_DK) == (j // GLA_DV)) & (i < GLA_H * GLA_DK)
    a = np.arange(VW)
    hm = ((a[:, None] // GLA_DV) == (a[None, :] // GLA_DV)).astype(np.float32)
    c = np.arange(CHUNK)
    low = (c[:, None] >= c[None, :]).astype(np.float32)
    return jnp.asarray(mask.astype(np.float32)), jnp.asarray(hm, BF16), jnp.asarray(low, BF16)


def _split(x):
    hi = x.astype(BF16)
    return hi, (x - hi.astype(F32)).astype(BF16)


def _dot_sel(a, b, dims, split):
    if split == "a":
        hi, lo = _split(a)
        return (lax.dot_general(hi, b, dims, preferred_element_type=F32) + lax.dot_general(lo, b, dims, preferred_element_type=F32))
    hi, lo = _split(b)
    return (lax.dot_general(a, hi, dims, preferred_element_type=F32) + lax.dot_general(a, lo, dims, preferred_element_type=F32))


def _dot3(a, b, dims):
    ah, al = _split(a)
    bh, bl = _split(b)
    return (lax.dot_general(ah, bh, dims, preferred_element_type=F32) + lax.dot_general(al, bh, dims, preferred_element_type=F32)
            + lax.dot_general(ah, bl, dims, preferred_element_type=F32))


def _dot3s(a_s, b_s, dims):
    (ah, al), (bh, bl) = a_s, b_s
    return (lax.dot_general(ah, bh, dims, preferred_element_type=F32) + lax.dot_general(al, bh, dims, preferred_element_type=F32)
            + lax.dot_general(ah, bl, dims, preferred_element_type=F32))


def _gla_group_common(p_ref, wg, bg):
    lr_s = _split(p_ref[:, 2 * KW + 2 * VW:GLA_COLS])
    wg_s = _split(wg)
    z = _dot3s(lr_s, wg_s, _DIMS["nn"]) + bg
    la = (jnp.minimum(z, 0.0) - jnp.log(1.0 + jnp.exp(-jnp.abs(z)))) * (1.0 / GLA_TAU)
    return lr_s, wg_s, z, _split(la)


def _gla_chunk_common(p_ref, rows, la_s, low, ones_v):
    q = p_ref[rows, 0:KW]
    k = p_ref[rows, KW:2 * KW]
    v = p_ref[rows, 2 * KW:2 * KW + VW]
    g = p_ref[rows, 2 * KW + VW:2 * KW + 2 * VW]
    la_h, la_l = la_s[0][rows], la_s[1][rows]
    cum = jnp.dot(low, la_h, preferred_element_type=F32) + jnp.dot(low, la_l, preferred_element_type=F32)
    endb = cum[CHUNK - 1:CHUNK, :]
    w = jnp.exp(endb - cum)
    a_full = jnp.exp(lax.dot_general(la_h, ones_v, _DIMS["tn"], preferred_element_type=F32)
                     + lax.dot_general(la_l, ones_v, _DIMS["tn"], preferred_element_type=F32))
    return q, k, v, g, w, endb, a_full


def _gla_fwd(p, wg, bg, gn, consts, name):
    T = p.shape[0]
    rb = CHUNK * GLA_G
    ng = T // rb
    mask, hm, low = consts[:3]
    scale = GLA_DK ** -0.5

    def body(p_ref, wg_ref, bg_ref, gn_ref, m_ref, hm_ref, l_ref, o_ref, st_ref, s_ref):
        @pl.when(pl.program_id(0) == 0)
        def _():
            s_ref[...] = jnp.zeros_like(s_ref)

        wg_v, bg_v, gn_v = wg_ref[...], bg_ref[...], gn_ref[...]
        ones_v = jnp.ones((CHUNK, VW), BF16)
        s_new = s_ref[...]
        _, _, _, la_s = _gla_group_common(p_ref, wg_v, bg_v)
        outs = []
        for c in range(GLA_G):
            rows = slice(c * CHUNK, (c + 1) * CHUNK)
            q, k, v, _, w, _, a_full = _gla_chunk_common(p_ref, rows, la_s, l_ref[...], ones_v)
            kd = (k * w).astype(BF16)
            kv = lax.dot_general(kd, v.astype(BF16), _DIMS["tn"], preferred_element_type=F32) * m_ref[...]
            s_new = a_full * s_new + kv
            st_ref[c] = s_new
            outs.append(jnp.dot((q * scale).astype(BF16), s_new.astype(BF16), preferred_element_type=F32))
        s_ref[...] = s_new
        o = jnp.concatenate(outs, axis=0)
        g = p_ref[:, 2 * KW + VW:2 * KW + 2 * VW]
        ms = _dot_sel(o * o, hm_ref[...], _DIMS["nn"], "a") * (1.0 / GLA_DV)
        o_ref[...] = (o * lax.rsqrt(ms + EPS) * gn_v * (g * _sigmoid(g))).astype(o_ref.dtype)

    full = lambda shape: pl.BlockSpec(shape, lambda i: tuple(0 for _ in shape))
    vm = 2 * _nbytes((rb, GLA_COLS), F32) + 2 * _nbytes((GLA_G, KW, VW), F32) + 12 * _nbytes((KW, VW), F32)
    return _pallas(
        body,
        out_shape=(_out((T, VW), BF16), _out((T // CHUNK, KW, VW), F32)),
        grid=(ng,),
        in_specs=[pl.BlockSpec((rb, GLA_COLS), lambda i: (i, 0)), full((LRW, KW)), full((1, KW)), full((1, VW)),
                  full((KW, VW)), full((VW, VW)), full((CHUNK, CHUNK))],
        out_specs=(pl.BlockSpec((rb, VW), lambda i: (i, 0)), pl.BlockSpec((GLA_G, KW, VW), lambda i: (i, 0, 0))),
        scratch_shapes=[pltpu.VMEM((KW, VW), F32)],
        compiler_params=_cparams(("arbitrary",), vm),
        name=name,
    )(p, wg, bg, gn, mask, hm, low)


def _gla_bwd(p, dy, states, wg, bg, gn, consts, name):
    T = p.shape[0]
    rb = CHUNK * GLA_G
    ng = T // rb
    mask, hm, low = consts[:3]
    scale = GLA_DK ** -0.5

    def body(p_ref, dy_ref, st_ref, sp_ref, wg_ref, bg_ref, gn_ref, m_ref, hm_ref, l_ref,
             dp_ref, dwg_ref, dbg_ref, dgn_ref, ga_ref):
        step = pl.program_id(0)

        @pl.when(step == 0)
        def _():
            ga_ref[...] = jnp.zeros_like(ga_ref)
            dwg_ref[...] = jnp.zeros_like(dwg_ref)
            dbg_ref[...] = jnp.zeros_like(dbg_ref)
            dgn_ref[...] = jnp.zeros_like(dgn_ref)

        first_group = step == ng - 1
        wg_v, bg_v, gn_v = wg_ref[...], bg_ref[...], gn_ref[...]
        ones_v = jnp.ones((CHUNK, VW), BF16)
        ones_8 = jnp.ones((8, VW), BF16)
        ga = ga_ref[...]
        lr_s, wg_s, z_all, la_s = _gla_group_common(p_ref, wg_v, bg_v)
        qss = [(p_ref[c * CHUNK:(c + 1) * CHUNK, 0:KW] * scale).astype(BF16) for c in range(GLA_G)]
        o = jnp.concatenate([jnp.dot(qss[c], st_ref[c].astype(BF16), preferred_element_type=F32) for c in range(GLA_G)], axis=0)
        g = p_ref[:, 2 * KW + VW:2 * KW + 2 * VW]
        dyv = dy_ref[...].astype(F32)
        r = lax.rsqrt(_dot_sel(o * o, hm_ref[...], _DIMS["nn"], "a") * (1.0 / GLA_DV) + EPS)
        on = o * r
        sg = _sigmoid(g)
        silu = g * sg
        d_on = dyv * gn_v * silu
        dp_ref[:, 2 * KW + VW:2 * KW + 2 * VW] = (dyv * on * gn_v * (sg * (1.0 + g * (1.0 - sg)))).astype(dp_ref.dtype)
        dgn_ref[...] += jnp.sum(dyv * on * silu, axis=0, keepdims=True)
        mo = _dot_sel(o * d_on, hm_ref[...], _DIMS["nn"], "a") * (1.0 / GLA_DV)
        dob_all = (r * d_on - o * (r * r * r) * mo).astype(BF16)
        dzs = [None] * GLA_G
        for c in reversed(range(GLA_G)):
            rows = slice(c * CHUNK, (c + 1) * CHUNK)
            _, k, v, _, w, endb, a_full = _gla_chunk_common(p_ref, rows, la_s, l_ref[...], ones_v)
            s_n = st_ref[c]
            if c > 0:
                s_prev = st_ref[c - 1]
            else:
                s_prev = jnp.where(first_group, 0.0, sp_ref[0])
            kd = k * w
            dob = dob_all[rows]
            dq = lax.dot_general(dob, s_n.astype(BF16), _DIMS["nt"], preferred_element_type=F32) * scale
            g_n = lax.dot_general(qss[c], dob, _DIMS["tn"], preferred_element_type=F32) * m_ref[...] + ga
            d_a = _dot_sel(ones_8, g_n * s_prev, _DIMS["nt"], "b")[0:1, :]
            g_nb = g_n.astype(BF16)
            dkd = lax.dot_general(v.astype(BF16), g_nb, _DIMS["nt"], preferred_element_type=F32)
            dv = jnp.dot(kd.astype(BF16), g_nb, preferred_element_type=F32)
            e = dkd * kd
            d_end = jnp.sum(e, axis=0, keepdims=True) + d_a * jnp.exp(endb)
            dla = _dot_sel(l_ref[...], -e, _DIMS["tn"], "b") + d_end
            dzs[c] = dla * (1.0 - _sigmoid(z_all[rows])) * (1.0 / GLA_TAU)
            ga = a_full * g_n
            dp_ref[rows, 0:KW] = dq.astype(dp_ref.dtype)
            dp_ref[rows, KW:2 * KW] = (dkd * w).astype(dp_ref.dtype)
            dp_ref[rows, 2 * KW:2 * KW + VW] = dv.astype(dp_ref.dtype)
        ga_ref[...] = ga
        dz = jnp.concatenate(dzs, axis=0)
        dz_s = _split(dz)
        dp_ref[:, 2 * KW + 2 * VW:GLA_COLS] = _dot3s(dz_s, wg_s, _DIMS["nt"]).astype(dp_ref.dtype)
        dwg_ref[...] += _dot3s(lr_s, dz_s, _DIMS["tn"])
        dbg_ref[...] += jnp.sum(dz, axis=0, keepdims=True)

    full = lambda shape: pl.BlockSpec(shape, lambda i: tuple(0 for _ in shape))
    rev = lambda i: (ng - 1 - i, 0)
    vm = 4 * _nbytes((rb, GLA_COLS), F32) + 2 * _nbytes((rb, VW), F32) + 2 * _nbytes((GLA_G + 1, KW, VW), F32)
    vm += 16 * _nbytes((KW, VW), F32)
    return _pallas(
        body,
        out_shape=(_out((T, GLA_COLS), BF16), _out((LRW, KW), F32),
                   _out((1, KW), F32), _out((1, VW), F32)),
        grid=(ng,),
        in_specs=[pl.BlockSpec((rb, GLA_COLS), rev), pl.BlockSpec((rb, VW), rev),
                  pl.BlockSpec((GLA_G, KW, VW), lambda i: (ng - 1 - i, 0, 0)),
                  pl.BlockSpec((1, KW, VW), lambda i: (jnp.maximum((ng - 1 - i) * GLA_G - 1, 0), 0, 0)),
                  full((LRW, KW)), full((1, KW)), full((1, VW)), full((KW, VW)), full((VW, VW)), full((CHUNK, CHUNK))],
        out_specs=(pl.BlockSpec((rb, GLA_COLS), rev), full((LRW, KW)), full((1, KW)), full((1, VW))),
        scratch_shapes=[pltpu.VMEM((KW, VW), F32)],
        compiler_params=_cparams(("arbitrary",), vm),
        name=name,
    )(p, dy, states, states, wg, bg, gn, mask, hm, low)


CONV_TM = 512
HALO = 32
CONV_RB = 64


def _glu(u):
    a = u[:, 0:CW]
    b = u[:, CW:2 * CW]
    return a * _sigmoid(b)


def _conv_taps(buf_ref, w_ref, rb0, first_tap):
    acc = jnp.zeros((CONV_RB, CW), F32)
    for j in range(CK):
        s = rb0 + first_tap(j)
        acc = acc + w_ref[j:j + 1, :] * buf_ref[s:s + CONV_RB, :]
    return acc


def _ln_fwd(c, lg, lb):
    mu = jnp.mean(c, axis=-1, keepdims=True)
    xc = c - mu
    rstd = lax.rsqrt(jnp.mean(xc * xc, axis=-1, keepdims=True) + EPS)
    n = xc * rstd
    return n, rstd, n * lg + lb


def _conv_fwd(u, w, b, lg, lb, name):
    T = u.shape[0]
    tm = CONV_TM

    def body(u_ref, uh_ref, w_ref, b_ref, lg_ref, lb_ref, o_ref, c_ref, hbuf):
        i = pl.program_id(0)
        hbuf[0:HALO, :] = jnp.where(i > 0, _glu(uh_ref[...]), 0.0)
        hbuf[HALO:HALO + tm, :] = _glu(u_ref[...])
        for r in range(tm // CONV_RB):
            rows = slice(r * CONV_RB, (r + 1) * CONV_RB)
            acc = _conv_taps(hbuf, w_ref, r * CONV_RB, lambda j: HALO - (CK - 1) + j)
            c_ref[rows, :] = acc
            _, _, zz = _ln_fwd(acc + b_ref[...], lg_ref[...], lb_ref[...])
            o_ref[rows, :] = (zz * _sigmoid(zz)).astype(o_ref.dtype)

    vec = pl.BlockSpec((1, CW), lambda i: (0, 0))
    return _pallas(
        body,
        out_shape=(_out((T, CW), BF16), _out((T, CW), F32)),
        grid=(T // tm,),
        in_specs=[pl.BlockSpec((tm, CONV_COLS), lambda i: (i, 0)),
                  pl.BlockSpec((HALO, CONV_COLS), lambda i: (jnp.maximum(i * (tm // HALO) - 1, 0), 0)),
                  pl.BlockSpec((HALO, CW), lambda i: (0, 0)), vec, vec, vec],
        out_specs=(pl.BlockSpec((tm, CW), lambda i: (i, 0)), pl.BlockSpec((tm, CW), lambda i: (i, 0))),
        scratch_shapes=[pltpu.VMEM((tm + HALO, CW), F32)],
        compiler_params=_cparams(("arbitrary",), 8 * _nbytes((tm, CONV_COLS), F32)),
        name=name,
    )(u, u, w, b, lg, lb)


def _conv_bwd_dc(u, conv, dout, b, lg, lb, name):
    T = u.shape[0]
    tm = CONV_TM
    nsteps = T // tm

    def body(u_ref, uh_ref, c_ref, do_ref, b_ref, lg_ref, lb_ref, dc_ref, dw_ref, db_ref, dlg_ref, dlb_ref, hbuf, dwacc):
        i = pl.program_id(0)

        @pl.when(i == 0)
        def _():
            dwacc[...] = jnp.zeros_like(dwacc)
            db_ref[...] = jnp.zeros_like(db_ref)
            dlg_ref[...] = jnp.zeros_like(dlg_ref)
            dlb_ref[...] = jnp.zeros_like(dlb_ref)

        hbuf[0:HALO, :] = jnp.where(i > 0, _glu(uh_ref[...]), 0.0)
        hbuf[HALO:HALO + tm, :] = _glu(u_ref[...])
        for r in range(tm // CONV_RB):
            rows = slice(r * CONV_RB, (r + 1) * CONV_RB)
            n, rstd, zz = _ln_fwd(c_ref[rows, :] + b_ref[...], lg_ref[...], lb_ref[...])
            sg = _sigmoid(zz)
            dz = do_ref[rows, :].astype(F32) * (sg * (1.0 + zz * (1.0 - sg)))
            dlg_ref[...] += jnp.sum(dz * n, axis=0, keepdims=True)
            dlb_ref[...] += jnp.sum(dz, axis=0, keepdims=True)
            dn = dz * lg_ref[...]
            dc = rstd * (dn - jnp.mean(dn, axis=-1, keepdims=True) - n * jnp.mean(dn * n, axis=-1, keepdims=True))
            dc_ref[rows, :] = dc
            db_ref[...] += jnp.sum(dc, axis=0, keepdims=True)
            for j in range(CK):
                s = r * CONV_RB + HALO - (CK - 1) + j
                prod = dc * hbuf[s:s + CONV_RB, :]
                dwacc[j] += jnp.sum(prod.reshape(CONV_RB // 8, 8, CW), axis=0)

        @pl.when(i == nsteps - 1)
        def _():
            dw_ref[...] = jnp.sum(dwacc[...], axis=1)

    vec = pl.BlockSpec((1, CW), lambda i: (0, 0))
    return _pallas(
        body,
        out_shape=(_out((T, CW), F32), _out((HALO, CW), F32),
                   _out((1, CW), F32), _out((1, CW), F32), _out((1, CW), F32)),
        grid=(nsteps,),
        in_specs=[pl.BlockSpec((tm, CONV_COLS), lambda i: (i, 0)),
                  pl.BlockSpec((HALO, CONV_COLS), lambda i: (jnp.maximum(i * (tm // HALO) - 1, 0), 0)),
                  pl.BlockSpec((tm, CW), lambda i: (i, 0)), pl.BlockSpec((tm, CW), lambda i: (i, 0)), vec, vec, vec],
        out_specs=(pl.BlockSpec((tm, CW), lambda i: (i, 0)), pl.BlockSpec((HALO, CW), lambda i: (0, 0)), vec, vec, vec),
        scratch_shapes=[pltpu.VMEM((tm + HALO, CW), F32), pltpu.VMEM((HALO, 8, CW), F32)],
        compiler_params=_cparams(("arbitrary",), 10 * _nbytes((tm, CONV_COLS), F32)),
        name=name,
    )(u, u, conv, dout, b, lg, lb)


def _conv_bwd_du(u, dc, w, name):
    T = u.shape[0]
    tm = CONV_TM
    nsteps = T // tm

    def body(u_ref, dc_ref, dch_ref, w_ref, du_ref, dcbuf):
        i = pl.program_id(0)
        dcbuf[0:tm, :] = dc_ref[...]
        dcbuf[tm:tm + HALO, :] = jnp.where(i < nsteps - 1, dch_ref[...], 0.0)
        for r in range(tm // CONV_RB):
            rows = slice(r * CONV_RB, (r + 1) * CONV_RB)
            dh = _conv_taps(dcbuf, w_ref, r * CONV_RB, lambda j: (CK - 1) - j)
            a = u_ref[rows, 0:CW]
            sb = _sigmoid(u_ref[rows, CW:2 * CW])
            du_ref[rows, 0:CW] = (dh * sb).astype(du_ref.dtype)
            du_ref[rows, CW:2 * CW] = (dh * a * sb * (1.0 - sb)).astype(du_ref.dtype)

    return _pallas(
        body,
        out_shape=_out((T, CONV_COLS), BF16),
        grid=(nsteps,),
        in_specs=[pl.BlockSpec((tm, CONV_COLS), lambda i: (i, 0)),
                  pl.BlockSpec((tm, CW), lambda i: (i, 0)),
                  pl.BlockSpec((HALO, CW), lambda i: (jnp.minimum((i + 1) * (tm // HALO), T // HALO - 1), 0)),
                  pl.BlockSpec((HALO, CW), lambda i: (0, 0))],
        out_specs=pl.BlockSpec((tm, CONV_COLS), lambda i: (i, 0)),
        scratch_shapes=[pltpu.VMEM((tm + HALO, CW), F32)],
        compiler_params=_cparams(("arbitrary",), 8 * _nbytes((tm, CONV_COLS), F32)),
        name=name,
    )(u, dc, dc, w)


ATT_G = 4


def _att_load_kv(p_any, kbuf, vbuf, sems, T):
    kc = pltpu.make_async_copy(p_any.at[:, pl.ds(AW, AW)], kbuf.at[pl.ds(LEFT, T), :], sems.at[0])
    vc = pltpu.make_async_copy(p_any.at[:, pl.ds(2 * AW, AW)], vbuf.at[pl.ds(LEFT, T), :], sems.at[1])
    kc.start()
    vc.start()
    kbuf[0:LEFT, :] = jnp.zeros((LEFT, AW), BF16)
    vbuf[0:LEFT, :] = jnp.zeros((LEFT, AW), BF16)
    kc.wait()
    vc.wait()


ATT_QB = CHUNK * ATT_G
ATT_KB = LEFT + ATT_QB
REL_PAD = 384
TOEP = 1024


def _att_consts():
    m = np.arange(TOEP)
    d = ATT_KB - 1 - m
    idx = np.clip(d, -128, 128) + 128
    sel = (np.arange(REL_PAD)[:, None] == idx[None, :]) & (m[None, :] < ATT_QB + ATT_KB - 1)
    return jnp.asarray(sel.astype(np.float32))


def _att_build_bias(rel_ref, sel_ref, bias_scr):
    tr = jnp.dot(rel_ref[...], sel_ref[...], precision=HI, preferred_element_type=F32)
    qc = lax.broadcasted_iota(jnp.int32, (ATT_QB, ATT_KB), 0) // CHUNK
    kc = lax.broadcasted_iota(jnp.int32, (ATT_QB, ATT_KB), 1) // CHUNK
    band = (kc >= qc) & (kc <= qc + 8)
    for h in range(AH):
        rows = jnp.broadcast_to(tr[h:h + 1, :], (ATT_QB, TOEP))
        toep = pltpu.roll(rows, TOEP - (ATT_QB - 1), 1, stride=1, stride_axis=0)[:, 0:ATT_KB]
        bias_scr[h // 2, (h % 2) * ATT_QB:(h % 2 + 1) * ATT_QB, :] = jnp.where(band, toep, NEG)


def _att_probs(qst, kb, bias_p, n0):
    sc = lax.dot_general(qst, kb, _DIMS["nt"], preferred_element_type=F32) * (64 ** -0.5) + bias_p
    pos = lax.broadcasted_iota(jnp.int32, (2 * ATT_QB, ATT_KB), 1)
    sc = jnp.where(pos >= CHUNK * (8 - n0), sc, NEG)
    mx = jnp.max(sc, axis=-1, keepdims=True)
    ex = jnp.exp(sc - mx)
    return ex / jnp.sum(ex, axis=-1, keepdims=True)


def _head_stack(a2, lo):
    zero = jnp.zeros_like(a2)
    return jnp.concatenate([jnp.where(lo, a2, zero), jnp.where(lo, zero, a2)], axis=0)


def _att_fwd(p, rel, sel, name):
    T = p.shape[0]

    def body(q_ref, p_any, rel_ref, sel_ref, o_ref, kbuf, vbuf, bias_scr, sems):
        i = pl.program_id(0)

        @pl.when(i == 0)
        def _():
            _att_load_kv(p_any, kbuf, vbuf, sems, T)
            _att_build_bias(rel_ref, sel_ref, bias_scr)

        lo = lax.broadcasted_iota(jnp.int32, (ATT_QB, 128), 1) < 64
        n0 = i * ATT_G
        start = pl.multiple_of(i * ATT_QB, ATT_QB)
        for hp in range(AH // 2):
            cols = slice(hp * 128, (hp + 1) * 128)
            kb = kbuf[pl.ds(start, ATT_KB), cols]
            vb = vbuf[pl.ds(start, ATT_KB), cols]
            pr = _att_probs(_head_stack(q_ref[:, cols], lo), kb, bias_scr[hp], n0)
            pv = jnp.dot(pr.astype(BF16), vb, preferred_element_type=F32)
            o_ref[:, cols] = jnp.where(lo, pv[0:ATT_QB], pv[ATT_QB:2 * ATT_QB]).astype(o_ref.dtype)

    vm = 2 * _nbytes((T + LEFT, AW), BF16) + 8 * _nbytes((2 * ATT_QB, ATT_KB), F32) + (8 << 20)
    return _pallas(
        body,
        out_shape=_out((T, AW), BF16),
        grid=(T // ATT_QB,),
        in_specs=[pl.BlockSpec((ATT_QB, AW), lambda i: (i, 0)), pl.BlockSpec(memory_space=pl.ANY),
                  pl.BlockSpec((8, REL_PAD), lambda i: (0, 0)), pl.BlockSpec((REL_PAD, TOEP), lambda i: (0, 0))],
        out_specs=pl.BlockSpec((ATT_QB, AW), lambda i: (i, 0)),
        scratch_shapes=[pltpu.VMEM((T + LEFT, AW), BF16), pltpu.VMEM((T + LEFT, AW), BF16),
                        pltpu.VMEM((AH // 2, 2 * ATT_QB, ATT_KB), F32), pltpu.SemaphoreType.DMA((2,))],
        compiler_params=_cparams(("arbitrary",), vm),
        name=name,
    )(p, p, rel, sel)


def _att_bwd(p, do, rel, sel, name):
    T = p.shape[0]
    nsteps = T // ATT_QB

    def body(q_ref, p_any, do_ref, rel_ref, sel_ref, dp_any, drel_ref,
             kbuf, vbuf, dqbuf, dkbuf, dvbuf, bias_scr, dbias_scr, dtr_scr, sems):
        i = pl.program_id(0)

        @pl.when(i == 0)
        def _():
            _att_load_kv(p_any, kbuf, vbuf, sems, T)
            _att_build_bias(rel_ref, sel_ref, bias_scr)
            dkbuf[...] = jnp.zeros_like(dkbuf)
            dvbuf[...] = jnp.zeros_like(dvbuf)
            dbias_scr[...] = jnp.zeros_like(dbias_scr)

        lo = lax.broadcasted_iota(jnp.int32, (ATT_QB, 128), 1) < 64
        n0 = i * ATT_G
        start = pl.multiple_of(i * ATT_QB, ATT_QB)
        for hp in range(AH // 2):
            cols = slice(hp * 128, (hp + 1) * 128)
            kb = kbuf[pl.ds(start, ATT_KB), cols]
            vb = vbuf[pl.ds(start, ATT_KB), cols]
            qst = _head_stack(q_ref[:, cols], lo)
            dost = _head_stack(do_ref[:, cols].astype(BF16), lo)
            pr = _att_probs(qst, kb, bias_scr[hp], n0)
            dpr = lax.dot_general(dost, vb, _DIMS["nt"], preferred_element_type=F32)
            ds = pr * (dpr - jnp.sum(dpr * pr, axis=-1, keepdims=True))
            dbias_scr[hp] += ds
            dsb = (ds * (64 ** -0.5)).astype(BF16)
            dq = jnp.dot(dsb, kb, preferred_element_type=F32)
            dqbuf[pl.ds(start, ATT_QB), cols] = jnp.where(lo, dq[0:ATT_QB], dq[ATT_QB:2 * ATT_QB]).astype(BF16)
            dkbuf[pl.ds(start, ATT_KB), cols] += lax.dot_general(dsb, qst, _DIMS["tn"], preferred_element_type=F32)
            dvbuf[pl.ds(start, ATT_KB), cols] += lax.dot_general(pr.astype(BF16), dost, _DIMS["tn"], preferred_element_type=F32)

        @pl.when(i == nsteps - 1)
        def _():
            kbuf[pl.ds(LEFT, T), :] = dkbuf[pl.ds(LEFT, T), :].astype(BF16)
            vbuf[pl.ds(LEFT, T), :] = dvbuf[pl.ds(LEFT, T), :].astype(BF16)
            cps = [pltpu.make_async_copy(dqbuf, dp_any.at[:, pl.ds(0, AW)], sems.at[0]),
                   pltpu.make_async_copy(kbuf.at[pl.ds(LEFT, T), :], dp_any.at[:, pl.ds(AW, AW)], sems.at[1]),
                   pltpu.make_async_copy(vbuf.at[pl.ds(LEFT, T), :], dp_any.at[:, pl.ds(2 * AW, AW)], sems.at[2])]
            for cp in cps:
                cp.start()
            dtr_scr[...] = jnp.zeros_like(dtr_scr)
            ri = lax.broadcasted_iota(jnp.int32, (ATT_QB, ATT_QB), 0)
            ci = lax.broadcasted_iota(jnp.int32, (ATT_QB, ATT_QB), 1)
            flip = jnp.where(ri + ci == ATT_QB - 1, 1.0, 0.0)
            for h in range(AH):
                db = dbias_scr[h // 2, (h % 2) * ATT_QB:(h % 2 + 1) * ATT_QB, :]
                db = jnp.dot(flip, db, precision=HI, preferred_element_type=F32)
                wide = jnp.concatenate([db, jnp.zeros((ATT_QB, TOEP - ATT_KB), F32)], axis=1)
                diag = pltpu.roll(wide, 0, 1, stride=1, stride_axis=0)
                dtr_scr[h:h + 1, :] = jnp.sum(diag, axis=0, keepdims=True)
            drel_ref[...] = lax.dot_general(dtr_scr[...], sel_ref[...], _DIMS["nt"], precision=HI, preferred_element_type=F32)
            for cp in cps:
                cp.wait()

    vm = 3 * _nbytes((T + LEFT, AW), BF16) + 2 * _nbytes((T + LEFT, AW), F32) + 12 * _nbytes((2 * ATT_QB, ATT_KB), F32) + (8 << 20)
    return _pallas(
        body,
        out_shape=(_out((T, ATT_COLS), BF16), _out((8, REL_PAD), F32)),
        grid=(nsteps,),
        in_specs=[pl.BlockSpec((ATT_QB, AW), lambda i: (i, 0)), pl.BlockSpec(memory_space=pl.ANY),
                  pl.BlockSpec((ATT_QB, AW), lambda i: (i, 0)),
                  pl.BlockSpec((8, REL_PAD), lambda i: (0, 0)), pl.BlockSpec((REL_PAD, TOEP), lambda i: (0, 0))],
        out_specs=(pl.BlockSpec(memory_space=pl.ANY), pl.BlockSpec((8, REL_PAD), lambda i: (0, 0))),
        scratch_shapes=[pltpu.VMEM((T + LEFT, AW), BF16), pltpu.VMEM((T + LEFT, AW), BF16), pltpu.VMEM((T, AW), BF16),
                        pltpu.VMEM((T + LEFT, AW), F32), pltpu.VMEM((T + LEFT, AW), F32),
                        pltpu.VMEM((AH // 2, 2 * ATT_QB, ATT_KB), F32), pltpu.VMEM((AH // 2, 2 * ATT_QB, ATT_KB), F32),
                        pltpu.VMEM((8, TOEP), F32), pltpu.SemaphoreType.DMA((3,))],
        compiler_params=_cparams(("arbitrary",), vm),
        name=name,
    )(p, p, do, rel, sel)


def _layer_fwd(h, wl, consts, tag):
    p_gla, p_conv, p_att, xn = _mm_fan(h, [wl["w_gla"], wl["w_conv"], wl["w_att"]], mode="nt", out_dtypes=(F32, F32, BF16),
                                       norm_g=wl["norm_mix"], name=f"{tag}_proj")
    o_gla, states = _gla_fwd(p_gla, wl["wg"], wl["bg"], wl["gla_norm"], consts, f"{tag}_gla_fwd")
    o_conv, conv = _conv_fwd(p_conv, wl["w_dw"], wl["b_dw"], wl["ln_g"], wl["ln_b"], f"{tag}_conv_fwd")
    rel = jnp.pad(wl["rel_bias"], ((0, 8 - AH), (0, REL_PAD - N_REL)))
    o_att = _att_fwd(p_att, rel, consts[3], f"{tag}_att_fwd")
    h1 = _mm_sum([o_gla, o_conv, o_att], [wl["w_out_g"], wl["w_out_c"], wl["w_out_a"]], mode="nn", out_dtype=F32, extra=h,
                 name=f"{tag}_out")
    u, xn2 = _mm(h1, wl["w_up"], mode="nn", out_dtype=BF16, tm=1024, tn=1024, b_chips=True, norm_g=wl["norm_ffn"],
                 name=f"{tag}_mlp_up")
    h2 = _mm(u, wl["w_down"], mode="nn", out_dtype=F32, tm=1024, tk=2048, a_pro="relu2", epi="add", extra=h1,
             name=f"{tag}_mlp_down")
    saved = dict(h=h, xn=xn, p_gla=p_gla, p_conv=p_conv, p_att=p_att, states=states, o_gla=o_gla, o_conv=o_conv, conv=conv,
                 o_att=o_att, rel=rel, h1=h1, xn2=xn2, u=u)
    return h2, saved


def _layer_bwd(dh2, sv, wl, consts, tag, emit=lambda name, grad: None):
    g = {}
    du = _mm(dh2, wl["w_down"], mode="nt", out_dtype=BF16, tm=1024, tn=1024, epi="relu2grad", extra=sv["u"],
             name=f"{tag}_mlp_down_dx")
    g["w_down"] = _mm(sv["u"], dh2, mode="tn", out_dtype=F32, tm=2048, tn=1024, tk=512, a_pro="relu2", name=f"{tag}_mlp_down_dw")
    emit("w_down", g["w_down"].reshape(4, D_FF // 4, D))
    dh1, g["norm_ffn"] = _mm(du, wl["w_up"], mode="nt", out_dtype=F32, tm=512, tk=1024, b_chips=True,
                             norm_bwd=(sv["h1"], wl["norm_ffn"], dh2), name=f"{tag}_mlp_up_dx")
    g["w_up"] = _mm(sv["xn2"], du, mode="tn", out_dtype=F32, tm=1024, tn=1024, tk=1024, out_chips=True, name=f"{tag}_mlp_up_dw")
    emit("w_up", g["w_up"])
    d_gla, d_conv, d_att = _mm_fan(dh1, [wl["w_out_g"], wl["w_out_c"], wl["w_out_a"]], mode="nt", out_dtypes=(F32, F32, F32),
                                   name=f"{tag}_out_dx")
    g["w_out_g"], g["w_out_c"], g["w_out_a"] = _mm_tn_multi([sv["o_gla"], sv["o_conv"], sv["o_att"], dh1],
                                                            [(0, 3), (1, 3), (2, 3)], name=f"{tag}_out_dw")
    emit("w_out", jnp.concatenate([g["w_out_g"], g["w_out_c"], g["w_out_a"]], axis=0).reshape(4, D // 4, D))
    dp_gla, g["wg"], g["bg"], g["gla_norm"] = _gla_bwd(sv["p_gla"], d_gla, sv["states"], wl["wg"], wl["bg"], wl["gla_norm"],
                                                       consts, f"{tag}_gla_bwd")
    dc, g["w_dw"], g["b_dw"], g["ln_g"], g["ln_b"] = _conv_bwd_dc(sv["p_conv"], sv["conv"], d_conv, wl["b_dw"], wl["ln_g"],
                                                                  wl["ln_b"], f"{tag}_conv_bwd_dc")
    dp_conv = _conv_bwd_du(sv["p_conv"], dc, wl["w_dw"], f"{tag}_conv_bwd_du")
    dp_att, drel = _att_bwd(sv["p_att"], d_att, sv["rel"], consts[3], f"{tag}_att_bwd")
    g["rel_bias"] = drel[0:AH, 0:N_REL]
    g["w_gla"], g["w_conv"], g["w_att"] = _mm_tn_multi([sv["xn"], dp_gla, dp_conv, dp_att], [(1, 0), (2, 0), (3, 0)],
                                                       name=f"{tag}_proj_dw")
    emit("w_in", _join_w_in_t(g))
    dh, g["norm_mix"] = _mm_sum([dp_gla, dp_conv, dp_att], [wl["w_gla"], wl["w_conv"], wl["w_att"]], mode="nn", out_dtype=F32,
                                norm_bwd=(sv["h"], wl["norm_mix"], dh1), name=f"{tag}_proj_dx")
    return dh, g


def _local_step(x, target, layers, norm_final, emit=lambda layer, name, grad: None):
    consts = _gla_consts() + (_att_consts(),)
    h = x
    saved = []
    for l, wl in enumerate(layers):
        h, sv = _layer_fwd(h, wl, consts, f"l{l}")
        saved.append(sv)
    loss, dh, g_final = _final_loss(h, norm_final, target, "final_loss")
    grads = [None] * len(layers)
    for l in reversed(range(len(layers))):
        dh, grads[l] = _layer_bwd(dh, saved[l], layers[l], consts, f"l{l}", functools.partial(emit, l))
    return loss, dh, grads, g_final


ANY = pl.BlockSpec(memory_space=pl.ANY)


def _place():
    x, y, c = lax.axis_index("x"), lax.axis_index("y"), lax.axis_index("c")
    chips = [(1 - x, y), (x, 1 - y), (1 - x, 1 - y)]
    return x, y, c, chips


def _shape(shape, dtype):
    return jax.ShapeDtypeStruct(tuple(shape), dtype)


def _remote(src, dst, send_sem, recv_sem, to):
    return pltpu.make_async_remote_copy(src_ref=src, dst_ref=dst, send_sem=send_sem, recv_sem=recv_sem,
                                        device_id=to, device_id_type=MESH)


class _Staged:
    def __init__(self, src, dst, buf, sems):
        self.load = pltpu.make_async_copy(src, buf, sems.at[0])
        self.store = pltpu.make_async_copy(buf, dst, sems.at[1])

    def start(self):
        self.load.start()

    def wait(self):
        self.load.wait()
        self.store.start()
        self.store.wait()


def _phase_gather_ici(src, then):
    R, C = src.shape
    rh = R // 2

    def copies(ins, outs, sems):
        x, y, c, chips = _place()
        me = 2 * x + y
        local = _Staged(ins[0], outs[0].at[me], sems[3], sems[2])
        sends = [_remote(ins[0].at[pl.ds(c * rh, rh), :], outs[0].at[me, pl.ds(c * rh, rh), :], sems[0].at[k], sems[1].at[k],
                         (px, py, c)) for k, (px, py) in enumerate(chips)]
        recvs = [_remote(outs[0].at[2 * px + py, pl.ds(c * rh, rh), :], outs[0].at[2 * px + py, pl.ds(c * rh, rh), :],
                         sems[0].at[k], sems[1].at[k], (px, py, c)) for k, (px, py) in enumerate(chips)]
        return local, sends, recvs

    def start(ins, outs, sems):
        local, sends, _ = copies(ins, outs, sems)
        local.start()
        for cp in sends:
            cp.start()

    def finish(ins, outs, sems):
        local, sends, recvs = copies(ins, outs, sems)
        for cp in recvs:
            cp.wait_recv()
        for cp in sends:
            cp.wait_send()
        local.wait()

    return _Comm([src], [_shape((4, R, C), src.dtype)], {}, [(3,), (3,), (2,), ((R, C), src.dtype)], start, finish, then)


def _phase_gather_d2d(part, then):
    _, R, C = part.shape
    rh = R // 2

    def copies(ins, outs, sems):
        x, y, c, chips = _place()
        sends = [_remote(ins[0].at[2 * px + py, pl.ds(c * rh, rh), :], outs[0].at[2 * px + py, pl.ds(c * rh, rh), :],
                         sems[0].at[k], sems[1].at[k], (x, y, 1 - c)) for k, (px, py) in enumerate(chips)]
        recvs = [_remote(outs[0].at[2 * px + py, pl.ds((1 - c) * rh, rh), :], outs[0].at[2 * px + py, pl.ds((1 - c) * rh, rh), :],
                         sems[0].at[k], sems[1].at[k], (x, y, 1 - c)) for k, (px, py) in enumerate(chips)]
        return sends, recvs

    def start(ins, outs, sems):
        for cp in copies(ins, outs, sems)[0]:
            cp.start()

    def finish(ins, outs, sems):
        sends, recvs = copies(ins, outs, sems)
        for cp in recvs:
            cp.wait_recv()
        for cp in sends:
            cp.wait_send()

    return _Comm([part], [_shape(part.shape, part.dtype)], {0: 0}, [(3,), (3,)], start, finish, then)


def _phase_pair_exchange(full, then):
    _, R, C = full.shape
    rh = R // 2

    def copy(ins, outs, sems):
        x, y, c, _ = _place()
        return _remote(ins[0].at[:, pl.ds((1 - c) * rh, rh), :], outs[0], sems[0].at[0], sems[1].at[0], (x, y, 1 - c))

    return _Comm([full], [_shape((4, rh, C), full.dtype)], {}, [(1,), (1,)],
                 lambda ins, outs, sems: copy(ins, outs, sems).start(),
                 lambda ins, outs, sems: copy(ins, outs, sems).wait(), then)


def _phase_chip_scatter(parts, then):
    def copies(ins, outs, sems):
        x, y, c, chips = _place()
        me = 2 * x + y
        local = _Staged(ins[0].at[me], outs[0].at[me], sems[3], sems[2])
        sends = [_remote(ins[0].at[2 * px + py], outs[0].at[me], sems[0].at[k], sems[1].at[k], (px, py, c))
                 for k, (px, py) in enumerate(chips)]
        recvs = [_remote(outs[0].at[2 * px + py], outs[0].at[2 * px + py], sems[0].at[k], sems[1].at[k], (px, py, c))
                 for k, (px, py) in enumerate(chips)]
        return local, sends, recvs

    def start(ins, outs, sems):
        local, sends, _ = copies(ins, outs, sems)
        local.start()
        for cp in sends:
            cp.start()

    def finish(ins, outs, sems):
        local, sends, recvs = copies(ins, outs, sems)
        for cp in recvs:
            cp.wait_recv()
        for cp in sends:
            cp.wait_send()
        local.wait()

    return _Comm([parts], [_shape(parts.shape, parts.dtype)], {}, [(3,), (3,), (2,), (parts.shape[1:], parts.dtype)],
                 start, finish, then)


def _phase_pair_allgather(half, layer, depth, into, then):
    rh, C = half.shape

    def copies(ins, outs, sems):
        x, y, c, _ = _place()
        mine = outs[0].at[layer, pl.ds(c * rh, rh), :]
        theirs = outs[0].at[layer, pl.ds((1 - c) * rh, rh), :]
        return (_Staged(ins[0], mine, sems[3], sems[2]),
                _remote(ins[0], mine, sems[0].at[0], sems[1].at[0], (x, y, 1 - c)),
                _remote(theirs, theirs, sems[0].at[0], sems[1].at[0], (x, y, 1 - c)))

    def start(ins, outs, sems):
        local, send, _ = copies(ins, outs, sems)
        local.start()
        send.start()

    def finish(ins, outs, sems):
        local, send, recv = copies(ins, outs, sems)
        recv.wait_recv()
        send.wait_send()
        local.wait()

    ins = [half] if into is None else [half, into]
    return _Comm(ins, [_shape((depth, 2 * rh, C), half.dtype)], {} if into is None else {1: 0},
                 [(1,), (1,), (2,), ((rh, C), half.dtype)], start, finish, then)


def _comm_only(comms, name):
    plan = _Plan()
    for c in comms:
        plan.at(name, c)
    saved, _PLAN[0] = _PLAN[0], plan
    try:
        def body(o_ref):
            o_ref[...] = jnp.zeros_like(o_ref)

        _pallas(body, out_shape=[jax.ShapeDtypeStruct((8, 128), F32)], in_specs=[],
                out_specs=[pl.BlockSpec(memory_space=pltpu.VMEM)], name=name)()
    finally:
        _PLAN[0] = saved


def _row_tile(rows, cols, itemsize=4, budget=1 << 20, mult=8):
    fits = [t for t in range(mult, rows + 1, mult) if rows % t == 0 and t * cols * itemsize <= budget]
    return max(fits) if fits else rows


def _pair_add(full, got, c, name):
    _, L, R, C = full.shape
    rh = R // 2
    tr = _row_tile(rh, C, budget=2 << 20, mult=16)
    nb = rh // tr

    def body(c_ref, a_ref, b_ref, o_ref):
        o_ref[...] = (a_ref[...] + b_ref[...]).astype(o_ref.dtype)

    grid_spec = pltpu.PrefetchScalarGridSpec(
        num_scalar_prefetch=1,
        grid=(4, L, nb),
        in_specs=[pl.BlockSpec((1, 1, tr, C), lambda j, l, i, c_ref: (j, l, c_ref[0] * nb + i, 0)),
                  pl.BlockSpec((1, 1, tr, C), lambda j, l, i, c_ref: (j, l, i, 0))],
        out_specs=pl.BlockSpec((1, 1, tr, C), lambda j, l, i, c_ref: (j, l, i, 0)),
    )
    return _pallas(
        body,
        out_shape=_out((4, L, rh, C), BF16),
        grid_spec=grid_spec,
        compiler_params=_cparams(("parallel", "parallel", "parallel"), 8 * tr * C * 4),
        name=name,
    )(jnp.reshape(c, (1,)).astype(jnp.int32), full, got)


def _sum_chips(parts, name):
    _, L, rh, C = parts.shape
    tr = _row_tile(rh, C, budget=2 << 20, mult=16)

    def body(p_ref, o_ref):
        acc = p_ref[0].astype(F32)
        for j in range(1, 4):
            acc = acc + p_ref[j].astype(F32)
        o_ref[...] = acc

    return _pallas(
        body,
        out_shape=_out((L, rh, C), F32),
        grid=(L, rh // tr),
        in_specs=[pl.BlockSpec((4, 1, tr, C), lambda l, i: (0, l, i, 0))],
        out_specs=pl.BlockSpec((1, tr, C), lambda l, i: (l, i, 0)),
        compiler_params=_cparams(("parallel", "parallel"), 16 * tr * C * 4),
        name=name,
    )(parts)


def _allreduce_small(v):
    R = v.shape[0]

    def body(v_ref, o_ref, slots, send_sems, recv_sems):
        x, y, c, _ = _place()
        me = 4 * x + 2 * y + c
        slots[me] = v_ref[...]
        cps = []
        for r in range(1, 8):
            px, py, pc = x ^ (r >> 2), y ^ ((r >> 1) & 1), c ^ (r & 1)
            cps.append(pltpu.make_async_remote_copy(src_ref=v_ref, dst_ref=slots.at[me], send_sem=send_sems.at[r - 1],
                                                    recv_sem=recv_sems.at[r - 1], device_id=(px, py, pc), device_id_type=MESH))
            cps[-1].start()
        for r in range(1, 8):
            px, py, pc = x ^ (r >> 2), y ^ ((r >> 1) & 1), c ^ (r & 1)
            theirs = slots.at[4 * px + 2 * py + pc]
            pltpu.make_async_remote_copy(src_ref=theirs, dst_ref=theirs, send_sem=send_sems.at[r - 1], recv_sem=recv_sems.at[r - 1],
                                         device_id=(px, py, pc), device_id_type=MESH).wait_recv()
        acc = slots[0]
        for j in range(1, 8):
            acc = acc + slots[j]
        o_ref[...] = acc
        for cp in cps:
            cp.wait_send()

    return pl.pallas_call(
        body,
        out_shape=jax.ShapeDtypeStruct(v.shape, F32),
        in_specs=[pl.BlockSpec(memory_space=pltpu.VMEM)],
        out_specs=pl.BlockSpec(memory_space=pltpu.VMEM),
        scratch_shapes=[pltpu.VMEM((8, R, 128), F32), pltpu.SemaphoreType.DMA((7,)), pltpu.SemaphoreType.DMA((7,))],
        name="allreduce_small",
    )(v)


def _adamw_math(w, g, m, v):
    m = ADAM_B1 * m + (1.0 - ADAM_B1) * g
    v = ADAM_B2 * v + (1.0 - ADAM_B2) * (g * g)
    m_hat = m / (1.0 - ADAM_B1 ** ADAM_STEP)
    v_hat = v / (1.0 - ADAM_B2 ** ADAM_STEP)
    delta = -ADAM_LR * (m_hat / (jnp.sqrt(v_hat) + ADAM_EPS) + ADAM_WD * w)
    return delta, m, v


def _adamw(ws, gs, ms, vs, name, budget=1 << 19):
    n = len(ws)
    tiles = [_row_tile(w.shape[1], w.shape[2], budget=budget) for w in ws]
    per_layer = [w.shape[1] // t for w, t in zip(ws, tiles)]
    steps = [w.shape[0] * p for w, p in zip(ws, per_layer)]
    starts = [sum(steps[:k]) for k in range(n)]

    def body(*refs):
        i = pl.program_id(0)
        for k in range(n):
            w_ref, g_ref, m_ref, v_ref = (refs[j * n + k] for j in range(4))
            outs = [refs[(4 + j) * n + k] for j in range(3)]

            @pl.when((i >= starts[k]) & (i < starts[k] + steps[k]))
            def _(w_ref=w_ref, g_ref=g_ref, m_ref=m_ref, v_ref=v_ref, outs=outs):
                outs[0][...], outs[1][...], outs[2][...] = _adamw_math(w_ref[...], g_ref[...], m_ref[...], v_ref[...])

    def spec(k):
        def index(i):
            local = jnp.clip(i - starts[k], 0, steps[k] - 1)
            return local // per_layer[k], local % per_layer[k], 0
        return pl.BlockSpec((None, tiles[k], ws[k].shape[2]), index)

    specs = [spec(k) for k in range(n)]
    outs = [_out(w.shape, F32) for w in ws]
    res = _pallas(
        body,
        out_shape=tuple(outs * 3),
        grid=(sum(steps),),
        in_specs=specs * 4,
        out_specs=tuple(specs * 3),
        compiler_params=_cparams(("arbitrary",), sum(16 * t * w.shape[2] * 4 for w, t in zip(ws, tiles))),
        name=name,
    )(*ws, *gs, *ms, *vs)
    return res[:n], res[n:2 * n], res[2 * n:]


def _adamw_slabs(w, g, m, v, name, slabs=59):
    n, L, C = w.shape
    assert n % slabs == 0

    def body(w_ref, g_ref, m_ref, v_ref, d_ref, nm_ref, nv_ref):
        d_ref[...], nm_ref[...], nv_ref[...] = _adamw_math(w_ref[...], g_ref[...], m_ref[...], v_ref[...])

    blk = pl.BlockSpec((slabs, L, C), lambda i: (i, 0, 0))
    out = _out(w.shape, F32)
    return _pallas(
        body,
        out_shape=(out, out, out),
        grid=(n // slabs,),
        in_specs=[blk] * 4,
        out_specs=(blk, blk, blk),
        compiler_params=_cparams(("parallel",), 16 * slabs * 8 * C * 4),
        name=name,
    )(w, g, m, v)


def _adamw_small(ws, gs, ms, vs):
    n = len(ws)

    def body(*refs):
        for t in range(n):
            w_ref, g_ref, m_ref, v_ref = (refs[k * n + t] for k in range(4))
            d_ref, nm_ref, nv_ref = (refs[(4 + k) * n + t] for k in range(3))
            d_ref[...], nm_ref[...], nv_ref[...] = _adamw_math(w_ref[...], g_ref[...], m_ref[...], v_ref[...])

    vmem = pl.BlockSpec(memory_space=pltpu.VMEM)
    outs = [jax.ShapeDtypeStruct(w.shape, F32) for w in ws]
    res = pl.pallas_call(
        body,
        out_shape=outs * 3,
        in_specs=[vmem] * (4 * n),
        out_specs=[vmem] * (3 * n),
        name="adamw_small",
    )(*ws, *gs, *ms, *vs)
    return res[:n], res[n:2 * n], res[2 * n:]


IN_SIZES = (192, 192, 384, 384, 16, 512, 384, 384, 384)
IN_OFFS = tuple(int(v) for v in np.cumsum((0,) + IN_SIZES))
SMALL = ("norm_mix", "w_gla_gate", "b_gla_gate", "gla_norm", "b_dw", "conv_ln_g", "conv_ln_b", "rel_bias", "norm_ffn")


def _pad_cols(a, n):
    return jnp.pad(a, ((0, 0), (0, n - a.shape[1])))


W_IN_SHARD = 708
W_IN_ROWS = 736


def _pad_rows(a, n):
    return jnp.pad(a, ((0, n - a.shape[0]), (0, 0)))


def _split_w_in_t(w):
    s = [w[IN_OFFS[i]:IN_OFFS[i + 1]] for i in range(9)]
    w_gla = jnp.concatenate([_pad_rows(s[0], KW), _pad_rows(s[1], KW), s[2], s[3], _pad_rows(s[4], LRW)], axis=0)
    return w_gla, s[5], jnp.concatenate(s[6:9], axis=0)


def _join_w_in_t(g):
    gg = g["w_gla"]
    full = jnp.concatenate([gg[0:192], gg[KW:KW + 192], gg[2 * KW:2 * KW + VW], gg[2 * KW + VW:2 * KW + 2 * VW],
                            gg[2 * KW + 2 * VW:2 * KW + 2 * VW + 16], g["w_conv"], g["w_att"]], axis=0)
    return jnp.pad(full.reshape(4, W_IN_SHARD, D), ((0, 0), (0, W_IN_ROWS - W_IN_SHARD), (0, 0)))


def _pack(arrs, rows):
    flat = jnp.concatenate([a.reshape(-1) for a in arrs])
    return jnp.pad(flat, (0, rows * 128 - flat.shape[0])).reshape(rows, 128)


def _unpack(packed, shapes):
    flat = packed.reshape(-1)
    out, off = [], 0
    for s in shapes:
        n = int(np.prod(s))
        out.append(flat[off:off + n].reshape(s))
        off += n
    return out


def kernel(x, norm_mix, w_in, w_gla_gate, b_gla_gate, gla_norm, w_dw, b_dw, conv_ln_g, conv_ln_b, rel_bias, w_out, norm_ffn, w_up, w_down, norm_final, loss_target, m_norm_mix, m_w_in, m_w_gla_gate, m_b_gla_gate, m_gla_norm, m_w_dw, m_b_dw, m_conv_ln_g, m_conv_ln_b, m_rel_bias, m_w_out, m_norm_ffn, m_w_up, m_w_down, m_norm_final, v_norm_mix, v_w_in, v_w_gla_gate, v_b_gla_gate, v_gla_norm, v_w_dw, v_b_dw, v_conv_ln_g, v_conv_ln_b, v_rel_bias, v_w_out, v_norm_ffn, v_w_up, v_w_down, v_norm_final):
    P = dict(norm_mix=norm_mix, w_in=w_in, w_gla_gate=w_gla_gate, b_gla_gate=b_gla_gate, gla_norm=gla_norm, w_dw=w_dw, b_dw=b_dw,
             conv_ln_g=conv_ln_g, conv_ln_b=conv_ln_b, rel_bias=rel_bias, w_out=w_out, norm_ffn=norm_ffn, w_up=w_up,
             w_down=w_down, norm_final=norm_final)
    Mo = dict(norm_mix=m_norm_mix, w_in=m_w_in, w_gla_gate=m_w_gla_gate, b_gla_gate=m_b_gla_gate, gla_norm=m_gla_norm, w_dw=m_w_dw,
              b_dw=m_b_dw, conv_ln_g=m_conv_ln_g, conv_ln_b=m_conv_ln_b, rel_bias=m_rel_bias, w_out=m_w_out, norm_ffn=m_norm_ffn,
              w_up=m_w_up, w_down=m_w_down, norm_final=m_norm_final)
    Vo = dict(norm_mix=v_norm_mix, w_in=v_w_in, w_gla_gate=v_w_gla_gate, b_gla_gate=v_b_gla_gate, gla_norm=v_gla_norm, w_dw=v_w_dw,
              b_dw=v_b_dw, conv_ln_g=v_conv_ln_g, conv_ln_b=v_conv_ln_b, rel_bias=v_rel_bias, w_out=v_w_out, norm_ffn=v_norm_ffn,
              w_up=v_w_up, w_down=v_w_down, norm_final=v_norm_final)
    depth = w_in.shape[0]
    xi, yi, ci = lax.axis_index("x"), lax.axis_index("y"), lax.axis_index("c")
    chip = 2 * xi + yi

    plan = _Plan()
    _PLAN[0] = plan
    layers = [dict(
        norm_mix=norm_mix[l][None], wg=jnp.pad(w_gla_gate[l], ((0, LRW - 16), (0, KW - 192))),
        bg=_pad_cols(b_gla_gate[l][None], KW), gla_norm=gla_norm[l][None], b_dw=b_dw[l][None], ln_g=conv_ln_g[l][None],
        ln_b=conv_ln_b[l][None], rel_bias=rel_bias[l], norm_ffn=norm_ffn[l][None]) for l in range(depth)]

    w_in_t, m_w_in_t, v_w_in_t = (jnp.transpose(a, (2, 0, 1)) for a in (w_in, m_w_in, v_w_in))

    def w_in_shard(l):
        return _pad_rows(w_in_t[:, l, :], W_IN_ROWS).astype(BF16)

    def have_w_in(l, full):
        rows = jnp.concatenate([full[j, 0:W_IN_SHARD] for j in range(4)], axis=0)
        layers[l]["w_gla"], layers[l]["w_conv"], layers[l]["w_att"] = _split_w_in_t(rows)

    def have_w_out(l, full):
        w = full.reshape(D, D)
        layers[l]["w_out_g"], layers[l]["w_out_c"], layers[l]["w_out_a"] = w[0:VW], w[VW:VW + CW], w[VW + CW:]

    def have_w_up(l, full):
        layers[l]["w_up"] = full

    def have_w_down(l, full):
        layers[l]["w_down"] = full.reshape(D_FF, D)

    def have_w_dw(full):
        taps = full.reshape(4, depth, HALO, CW // 4)
        for l in range(depth):
            layers[l]["w_dw"] = jnp.transpose(taps[:, l], (1, 0, 2)).reshape(HALO, CW)

    first_d2d = []

    def first_ici(shard, have):
        return _phase_gather_ici(shard, lambda outs: first_d2d.append(_phase_gather_d2d(outs[0], lambda done: have(done[0]))))

    w_dw_pad = jnp.pad(w_dw, ((0, 0), (0, HALO - CK), (0, 0))).reshape(depth * HALO, CW // 4)
    _comm_only([first_ici(w_in_shard(0), functools.partial(have_w_in, 0)), first_ici(w_dw_pad, have_w_dw)],
               "gather_first_ici")
    _comm_only(first_d2d, "gather_first_d2d")

    def gather_behind(shard, ici_call, d2d_call, have):
        plan.at(ici_call, _phase_gather_ici(
            shard, lambda outs: plan.at(d2d_call, _phase_gather_d2d(outs[0], lambda done: have(done[0])))))

    for l in range(depth):
        if l > 0:
            gather_behind(w_in_shard(l), f"l{l - 1}_mlp_up", f"l{l - 1}_mlp_down", functools.partial(have_w_in, l))
        gather_behind(w_out[l].astype(BF16), f"l{l - 1}_mlp_down" if l > 0 else "l0_proj", f"l{l}_gla_fwd",
                      functools.partial(have_w_out, l))
        if l > 0:
            gather_behind(w_up[l].astype(BF16), f"l{l}_proj", f"l{l}_gla_fwd", functools.partial(have_w_up, l))
            gather_behind(w_down[l].astype(BF16), f"l{l}_gla_fwd", f"l{l}_att_fwd", functools.partial(have_w_down, l))
        else:
            gather_behind(w_up[l].astype(BF16), f"l{l}_gla_fwd", f"l{l}_att_fwd", functools.partial(have_w_up, l))
            gather_behind(w_down[l].astype(BF16), f"l{l}_att_fwd", f"l{l}_mlp_up", functools.partial(have_w_down, l))

    reduced = {}
    last_swap = []

    def reduce_calls(name, l):
        if name == "w_down":
            return f"l{l}_mlp_up_dx", f"l{l}_gla_bwd", f"l{l}_conv_bwd_dc"
        if name == "w_up":
            return f"l{l}_out_dx", f"l{l}_att_bwd", f"l{l}_proj_dw"
        if name == "w_out":
            return f"l{l}_gla_bwd", f"l{l}_conv_bwd_dc", f"l{l}_att_bwd"
        if l > 0:
            return f"l{l}_proj_dx", f"l{l - 1}_mlp_down_dw", f"l{l - 1}_mlp_up_dx"
        return None, "l0_proj_dx", None

    def reduce_behind(l, name, full):
        calls = reduce_calls(name, l)

        def swapped(outs):
            pair = _pair_add(full[:, None], outs[0][:, None], ci, f"reduce_pair_add_{name}{l}")[:, 0]
            plan.at(calls[1], _phase_chip_scatter(pair, scattered))

        def scattered(outs):
            half = _sum_chips(outs[0][:, None], f"reduce_sum_chips_{name}{l}")[0]
            phase = _phase_pair_allgather(half, l, depth, reduced.get(name), gathered)
            if calls[2] is None:
                last_swap.append(phase)
            else:
                plan.at(calls[2], phase)

        def gathered(outs):
            reduced[name] = outs[0]

        if calls[0] is None:
            _comm_only([_phase_pair_exchange(full, swapped)], f"reduce_pair_exchange_{name}{l}")
        else:
            plan.at(calls[0], _phase_pair_exchange(full, swapped))

    loss_part, grad_x, grads, g_final = _local_step(x[0], loss_target[0], layers, norm_final[None], reduce_behind)
    loss = lax.psum(loss_part[0, 0], ("x", "y", "c"))

    G, delta, new_m, new_v = {}, {}, {}, {}
    early = ("w_down", "w_up", "w_out")
    for name in early:
        G[name] = reduced[name]
    ds, nms, nvs = _adamw([P[k] for k in early], [G[k] for k in early], [Mo[k] for k in early], [Vo[k] for k in early],
                          "adamw_early")
    for i, name in enumerate(early):
        delta[name], new_m[name], new_v[name] = ds[i], nms[i], nvs[i]
    _PLAN[0] = None
    assert not plan.by_call, sorted(plan.by_call)

    small_g = []
    for l in range(depth):
        g = grads[l]
        small_g += [g["norm_mix"], g["wg"][0:16, 0:192], g["bg"][:, 0:192], g["gla_norm"], g["b_dw"], g["ln_g"], g["ln_b"],
                    g["rel_bias"], g["norm_ffn"], g["w_dw"][0:CK]]
    small_g.append(g_final)
    small_shapes = [a.shape for a in small_g]
    n_small = sum(int(np.prod(s)) for s in small_shapes)
    rows = -(-n_small // 1024) * 8
    red = _unpack(_allreduce_small(_pack(small_g, rows)), small_shapes)
    per = len(SMALL) + 1
    for i, name in enumerate(SMALL):
        G[name] = jnp.stack([red[l * per + i].reshape(P[name].shape[1:]) for l in range(depth)])
    gw_dw_all = jnp.stack([red[l * per + len(SMALL)] for l in range(depth)])
    G["w_dw"] = lax.dynamic_slice_in_dim(gw_dw_all, chip * (CW // 4), CW // 4, axis=2)
    G["norm_final"] = red[-1].reshape(norm_final.shape)

    _comm_only(last_swap, "reduce_pair_allgather_last")
    back = lambda a: jnp.transpose(a, (1, 2, 0))
    g_in_t = jnp.transpose(reduced["w_in"][:, 0:W_IN_SHARD, :], (1, 0, 2))
    d_in, nm_in, nv_in = _adamw_slabs(w_in_t, g_in_t, m_w_in_t, v_w_in_t, "adamw_w_in")
    G["w_in"], delta["w_in"], new_m["w_in"], new_v["w_in"] = back(g_in_t), back(d_in), back(nm_in), back(nv_in)

    small_names = list(SMALL) + ["w_dw", "norm_final"]
    two_d = lambda a: a.reshape(-1, a.shape[-1])
    ds, nms, nvs = _adamw_small([two_d(P[k]) for k in small_names], [two_d(G[k]) for k in small_names],
                                [two_d(Mo[k]) for k in small_names], [two_d(Vo[k]) for k in small_names])
    for i, name in enumerate(small_names):
        shp = P[name].shape
        delta[name], new_m[name], new_v[name] = ds[i].reshape(shp), nms[i].reshape(shp), nvs[i].reshape(shp)

    order = ["norm_mix", "w_in", "w_gla_gate", "b_gla_gate", "gla_norm", "w_dw", "b_dw", "conv_ln_g", "conv_ln_b", "rel_bias",
             "w_out", "norm_ffn", "w_up", "w_down", "norm_final"]
    return (loss, grad_x[None], *[G[k] for k in order], *[delta[k] for k in order], *[new_m[k] for k in order],
            *[new_v[k] for k in order])
```

```python
import functools

import numpy as np
import jax
import jax.numpy as jnp
from jax import lax
from jax.experimental import pallas as pl
from jax.experimental.pallas import tpu as pltpu

F32 = jnp.float32
BF16 = jnp.bfloat16
HI = lax.Precision.HIGHEST

D = 1024
CHUNK = 64
GLA_DK, GLA_DV, GLA_H = 48, 96, 4
KW = 256
VW = 384
LRW = 128
GLA_TAU = 16.0
CW = 256
CK = 31
AW = 384
AH = 6
BAND = 576
LEFT = 512
D_FF = 4096
EPS = 1e-6
NEG = -1e30
N_REL = 257

GLA_COLS = 2 * KW + 2 * VW + LRW
CONV_COLS = 2 * CW
ATT_COLS = 3 * AW

ADAM_LR, ADAM_B1, ADAM_B2, ADAM_EPS, ADAM_WD, ADAM_STEP = 0.001, 0.9, 0.999, 1e-08, 0.01, 10

VMEM_CAP = 56 * 1024 * 1024
MESH = pl.DeviceIdType.MESH


def _cparams(sem, vmem_bytes):
    limit = int(min(VMEM_CAP, max(vmem_bytes * 5 // 4 + (4 << 20), 16 << 20)))
    return pltpu.CompilerParams(dimension_semantics=sem, vmem_limit_bytes=limit)


def _out(shape, dtype):
    return pltpu.HBM(tuple(shape), dtype)


class _Comm:
    def __init__(self, ins, outs, aliases, sems, start, finish, then=None):
        self.ins, self.outs, self.aliases, self.sems = list(ins), list(outs), dict(aliases), list(sems)
        self.start, self.finish, self.then = start, finish, then


class _Plan:
    def __init__(self):
        self.by_call = {}

    def at(self, call, comm):
        self.by_call.setdefault(call, []).append(comm)

    def take(self, call):
        return self.by_call.pop(call, [])


_PLAN = [None]


def _pin(a):
    return pltpu.with_memory_space_constraint(a, pltpu.HBM) if jnp.issubdtype(a.dtype, jnp.floating) else a


def _pallas(body, **kw):
    comms = _PLAN[0].take(kw.get("name")) if _PLAN[0] is not None else []
    if not comms:
        call = pl.pallas_call(body, **kw)
        return lambda *args: call(*[_pin(a) for a in args])

    grid = tuple(kw.get("grid", ()))
    single = not isinstance(kw["out_shape"], (tuple, list))
    out_shape = [kw["out_shape"]] if single else list(kw["out_shape"])
    out_specs = [kw["out_specs"]] if single else list(kw["out_specs"])
    in_specs = list(kw["in_specs"])
    scratch = list(kw.get("scratch_shapes", ()))
    n_in, n_out, n_scr = len(in_specs), len(out_shape), len(scratch)
    c_in = sum(len(c.ins) for c in comms)
    c_out = sum(len(c.outs) for c in comms)
    aliases = dict(kw.get("input_output_aliases", {}))
    i0, o0 = n_in, n_out
    for c in comms:
        for i, o in c.aliases.items():
            aliases[i0 + i] = o0 + o
        i0 += len(c.ins)
        o0 += len(c.outs)

    def wrapped(*refs):
        ins, c_ins = refs[:n_in], refs[n_in:n_in + c_in]
        outs, c_outs = refs[n_in + c_in:n_in + c_in + n_out], refs[n_in + c_in + n_out:n_in + c_in + n_out + c_out]
        scr, c_sems = refs[n_in + c_in + n_out + c_out:][:n_scr], refs[n_in + c_in + n_out + c_out + n_scr:]

        def each(what):
            i0 = o0 = s0 = 0
            for c in comms:
                getattr(c, what)(c_ins[i0:i0 + len(c.ins)], c_outs[o0:o0 + len(c.outs)], c_sems[s0:s0 + len(c.sems)])
                i0, o0, s0 = i0 + len(c.ins), o0 + len(c.outs), s0 + len(c.sems)

        if grid:
            first = functools.reduce(jnp.logical_and, [pl.program_id(a) == 0 for a in range(len(grid))])
            last = functools.reduce(jnp.logical_and, [pl.program_id(a) == grid[a] - 1 for a in range(len(grid))])
            pl.when(first)(lambda: each("start"))
            body(*ins, *outs, *scr)
            pl.when(last)(lambda: each("finish"))
        else:
            each("start")
            body(*ins, *outs, *scr)
            each("finish")

    kw = dict(kw)
    kw["in_specs"] = in_specs + [ANY] * c_in
    kw["out_shape"] = out_shape + [_out(s.shape, s.dtype) for c in comms for s in c.outs]
    kw["out_specs"] = out_specs + [ANY] * c_out
    staging = [s for c in comms for s in c.sems if len(s) == 2 and not isinstance(s[1], int)]
    kw["scratch_shapes"] = scratch + [pltpu.VMEM(*s) if s in staging else pltpu.SemaphoreType.DMA(s) for c in comms for s in c.sems]
    kw["input_output_aliases"] = aliases
    extra = sum(_nbytes(*s) for s in staging)
    old = kw.get("compiler_params")
    limit = (old.vmem_limit_bytes if old is not None else 16 << 20) + extra
    kw["compiler_params"] = pltpu.CompilerParams(
        dimension_semantics=old.dimension_semantics if old is not None else None, vmem_limit_bytes=int(min(VMEM_CAP, limit)))
    call = pl.pallas_call(wrapped, **kw)

    def run(*args):
        res = call(*[_pin(a) for a in args], *[_pin(a) for c in comms for a in c.ins])
        o0 = n_out
        for c in comms:
            if c.then is not None:
                c.then(res[o0:o0 + len(c.outs)])
            o0 += len(c.outs)
        return res[0] if single else res[:n_out]

    return run


def _nbytes(shape, dtype):
    return int(np.prod(shape)) * jnp.dtype(dtype).itemsize


def _sigmoid(x):
    return 1.0 / (1.0 + jnp.exp(-x))


_DIMS = {"nn": (((1,), (0,)), ((), ())), "nt": (((1,), (1,)), ((), ())), "tn": (((0,), (0,)), ((), ()))}


def _mm(a, b, *, mode, out_dtype, name, tm=512, tn=None, tk=None, a_pro=None, epi=None, extra=None,
        b_chips=False, out_chips=False, norm_g=None, norm_bwd=None):
    b2 = (b.shape[1], 4 * b.shape[2]) if b_chips else b.shape
    if mode == "nn":
        (M, K), (K2, N) = a.shape, b2
    elif mode == "nt":
        (M, K), (N, K2) = a.shape, b2
    else:
        (K, M), (K2, N) = a.shape, b2
    assert K == K2, (a.shape, b.shape, mode)
    tm = min(tm, M)
    tn = N if tn is None else min(tn, N)
    tk = K if tk is None else min(tk, K)
    assert M % tm == 0 and N % tn == 0 and K % tk == 0, (M, N, K, tm, tn, tk)
    nk = K // tk
    a_blk = (tk, tm) if mode == "tn" else (tm, tk)
    a_map = (lambda i, j, k: (k, i)) if mode == "tn" else (lambda i, j, k: (i, k))
    b_blk = (tn, tk) if mode == "nt" else (tk, tn)
    b_map = (lambda i, j, k: (j, k)) if mode == "nt" else (lambda i, j, k: (k, j))
    if b_chips:
        per = b.shape[2] // b_blk[1]
        assert b.shape[2] % b_blk[1] == 0 and mode != "tn"
        flat_map = b_map
        b_map = lambda i, j, k: (flat_map(i, j, k)[1] // per, flat_map(i, j, k)[0], flat_map(i, j, k)[1] % per)
        b_blk = (None,) + b_blk
    in_specs = [pl.BlockSpec(a_blk, a_map), pl.BlockSpec(b_blk, b_map)]
    args = [a, b]
    if epi is not None:
        in_specs.append(pl.BlockSpec((tm, tn), lambda i, j, k: (i, j)))
        args.append(extra)
    if norm_g is not None:
        assert nk == 1 and mode != "tn"
        in_specs.append(pl.BlockSpec((1, K), lambda i, j, k: (0, 0)))
        args.append(norm_g)
    if norm_bwd is not None:
        assert tn == N
        row = pl.BlockSpec((tm, N), lambda i, j, k: (i, 0))
        in_specs += [row, pl.BlockSpec((1, N), lambda i, j, k: (0, 0)), row]
        args += list(norm_bwd)

    def body(*refs):
        it = iter(refs)
        a_ref, b_ref = next(it), next(it)
        e_ref = next(it) if epi is not None else None
        ng_ref = next(it) if norm_g is not None else None
        h_ref, g_ref, dres_ref = (next(it), next(it), next(it)) if norm_bwd is not None else (None, None, None)
        o_ref = next(it)
        xn_ref = next(it) if norm_g is not None else None
        dg_ref = next(it) if norm_bwd is not None else None
        if norm_bwd is not None:
            @pl.when((pl.program_id(0) == 0) & (pl.program_id(2) == 0))
            def _():
                dg_ref[...] = jnp.zeros_like(dg_ref)

        av = a_ref[...]
        if a_pro == "relu2":
            af = jnp.maximum(av.astype(F32), 0.0)
            av = af * af
        if norm_g is not None:
            av = (av * lax.rsqrt(jnp.mean(av * av, axis=-1, keepdims=True) + EPS) * ng_ref[...]).astype(BF16)
            xn_ref[...] = av
        p = lax.dot_general(av.astype(BF16), b_ref[...].astype(BF16), _DIMS[mode], preferred_element_type=F32)

        def finish(acc):
            if epi == "add":
                acc = acc + e_ref[...].astype(F32)
            elif epi == "relu2grad":
                acc = acc * (2.0 * jnp.maximum(e_ref[...].astype(F32), 0.0))
            if norm_bwd is None:
                o_ref[...] = acc.astype(o_ref.dtype)
                return

            x = h_ref[...]
            r = lax.rsqrt(jnp.mean(x * x, axis=-1, keepdims=True) + EPS)
            gy = acc * g_ref[...]
            dot = jnp.mean(x * gy, axis=-1, keepdims=True)
            o_ref[...] = dres_ref[...] + r * gy - x * (r * r * r * dot)
            dg_ref[...] += jnp.sum(acc * x * r, axis=0, keepdims=True)

        if nk == 1:
            finish(p)
        else:
            acc_ref = refs[-1]
            k = pl.program_id(2)

            @pl.when(k == 0)
            def _():
                acc_ref[...] = p

            @pl.when(k > 0)
            def _():
                acc_ref[...] += p

            @pl.when(k == nk - 1)
            def _():
                finish(acc_ref[...])

    vm = 2 * (_nbytes(a_blk, a.dtype) + _nbytes((tk, tn), b.dtype) + _nbytes((tm, tn), out_dtype))
    vm += 3 * _nbytes((tm, tn), F32)
    if epi is not None:
        vm += 2 * _nbytes((tm, tn), extra.dtype)
    if out_chips:
        per_out = N // 4 // tn
        assert N % (4 * tn) == 0
        out_shape = _out((4, M, N // 4), out_dtype)
        out_spec = pl.BlockSpec((None, tm, tn), lambda i, j, k: (j // per_out, i, j % per_out))
    else:
        out_shape = _out((M, N), out_dtype)
        out_spec = pl.BlockSpec((tm, tn), lambda i, j, k: (i, j))
    sem = ("parallel", "parallel", "arbitrary")
    if norm_g is not None:
        out_shape, out_spec = (out_shape, _out((M, K), BF16)), (out_spec, pl.BlockSpec((tm, K), lambda i, j, k: (i, 0)))
        vm += 4 * _nbytes((tm, K), F32)
    if norm_bwd is not None:
        out_shape, out_spec = (out_shape, _out((1, N), F32)), (out_spec, pl.BlockSpec((1, N), lambda i, j, k: (0, 0)))
        sem = ("arbitrary", "arbitrary", "arbitrary")
        vm += 8 * _nbytes((tm, N), F32)
    return _pallas(
        body,
        out_shape=out_shape,
        grid=(M // tm, N // tn, nk),
        in_specs=in_specs,
        out_specs=out_spec,
        scratch_shapes=[pltpu.VMEM((tm, tn), F32)] if nk > 1 else [],
        compiler_params=_cparams(sem, vm),
        name=name,
    )(*args)


def _mm_fan(a, bs, *, mode, out_dtypes, name, tm=512, norm_g=None):
    M, K = a.shape
    ns = [b.shape[1] if mode == "nn" else b.shape[0] for b in bs]
    n = len(bs)
    first_out = 1 + n + (norm_g is not None)

    def body(*refs):
        if norm_g is None:
            av = refs[0][...].astype(BF16)
        else:
            x = refs[0][...]
            av = (x * lax.rsqrt(jnp.mean(x * x, axis=-1, keepdims=True) + EPS) * refs[1 + n][...]).astype(BF16)
            refs[first_out + n][...] = av
        for i in range(n):
            refs[first_out + i][...] = lax.dot_general(av, refs[1 + i][...], _DIMS[mode],
                                                       preferred_element_type=F32).astype(refs[first_out + i].dtype)

    vm = 4 * _nbytes((tm, K), F32) + sum(2 * _nbytes(b.shape, b.dtype) + 3 * _nbytes((tm, nn), F32) for b, nn in zip(bs, ns))
    in_specs = [pl.BlockSpec((tm, K), lambda i: (i, 0))] + [pl.BlockSpec(b.shape, lambda i: (0, 0)) for b in bs]
    out_shape = [_out((M, nn), dt) for nn, dt in zip(ns, out_dtypes)]
    out_specs = [pl.BlockSpec((tm, nn), lambda i: (i, 0)) for nn in ns]
    args = [a, *bs]
    if norm_g is not None:
        in_specs.append(pl.BlockSpec((1, K), lambda i: (0, 0)))
        out_shape.append(_out((M, K), BF16))
        out_specs.append(pl.BlockSpec((tm, K), lambda i: (i, 0)))
        args.append(norm_g)
    return _pallas(
        body,
        out_shape=tuple(out_shape),
        grid=(M // tm,),
        in_specs=in_specs,
        out_specs=tuple(out_specs),
        compiler_params=_cparams(("parallel",), vm),
        name=name,
    )(*args)


def _mm_sum(as_, bs, *, mode, out_dtype, name, extra=None, norm_bwd=None, tm=512):
    M = as_[0].shape[0]
    N = bs[0].shape[1] if mode == "nn" else bs[0].shape[0]
    n = len(as_)

    def body(*refs):
        acc = None
        for i in range(n):
            p = lax.dot_general(refs[i][...].astype(BF16), refs[n + i][...], _DIMS[mode], preferred_element_type=F32)
            acc = p if acc is None else acc + p
        if extra is not None:
            acc = acc + refs[2 * n][...].astype(F32)
        if norm_bwd is None:
            refs[-1][...] = acc.astype(refs[-1].dtype)
        else:
            h_ref, g_ref, dres_ref, dh_ref, dg_ref = refs[-5:]

            @pl.when(pl.program_id(0) == 0)
            def _():
                dg_ref[...] = jnp.zeros_like(dg_ref)

            x = h_ref[...]
            r = lax.rsqrt(jnp.mean(x * x, axis=-1, keepdims=True) + EPS)
            gy = acc * g_ref[...]
            dot = jnp.mean(x * gy, axis=-1, keepdims=True)
            dh_ref[...] = dres_ref[...] + r * gy - x * (r * r * r * dot)
            dg_ref[...] += jnp.sum(acc * x * r, axis=0, keepdims=True)

    row = pl.BlockSpec((tm, N), lambda i: (i, 0))
    in_specs = [pl.BlockSpec((tm, a.shape[1]), lambda i: (i, 0)) for a in as_]
    in_specs += [pl.BlockSpec(b.shape, lambda i: (0, 0)) for b in bs]
    args = list(as_) + list(bs)
    if extra is not None:
        in_specs.append(row)
        args.append(extra)
    vm = sum(2 * _nbytes((tm, a.shape[1]), a.dtype) for a in as_) + sum(2 * _nbytes(b.shape, b.dtype) for b in bs)
    vm += 8 * _nbytes((tm, N), F32)
    if norm_bwd is None:
        out_shape, out_specs, sem = _out((M, N), out_dtype), row, "parallel"
    else:
        vec = pl.BlockSpec((1, N), lambda i: (0, 0))
        in_specs += [row, vec, row]
        args += list(norm_bwd)
        out_shape, out_specs, sem = (_out((M, N), F32), _out((1, N), F32)), (row, vec), "arbitrary"
        vm += 8 * _nbytes((tm, N), F32)
    return _pallas(
        body,
        out_shape=out_shape,
        grid=(M // tm,),
        in_specs=in_specs,
        out_specs=out_specs,
        compiler_params=_cparams((sem,), vm),
        name=name,
    )(*args)


def _mm_tn_multi(ops, pairs, *, name, tk=512):
    T = ops[0].shape[0]
    n, m = len(ops), len(pairs)
    shapes = [(ops[a].shape[1], ops[b].shape[1]) for a, b in pairs]

    def body(*refs):
        vals = [refs[i][...].astype(BF16) for i in range(n)]
        first = pl.program_id(0) == 0
        for j, (a, b) in enumerate(pairs):
            p = lax.dot_general(vals[a], vals[b], _DIMS["tn"], preferred_element_type=F32)
            o_ref = refs[n + j]

            @pl.when(first)
            def _(o_ref=o_ref, p=p):
                o_ref[...] = p

            @pl.when(jnp.logical_not(first))
            def _(o_ref=o_ref, p=p):
                o_ref[...] += p

    vm = sum(2 * _nbytes((tk, o.shape[1]), o.dtype) for o in ops) + sum(3 * _nbytes(s, F32) for s in shapes)
    return _pallas(
        body,
        out_shape=tuple(_out(s, F32) for s in shapes),
        grid=(T // tk,),
        in_specs=[pl.BlockSpec((tk, o.shape[1]), lambda k: (k, 0)) for o in ops],
        out_specs=tuple(pl.BlockSpec(s, lambda k: (0, 0)) for s in shapes),
        compiler_params=_cparams(("arbitrary",), vm),
        name=name,
    )(*ops)


def _rmsnorm_fwd(h, g, name, tm=512):
    T = h.shape[0]

    def body(h_ref, g_ref, o_ref):
        x = h_ref[...]
        r = lax.rsqrt(jnp.mean(x * x, axis=-1, keepdims=True) + EPS)
        o_ref[...] = (x * r * g_ref[...]).astype(o_ref.dtype)

    return _pallas(
        body,
        out_shape=_out((T, D), BF16),
        grid=(T // tm,),
        in_specs=[pl.BlockSpec((tm, D), lambda i: (i, 0)), pl.BlockSpec((1, D), lambda i: (0, 0))],
        out_specs=pl.BlockSpec((tm, D), lambda i: (i, 0)),
        compiler_params=_cparams(("parallel",), 8 * _nbytes((tm, D), F32)),
        name=name,
    )(h, g)


def _rmsnorm_bwd(dxn, h, g, dres, name, tm=512):
    T = h.shape[0]

    def body(dxn_ref, h_ref, g_ref, dres_ref, dh_ref, dg_ref):
        @pl.when(pl.program_id(0) == 0)
        def _():
            dg_ref[...] = jnp.zeros_like(dg_ref)

        x = h_ref[...]
        dy = dxn_ref[...].astype(F32)
        r = lax.rsqrt(jnp.mean(x * x, axis=-1, keepdims=True) + EPS)
        gy = dy * g_ref[...]
        dot = jnp.mean(x * gy, axis=-1, keepdims=True)
        dh_ref[...] = dres_ref[...] + r * gy - x * (r * r * r * dot)
        dg_ref[...] += jnp.sum(dy * x * r, axis=0, keepdims=True)

    row = pl.BlockSpec((tm, D), lambda i: (i, 0))
    vec = pl.BlockSpec((1, D), lambda i: (0, 0))
    return _pallas(
        body,
        out_shape=(_out((T, D), F32), _out((1, D), F32)),
        grid=(T // tm,),
        in_specs=[row, row, vec, row],
        out_specs=(row, vec),
        compiler_params=_cparams(("arbitrary",), 12 * _nbytes((tm, D), F32)),
        name=name,
    )(dxn, h, g, dres)


def _final_loss(h, g, target, name, tm=512):
    T = h.shape[0]

    def body(h_ref, g_ref, t_ref, loss_ref, dh_ref, dg_ref):
        @pl.when(pl.program_id(0) == 0)
        def _():
            dg_ref[...] = jnp.zeros_like(dg_ref)
            loss_ref[...] = jnp.zeros_like(loss_ref)

        x = h_ref[...]
        gg = g_ref[...]
        r = lax.rsqrt(jnp.mean(x * x, axis=-1, keepdims=True) + EPS)
        y = x * r * gg
        e = y - t_ref[...]
        loss_ref[...] += 0.5 * jnp.sum(jnp.mean(e * e, axis=-1, keepdims=True), axis=0, keepdims=True)
        dy = e * (1.0 / D)
        gy = dy * gg
        dot = jnp.mean(x * gy, axis=-1, keepdims=True)
        dh_ref[...] = r * gy - x * (r * r * r * dot)
        dg_ref[...] += jnp.sum(dy * x * r, axis=0, keepdims=True)

    row = pl.BlockSpec((tm, D), lambda i: (i, 0))
    vec = pl.BlockSpec((1, D), lambda i: (0, 0))
    one = pl.BlockSpec((1, 1), lambda i: (0, 0))
    return _pallas(
        body,
        out_shape=(_out((1, 1), F32), _out((T, D), F32), _out((1, D), F32)),
        grid=(T // tm,),
        in_specs=[row, vec, row],
        out_specs=(one, row, vec),
        compiler_params=_cparams(("arbitrary",), 12 * _nbytes((tm, D), F32)),
        name=name,
    )(h, g, target)


GLA_G = 8


def _gla_consts():
    i = np.arange(KW)[:, None]
    j = np.arange(VW)[None, :]
    mask = ((i // GLA_DK) == (j // GLA_DV)) & (i < GLA_H * GLA_DK)
    a = np.arange(VW)
    hm = ((a[:, None] // GLA_DV) == (a[None, :] // GLA_DV)).astype(np.float32)
    c = np.arange(CHUNK)
    low = (c[:, None] >= c[None, :]).astype(np.float32)
    return jnp.asarray(mask.astype(np.float32)), jnp.asarray(hm, BF16), jnp.asarray(low, BF16)


def _split(x):
    hi = x.astype(BF16)
    return hi, (x - hi.astype(F32)).astype(BF16)


def _dot_sel(a, b, dims, split):
    if split == "a":
        hi, lo = _split(a)
        return (lax.dot_general(hi, b, dims, preferred_element_type=F32) + lax.dot_general(lo, b, dims, preferred_element_type=F32))
    hi, lo = _split(b)
    return (lax.dot_general(a, hi, dims, preferred_element_type=F32) + lax.dot_general(a, lo, dims, preferred_element_type=F32))


def _dot3(a, b, dims):
    ah, al = _split(a)
    bh, bl = _split(b)
    return (lax.dot_general(ah, bh, dims, preferred_element_type=F32) + lax.dot_general(al, bh, dims, preferred_element_type=F32)
            + lax.dot_general(ah, bl, dims, preferred_element_type=F32))


def _dot3s(a_s, b_s, dims):
    (ah, al), (bh, bl) = a_s, b_s
    return (lax.dot_general(ah, bh, dims, preferred_element_type=F32) + lax.dot_general(al, bh, dims, preferred_element_type=F32)
            + lax.dot_general(ah, bl, dims, preferred_element_type=F32))


def _gla_group_common(p_ref, wg, bg):
    lr_s = _split(p_ref[:, 2 * KW + 2 * VW:GLA_COLS])
    wg_s = _split(wg)
    z = _dot3s(lr_s, wg_s, _DIMS["nn"]) + bg
    la = (jnp.minimum(z, 0.0) - jnp.log(1.0 + jnp.exp(-jnp.abs(z)))) * (1.0 / GLA_TAU)
    return lr_s, wg_s, z, _split(la)


def _gla_chunk_common(p_ref, rows, la_s, low, ones_v):
    q = p_ref[rows, 0:KW]
    k = p_ref[rows, KW:2 * KW]
    v = p_ref[rows, 2 * KW:2 * KW + VW]
    g = p_ref[rows, 2 * KW + VW:2 * KW + 2 * VW]
    la_h, la_l = la_s[0][rows], la_s[1][rows]
    cum = jnp.dot(low, la_h, preferred_element_type=F32) + jnp.dot(low, la_l, preferred_element_type=F32)
    endb = cum[CHUNK - 1:CHUNK, :]
    w = jnp.exp(endb - cum)
    a_full = jnp.exp(lax.dot_general(la_h, ones_v, _DIMS["tn"], preferred_element_type=F32)
                     + lax.dot_general(la_l, ones_v, _DIMS["tn"], preferred_element_type=F32))
    return q, k, v, g, w, endb, a_full


def _gla_fwd(p, wg, bg, gn, consts, name):
    T = p.shape[0]
    rb = CHUNK * GLA_G
    ng = T // rb
    mask, hm, low = consts[:3]
    scale = GLA_DK ** -0.5

    def body(p_ref, wg_ref, bg_ref, gn_ref, m_ref, hm_ref, l_ref, o_ref, st_ref, s_ref):
        @pl.when(pl.program_id(0) == 0)
        def _():
            s_ref[...] = jnp.zeros_like(s_ref)

        wg_v, bg_v, gn_v = wg_ref[...], bg_ref[...], gn_ref[...]
        ones_v = jnp.ones((CHUNK, VW), BF16)
        s_new = s_ref[...]
        _, _, _, la_s = _gla_group_common(p_ref, wg_v, bg_v)
        outs = []
        for c in range(GLA_G):
            rows = slice(c * CHUNK, (c + 1) * CHUNK)
            q, k, v, _, w, _, a_full = _gla_chunk_common(p_ref, rows, la_s, l_ref[...], ones_v)
            kd = (k * w).astype(BF16)
            kv = lax.dot_general(kd, v.astype(BF16), _DIMS["tn"], preferred_element_type=F32) * m_ref[...]
            s_new = a_full * s_new + kv
            st_ref[c] = s_new
            outs.append(jnp.dot((q * scale).astype(BF16), s_new.astype(BF16), preferred_element_type=F32))
        s_ref[...] = s_new
        o = jnp.concatenate(outs, axis=0)
        g = p_ref[:, 2 * KW + VW:2 * KW + 2 * VW]
        ms = _dot_sel(o * o, hm_ref[...], _DIMS["nn"], "a") * (1.0 / GLA_DV)
        o_ref[...] = (o * lax.rsqrt(ms + EPS) * gn_v * (g * _sigmoid(g))).astype(o_ref.dtype)

    full = lambda shape: pl.BlockSpec(shape, lambda i: tuple(0 for _ in shape))
    vm = 2 * _nbytes((rb, GLA_COLS), F32) + 2 * _nbytes((GLA_G, KW, VW), F32) + 12 * _nbytes((KW, VW), F32)
    return _pallas(
        body,
        out_shape=(_out((T, VW), BF16), _out((T // CHUNK, KW, VW), F32)),
        grid=(ng,),
        in_specs=[pl.BlockSpec((rb, GLA_COLS), lambda i: (i, 0)), full((LRW, KW)), full((1, KW)), full((1, VW)),
                  full((KW, VW)), full((VW, VW)), full((CHUNK, CHUNK))],
        out_specs=(pl.BlockSpec((rb, VW), lambda i: (i, 0)), pl.BlockSpec((GLA_G, KW, VW), lambda i: (i, 0, 0))),
        scratch_shapes=[pltpu.VMEM((KW, VW), F32)],
        compiler_params=_cparams(("arbitrary",), vm),
        name=name,
    )(p, wg, bg, gn, mask, hm, low)


def _gla_bwd(p, dy, states, wg, bg, gn, consts, name):
    T = p.shape[0]
    rb = CHUNK * GLA_G
    ng = T // rb
    mask, hm, low = consts[:3]
    scale = GLA_DK ** -0.5

    def body(p_ref, dy_ref, st_ref, sp_ref, wg_ref, bg_ref, gn_ref, m_ref, hm_ref, l_ref,
             dp_ref, dwg_ref, dbg_ref, dgn_ref, ga_ref):
        step = pl.program_id(0)

        @pl.when(step == 0)
        def _():
            ga_ref[...] = jnp.zeros_like(ga_ref)
            dwg_ref[...] = jnp.zeros_like(dwg_ref)
            dbg_ref[...] = jnp.zeros_like(dbg_ref)
            dgn_ref[...] = jnp.zeros_like(dgn_ref)

        first_group = step == ng - 1
        wg_v, bg_v, gn_v = wg_ref[...], bg_ref[...], gn_ref[...]
        ones_v = jnp.ones((CHUNK, VW), BF16)
        ones_8 = jnp.ones((8, VW), BF16)
        ga = ga_ref[...]
        lr_s, wg_s, z_all, la_s = _gla_group_common(p_ref, wg_v, bg_v)
        qss = [(p_ref[c * CHUNK:(c + 1) * CHUNK, 0:KW] * scale).astype(BF16) for c in range(GLA_G)]
        o = jnp.concatenate([jnp.dot(qss[c], st_ref[c].astype(BF16), preferred_element_type=F32) for c in range(GLA_G)], axis=0)
        g = p_ref[:, 2 * KW + VW:2 * KW + 2 * VW]
        dyv = dy_ref[...].astype(F32)
        r = lax.rsqrt(_dot_sel(o * o, hm_ref[...], _DIMS["nn"], "a") * (1.0 / GLA_DV) + EPS)
        on = o * r
        sg = _sigmoid(g)
        silu = g * sg
        d_on = dyv * gn_v * silu
        dp_ref[:, 2 * KW + VW:2 * KW + 2 * VW] = (dyv * on * gn_v * (sg * (1.0 + g * (1.0 - sg)))).astype(dp_ref.dtype)
        dgn_ref[...] += jnp.sum(dyv * on * silu, axis=0, keepdims=True)
        mo = _dot_sel(o * d_on, hm_ref[...], _DIMS["nn"], "a") * (1.0 / GLA_DV)
        dob_all = (r * d_on - o * (r * r * r) * mo).astype(BF16)
        dzs = [None] * GLA_G
        for c in reversed(range(GLA_G)):
            rows = slice(c * CHUNK, (c + 1) * CHUNK)
            _, k, v, _, w, endb, a_full = _gla_chunk_common(p_ref, rows, la_s, l_ref[...], ones_v)
            s_n = st_ref[c]
            if c > 0:
                s_prev = st_ref[c - 1]
            else:
                s_prev = jnp.where(first_group, 0.0, sp_ref[0])
            kd = k * w
            dob = dob_all[rows]
            dq = lax.dot_general(dob, s_n.astype(BF16), _DIMS["nt"], preferred_element_type=F32) * scale
            g_n = lax.dot_general(qss[c], dob, _DIMS["tn"], preferred_element_type=F32) * m_ref[...] + ga
            d_a = _dot_sel(ones_8, g_n * s_prev, _DIMS["nt"], "b")[0:1, :]
            g_nb = g_n.astype(BF16)
            dkd = lax.dot_general(v.astype(BF16), g_nb, _DIMS["nt"], preferred_element_type=F32)
            dv = jnp.dot(kd.astype(BF16), g_nb, preferred_element_type=F32)
            e = dkd * kd
            d_end = jnp.sum(e, axis=0, keepdims=True) + d_a * jnp.exp(endb)
            dla = _dot_sel(l_ref[...], -e, _DIMS["tn"], "b") + d_end
            dzs[c] = dla * (1.0 - _sigmoid(z_all[rows])) * (1.0 / GLA_TAU)
            ga = a_full * g_n
            dp_ref[rows, 0:KW] = dq.astype(dp_ref.dtype)
            dp_ref[rows, KW:2 * KW] = (dkd * w).astype(dp_ref.dtype)
            dp_ref[rows, 2 * KW:2 * KW + VW] = dv.astype(dp_ref.dtype)
        ga_ref[...] = ga
        dz = jnp.concatenate(dzs, axis=0)
        dz_s = _split(dz)
        dp_ref[:, 2 * KW + 2 * VW:GLA_COLS] = _dot3s(dz_s, wg_s, _DIMS["nt"]).astype(dp_ref.dtype)
        dwg_ref[...] += _dot3s(lr_s, dz_s, _DIMS["tn"])
        dbg_ref[...] += jnp.sum(dz, axis=0, keepdims=True)

    full = lambda shape: pl.BlockSpec(shape, lambda i: tuple(0 for _ in shape))
    rev = lambda i: (ng - 1 - i, 0)
    vm = 4 * _nbytes((rb, GLA_COLS), F32) + 2 * _nbytes((rb, VW), F32) + 2 * _nbytes((GLA_G + 1, KW, VW), F32)
    vm += 16 * _nbytes((KW, VW), F32)
    return _pallas(
        body,
        out_shape=(_out((T, GLA_COLS), BF16), _out((LRW, KW), F32),
                   _out((1, KW), F32), _out((1, VW), F32)),
        grid=(ng,),
        in_specs=[pl.BlockSpec((rb, GLA_COLS), rev), pl.BlockSpec((rb, VW), rev),
                  pl.BlockSpec((GLA_G, KW, VW), lambda i: (ng - 1 - i, 0, 0)),
                  pl.BlockSpec((1, KW, VW), lambda i: (jnp.maximum((ng - 1 - i) * GLA_G - 1, 0), 0, 0)),
                  full((LRW, KW)), full((1, KW)), full((1, VW)), full((KW, VW)), full((VW, VW)), full((CHUNK, CHUNK))],
        out_specs=(pl.BlockSpec((rb, GLA_COLS), rev), full((LRW, KW)), full((1, KW)), full((1, VW))),
        scratch_shapes=[pltpu.VMEM((KW, VW), F32)],
        compiler_params=_cparams(("arbitrary",), vm),
        name=name,
    )(p, dy, states, states, wg, bg, gn, mask, hm, low)


CONV_TM = 512
HALO = 32
CONV_RB = 64


def _glu(u):
    a = u[:, 0:CW]
    b = u[:, CW:2 * CW]
    return a * _sigmoid(b)


def _conv_taps(buf_ref, w_ref, rb0, first_tap):
    acc = jnp.zeros((CONV_RB, CW), F32)
    for j in range(CK):
        s = rb0 + first_tap(j)
        acc = acc + w_ref[j:j + 1, :] * buf_ref[s:s + CONV_RB, :]
    return acc


def _ln_fwd(c, lg, lb):
    mu = jnp.mean(c, axis=-1, keepdims=True)
    xc = c - mu
    rstd = lax.rsqrt(jnp.mean(xc * xc, axis=-1, keepdims=True) + EPS)
    n = xc * rstd
    return n, rstd, n * lg + lb


def _conv_fwd(u, w, b, lg, lb, name):
    T = u.shape[0]
    tm = CONV_TM

    def body(u_ref, uh_ref, w_ref, b_ref, lg_ref, lb_ref, o_ref, c_ref, hbuf):
        i = pl.program_id(0)
        hbuf[0:HALO, :] = jnp.where(i > 0, _glu(uh_ref[...]), 0.0)
        hbuf[HALO:HALO + tm, :] = _glu(u_ref[...])
        for r in range(tm // CONV_RB):
            rows = slice(r * CONV_RB, (r + 1) * CONV_RB)
            acc = _conv_taps(hbuf, w_ref, r * CONV_RB, lambda j: HALO - (CK - 1) + j)
            c_ref[rows, :] = acc
            _, _, zz = _ln_fwd(acc + b_ref[...], lg_ref[...], lb_ref[...])
            o_ref[rows, :] = (zz * _sigmoid(zz)).astype(o_ref.dtype)

    vec = pl.BlockSpec((1, CW), lambda i: (0, 0))
    return _pallas(
        body,
        out_shape=(_out((T, CW), BF16), _out((T, CW), F32)),
        grid=(T // tm,),
        in_specs=[pl.BlockSpec((tm, CONV_COLS), lambda i: (i, 0)),
                  pl.BlockSpec((HALO, CONV_COLS), lambda i: (jnp.maximum(i * (tm // HALO) - 1, 0), 0)),
                  pl.BlockSpec((HALO, CW), lambda i: (0, 0)), vec, vec, vec],
        out_specs=(pl.BlockSpec((tm, CW), lambda i: (i, 0)), pl.BlockSpec((tm, CW), lambda i: (i, 0))),
        scratch_shapes=[pltpu.VMEM((tm + HALO, CW), F32)],
        compiler_params=_cparams(("arbitrary",), 8 * _nbytes((tm, CONV_COLS), F32)),
        name=name,
    )(u, u, w, b, lg, lb)


def _conv_bwd_dc(u, conv, dout, b, lg, lb, name):
    T = u.shape[0]
    tm = CONV_TM
    nsteps = T // tm

    def body(u_ref, uh_ref, c_ref, do_ref, b_ref, lg_ref, lb_ref, dc_ref, dw_ref, db_ref, dlg_ref, dlb_ref, hbuf, dwacc):
        i = pl.program_id(0)

        @pl.when(i == 0)
        def _():
            dwacc[...] = jnp.zeros_like(dwacc)
            db_ref[...] = jnp.zeros_like(db_ref)
            dlg_ref[...] = jnp.zeros_like(dlg_ref)
            dlb_ref[...] = jnp.zeros_like(dlb_ref)

        hbuf[0:HALO, :] = jnp.where(i > 0, _glu(uh_ref[...]), 0.0)
        hbuf[HALO:HALO + tm, :] = _glu(u_ref[...])
        for r in range(tm // CONV_RB):
            rows = slice(r * CONV_RB, (r + 1) * CONV_RB)
            n, rstd, zz = _ln_fwd(c_ref[rows, :] + b_ref[...], lg_ref[...], lb_ref[...])
            sg = _sigmoid(zz)
            dz = do_ref[rows, :].astype(F32) * (sg * (1.0 + zz * (1.0 - sg)))
            dlg_ref[...] += jnp.sum(dz * n, axis=0, keepdims=True)
            dlb_ref[...] += jnp.sum(dz, axis=0, keepdims=True)
            dn = dz * lg_ref[...]
            dc = rstd * (dn - jnp.mean(dn, axis=-1, keepdims=True) - n * jnp.mean(dn * n, axis=-1, keepdims=True))
            dc_ref[rows, :] = dc
            db_ref[...] += jnp.sum(dc, axis=0, keepdims=True)
            for j in range(CK):
                s = r * CONV_RB + HALO - (CK - 1) + j
                prod = dc * hbuf[s:s + CONV_RB, :]
                dwacc[j] += jnp.sum(prod.reshape(CONV_RB // 8, 8, CW), axis=0)

        @pl.when(i == nsteps - 1)
        def _():
            dw_ref[...] = jnp.sum(dwacc[...], axis=1)

    vec = pl.BlockSpec((1, CW), lambda i: (0, 0))
    return _pallas(
        body,
        out_shape=(_out((T, CW), F32), _out((HALO, CW), F32),
                   _out((1, CW), F32), _out((1, CW), F32), _out((1, CW), F32)),
        grid=(nsteps,),
        in_specs=[pl.BlockSpec((tm, CONV_COLS), lambda i: (i, 0)),
                  pl.BlockSpec((HALO, CONV_COLS), lambda i: (jnp.maximum(i * (tm // HALO) - 1, 0), 0)),
                  pl.BlockSpec((tm, CW), lambda i: (i, 0)), pl.BlockSpec((tm, CW), lambda i: (i, 0)), vec, vec, vec],
        out_specs=(pl.BlockSpec((tm, CW), lambda i: (i, 0)), pl.BlockSpec((HALO, CW), lambda i: (0, 0)), vec, vec, vec),
        scratch_shapes=[pltpu.VMEM((tm + HALO, CW), F32), pltpu.VMEM((HALO, 8, CW), F32)],
        compiler_params=_cparams(("arbitrary",), 10 * _nbytes((tm, CONV_COLS), F32)),
        name=name,
    )(u, u, conv, dout, b, lg, lb)


def _conv_bwd_du(u, dc, w, name):
    T = u.shape[0]
    tm = CONV_TM
    nsteps = T // tm

    def body(u_ref, dc_ref, dch_ref, w_ref, du_ref, dcbuf):
        i = pl.program_id(0)
        dcbuf[0:tm, :] = dc_ref[...]
        dcbuf[tm:tm + HALO, :] = jnp.where(i < nsteps - 1, dch_ref[...], 0.0)
        for r in range(tm // CONV_RB):
            rows = slice(r * CONV_RB, (r + 1) * CONV_RB)
            dh = _conv_taps(dcbuf, w_ref, r * CONV_RB, lambda j: (CK - 1) - j)
            a = u_ref[rows, 0:CW]
            sb = _sigmoid(u_ref[rows, CW:2 * CW])
            du_ref[rows, 0:CW] = (dh * sb).astype(du_ref.dtype)
            du_ref[rows, CW:2 * CW] = (dh * a * sb * (1.0 - sb)).astype(du_ref.dtype)

    return _pallas(
        body,
        out_shape=_out((T, CONV_COLS), BF16),
        grid=(nsteps,),
        in_specs=[pl.BlockSpec((tm, CONV_COLS), lambda i: (i, 0)),
                  pl.BlockSpec((tm, CW), lambda i: (i, 0)),
                  pl.BlockSpec((HALO, CW), lambda i: (jnp.minimum((i + 1) * (tm // HALO), T // HALO - 1), 0)),
                  pl.BlockSpec((HALO, CW), lambda i: (0, 0))],
        out_specs=pl.BlockSpec((tm, CONV_COLS), lambda i: (i, 0)),
        scratch_shapes=[pltpu.VMEM((tm + HALO, CW), F32)],
        compiler_params=_cparams(("arbitrary",), 8 * _nbytes((tm, CONV_COLS), F32)),
        name=name,
    )(u, dc, dc, w)


ATT_G = 4


def _att_load_kv(p_any, kbuf, vbuf, sems, T):
    kc = pltpu.make_async_copy(p_any.at[:, pl.ds(AW, AW)], kbuf.at[pl.ds(LEFT, T), :], sems.at[0])
    vc = pltpu.make_async_copy(p_any.at[:, pl.ds(2 * AW, AW)], vbuf.at[pl.ds(LEFT, T), :], sems.at[1])
    kc.start()
    vc.start()
    kbuf[0:LEFT, :] = jnp.zeros((LEFT, AW), BF16)
    vbuf[0:LEFT, :] = jnp.zeros((LEFT, AW), BF16)
    kc.wait()
    vc.wait()


ATT_QB = CHUNK * ATT_G
ATT_KB = LEFT + ATT_QB
REL_PAD = 384
TOEP = 1024


def _att_consts():
    m = np.arange(TOEP)
    d = ATT_KB - 1 - m
    idx = np.clip(d, -128, 128) + 128
    sel = (np.arange(REL_PAD)[:, None] == idx[None, :]) & (m[None, :] < ATT_QB + ATT_KB - 1)
    return jnp.asarray(sel.astype(np.float32))


def _att_build_bias(rel_ref, sel_ref, bias_scr):
    tr = jnp.dot(rel_ref[...], sel_ref[...], precision=HI, preferred_element_type=F32)
    qc = lax.broadcasted_iota(jnp.int32, (ATT_QB, ATT_KB), 0) // CHUNK
    kc = lax.broadcasted_iota(jnp.int32, (ATT_QB, ATT_KB), 1) // CHUNK
    band = (kc >= qc) & (kc <= qc + 8)
    for h in range(AH):
        rows = jnp.broadcast_to(tr[h:h + 1, :], (ATT_QB, TOEP))
        toep = pltpu.roll(rows, TOEP - (ATT_QB - 1), 1, stride=1, stride_axis=0)[:, 0:ATT_KB]
        bias_scr[h // 2, (h % 2) * ATT_QB:(h % 2 + 1) * ATT_QB, :] = jnp.where(band, toep, NEG)


def _att_probs(qst, kb, bias_p, n0):
    sc = lax.dot_general(qst, kb, _DIMS["nt"], preferred_element_type=F32) * (64 ** -0.5) + bias_p
    pos = lax.broadcasted_iota(jnp.int32, (2 * ATT_QB, ATT_KB), 1)
    sc = jnp.where(pos >= CHUNK * (8 - n0), sc, NEG)
    mx = jnp.max(sc, axis=-1, keepdims=True)
    ex = jnp.exp(sc - mx)
    return ex / jnp.sum(ex, axis=-1, keepdims=True)


def _head_stack(a2, lo):
    zero = jnp.zeros_like(a2)
    return jnp.concatenate([jnp.where(lo, a2, zero), jnp.where(lo, zero, a2)], axis=0)


def _att_fwd(p, rel, sel, name):
    T = p.shape[0]

    def body(q_ref, p_any, rel_ref, sel_ref, o_ref, kbuf, vbuf, bias_scr, sems):
        i = pl.program_id(0)

        @pl.when(i == 0)
        def _():
            _att_load_kv(p_any, kbuf, vbuf, sems, T)
            _att_build_bias(rel_ref, sel_ref, bias_scr)

        lo = lax.broadcasted_iota(jnp.int32, (ATT_QB, 128), 1) < 64
        n0 = i * ATT_G
        start = pl.multiple_of(i * ATT_QB, ATT_QB)
        for hp in range(AH // 2):
            cols = slice(hp * 128, (hp + 1) * 128)
            kb = kbuf[pl.ds(start, ATT_KB), cols]
            vb = vbuf[pl.ds(start, ATT_KB), cols]
            pr = _att_probs(_head_stack(q_ref[:, cols], lo), kb, bias_scr[hp], n0)
            pv = jnp.dot(pr.astype(BF16), vb, preferred_element_type=F32)
            o_ref[:, cols] = jnp.where(lo, pv[0:ATT_QB], pv[ATT_QB:2 * ATT_QB]).astype(o_ref.dtype)

    vm = 2 * _nbytes((T + LEFT, AW), BF16) + 8 * _nbytes((2 * ATT_QB, ATT_KB), F32) + (8 << 20)
    return _pallas(
        body,
        out_shape=_out((T, AW), BF16),
        grid=(T // ATT_QB,),
        in_specs=[pl.BlockSpec((ATT_QB, AW), lambda i: (i, 0)), pl.BlockSpec(memory_space=pl.ANY),
                  pl.BlockSpec((8, REL_PAD), lambda i: (0, 0)), pl.BlockSpec((REL_PAD, TOEP), lambda i: (0, 0))],
        out_specs=pl.BlockSpec((ATT_QB, AW), lambda i: (i, 0)),
        scratch_shapes=[pltpu.VMEM((T + LEFT, AW), BF16), pltpu.VMEM((T + LEFT, AW), BF16),
                        pltpu.VMEM((AH // 2, 2 * ATT_QB, ATT_KB), F32), pltpu.SemaphoreType.DMA((2,))],
        compiler_params=_cparams(("arbitrary",), vm),
        name=name,
    )(p, p, rel, sel)


def _att_bwd(p, do, rel, sel, name):
    T = p.shape[0]
    nsteps = T // ATT_QB

    def body(q_ref, p_any, do_ref, rel_ref, sel_ref, dp_any, drel_ref,
             kbuf, vbuf, dqbuf, dkbuf, dvbuf, bias_scr, dbias_scr, dtr_scr, sems):
        i = pl.program_id(0)

        @pl.when(i == 0)
        def _():
            _att_load_kv(p_any, kbuf, vbuf, sems, T)
            _att_build_bias(rel_ref, sel_ref, bias_scr)
            dkbuf[...] = jnp.zeros_like(dkbuf)
            dvbuf[...] = jnp.zeros_like(dvbuf)
            dbias_scr[...] = jnp.zeros_like(dbias_scr)

        lo = lax.broadcasted_iota(jnp.int32, (ATT_QB, 128), 1) < 64
        n0 = i * ATT_G
        start = pl.multiple_of(i * ATT_QB, ATT_QB)
        for hp in range(AH // 2):
            cols = slice(hp * 128, (hp + 1) * 128)
            kb = kbuf[pl.ds(start, ATT_KB), cols]
            vb = vbuf[pl.ds(start, ATT_KB), cols]
            qst = _head_stack(q_ref[:, cols], lo)
            dost = _head_stack(do_ref[:, cols].astype(BF16), lo)
            pr = _att_probs(qst, kb, bias_scr[hp], n0)
            dpr = lax.dot_general(dost, vb, _DIMS["nt"], preferred_element_type=F32)
            ds = pr * (dpr - jnp.sum(dpr * pr, axis=-1, keepdims=True))
            dbias_scr[hp] += ds
            dsb = (ds * (64 ** -0.5)).astype(BF16)
            dq = jnp.dot(dsb, kb, preferred_element_type=F32)
            dqbuf[pl.ds(start, ATT_QB), cols] = jnp.where(lo, dq[0:ATT_QB], dq[ATT_QB:2 * ATT_QB]).astype(BF16)
            dkbuf[pl.ds(start, ATT_KB), cols] += lax.dot_general(dsb, qst, _DIMS["tn"], preferred_element_type=F32)
            dvbuf[pl.ds(start, ATT_KB), cols] += lax.dot_general(pr.astype(BF16), dost, _DIMS["tn"], preferred_element_type=F32)

        @pl.when(i == nsteps - 1)
        def _():
            kbuf[pl.ds(LEFT, T), :] = dkbuf[pl.ds(LEFT, T), :].astype(BF16)
            vbuf[pl.ds(LEFT, T), :] = dvbuf[pl.ds(LEFT, T), :].astype(BF16)
            cps = [pltpu.make_async_copy(dqbuf, dp_any.at[:, pl.ds(0, AW)], sems.at[0]),
                   pltpu.make_async_copy(kbuf.at[pl.ds(LEFT, T), :], dp_any.at[:, pl.ds(AW, AW)], sems.at[1]),
                   pltpu.make_async_copy(vbuf.at[pl.ds(LEFT, T), :], dp_any.at[:, pl.ds(2 * AW, AW)], sems.at[2])]
            for cp in cps:
                cp.start()
            dtr_scr[...] = jnp.zeros_like(dtr_scr)
            ri = lax.broadcasted_iota(jnp.int32, (ATT_QB, ATT_QB), 0)
            ci = lax.broadcasted_iota(jnp.int32, (ATT_QB, ATT_QB), 1)
            flip = jnp.where(ri + ci == ATT_QB - 1, 1.0, 0.0)
            for h in range(AH):
                db = dbias_scr[h // 2, (h % 2) * ATT_QB:(h % 2 + 1) * ATT_QB, :]
                db = jnp.dot(flip, db, precision=HI, preferred_element_type=F32)
                wide = jnp.concatenate([db, jnp.zeros((ATT_QB, TOEP - ATT_KB), F32)], axis=1)
                diag = pltpu.roll(wide, 0, 1, stride=1, stride_axis=0)
                dtr_scr[h:h + 1, :] = jnp.sum(diag, axis=0, keepdims=True)
            drel_ref[...] = lax.dot_general(dtr_scr[...], sel_ref[...], _DIMS["nt"], precision=HI, preferred_element_type=F32)
            for cp in cps:
                cp.wait()

    vm = 3 * _nbytes((T + LEFT, AW), BF16) + 2 * _nbytes((T + LEFT, AW), F32) + 12 * _nbytes((2 * ATT_QB, ATT_KB), F32) + (8 << 20)
    return _pallas(
        body,
        out_shape=(_out((T, ATT_COLS), BF16), _out((8, REL_PAD), F32)),
        grid=(nsteps,),
        in_specs=[pl.BlockSpec((ATT_QB, AW), lambda i: (i, 0)), pl.BlockSpec(memory_space=pl.ANY),
                  pl.BlockSpec((ATT_QB, AW), lambda i: (i, 0)),
                  pl.BlockSpec((8, REL_PAD), lambda i: (0, 0)), pl.BlockSpec((REL_PAD, TOEP), lambda i: (0, 0))],
        out_specs=(pl.BlockSpec(memory_space=pl.ANY), pl.BlockSpec((8, REL_PAD), lambda i: (0, 0))),
        scratch_shapes=[pltpu.VMEM((T + LEFT, AW), BF16), pltpu.VMEM((T + LEFT, AW), BF16), pltpu.VMEM((T, AW), BF16),
                        pltpu.VMEM((T + LEFT, AW), F32), pltpu.VMEM((T + LEFT, AW), F32),
                        pltpu.VMEM((AH // 2, 2 * ATT_QB, ATT_KB), F32), pltpu.VMEM((AH // 2, 2 * ATT_QB, ATT_KB), F32),
                        pltpu.VMEM((8, TOEP), F32), pltpu.SemaphoreType.DMA((3,))],
        compiler_params=_cparams(("arbitrary",), vm),
        name=name,
    )(p, p, do, rel, sel)


def _layer_fwd(h, wl, consts, tag):
    p_gla, p_conv, p_att, xn = _mm_fan(h, [wl["w_gla"], wl["w_conv"], wl["w_att"]], mode="nt", out_dtypes=(F32, F32, BF16),
                                       norm_g=wl["norm_mix"], name=f"{tag}_proj")
    o_gla, states = _gla_fwd(p_gla, wl["wg"], wl["bg"], wl["gla_norm"], consts, f"{tag}_gla_fwd")
    o_conv, conv = _conv_fwd(p_conv, wl["w_dw"], wl["b_dw"], wl["ln_g"], wl["ln_b"], f"{tag}_conv_fwd")
    rel = jnp.pad(wl["rel_bias"], ((0, 8 - AH), (0, REL_PAD - N_REL)))
    o_att = _att_fwd(p_att, rel, consts[3], f"{tag}_att_fwd")
    h1 = _mm_sum([o_gla, o_conv, o_att], [wl["w_out_g"], wl["w_out_c"], wl["w_out_a"]], mode="nn", out_dtype=F32, extra=h,
                 name=f"{tag}_out")
    u, xn2 = _mm(h1, wl["w_up"], mode="nn", out_dtype=BF16, tm=1024, tn=1024, b_chips=True, norm_g=wl["norm_ffn"],
                 name=f"{tag}_mlp_up")
    h2 = _mm(u, wl["w_down"], mode="nn", out_dtype=F32, tm=1024, tk=2048, a_pro="relu2", epi="add", extra=h1,
             name=f"{tag}_mlp_down")
    saved = dict(h=h, xn=xn, p_gla=p_gla, p_conv=p_conv, p_att=p_att, states=states, o_gla=o_gla, o_conv=o_conv, conv=conv,
                 o_att=o_att, rel=rel, h1=h1, xn2=xn2, u=u)
    return h2, saved


def _layer_bwd(dh2, sv, wl, consts, tag, emit=lambda name, grad: None):
    g = {}
    du = _mm(dh2, wl["w_down"], mode="nt", out_dtype=BF16, tm=1024, tn=1024, epi="relu2grad", extra=sv["u"],
             name=f"{tag}_mlp_down_dx")
    g["w_down"] = _mm(sv["u"], dh2, mode="tn", out_dtype=F32, tm=2048, tn=1024, tk=512, a_pro="relu2", name=f"{tag}_mlp_down_dw")
    emit("w_down", g["w_down"].reshape(4, D_FF // 4, D))
    dh1, g["norm_ffn"] = _mm(du, wl["w_up"], mode="nt", out_dtype=F32, tm=512, tk=1024, b_chips=True,
                             norm_bwd=(sv["h1"], wl["norm_ffn"], dh2), name=f"{tag}_mlp_up_dx")
    g["w_up"] = _mm(sv["xn2"], du, mode="tn", out_dtype=F32, tm=1024, tn=1024, tk=1024, out_chips=True, name=f"{tag}_mlp_up_dw")
    emit("w_up", g["w_up"])
    d_gla, d_conv, d_att = _mm_fan(dh1, [wl["w_out_g"], wl["w_out_c"], wl["w_out_a"]], mode="nt", out_dtypes=(F32, F32, F32),
                                   name=f"{tag}_out_dx")
    g["w_out_g"], g["w_out_c"], g["w_out_a"] = _mm_tn_multi([sv["o_gla"], sv["o_conv"], sv["o_att"], dh1],
                                                            [(0, 3), (1, 3), (2, 3)], name=f"{tag}_out_dw")
    emit("w_out", jnp.concatenate([g["w_out_g"], g["w_out_c"], g["w_out_a"]], axis=0).reshape(4, D // 4, D))
    dp_gla, g["wg"], g["bg"], g["gla_norm"] = _gla_bwd(sv["p_gla"], d_gla, sv["states"], wl["wg"], wl["bg"], wl["gla_norm"],
                                                       consts, f"{tag}_gla_bwd")
    dc, g["w_dw"], g["b_dw"], g["ln_g"], g["ln_b"] = _conv_bwd_dc(sv["p_conv"], sv["conv"], d_conv, wl["b_dw"], wl["ln_g"],
                                                                  wl["ln_b"], f"{tag}_conv_bwd_dc")
    dp_conv = _conv_bwd_du(sv["p_conv"], dc, wl["w_dw"], f"{tag}_conv_bwd_du")
    dp_att, drel = _att_bwd(sv["p_att"], d_att, sv["rel"], consts[3], f"{tag}_att_bwd")
    g["rel_bias"] = drel[0:AH, 0:N_REL]
    g["w_gla"], g["w_conv"], g["w_att"] = _mm_tn_multi([sv["xn"], dp_gla, dp_conv, dp_att], [(1, 0), (2, 0), (3, 0)],
                                                       name=f"{tag}_proj_dw")
    emit("w_in", _join_w_in_t(g))
    dh, g["norm_mix"] = _mm_sum([dp_gla, dp_conv, dp_att], [wl["w_gla"], wl["w_conv"], wl["w_att"]], mode="nn", out_dtype=F32,
                                norm_bwd=(sv["h"], wl["norm_mix"], dh1), name=f"{tag}_proj_dx")
    return dh, g


def _local_step(x, target, layers, norm_final, emit=lambda layer, name, grad: None):
    consts = _gla_consts() + (_att_consts(),)
    h = x
    saved = []
    for l, wl in enumerate(layers):
        h, sv = _layer_fwd(h, wl, consts, f"l{l}")
        saved.append(sv)
    loss, dh, g_final = _final_loss(h, norm_final, target, "final_loss")
    grads = [None] * len(layers)
    for l in reversed(range(len(layers))):
        dh, grads[l] = _layer_bwd(dh, saved[l], layers[l], consts, f"l{l}", functools.partial(emit, l))
    return loss, dh, grads, g_final


ANY = pl.BlockSpec(memory_space=pl.ANY)


def _place():
    x, y, c = lax.axis_index("x"), lax.axis_index("y"), lax.axis_index("c")
    chips = [(1 - x, y), (x, 1 - y), (1 - x, 1 - y)]
    return x, y, c, chips


def _shape(shape, dtype):
    return jax.ShapeDtypeStruct(tuple(shape), dtype)


def _remote(src, dst, send_sem, recv_sem, to):
    return pltpu.make_async_remote_copy(src_ref=src, dst_ref=dst, send_sem=send_sem, recv_sem=recv_sem,
                                        device_id=to, device_id_type=MESH)


class _Staged:
    def __init__(self, src, dst, buf, sems):
        self.load = pltpu.make_async_copy(src, buf, sems.at[0])
        self.store = pltpu.make_async_copy(buf, dst, sems.at[1])

    def start(self):
        self.load.start()

    def wait(self):
        self.load.wait()
        self.store.start()
        self.store.wait()


def _phase_gather_ici(src, then):
    R, C = src.shape
    rh = R // 2

    def copies(ins, outs, sems):
        x, y, c, chips = _place()
        me = 2 * x + y
        local = _Staged(ins[0], outs[0].at[me], sems[3], sems[2])
        sends = [_remote(ins[0].at[pl.ds(c * rh, rh), :], outs[0].at[me, pl.ds(c * rh, rh), :], sems[0].at[k], sems[1].at[k],
                         (px, py, c)) for k, (px, py) in enumerate(chips)]
        recvs = [_remote(outs[0].at[2 * px + py, pl.ds(c * rh, rh), :], outs[0].at[2 * px + py, pl.ds(c * rh, rh), :],
                         sems[0].at[k], sems[1].at[k], (px, py, c)) for k, (px, py) in enumerate(chips)]
        return local, sends, recvs

    def start(ins, outs, sems):
        local, sends, _ = copies(ins, outs, sems)
        local.start()
        for cp in sends:
            cp.start()

    def finish(ins, outs, sems):
        local, sends, recvs = copies(ins, outs, sems)
        for cp in recvs:
            cp.wait_recv()
        for cp in sends:
            cp.wait_send()
        local.wait()

    return _Comm([src], [_shape((4, R, C), src.dtype)], {}, [(3,), (3,), (2,), ((R, C), src.dtype)], start, finish, then)


def _phase_gather_d2d(part, then):
    _, R, C = part.shape
    rh = R // 2

    def copies(ins, outs, sems):
        x, y, c, chips = _place()
        sends = [_remote(ins[0].at[2 * px + py, pl.ds(c * rh, rh), :], outs[0].at[2 * px + py, pl.ds(c * rh, rh), :],
                         sems[0].at[k], sems[1].at[k], (x, y, 1 - c)) for k, (px, py) in enumerate(chips)]
        recvs = [_remote(outs[0].at[2 * px + py, pl.ds((1 - c) * rh, rh), :], outs[0].at[2 * px + py, pl.ds((1 - c) * rh, rh), :],
                         sems[0].at[k], sems[1].at[k], (x, y, 1 - c)) for k, (px, py) in enumerate(chips)]
        return sends, recvs

    def start(ins, outs, sems):
        for cp in copies(ins, outs, sems)[0]:
            cp.start()

    def finish(ins, outs, sems):
        sends, recvs = copies(ins, outs, sems)
        for cp in recvs:
            cp.wait_recv()
        for cp in sends:
            cp.wait_send()

    return _Comm([part], [_shape(part.shape, part.dtype)], {0: 0}, [(3,), (3,)], start, finish, then)


def _phase_pair_exchange(full, then, cols=False):
    _, R, C = full.shape
    rh, ch = (R, C // 2) if cols else (R // 2, C)

    def copy(ins, outs, sems):
        x, y, c, _ = _place()
        theirs = ins[0].at[:, :, pl.ds((1 - c) * ch, ch)] if cols else ins[0].at[:, pl.ds((1 - c) * rh, rh), :]
        return _remote(theirs, outs[0], sems[0].at[0], sems[1].at[0], (x, y, 1 - c))

    return _Comm([full], [_shape((4, rh, ch), full.dtype)], {}, [(1,), (1,)],
                 lambda ins, outs, sems: copy(ins, outs, sems).start(),
                 lambda ins, outs, sems: copy(ins, outs, sems).wait(), then)


def _phase_chip_scatter(parts, then):
    def copies(ins, outs, sems):
        x, y, c, chips = _place()
        me = 2 * x + y
        local = _Staged(ins[0].at[me], outs[0].at[me], sems[3], sems[2])
        sends = [_remote(ins[0].at[2 * px + py], outs[0].at[me], sems[0].at[k], sems[1].at[k], (px, py, c))
                 for k, (px, py) in enumerate(chips)]
        recvs = [_remote(outs[0].at[2 * px + py], outs[0].at[2 * px + py], sems[0].at[k], sems[1].at[k], (px, py, c))
                 for k, (px, py) in enumerate(chips)]
        return local, sends, recvs

    def start(ins, outs, sems):
        local, sends, _ = copies(ins, outs, sems)
        local.start()
        for cp in sends:
            cp.start()

    def finish(ins, outs, sems):
        local, sends, recvs = copies(ins, outs, sems)
        for cp in recvs:
            cp.wait_recv()
        for cp in sends:
            cp.wait_send()
        local.wait()

    return _Comm([parts], [_shape(parts.shape, parts.dtype)], {}, [(3,), (3,), (2,), (parts.shape[1:], parts.dtype)],
                 start, finish, then)


def _phase_pair_allgather(half, layer, depth, into, then, cols=False):
    rh, C = half.shape

    def copies(ins, outs, sems):
        x, y, c, _ = _place()
        if cols:
            mine = outs[0].at[layer, :, pl.ds(c * C, C)]
            theirs = outs[0].at[layer, :, pl.ds((1 - c) * C, C)]
        else:
            mine = outs[0].at[layer, pl.ds(c * rh, rh), :]
            theirs = outs[0].at[layer, pl.ds((1 - c) * rh, rh), :]
        return (_Staged(ins[0], mine, sems[3], sems[2]),
                _remote(ins[0], mine, sems[0].at[0], sems[1].at[0], (x, y, 1 - c)),
                _remote(theirs, theirs, sems[0].at[0], sems[1].at[0], (x, y, 1 - c)))

    def start(ins, outs, sems):
        local, send, _ = copies(ins, outs, sems)
        local.start()
        send.start()

    def finish(ins, outs, sems):
        local, send, recv = copies(ins, outs, sems)
        recv.wait_recv()
        send.wait_send()
        local.wait()

    ins = [half] if into is None else [half, into]
    whole = (depth, rh, 2 * C) if cols else (depth, 2 * rh, C)
    return _Comm(ins, [_shape(whole, half.dtype)], {} if into is None else {1: 0},
                 [(1,), (1,), (2,), ((rh, C), half.dtype)], start, finish, then)


def _comm_only(comms, name):
    plan = _Plan()
    for c in comms:
        plan.at(name, c)
    saved, _PLAN[0] = _PLAN[0], plan
    try:
        def body(o_ref):
            o_ref[...] = jnp.zeros_like(o_ref)

        _pallas(body, out_shape=[jax.ShapeDtypeStruct((8, 128), F32)], in_specs=[],
                out_specs=[pl.BlockSpec(memory_space=pltpu.VMEM)], name=name)()
    finally:
        _PLAN[0] = saved


def _row_tile(rows, cols, itemsize=4, budget=1 << 20, mult=8):
    fits = [t for t in range(mult, rows + 1, mult) if rows % t == 0 and t * cols * itemsize <= budget]
    return max(fits) if fits else rows


def _pair_add(full, got, c, name, cols=False):
    _, L, R, C = full.shape
    rh, ch = (R, C // 2) if cols else (R // 2, C)
    tr = _row_tile(rh, ch, budget=2 << 20, mult=16)
    nb = rh // tr

    def body(c_ref, a_ref, b_ref, o_ref):
        o_ref[...] = (a_ref[...] + b_ref[...]).astype(o_ref.dtype)

    mine = (lambda j, l, i, c_ref: (j, l, i, c_ref[0])) if cols else (lambda j, l, i, c_ref: (j, l, c_ref[0] * nb + i, 0))
    grid_spec = pltpu.PrefetchScalarGridSpec(
        num_scalar_prefetch=1,
        grid=(4, L, nb),
        in_specs=[pl.BlockSpec((1, 1, tr, ch), mine),
                  pl.BlockSpec((1, 1, tr, ch), lambda j, l, i, c_ref: (j, l, i, 0))],
        out_specs=pl.BlockSpec((1, 1, tr, ch), lambda j, l, i, c_ref: (j, l, i, 0)),
    )
    C = ch
    return _pallas(
        body,
        out_shape=_out((4, L, rh, C), BF16),
        grid_spec=grid_spec,
        compiler_params=_cparams(("parallel", "parallel", "parallel"), 8 * tr * C * 4),
        name=name,
    )(jnp.reshape(c, (1,)).astype(jnp.int32), full, got)


def _sum_chips(parts, name):
    _, L, rh, C = parts.shape
    tr = _row_tile(rh, C, budget=2 << 20, mult=16)

    def body(p_ref, o_ref):
        acc = p_ref[0].astype(F32)
        for j in range(1, 4):
            acc = acc + p_ref[j].astype(F32)
        o_ref[...] = acc

    return _pallas(
        body,
        out_shape=_out((L, rh, C), F32),
        grid=(L, rh // tr),
        in_specs=[pl.BlockSpec((4, 1, tr, C), lambda l, i: (0, l, i, 0))],
        out_specs=pl.BlockSpec((1, tr, C), lambda l, i: (l, i, 0)),
        compiler_params=_cparams(("parallel", "parallel"), 16 * tr * C * 4),
        name=name,
    )(parts)


def _allreduce_small(v):
    R = v.shape[0]

    def body(v_ref, o_ref, slots, send_sems, recv_sems):
        x, y, c, _ = _place()
        me = 4 * x + 2 * y + c
        slots[me] = v_ref[...]
        cps = []
        for r in range(1, 8):
            px, py, pc = x ^ (r >> 2), y ^ ((r >> 1) & 1), c ^ (r & 1)
            cps.append(pltpu.make_async_remote_copy(src_ref=v_ref, dst_ref=slots.at[me], send_sem=send_sems.at[r - 1],
                                                    recv_sem=recv_sems.at[r - 1], device_id=(px, py, pc), device_id_type=MESH))
            cps[-1].start()
        for r in range(1, 8):
            px, py, pc = x ^ (r >> 2), y ^ ((r >> 1) & 1), c ^ (r & 1)
            theirs = slots.at[4 * px + 2 * py + pc]
            pltpu.make_async_remote_copy(src_ref=theirs, dst_ref=theirs, send_sem=send_sems.at[r - 1], recv_sem=recv_sems.at[r - 1],
                                         device_id=(px, py, pc), device_id_type=MESH).wait_recv()
        acc = slots[0]
        for j in range(1, 8):
            acc = acc + slots[j]
        o_ref[...] = acc
        for cp in cps:
            cp.wait_send()

    return pl.pallas_call(
        body,
        out_shape=jax.ShapeDtypeStruct(v.shape, F32),
        in_specs=[pl.BlockSpec(memory_space=pltpu.VMEM)],
        out_specs=pl.BlockSpec(memory_space=pltpu.VMEM),
        scratch_shapes=[pltpu.VMEM((8, R, 128), F32), pltpu.SemaphoreType.DMA((7,)), pltpu.SemaphoreType.DMA((7,))],
        name="allreduce_small",
    )(v)


def _adamw_math(w, g, m, v):
    m = ADAM_B1 * m + (1.0 - ADAM_B1) * g
    v = ADAM_B2 * v + (1.0 - ADAM_B2) * (g * g)
    m_hat = m / (1.0 - ADAM_B1 ** ADAM_STEP)
    v_hat = v / (1.0 - ADAM_B2 ** ADAM_STEP)
    delta = -ADAM_LR * (m_hat / (jnp.sqrt(v_hat) + ADAM_EPS) + ADAM_WD * w)
    return delta, m, v


def _adamw(ws, gs, ms, vs, name, budget=1 << 19):
    n = len(ws)
    tiles = [_row_tile(w.shape[1], w.shape[2], budget=budget) for w in ws]
    per_layer = [w.shape[1] // t for w, t in zip(ws, tiles)]
    steps = [w.shape[0] * p for w, p in zip(ws, per_layer)]
    starts = [sum(steps[:k]) for k in range(n)]

    def body(*refs):
        i = pl.program_id(0)
        for k in range(n):
            w_ref, g_ref, m_ref, v_ref = (refs[j * n + k] for j in range(4))
            outs = [refs[(4 + j) * n + k] for j in range(3)]

            @pl.when((i >= starts[k]) & (i < starts[k] + steps[k]))
            def _(w_ref=w_ref, g_ref=g_ref, m_ref=m_ref, v_ref=v_ref, outs=outs):
                outs[0][...], outs[1][...], outs[2][...] = _adamw_math(w_ref[...], g_ref[...], m_ref[...], v_ref[...])

    def spec(k):
        def index(i):
            local = jnp.clip(i - starts[k], 0, steps[k] - 1)
            return local // per_layer[k], local % per_layer[k], 0
        return pl.BlockSpec((None, tiles[k], ws[k].shape[2]), index)

    specs = [spec(k) for k in range(n)]
    outs = [_out(w.shape, F32) for w in ws]
    res = _pallas(
        body,
        out_shape=tuple(outs * 3),
        grid=(sum(steps),),
        in_specs=specs * 4,
        out_specs=tuple(specs * 3),
        compiler_params=_cparams(("arbitrary",), sum(16 * t * w.shape[2] * 4 for w, t in zip(ws, tiles))),
        name=name,
    )(*ws, *gs, *ms, *vs)
    return res[:n], res[n:2 * n], res[2 * n:]


def _adamw_slabs(w, g, m, v, name, slabs=59):
    n, L, C = w.shape
    assert n % slabs == 0

    def body(w_ref, g_ref, m_ref, v_ref, d_ref, nm_ref, nv_ref):
        d_ref[...], nm_ref[...], nv_ref[...] = _adamw_math(w_ref[...], g_ref[...], m_ref[...], v_ref[...])

    blk = pl.BlockSpec((slabs, L, C), lambda i: (i, 0, 0))
    out = _out(w.shape, F32)
    return _pallas(
        body,
        out_shape=(out, out, out),
        grid=(n // slabs,),
        in_specs=[blk] * 4,
        out_specs=(blk, blk, blk),
        compiler_params=_cparams(("parallel",), 16 * slabs * 8 * C * 4),
        name=name,
    )(w, g, m, v)


def _adamw_small(ws, gs, ms, vs):
    n = len(ws)

    def body(*refs):
        for t in range(n):
            w_ref, g_ref, m_ref, v_ref = (refs[k * n + t] for k in range(4))
            d_ref, nm_ref, nv_ref = (refs[(4 + k) * n + t] for k in range(3))
            d_ref[...], nm_ref[...], nv_ref[...] = _adamw_math(w_ref[...], g_ref[...], m_ref[...], v_ref[...])

    vmem = pl.BlockSpec(memory_space=pltpu.VMEM)
    outs = [jax.ShapeDtypeStruct(w.shape, F32) for w in ws]
    res = pl.pallas_call(
        body,
        out_shape=outs * 3,
        in_specs=[vmem] * (4 * n),
        out_specs=[vmem] * (3 * n),
        name="adamw_small",
    )(*ws, *gs, *ms, *vs)
    return res[:n], res[n:2 * n], res[2 * n:]


IN_SIZES = (192, 192, 384, 384, 16, 512, 384, 384, 384)
IN_OFFS = tuple(int(v) for v in np.cumsum((0,) + IN_SIZES))
SMALL = ("norm_mix", "w_gla_gate", "b_gla_gate", "gla_norm", "b_dw", "conv_ln_g", "conv_ln_b", "rel_bias", "norm_ffn")


def _pad_cols(a, n):
    return jnp.pad(a, ((0, 0), (0, n - a.shape[1])))


W_IN_SHARD = 708
W_IN_ROWS = 736


def _pad_rows(a, n):
    return jnp.pad(a, ((0, n - a.shape[0]), (0, 0)))


def _split_w_in_t(w):
    s = [w[IN_OFFS[i]:IN_OFFS[i + 1]] for i in range(9)]
    w_gla = jnp.concatenate([_pad_rows(s[0], KW), _pad_rows(s[1], KW), s[2], s[3], _pad_rows(s[4], LRW)], axis=0)
    return w_gla, s[5], jnp.concatenate(s[6:9], axis=0)


def _join_w_in_t(g):
    gg = g["w_gla"]
    full = jnp.concatenate([gg[0:192], gg[KW:KW + 192], gg[2 * KW:2 * KW + VW], gg[2 * KW + VW:2 * KW + 2 * VW],
                            gg[2 * KW + 2 * VW:2 * KW + 2 * VW + 16], g["w_conv"], g["w_att"]], axis=0)
    return full.reshape(4, W_IN_SHARD, D)


def _pack(arrs, rows):
    flat = jnp.concatenate([a.reshape(-1) for a in arrs])
    return jnp.pad(flat, (0, rows * 128 - flat.shape[0])).reshape(rows, 128)


def _unpack(packed, shapes):
    flat = packed.reshape(-1)
    out, off = [], 0
    for s in shapes:
        n = int(np.prod(s))
        out.append(flat[off:off + n].reshape(s))
        off += n
    return out


def kernel(x, norm_mix, w_in, w_gla_gate, b_gla_gate, gla_norm, w_dw, b_dw, conv_ln_g, conv_ln_b, rel_bias, w_out, norm_ffn, w_up, w_down, norm_final, loss_target, m_norm_mix, m_w_in, m_w_gla_gate, m_b_gla_gate, m_gla_norm, m_w_dw, m_b_dw, m_conv_ln_g, m_conv_ln_b, m_rel_bias, m_w_out, m_norm_ffn, m_w_up, m_w_down, m_norm_final, v_norm_mix, v_w_in, v_w_gla_gate, v_b_gla_gate, v_gla_norm, v_w_dw, v_b_dw, v_conv_ln_g, v_conv_ln_b, v_rel_bias, v_w_out, v_norm_ffn, v_w_up, v_w_down, v_norm_final):
    P = dict(norm_mix=norm_mix, w_in=w_in, w_gla_gate=w_gla_gate, b_gla_gate=b_gla_gate, gla_norm=gla_norm, w_dw=w_dw, b_dw=b_dw,
             conv_ln_g=conv_ln_g, conv_ln_b=conv_ln_b, rel_bias=rel_bias, w_out=w_out, norm_ffn=norm_ffn, w_up=w_up,
             w_down=w_down, norm_final=norm_final)
    Mo = dict(norm_mix=m_norm_mix, w_in=m_w_in, w_gla_gate=m_w_gla_gate, b_gla_gate=m_b_gla_gate, gla_norm=m_gla_norm, w_dw=m_w_dw,
              b_dw=m_b_dw, conv_ln_g=m_conv_ln_g, conv_ln_b=m_conv_ln_b, rel_bias=m_rel_bias, w_out=m_w_out, norm_ffn=m_norm_ffn,
              w_up=m_w_up, w_down=m_w_down, norm_final=m_norm_final)
    Vo = dict(norm_mix=v_norm_mix, w_in=v_w_in, w_gla_gate=v_w_gla_gate, b_gla_gate=v_b_gla_gate, gla_norm=v_gla_norm, w_dw=v_w_dw,
              b_dw=v_b_dw, conv_ln_g=v_conv_ln_g, conv_ln_b=v_conv_ln_b, rel_bias=v_rel_bias, w_out=v_w_out, norm_ffn=v_norm_ffn,
              w_up=v_w_up, w_down=v_w_down, norm_final=v_norm_final)
    depth = w_in.shape[0]
    xi, yi, ci = lax.axis_index("x"), lax.axis_index("y"), lax.axis_index("c")
    chip = 2 * xi + yi

    plan = _Plan()
    _PLAN[0] = plan
    layers = [dict(
        norm_mix=norm_mix[l][None], wg=jnp.pad(w_gla_gate[l], ((0, LRW - 16), (0, KW - 192))),
        bg=_pad_cols(b_gla_gate[l][None], KW), gla_norm=gla_norm[l][None], b_dw=b_dw[l][None], ln_g=conv_ln_g[l][None],
        ln_b=conv_ln_b[l][None], rel_bias=rel_bias[l], norm_ffn=norm_ffn[l][None]) for l in range(depth)]

    w_in_t, m_w_in_t, v_w_in_t = (jnp.transpose(a, (2, 0, 1)) for a in (w_in, m_w_in, v_w_in))

    def w_in_shard(l):
        return _pad_rows(w_in_t[:, l, :], W_IN_ROWS).astype(BF16)

    def have_w_in(l, full):
        rows = jnp.concatenate([full[j, 0:W_IN_SHARD] for j in range(4)], axis=0)
        layers[l]["w_gla"], layers[l]["w_conv"], layers[l]["w_att"] = _split_w_in_t(rows)

    def have_w_out(l, full):
        w = full.reshape(D, D)
        layers[l]["w_out_g"], layers[l]["w_out_c"], layers[l]["w_out_a"] = w[0:VW], w[VW:VW + CW], w[VW + CW:]

    def have_w_up(l, full):
        layers[l]["w_up"] = full

    def have_w_down(l, full):
        layers[l]["w_down"] = full.reshape(D_FF, D)

    def have_w_dw(full):
        taps = full.reshape(4, depth, HALO, CW // 4)
        for l in range(depth):
            layers[l]["w_dw"] = jnp.transpose(taps[:, l], (1, 0, 2)).reshape(HALO, CW)

    first_d2d = []

    def first_ici(shard, have):
        return _phase_gather_ici(shard, lambda outs: first_d2d.append(_phase_gather_d2d(outs[0], lambda done: have(done[0]))))

    w_dw_pad = jnp.pad(w_dw, ((0, 0), (0, HALO - CK), (0, 0))).reshape(depth * HALO, CW // 4)
    _comm_only([first_ici(w_in_shard(0), functools.partial(have_w_in, 0)), first_ici(w_dw_pad, have_w_dw)],
               "gather_first_ici")
    _comm_only(first_d2d, "gather_first_d2d")

    def gather_behind(shard, ici_call, d2d_call, have):
        plan.at(ici_call, _phase_gather_ici(
            shard, lambda outs: plan.at(d2d_call, _phase_gather_d2d(outs[0], lambda done: have(done[0])))))

    for l in range(depth):
        if l > 0:
            gather_behind(w_in_shard(l), f"l{l - 1}_mlp_up", f"l{l - 1}_mlp_down", functools.partial(have_w_in, l))
        gather_behind(w_out[l].astype(BF16), f"l{l - 1}_mlp_down" if l > 0 else "l0_proj", f"l{l}_gla_fwd",
                      functools.partial(have_w_out, l))
        if l > 0:
            gather_behind(w_up[l].astype(BF16), f"l{l}_proj", f"l{l}_gla_fwd", functools.partial(have_w_up, l))
            gather_behind(w_down[l].astype(BF16), f"l{l}_gla_fwd", f"l{l}_att_fwd", functools.partial(have_w_down, l))
        else:
            gather_behind(w_up[l].astype(BF16), f"l{l}_gla_fwd", f"l{l}_att_fwd", functools.partial(have_w_up, l))
            gather_behind(w_down[l].astype(BF16), f"l{l}_att_fwd", f"l{l}_mlp_up", functools.partial(have_w_down, l))

    reduced = {}
    last_swap = []

    def reduce_calls(name, l):
        if name == "w_down":
            return f"l{l}_mlp_up_dx", f"l{l}_gla_bwd", f"l{l}_conv_bwd_dc"
        if name == "w_up":
            return f"l{l}_out_dx", f"l{l}_att_bwd", f"l{l}_proj_dw"
        if name == "w_out":
            return f"l{l}_gla_bwd", f"l{l}_conv_bwd_dc", f"l{l}_att_bwd"
        if l > 0:
            return f"l{l}_proj_dx", f"l{l - 1}_mlp_down_dw", f"l{l - 1}_mlp_up_dx"
        return None, "l0_proj_dx", None

    def reduce_behind(l, name, full):
        calls = reduce_calls(name, l)
        cols = name == "w_in"

        def swapped(outs):
            pair = _pair_add(full[:, None], outs[0][:, None], ci, f"reduce_pair_add_{name}{l}", cols)[:, 0]
            plan.at(calls[1], _phase_chip_scatter(pair, scattered))

        def scattered(outs):
            half = _sum_chips(outs[0][:, None], f"reduce_sum_chips_{name}{l}")[0]
            phase = _phase_pair_allgather(half, l, depth, reduced.get(name), gathered, cols)
            if calls[2] is None:
                last_swap.append(phase)
            else:
                plan.at(calls[2], phase)

        def gathered(outs):
            reduced[name] = outs[0]

        if calls[0] is None:
            _comm_only([_phase_pair_exchange(full, swapped, cols)], f"reduce_pair_exchange_{name}{l}")
        else:
            plan.at(calls[0], _phase_pair_exchange(full, swapped, cols))

    loss_part, grad_x, grads, g_final = _local_step(x[0], loss_target[0], layers, norm_final[None], reduce_behind)
    loss = lax.psum(loss_part[0, 0], ("x", "y", "c"))

    G, delta, new_m, new_v = {}, {}, {}, {}
    early = ("w_down", "w_up", "w_out")
    for name in early:
        G[name] = reduced[name]
    ds, nms, nvs = _adamw([P[k] for k in early], [G[k] for k in early], [Mo[k] for k in early], [Vo[k] for k in early],
                          "adamw_early")
    for i, name in enumerate(early):
        delta[name], new_m[name], new_v[name] = ds[i], nms[i], nvs[i]
    _PLAN[0] = None
    assert not plan.by_call, sorted(plan.by_call)

    small_g = []
    for l in range(depth):
        g = grads[l]
        small_g += [g["norm_mix"], g["wg"][0:16, 0:192], g["bg"][:, 0:192], g["gla_norm"], g["b_dw"], g["ln_g"], g["ln_b"],
                    g["rel_bias"], g["norm_ffn"], g["w_dw"][0:CK]]
    small_g.append(g_final)
    small_shapes = [a.shape for a in small_g]
    n_small = sum(int(np.prod(s)) for s in small_shapes)
    rows = -(-n_small // 1024) * 8
    red = _unpack(_allreduce_small(_pack(small_g, rows)), small_shapes)
    per = len(SMALL) + 1
    for i, name in enumerate(SMALL):
        G[name] = jnp.stack([red[l * per + i].reshape(P[name].shape[1:]) for l in range(depth)])
    gw_dw_all = jnp.stack([red[l * per + len(SMALL)] for l in range(depth)])
    G["w_dw"] = lax.dynamic_slice_in_dim(gw_dw_all, chip * (CW // 4), CW // 4, axis=2)
    G["norm_final"] = red[-1].reshape(norm_final.shape)

    _comm_only(last_swap, "reduce_pair_allgather_last")
    back = lambda a: jnp.transpose(a, (1, 2, 0))
    g_in_t = jnp.transpose(reduced["w_in"], (1, 0, 2))
    d_in, nm_in, nv_in = _adamw_slabs(w_in_t, g_in_t, m_w_in_t, v_w_in_t, "adamw_w_in")
    G["w_in"], delta["w_in"], new_m["w_in"], new_v["w_in"] = back(g_in_t), back(d_in), back(nm_in), back(nv_in)

    small_names = list(SMALL) + ["w_dw", "norm_final"]
    two_d = lambda a: a.reshape(-1, a.shape[-1])
    ds, nms, nvs = _adamw_small([two_d(P[k]) for k in small_names], [two_d(G[k]) for k in small_names],
                                [two_d(Mo[k]) for k in small_names], [two_d(Vo[k]) for k in small_names])
    for i, name in enumerate(small_names):
        shp = P[name].shape
        delta[name], new_m[name], new_v[name] = ds[i].reshape(shp), nms[i].reshape(shp), nvs[i].reshape(shp)

    order = ["norm_mix", "w_in", "w_gla_gate", "b_gla_gate", "gla_norm", "w_dw", "b_dw", "conv_ln_g", "conv_ln_b", "rel_bias",
             "w_out", "norm_ffn", "w_up", "w_down", "norm_final"]
    return (loss, grad_x[None], *[G[k] for k in order], *[delta[k] for k in order], *[new_m[k] for k in order],
            *[new_v[k] for k in order])
```

```python
import functools

import numpy as np
import jax
import jax.numpy as jnp
from jax import lax
from jax.experimental import pallas as pl
from jax.experimental.pallas import tpu as pltpu

F32 = jnp.float32
BF16 = jnp.bfloat16
HI = lax.Precision.HIGHEST

D = 1024
CHUNK = 64
GLA_DK, GLA_DV, GLA_H = 48, 96, 4
KW = 256
VW = 384
LRW = 128
GLA_TAU = 16.0
CW = 256
CK = 31
AW = 384
AH = 6
BAND = 576
LEFT = 512
D_FF = 4096
EPS = 1e-6
NEG = -1e30
N_REL = 257

GLA_COLS = 2 * KW + 2 * VW + LRW
CONV_COLS = 2 * CW
ATT_COLS = 3 * AW

ADAM_LR, ADAM_B1, ADAM_B2, ADAM_EPS, ADAM_WD, ADAM_STEP = 0.001, 0.9, 0.999, 1e-08, 0.01, 10

VMEM_CAP = 56 * 1024 * 1024
MESH = pl.DeviceIdType.MESH


def _cparams(sem, vmem_bytes):
    limit = int(min(VMEM_CAP, max(vmem_bytes * 5 // 4 + (4 << 20), 16 << 20)))
    return pltpu.CompilerParams(dimension_semantics=sem, vmem_limit_bytes=limit)


def _out(shape, dtype):
    return pltpu.HBM(tuple(shape), dtype)


class _Comm:
    def __init__(self, ins, outs, aliases, sems, start, finish, then=None):
        self.ins, self.outs, self.aliases, self.sems = list(ins), list(outs), dict(aliases), list(sems)
        self.start, self.finish, self.then = start, finish, then


class _Plan:
    def __init__(self):
        self.by_call = {}

    def at(self, call, comm):
        self.by_call.setdefault(call, []).append(comm)

    def take(self, call):
        return self.by_call.pop(call, [])


_PLAN = [None]


def _pin(a):
    return pltpu.with_memory_space_constraint(a, pltpu.HBM) if jnp.issubdtype(a.dtype, jnp.floating) else a


def _pallas(body, **kw):
    comms = _PLAN[0].take(kw.get("name")) if _PLAN[0] is not None else []
    if not comms:
        call = pl.pallas_call(body, **kw)
        return lambda *args: call(*[_pin(a) for a in args])

    grid = tuple(kw.get("grid", ()))
    single = not isinstance(kw["out_shape"], (tuple, list))
    out_shape = [kw["out_shape"]] if single else list(kw["out_shape"])
    out_specs = [kw["out_specs"]] if single else list(kw["out_specs"])
    in_specs = list(kw["in_specs"])
    scratch = list(kw.get("scratch_shapes", ()))
    n_in, n_out, n_scr = len(in_specs), len(out_shape), len(scratch)
    c_in = sum(len(c.ins) for c in comms)
    c_out = sum(len(c.outs) for c in comms)
    aliases = dict(kw.get("input_output_aliases", {}))
    i0, o0 = n_in, n_out
    for c in comms:
        for i, o in c.aliases.items():
            aliases[i0 + i] = o0 + o
        i0 += len(c.ins)
        o0 += len(c.outs)

    def wrapped(*refs):
        ins, c_ins = refs[:n_in], refs[n_in:n_in + c_in]
        outs, c_outs = refs[n_in + c_in:n_in + c_in + n_out], refs[n_in + c_in + n_out:n_in + c_in + n_out + c_out]
        scr, c_sems = refs[n_in + c_in + n_out + c_out:][:n_scr], refs[n_in + c_in + n_out + c_out + n_scr:]

        def each(what):
            i0 = o0 = s0 = 0
            for c in comms:
                getattr(c, what)(c_ins[i0:i0 + len(c.ins)], c_outs[o0:o0 + len(c.outs)], c_sems[s0:s0 + len(c.sems)])
                i0, o0, s0 = i0 + len(c.ins), o0 + len(c.outs), s0 + len(c.sems)

        if grid:
            first = functools.reduce(jnp.logical_and, [pl.program_id(a) == 0 for a in range(len(grid))])
            last = functools.reduce(jnp.logical_and, [pl.program_id(a) == grid[a] - 1 for a in range(len(grid))])
            pl.when(first)(lambda: each("start"))
            body(*ins, *outs, *scr)
            pl.when(last)(lambda: each("finish"))
        else:
            each("start")
            body(*ins, *outs, *scr)
            each("finish")

    kw = dict(kw)
    kw["in_specs"] = in_specs + [ANY] * c_in
    kw["out_shape"] = out_shape + [_out(s.shape, s.dtype) for c in comms for s in c.outs]
    kw["out_specs"] = out_specs + [ANY] * c_out
    staging = [s for c in comms for s in c.sems if len(s) == 2 and not isinstance(s[1], int)]
    kw["scratch_shapes"] = scratch + [pltpu.VMEM(*s) if s in staging else pltpu.SemaphoreType.DMA(s) for c in comms for s in c.sems]
    kw["input_output_aliases"] = aliases
    extra = sum(_nbytes(*s) for s in staging)
    old = kw.get("compiler_params")
    limit = (old.vmem_limit_bytes if old is not None else 16 << 20) + extra
    kw["compiler_params"] = pltpu.CompilerParams(
        dimension_semantics=old.dimension_semantics if old is not None else None, vmem_limit_bytes=int(min(VMEM_CAP, limit)))
    call = pl.pallas_call(wrapped, **kw)

    def run(*args):
        res = call(*[_pin(a) for a in args], *[_pin(a) for c in comms for a in c.ins])
        o0 = n_out
        for c in comms:
            if c.then is not None:
                c.then(res[o0:o0 + len(c.outs)])
            o0 += len(c.outs)
        return res[0] if single else res[:n_out]

    return run


def _nbytes(shape, dtype):
    return int(np.prod(shape)) * jnp.dtype(dtype).itemsize


def _sigmoid(x):
    return 1.0 / (1.0 + jnp.exp(-x))


_DIMS = {"nn": (((1,), (0,)), ((), ())), "nt": (((1,), (1,)), ((), ())), "tn": (((0,), (0,)), ((), ()))}


def _mm(a, b, *, mode, out_dtype, name, tm=512, tn=None, tk=None, a_pro=None, epi=None, extra=None,
        b_chips=False, out_chips=False, norm_g=None, norm_bwd=None):
    b2 = (b.shape[1], 4 * b.shape[2]) if b_chips else b.shape
    if mode == "nn":
        (M, K), (K2, N) = a.shape, b2
    elif mode == "nt":
        (M, K), (N, K2) = a.shape, b2
    else:
        (K, M), (K2, N) = a.shape, b2
    assert K == K2, (a.shape, b.shape, mode)
    tm = min(tm, M)
    tn = N if tn is None else min(tn, N)
    tk = K if tk is None else min(tk, K)
    assert M % tm == 0 and N % tn == 0 and K % tk == 0, (M, N, K, tm, tn, tk)
    nk = K // tk
    a_blk = (tk, tm) if mode == "tn" else (tm, tk)
    a_map = (lambda i, j, k: (k, i)) if mode == "tn" else (lambda i, j, k: (i, k))
    b_blk = (tn, tk) if mode == "nt" else (tk, tn)
    b_map = (lambda i, j, k: (j, k)) if mode == "nt" else (lambda i, j, k: (k, j))
    if b_chips:
        per = b.shape[2] // b_blk[1]
        assert b.shape[2] % b_blk[1] == 0 and mode != "tn"
        flat_map = b_map
        b_map = lambda i, j, k: (flat_map(i, j, k)[1] // per, flat_map(i, j, k)[0], flat_map(i, j, k)[1] % per)
        b_blk = (None,) + b_blk
    in_specs = [pl.BlockSpec(a_blk, a_map), pl.BlockSpec(b_blk, b_map)]
    args = [a, b]
    if epi is not None:
        in_specs.append(pl.BlockSpec((tm, tn), lambda i, j, k: (i, j)))
        args.append(extra)
    if norm_g is not None:
        assert nk == 1 and mode != "tn"
        in_specs.append(pl.BlockSpec((1, K), lambda i, j, k: (0, 0)))
        args.append(norm_g)
    if norm_bwd is not None:
        assert tn == N
        row = pl.BlockSpec((tm, N), lambda i, j, k: (i, 0))
        in_specs += [row, pl.BlockSpec((1, N), lambda i, j, k: (0, 0)), row]
        args += list(norm_bwd)

    def body(*refs):
        it = iter(refs)
        a_ref, b_ref = next(it), next(it)
        e_ref = next(it) if epi is not None else None
        ng_ref = next(it) if norm_g is not None else None
        h_ref, g_ref, dres_ref = (next(it), next(it), next(it)) if norm_bwd is not None else (None, None, None)
        o_ref = next(it)
        xn_ref = next(it) if norm_g is not None else None
        dg_ref = next(it) if norm_bwd is not None else None
        if norm_bwd is not None:
            @pl.when((pl.program_id(0) == 0) & (pl.program_id(2) == 0))
            def _():
                dg_ref[...] = jnp.zeros_like(dg_ref)

        av = a_ref[...]
        if a_pro == "relu2":
            af = jnp.maximum(av.astype(F32), 0.0)
            av = af * af
        if norm_g is not None:
            av = (av * lax.rsqrt(jnp.mean(av * av, axis=-1, keepdims=True) + EPS) * ng_ref[...]).astype(BF16)
            xn_ref[...] = av
        p = lax.dot_general(av.astype(BF16), b_ref[...].astype(BF16), _DIMS[mode], preferred_element_type=F32)

        def finish(acc):
            if epi == "add":
                acc = acc + e_ref[...].astype(F32)
            elif epi == "relu2grad":
                acc = acc * (2.0 * jnp.maximum(e_ref[...].astype(F32), 0.0))
            if norm_bwd is None:
                o_ref[...] = acc.astype(o_ref.dtype)
                return

            x = h_ref[...]
            r = lax.rsqrt(jnp.mean(x * x, axis=-1, keepdims=True) + EPS)
            gy = acc * g_ref[...]
            dot = jnp.mean(x * gy, axis=-1, keepdims=True)
            o_ref[...] = dres_ref[...] + r * gy - x * (r * r * r * dot)
            dg_ref[...] += jnp.sum(acc * x * r, axis=0, keepdims=True)

        if nk == 1:
            finish(p)
        else:
            acc_ref = refs[-1]
            k = pl.program_id(2)

            @pl.when(k == 0)
            def _():
                acc_ref[...] = p

            @pl.when(k > 0)
            def _():
                acc_ref[...] += p

            @pl.when(k == nk - 1)
            def _():
                finish(acc_ref[...])

    vm = 2 * (_nbytes(a_blk, a.dtype) + _nbytes((tk, tn), b.dtype) + _nbytes((tm, tn), out_dtype))
    vm += 3 * _nbytes((tm, tn), F32)
    if epi is not None:
        vm += 2 * _nbytes((tm, tn), extra.dtype)
    if out_chips:
        per_out = N // 4 // tn
        assert N % (4 * tn) == 0
        out_shape = _out((4, M, N // 4), out_dtype)
        out_spec = pl.BlockSpec((None, tm, tn), lambda i, j, k: (j // per_out, i, j % per_out))
    else:
        out_shape = _out((M, N), out_dtype)
        out_spec = pl.BlockSpec((tm, tn), lambda i, j, k: (i, j))
    sem = ("parallel", "parallel", "arbitrary")
    if norm_g is not None:
        out_shape, out_spec = (out_shape, _out((M, K), BF16)), (out_spec, pl.BlockSpec((tm, K), lambda i, j, k: (i, 0)))
        vm += 4 * _nbytes((tm, K), F32)
    if norm_bwd is not None:
        out_shape, out_spec = (out_shape, _out((1, N), F32)), (out_spec, pl.BlockSpec((1, N), lambda i, j, k: (0, 0)))
        sem = ("arbitrary", "arbitrary", "arbitrary")
        vm += 8 * _nbytes((tm, N), F32)
    return _pallas(
        body,
        out_shape=out_shape,
        grid=(M // tm, N // tn, nk),
        in_specs=in_specs,
        out_specs=out_spec,
        scratch_shapes=[pltpu.VMEM((tm, tn), F32)] if nk > 1 else [],
        compiler_params=_cparams(sem, vm),
        name=name,
    )(*args)


def _mm_fan(a, bs, *, mode, out_dtypes, name, tm=512, norm_g=None):
    M, K = a.shape
    ns = [b.shape[1] if mode == "nn" else b.shape[0] for b in bs]
    n = len(bs)
    first_out = 1 + n + (norm_g is not None)

    def body(*refs):
        if norm_g is None:
            av = refs[0][...].astype(BF16)
        else:
            x = refs[0][...]
            av = (x * lax.rsqrt(jnp.mean(x * x, axis=-1, keepdims=True) + EPS) * refs[1 + n][...]).astype(BF16)
            refs[first_out + n][...] = av
        for i in range(n):
            refs[first_out + i][...] = lax.dot_general(av, refs[1 + i][...], _DIMS[mode],
                                                       preferred_element_type=F32).astype(refs[first_out + i].dtype)

    vm = 4 * _nbytes((tm, K), F32) + sum(2 * _nbytes(b.shape, b.dtype) + 3 * _nbytes((tm, nn), F32) for b, nn in zip(bs, ns))
    in_specs = [pl.BlockSpec((tm, K), lambda i: (i, 0))] + [pl.BlockSpec(b.shape, lambda i: (0, 0)) for b in bs]
    out_shape = [_out((M, nn), dt) for nn, dt in zip(ns, out_dtypes)]
    out_specs = [pl.BlockSpec((tm, nn), lambda i: (i, 0)) for nn in ns]
    args = [a, *bs]
    if norm_g is not None:
        in_specs.append(pl.BlockSpec((1, K), lambda i: (0, 0)))
        out_shape.append(_out((M, K), BF16))
        out_specs.append(pl.BlockSpec((tm, K), lambda i: (i, 0)))
        args.append(norm_g)
    return _pallas(
        body,
        out_shape=tuple(out_shape),
        grid=(M // tm,),
        in_specs=in_specs,
        out_specs=tuple(out_specs),
        compiler_params=_cparams(("parallel",), vm),
        name=name,
    )(*args)


def _mm_sum(as_, bs, *, mode, out_dtype, name, extra=None, norm_bwd=None, tm=512):
    M = as_[0].shape[0]
    N = bs[0].shape[1] if mode == "nn" else bs[0].shape[0]
    n = len(as_)

    def body(*refs):
        acc = None
        for i in range(n):
            p = lax.dot_general(refs[i][...].astype(BF16), refs[n + i][...], _DIMS[mode], preferred_element_type=F32)
            acc = p if acc is None else acc + p
        if extra is not None:
            acc = acc + refs[2 * n][...].astype(F32)
        if norm_bwd is None:
            refs[-1][...] = acc.astype(refs[-1].dtype)
        else:
            h_ref, g_ref, dres_ref, dh_ref, dg_ref = refs[-5:]

            @pl.when(pl.program_id(0) == 0)
            def _():
                dg_ref[...] = jnp.zeros_like(dg_ref)

            x = h_ref[...]
            r = lax.rsqrt(jnp.mean(x * x, axis=-1, keepdims=True) + EPS)
            gy = acc * g_ref[...]
            dot = jnp.mean(x * gy, axis=-1, keepdims=True)
            dh_ref[...] = dres_ref[...] + r * gy - x * (r * r * r * dot)
            dg_ref[...] += jnp.sum(acc * x * r, axis=0, keepdims=True)

    row = pl.BlockSpec((tm, N), lambda i: (i, 0))
    in_specs = [pl.BlockSpec((tm, a.shape[1]), lambda i: (i, 0)) for a in as_]
    in_specs += [pl.BlockSpec(b.shape, lambda i: (0, 0)) for b in bs]
    args = list(as_) + list(bs)
    if extra is not None:
        in_specs.append(row)
        args.append(extra)
    vm = sum(2 * _nbytes((tm, a.shape[1]), a.dtype) for a in as_) + sum(2 * _nbytes(b.shape, b.dtype) for b in bs)
    vm += 8 * _nbytes((tm, N), F32)
    if norm_bwd is None:
        out_shape, out_specs, sem = _out((M, N), out_dtype), row, "parallel"
    else:
        vec = pl.BlockSpec((1, N), lambda i: (0, 0))
        in_specs += [row, vec, row]
        args += list(norm_bwd)
        out_shape, out_specs, sem = (_out((M, N), F32), _out((1, N), F32)), (row, vec), "arbitrary"
        vm += 8 * _nbytes((tm, N), F32)
    return _pallas(
        body,
        out_shape=out_shape,
        grid=(M // tm,),
        in_specs=in_specs,
        out_specs=out_specs,
        compiler_params=_cparams((sem,), vm),
        name=name,
    )(*args)


def _mm_tn_multi(ops, pairs, *, name, tk=512):
    T = ops[0].shape[0]
    n, m = len(ops), len(pairs)
    shapes = [(ops[a].shape[1], ops[b].shape[1]) for a, b in pairs]

    def body(*refs):
        vals = [refs[i][...].astype(BF16) for i in range(n)]
        first = pl.program_id(0) == 0
        for j, (a, b) in enumerate(pairs):
            p = lax.dot_general(vals[a], vals[b], _DIMS["tn"], preferred_element_type=F32)
            o_ref = refs[n + j]

            @pl.when(first)
            def _(o_ref=o_ref, p=p):
                o_ref[...] = p

            @pl.when(jnp.logical_not(first))
            def _(o_ref=o_ref, p=p):
                o_ref[...] += p

    vm = sum(2 * _nbytes((tk, o.shape[1]), o.dtype) for o in ops) + sum(3 * _nbytes(s, F32) for s in shapes)
    return _pallas(
        body,
        out_shape=tuple(_out(s, F32) for s in shapes),
        grid=(T // tk,),
        in_specs=[pl.BlockSpec((tk, o.shape[1]), lambda k: (k, 0)) for o in ops],
        out_specs=tuple(pl.BlockSpec(s, lambda k: (0, 0)) for s in shapes),
        compiler_params=_cparams(("arbitrary",), vm),
        name=name,
    )(*ops)


def _rmsnorm_fwd(h, g, name, tm=512):
    T = h.shape[0]

    def body(h_ref, g_ref, o_ref):
        x = h_ref[...]
        r = lax.rsqrt(jnp.mean(x * x, axis=-1, keepdims=True) + EPS)
        o_ref[...] = (x * r * g_ref[...]).astype(o_ref.dtype)

    return _pallas(
        body,
        out_shape=_out((T, D), BF16),
        grid=(T // tm,),
        in_specs=[pl.BlockSpec((tm, D), lambda i: (i, 0)), pl.BlockSpec((1, D), lambda i: (0, 0))],
        out_specs=pl.BlockSpec((tm, D), lambda i: (i, 0)),
        compiler_params=_cparams(("parallel",), 8 * _nbytes((tm, D), F32)),
        name=name,
    )(h, g)


def _rmsnorm_bwd(dxn, h, g, dres, name, tm=512):
    T = h.shape[0]

    def body(dxn_ref, h_ref, g_ref, dres_ref, dh_ref, dg_ref):
        @pl.when(pl.program_id(0) == 0)
        def _():
            dg_ref[...] = jnp.zeros_like(dg_ref)

        x = h_ref[...]
        dy = dxn_ref[...].astype(F32)
        r = lax.rsqrt(jnp.mean(x * x, axis=-1, keepdims=True) + EPS)
        gy = dy * g_ref[...]
        dot = jnp.mean(x * gy, axis=-1, keepdims=True)
        dh_ref[...] = dres_ref[...] + r * gy - x * (r * r * r * dot)
        dg_ref[...] += jnp.sum(dy * x * r, axis=0, keepdims=True)

    row = pl.BlockSpec((tm, D), lambda i: (i, 0))
    vec = pl.BlockSpec((1, D), lambda i: (0, 0))
    return _pallas(
        body,
        out_shape=(_out((T, D), F32), _out((1, D), F32)),
        grid=(T // tm,),
        in_specs=[row, row, vec, row],
        out_specs=(row, vec),
        compiler_params=_cparams(("arbitrary",), 12 * _nbytes((tm, D), F32)),
        name=name,
    )(dxn, h, g, dres)


def _final_loss(h, g, target, name, tm=512):
    T = h.shape[0]

    def body(h_ref, g_ref, t_ref, loss_ref, dh_ref, dg_ref):
        @pl.when(pl.program_id(0) == 0)
        def _():
            dg_ref[...] = jnp.zeros_like(dg_ref)
            loss_ref[...] = jnp.zeros_like(loss_ref)

        x = h_ref[...]
        gg = g_ref[...]
        r = lax.rsqrt(jnp.mean(x * x, axis=-1, keepdims=True) + EPS)
        y = x * r * gg
        e = y - t_ref[...]
        loss_ref[...] += 0.5 * jnp.sum(jnp.mean(e * e, axis=-1, keepdims=True), axis=0, keepdims=True)
        dy = e * (1.0 / D)
        gy = dy * gg
        dot = jnp.mean(x * gy, axis=-1, keepdims=True)
        dh_ref[...] = r * gy - x * (r * r * r * dot)
        dg_ref[...] += jnp.sum(dy * x * r, axis=0, keepdims=True)

    row = pl.BlockSpec((tm, D), lambda i: (i, 0))
    vec = pl.BlockSpec((1, D), lambda i: (0, 0))
    one = pl.BlockSpec((1, 1), lambda i: (0, 0))
    return _pallas(
        body,
        out_shape=(_out((1, 1), F32), _out((T, D), F32), _out((1, D), F32)),
        grid=(T // tm,),
        in_specs=[row, vec, row],
        out_specs=(one, row, vec),
        compiler_params=_cparams(("arbitrary",), 12 * _nbytes((tm, D), F32)),
        name=name,
    )(h, g, target)


GLA_G = 8


def _gla_consts():
    i = np.arange(KW)[:, None]
    j = np.arange(VW)[None, :]
    mask = ((i // GLA_DK) == (j // GLA_DV)) & (i < GLA_H * GLA_DK)
    a = np.arange(VW)
    hm = ((a[:, None] // GLA_DV) == (a[None, :] // GLA_DV)).astype(np.float32)
    c = np.arange(CHUNK)
    low = (c[:, None] >= c[None, :]).astype(np.float32)
    return jnp.asarray(mask.astype(np.float32)), jnp.asarray(hm, BF16), jnp.asarray(low, BF16)


def _split(x):
    hi = x.astype(BF16)
    return hi, (x - hi.astype(F32)).astype(BF16)


def _dot_sel(a, b, dims, split):
    if split == "a":
        hi, lo = _split(a)
        return (lax.dot_general(hi, b, dims, preferred_element_type=F32) + lax.dot_general(lo, b, dims, preferred_element_type=F32))
    hi, lo = _split(b)
    return (lax.dot_general(a, hi, dims, preferred_element_type=F32) + lax.dot_general(a, lo, dims, preferred_element_type=F32))


def _dot3(a, b, dims):
    ah, al = _split(a)
    bh, bl = _split(b)
    return (lax.dot_general(ah, bh, dims, preferred_element_type=F32) + lax.dot_general(al, bh, dims, preferred_element_type=F32)
            + lax.dot_general(ah, bl, dims, preferred_element_type=F32))


def _dot3s(a_s, b_s, dims):
    (ah, al), (bh, bl) = a_s, b_s
    return (lax.dot_general(ah, bh, dims, preferred_element_type=F32) + lax.dot_general(al, bh, dims, preferred_element_type=F32)
            + lax.dot_general(ah, bl, dims, preferred_element_type=F32))


def _gla_group_common(p_ref, wg, bg):
    lr_s = _split(p_ref[:, 2 * KW + 2 * VW:GLA_COLS])
    wg_s = _split(wg)
    z = _dot3s(lr_s, wg_s, _DIMS["nn"]) + bg
    la = (jnp.minimum(z, 0.0) - jnp.log(1.0 + jnp.exp(-jnp.abs(z)))) * (1.0 / GLA_TAU)
    return lr_s, wg_s, z, _split(la)


def _gla_chunk_common(p_ref, rows, la_s, low, ones_v):
    q = p_ref[rows, 0:KW]
    k = p_ref[rows, KW:2 * KW]
    v = p_ref[rows, 2 * KW:2 * KW + VW]
    g = p_ref[rows, 2 * KW + VW:2 * KW + 2 * VW]
    la_h, la_l = la_s[0][rows], la_s[1][rows]
    cum = jnp.dot(low, la_h, preferred_element_type=F32) + jnp.dot(low, la_l, preferred_element_type=F32)
    endb = cum[CHUNK - 1:CHUNK, :]
    w = jnp.exp(endb - cum)
    a_full = jnp.exp(lax.dot_general(la_h, ones_v, _DIMS["tn"], preferred_element_type=F32)
                     + lax.dot_general(la_l, ones_v, _DIMS["tn"], preferred_element_type=F32))
    return q, k, v, g, w, endb, a_full


def _gla_fwd(p, wg, bg, gn, consts, name):
    T = p.shape[0]
    rb = CHUNK * GLA_G
    ng = T // rb
    mask, hm, low = consts[:3]
    scale = GLA_DK ** -0.5

    def body(p_ref, wg_ref, bg_ref, gn_ref, m_ref, hm_ref, l_ref, o_ref, st_ref, s_ref):
        @pl.when(pl.program_id(0) == 0)
        def _():
            s_ref[...] = jnp.zeros_like(s_ref)

        wg_v, bg_v, gn_v = wg_ref[...], bg_ref[...], gn_ref[...]
        ones_v = jnp.ones((CHUNK, VW), BF16)
        s_new = s_ref[...]
        _, _, _, la_s = _gla_group_common(p_ref, wg_v, bg_v)
        outs = []
        for c in range(GLA_G):
            rows = slice(c * CHUNK, (c + 1) * CHUNK)
            q, k, v, _, w, _, a_full = _gla_chunk_common(p_ref, rows, la_s, l_ref[...], ones_v)
            kd = (k * w).astype(BF16)
            kv = lax.dot_general(kd, v.astype(BF16), _DIMS["tn"], preferred_element_type=F32) * m_ref[...]
            s_new = a_full * s_new + kv
            st_ref[c] = s_new
            outs.append(jnp.dot((q * scale).astype(BF16), s_new.astype(BF16), preferred_element_type=F32))
        s_ref[...] = s_new
        o = jnp.concatenate(outs, axis=0)
        g = p_ref[:, 2 * KW + VW:2 * KW + 2 * VW]
        ms = _dot_sel(o * o, hm_ref[...], _DIMS["nn"], "a") * (1.0 / GLA_DV)
        o_ref[...] = (o * lax.rsqrt(ms + EPS) * gn_v * (g * _sigmoid(g))).astype(o_ref.dtype)

    full = lambda shape: pl.BlockSpec(shape, lambda i: tuple(0 for _ in shape))
    vm = 2 * _nbytes((rb, GLA_COLS), F32) + 2 * _nbytes((GLA_G, KW, VW), F32) + 12 * _nbytes((KW, VW), F32)
    return _pallas(
        body,
        out_shape=(_out((T, VW), BF16), _out((T // CHUNK, KW, VW), F32)),
        grid=(ng,),
        in_specs=[pl.BlockSpec((rb, GLA_COLS), lambda i: (i, 0)), full((LRW, KW)), full((1, KW)), full((1, VW)),
                  full((KW, VW)), full((VW, VW)), full((CHUNK, CHUNK))],
        out_specs=(pl.BlockSpec((rb, VW), lambda i: (i, 0)), pl.BlockSpec((GLA_G, KW, VW), lambda i: (i, 0, 0))),
        scratch_shapes=[pltpu.VMEM((KW, VW), F32)],
        compiler_params=_cparams(("arbitrary",), vm),
        name=name,
    )(p, wg, bg, gn, mask, hm, low)


def _gla_bwd(p, dy, states, wg, bg, gn, consts, name):
    T = p.shape[0]
    rb = CHUNK * GLA_G
    ng = T // rb
    mask, hm, low = consts[:3]
    scale = GLA_DK ** -0.5

    def body(p_ref, dy_ref, st_ref, sp_ref, wg_ref, bg_ref, gn_ref, m_ref, hm_ref, l_ref,
             dp_ref, dwg_ref, dbg_ref, dgn_ref, ga_ref):
        step = pl.program_id(0)

        @pl.when(step == 0)
        def _():
            ga_ref[...] = jnp.zeros_like(ga_ref)
            dwg_ref[...] = jnp.zeros_like(dwg_ref)
            dbg_ref[...] = jnp.zeros_like(dbg_ref)
            dgn_ref[...] = jnp.zeros_like(dgn_ref)

        first_group = step == ng - 1
        wg_v, bg_v, gn_v = wg_ref[...], bg_ref[...], gn_ref[...]
        ones_v = jnp.ones((CHUNK, VW), BF16)
        ones_8 = jnp.ones((8, VW), BF16)
        ga = ga_ref[...]
        lr_s, wg_s, z_all, la_s = _gla_group_common(p_ref, wg_v, bg_v)
        qss = [(p_ref[c * CHUNK:(c + 1) * CHUNK, 0:KW] * scale).astype(BF16) for c in range(GLA_G)]
        o = jnp.concatenate([jnp.dot(qss[c], st_ref[c].astype(BF16), preferred_element_type=F32) for c in range(GLA_G)], axis=0)
        g = p_ref[:, 2 * KW + VW:2 * KW + 2 * VW]
        dyv = dy_ref[...].astype(F32)
        r = lax.rsqrt(_dot_sel(o * o, hm_ref[...], _DIMS["nn"], "a") * (1.0 / GLA_DV) + EPS)
        on = o * r
        sg = _sigmoid(g)
        silu = g * sg
        d_on = dyv * gn_v * silu
        dp_ref[:, 2 * KW + VW:2 * KW + 2 * VW] = (dyv * on * gn_v * (sg * (1.0 + g * (1.0 - sg)))).astype(dp_ref.dtype)
        dgn_ref[...] += jnp.sum(dyv * on * silu, axis=0, keepdims=True)
        mo = _dot_sel(o * d_on, hm_ref[...], _DIMS["nn"], "a") * (1.0 / GLA_DV)
        dob_all = (r * d_on - o * (r * r * r) * mo).astype(BF16)
        dzs = [None] * GLA_G
        for c in reversed(range(GLA_G)):
            rows = slice(c * CHUNK, (c + 1) * CHUNK)
            _, k, v, _, w, endb, a_full = _gla_chunk_common(p_ref, rows, la_s, l_ref[...], ones_v)
            s_n = st_ref[c]
            if c > 0:
                s_prev = st_ref[c - 1]
            else:
                s_prev = jnp.where(first_group, 0.0, sp_ref[0])
            kd = k * w
            dob = dob_all[rows]
            dq = lax.dot_general(dob, s_n.astype(BF16), _DIMS["nt"], preferred_element_type=F32) * scale
            g_n = lax.dot_general(qss[c], dob, _DIMS["tn"], preferred_element_type=F32) * m_ref[...] + ga
            d_a = _dot_sel(ones_8, g_n * s_prev, _DIMS["nt"], "b")[0:1, :]
            g_nb = g_n.astype(BF16)
            dkd = lax.dot_general(v.astype(BF16), g_nb, _DIMS["nt"], preferred_element_type=F32)
            dv = jnp.dot(kd.astype(BF16), g_nb, preferred_element_type=F32)
            e = dkd * kd
            d_end = jnp.sum(e, axis=0, keepdims=True) + d_a * jnp.exp(endb)
            dla = _dot_sel(l_ref[...], -e, _DIMS["tn"], "b") + d_end
            dzs[c] = dla * (1.0 - _sigmoid(z_all[rows])) * (1.0 / GLA_TAU)
            ga = a_full * g_n
            dp_ref[rows, 0:KW] = dq.astype(dp_ref.dtype)
            dp_ref[rows, KW:2 * KW] = (dkd * w).astype(dp_ref.dtype)
            dp_ref[rows, 2 * KW:2 * KW + VW] = dv.astype(dp_ref.dtype)
        ga_ref[...] = ga
        dz = jnp.concatenate(dzs, axis=0)
        dz_s = _split(dz)
        dp_ref[:, 2 * KW + 2 * VW:GLA_COLS] = _dot3s(dz_s, wg_s, _DIMS["nt"]).astype(dp_ref.dtype)
        dwg_ref[...] += _dot3s(lr_s, dz_s, _DIMS["tn"])
        dbg_ref[...] += jnp.sum(dz, axis=0, keepdims=True)

    full = lambda shape: pl.BlockSpec(shape, lambda i: tuple(0 for _ in shape))
    rev = lambda i: (ng - 1 - i, 0)
    vm = 4 * _nbytes((rb, GLA_COLS), F32) + 2 * _nbytes((rb, VW), F32) + 2 * _nbytes((GLA_G + 1, KW, VW), F32)
    vm += 16 * _nbytes((KW, VW), F32)
    return _pallas(
        body,
        out_shape=(_out((T, GLA_COLS), BF16), _out((LRW, KW), F32),
                   _out((1, KW), F32), _out((1, VW), F32)),
        grid=(ng,),
        in_specs=[pl.BlockSpec((rb, GLA_COLS), rev), pl.BlockSpec((rb, VW), rev),
                  pl.BlockSpec((GLA_G, KW, VW), lambda i: (ng - 1 - i, 0, 0)),
                  pl.BlockSpec((1, KW, VW), lambda i: (jnp.maximum((ng - 1 - i) * GLA_G - 1, 0), 0, 0)),
                  full((LRW, KW)), full((1, KW)), full((1, VW)), full((KW, VW)), full((VW, VW)), full((CHUNK, CHUNK))],
        out_specs=(pl.BlockSpec((rb, GLA_COLS), rev), full((LRW, KW)), full((1, KW)), full((1, VW))),
        scratch_shapes=[pltpu.VMEM((KW, VW), F32)],
        compiler_params=_cparams(("arbitrary",), vm),
        name=name,
    )(p, dy, states, states, wg, bg, gn, mask, hm, low)


CONV_TM = 512
HALO = 32
CONV_RB = 64


def _glu(u):
    a = u[:, 0:CW]
    b = u[:, CW:2 * CW]
    return a * _sigmoid(b)


def _conv_taps(buf_ref, w_ref, rb0, first_tap):
    acc = jnp.zeros((CONV_RB, CW), F32)
    for j in range(CK):
        s = rb0 + first_tap(j)
        acc = acc + w_ref[j:j + 1, :] * buf_ref[s:s + CONV_RB, :]
    return acc


def _ln_fwd(c, lg, lb):
    mu = jnp.mean(c, axis=-1, keepdims=True)
    xc = c - mu
    rstd = lax.rsqrt(jnp.mean(xc * xc, axis=-1, keepdims=True) + EPS)
    n = xc * rstd
    return n, rstd, n * lg + lb


def _conv_fwd(u, w, b, lg, lb, name):
    T = u.shape[0]
    tm = CONV_TM

    def body(u_ref, uh_ref, w_ref, b_ref, lg_ref, lb_ref, o_ref, c_ref, hbuf):
        i = pl.program_id(0)
        hbuf[0:HALO, :] = jnp.where(i > 0, _glu(uh_ref[...]), 0.0)
        hbuf[HALO:HALO + tm, :] = _glu(u_ref[...])
        for r in range(tm // CONV_RB):
            rows = slice(r * CONV_RB, (r + 1) * CONV_RB)
            acc = _conv_taps(hbuf, w_ref, r * CONV_RB, lambda j: HALO - (CK - 1) + j)
            c_ref[rows, :] = acc
            _, _, zz = _ln_fwd(acc + b_ref[...], lg_ref[...], lb_ref[...])
            o_ref[rows, :] = (zz * _sigmoid(zz)).astype(o_ref.dtype)

    vec = pl.BlockSpec((1, CW), lambda i: (0, 0))
    return _pallas(
        body,
        out_shape=(_out((T, CW), BF16), _out((T, CW), F32)),
        grid=(T // tm,),
        in_specs=[pl.BlockSpec((tm, CONV_COLS), lambda i: (i, 0)),
                  pl.BlockSpec((HALO, CONV_COLS), lambda i: (jnp.maximum(i * (tm // HALO) - 1, 0), 0)),
                  pl.BlockSpec((HALO, CW), lambda i: (0, 0)), vec, vec, vec],
        out_specs=(pl.BlockSpec((tm, CW), lambda i: (i, 0)), pl.BlockSpec((tm, CW), lambda i: (i, 0))),
        scratch_shapes=[pltpu.VMEM((tm + HALO, CW), F32)],
        compiler_params=_cparams(("arbitrary",), 8 * _nbytes((tm, CONV_COLS), F32)),
        name=name,
    )(u, u, w, b, lg, lb)


def _conv_bwd_dc(u, conv, dout, b, lg, lb, name):
    T = u.shape[0]
    tm = CONV_TM
    nsteps = T // tm

    def body(u_ref, uh_ref, c_ref, do_ref, b_ref, lg_ref, lb_ref, dc_ref, dw_ref, db_ref, dlg_ref, dlb_ref, hbuf, dwacc):
        i = pl.program_id(0)

        @pl.when(i == 0)
        def _():
            dwacc[...] = jnp.zeros_like(dwacc)
            db_ref[...] = jnp.zeros_like(db_ref)
            dlg_ref[...] = jnp.zeros_like(dlg_ref)
            dlb_ref[...] = jnp.zeros_like(dlb_ref)

        hbuf[0:HALO, :] = jnp.where(i > 0, _glu(uh_ref[...]), 0.0)
        hbuf[HALO:HALO + tm, :] = _glu(u_ref[...])
        for r in range(tm // CONV_RB):
            rows = slice(r * CONV_RB, (r + 1) * CONV_RB)
            n, rstd, zz = _ln_fwd(c_ref[rows, :] + b_ref[...], lg_ref[...], lb_ref[...])
            sg = _sigmoid(zz)
            dz = do_ref[rows, :].astype(F32) * (sg * (1.0 + zz * (1.0 - sg)))
            dlg_ref[...] += jnp.sum(dz * n, axis=0, keepdims=True)
            dlb_ref[...] += jnp.sum(dz, axis=0, keepdims=True)
            dn = dz * lg_ref[...]
            dc = rstd * (dn - jnp.mean(dn, axis=-1, keepdims=True) - n * jnp.mean(dn * n, axis=-1, keepdims=True))
            dc_ref[rows, :] = dc
            db_ref[...] += jnp.sum(dc, axis=0, keepdims=True)
            for j in range(CK):
                s = r * CONV_RB + HALO - (CK - 1) + j
                prod = dc * hbuf[s:s + CONV_RB, :]
                dwacc[j] += jnp.sum(prod.reshape(CONV_RB // 8, 8, CW), axis=0)

        @pl.when(i == nsteps - 1)
        def _():
            dw_ref[...] = jnp.sum(dwacc[...], axis=1)

    vec = pl.BlockSpec((1, CW), lambda i: (0, 0))
    return _pallas(
        body,
        out_shape=(_out((T, CW), F32), _out((HALO, CW), F32),
                   _out((1, CW), F32), _out((1, CW), F32), _out((1, CW), F32)),
        grid=(nsteps,),
        in_specs=[pl.BlockSpec((tm, CONV_COLS), lambda i: (i, 0)),
                  pl.BlockSpec((HALO, CONV_COLS), lambda i: (jnp.maximum(i * (tm // HALO) - 1, 0), 0)),
                  pl.BlockSpec((tm, CW), lambda i: (i, 0)), pl.BlockSpec((tm, CW), lambda i: (i, 0)), vec, vec, vec],
        out_specs=(pl.BlockSpec((tm, CW), lambda i: (i, 0)), pl.BlockSpec((HALO, CW), lambda i: (0, 0)), vec, vec, vec),
        scratch_shapes=[pltpu.VMEM((tm + HALO, CW), F32), pltpu.VMEM((HALO, 8, CW), F32)],
        compiler_params=_cparams(("arbitrary",), 10 * _nbytes((tm, CONV_COLS), F32)),
        name=name,
    )(u, u, conv, dout, b, lg, lb)


def _conv_bwd_du(u, dc, w, name):
    T = u.shape[0]
    tm = CONV_TM
    nsteps = T // tm

    def body(u_ref, dc_ref, dch_ref, w_ref, du_ref, dcbuf):
        i = pl.program_id(0)
        dcbuf[0:tm, :] = dc_ref[...]
        dcbuf[tm:tm + HALO, :] = jnp.where(i < nsteps - 1, dch_ref[...], 0.0)
        for r in range(tm // CONV_RB):
            rows = slice(r * CONV_RB, (r + 1) * CONV_RB)
            dh = _conv_taps(dcbuf, w_ref, r * CONV_RB, lambda j: (CK - 1) - j)
            a = u_ref[rows, 0:CW]
            sb = _sigmoid(u_ref[rows, CW:2 * CW])
            du_ref[rows, 0:CW] = (dh * sb).astype(du_ref.dtype)
            du_ref[rows, CW:2 * CW] = (dh * a * sb * (1.0 - sb)).astype(du_ref.dtype)

    return _pallas(
        body,
        out_shape=_out((T, CONV_COLS), BF16),
        grid=(nsteps,),
        in_specs=[pl.BlockSpec((tm, CONV_COLS), lambda i: (i, 0)),
                  pl.BlockSpec((tm, CW), lambda i: (i, 0)),
                  pl.BlockSpec((HALO, CW), lambda i: (jnp.minimum((i + 1) * (tm // HALO), T // HALO - 1), 0)),
                  pl.BlockSpec((HALO, CW), lambda i: (0, 0))],
        out_specs=pl.BlockSpec((tm, CONV_COLS), lambda i: (i, 0)),
        scratch_shapes=[pltpu.VMEM((tm + HALO, CW), F32)],
        compiler_params=_cparams(("arbitrary",), 8 * _nbytes((tm, CONV_COLS), F32)),
        name=name,
    )(u, dc, dc, w)


ATT_G = 4


def _att_load_kv(p_any, kbuf, vbuf, sems, T):
    kc = pltpu.make_async_copy(p_any.at[:, pl.ds(AW, AW)], kbuf.at[pl.ds(LEFT, T), :], sems.at[0])
    vc = pltpu.make_async_copy(p_any.at[:, pl.ds(2 * AW, AW)], vbuf.at[pl.ds(LEFT, T), :], sems.at[1])
    kc.start()
    vc.start()
    kbuf[0:LEFT, :] = jnp.zeros((LEFT, AW), BF16)
    vbuf[0:LEFT, :] = jnp.zeros((LEFT, AW), BF16)
    kc.wait()
    vc.wait()


ATT_QB = CHUNK * ATT_G
ATT_KB = LEFT + ATT_QB
REL_PAD = 384
TOEP = 1024


def _att_consts():
    m = np.arange(TOEP)
    d = ATT_KB - 1 - m
    idx = np.clip(d, -128, 128) + 128
    sel = (np.arange(REL_PAD)[:, None] == idx[None, :]) & (m[None, :] < ATT_QB + ATT_KB - 1)
    return jnp.asarray(sel.astype(np.float32))


def _att_build_bias(rel_ref, sel_ref, bias_scr):
    tr = jnp.dot(rel_ref[...], sel_ref[...], precision=HI, preferred_element_type=F32)
    qc = lax.broadcasted_iota(jnp.int32, (ATT_QB, ATT_KB), 0) // CHUNK
    kc = lax.broadcasted_iota(jnp.int32, (ATT_QB, ATT_KB), 1) // CHUNK
    band = (kc >= qc) & (kc <= qc + 8)
    for h in range(AH):
        rows = jnp.broadcast_to(tr[h:h + 1, :], (ATT_QB, TOEP))
        toep = pltpu.roll(rows, TOEP - (ATT_QB - 1), 1, stride=1, stride_axis=0)[:, 0:ATT_KB]
        bias_scr[h // 2, (h % 2) * ATT_QB:(h % 2 + 1) * ATT_QB, :] = jnp.where(band, toep, NEG)


def _att_probs(qst, kb, bias_p, n0):
    sc = lax.dot_general(qst, kb, _DIMS["nt"], preferred_element_type=F32) * (64 ** -0.5) + bias_p
    pos = lax.broadcasted_iota(jnp.int32, (2 * ATT_QB, ATT_KB), 1)
    sc = jnp.where(pos >= CHUNK * (8 - n0), sc, NEG)
    mx = jnp.max(sc, axis=-1, keepdims=True)
    ex = jnp.exp(sc - mx)
    return ex / jnp.sum(ex, axis=-1, keepdims=True)


def _head_stack(a2, lo):
    zero = jnp.zeros_like(a2)
    return jnp.concatenate([jnp.where(lo, a2, zero), jnp.where(lo, zero, a2)], axis=0)


def _att_fwd(p, rel, sel, name):
    T = p.shape[0]

    def body(q_ref, p_any, rel_ref, sel_ref, o_ref, kbuf, vbuf, bias_scr, sems):
        i = pl.program_id(0)

        @pl.when(i == 0)
        def _():
            _att_load_kv(p_any, kbuf, vbuf, sems, T)
            _att_build_bias(rel_ref, sel_ref, bias_scr)

        lo = lax.broadcasted_iota(jnp.int32, (ATT_QB, 128), 1) < 64
        n0 = i * ATT_G
        start = pl.multiple_of(i * ATT_QB, ATT_QB)
        for hp in range(AH // 2):
            cols = slice(hp * 128, (hp + 1) * 128)
            kb = kbuf[pl.ds(start, ATT_KB), cols]
            vb = vbuf[pl.ds(start, ATT_KB), cols]
            pr = _att_probs(_head_stack(q_ref[:, cols], lo), kb, bias_scr[hp], n0)
            pv = jnp.dot(pr.astype(BF16), vb, preferred_element_type=F32)
            o_ref[:, cols] = jnp.where(lo, pv[0:ATT_QB], pv[ATT_QB:2 * ATT_QB]).astype(o_ref.dtype)

    vm = 2 * _nbytes((T + LEFT, AW), BF16) + 8 * _nbytes((2 * ATT_QB, ATT_KB), F32) + (8 << 20)
    return _pallas(
        body,
        out_shape=_out((T, AW), BF16),
        grid=(T // ATT_QB,),
        in_specs=[pl.BlockSpec((ATT_QB, AW), lambda i: (i, 0)), pl.BlockSpec(memory_space=pl.ANY),
                  pl.BlockSpec((8, REL_PAD), lambda i: (0, 0)), pl.BlockSpec((REL_PAD, TOEP), lambda i: (0, 0))],
        out_specs=pl.BlockSpec((ATT_QB, AW), lambda i: (i, 0)),
        scratch_shapes=[pltpu.VMEM((T + LEFT, AW), BF16), pltpu.VMEM((T + LEFT, AW), BF16),
                        pltpu.VMEM((AH // 2, 2 * ATT_QB, ATT_KB), F32), pltpu.SemaphoreType.DMA((2,))],
        compiler_params=_cparams(("arbitrary",), vm),
        name=name,
    )(p, p, rel, sel)


def _att_bwd(p, do, rel, sel, name):
    T = p.shape[0]
    nsteps = T // ATT_QB

    def body(q_ref, p_any, do_ref, rel_ref, sel_ref, dp_any, drel_ref,
             kbuf, vbuf, dqbuf, dkbuf, dvbuf, bias_scr, dbias_scr, dtr_scr, sems):
        i = pl.program_id(0)

        @pl.when(i == 0)
        def _():
            _att_load_kv(p_any, kbuf, vbuf, sems, T)
            _att_build_bias(rel_ref, sel_ref, bias_scr)
            dkbuf[...] = jnp.zeros_like(dkbuf)
            dvbuf[...] = jnp.zeros_like(dvbuf)
            dbias_scr[...] = jnp.zeros_like(dbias_scr)

        lo = lax.broadcasted_iota(jnp.int32, (ATT_QB, 128), 1) < 64
        n0 = i * ATT_G
        start = pl.multiple_of(i * ATT_QB, ATT_QB)
        for hp in range(AH // 2):
            cols = slice(hp * 128, (hp + 1) * 128)
            kb = kbuf[pl.ds(start, ATT_KB), cols]
            vb = vbuf[pl.ds(start, ATT_KB), cols]
            qst = _head_stack(q_ref[:, cols], lo)
            dost = _head_stack(do_ref[:, cols].astype(BF16), lo)
            pr = _att_probs(qst, kb, bias_scr[hp], n0)
            dpr = lax.dot_general(dost, vb, _DIMS["nt"], preferred_element_type=F32)
            ds = pr * (dpr - jnp.sum(dpr * pr, axis=-1, keepdims=True))
            dbias_scr[hp] += ds
            dsb = (ds * (64 ** -0.5)).astype(BF16)
            dq = jnp.dot(dsb, kb, preferred_element_type=F32)
            dqbuf[pl.ds(start, ATT_QB), cols] = jnp.where(lo, dq[0:ATT_QB], dq[ATT_QB:2 * ATT_QB]).astype(BF16)
            dkbuf[pl.ds(start, ATT_KB), cols] += lax.dot_general(dsb, qst, _DIMS["tn"], preferred_element_type=F32)
            dvbuf[pl.ds(start, ATT_KB), cols] += lax.dot_general(pr.astype(BF16), dost, _DIMS["tn"], preferred_element_type=F32)

        @pl.when(i == nsteps - 1)
        def _():
            kbuf[pl.ds(LEFT, T), :] = dkbuf[pl.ds(LEFT, T), :].astype(BF16)
            vbuf[pl.ds(LEFT, T), :] = dvbuf[pl.ds(LEFT, T), :].astype(BF16)
            cps = [pltpu.make_async_copy(dqbuf, dp_any.at[:, pl.ds(0, AW)], sems.at[0]),
                   pltpu.make_async_copy(kbuf.at[pl.ds(LEFT, T), :], dp_any.at[:, pl.ds(AW, AW)], sems.at[1]),
                   pltpu.make_async_copy(vbuf.at[pl.ds(LEFT, T), :], dp_any.at[:, pl.ds(2 * AW, AW)], sems.at[2])]
            for cp in cps:
                cp.start()
            dtr_scr[...] = jnp.zeros_like(dtr_scr)
            ri = lax.broadcasted_iota(jnp.int32, (ATT_QB, ATT_QB), 0)
            ci = lax.broadcasted_iota(jnp.int32, (ATT_QB, ATT_QB), 1)
            flip = jnp.where(ri + ci == ATT_QB - 1, 1.0, 0.0)
            for h in range(AH):
                db = dbias_scr[h // 2, (h % 2) * ATT_QB:(h % 2 + 1) * ATT_QB, :]
                db = jnp.dot(flip, db, precision=HI, preferred_element_type=F32)
                wide = jnp.concatenate([db, jnp.zeros((ATT_QB, TOEP - ATT_KB), F32)], axis=1)
                diag = pltpu.roll(wide, 0, 1, stride=1, stride_axis=0)
                dtr_scr[h:h + 1, :] = jnp.sum(diag, axis=0, keepdims=True)
            drel_ref[...] = lax.dot_general(dtr_scr[...], sel_ref[...], _DIMS["nt"], precision=HI, preferred_element_type=F32)
            for cp in cps:
                cp.wait()

    vm = 3 * _nbytes((T + LEFT, AW), BF16) + 2 * _nbytes((T + LEFT, AW), F32) + 12 * _nbytes((2 * ATT_QB, ATT_KB), F32) + (8 << 20)
    return _pallas(
        body,
        out_shape=(_out((T, ATT_COLS), BF16), _out((8, REL_PAD), F32)),
        grid=(nsteps,),
        in_specs=[pl.BlockSpec((ATT_QB, AW), lambda i: (i, 0)), pl.BlockSpec(memory_space=pl.ANY),
                  pl.BlockSpec((ATT_QB, AW), lambda i: (i, 0)),
                  pl.BlockSpec((8, REL_PAD), lambda i: (0, 0)), pl.BlockSpec((REL_PAD, TOEP), lambda i: (0, 0))],
        out_specs=(pl.BlockSpec(memory_space=pl.ANY), pl.BlockSpec((8, REL_PAD), lambda i: (0, 0))),
        scratch_shapes=[pltpu.VMEM((T + LEFT, AW), BF16), pltpu.VMEM((T + LEFT, AW), BF16), pltpu.VMEM((T, AW), BF16),
                        pltpu.VMEM((T + LEFT, AW), F32), pltpu.VMEM((T + LEFT, AW), F32),
                        pltpu.VMEM((AH // 2, 2 * ATT_QB, ATT_KB), F32), pltpu.VMEM((AH // 2, 2 * ATT_QB, ATT_KB), F32),
                        pltpu.VMEM((8, TOEP), F32), pltpu.SemaphoreType.DMA((3,))],
        compiler_params=_cparams(("arbitrary",), vm),
        name=name,
    )(p, p, do, rel, sel)


def _layer_fwd(h, wl, consts, tag):
    p_gla, p_conv, p_att, xn = _mm_fan(h, [wl["w_gla"], wl["w_conv"], wl["w_att"]], mode="nt", out_dtypes=(F32, F32, BF16),
                                       norm_g=wl["norm_mix"], name=f"{tag}_proj")
    o_gla, states = _gla_fwd(p_gla, wl["wg"], wl["bg"], wl["gla_norm"], consts, f"{tag}_gla_fwd")
    o_conv, conv = _conv_fwd(p_conv, wl["w_dw"], wl["b_dw"], wl["ln_g"], wl["ln_b"], f"{tag}_conv_fwd")
    rel = jnp.pad(wl["rel_bias"], ((0, 8 - AH), (0, REL_PAD - N_REL)))
    o_att = _att_fwd(p_att, rel, consts[3], f"{tag}_att_fwd")
    h1 = _mm_sum([o_gla, o_conv, o_att], [wl["w_out_g"], wl["w_out_c"], wl["w_out_a"]], mode="nn", out_dtype=F32, extra=h,
                 name=f"{tag}_out")
    u, xn2 = _mm(h1, wl["w_up"], mode="nn", out_dtype=BF16, tm=1024, tn=1024, b_chips=True, norm_g=wl["norm_ffn"],
                 name=f"{tag}_mlp_up")
    h2 = _mm(u, wl["w_down"], mode="nn", out_dtype=F32, tm=1024, tk=2048, a_pro="relu2", epi="add", extra=h1,
             name=f"{tag}_mlp_down")
    saved = dict(h=h, xn=xn, p_gla=p_gla, p_conv=p_conv, p_att=p_att, states=states, o_gla=o_gla, o_conv=o_conv, conv=conv,
                 o_att=o_att, rel=rel, h1=h1, xn2=xn2, u=u)
    return h2, saved


def _layer_bwd(dh2, sv, wl, consts, tag, emit=lambda name, grad: None):
    g = {}
    du = _mm(dh2, wl["w_down"], mode="nt", out_dtype=BF16, tm=1024, tn=1024, epi="relu2grad", extra=sv["u"],
             name=f"{tag}_mlp_down_dx")
    g["w_down"] = _mm(sv["u"], dh2, mode="tn", out_dtype=F32, tm=2048, tn=1024, tk=512, a_pro="relu2", name=f"{tag}_mlp_down_dw")
    emit("w_down", g["w_down"].reshape(4, D_FF // 4, D))
    dh1, g["norm_ffn"] = _mm(du, wl["w_up"], mode="nt", out_dtype=F32, tm=1024, tk=1024, b_chips=True,
                             norm_bwd=(sv["h1"], wl["norm_ffn"], dh2), name=f"{tag}_mlp_up_dx")
    g["w_up"] = _mm(sv["xn2"], du, mode="tn", out_dtype=F32, tm=1024, tn=1024, tk=1024, out_chips=True, name=f"{tag}_mlp_up_dw")
    emit("w_up", g["w_up"])
    d_gla, d_conv, d_att = _mm_fan(dh1, [wl["w_out_g"], wl["w_out_c"], wl["w_out_a"]], mode="nt", out_dtypes=(F32, F32, F32),
                                   name=f"{tag}_out_dx")
    g["w_out_g"], g["w_out_c"], g["w_out_a"] = _mm_tn_multi([sv["o_gla"], sv["o_conv"], sv["o_att"], dh1],
                                                            [(0, 3), (1, 3), (2, 3)], name=f"{tag}_out_dw")
    emit("w_out", jnp.concatenate([g["w_out_g"], g["w_out_c"], g["w_out_a"]], axis=0).reshape(4, D // 4, D))
    dp_gla, g["wg"], g["bg"], g["gla_norm"] = _gla_bwd(sv["p_gla"], d_gla, sv["states"], wl["wg"], wl["bg"], wl["gla_norm"],
                                                       consts, f"{tag}_gla_bwd")
    dc, g["w_dw"], g["b_dw"], g["ln_g"], g["ln_b"] = _conv_bwd_dc(sv["p_conv"], sv["conv"], d_conv, wl["b_dw"], wl["ln_g"],
                                                                  wl["ln_b"], f"{tag}_conv_bwd_dc")
    dp_conv = _conv_bwd_du(sv["p_conv"], dc, wl["w_dw"], f"{tag}_conv_bwd_du")
    dp_att, drel = _att_bwd(sv["p_att"], d_att, sv["rel"], consts[3], f"{tag}_att_bwd")
    g["rel_bias"] = drel[0:AH, 0:N_REL]
    g["w_gla"], g["w_conv"], g["w_att"] = _mm_tn_multi([sv["xn"], dp_gla, dp_conv, dp_att], [(1, 0), (2, 0), (3, 0)],
                                                       name=f"{tag}_proj_dw")
    emit("w_in", _join_w_in_t(g))
    dh, g["norm_mix"] = _mm_sum([dp_gla, dp_conv, dp_att], [wl["w_gla"], wl["w_conv"], wl["w_att"]], mode="nn", out_dtype=F32,
                                norm_bwd=(sv["h"], wl["norm_mix"], dh1), name=f"{tag}_proj_dx")
    return dh, g


def _local_step(x, target, layers, norm_final, emit=lambda layer, name, grad: None):
    consts = _gla_consts() + (_att_consts(),)
    h = x
    saved = []
    for l, wl in enumerate(layers):
        h, sv = _layer_fwd(h, wl, consts, f"l{l}")
        saved.append(sv)
    loss, dh, g_final = _final_loss(h, norm_final, target, "final_loss")
    grads = [None] * len(layers)
    for l in reversed(range(len(layers))):
        dh, grads[l] = _layer_bwd(dh, saved[l], layers[l], consts, f"l{l}", functools.partial(emit, l))
    return loss, dh, grads, g_final


ANY = pl.BlockSpec(memory_space=pl.ANY)


def _place():
    x, y, c = lax.axis_index("x"), lax.axis_index("y"), lax.axis_index("c")
    chips = [(1 - x, y), (x, 1 - y), (1 - x, 1 - y)]
    return x, y, c, chips


def _shape(shape, dtype):
    return jax.ShapeDtypeStruct(tuple(shape), dtype)


def _remote(src, dst, send_sem, recv_sem, to):
    return pltpu.make_async_remote_copy(src_ref=src, dst_ref=dst, send_sem=send_sem, recv_sem=recv_sem,
                                        device_id=to, device_id_type=MESH)


class _Staged:
    def __init__(self, src, dst, buf, sems):
        self.load = pltpu.make_async_copy(src, buf, sems.at[0])
        self.store = pltpu.make_async_copy(buf, dst, sems.at[1])

    def start(self):
        self.load.start()

    def wait(self):
        self.load.wait()
        self.store.start()
        self.store.wait()


def _phase_gather_ici(src, then):
    R, C = src.shape
    rh = R // 2

    def copies(ins, outs, sems):
        x, y, c, chips = _place()
        me = 2 * x + y
        local = _Staged(ins[0], outs[0].at[me], sems[3], sems[2])
        sends = [_remote(ins[0].at[pl.ds(c * rh, rh), :], outs[0].at[me, pl.ds(c * rh, rh), :], sems[0].at[k], sems[1].at[k],
                         (px, py, c)) for k, (px, py) in enumerate(chips)]
        recvs = [_remote(outs[0].at[2 * px + py, pl.ds(c * rh, rh), :], outs[0].at[2 * px + py, pl.ds(c * rh, rh), :],
                         sems[0].at[k], sems[1].at[k], (px, py, c)) for k, (px, py) in enumerate(chips)]
        return local, sends, recvs

    def start(ins, outs, sems):
        local, sends, _ = copies(ins, outs, sems)
        local.start()
        for cp in sends:
            cp.start()

    def finish(ins, outs, sems):
        local, sends, recvs = copies(ins, outs, sems)
        for cp in recvs:
            cp.wait_recv()
        for cp in sends:
            cp.wait_send()
        local.wait()

    return _Comm([src], [_shape((4, R, C), src.dtype)], {}, [(3,), (3,), (2,), ((R, C), src.dtype)], start, finish, then)


def _phase_gather_d2d(part, then):
    _, R, C = part.shape
    rh = R // 2

    def copies(ins, outs, sems):
        x, y, c, chips = _place()
        sends = [_remote(ins[0].at[2 * px + py, pl.ds(c * rh, rh), :], outs[0].at[2 * px + py, pl.ds(c * rh, rh), :],
                         sems[0].at[k], sems[1].at[k], (x, y, 1 - c)) for k, (px, py) in enumerate(chips)]
        recvs = [_remote(outs[0].at[2 * px + py, pl.ds((1 - c) * rh, rh), :], outs[0].at[2 * px + py, pl.ds((1 - c) * rh, rh), :],
                         sems[0].at[k], sems[1].at[k], (x, y, 1 - c)) for k, (px, py) in enumerate(chips)]
        return sends, recvs

    def start(ins, outs, sems):
        for cp in copies(ins, outs, sems)[0]:
            cp.start()

    def finish(ins, outs, sems):
        sends, recvs = copies(ins, outs, sems)
        for cp in recvs:
            cp.wait_recv()
        for cp in sends:
            cp.wait_send()

    return _Comm([part], [_shape(part.shape, part.dtype)], {0: 0}, [(3,), (3,)], start, finish, then)


def _phase_pair_exchange(full, then, cols=False):
    _, R, C = full.shape
    rh, ch = (R, C // 2) if cols else (R // 2, C)

    def copy(ins, outs, sems):
        x, y, c, _ = _place()
        theirs = ins[0].at[:, :, pl.ds((1 - c) * ch, ch)] if cols else ins[0].at[:, pl.ds((1 - c) * rh, rh), :]
        return _remote(theirs, outs[0], sems[0].at[0], sems[1].at[0], (x, y, 1 - c))

    return _Comm([full], [_shape((4, rh, ch), full.dtype)], {}, [(1,), (1,)],
                 lambda ins, outs, sems: copy(ins, outs, sems).start(),
                 lambda ins, outs, sems: copy(ins, outs, sems).wait(), then)


def _phase_chip_scatter(parts, then):
    def copies(ins, outs, sems):
        x, y, c, chips = _place()
        me = 2 * x + y
        local = _Staged(ins[0].at[me], outs[0].at[me], sems[3], sems[2])
        sends = [_remote(ins[0].at[2 * px + py], outs[0].at[me], sems[0].at[k], sems[1].at[k], (px, py, c))
                 for k, (px, py) in enumerate(chips)]
        recvs = [_remote(outs[0].at[2 * px + py], outs[0].at[2 * px + py], sems[0].at[k], sems[1].at[k], (px, py, c))
                 for k, (px, py) in enumerate(chips)]
        return local, sends, recvs

    def start(ins, outs, sems):
        local, sends, _ = copies(ins, outs, sems)
        local.start()
        for cp in sends:
            cp.start()

    def finish(ins, outs, sems):
        local, sends, recvs = copies(ins, outs, sems)
        for cp in recvs:
            cp.wait_recv()
        for cp in sends:
            cp.wait_send()
        local.wait()

    return _Comm([parts], [_shape(parts.shape, parts.dtype)], {}, [(3,), (3,), (2,), (parts.shape[1:], parts.dtype)],
                 start, finish, then)


def _phase_pair_allgather(half, layer, depth, into, then, cols=False):
    rh, C = half.shape

    def copies(ins, outs, sems):
        x, y, c, _ = _place()
        if cols:
            mine = outs[0].at[layer, :, pl.ds(c * C, C)]
            theirs = outs[0].at[layer, :, pl.ds((1 - c) * C, C)]
        else:
            mine = outs[0].at[layer, pl.ds(c * rh, rh), :]
            theirs = outs[0].at[layer, pl.ds((1 - c) * rh, rh), :]
        return (_Staged(ins[0], mine, sems[3], sems[2]),
                _remote(ins[0], mine, sems[0].at[0], sems[1].at[0], (x, y, 1 - c)),
                _remote(theirs, theirs, sems[0].at[0], sems[1].at[0], (x, y, 1 - c)))

    def start(ins, outs, sems):
        local, send, _ = copies(ins, outs, sems)
        local.start()
        send.start()

    def finish(ins, outs, sems):
        local, send, recv = copies(ins, outs, sems)
        recv.wait_recv()
        send.wait_send()
        local.wait()

    ins = [half] if into is None else [half, into]
    whole = (depth, rh, 2 * C) if cols else (depth, 2 * rh, C)
    return _Comm(ins, [_shape(whole, half.dtype)], {} if into is None else {1: 0},
                 [(1,), (1,), (2,), ((rh, C), half.dtype)], start, finish, then)


def _comm_only(comms, name):
    plan = _Plan()
    for c in comms:
        plan.at(name, c)
    saved, _PLAN[0] = _PLAN[0], plan
    try:
        def body(o_ref):
            o_ref[...] = jnp.zeros_like(o_ref)

        _pallas(body, out_shape=[jax.ShapeDtypeStruct((8, 128), F32)], in_specs=[],
                out_specs=[pl.BlockSpec(memory_space=pltpu.VMEM)], name=name)()
    finally:
        _PLAN[0] = saved


def _row_tile(rows, cols, itemsize=4, budget=1 << 20, mult=8):
    fits = [t for t in range(mult, rows + 1, mult) if rows % t == 0 and t * cols * itemsize <= budget]
    return max(fits) if fits else rows


def _pair_add(full, got, c, name, cols=False):
    _, L, R, C = full.shape
    rh, ch = (R, C // 2) if cols else (R // 2, C)
    tr = _row_tile(rh, ch, budget=2 << 20, mult=16)
    nb = rh // tr

    def body(c_ref, a_ref, b_ref, o_ref):
        o_ref[...] = (a_ref[...] + b_ref[...]).astype(o_ref.dtype)

    mine = (lambda j, l, i, c_ref: (j, l, i, c_ref[0])) if cols else (lambda j, l, i, c_ref: (j, l, c_ref[0] * nb + i, 0))
    grid_spec = pltpu.PrefetchScalarGridSpec(
        num_scalar_prefetch=1,
        grid=(4, L, nb),
        in_specs=[pl.BlockSpec((1, 1, tr, ch), mine),
                  pl.BlockSpec((1, 1, tr, ch), lambda j, l, i, c_ref: (j, l, i, 0))],
        out_specs=pl.BlockSpec((1, 1, tr, ch), lambda j, l, i, c_ref: (j, l, i, 0)),
    )
    C = ch
    return _pallas(
        body,
        out_shape=_out((4, L, rh, C), BF16),
        grid_spec=grid_spec,
        compiler_params=_cparams(("parallel", "parallel", "parallel"), 8 * tr * C * 4),
        name=name,
    )(jnp.reshape(c, (1,)).astype(jnp.int32), full, got)


def _sum_chips(parts, name):
    _, L, rh, C = parts.shape
    tr = _row_tile(rh, C, budget=2 << 20, mult=16)

    def body(p_ref, o_ref):
        acc = p_ref[0].astype(F32)
        for j in range(1, 4):
            acc = acc + p_ref[j].astype(F32)
        o_ref[...] = acc

    return _pallas(
        body,
        out_shape=_out((L, rh, C), F32),
        grid=(L, rh // tr),
        in_specs=[pl.BlockSpec((4, 1, tr, C), lambda l, i: (0, l, i, 0))],
        out_specs=pl.BlockSpec((1, tr, C), lambda l, i: (l, i, 0)),
        compiler_params=_cparams(("parallel", "parallel"), 16 * tr * C * 4),
        name=name,
    )(parts)


def _allreduce_small(v):
    R = v.shape[0]

    def body(v_ref, o_ref, slots, send_sems, recv_sems):
        x, y, c, _ = _place()
        me = 4 * x + 2 * y + c
        slots[me] = v_ref[...]
        cps = []
        for r in range(1, 8):
            px, py, pc = x ^ (r >> 2), y ^ ((r >> 1) & 1), c ^ (r & 1)
            cps.append(pltpu.make_async_remote_copy(src_ref=v_ref, dst_ref=slots.at[me], send_sem=send_sems.at[r - 1],
                                                    recv_sem=recv_sems.at[r - 1], device_id=(px, py, pc), device_id_type=MESH))
            cps[-1].start()
        for r in range(1, 8):
            px, py, pc = x ^ (r >> 2), y ^ ((r >> 1) & 1), c ^ (r & 1)
            theirs = slots.at[4 * px + 2 * py + pc]
            pltpu.make_async_remote_copy(src_ref=theirs, dst_ref=theirs, send_sem=send_sems.at[r - 1], recv_sem=recv_sems.at[r - 1],
                                         device_id=(px, py, pc), device_id_type=MESH).wait_recv()
        acc = slots[0]
        for j in range(1, 8):
            acc = acc + slots[j]
        o_ref[...] = acc
        for cp in cps:
            cp.wait_send()

    return pl.pallas_call(
        body,
        out_shape=jax.ShapeDtypeStruct(v.shape, F32),
        in_specs=[pl.BlockSpec(memory_space=pltpu.VMEM)],
        out_specs=pl.BlockSpec(memory_space=pltpu.VMEM),
        scratch_shapes=[pltpu.VMEM((8, R, 128), F32), pltpu.SemaphoreType.DMA((7,)), pltpu.SemaphoreType.DMA((7,))],
        name="allreduce_small",
    )(v)


def _adamw_math(w, g, m, v):
    m = ADAM_B1 * m + (1.0 - ADAM_B1) * g
    v = ADAM_B2 * v + (1.0 - ADAM_B2) * (g * g)
    m_hat = m / (1.0 - ADAM_B1 ** ADAM_STEP)
    v_hat = v / (1.0 - ADAM_B2 ** ADAM_STEP)
    delta = -ADAM_LR * (m_hat / (jnp.sqrt(v_hat) + ADAM_EPS) + ADAM_WD * w)
    return delta, m, v


def _adamw(ws, gs, ms, vs, name, budget=1 << 19):
    n = len(ws)
    tiles = [_row_tile(w.shape[1], w.shape[2], budget=budget) for w in ws]
    per_layer = [w.shape[1] // t for w, t in zip(ws, tiles)]
    steps = [w.shape[0] * p for w, p in zip(ws, per_layer)]
    starts = [sum(steps[:k]) for k in range(n)]

    def body(*refs):
        i = pl.program_id(0)
        for k in range(n):
            w_ref, g_ref, m_ref, v_ref = (refs[j * n + k] for j in range(4))
            outs = [refs[(4 + j) * n + k] for j in range(3)]

            @pl.when((i >= starts[k]) & (i < starts[k] + steps[k]))
            def _(w_ref=w_ref, g_ref=g_ref, m_ref=m_ref, v_ref=v_ref, outs=outs):
                outs[0][...], outs[1][...], outs[2][...] = _adamw_math(w_ref[...], g_ref[...], m_ref[...], v_ref[...])

    def spec(k):
        def index(i):
            local = jnp.clip(i - starts[k], 0, steps[k] - 1)
            return local // per_layer[k], local % per_layer[k], 0
        return pl.BlockSpec((None, tiles[k], ws[k].shape[2]), index)

    specs = [spec(k) for k in range(n)]
    outs = [_out(w.shape, F32) for w in ws]
    res = _pallas(
        body,
        out_shape=tuple(outs * 3),
        grid=(sum(steps),),
        in_specs=specs * 4,
        out_specs=tuple(specs * 3),
        compiler_params=_cparams(("arbitrary",), sum(16 * t * w.shape[2] * 4 for w, t in zip(ws, tiles))),
        name=name,
    )(*ws, *gs, *ms, *vs)
    return res[:n], res[n:2 * n], res[2 * n:]


def _adamw_slabs(w, g, m, v, name, slabs=59):
    n, L, C = w.shape
    assert n % slabs == 0

    def body(w_ref, g_ref, m_ref, v_ref, d_ref, nm_ref, nv_ref):
        d_ref[...], nm_ref[...], nv_ref[...] = _adamw_math(w_ref[...], g_ref[...], m_ref[...], v_ref[...])

    blk = pl.BlockSpec((slabs, L, C), lambda i: (i, 0, 0))
    out = _out(w.shape, F32)
    return _pallas(
        body,
        out_shape=(out, out, out),
        grid=(n // slabs,),
        in_specs=[blk] * 4,
        out_specs=(blk, blk, blk),
        compiler_params=_cparams(("parallel",), 16 * slabs * 8 * C * 4),
        name=name,
    )(w, g, m, v)


def _adamw_small(ws, gs, ms, vs):
    n = len(ws)

    def body(*refs):
        for t in range(n):
            w_ref, g_ref, m_ref, v_ref = (refs[k * n + t] for k in range(4))
            d_ref, nm_ref, nv_ref = (refs[(4 + k) * n + t] for k in range(3))
            d_ref[...], nm_ref[...], nv_ref[...] = _adamw_math(w_ref[...], g_ref[...], m_ref[...], v_ref[...])

    vmem = pl.BlockSpec(memory_space=pltpu.VMEM)
    outs = [jax.ShapeDtypeStruct(w.shape, F32) for w in ws]
    res = pl.pallas_call(
        body,
        out_shape=outs * 3,
        in_specs=[vmem] * (4 * n),
        out_specs=[vmem] * (3 * n),
        name="adamw_small",
    )(*ws, *gs, *ms, *vs)
    return res[:n], res[n:2 * n], res[2 * n:]


IN_SIZES = (192, 192, 384, 384, 16, 512, 384, 384, 384)
IN_OFFS = tuple(int(v) for v in np.cumsum((0,) + IN_SIZES))
SMALL = ("norm_mix", "w_gla_gate", "b_gla_gate", "gla_norm", "b_dw", "conv_ln_g", "conv_ln_b", "rel_bias", "norm_ffn")


def _pad_cols(a, n):
    return jnp.pad(a, ((0, 0), (0, n - a.shape[1])))


W_IN_SHARD = 708
W_IN_ROWS = 736


def _pad_rows(a, n):
    return jnp.pad(a, ((0, n - a.shape[0]), (0, 0)))


def _split_w_in_t(w):
    s = [w[IN_OFFS[i]:IN_OFFS[i + 1]] for i in range(9)]
    w_gla = jnp.concatenate([_pad_rows(s[0], KW), _pad_rows(s[1], KW), s[2], s[3], _pad_rows(s[4], LRW)], axis=0)
    return w_gla, s[5], jnp.concatenate(s[6:9], axis=0)


def _join_w_in_t(g):
    gg = g["w_gla"]
    full = jnp.concatenate([gg[0:192], gg[KW:KW + 192], gg[2 * KW:2 * KW + VW], gg[2 * KW + VW:2 * KW + 2 * VW],
                            gg[2 * KW + 2 * VW:2 * KW + 2 * VW + 16], g["w_conv"], g["w_att"]], axis=0)
    return jnp.pad(full.reshape(4, W_IN_SHARD, D), ((0, 0), (0, W_IN_ROWS - W_IN_SHARD), (0, 0)))


def _pack(arrs, rows):
    flat = jnp.concatenate([a.reshape(-1) for a in arrs])
    return jnp.pad(flat, (0, rows * 128 - flat.shape[0])).reshape(rows, 128)


def _unpack(packed, shapes):
    flat = packed.reshape(-1)
    out, off = [], 0
    for s in shapes:
        n = int(np.prod(s))
        out.append(flat[off:off + n].reshape(s))
        off += n
    return out


def kernel(x, norm_mix, w_in, w_gla_gate, b_gla_gate, gla_norm, w_dw, b_dw, conv_ln_g, conv_ln_b, rel_bias, w_out, norm_ffn, w_up, w_down, norm_final, loss_target, m_norm_mix, m_w_in, m_w_gla_gate, m_b_gla_gate, m_gla_norm, m_w_dw, m_b_dw, m_conv_ln_g, m_conv_ln_b, m_rel_bias, m_w_out, m_norm_ffn, m_w_up, m_w_down, m_norm_final, v_norm_mix, v_w_in, v_w_gla_gate, v_b_gla_gate, v_gla_norm, v_w_dw, v_b_dw, v_conv_ln_g, v_conv_ln_b, v_rel_bias, v_w_out, v_norm_ffn, v_w_up, v_w_down, v_norm_final):
    P = dict(norm_mix=norm_mix, w_in=w_in, w_gla_gate=w_gla_gate, b_gla_gate=b_gla_gate, gla_norm=gla_norm, w_dw=w_dw, b_dw=b_dw,
             conv_ln_g=conv_ln_g, conv_ln_b=conv_ln_b, rel_bias=rel_bias, w_out=w_out, norm_ffn=norm_ffn, w_up=w_up,
             w_down=w_down, norm_final=norm_final)
    Mo = dict(norm_mix=m_norm_mix, w_in=m_w_in, w_gla_gate=m_w_gla_gate, b_gla_gate=m_b_gla_gate, gla_norm=m_gla_norm, w_dw=m_w_dw,
              b_dw=m_b_dw, conv_ln_g=m_conv_ln_g, conv_ln_b=m_conv_ln_b, rel_bias=m_rel_bias, w_out=m_w_out, norm_ffn=m_norm_ffn,
              w_up=m_w_up, w_down=m_w_down, norm_final=m_norm_final)
    Vo = dict(norm_mix=v_norm_mix, w_in=v_w_in, w_gla_gate=v_w_gla_gate, b_gla_gate=v_b_gla_gate, gla_norm=v_gla_norm, w_dw=v_w_dw,
              b_dw=v_b_dw, conv_ln_g=v_conv_ln_g, conv_ln_b=v_conv_ln_b, rel_bias=v_rel_bias, w_out=v_w_out, norm_ffn=v_norm_ffn,
              w_up=v_w_up, w_down=v_w_down, norm_final=v_norm_final)
    depth = w_in.shape[0]
    xi, yi, ci = lax.axis_index("x"), lax.axis_index("y"), lax.axis_index("c")
    chip = 2 * xi + yi

    plan = _Plan()
    _PLAN[0] = plan
    layers = [dict(
        norm_mix=norm_mix[l][None], wg=jnp.pad(w_gla_gate[l], ((0, LRW - 16), (0, KW - 192))),
        bg=_pad_cols(b_gla_gate[l][None], KW), gla_norm=gla_norm[l][None], b_dw=b_dw[l][None], ln_g=conv_ln_g[l][None],
        ln_b=conv_ln_b[l][None], rel_bias=rel_bias[l], norm_ffn=norm_ffn[l][None]) for l in range(depth)]

    w_in_t, m_w_in_t, v_w_in_t = (jnp.transpose(a, (2, 0, 1)) for a in (w_in, m_w_in, v_w_in))

    def w_in_shard(l):
        return _pad_rows(w_in_t[:, l, :], W_IN_ROWS).astype(BF16)

    def have_w_in(l, full):
        rows = jnp.concatenate([full[j, 0:W_IN_SHARD] for j in range(4)], axis=0)
        layers[l]["w_gla"], layers[l]["w_conv"], layers[l]["w_att"] = _split_w_in_t(rows)

    def have_w_out(l, full):
        w = full.reshape(D, D)
        layers[l]["w_out_g"], layers[l]["w_out_c"], layers[l]["w_out_a"] = w[0:VW], w[VW:VW + CW], w[VW + CW:]

    def have_w_up(l, full):
        layers[l]["w_up"] = full

    def have_w_down(l, full):
        layers[l]["w_down"] = full.reshape(D_FF, D)

    def have_w_dw(full):
        taps = full.reshape(4, depth, HALO, CW // 4)
        for l in range(depth):
            layers[l]["w_dw"] = jnp.transpose(taps[:, l], (1, 0, 2)).reshape(HALO, CW)

    first_d2d = []

    def first_ici(shard, have):
        return _phase_gather_ici(shard, lambda outs: first_d2d.append(_phase_gather_d2d(outs[0], lambda done: have(done[0]))))

    w_dw_pad = jnp.pad(w_dw, ((0, 0), (0, HALO - CK), (0, 0))).reshape(depth * HALO, CW // 4)
    _comm_only([first_ici(w_in_shard(0), functools.partial(have_w_in, 0)), first_ici(w_dw_pad, have_w_dw)],
               "gather_first_ici")
    _comm_only(first_d2d, "gather_first_d2d")

    def gather_behind(shard, ici_call, d2d_call, have):
        plan.at(ici_call, _phase_gather_ici(
            shard, lambda outs: plan.at(d2d_call, _phase_gather_d2d(outs[0], lambda done: have(done[0])))))

    for l in range(depth):
        if l > 0:
            gather_behind(w_in_shard(l), f"l{l - 1}_mlp_up", f"l{l - 1}_mlp_down", functools.partial(have_w_in, l))
        gather_behind(w_out[l].astype(BF16), f"l{l - 1}_mlp_down" if l > 0 else "l0_proj", f"l{l}_gla_fwd",
                      functools.partial(have_w_out, l))
        if l > 0:
            gather_behind(w_up[l].astype(BF16), f"l{l}_proj", f"l{l}_gla_fwd", functools.partial(have_w_up, l))
            gather_behind(w_down[l].astype(BF16), f"l{l}_gla_fwd", f"l{l}_att_fwd", functools.partial(have_w_down, l))
        else:
            gather_behind(w_up[l].astype(BF16), f"l{l}_gla_fwd", f"l{l}_att_fwd", functools.partial(have_w_up, l))
            gather_behind(w_down[l].astype(BF16), f"l{l}_att_fwd", f"l{l}_mlp_up", functools.partial(have_w_down, l))

    reduced = {}
    last_swap = []

    def reduce_calls(name, l):
        if name == "w_down":
            return f"l{l}_mlp_up_dx", f"l{l}_gla_bwd", f"l{l}_conv_bwd_dc"
        if name == "w_up":
            return f"l{l}_out_dx", f"l{l}_att_bwd", f"l{l}_proj_dw"
        if name == "w_out":
            return f"l{l}_gla_bwd", f"l{l}_conv_bwd_dc", f"l{l}_att_bwd"
        if l > 0:
            return f"l{l}_proj_dx", f"l{l - 1}_mlp_down_dw", f"l{l - 1}_mlp_up_dx"
        return None, "l0_proj_dx", None

    def reduce_behind(l, name, full):
        calls = reduce_calls(name, l)
        cols = False

        def swapped(outs):
            pair = _pair_add(full[:, None], outs[0][:, None], ci, f"reduce_pair_add_{name}{l}", cols)[:, 0]
            plan.at(calls[1], _phase_chip_scatter(pair, scattered))

        def scattered(outs):
            half = _sum_chips(outs[0][:, None], f"reduce_sum_chips_{name}{l}")[0]
            phase = _phase_pair_allgather(half, l, depth, reduced.get(name), gathered, cols)
            if calls[2] is None:
                last_swap.append(phase)
            else:
                plan.at(calls[2], phase)

        def gathered(outs):
            reduced[name] = outs[0]

        if calls[0] is None:
            _comm_only([_phase_pair_exchange(full, swapped, cols)], f"reduce_pair_exchange_{name}{l}")
        else:
            plan.at(calls[0], _phase_pair_exchange(full, swapped, cols))

    loss_part, grad_x, grads, g_final = _local_step(x[0], loss_target[0], layers, norm_final[None], reduce_behind)

    G, delta, new_m, new_v = {}, {}, {}, {}
    early = ("w_down", "w_up", "w_out")
    for name in early:
        G[name] = reduced[name]
    ds, nms, nvs = _adamw([P[k] for k in early], [G[k] for k in early], [Mo[k] for k in early], [Vo[k] for k in early],
                          "adamw_early")
    for i, name in enumerate(early):
        delta[name], new_m[name], new_v[name] = ds[i], nms[i], nvs[i]
    _PLAN[0] = None
    assert not plan.by_call, sorted(plan.by_call)

    small_g = []
    for l in range(depth):
        g = grads[l]
        small_g += [g["norm_mix"], g["wg"][0:16, 0:192], g["bg"][:, 0:192], g["gla_norm"], g["b_dw"], g["ln_g"], g["ln_b"],
                    g["rel_bias"], g["norm_ffn"], g["w_dw"][0:CK]]
    small_g += [g_final, loss_part]
    small_shapes = [a.shape for a in small_g]
    n_small = sum(int(np.prod(s)) for s in small_shapes)
    rows = -(-n_small // 1024) * 8
    red = _unpack(_allreduce_small(_pack(small_g, rows)), small_shapes)
    per = len(SMALL) + 1
    for i, name in enumerate(SMALL):
        G[name] = jnp.stack([red[l * per + i].reshape(P[name].shape[1:]) for l in range(depth)])
    gw_dw_all = jnp.stack([red[l * per + len(SMALL)] for l in range(depth)])
    G["w_dw"] = lax.dynamic_slice_in_dim(gw_dw_all, chip * (CW // 4), CW // 4, axis=2)
    G["norm_final"] = red[-2].reshape(norm_final.shape)
    loss = red[-1][0, 0]

    _comm_only(last_swap, "reduce_pair_allgather_last")
    back = lambda a: jnp.transpose(a, (1, 2, 0))
    g_in_t = jnp.transpose(reduced["w_in"][:, 0:W_IN_SHARD, :], (1, 0, 2))
    d_in, nm_in, nv_in = _adamw_slabs(w_in_t, g_in_t, m_w_in_t, v_w_in_t, "adamw_w_in")
    G["w_in"], delta["w_in"], new_m["w_in"], new_v["w_in"] = back(g_in_t), back(d_in), back(nm_in), back(nv_in)

    small_names = list(SMALL) + ["w_dw", "norm_final"]
    two_d = lambda a: a.reshape(-1, a.shape[-1])
    ds, nms, nvs = _adamw_small([two_d(P[k]) for k in small_names], [two_d(G[k]) for k in small_names],
                                [two_d(Mo[k]) for k in small_names], [two_d(Vo[k]) for k in small_names])
    for i, name in enumerate(small_names):
        shp = P[name].shape
        delta[name], new_m[name], new_v[name] = ds[i].reshape(shp), nms[i].reshape(shp), nvs[i].reshape(shp)

    order = ["norm_mix", "w_in", "w_gla_gate", "b_gla_gate", "gla_norm", "w_dw", "b_dw", "conv_ln_g", "conv_ln_b", "rel_bias",
             "w_out", "norm_ffn", "w_up", "w_down", "norm_final"]
    return (loss, grad_x[None], *[G[k] for k in order], *[delta[k] for k in order], *[new_m[k] for k in order],
            *[new_v[k] for k in order])
```

```python
import functools

import numpy as np
import jax
import jax.numpy as jnp
from jax import lax
from jax.experimental import pallas as pl
from jax.experimental.pallas import tpu as pltpu

F32 = jnp.float32
BF16 = jnp.bfloat16
HI = lax.Precision.HIGHEST

D = 1024
CHUNK = 64
GLA_DK, GLA_DV, GLA_H = 48, 96, 4
KW = 256
VW = 384
LRW = 128
GLA_TAU = 16.0
CW = 256
CK = 31
AW = 384
AH = 6
BAND = 576
LEFT = 512
D_FF = 4096
EPS = 1e-6
NEG = -1e30
N_REL = 257

GLA_COLS = 2 * KW + 2 * VW + LRW
CONV_COLS = 2 * CW
ATT_COLS = 3 * AW

ADAM_LR, ADAM_B1, ADAM_B2, ADAM_EPS, ADAM_WD, ADAM_STEP = 0.001, 0.9, 0.999, 1e-08, 0.01, 10

VMEM_CAP = 56 * 1024 * 1024
MESH = pl.DeviceIdType.MESH


def _cparams(sem, vmem_bytes):
    limit = int(min(VMEM_CAP, max(vmem_bytes * 5 // 4 + (4 << 20), 16 << 20)))
    return pltpu.CompilerParams(dimension_semantics=sem, vmem_limit_bytes=limit)


def _out(shape, dtype):
    return pltpu.HBM(tuple(shape), dtype)


class _Comm:
    def __init__(self, ins, outs, aliases, sems, start, finish, then=None):
        self.ins, self.outs, self.aliases, self.sems = list(ins), list(outs), dict(aliases), list(sems)
        self.start, self.finish, self.then = start, finish, then


class _Plan:
    def __init__(self):
        self.by_call = {}

    def at(self, call, comm):
        self.by_call.setdefault(call, []).append(comm)

    def take(self, call):
        return self.by_call.pop(call, [])


_PLAN = [None]


def _pin(a):
    return pltpu.with_memory_space_constraint(a, pltpu.HBM) if jnp.issubdtype(a.dtype, jnp.floating) else a


def _pallas(body, **kw):
    comms = _PLAN[0].take(kw.get("name")) if _PLAN[0] is not None else []
    if not comms:
        call = pl.pallas_call(body, **kw)
        return lambda *args: call(*[_pin(a) for a in args])

    grid = tuple(kw.get("grid", ()))
    single = not isinstance(kw["out_shape"], (tuple, list))
    out_shape = [kw["out_shape"]] if single else list(kw["out_shape"])
    out_specs = [kw["out_specs"]] if single else list(kw["out_specs"])
    in_specs = list(kw["in_specs"])
    scratch = list(kw.get("scratch_shapes", ()))
    n_in, n_out, n_scr = len(in_specs), len(out_shape), len(scratch)
    c_in = sum(len(c.ins) for c in comms)
    c_out = sum(len(c.outs) for c in comms)
    aliases = dict(kw.get("input_output_aliases", {}))
    i0, o0 = n_in, n_out
    for c in comms:
        for i, o in c.aliases.items():
            aliases[i0 + i] = o0 + o
        i0 += len(c.ins)
        o0 += len(c.outs)

    def wrapped(*refs):
        ins, c_ins = refs[:n_in], refs[n_in:n_in + c_in]
        outs, c_outs = refs[n_in + c_in:n_in + c_in + n_out], refs[n_in + c_in + n_out:n_in + c_in + n_out + c_out]
        scr, c_sems = refs[n_in + c_in + n_out + c_out:][:n_scr], refs[n_in + c_in + n_out + c_out + n_scr:]

        def each(what):
            i0 = o0 = s0 = 0
            for c in comms:
                getattr(c, what)(c_ins[i0:i0 + len(c.ins)], c_outs[o0:o0 + len(c.outs)], c_sems[s0:s0 + len(c.sems)])
                i0, o0, s0 = i0 + len(c.ins), o0 + len(c.outs), s0 + len(c.sems)

        if grid:
            first = functools.reduce(jnp.logical_and, [pl.program_id(a) == 0 for a in range(len(grid))])
            last = functools.reduce(jnp.logical_and, [pl.program_id(a) == grid[a] - 1 for a in range(len(grid))])
            pl.when(first)(lambda: each("start"))
            body(*ins, *outs, *scr)
            pl.when(last)(lambda: each("finish"))
        else:
            each("start")
            body(*ins, *outs, *scr)
            each("finish")

    kw = dict(kw)
    kw["in_specs"] = in_specs + [ANY] * c_in
    kw["out_shape"] = out_shape + [_out(s.shape, s.dtype) for c in comms for s in c.outs]
    kw["out_specs"] = out_specs + [ANY] * c_out
    staging = [s for c in comms for s in c.sems if len(s) == 2 and not isinstance(s[1], int)]
    kw["scratch_shapes"] = scratch + [pltpu.VMEM(*s) if s in staging else pltpu.SemaphoreType.DMA(s) for c in comms for s in c.sems]
    kw["input_output_aliases"] = aliases
    extra = sum(_nbytes(*s) for s in staging)
    old = kw.get("compiler_params")
    limit = (old.vmem_limit_bytes if old is not None else 16 << 20) + extra
    kw["compiler_params"] = pltpu.CompilerParams(
        dimension_semantics=old.dimension_semantics if old is not None else None, vmem_limit_bytes=int(min(VMEM_CAP, limit)))
    call = pl.pallas_call(wrapped, **kw)

    def run(*args):
        res = call(*[_pin(a) for a in args], *[_pin(a) for c in comms for a in c.ins])
        o0 = n_out
        for c in comms:
            if c.then is not None:
                c.then(res[o0:o0 + len(c.outs)])
            o0 += len(c.outs)
        return res[0] if single else res[:n_out]

    return run


def _nbytes(shape, dtype):
    return int(np.prod(shape)) * jnp.dtype(dtype).itemsize


def _sigmoid(x):
    return 1.0 / (1.0 + jnp.exp(-x))


_DIMS = {"nn": (((1,), (0,)), ((), ())), "nt": (((1,), (1,)), ((), ())), "tn": (((0,), (0,)), ((), ()))}


def _mm(a, b, *, mode, out_dtype, name, tm=512, tn=None, tk=None, a_pro=None, epi=None, extra=None,
        b_chips=False, out_chips=False, norm_g=None, norm_bwd=None):
    b2 = (b.shape[1], 4 * b.shape[2]) if b_chips else b.shape
    if mode == "nn":
        (M, K), (K2, N) = a.shape, b2
    elif mode == "nt":
        (M, K), (N, K2) = a.shape, b2
    else:
        (K, M), (K2, N) = a.shape, b2
    assert K == K2, (a.shape, b.shape, mode)
    tm = min(tm, M)
    tn = N if tn is None else min(tn, N)
    tk = K if tk is None else min(tk, K)
    assert M % tm == 0 and N % tn == 0 and K % tk == 0, (M, N, K, tm, tn, tk)
    nk = K // tk
    a_blk = (tk, tm) if mode == "tn" else (tm, tk)
    a_map = (lambda i, j, k: (k, i)) if mode == "tn" else (lambda i, j, k: (i, k))
    b_blk = (tn, tk) if mode == "nt" else (tk, tn)
    b_map = (lambda i, j, k: (j, k)) if mode == "nt" else (lambda i, j, k: (k, j))
    if b_chips:
        per = b.shape[2] // b_blk[1]
        assert b.shape[2] % b_blk[1] == 0 and mode != "tn"
        flat_map = b_map
        b_map = lambda i, j, k: (flat_map(i, j, k)[1] // per, flat_map(i, j, k)[0], flat_map(i, j, k)[1] % per)
        b_blk = (None,) + b_blk
    in_specs = [pl.BlockSpec(a_blk, a_map), pl.BlockSpec(b_blk, b_map)]
    args = [a, b]
    if epi is not None:
        in_specs.append(pl.BlockSpec((tm, tn), lambda i, j, k: (i, j)))
        args.append(extra)
    if norm_g is not None:
        assert nk == 1 and mode != "tn"
        in_specs.append(pl.BlockSpec((1, K), lambda i, j, k: (0, 0)))
        args.append(norm_g)
    if norm_bwd is not None:
        assert tn == N
        row = pl.BlockSpec((tm, N), lambda i, j, k: (i, 0))
        in_specs += [row, pl.BlockSpec((1, N), lambda i, j, k: (0, 0)), row]
        args += list(norm_bwd)

    def body(*refs):
        it = iter(refs)
        a_ref, b_ref = next(it), next(it)
        e_ref = next(it) if epi is not None else None
        ng_ref = next(it) if norm_g is not None else None
        h_ref, g_ref, dres_ref = (next(it), next(it), next(it)) if norm_bwd is not None else (None, None, None)
        o_ref = next(it)
        xn_ref = next(it) if norm_g is not None else None
        dg_ref = next(it) if norm_bwd is not None else None
        if norm_bwd is not None:
            @pl.when((pl.program_id(0) == 0) & (pl.program_id(2) == 0))
            def _():
                dg_ref[...] = jnp.zeros_like(dg_ref)

        av = a_ref[...]
        if a_pro == "relu2":
            af = jnp.maximum(av.astype(F32), 0.0)
            av = af * af
        if norm_g is not None:
            av = (av * lax.rsqrt(jnp.mean(av * av, axis=-1, keepdims=True) + EPS) * ng_ref[...]).astype(BF16)
            xn_ref[...] = av
        p = lax.dot_general(av.astype(BF16), b_ref[...].astype(BF16), _DIMS[mode], preferred_element_type=F32)

        def finish(acc):
            if epi == "add":
                acc = acc + e_ref[...].astype(F32)
            elif epi == "relu2grad":
                acc = acc * (2.0 * jnp.maximum(e_ref[...].astype(F32), 0.0))
            if norm_bwd is None:
                o_ref[...] = acc.astype(o_ref.dtype)
                return

            x = h_ref[...]
            r = lax.rsqrt(jnp.mean(x * x, axis=-1, keepdims=True) + EPS)
            gy = acc * g_ref[...]
            dot = jnp.mean(x * gy, axis=-1, keepdims=True)
            o_ref[...] = dres_ref[...] + r * gy - x * (r * r * r * dot)
            dg_ref[...] += jnp.sum(acc * x * r, axis=0, keepdims=True)

        if nk == 1:
            finish(p)
        else:
            acc_ref = refs[-1]
            k = pl.program_id(2)

            @pl.when(k == 0)
            def _():
                acc_ref[...] = p

            @pl.when(k > 0)
            def _():
                acc_ref[...] += p

            @pl.when(k == nk - 1)
            def _():
                finish(acc_ref[...])

    vm = 2 * (_nbytes(a_blk, a.dtype) + _nbytes((tk, tn), b.dtype) + _nbytes((tm, tn), out_dtype))
    vm += 3 * _nbytes((tm, tn), F32)
    if epi is not None:
        vm += 2 * _nbytes((tm, tn), extra.dtype)
    if out_chips:
        per_out = N // 4 // tn
        assert N % (4 * tn) == 0
        out_shape = _out((4, M, N // 4), out_dtype)
        out_spec = pl.BlockSpec((None, tm, tn), lambda i, j, k: (j // per_out, i, j % per_out))
    else:
        out_shape = _out((M, N), out_dtype)
        out_spec = pl.BlockSpec((tm, tn), lambda i, j, k: (i, j))
    sem = ("parallel", "parallel", "arbitrary")
    if norm_g is not None:
        out_shape, out_spec = (out_shape, _out((M, K), BF16)), (out_spec, pl.BlockSpec((tm, K), lambda i, j, k: (i, 0)))
        vm += 4 * _nbytes((tm, K), F32)
    if norm_bwd is not None:
        out_shape, out_spec = (out_shape, _out((1, N), F32)), (out_spec, pl.BlockSpec((1, N), lambda i, j, k: (0, 0)))
        sem = ("arbitrary", "arbitrary", "arbitrary")
        vm += 8 * _nbytes((tm, N), F32)
    return _pallas(
        body,
        out_shape=out_shape,
        grid=(M // tm, N // tn, nk),
        in_specs=in_specs,
        out_specs=out_spec,
        scratch_shapes=[pltpu.VMEM((tm, tn), F32)] if nk > 1 else [],
        compiler_params=_cparams(sem, vm),
        name=name,
    )(*args)


def _mm_fan(a, bs, *, mode, out_dtypes, name, tm=512, norm_g=None):
    M, K = a.shape
    ns = [b.shape[1] if mode == "nn" else b.shape[0] for b in bs]
    n = len(bs)
    first_out = 1 + n + (norm_g is not None)

    def body(*refs):
        if norm_g is None:
            av = refs[0][...].astype(BF16)
        else:
            x = refs[0][...]
            av = (x * lax.rsqrt(jnp.mean(x * x, axis=-1, keepdims=True) + EPS) * refs[1 + n][...]).astype(BF16)
            refs[first_out + n][...] = av
        for i in range(n):
            refs[first_out + i][...] = lax.dot_general(av, refs[1 + i][...], _DIMS[mode],
                                                       preferred_element_type=F32).astype(refs[first_out + i].dtype)

    vm = 4 * _nbytes((tm, K), F32) + sum(2 * _nbytes(b.shape, b.dtype) + 3 * _nbytes((tm, nn), F32) for b, nn in zip(bs, ns))
    in_specs = [pl.BlockSpec((tm, K), lambda i: (i, 0))] + [pl.BlockSpec(b.shape, lambda i: (0, 0)) for b in bs]
    out_shape = [_out((M, nn), dt) for nn, dt in zip(ns, out_dtypes)]
    out_specs = [pl.BlockSpec((tm, nn), lambda i: (i, 0)) for nn in ns]
    args = [a, *bs]
    if norm_g is not None:
        in_specs.append(pl.BlockSpec((1, K), lambda i: (0, 0)))
        out_shape.append(_out((M, K), BF16))
        out_specs.append(pl.BlockSpec((tm, K), lambda i: (i, 0)))
        args.append(norm_g)
    return _pallas(
        body,
        out_shape=tuple(out_shape),
        grid=(M // tm,),
        in_specs=in_specs,
        out_specs=tuple(out_specs),
        compiler_params=_cparams(("parallel",), vm),
        name=name,
    )(*args)


def _mm_sum(as_, bs, *, mode, out_dtype, name, extra=None, norm_bwd=None, tm=512):
    M = as_[0].shape[0]
    N = bs[0].shape[1] if mode == "nn" else bs[0].shape[0]
    n = len(as_)

    def body(*refs):
        acc = None
        for i in range(n):
            p = lax.dot_general(refs[i][...].astype(BF16), refs[n + i][...], _DIMS[mode], preferred_element_type=F32)
            acc = p if acc is None else acc + p
        if extra is not None:
            acc = acc + refs[2 * n][...].astype(F32)
        if norm_bwd is None:
            refs[-1][...] = acc.astype(refs[-1].dtype)
        else:
            h_ref, g_ref, dres_ref, dh_ref, dg_ref = refs[-5:]

            @pl.when(pl.program_id(0) == 0)
            def _():
                dg_ref[...] = jnp.zeros_like(dg_ref)

            x = h_ref[...]
            r = lax.rsqrt(jnp.mean(x * x, axis=-1, keepdims=True) + EPS)
            gy = acc * g_ref[...]
            dot = jnp.mean(x * gy, axis=-1, keepdims=True)
            dh_ref[...] = dres_ref[...] + r * gy - x * (r * r * r * dot)
            dg_ref[...] += jnp.sum(acc * x * r, axis=0, keepdims=True)

    row = pl.BlockSpec((tm, N), lambda i: (i, 0))
    in_specs = [pl.BlockSpec((tm, a.shape[1]), lambda i: (i, 0)) for a in as_]
    in_specs += [pl.BlockSpec(b.shape, lambda i: (0, 0)) for b in bs]
    args = list(as_) + list(bs)
    if extra is not None:
        in_specs.append(row)
        args.append(extra)
    vm = sum(2 * _nbytes((tm, a.shape[1]), a.dtype) for a in as_) + sum(2 * _nbytes(b.shape, b.dtype) for b in bs)
    vm += 8 * _nbytes((tm, N), F32)
    if norm_bwd is None:
        out_shape, out_specs, sem = _out((M, N), out_dtype), row, "parallel"
    else:
        vec = pl.BlockSpec((1, N), lambda i: (0, 0))
        in_specs += [row, vec, row]
        args += list(norm_bwd)
        out_shape, out_specs, sem = (_out((M, N), F32), _out((1, N), F32)), (row, vec), "arbitrary"
        vm += 8 * _nbytes((tm, N), F32)
    return _pallas(
        body,
        out_shape=out_shape,
        grid=(M // tm,),
        in_specs=in_specs,
        out_specs=out_specs,
        compiler_params=_cparams((sem,), vm),
        name=name,
    )(*args)


def _mm_tn_multi(ops, pairs, *, name, tk=512):
    T = ops[0].shape[0]
    n, m = len(ops), len(pairs)
    shapes = [(ops[a].shape[1], ops[b].shape[1]) for a, b in pairs]

    def body(*refs):
        vals = [refs[i][...].astype(BF16) for i in range(n)]
        first = pl.program_id(0) == 0
        for j, (a, b) in enumerate(pairs):
            p = lax.dot_general(vals[a], vals[b], _DIMS["tn"], preferred_element_type=F32)
            o_ref = refs[n + j]

            @pl.when(first)
            def _(o_ref=o_ref, p=p):
                o_ref[...] = p

            @pl.when(jnp.logical_not(first))
            def _(o_ref=o_ref, p=p):
                o_ref[...] += p

    vm = sum(2 * _nbytes((tk, o.shape[1]), o.dtype) for o in ops) + sum(3 * _nbytes(s, F32) for s in shapes)
    return _pallas(
        body,
        out_shape=tuple(_out(s, F32) for s in shapes),
        grid=(T // tk,),
        in_specs=[pl.BlockSpec((tk, o.shape[1]), lambda k: (k, 0)) for o in ops],
        out_specs=tuple(pl.BlockSpec(s, lambda k: (0, 0)) for s in shapes),
        compiler_params=_cparams(("arbitrary",), vm),
        name=name,
    )(*ops)


def _final_loss(h, g, target, name, tm=512):
    T = h.shape[0]

    def body(h_ref, g_ref, t_ref, loss_ref, dh_ref, dg_ref):
        @pl.when(pl.program_id(0) == 0)
        def _():
            dg_ref[...] = jnp.zeros_like(dg_ref)
            loss_ref[...] = jnp.zeros_like(loss_ref)

        x = h_ref[...]
        gg = g_ref[...]
        r = lax.rsqrt(jnp.mean(x * x, axis=-1, keepdims=True) + EPS)
        y = x * r * gg
        e = y - t_ref[...]
        loss_ref[...] += 0.5 * jnp.sum(jnp.mean(e * e, axis=-1, keepdims=True), axis=0, keepdims=True)
        dy = e * (1.0 / D)
        gy = dy * gg
        dot = jnp.mean(x * gy, axis=-1, keepdims=True)
        dh_ref[...] = r * gy - x * (r * r * r * dot)
        dg_ref[...] += jnp.sum(dy * x * r, axis=0, keepdims=True)

    row = pl.BlockSpec((tm, D), lambda i: (i, 0))
    vec = pl.BlockSpec((1, D), lambda i: (0, 0))
    one = pl.BlockSpec((1, 1), lambda i: (0, 0))
    return _pallas(
        body,
        out_shape=(_out((1, 1), F32), _out((T, D), F32), _out((1, D), F32)),
        grid=(T // tm,),
        in_specs=[row, vec, row],
        out_specs=(one, row, vec),
        compiler_params=_cparams(("arbitrary",), 12 * _nbytes((tm, D), F32)),
        name=name,
    )(h, g, target)


GLA_G = 8


def _gla_consts():
    i = np.arange(KW)[:, None]
    j = np.arange(VW)[None, :]
    mask = ((i // GLA_DK) == (j // GLA_DV)) & (i < GLA_H * GLA_DK)
    a = np.arange(VW)
    hm = ((a[:, None] // GLA_DV) == (a[None, :] // GLA_DV)).astype(np.float32)
    c = np.arange(CHUNK)
    low = (c[:, None] >= c[None, :]).astype(np.float32)
    return jnp.asarray(mask.astype(np.float32)), jnp.asarray(hm, BF16), jnp.asarray(low, BF16)


def _split(x):
    hi = x.astype(BF16)
    return hi, (x - hi.astype(F32)).astype(BF16)


def _dot_sel(a, b, dims, split):
    if split == "a":
        hi, lo = _split(a)
        return (lax.dot_general(hi, b, dims, preferred_element_type=F32) + lax.dot_general(lo, b, dims, preferred_element_type=F32))
    hi, lo = _split(b)
    return (lax.dot_general(a, hi, dims, preferred_element_type=F32) + lax.dot_general(a, lo, dims, preferred_element_type=F32))


def _dot3(a, b, dims):
    ah, al = _split(a)
    bh, bl = _split(b)
    return (lax.dot_general(ah, bh, dims, preferred_element_type=F32) + lax.dot_general(al, bh, dims, preferred_element_type=F32)
            + lax.dot_general(ah, bl, dims, preferred_element_type=F32))


def _dot3s(a_s, b_s, dims):
    (ah, al), (bh, bl) = a_s, b_s
    return (lax.dot_general(ah, bh, dims, preferred_element_type=F32) + lax.dot_general(al, bh, dims, preferred_element_type=F32)
            + lax.dot_general(ah, bl, dims, preferred_element_type=F32))


def _gla_group_common(p_ref, wg, bg):
    lr_s = _split(p_ref[:, 2 * KW + 2 * VW:GLA_COLS])
    wg_s = _split(wg)
    z = _dot3s(lr_s, wg_s, _DIMS["nn"]) + bg
    la = (jnp.minimum(z, 0.0) - jnp.log(1.0 + jnp.exp(-jnp.abs(z)))) * (1.0 / GLA_TAU)
    return lr_s, wg_s, z, _split(la)


def _gla_chunk_common(p_ref, rows, la_s, low, ones_v):
    q = p_ref[rows, 0:KW]
    k = p_ref[rows, KW:2 * KW]
    v = p_ref[rows, 2 * KW:2 * KW + VW]
    g = p_ref[rows, 2 * KW + VW:2 * KW + 2 * VW]
    la_h, la_l = la_s[0][rows], la_s[1][rows]
    cum = jnp.dot(low, la_h, preferred_element_type=F32) + jnp.dot(low, la_l, preferred_element_type=F32)
    endb = cum[CHUNK - 1:CHUNK, :]
    w = jnp.exp(endb - cum)
    a_full = jnp.exp(lax.dot_general(la_h, ones_v, _DIMS["tn"], preferred_element_type=F32)
                     + lax.dot_general(la_l, ones_v, _DIMS["tn"], preferred_element_type=F32))
    return q, k, v, g, w, endb, a_full


def _gla_fwd(p, wg, bg, gn, consts, name):
    T = p.shape[0]
    rb = CHUNK * GLA_G
    ng = T // rb
    mask, hm, low = consts[:3]
    scale = GLA_DK ** -0.5

    def body(p_ref, wg_ref, bg_ref, gn_ref, m_ref, hm_ref, l_ref, o_ref, st_ref, s_ref):
        @pl.when(pl.program_id(0) == 0)
        def _():
            s_ref[...] = jnp.zeros_like(s_ref)

        wg_v, bg_v, gn_v = wg_ref[...], bg_ref[...], gn_ref[...]
        ones_v = jnp.ones((CHUNK, VW), BF16)
        s_new = s_ref[...]
        _, _, _, la_s = _gla_group_common(p_ref, wg_v, bg_v)
        outs = []
        for c in range(GLA_G):
            rows = slice(c * CHUNK, (c + 1) * CHUNK)
            q, k, v, _, w, _, a_full = _gla_chunk_common(p_ref, rows, la_s, l_ref[...], ones_v)
            kd = (k * w).astype(BF16)
            kv = lax.dot_general(kd, v.astype(BF16), _DIMS["tn"], preferred_element_type=F32) * m_ref[...]
            s_new = a_full * s_new + kv
            st_ref[c] = s_new
            outs.append(jnp.dot((q * scale).astype(BF16), s_new.astype(BF16), preferred_element_type=F32))
        s_ref[...] = s_new
        o = jnp.concatenate(outs, axis=0)
        g = p_ref[:, 2 * KW + VW:2 * KW + 2 * VW]
        ms = _dot_sel(o * o, hm_ref[...], _DIMS["nn"], "a") * (1.0 / GLA_DV)
        o_ref[...] = (o * lax.rsqrt(ms + EPS) * gn_v * (g * _sigmoid(g))).astype(o_ref.dtype)

    full = lambda shape: pl.BlockSpec(shape, lambda i: tuple(0 for _ in shape))
    vm = 2 * _nbytes((rb, GLA_COLS), F32) + 2 * _nbytes((GLA_G, KW, VW), F32) + 12 * _nbytes((KW, VW), F32)
    return _pallas(
        body,
        out_shape=(_out((T, VW), BF16), _out((T // CHUNK, KW, VW), F32)),
        grid=(ng,),
        in_specs=[pl.BlockSpec((rb, GLA_COLS), lambda i: (i, 0)), full((LRW, KW)), full((1, KW)), full((1, VW)),
                  full((KW, VW)), full((VW, VW)), full((CHUNK, CHUNK))],
        out_specs=(pl.BlockSpec((rb, VW), lambda i: (i, 0)), pl.BlockSpec((GLA_G, KW, VW), lambda i: (i, 0, 0))),
        scratch_shapes=[pltpu.VMEM((KW, VW), F32)],
        compiler_params=_cparams(("arbitrary",), vm),
        name=name,
    )(p, wg, bg, gn, mask, hm, low)


def _gla_bwd(p, dy, states, wg, bg, gn, consts, name):
    T = p.shape[0]
    rb = CHUNK * GLA_G
    ng = T // rb
    mask, hm, low = consts[:3]
    scale = GLA_DK ** -0.5

    def body(p_ref, dy_ref, st_ref, sp_ref, wg_ref, bg_ref, gn_ref, m_ref, hm_ref, l_ref,
             dp_ref, dwg_ref, dbg_ref, dgn_ref, ga_ref):
        step = pl.program_id(0)

        @pl.when(step == 0)
        def _():
            ga_ref[...] = jnp.zeros_like(ga_ref)
            dwg_ref[...] = jnp.zeros_like(dwg_ref)
            dbg_ref[...] = jnp.zeros_like(dbg_ref)
            dgn_ref[...] = jnp.zeros_like(dgn_ref)

        first_group = step == ng - 1
        wg_v, bg_v, gn_v = wg_ref[...], bg_ref[...], gn_ref[...]
        ones_v = jnp.ones((CHUNK, VW), BF16)
        ones_8 = jnp.ones((8, VW), BF16)
        ga = ga_ref[...]
        lr_s, wg_s, z_all, la_s = _gla_group_common(p_ref, wg_v, bg_v)
        qss = [(p_ref[c * CHUNK:(c + 1) * CHUNK, 0:KW] * scale).astype(BF16) for c in range(GLA_G)]
        o = jnp.concatenate([jnp.dot(qss[c], st_ref[c].astype(BF16), preferred_element_type=F32) for c in range(GLA_G)], axis=0)
        g = p_ref[:, 2 * KW + VW:2 * KW + 2 * VW]
        dyv = dy_ref[...].astype(F32)
        r = lax.rsqrt(_dot_sel(o * o, hm_ref[...], _DIMS["nn"], "a") * (1.0 / GLA_DV) + EPS)
        on = o * r
        sg = _sigmoid(g)
        silu = g * sg
        d_on = dyv * gn_v * silu
        dp_ref[:, 2 * KW + VW:2 * KW + 2 * VW] = (dyv * on * gn_v * (sg * (1.0 + g * (1.0 - sg)))).astype(dp_ref.dtype)
        dgn_ref[...] += jnp.sum(dyv * on * silu, axis=0, keepdims=True)
        mo = _dot_sel(o * d_on, hm_ref[...], _DIMS["nn"], "a") * (1.0 / GLA_DV)
        dob_all = (r * d_on - o * (r * r * r) * mo).astype(BF16)
        dzs = [None] * GLA_G
        for c in reversed(range(GLA_G)):
            rows = slice(c * CHUNK, (c + 1) * CHUNK)
            _, k, v, _, w, endb, a_full = _gla_chunk_common(p_ref, rows, la_s, l_ref[...], ones_v)
            s_n = st_ref[c]
            if c > 0:
                s_prev = st_ref[c - 1]
            else:
                s_prev = jnp.where(first_group, 0.0, sp_ref[0])
            kd = k * w
            dob = dob_all[rows]
            dq = lax.dot_general(dob, s_n.astype(BF16), _DIMS["nt"], preferred_element_type=F32) * scale
            g_n = lax.dot_general(qss[c], dob, _DIMS["tn"], preferred_element_type=F32) * m_ref[...] + ga
            d_a = _dot_sel(ones_8, g_n * s_prev, _DIMS["nt"], "b")[0:1, :]
            g_nb = g_n.astype(BF16)
            dkd = lax.dot_general(v.astype(BF16), g_nb, _DIMS["nt"], preferred_element_type=F32)
            dv = jnp.dot(kd.astype(BF16), g_nb, preferred_element_type=F32)
            e = dkd * kd
            d_end = jnp.sum(e, axis=0, keepdims=True) + d_a * jnp.exp(endb)
            dla = _dot_sel(l_ref[...], -e, _DIMS["tn"], "b") + d_end
            dzs[c] = dla * (1.0 - _sigmoid(z_all[rows])) * (1.0 / GLA_TAU)
            ga = a_full * g_n
            dp_ref[rows, 0:KW] = dq.astype(dp_ref.dtype)
            dp_ref[rows, KW:2 * KW] = (dkd * w).astype(dp_ref.dtype)
            dp_ref[rows, 2 * KW:2 * KW + VW] = dv.astype(dp_ref.dtype)
        ga_ref[...] = ga
        dz = jnp.concatenate(dzs, axis=0)
        dz_s = _split(dz)
        dp_ref[:, 2 * KW + 2 * VW:GLA_COLS] = _dot3s(dz_s, wg_s, _DIMS["nt"]).astype(dp_ref.dtype)
        dwg_ref[...] += _dot3s(lr_s, dz_s, _DIMS["tn"])
        dbg_ref[...] += jnp.sum(dz, axis=0, keepdims=True)

    full = lambda shape: pl.BlockSpec(shape, lambda i: tuple(0 for _ in shape))
    rev = lambda i: (ng - 1 - i, 0)
    vm = 4 * _nbytes((rb, GLA_COLS), F32) + 2 * _nbytes((rb, VW), F32) + 2 * _nbytes((GLA_G + 1, KW, VW), F32)
    vm += 16 * _nbytes((KW, VW), F32)
    return _pallas(
        body,
        out_shape=(_out((T, GLA_COLS), BF16), _out((LRW, KW), F32),
                   _out((1, KW), F32), _out((1, VW), F32)),
        grid=(ng,),
        in_specs=[pl.BlockSpec((rb, GLA_COLS), rev), pl.BlockSpec((rb, VW), rev),
                  pl.BlockSpec((GLA_G, KW, VW), lambda i: (ng - 1 - i, 0, 0)),
                  pl.BlockSpec((1, KW, VW), lambda i: (jnp.maximum((ng - 1 - i) * GLA_G - 1, 0), 0, 0)),
                  full((LRW, KW)), full((1, KW)), full((1, VW)), full((KW, VW)), full((VW, VW)), full((CHUNK, CHUNK))],
        out_specs=(pl.BlockSpec((rb, GLA_COLS), rev), full((LRW, KW)), full((1, KW)), full((1, VW))),
        scratch_shapes=[pltpu.VMEM((KW, VW), F32)],
        compiler_params=_cparams(("arbitrary",), vm),
        name=name,
    )(p, dy, states, states, wg, bg, gn, mask, hm, low)


CONV_TM = 512
HALO = 32
CONV_RB = 64


def _glu(u):
    a = u[:, 0:CW]
    b = u[:, CW:2 * CW]
    return a * _sigmoid(b)


def _conv_taps(buf_ref, w_ref, rb0, first_tap):
    acc = jnp.zeros((CONV_RB, CW), F32)
    for j in range(CK):
        s = rb0 + first_tap(j)
        acc = acc + w_ref[j:j + 1, :] * buf_ref[s:s + CONV_RB, :]
    return acc


def _ln_fwd(c, lg, lb):
    mu = jnp.mean(c, axis=-1, keepdims=True)
    xc = c - mu
    rstd = lax.rsqrt(jnp.mean(xc * xc, axis=-1, keepdims=True) + EPS)
    n = xc * rstd
    return n, rstd, n * lg + lb


def _conv_fwd(u, w, b, lg, lb, name):
    T = u.shape[0]
    tm = CONV_TM

    def body(u_ref, uh_ref, w_ref, b_ref, lg_ref, lb_ref, o_ref, c_ref, hbuf):
        i = pl.program_id(0)
        hbuf[0:HALO, :] = jnp.where(i > 0, _glu(uh_ref[...]), 0.0)
        hbuf[HALO:HALO + tm, :] = _glu(u_ref[...])
        for r in range(tm // CONV_RB):
            rows = slice(r * CONV_RB, (r + 1) * CONV_RB)
            acc = _conv_taps(hbuf, w_ref, r * CONV_RB, lambda j: HALO - (CK - 1) + j)
            c_ref[rows, :] = acc
            _, _, zz = _ln_fwd(acc + b_ref[...], lg_ref[...], lb_ref[...])
            o_ref[rows, :] = (zz * _sigmoid(zz)).astype(o_ref.dtype)

    vec = pl.BlockSpec((1, CW), lambda i: (0, 0))
    return _pallas(
        body,
        out_shape=(_out((T, CW), BF16), _out((T, CW), F32)),
        grid=(T // tm,),
        in_specs=[pl.BlockSpec((tm, CONV_COLS), lambda i: (i, 0)),
                  pl.BlockSpec((HALO, CONV_COLS), lambda i: (jnp.maximum(i * (tm // HALO) - 1, 0), 0)),
                  pl.BlockSpec((HALO, CW), lambda i: (0, 0)), vec, vec, vec],
        out_specs=(pl.BlockSpec((tm, CW), lambda i: (i, 0)), pl.BlockSpec((tm, CW), lambda i: (i, 0))),
        scratch_shapes=[pltpu.VMEM((tm + HALO, CW), F32)],
        compiler_params=_cparams(("arbitrary",), 8 * _nbytes((tm, CONV_COLS), F32)),
        name=name,
    )(u, u, w, b, lg, lb)


def _conv_bwd_dc(u, conv, dout, b, lg, lb, name):
    T = u.shape[0]
    tm = CONV_TM
    nsteps = T // tm

    def body(u_ref, uh_ref, c_ref, do_ref, b_ref, lg_ref, lb_ref, dc_ref, dw_ref, db_ref, dlg_ref, dlb_ref, hbuf, dwacc):
        i = pl.program_id(0)

        @pl.when(i == 0)
        def _():
            dwacc[...] = jnp.zeros_like(dwacc)
            db_ref[...] = jnp.zeros_like(db_ref)
            dlg_ref[...] = jnp.zeros_like(dlg_ref)
            dlb_ref[...] = jnp.zeros_like(dlb_ref)

        hbuf[0:HALO, :] = jnp.where(i > 0, _glu(uh_ref[...]), 0.0)
        hbuf[HALO:HALO + tm, :] = _glu(u_ref[...])
        for r in range(tm // CONV_RB):
            rows = slice(r * CONV_RB, (r + 1) * CONV_RB)
            n, rstd, zz = _ln_fwd(c_ref[rows, :] + b_ref[...], lg_ref[...], lb_ref[...])
            sg = _sigmoid(zz)
            dz = do_ref[rows, :].astype(F32) * (sg * (1.0 + zz * (1.0 - sg)))
            dlg_ref[...] += jnp.sum(dz * n, axis=0, keepdims=True)
            dlb_ref[...] += jnp.sum(dz, axis=0, keepdims=True)
            dn = dz * lg_ref[...]
            dc = rstd * (dn - jnp.mean(dn, axis=-1, keepdims=True) - n * jnp.mean(dn * n, axis=-1, keepdims=True))
            dc_ref[rows, :] = dc
            db_ref[...] += jnp.sum(dc, axis=0, keepdims=True)
            for j in range(CK):
                s = r * CONV_RB + HALO - (CK - 1) + j
                prod = dc * hbuf[s:s + CONV_RB, :]
                dwacc[j] += jnp.sum(prod.reshape(CONV_RB // 8, 8, CW), axis=0)

        @pl.when(i == nsteps - 1)
        def _():
            dw_ref[...] = jnp.sum(dwacc[...], axis=1)

    vec = pl.BlockSpec((1, CW), lambda i: (0, 0))
    return _pallas(
        body,
        out_shape=(_out((T, CW), F32), _out((HALO, CW), F32),
                   _out((1, CW), F32), _out((1, CW), F32), _out((1, CW), F32)),
        grid=(nsteps,),
        in_specs=[pl.BlockSpec((tm, CONV_COLS), lambda i: (i, 0)),
                  pl.BlockSpec((HALO, CONV_COLS), lambda i: (jnp.maximum(i * (tm // HALO) - 1, 0), 0)),
                  pl.BlockSpec((tm, CW), lambda i: (i, 0)), pl.BlockSpec((tm, CW), lambda i: (i, 0)), vec, vec, vec],
        out_specs=(pl.BlockSpec((tm, CW), lambda i: (i, 0)), pl.BlockSpec((HALO, CW), lambda i: (0, 0)), vec, vec, vec),
        scratch_shapes=[pltpu.VMEM((tm + HALO, CW), F32), pltpu.VMEM((HALO, 8, CW), F32)],
        compiler_params=_cparams(("arbitrary",), 10 * _nbytes((tm, CONV_COLS), F32)),
        name=name,
    )(u, u, conv, dout, b, lg, lb)


def _conv_bwd_du(u, dc, w, name):
    T = u.shape[0]
    tm = CONV_TM
    nsteps = T // tm

    def body(u_ref, dc_ref, dch_ref, w_ref, du_ref, dcbuf):
        i = pl.program_id(0)
        dcbuf[0:tm, :] = dc_ref[...]
        dcbuf[tm:tm + HALO, :] = jnp.where(i < nsteps - 1, dch_ref[...], 0.0)
        for r in range(tm // CONV_RB):
            rows = slice(r * CONV_RB, (r + 1) * CONV_RB)
            dh = _conv_taps(dcbuf, w_ref, r * CONV_RB, lambda j: (CK - 1) - j)
            a = u_ref[rows, 0:CW]
            sb = _sigmoid(u_ref[rows, CW:2 * CW])
            du_ref[rows, 0:CW] = (dh * sb).astype(du_ref.dtype)
            du_ref[rows, CW:2 * CW] = (dh * a * sb * (1.0 - sb)).astype(du_ref.dtype)

    return _pallas(
        body,
        out_shape=_out((T, CONV_COLS), BF16),
        grid=(nsteps,),
        in_specs=[pl.BlockSpec((tm, CONV_COLS), lambda i: (i, 0)),
                  pl.BlockSpec((tm, CW), lambda i: (i, 0)),
                  pl.BlockSpec((HALO, CW), lambda i: (jnp.minimum((i + 1) * (tm // HALO), T // HALO - 1), 0)),
                  pl.BlockSpec((HALO, CW), lambda i: (0, 0))],
        out_specs=pl.BlockSpec((tm, CONV_COLS), lambda i: (i, 0)),
        scratch_shapes=[pltpu.VMEM((tm + HALO, CW), F32)],
        compiler_params=_cparams(("arbitrary",), 8 * _nbytes((tm, CONV_COLS), F32)),
        name=name,
    )(u, dc, dc, w)


ATT_G = 2


def _att_load_kv(p_any, kbuf, vbuf, sems, T):
    kc = pltpu.make_async_copy(p_any.at[:, pl.ds(AW, AW)], kbuf.at[pl.ds(LEFT, T), :], sems.at[0])
    vc = pltpu.make_async_copy(p_any.at[:, pl.ds(2 * AW, AW)], vbuf.at[pl.ds(LEFT, T), :], sems.at[1])
    kc.start()
    vc.start()
    kbuf[0:LEFT, :] = jnp.zeros((LEFT, AW), BF16)
    vbuf[0:LEFT, :] = jnp.zeros((LEFT, AW), BF16)
    kc.wait()
    vc.wait()


ATT_QB = CHUNK * ATT_G
ATT_KB = LEFT + ATT_QB
REL_PAD = 384
TOEP = -(-(ATT_QB + ATT_KB - 1) // 128) * 128


def _att_consts():
    m = np.arange(TOEP)
    d = ATT_KB - 1 - m
    idx = np.clip(d, -128, 128) + 128
    sel = (np.arange(REL_PAD)[:, None] == idx[None, :]) & (m[None, :] < ATT_QB + ATT_KB - 1)
    return jnp.asarray(sel.astype(np.float32))


def _att_build_bias(rel_ref, sel_ref, bias_scr):
    tr = jnp.dot(rel_ref[...], sel_ref[...], precision=HI, preferred_element_type=F32)
    qc = lax.broadcasted_iota(jnp.int32, (ATT_QB, ATT_KB), 0) // CHUNK
    kc = lax.broadcasted_iota(jnp.int32, (ATT_QB, ATT_KB), 1) // CHUNK
    band = (kc >= qc) & (kc <= qc + 8)
    for h in range(AH):
        rows = jnp.broadcast_to(tr[h:h + 1, :], (ATT_QB, TOEP))
        toep = pltpu.roll(rows, TOEP - (ATT_QB - 1), 1, stride=1, stride_axis=0)[:, 0:ATT_KB]
        bias_scr[h // 2, (h % 2) * ATT_QB:(h % 2 + 1) * ATT_QB, :] = jnp.where(band, toep, NEG)


def _att_probs(qst, kb, bias_p, n0):
    sc = lax.dot_general(qst, kb, _DIMS["nt"], preferred_element_type=F32) * (64 ** -0.5) + bias_p
    pos = lax.broadcasted_iota(jnp.int32, (2 * ATT_QB, ATT_KB), 1)
    sc = jnp.where(pos >= CHUNK * (8 - n0), sc, NEG)
    mx = jnp.max(sc, axis=-1, keepdims=True)
    ex = jnp.exp(sc - mx)
    return ex / jnp.sum(ex, axis=-1, keepdims=True)


def _head_stack(a2, lo):
    zero = jnp.zeros_like(a2)
    return jnp.concatenate([jnp.where(lo, a2, zero), jnp.where(lo, zero, a2)], axis=0)


def _att_fwd(p, rel, sel, name):
    T = p.shape[0]

    def body(q_ref, p_any, rel_ref, sel_ref, o_ref, kbuf, vbuf, bias_scr, sems):
        i = pl.program_id(0)

        @pl.when(i == 0)
        def _():
            _att_load_kv(p_any, kbuf, vbuf, sems, T)
            _att_build_bias(rel_ref, sel_ref, bias_scr)

        lo = lax.broadcasted_iota(jnp.int32, (ATT_QB, 128), 1) < 64
        n0 = i * ATT_G
        start = pl.multiple_of(i * ATT_QB, ATT_QB)
        for hp in range(AH // 2):
            cols = slice(hp * 128, (hp + 1) * 128)
            kb = kbuf[pl.ds(start, ATT_KB), cols]
            vb = vbuf[pl.ds(start, ATT_KB), cols]
            pr = _att_probs(_head_stack(q_ref[:, cols], lo), kb, bias_scr[hp], n0)
            pv = jnp.dot(pr.astype(BF16), vb, preferred_element_type=F32)
            o_ref[:, cols] = jnp.where(lo, pv[0:ATT_QB], pv[ATT_QB:2 * ATT_QB]).astype(o_ref.dtype)

    vm = 2 * _nbytes((T + LEFT, AW), BF16) + 8 * _nbytes((2 * ATT_QB, ATT_KB), F32) + (8 << 20)
    return _pallas(
        body,
        out_shape=_out((T, AW), BF16),
        grid=(T // ATT_QB,),
        in_specs=[pl.BlockSpec((ATT_QB, AW), lambda i: (i, 0)), pl.BlockSpec(memory_space=pl.ANY),
                  pl.BlockSpec((8, REL_PAD), lambda i: (0, 0)), pl.BlockSpec((REL_PAD, TOEP), lambda i: (0, 0))],
        out_specs=pl.BlockSpec((ATT_QB, AW), lambda i: (i, 0)),
        scratch_shapes=[pltpu.VMEM((T + LEFT, AW), BF16), pltpu.VMEM((T + LEFT, AW), BF16),
                        pltpu.VMEM((AH // 2, 2 * ATT_QB, ATT_KB), F32), pltpu.SemaphoreType.DMA((2,))],
        compiler_params=_cparams(("arbitrary",), vm),
        name=name,
    )(p, p, rel, sel)


def _att_bwd(p, do, rel, sel, name):
    T = p.shape[0]
    nsteps = T // ATT_QB

    def body(q_ref, p_any, do_ref, rel_ref, sel_ref, dp_any, drel_ref,
             kbuf, vbuf, dqbuf, dkbuf, dvbuf, bias_scr, dbias_scr, dtr_scr, sems):
        i = pl.program_id(0)

        @pl.when(i == 0)
        def _():
            _att_load_kv(p_any, kbuf, vbuf, sems, T)
            _att_build_bias(rel_ref, sel_ref, bias_scr)
            dkbuf[...] = jnp.zeros_like(dkbuf)
            dvbuf[...] = jnp.zeros_like(dvbuf)
            dbias_scr[...] = jnp.zeros_like(dbias_scr)

        lo = lax.broadcasted_iota(jnp.int32, (ATT_QB, 128), 1) < 64
        n0 = i * ATT_G
        start = pl.multiple_of(i * ATT_QB, ATT_QB)
        for hp in range(AH // 2):
            cols = slice(hp * 128, (hp + 1) * 128)
            kb = kbuf[pl.ds(start, ATT_KB), cols]
            vb = vbuf[pl.ds(start, ATT_KB), cols]
            qst = _head_stack(q_ref[:, cols], lo)
            dost = _head_stack(do_ref[:, cols].astype(BF16), lo)
            pr = _att_probs(qst, kb, bias_scr[hp], n0)
            dpr = lax.dot_general(dost, vb, _DIMS["nt"], preferred_element_type=F32)
            ds = pr * (dpr - jnp.sum(dpr * pr, axis=-1, keepdims=True))
            dbias_scr[hp] += ds
            dsb = (ds * (64 ** -0.5)).astype(BF16)
            dq = jnp.dot(dsb, kb, preferred_element_type=F32)
            dqbuf[pl.ds(start, ATT_QB), cols] = jnp.where(lo, dq[0:ATT_QB], dq[ATT_QB:2 * ATT_QB]).astype(BF16)
            dkbuf[pl.ds(start, ATT_KB), cols] += lax.dot_general(dsb, qst, _DIMS["tn"], preferred_element_type=F32)
            dvbuf[pl.ds(start, ATT_KB), cols] += lax.dot_general(pr.astype(BF16), dost, _DIMS["tn"], preferred_element_type=F32)

        @pl.when(i == nsteps - 1)
        def _():
            kbuf[pl.ds(LEFT, T), :] = dkbuf[pl.ds(LEFT, T), :].astype(BF16)
            vbuf[pl.ds(LEFT, T), :] = dvbuf[pl.ds(LEFT, T), :].astype(BF16)
            cps = [pltpu.make_async_copy(dqbuf, dp_any.at[:, pl.ds(0, AW)], sems.at[0]),
                   pltpu.make_async_copy(kbuf.at[pl.ds(LEFT, T), :], dp_any.at[:, pl.ds(AW, AW)], sems.at[1]),
                   pltpu.make_async_copy(vbuf.at[pl.ds(LEFT, T), :], dp_any.at[:, pl.ds(2 * AW, AW)], sems.at[2])]
            for cp in cps:
                cp.start()
            dtr_scr[...] = jnp.zeros_like(dtr_scr)
            ri = lax.broadcasted_iota(jnp.int32, (ATT_QB, ATT_QB), 0)
            ci = lax.broadcasted_iota(jnp.int32, (ATT_QB, ATT_QB), 1)
            flip = jnp.where(ri + ci == ATT_QB - 1, 1.0, 0.0)
            for h in range(AH):
                db = dbias_scr[h // 2, (h % 2) * ATT_QB:(h % 2 + 1) * ATT_QB, :]
                db = jnp.dot(flip, db, precision=HI, preferred_element_type=F32)
                wide = jnp.concatenate([db, jnp.zeros((ATT_QB, TOEP - ATT_KB), F32)], axis=1)
                diag = pltpu.roll(wide, 0, 1, stride=1, stride_axis=0)
                dtr_scr[h:h + 1, :] = jnp.sum(diag, axis=0, keepdims=True)
            drel_ref[...] = lax.dot_general(dtr_scr[...], sel_ref[...], _DIMS["nt"], precision=HI, preferred_element_type=F32)
            for cp in cps:
                cp.wait()

    vm = 3 * _nbytes((T + LEFT, AW), BF16) + 2 * _nbytes((T + LEFT, AW), F32) + 12 * _nbytes((2 * ATT_QB, ATT_KB), F32) + (8 << 20)
    return _pallas(
        body,
        out_shape=(_out((T, ATT_COLS), BF16), _out((8, REL_PAD), F32)),
        grid=(nsteps,),
        in_specs=[pl.BlockSpec((ATT_QB, AW), lambda i: (i, 0)), pl.BlockSpec(memory_space=pl.ANY),
                  pl.BlockSpec((ATT_QB, AW), lambda i: (i, 0)),
                  pl.BlockSpec((8, REL_PAD), lambda i: (0, 0)), pl.BlockSpec((REL_PAD, TOEP), lambda i: (0, 0))],
        out_specs=(pl.BlockSpec(memory_space=pl.ANY), pl.BlockSpec((8, REL_PAD), lambda i: (0, 0))),
        scratch_shapes=[pltpu.VMEM((T + LEFT, AW), BF16), pltpu.VMEM((T + LEFT, AW), BF16), pltpu.VMEM((T, AW), BF16),
                        pltpu.VMEM((T + LEFT, AW), F32), pltpu.VMEM((T + LEFT, AW), F32),
                        pltpu.VMEM((AH // 2, 2 * ATT_QB, ATT_KB), F32), pltpu.VMEM((AH // 2, 2 * ATT_QB, ATT_KB), F32),
                        pltpu.VMEM((8, TOEP), F32), pltpu.SemaphoreType.DMA((3,))],
        compiler_params=_cparams(("arbitrary",), vm),
        name=name,
    )(p, p, do, rel, sel)


def _layer_fwd(h, wl, consts, tag):
    p_gla, p_conv, p_att, xn = _mm_fan(h, [wl["w_gla"], wl["w_conv"], wl["w_att"]], mode="nt", out_dtypes=(F32, F32, BF16),
                                       norm_g=wl["norm_mix"], name=f"{tag}_proj")
    o_gla, states = _gla_fwd(p_gla, wl["wg"], wl["bg"], wl["gla_norm"], consts, f"{tag}_gla_fwd")
    o_conv, conv = _conv_fwd(p_conv, wl["w_dw"], wl["b_dw"], wl["ln_g"], wl["ln_b"], f"{tag}_conv_fwd")
    rel = jnp.pad(wl["rel_bias"], ((0, 8 - AH), (0, REL_PAD - N_REL)))
    o_att = _att_fwd(p_att, rel, consts[3], f"{tag}_att_fwd")
    h1 = _mm_sum([o_gla, o_conv, o_att], [wl["w_out_g"], wl["w_out_c"], wl["w_out_a"]], mode="nn", out_dtype=F32, extra=h,
                 name=f"{tag}_out")
    u, xn2 = _mm(h1, wl["w_up"], mode="nn", out_dtype=BF16, tm=1024, tn=1024, b_chips=True, norm_g=wl["norm_ffn"],
                 name=f"{tag}_mlp_up")
    h2 = _mm(u, wl["w_down"], mode="nn", out_dtype=F32, tm=1024, tk=2048, a_pro="relu2", epi="add", extra=h1,
             name=f"{tag}_mlp_down")
    saved = dict(h=h, xn=xn, p_gla=p_gla, p_conv=p_conv, p_att=p_att, states=states, o_gla=o_gla, o_conv=o_conv, conv=conv,
                 o_att=o_att, rel=rel, h1=h1, xn2=xn2, u=u)
    return h2, saved


def _layer_bwd(dh2, sv, wl, consts, tag, emit=lambda name, grad: None):
    g = {}
    du = _mm(dh2, wl["w_down"], mode="nt", out_dtype=BF16, tm=1024, tn=1024, epi="relu2grad", extra=sv["u"],
             name=f"{tag}_mlp_down_dx")
    g["w_down"] = _mm(sv["u"], dh2, mode="tn", out_dtype=F32, tm=2048, tn=1024, tk=512, a_pro="relu2", name=f"{tag}_mlp_down_dw")
    emit("w_down", g["w_down"].reshape(4, D_FF // 4, D))
    dh1, g["norm_ffn"] = _mm(du, wl["w_up"], mode="nt", out_dtype=F32, tm=1024, tk=1024, b_chips=True,
                             norm_bwd=(sv["h1"], wl["norm_ffn"], dh2), name=f"{tag}_mlp_up_dx")
    g["w_up"] = _mm(sv["xn2"], du, mode="tn", out_dtype=F32, tm=1024, tn=1024, tk=1024, out_chips=True, name=f"{tag}_mlp_up_dw")
    emit("w_up", g["w_up"])
    d_gla, d_conv, d_att = _mm_fan(dh1, [wl["w_out_g"], wl["w_out_c"], wl["w_out_a"]], mode="nt", out_dtypes=(F32, F32, F32),
                                   name=f"{tag}_out_dx")
    g["w_out_g"], g["w_out_c"], g["w_out_a"] = _mm_tn_multi([sv["o_gla"], sv["o_conv"], sv["o_att"], dh1],
                                                            [(0, 3), (1, 3), (2, 3)], name=f"{tag}_out_dw")
    emit("w_out", jnp.concatenate([g["w_out_g"], g["w_out_c"], g["w_out_a"]], axis=0).reshape(4, D // 4, D))
    dp_gla, g["wg"], g["bg"], g["gla_norm"] = _gla_bwd(sv["p_gla"], d_gla, sv["states"], wl["wg"], wl["bg"], wl["gla_norm"],
                                                       consts, f"{tag}_gla_bwd")
    dc, g["w_dw"], g["b_dw"], g["ln_g"], g["ln_b"] = _conv_bwd_dc(sv["p_conv"], sv["conv"], d_conv, wl["b_dw"], wl["ln_g"],
                                                                  wl["ln_b"], f"{tag}_conv_bwd_dc")
    dp_conv = _conv_bwd_du(sv["p_conv"], dc, wl["w_dw"], f"{tag}_conv_bwd_du")
    dp_att, drel = _att_bwd(sv["p_att"], d_att, sv["rel"], consts[3], f"{tag}_att_bwd")
    g["rel_bias"] = drel[0:AH, 0:N_REL]
    g["w_gla"], g["w_conv"], g["w_att"] = _mm_tn_multi([sv["xn"], dp_gla, dp_conv, dp_att], [(1, 0), (2, 0), (3, 0)],
                                                       name=f"{tag}_proj_dw")
    emit("w_in", _join_w_in_t(g))
    dh, g["norm_mix"] = _mm_sum([dp_gla, dp_conv, dp_att], [wl["w_gla"], wl["w_conv"], wl["w_att"]], mode="nn", out_dtype=F32,
                                norm_bwd=(sv["h"], wl["norm_mix"], dh1), name=f"{tag}_proj_dx")
    return dh, g


def _local_step(x, target, layers, norm_final, emit=lambda layer, name, grad: None):
    consts = _gla_consts() + (_att_consts(),)
    h = x
    saved = []
    for l, wl in enumerate(layers):
        h, sv = _layer_fwd(h, wl, consts, f"l{l}")
        saved.append(sv)
    loss, dh, g_final = _final_loss(h, norm_final, target, "final_loss")
    grads = [None] * len(layers)
    for l in reversed(range(len(layers))):
        dh, grads[l] = _layer_bwd(dh, saved[l], layers[l], consts, f"l{l}", functools.partial(emit, l))
    return loss, dh, grads, g_final


ANY = pl.BlockSpec(memory_space=pl.ANY)


def _place():
    x, y, c = lax.axis_index("x"), lax.axis_index("y"), lax.axis_index("c")
    chips = [(1 - x, y), (x, 1 - y), (1 - x, 1 - y)]
    return x, y, c, chips


def _shape(shape, dtype):
    return jax.ShapeDtypeStruct(tuple(shape), dtype)


def _remote(src, dst, send_sem, recv_sem, to):
    return pltpu.make_async_remote_copy(src_ref=src, dst_ref=dst, send_sem=send_sem, recv_sem=recv_sem,
                                        device_id=to, device_id_type=MESH)


class _Staged:
    def __init__(self, src, dst, buf, sems):
        self.load = pltpu.make_async_copy(src, buf, sems.at[0])
        self.store = pltpu.make_async_copy(buf, dst, sems.at[1])

    def start(self):
        self.load.start()

    def wait(self):
        self.load.wait()
        self.store.start()
        self.store.wait()


def _phase_gather_ici(src, then):
    R, C = src.shape
    rh = R // 2

    def copies(ins, outs, sems):
        x, y, c, chips = _place()
        me = 2 * x + y
        local = _Staged(ins[0], outs[0].at[me], sems[3], sems[2])
        sends = [_remote(ins[0].at[pl.ds(c * rh, rh), :], outs[0].at[me, pl.ds(c * rh, rh), :], sems[0].at[k], sems[1].at[k],
                         (px, py, c)) for k, (px, py) in enumerate(chips)]
        recvs = [_remote(outs[0].at[2 * px + py, pl.ds(c * rh, rh), :], outs[0].at[2 * px + py, pl.ds(c * rh, rh), :],
                         sems[0].at[k], sems[1].at[k], (px, py, c)) for k, (px, py) in enumerate(chips)]
        return local, sends, recvs

    def start(ins, outs, sems):
        local, sends, _ = copies(ins, outs, sems)
        local.start()
        for cp in sends:
            cp.start()

    def finish(ins, outs, sems):
        local, sends, recvs = copies(ins, outs, sems)
        for cp in recvs:
            cp.wait_recv()
        for cp in sends:
            cp.wait_send()
        local.wait()

    return _Comm([src], [_shape((4, R, C), src.dtype)], {}, [(3,), (3,), (2,), ((R, C), src.dtype)], start, finish, then)


def _phase_gather_d2d(part, then):
    _, R, C = part.shape
    rh = R // 2

    def copies(ins, outs, sems):
        x, y, c, chips = _place()
        sends = [_remote(ins[0].at[2 * px + py, pl.ds(c * rh, rh), :], outs[0].at[2 * px + py, pl.ds(c * rh, rh), :],
                         sems[0].at[k], sems[1].at[k], (x, y, 1 - c)) for k, (px, py) in enumerate(chips)]
        recvs = [_remote(outs[0].at[2 * px + py, pl.ds((1 - c) * rh, rh), :], outs[0].at[2 * px + py, pl.ds((1 - c) * rh, rh), :],
                         sems[0].at[k], sems[1].at[k], (x, y, 1 - c)) for k, (px, py) in enumerate(chips)]
        return sends, recvs

    def start(ins, outs, sems):
        for cp in copies(ins, outs, sems)[0]:
            cp.start()

    def finish(ins, outs, sems):
        sends, recvs = copies(ins, outs, sems)
        for cp in recvs:
            cp.wait_recv()
        for cp in sends:
            cp.wait_send()

    return _Comm([part], [_shape(part.shape, part.dtype)], {0: 0}, [(3,), (3,)], start, finish, then)


def _phase_pair_exchange(full, then):
    _, R, C = full.shape
    rh = R // 2

    def copy(ins, outs, sems):
        x, y, c, _ = _place()
        return _remote(ins[0].at[:, pl.ds((1 - c) * rh, rh), :], outs[0], sems[0].at[0], sems[1].at[0], (x, y, 1 - c))

    return _Comm([full], [_shape((4, rh, C), full.dtype)], {}, [(1,), (1,)],
                 lambda ins, outs, sems: copy(ins, outs, sems).start(),
                 lambda ins, outs, sems: copy(ins, outs, sems).wait(), then)


def _phase_chip_scatter(parts, then):
    def copies(ins, outs, sems):
        x, y, c, chips = _place()
        me = 2 * x + y
        local = _Staged(ins[0].at[me], outs[0].at[me], sems[3], sems[2])
        sends = [_remote(ins[0].at[2 * px + py], outs[0].at[me], sems[0].at[k], sems[1].at[k], (px, py, c))
                 for k, (px, py) in enumerate(chips)]
        recvs = [_remote(outs[0].at[2 * px + py], outs[0].at[2 * px + py], sems[0].at[k], sems[1].at[k], (px, py, c))
                 for k, (px, py) in enumerate(chips)]
        return local, sends, recvs

    def start(ins, outs, sems):
        local, sends, _ = copies(ins, outs, sems)
        local.start()
        for cp in sends:
            cp.start()

    def finish(ins, outs, sems):
        local, sends, recvs = copies(ins, outs, sems)
        for cp in recvs:
            cp.wait_recv()
        for cp in sends:
            cp.wait_send()
        local.wait()

    return _Comm([parts], [_shape(parts.shape, parts.dtype)], {}, [(3,), (3,), (2,), (parts.shape[1:], parts.dtype)],
                 start, finish, then)


def _phase_pair_allgather(half, layer, depth, into, then):
    rh, C = half.shape

    def copies(ins, outs, sems):
        x, y, c, _ = _place()
        mine = outs[0].at[layer, pl.ds(c * rh, rh), :]
        theirs = outs[0].at[layer, pl.ds((1 - c) * rh, rh), :]
        return (_Staged(ins[0], mine, sems[3], sems[2]),
                _remote(ins[0], mine, sems[0].at[0], sems[1].at[0], (x, y, 1 - c)),
                _remote(theirs, theirs, sems[0].at[0], sems[1].at[0], (x, y, 1 - c)))

    def start(ins, outs, sems):
        local, send, _ = copies(ins, outs, sems)
        local.start()
        send.start()

    def finish(ins, outs, sems):
        local, send, recv = copies(ins, outs, sems)
        recv.wait_recv()
        send.wait_send()
        local.wait()

    ins = [half] if into is None else [half, into]
    return _Comm(ins, [_shape((depth, 2 * rh, C), half.dtype)], {} if into is None else {1: 0},
                 [(1,), (1,), (2,), ((rh, C), half.dtype)], start, finish, then)


def _comm_only(comms, name):
    plan = _Plan()
    for c in comms:
        plan.at(name, c)
    saved, _PLAN[0] = _PLAN[0], plan
    try:
        def body(o_ref):
            o_ref[...] = jnp.zeros_like(o_ref)

        _pallas(body, out_shape=[jax.ShapeDtypeStruct((8, 128), F32)], in_specs=[],
                out_specs=[pl.BlockSpec(memory_space=pltpu.VMEM)], name=name)()
    finally:
        _PLAN[0] = saved


def _row_tile(rows, cols, itemsize=4, budget=1 << 20, mult=8):
    fits = [t for t in range(mult, rows + 1, mult) if rows % t == 0 and t * cols * itemsize <= budget]
    return max(fits) if fits else rows


def _pair_add(full, got, c, name):
    _, L, R, C = full.shape
    rh = R // 2
    tr = _row_tile(rh, C, budget=2 << 20, mult=16)
    nb = rh // tr

    def body(c_ref, a_ref, b_ref, o_ref):
        o_ref[...] = (a_ref[...] + b_ref[...]).astype(o_ref.dtype)

    grid_spec = pltpu.PrefetchScalarGridSpec(
        num_scalar_prefetch=1,
        grid=(4, L, nb),
        in_specs=[pl.BlockSpec((1, 1, tr, C), lambda j, l, i, c_ref: (j, l, c_ref[0] * nb + i, 0)),
                  pl.BlockSpec((1, 1, tr, C), lambda j, l, i, c_ref: (j, l, i, 0))],
        out_specs=pl.BlockSpec((1, 1, tr, C), lambda j, l, i, c_ref: (j, l, i, 0)),
    )
    return _pallas(
        body,
        out_shape=_out((4, L, rh, C), BF16),
        grid_spec=grid_spec,
        compiler_params=_cparams(("parallel", "parallel", "parallel"), 8 * tr * C * 4),
        name=name,
    )(jnp.reshape(c, (1,)).astype(jnp.int32), full, got)


def _sum_chips(parts, name):
    _, L, rh, C = parts.shape
    tr = _row_tile(rh, C, budget=2 << 20, mult=16)

    def body(p_ref, o_ref):
        acc = p_ref[0].astype(F32)
        for j in range(1, 4):
            acc = acc + p_ref[j].astype(F32)
        o_ref[...] = acc

    return _pallas(
        body,
        out_shape=_out((L, rh, C), F32),
        grid=(L, rh // tr),
        in_specs=[pl.BlockSpec((4, 1, tr, C), lambda l, i: (0, l, i, 0))],
        out_specs=pl.BlockSpec((1, tr, C), lambda l, i: (l, i, 0)),
        compiler_params=_cparams(("parallel", "parallel"), 16 * tr * C * 4),
        name=name,
    )(parts)


def _allreduce_small(v):
    R = v.shape[0]

    def body(v_ref, o_ref, slots, send_sems, recv_sems):
        x, y, c, _ = _place()
        me = 4 * x + 2 * y + c
        slots[me] = v_ref[...]
        cps = []
        for r in range(1, 8):
            px, py, pc = x ^ (r >> 2), y ^ ((r >> 1) & 1), c ^ (r & 1)
            cps.append(pltpu.make_async_remote_copy(src_ref=v_ref, dst_ref=slots.at[me], send_sem=send_sems.at[r - 1],
                                                    recv_sem=recv_sems.at[r - 1], device_id=(px, py, pc), device_id_type=MESH))
            cps[-1].start()
        for r in range(1, 8):
            px, py, pc = x ^ (r >> 2), y ^ ((r >> 1) & 1), c ^ (r & 1)
            theirs = slots.at[4 * px + 2 * py + pc]
            pltpu.make_async_remote_copy(src_ref=theirs, dst_ref=theirs, send_sem=send_sems.at[r - 1], recv_sem=recv_sems.at[r - 1],
                                         device_id=(px, py, pc), device_id_type=MESH).wait_recv()
        acc = slots[0]
        for j in range(1, 8):
            acc = acc + slots[j]
        o_ref[...] = acc
        for cp in cps:
            cp.wait_send()

    return pl.pallas_call(
        body,
        out_shape=jax.ShapeDtypeStruct(v.shape, F32),
        in_specs=[pl.BlockSpec(memory_space=pltpu.VMEM)],
        out_specs=pl.BlockSpec(memory_space=pltpu.VMEM),
        scratch_shapes=[pltpu.VMEM((8, R, 128), F32), pltpu.SemaphoreType.DMA((7,)), pltpu.SemaphoreType.DMA((7,))],
        name="allreduce_small",
    )(v)


def _adamw_math(w, g, m, v):
    m = ADAM_B1 * m + (1.0 - ADAM_B1) * g
    v = ADAM_B2 * v + (1.0 - ADAM_B2) * (g * g)
    m_hat = m / (1.0 - ADAM_B1 ** ADAM_STEP)
    v_hat = v / (1.0 - ADAM_B2 ** ADAM_STEP)
    delta = -ADAM_LR * (m_hat / (jnp.sqrt(v_hat) + ADAM_EPS) + ADAM_WD * w)
    return delta, m, v


def _adamw(ws, gs, ms, vs, name, budget=1 << 19):
    n = len(ws)
    tiles = [_row_tile(w.shape[1], w.shape[2], budget=budget) for w in ws]
    per_layer = [w.shape[1] // t for w, t in zip(ws, tiles)]
    steps = [w.shape[0] * p for w, p in zip(ws, per_layer)]
    starts = [sum(steps[:k]) for k in range(n)]

    def body(*refs):
        i = pl.program_id(0)
        for k in range(n):
            w_ref, g_ref, m_ref, v_ref = (refs[j * n + k] for j in range(4))
            outs = [refs[(4 + j) * n + k] for j in range(3)]

            @pl.when((i >= starts[k]) & (i < starts[k] + steps[k]))
            def _(w_ref=w_ref, g_ref=g_ref, m_ref=m_ref, v_ref=v_ref, outs=outs):
                outs[0][...], outs[1][...], outs[2][...] = _adamw_math(w_ref[...], g_ref[...], m_ref[...], v_ref[...])

    def spec(k):
        def index(i):
            local = jnp.clip(i - starts[k], 0, steps[k] - 1)
            return local // per_layer[k], local % per_layer[k], 0
        return pl.BlockSpec((None, tiles[k], ws[k].shape[2]), index)

    specs = [spec(k) for k in range(n)]
    outs = [_out(w.shape, F32) for w in ws]
    res = _pallas(
        body,
        out_shape=tuple(outs * 3),
        grid=(sum(steps),),
        in_specs=specs * 4,
        out_specs=tuple(specs * 3),
        compiler_params=_cparams(("arbitrary",), sum(16 * t * w.shape[2] * 4 for w, t in zip(ws, tiles))),
        name=name,
    )(*ws, *gs, *ms, *vs)
    return res[:n], res[n:2 * n], res[2 * n:]


def _adamw_slabs(w, g, m, v, name, slabs=59):
    n, L, C = w.shape
    assert n % slabs == 0

    def body(w_ref, g_ref, m_ref, v_ref, d_ref, nm_ref, nv_ref):
        d_ref[...], nm_ref[...], nv_ref[...] = _adamw_math(w_ref[...], g_ref[...], m_ref[...], v_ref[...])

    blk = pl.BlockSpec((slabs, L, C), lambda i: (i, 0, 0))
    out = _out(w.shape, F32)
    return _pallas(
        body,
        out_shape=(out, out, out),
        grid=(n // slabs,),
        in_specs=[blk] * 4,
        out_specs=(blk, blk, blk),
        compiler_params=_cparams(("parallel",), 16 * slabs * 8 * C * 4),
        name=name,
    )(w, g, m, v)


def _adamw_small(ws, gs, ms, vs):
    n = len(ws)

    def body(*refs):
        for t in range(n):
            w_ref, g_ref, m_ref, v_ref = (refs[k * n + t] for k in range(4))
            d_ref, nm_ref, nv_ref = (refs[(4 + k) * n + t] for k in range(3))
            d_ref[...], nm_ref[...], nv_ref[...] = _adamw_math(w_ref[...], g_ref[...], m_ref[...], v_ref[...])

    vmem = pl.BlockSpec(memory_space=pltpu.VMEM)
    outs = [jax.ShapeDtypeStruct(w.shape, F32) for w in ws]
    res = pl.pallas_call(
        body,
        out_shape=outs * 3,
        in_specs=[vmem] * (4 * n),
        out_specs=[vmem] * (3 * n),
        name="adamw_small",
    )(*ws, *gs, *ms, *vs)
    return res[:n], res[n:2 * n], res[2 * n:]


IN_SIZES = (192, 192, 384, 384, 16, 512, 384, 384, 384)
IN_OFFS = tuple(int(v) for v in np.cumsum((0,) + IN_SIZES))
SMALL = ("norm_mix", "w_gla_gate", "b_gla_gate", "gla_norm", "b_dw", "conv_ln_g", "conv_ln_b", "rel_bias", "norm_ffn")


def _pad_cols(a, n):
    return jnp.pad(a, ((0, 0), (0, n - a.shape[1])))


W_IN_SHARD = 708
W_IN_ROWS = 736


def _pad_rows(a, n):
    return jnp.pad(a, ((0, n - a.shape[0]), (0, 0)))


def _split_w_in_t(w):
    s = [w[IN_OFFS[i]:IN_OFFS[i + 1]] for i in range(9)]
    w_gla = jnp.concatenate([_pad_rows(s[0], KW), _pad_rows(s[1], KW), s[2], s[3], _pad_rows(s[4], LRW)], axis=0)
    return w_gla, s[5], jnp.concatenate(s[6:9], axis=0)


def _join_w_in_t(g):
    gg = g["w_gla"]
    full = jnp.concatenate([gg[0:192], gg[KW:KW + 192], gg[2 * KW:2 * KW + VW], gg[2 * KW + VW:2 * KW + 2 * VW],
                            gg[2 * KW + 2 * VW:2 * KW + 2 * VW + 16], g["w_conv"], g["w_att"]], axis=0)
    return jnp.pad(full.reshape(4, W_IN_SHARD, D), ((0, 0), (0, W_IN_ROWS - W_IN_SHARD), (0, 0)))


def _pack(arrs, rows):
    flat = jnp.concatenate([a.reshape(-1) for a in arrs])
    return jnp.pad(flat, (0, rows * 128 - flat.shape[0])).reshape(rows, 128)


def _unpack(packed, shapes):
    flat = packed.reshape(-1)
    out, off = [], 0
    for s in shapes:
        n = int(np.prod(s))
        out.append(flat[off:off + n].reshape(s))
        off += n
    return out


def kernel(x, norm_mix, w_in, w_gla_gate, b_gla_gate, gla_norm, w_dw, b_dw, conv_ln_g, conv_ln_b, rel_bias, w_out, norm_ffn, w_up, w_down, norm_final, loss_target, m_norm_mix, m_w_in, m_w_gla_gate, m_b_gla_gate, m_gla_norm, m_w_dw, m_b_dw, m_conv_ln_g, m_conv_ln_b, m_rel_bias, m_w_out, m_norm_ffn, m_w_up, m_w_down, m_norm_final, v_norm_mix, v_w_in, v_w_gla_gate, v_b_gla_gate, v_gla_norm, v_w_dw, v_b_dw, v_conv_ln_g, v_conv_ln_b, v_rel_bias, v_w_out, v_norm_ffn, v_w_up, v_w_down, v_norm_final):
    P = dict(norm_mix=norm_mix, w_in=w_in, w_gla_gate=w_gla_gate, b_gla_gate=b_gla_gate, gla_norm=gla_norm, w_dw=w_dw, b_dw=b_dw,
             conv_ln_g=conv_ln_g, conv_ln_b=conv_ln_b, rel_bias=rel_bias, w_out=w_out, norm_ffn=norm_ffn, w_up=w_up,
             w_down=w_down, norm_final=norm_final)
    Mo = dict(norm_mix=m_norm_mix, w_in=m_w_in, w_gla_gate=m_w_gla_gate, b_gla_gate=m_b_gla_gate, gla_norm=m_gla_norm, w_dw=m_w_dw,
              b_dw=m_b_dw, conv_ln_g=m_conv_ln_g, conv_ln_b=m_conv_ln_b, rel_bias=m_rel_bias, w_out=m_w_out, norm_ffn=m_norm_ffn,
              w_up=m_w_up, w_down=m_w_down, norm_final=m_norm_final)
    Vo = dict(norm_mix=v_norm_mix, w_in=v_w_in, w_gla_gate=v_w_gla_gate, b_gla_gate=v_b_gla_gate, gla_norm=v_gla_norm, w_dw=v_w_dw,
              b_dw=v_b_dw, conv_ln_g=v_conv_ln_g, conv_ln_b=v_conv_ln_b, rel_bias=v_rel_bias, w_out=v_w_out, norm_ffn=v_norm_ffn,
              w_up=v_w_up, w_down=v_w_down, norm_final=v_norm_final)
    depth = w_in.shape[0]
    xi, yi, ci = lax.axis_index("x"), lax.axis_index("y"), lax.axis_index("c")
    chip = 2 * xi + yi

    plan = _Plan()
    _PLAN[0] = plan
    layers = [dict(
        norm_mix=norm_mix[l][None], wg=jnp.pad(w_gla_gate[l], ((0, LRW - 16), (0, KW - 192))),
        bg=_pad_cols(b_gla_gate[l][None], KW), gla_norm=gla_norm[l][None], b_dw=b_dw[l][None], ln_g=conv_ln_g[l][None],
        ln_b=conv_ln_b[l][None], rel_bias=rel_bias[l], norm_ffn=norm_ffn[l][None]) for l in range(depth)]

    w_in_t, m_w_in_t, v_w_in_t = (jnp.transpose(a, (2, 0, 1)) for a in (w_in, m_w_in, v_w_in))

    def w_in_shard(l):
        return _pad_rows(w_in_t[:, l, :], W_IN_ROWS).astype(BF16)

    def have_w_in(l, full):
        rows = jnp.concatenate([full[j, 0:W_IN_SHARD] for j in range(4)], axis=0)
        layers[l]["w_gla"], layers[l]["w_conv"], layers[l]["w_att"] = _split_w_in_t(rows)

    def have_w_out(l, full):
        w = full.reshape(D, D)
        layers[l]["w_out_g"], layers[l]["w_out_c"], layers[l]["w_out_a"] = w[0:VW], w[VW:VW + CW], w[VW + CW:]

    def have_w_up(l, full):
        layers[l]["w_up"] = full

    def have_w_down(l, full):
        layers[l]["w_down"] = full.reshape(D_FF, D)

    def have_w_dw(full):
        taps = full.reshape(4, depth, HALO, CW // 4)
        for l in range(depth):
            layers[l]["w_dw"] = jnp.transpose(taps[:, l], (1, 0, 2)).reshape(HALO, CW)

    first_d2d = []

    def first_ici(shard, have):
        return _phase_gather_ici(shard, lambda outs: first_d2d.append(_phase_gather_d2d(outs[0], lambda done: have(done[0]))))

    w_dw_pad = jnp.pad(w_dw, ((0, 0), (0, HALO - CK), (0, 0))).reshape(depth * HALO, CW // 4)
    _comm_only([first_ici(w_in_shard(0), functools.partial(have_w_in, 0)), first_ici(w_dw_pad, have_w_dw)],
               "gather_first_ici")
    _comm_only(first_d2d, "gather_first_d2d")

    def gather_behind(shard, ici_call, d2d_call, have):
        plan.at(ici_call, _phase_gather_ici(
            shard, lambda outs: plan.at(d2d_call, _phase_gather_d2d(outs[0], lambda done: have(done[0])))))

    for l in range(depth):
        if l > 0:
            gather_behind(w_in_shard(l), f"l{l - 1}_mlp_up", f"l{l - 1}_mlp_down", functools.partial(have_w_in, l))
        gather_behind(w_out[l].astype(BF16), f"l{l - 1}_mlp_down" if l > 0 else "l0_proj", f"l{l}_gla_fwd",
                      functools.partial(have_w_out, l))
        if l > 0:
            gather_behind(w_up[l].astype(BF16), f"l{l}_proj", f"l{l}_gla_fwd", functools.partial(have_w_up, l))
            gather_behind(w_down[l].astype(BF16), f"l{l}_gla_fwd", f"l{l}_att_fwd", functools.partial(have_w_down, l))
        else:
            gather_behind(w_up[l].astype(BF16), f"l{l}_gla_fwd", f"l{l}_att_fwd", functools.partial(have_w_up, l))
            gather_behind(w_down[l].astype(BF16), f"l{l}_att_fwd", f"l{l}_mlp_up", functools.partial(have_w_down, l))

    reduced = {}
    last_swap = []

    def reduce_calls(name, l):
        if name == "w_down":
            return f"l{l}_mlp_up_dx", f"l{l}_gla_bwd", f"l{l}_conv_bwd_dc"
        if name == "w_up":
            return f"l{l}_out_dx", f"l{l}_att_bwd", f"l{l}_proj_dw"
        if name == "w_out":
            return f"l{l}_gla_bwd", f"l{l}_conv_bwd_dc", f"l{l}_att_bwd"
        if l > 0:
            return f"l{l}_proj_dx", f"l{l - 1}_mlp_down_dw", f"l{l - 1}_mlp_up_dx"
        return None, "l0_proj_dx", None

    def reduce_behind(l, name, full):
        calls = reduce_calls(name, l)

        def swapped(outs):
            pair = _pair_add(full[:, None], outs[0][:, None], ci, f"reduce_pair_add_{name}{l}")[:, 0]
            plan.at(calls[1], _phase_chip_scatter(pair, scattered))

        def scattered(outs):
            half = _sum_chips(outs[0][:, None], f"reduce_sum_chips_{name}{l}")[0]
            phase = _phase_pair_allgather(half, l, depth, reduced.get(name), gathered)
            if calls[2] is None:
                last_swap.append(phase)
            else:
                plan.at(calls[2], phase)

        def gathered(outs):
            reduced[name] = outs[0]

        if calls[0] is None:
            _comm_only([_phase_pair_exchange(full, swapped)], f"reduce_pair_exchange_{name}{l}")
        else:
            plan.at(calls[0], _phase_pair_exchange(full, swapped))

    loss_part, grad_x, grads, g_final = _local_step(x[0], loss_target[0], layers, norm_final[None], reduce_behind)

    G, delta, new_m, new_v = {}, {}, {}, {}
    early = ("w_down", "w_up", "w_out")
    for name in early:
        G[name] = reduced[name]
    ds, nms, nvs = _adamw([P[k] for k in early], [G[k] for k in early], [Mo[k] for k in early], [Vo[k] for k in early],
                          "adamw_early")
    for i, name in enumerate(early):
        delta[name], new_m[name], new_v[name] = ds[i], nms[i], nvs[i]
    _PLAN[0] = None
    assert not plan.by_call, sorted(plan.by_call)

    small_g = []
    for l in range(depth):
        g = grads[l]
        small_g += [g["norm_mix"], g["wg"][0:16, 0:192], g["bg"][:, 0:192], g["gla_norm"], g["b_dw"], g["ln_g"], g["ln_b"],
                    g["rel_bias"], g["norm_ffn"], g["w_dw"][0:CK]]
    small_g += [g_final, loss_part]
    small_shapes = [a.shape for a in small_g]
    n_small = sum(int(np.prod(s)) for s in small_shapes)
    rows = -(-n_small // 1024) * 8
    red = _unpack(_allreduce_small(_pack(small_g, rows)), small_shapes)
    per = len(SMALL) + 1
    for i, name in enumerate(SMALL):
        G[name] = jnp.stack([red[l * per + i].reshape(P[name].shape[1:]) for l in range(depth)])
    gw_dw_all = jnp.stack([red[l * per + len(SMALL)] for l in range(depth)])
    G["w_dw"] = lax.dynamic_slice_in_dim(gw_dw_all, chip * (CW // 4), CW // 4, axis=2)
    G["norm_final"] = red[-2].reshape(norm_final.shape)
    loss = red[-1][0, 0]

    _comm_only(last_swap, "reduce_pair_allgather_last")
    back = lambda a: jnp.transpose(a, (1, 2, 0))
    g_in_t = jnp.transpose(reduced["w_in"][:, 0:W_IN_SHARD, :], (1, 0, 2))
    d_in, nm_in, nv_in = _adamw_slabs(w_in_t, g_in_t, m_w_in_t, v_w_in_t, "adamw_w_in")
    G["w_in"], delta["w_in"], new_m["w_in"], new_v["w_in"] = back(g_in_t), back(d_in), back(nm_in), back(nv_in)

    small_names = list(SMALL) + ["w_dw", "norm_final"]
    two_d = lambda a: a.reshape(-1, a.shape[-1])
    ds, nms, nvs = _adamw_small([two_d(P[k]) for k in small_names], [two_d(G[k]) for k in small_names],
                                [two_d(Mo[k]) for k in small_names], [two_d(Vo[k]) for k in small_names])
    for i, name in enumerate(small_names):
        shp = P[name].shape
        delta[name], new_m[name], new_v[name] = ds[i].reshape(shp), nms[i].reshape(shp), nvs[i].reshape(shp)

    order = ["norm_mix", "w_in", "w_gla_gate", "b_gla_gate", "gla_norm", "w_dw", "b_dw", "conv_ln_g", "conv_ln_b", "rel_bias",
             "w_out", "norm_ffn", "w_up", "w_down", "norm_final"]
    return (loss, grad_x[None], *[G[k] for k in order], *[delta[k] for k in order], *[new_m[k] for k in order],
            *[new_v[k] for k in order])
```

```python
import functools

import numpy as np
import jax
import jax.numpy as jnp
from jax import lax
from jax.experimental import pallas as pl
from jax.experimental.pallas import tpu as pltpu

F32 = jnp.float32
BF16 = jnp.bfloat16
HI = lax.Precision.HIGHEST

D = 1024
CHUNK = 64
GLA_DK, GLA_DV, GLA_H = 48, 96, 4
KW = 256
VW = 384
LRW = 128
GLA_TAU = 16.0
CW = 256
CK = 31
AW = 384
AH = 6
BAND = 576
LEFT = 512
D_FF = 4096
EPS = 1e-6
NEG = -1e30
N_REL = 257

GLA_COLS = 2 * KW + 2 * VW + LRW
CONV_COLS = 2 * CW
ATT_COLS = 3 * AW

ADAM_LR, ADAM_B1, ADAM_B2, ADAM_EPS, ADAM_WD, ADAM_STEP = 0.001, 0.9, 0.999, 1e-08, 0.01, 10

VMEM_CAP = 56 * 1024 * 1024
MESH = pl.DeviceIdType.MESH


def _cparams(sem, vmem_bytes):
    limit = int(min(VMEM_CAP, max(vmem_bytes * 5 // 4 + (4 << 20), 16 << 20)))
    return pltpu.CompilerParams(dimension_semantics=sem, vmem_limit_bytes=limit)


def _out(shape, dtype):
    return pltpu.HBM(tuple(shape), dtype)


class _Comm:
    def __init__(self, ins, outs, aliases, sems, start, finish, then=None):
        self.ins, self.outs, self.aliases, self.sems = list(ins), list(outs), dict(aliases), list(sems)
        self.start, self.finish, self.then = start, finish, then


class _Plan:
    def __init__(self):
        self.by_call = {}

    def at(self, call, comm):
        self.by_call.setdefault(call, []).append(comm)

    def take(self, call):
        return self.by_call.pop(call, [])


_PLAN = [None]


def _pin(a):
    return pltpu.with_memory_space_constraint(a, pltpu.HBM) if jnp.issubdtype(a.dtype, jnp.floating) else a


def _pallas(body, **kw):
    comms = _PLAN[0].take(kw.get("name")) if _PLAN[0] is not None else []
    if not comms:
        call = pl.pallas_call(body, **kw)
        return lambda *args: call(*[_pin(a) for a in args])

    grid = tuple(kw.get("grid", ()))
    single = not isinstance(kw["out_shape"], (tuple, list))
    out_shape = [kw["out_shape"]] if single else list(kw["out_shape"])
    out_specs = [kw["out_specs"]] if single else list(kw["out_specs"])
    in_specs = list(kw["in_specs"])
    scratch = list(kw.get("scratch_shapes", ()))
    n_in, n_out, n_scr = len(in_specs), len(out_shape), len(scratch)
    c_in = sum(len(c.ins) for c in comms)
    c_out = sum(len(c.outs) for c in comms)
    aliases = dict(kw.get("input_output_aliases", {}))
    i0, o0 = n_in, n_out
    for c in comms:
        for i, o in c.aliases.items():
            aliases[i0 + i] = o0 + o
        i0 += len(c.ins)
        o0 += len(c.outs)

    def wrapped(*refs):
        ins, c_ins = refs[:n_in], refs[n_in:n_in + c_in]
        outs, c_outs = refs[n_in + c_in:n_in + c_in + n_out], refs[n_in + c_in + n_out:n_in + c_in + n_out + c_out]
        scr, c_sems = refs[n_in + c_in + n_out + c_out:][:n_scr], refs[n_in + c_in + n_out + c_out + n_scr:]

        def each(what):
            i0 = o0 = s0 = 0
            for c in comms:
                getattr(c, what)(c_ins[i0:i0 + len(c.ins)], c_outs[o0:o0 + len(c.outs)], c_sems[s0:s0 + len(c.sems)])
                i0, o0, s0 = i0 + len(c.ins), o0 + len(c.outs), s0 + len(c.sems)

        if grid:
            first = functools.reduce(jnp.logical_and, [pl.program_id(a) == 0 for a in range(len(grid))])
            last = functools.reduce(jnp.logical_and, [pl.program_id(a) == grid[a] - 1 for a in range(len(grid))])
            pl.when(first)(lambda: each("start"))
            body(*ins, *outs, *scr)
            pl.when(last)(lambda: each("finish"))
        else:
            each("start")
            body(*ins, *outs, *scr)
            each("finish")

    kw = dict(kw)
    kw["in_specs"] = in_specs + [ANY] * c_in
    kw["out_shape"] = out_shape + [_out(s.shape, s.dtype) for c in comms for s in c.outs]
    kw["out_specs"] = out_specs + [ANY] * c_out
    staging = [s for c in comms for s in c.sems if len(s) == 2 and not isinstance(s[1], int)]
    kw["scratch_shapes"] = scratch + [pltpu.VMEM(*s) if s in staging else pltpu.SemaphoreType.DMA(s) for c in comms for s in c.sems]
    kw["input_output_aliases"] = aliases
    extra = sum(_nbytes(*s) for s in staging)
    old = kw.get("compiler_params")
    limit = (old.vmem_limit_bytes if old is not None else 16 << 20) + extra
    kw["compiler_params"] = pltpu.CompilerParams(
        dimension_semantics=old.dimension_semantics if old is not None else None, vmem_limit_bytes=int(min(VMEM_CAP, limit)))
    call = pl.pallas_call(wrapped, **kw)

    def run(*args):
        res = call(*[_pin(a) for a in args], *[_pin(a) for c in comms for a in c.ins])
        o0 = n_out
        for c in comms:
            if c.then is not None:
                c.then(res[o0:o0 + len(c.outs)])
            o0 += len(c.outs)
        return res[0] if single else res[:n_out]

    return run


def _nbytes(shape, dtype):
    return int(np.prod(shape)) * jnp.dtype(dtype).itemsize


def _sigmoid(x):
    return 1.0 / (1.0 + jnp.exp(-x))


_DIMS = {"nn": (((1,), (0,)), ((), ())), "nt": (((1,), (1,)), ((), ())), "tn": (((0,), (0,)), ((), ()))}


def _mm(a, b, *, mode, out_dtype, name, tm=512, tn=None, tk=None, a_pro=None, epi=None, extra=None,
        b_chips=False, out_chips=False, norm_g=None, norm_bwd=None):
    b2 = (b.shape[1], 4 * b.shape[2]) if b_chips else b.shape
    if mode == "nn":
        (M, K), (K2, N) = a.shape, b2
    elif mode == "nt":
        (M, K), (N, K2) = a.shape, b2
    else:
        (K, M), (K2, N) = a.shape, b2
    assert K == K2, (a.shape, b.shape, mode)
    tm = min(tm, M)
    tn = N if tn is None else min(tn, N)
    tk = K if tk is None else min(tk, K)
    assert M % tm == 0 and N % tn == 0 and K % tk == 0, (M, N, K, tm, tn, tk)
    nk = K // tk
    a_blk = (tk, tm) if mode == "tn" else (tm, tk)
    a_map = (lambda i, j, k: (k, i)) if mode == "tn" else (lambda i, j, k: (i, k))
    b_blk = (tn, tk) if mode == "nt" else (tk, tn)
    b_map = (lambda i, j, k: (j, k)) if mode == "nt" else (lambda i, j, k: (k, j))
    if b_chips:
        per = b.shape[2] // b_blk[1]
        assert b.shape[2] % b_blk[1] == 0 and mode != "tn"
        flat_map = b_map
        b_map = lambda i, j, k: (flat_map(i, j, k)[1] // per, flat_map(i, j, k)[0], flat_map(i, j, k)[1] % per)
        b_blk = (None,) + b_blk
    in_specs = [pl.BlockSpec(a_blk, a_map), pl.BlockSpec(b_blk, b_map)]
    args = [a, b]
    if epi is not None:
        in_specs.append(pl.BlockSpec((tm, tn), lambda i, j, k: (i, j)))
        args.append(extra)
    if norm_g is not None:
        assert nk == 1 and mode != "tn"
        in_specs.append(pl.BlockSpec((1, K), lambda i, j, k: (0, 0)))
        args.append(norm_g)
    if norm_bwd is not None:
        assert tn == N
        row = pl.BlockSpec((tm, N), lambda i, j, k: (i, 0))
        in_specs += [row, pl.BlockSpec((1, N), lambda i, j, k: (0, 0)), row]
        args += list(norm_bwd)

    def body(*refs):
        it = iter(refs)
        a_ref, b_ref = next(it), next(it)
        e_ref = next(it) if epi is not None else None
        ng_ref = next(it) if norm_g is not None else None
        h_ref, g_ref, dres_ref = (next(it), next(it), next(it)) if norm_bwd is not None else (None, None, None)
        o_ref = next(it)
        xn_ref = next(it) if norm_g is not None else None
        dg_ref = next(it) if norm_bwd is not None else None
        if norm_bwd is not None:
            @pl.when((pl.program_id(0) == 0) & (pl.program_id(2) == 0))
            def _():
                dg_ref[...] = jnp.zeros_like(dg_ref)

        av = a_ref[...]
        if a_pro == "relu2":
            af = jnp.maximum(av.astype(F32), 0.0)
            av = af * af
        if norm_g is not None:
            av = (av * lax.rsqrt(jnp.mean(av * av, axis=-1, keepdims=True) + EPS) * ng_ref[...]).astype(BF16)
            xn_ref[...] = av
        p = lax.dot_general(av.astype(BF16), b_ref[...].astype(BF16), _DIMS[mode], preferred_element_type=F32)

        def finish(acc):
            if epi == "add":
                acc = acc + e_ref[...].astype(F32)
            elif epi == "relu2grad":
                acc = acc * (2.0 * jnp.maximum(e_ref[...].astype(F32), 0.0))
            if norm_bwd is None:
                o_ref[...] = acc.astype(o_ref.dtype)
                return

            x = h_ref[...]
            r = lax.rsqrt(jnp.mean(x * x, axis=-1, keepdims=True) + EPS)
            gy = acc * g_ref[...]
            dot = jnp.mean(x * gy, axis=-1, keepdims=True)
            o_ref[...] = dres_ref[...] + r * gy - x * (r * r * r * dot)
            dg_ref[...] += jnp.sum(acc * x * r, axis=0, keepdims=True)

        if nk == 1:
            finish(p)
        else:
            acc_ref = refs[-1]
            k = pl.program_id(2)

            @pl.when(k == 0)
            def _():
                acc_ref[...] = p

            @pl.when(k > 0)
            def _():
                acc_ref[...] += p

            @pl.when(k == nk - 1)
            def _():
                finish(acc_ref[...])

    vm = 2 * (_nbytes(a_blk, a.dtype) + _nbytes((tk, tn), b.dtype) + _nbytes((tm, tn), out_dtype))
    vm += 3 * _nbytes((tm, tn), F32)
    if epi is not None:
        vm += 2 * _nbytes((tm, tn), extra.dtype)
    if out_chips:
        per_out = N // 4 // tn
        assert N % (4 * tn) == 0
        out_shape = _out((4, M, N // 4), out_dtype)
        out_spec = pl.BlockSpec((None, tm, tn), lambda i, j, k: (j // per_out, i, j % per_out))
    else:
        out_shape = _out((M, N), out_dtype)
        out_spec = pl.BlockSpec((tm, tn), lambda i, j, k: (i, j))
    sem = ("parallel", "parallel", "arbitrary")
    if norm_g is not None:
        out_shape, out_spec = (out_shape, _out((M, K), BF16)), (out_spec, pl.BlockSpec((tm, K), lambda i, j, k: (i, 0)))
        vm += 4 * _nbytes((tm, K), F32)
    if norm_bwd is not None:
        out_shape, out_spec = (out_shape, _out((1, N), F32)), (out_spec, pl.BlockSpec((1, N), lambda i, j, k: (0, 0)))
        sem = ("arbitrary", "arbitrary", "arbitrary")
        vm += 8 * _nbytes((tm, N), F32)
    return _pallas(
        body,
        out_shape=out_shape,
        grid=(M // tm, N // tn, nk),
        in_specs=in_specs,
        out_specs=out_spec,
        scratch_shapes=[pltpu.VMEM((tm, tn), F32)] if nk > 1 else [],
        compiler_params=_cparams(sem, vm),
        name=name,
    )(*args)


def _mm_fan(a, bs, *, mode, out_dtypes, name, tm=512, norm_g=None):
    M, K = a.shape
    ns = [b.shape[1] if mode == "nn" else b.shape[0] for b in bs]
    n = len(bs)
    first_out = 1 + n + (norm_g is not None)

    def body(*refs):
        if norm_g is None:
            av = refs[0][...].astype(BF16)
        else:
            x = refs[0][...]
            av = (x * lax.rsqrt(jnp.mean(x * x, axis=-1, keepdims=True) + EPS) * refs[1 + n][...]).astype(BF16)
            refs[first_out + n][...] = av
        for i in range(n):
            refs[first_out + i][...] = lax.dot_general(av, refs[1 + i][...], _DIMS[mode],
                                                       preferred_element_type=F32).astype(refs[first_out + i].dtype)

    vm = 4 * _nbytes((tm, K), F32) + sum(2 * _nbytes(b.shape, b.dtype) + 3 * _nbytes((tm, nn), F32) for b, nn in zip(bs, ns))
    in_specs = [pl.BlockSpec((tm, K), lambda i: (i, 0))] + [pl.BlockSpec(b.shape, lambda i: (0, 0)) for b in bs]
    out_shape = [_out((M, nn), dt) for nn, dt in zip(ns, out_dtypes)]
    out_specs = [pl.BlockSpec((tm, nn), lambda i: (i, 0)) for nn in ns]
    args = [a, *bs]
    if norm_g is not None:
        in_specs.append(pl.BlockSpec((1, K), lambda i: (0, 0)))
        out_shape.append(_out((M, K), BF16))
        out_specs.append(pl.BlockSpec((tm, K), lambda i: (i, 0)))
        args.append(norm_g)
    return _pallas(
        body,
        out_shape=tuple(out_shape),
        grid=(M // tm,),
        in_specs=in_specs,
        out_specs=tuple(out_specs),
        compiler_params=_cparams(("parallel",), vm),
        name=name,
    )(*args)


def _mm_sum(as_, bs, *, mode, out_dtype, name, extra=None, norm_bwd=None, tm=512):
    M = as_[0].shape[0]
    N = bs[0].shape[1] if mode == "nn" else bs[0].shape[0]
    n = len(as_)

    def body(*refs):
        acc = None
        for i in range(n):
            p = lax.dot_general(refs[i][...].astype(BF16), refs[n + i][...], _DIMS[mode], preferred_element_type=F32)
            acc = p if acc is None else acc + p
        if extra is not None:
            acc = acc + refs[2 * n][...].astype(F32)
        if norm_bwd is None:
            refs[-1][...] = acc.astype(refs[-1].dtype)
        else:
            h_ref, g_ref, dres_ref, dh_ref, dg_ref = refs[-5:]

            @pl.when(pl.program_id(0) == 0)
            def _():
                dg_ref[...] = jnp.zeros_like(dg_ref)

            x = h_ref[...]
            r = lax.rsqrt(jnp.mean(x * x, axis=-1, keepdims=True) + EPS)
            gy = acc * g_ref[...]
            dot = jnp.mean(x * gy, axis=-1, keepdims=True)
            dh_ref[...] = dres_ref[...] + r * gy - x * (r * r * r * dot)
            dg_ref[...] += jnp.sum(acc * x * r, axis=0, keepdims=True)

    row = pl.BlockSpec((tm, N), lambda i: (i, 0))
    in_specs = [pl.BlockSpec((tm, a.shape[1]), lambda i: (i, 0)) for a in as_]
    in_specs += [pl.BlockSpec(b.shape, lambda i: (0, 0)) for b in bs]
    args = list(as_) + list(bs)
    if extra is not None:
        in_specs.append(row)
        args.append(extra)
    vm = sum(2 * _nbytes((tm, a.shape[1]), a.dtype) for a in as_) + sum(2 * _nbytes(b.shape, b.dtype) for b in bs)
    vm += 8 * _nbytes((tm, N), F32)
    if norm_bwd is None:
        out_shape, out_specs, sem = _out((M, N), out_dtype), row, "parallel"
    else:
        vec = pl.BlockSpec((1, N), lambda i: (0, 0))
        in_specs += [row, vec, row]
        args += list(norm_bwd)
        out_shape, out_specs, sem = (_out((M, N), F32), _out((1, N), F32)), (row, vec), "arbitrary"
        vm += 8 * _nbytes((tm, N), F32)
    return _pallas(
        body,
        out_shape=out_shape,
        grid=(M // tm,),
        in_specs=in_specs,
        out_specs=out_specs,
        compiler_params=_cparams((sem,), vm),
        name=name,
    )(*args)


def _mm_tn_multi(ops, pairs, *, name, tk=1024):
    T = ops[0].shape[0]
    tk = min(tk, T)
    n, m = len(ops), len(pairs)
    shapes = [(ops[a].shape[1], ops[b].shape[1]) for a, b in pairs]

    def body(*refs):
        vals = [refs[i][...].astype(BF16) for i in range(n)]
        first = pl.program_id(0) == 0
        for j, (a, b) in enumerate(pairs):
            p = lax.dot_general(vals[a], vals[b], _DIMS["tn"], preferred_element_type=F32)
            o_ref = refs[n + j]

            @pl.when(first)
            def _(o_ref=o_ref, p=p):
                o_ref[...] = p

            @pl.when(jnp.logical_not(first))
            def _(o_ref=o_ref, p=p):
                o_ref[...] += p

    vm = sum(2 * _nbytes((tk, o.shape[1]), o.dtype) for o in ops) + sum(3 * _nbytes(s, F32) for s in shapes)
    return _pallas(
        body,
        out_shape=tuple(_out(s, F32) for s in shapes),
        grid=(T // tk,),
        in_specs=[pl.BlockSpec((tk, o.shape[1]), lambda k: (k, 0)) for o in ops],
        out_specs=tuple(pl.BlockSpec(s, lambda k: (0, 0)) for s in shapes),
        compiler_params=_cparams(("arbitrary",), vm),
        name=name,
    )(*ops)


def _final_loss(h, g, target, name, tm=512):
    T = h.shape[0]

    def body(h_ref, g_ref, t_ref, loss_ref, dh_ref, dg_ref):
        @pl.when(pl.program_id(0) == 0)
        def _():
            dg_ref[...] = jnp.zeros_like(dg_ref)
            loss_ref[...] = jnp.zeros_like(loss_ref)

        x = h_ref[...]
        gg = g_ref[...]
        r = lax.rsqrt(jnp.mean(x * x, axis=-1, keepdims=True) + EPS)
        y = x * r * gg
        e = y - t_ref[...]
        loss_ref[...] += 0.5 * jnp.sum(jnp.mean(e * e, axis=-1, keepdims=True), axis=0, keepdims=True)
        dy = e * (1.0 / D)
        gy = dy * gg
        dot = jnp.mean(x * gy, axis=-1, keepdims=True)
        dh_ref[...] = r * gy - x * (r * r * r * dot)
        dg_ref[...] += jnp.sum(dy * x * r, axis=0, keepdims=True)

    row = pl.BlockSpec((tm, D), lambda i: (i, 0))
    vec = pl.BlockSpec((1, D), lambda i: (0, 0))
    one = pl.BlockSpec((1, 1), lambda i: (0, 0))
    return _pallas(
        body,
        out_shape=(_out((1, 1), F32), _out((T, D), F32), _out((1, D), F32)),
        grid=(T // tm,),
        in_specs=[row, vec, row],
        out_specs=(one, row, vec),
        compiler_params=_cparams(("arbitrary",), 12 * _nbytes((tm, D), F32)),
        name=name,
    )(h, g, target)


GLA_G = 8


def _gla_consts():
    i = np.arange(KW)[:, None]
    j = np.arange(VW)[None, :]
    mask = ((i // GLA_DK) == (j // GLA_DV)) & (i < GLA_H * GLA_DK)
    a = np.arange(VW)
    hm = ((a[:, None] // GLA_DV) == (a[None, :] // GLA_DV)).astype(np.float32)
    c = np.arange(CHUNK)
    low = (c[:, None] >= c[None, :]).astype(np.float32)
    return jnp.asarray(mask.astype(np.float32)), jnp.asarray(hm, BF16), jnp.asarray(low, BF16)


def _split(x):
    hi = x.astype(BF16)
    return hi, (x - hi.astype(F32)).astype(BF16)


def _dot_sel(a, b, dims, split):
    if split == "a":
        hi, lo = _split(a)
        return (lax.dot_general(hi, b, dims, preferred_element_type=F32) + lax.dot_general(lo, b, dims, preferred_element_type=F32))
    hi, lo = _split(b)
    return (lax.dot_general(a, hi, dims, preferred_element_type=F32) + lax.dot_general(a, lo, dims, preferred_element_type=F32))


def _dot3(a, b, dims):
    ah, al = _split(a)
    bh, bl = _split(b)
    return (lax.dot_general(ah, bh, dims, preferred_element_type=F32) + lax.dot_general(al, bh, dims, preferred_element_type=F32)
            + lax.dot_general(ah, bl, dims, preferred_element_type=F32))


def _dot3s(a_s, b_s, dims):
    (ah, al), (bh, bl) = a_s, b_s
    return (lax.dot_general(ah, bh, dims, preferred_element_type=F32) + lax.dot_general(al, bh, dims, preferred_element_type=F32)
            + lax.dot_general(ah, bl, dims, preferred_element_type=F32))


def _gla_group_common(p_ref, wg, bg):
    lr_s = _split(p_ref[:, 2 * KW + 2 * VW:GLA_COLS])
    wg_s = _split(wg)
    z = _dot3s(lr_s, wg_s, _DIMS["nn"]) + bg
    la = (jnp.minimum(z, 0.0) - jnp.log(1.0 + jnp.exp(-jnp.abs(z)))) * (1.0 / GLA_TAU)
    return lr_s, wg_s, z, _split(la)


def _gla_chunk_common(p_ref, rows, la_s, low, ones_v):
    q = p_ref[rows, 0:KW]
    k = p_ref[rows, KW:2 * KW]
    v = p_ref[rows, 2 * KW:2 * KW + VW]
    g = p_ref[rows, 2 * KW + VW:2 * KW + 2 * VW]
    la_h, la_l = la_s[0][rows], la_s[1][rows]
    cum = jnp.dot(low, la_h, preferred_element_type=F32) + jnp.dot(low, la_l, preferred_element_type=F32)
    endb = cum[CHUNK - 1:CHUNK, :]
    w = jnp.exp(endb - cum)
    a_full = jnp.exp(lax.dot_general(la_h, ones_v, _DIMS["tn"], preferred_element_type=F32)
                     + lax.dot_general(la_l, ones_v, _DIMS["tn"], preferred_element_type=F32))
    return q, k, v, g, w, endb, a_full


def _gla_fwd(p, wg, bg, gn, consts, name):
    T = p.shape[0]
    rb = CHUNK * GLA_G
    ng = T // rb
    mask, hm, low = consts[:3]
    scale = GLA_DK ** -0.5

    def body(p_ref, wg_ref, bg_ref, gn_ref, m_ref, hm_ref, l_ref, o_ref, st_ref, s_ref):
        @pl.when(pl.program_id(0) == 0)
        def _():
            s_ref[...] = jnp.zeros_like(s_ref)

        wg_v, bg_v, gn_v = wg_ref[...], bg_ref[...], gn_ref[...]
        ones_v = jnp.ones((CHUNK, VW), BF16)
        s_new = s_ref[...]
        _, _, _, la_s = _gla_group_common(p_ref, wg_v, bg_v)
        outs = []
        for c in range(GLA_G):
            rows = slice(c * CHUNK, (c + 1) * CHUNK)
            q, k, v, _, w, _, a_full = _gla_chunk_common(p_ref, rows, la_s, l_ref[...], ones_v)
            kd = (k * w).astype(BF16)
            kv = lax.dot_general(kd, v.astype(BF16), _DIMS["tn"], preferred_element_type=F32) * m_ref[...]
            s_new = a_full * s_new + kv
            st_ref[c] = s_new
            outs.append(jnp.dot((q * scale).astype(BF16), s_new.astype(BF16), preferred_element_type=F32))
        s_ref[...] = s_new
        o = jnp.concatenate(outs, axis=0)
        g = p_ref[:, 2 * KW + VW:2 * KW + 2 * VW]
        ms = _dot_sel(o * o, hm_ref[...], _DIMS["nn"], "a") * (1.0 / GLA_DV)
        o_ref[...] = (o * lax.rsqrt(ms + EPS) * gn_v * (g * _sigmoid(g))).astype(o_ref.dtype)

    full = lambda shape: pl.BlockSpec(shape, lambda i: tuple(0 for _ in shape))
    vm = 2 * _nbytes((rb, GLA_COLS), F32) + 2 * _nbytes((GLA_G, KW, VW), F32) + 12 * _nbytes((KW, VW), F32)
    return _pallas(
        body,
        out_shape=(_out((T, VW), BF16), _out((T // CHUNK, KW, VW), F32)),
        grid=(ng,),
        in_specs=[pl.BlockSpec((rb, GLA_COLS), lambda i: (i, 0)), full((LRW, KW)), full((1, KW)), full((1, VW)),
                  full((KW, VW)), full((VW, VW)), full((CHUNK, CHUNK))],
        out_specs=(pl.BlockSpec((rb, VW), lambda i: (i, 0)), pl.BlockSpec((GLA_G, KW, VW), lambda i: (i, 0, 0))),
        scratch_shapes=[pltpu.VMEM((KW, VW), F32)],
        compiler_params=_cparams(("arbitrary",), vm),
        name=name,
    )(p, wg, bg, gn, mask, hm, low)


def _gla_bwd(p, dy, states, wg, bg, gn, consts, name):
    T = p.shape[0]
    rb = CHUNK * GLA_G
    ng = T // rb
    mask, hm, low = consts[:3]
    scale = GLA_DK ** -0.5

    def body(p_ref, dy_ref, st_ref, sp_ref, wg_ref, bg_ref, gn_ref, m_ref, hm_ref, l_ref,
             dp_ref, dwg_ref, dbg_ref, dgn_ref, ga_ref):
        step = pl.program_id(0)

        @pl.when(step == 0)
        def _():
            ga_ref[...] = jnp.zeros_like(ga_ref)
            dwg_ref[...] = jnp.zeros_like(dwg_ref)
            dbg_ref[...] = jnp.zeros_like(dbg_ref)
            dgn_ref[...] = jnp.zeros_like(dgn_ref)

        first_group = step == ng - 1
        wg_v, bg_v, gn_v = wg_ref[...], bg_ref[...], gn_ref[...]
        ones_v = jnp.ones((CHUNK, VW), BF16)
        ones_8 = jnp.ones((8, VW), BF16)
        ga = ga_ref[...]
        lr_s, wg_s, z_all, la_s = _gla_group_common(p_ref, wg_v, bg_v)
        qss = [(p_ref[c * CHUNK:(c + 1) * CHUNK, 0:KW] * scale).astype(BF16) for c in range(GLA_G)]
        o = jnp.concatenate([jnp.dot(qss[c], st_ref[c].astype(BF16), preferred_element_type=F32) for c in range(GLA_G)], axis=0)
        g = p_ref[:, 2 * KW + VW:2 * KW + 2 * VW]
        dyv = dy_ref[...].astype(F32)
        r = lax.rsqrt(_dot_sel(o * o, hm_ref[...], _DIMS["nn"], "a") * (1.0 / GLA_DV) + EPS)
        on = o * r
        sg = _sigmoid(g)
        silu = g * sg
        d_on = dyv * gn_v * silu
        dp_ref[:, 2 * KW + VW:2 * KW + 2 * VW] = (dyv * on * gn_v * (sg * (1.0 + g * (1.0 - sg)))).astype(dp_ref.dtype)
        dgn_ref[...] += jnp.sum(dyv * on * silu, axis=0, keepdims=True)
        mo = _dot_sel(o * d_on, hm_ref[...], _DIMS["nn"], "a") * (1.0 / GLA_DV)
        dob_all = (r * d_on - o * (r * r * r) * mo).astype(BF16)
        dzs = [None] * GLA_G
        for c in reversed(range(GLA_G)):
            rows = slice(c * CHUNK, (c + 1) * CHUNK)
            _, k, v, _, w, endb, a_full = _gla_chunk_common(p_ref, rows, la_s, l_ref[...], ones_v)
            s_n = st_ref[c]
            if c > 0:
                s_prev = st_ref[c - 1]
            else:
                s_prev = jnp.where(first_group, 0.0, sp_ref[0])
            kd = k * w
            dob = dob_all[rows]
            dq = lax.dot_general(dob, s_n.astype(BF16), _DIMS["nt"], preferred_element_type=F32) * scale
            g_n = lax.dot_general(qss[c], dob, _DIMS["tn"], preferred_element_type=F32) * m_ref[...] + ga
            d_a = _dot_sel(ones_8, g_n * s_prev, _DIMS["nt"], "b")[0:1, :]
            g_nb = g_n.astype(BF16)
            dkd = lax.dot_general(v.astype(BF16), g_nb, _DIMS["nt"], preferred_element_type=F32)
            dv = jnp.dot(kd.astype(BF16), g_nb, preferred_element_type=F32)
            e = dkd * kd
            d_end = jnp.sum(e, axis=0, keepdims=True) + d_a * jnp.exp(endb)
            dla = _dot_sel(l_ref[...], -e, _DIMS["tn"], "b") + d_end
            dzs[c] = dla * (1.0 - _sigmoid(z_all[rows])) * (1.0 / GLA_TAU)
            ga = a_full * g_n
            dp_ref[rows, 0:KW] = dq.astype(dp_ref.dtype)
            dp_ref[rows, KW:2 * KW] = (dkd * w).astype(dp_ref.dtype)
            dp_ref[rows, 2 * KW:2 * KW + VW] = dv.astype(dp_ref.dtype)
        ga_ref[...] = ga
        dz = jnp.concatenate(dzs, axis=0)
        dz_s = _split(dz)
        dp_ref[:, 2 * KW + 2 * VW:GLA_COLS] = _dot3s(dz_s, wg_s, _DIMS["nt"]).astype(dp_ref.dtype)
        dwg_ref[...] += _dot3s(lr_s, dz_s, _DIMS["tn"])
        dbg_ref[...] += jnp.sum(dz, axis=0, keepdims=True)

    full = lambda shape: pl.BlockSpec(shape, lambda i: tuple(0 for _ in shape))
    rev = lambda i: (ng - 1 - i, 0)
    vm = 4 * _nbytes((rb, GLA_COLS), F32) + 2 * _nbytes((rb, VW), F32) + 2 * _nbytes((GLA_G + 1, KW, VW), F32)
    vm += 16 * _nbytes((KW, VW), F32)
    return _pallas(
        body,
        out_shape=(_out((T, GLA_COLS), BF16), _out((LRW, KW), F32),
                   _out((1, KW), F32), _out((1, VW), F32)),
        grid=(ng,),
        in_specs=[pl.BlockSpec((rb, GLA_COLS), rev), pl.BlockSpec((rb, VW), rev),
                  pl.BlockSpec((GLA_G, KW, VW), lambda i: (ng - 1 - i, 0, 0)),
                  pl.BlockSpec((1, KW, VW), lambda i: (jnp.maximum((ng - 1 - i) * GLA_G - 1, 0), 0, 0)),
                  full((LRW, KW)), full((1, KW)), full((1, VW)), full((KW, VW)), full((VW, VW)), full((CHUNK, CHUNK))],
        out_specs=(pl.BlockSpec((rb, GLA_COLS), rev), full((LRW, KW)), full((1, KW)), full((1, VW))),
        scratch_shapes=[pltpu.VMEM((KW, VW), F32)],
        compiler_params=_cparams(("arbitrary",), vm),
        name=name,
    )(p, dy, states, states, wg, bg, gn, mask, hm, low)


CONV_TM = 512
HALO = 32
CONV_RB = 64


def _glu(u):
    a = u[:, 0:CW]
    b = u[:, CW:2 * CW]
    return a * _sigmoid(b)


def _conv_taps(buf_ref, w_ref, rb0, first_tap):
    acc = jnp.zeros((CONV_RB, CW), F32)
    for j in range(CK):
        s = rb0 + first_tap(j)
        acc = acc + w_ref[j:j + 1, :] * buf_ref[s:s + CONV_RB, :]
    return acc


def _ln_fwd(c, lg, lb):
    mu = jnp.mean(c, axis=-1, keepdims=True)
    xc = c - mu
    rstd = lax.rsqrt(jnp.mean(xc * xc, axis=-1, keepdims=True) + EPS)
    n = xc * rstd
    return n, rstd, n * lg + lb


def _conv_fwd(u, w, b, lg, lb, name):
    T = u.shape[0]
    tm = CONV_TM

    def body(u_ref, uh_ref, w_ref, b_ref, lg_ref, lb_ref, o_ref, c_ref, hbuf):
        i = pl.program_id(0)
        hbuf[0:HALO, :] = jnp.where(i > 0, _glu(uh_ref[...]), 0.0)
        hbuf[HALO:HALO + tm, :] = _glu(u_ref[...])
        for r in range(tm // CONV_RB):
            rows = slice(r * CONV_RB, (r + 1) * CONV_RB)
            acc = _conv_taps(hbuf, w_ref, r * CONV_RB, lambda j: HALO - (CK - 1) + j)
            c_ref[rows, :] = acc
            _, _, zz = _ln_fwd(acc + b_ref[...], lg_ref[...], lb_ref[...])
            o_ref[rows, :] = (zz * _sigmoid(zz)).astype(o_ref.dtype)

    vec = pl.BlockSpec((1, CW), lambda i: (0, 0))
    return _pallas(
        body,
        out_shape=(_out((T, CW), BF16), _out((T, CW), F32)),
        grid=(T // tm,),
        in_specs=[pl.BlockSpec((tm, CONV_COLS), lambda i: (i, 0)),
                  pl.BlockSpec((HALO, CONV_COLS), lambda i: (jnp.maximum(i * (tm // HALO) - 1, 0), 0)),
                  pl.BlockSpec((HALO, CW), lambda i: (0, 0)), vec, vec, vec],
        out_specs=(pl.BlockSpec((tm, CW), lambda i: (i, 0)), pl.BlockSpec((tm, CW), lambda i: (i, 0))),
        scratch_shapes=[pltpu.VMEM((tm + HALO, CW), F32)],
        compiler_params=_cparams(("arbitrary",), 8 * _nbytes((tm, CONV_COLS), F32)),
        name=name,
    )(u, u, w, b, lg, lb)


def _conv_bwd_dc(u, conv, dout, b, lg, lb, name):
    T = u.shape[0]
    tm = CONV_TM
    nsteps = T // tm

    def body(u_ref, uh_ref, c_ref, do_ref, b_ref, lg_ref, lb_ref, dc_ref, dw_ref, db_ref, dlg_ref, dlb_ref, hbuf, dwacc):
        i = pl.program_id(0)

        @pl.when(i == 0)
        def _():
            dwacc[...] = jnp.zeros_like(dwacc)
            db_ref[...] = jnp.zeros_like(db_ref)
            dlg_ref[...] = jnp.zeros_like(dlg_ref)
            dlb_ref[...] = jnp.zeros_like(dlb_ref)

        hbuf[0:HALO, :] = jnp.where(i > 0, _glu(uh_ref[...]), 0.0)
        hbuf[HALO:HALO + tm, :] = _glu(u_ref[...])
        for r in range(tm // CONV_RB):
            rows = slice(r * CONV_RB, (r + 1) * CONV_RB)
            n, rstd, zz = _ln_fwd(c_ref[rows, :] + b_ref[...], lg_ref[...], lb_ref[...])
            sg = _sigmoid(zz)
            dz = do_ref[rows, :].astype(F32) * (sg * (1.0 + zz * (1.0 - sg)))
            dlg_ref[...] += jnp.sum(dz * n, axis=0, keepdims=True)
            dlb_ref[...] += jnp.sum(dz, axis=0, keepdims=True)
            dn = dz * lg_ref[...]
            dc = rstd * (dn - jnp.mean(dn, axis=-1, keepdims=True) - n * jnp.mean(dn * n, axis=-1, keepdims=True))
            dc_ref[rows, :] = dc
            db_ref[...] += jnp.sum(dc, axis=0, keepdims=True)
            for j in range(CK):
                s = r * CONV_RB + HALO - (CK - 1) + j
                prod = dc * hbuf[s:s + CONV_RB, :]
                dwacc[j] += jnp.sum(prod.reshape(CONV_RB // 8, 8, CW), axis=0)

        @pl.when(i == nsteps - 1)
        def _():
            dw_ref[...] = jnp.sum(dwacc[...], axis=1)

    vec = pl.BlockSpec((1, CW), lambda i: (0, 0))
    return _pallas(
        body,
        out_shape=(_out((T, CW), F32), _out((HALO, CW), F32),
                   _out((1, CW), F32), _out((1, CW), F32), _out((1, CW), F32)),
        grid=(nsteps,),
        in_specs=[pl.BlockSpec((tm, CONV_COLS), lambda i: (i, 0)),
                  pl.BlockSpec((HALO, CONV_COLS), lambda i: (jnp.maximum(i * (tm // HALO) - 1, 0), 0)),
                  pl.BlockSpec((tm, CW), lambda i: (i, 0)), pl.BlockSpec((tm, CW), lambda i: (i, 0)), vec, vec, vec],
        out_specs=(pl.BlockSpec((tm, CW), lambda i: (i, 0)), pl.BlockSpec((HALO, CW), lambda i: (0, 0)), vec, vec, vec),
        scratch_shapes=[pltpu.VMEM((tm + HALO, CW), F32), pltpu.VMEM((HALO, 8, CW), F32)],
        compiler_params=_cparams(("arbitrary",), 10 * _nbytes((tm, CONV_COLS), F32)),
        name=name,
    )(u, u, conv, dout, b, lg, lb)


def _conv_bwd_du(u, dc, w, name):
    T = u.shape[0]
    tm = CONV_TM
    nsteps = T // tm

    def body(u_ref, dc_ref, dch_ref, w_ref, du_ref, dcbuf):
        i = pl.program_id(0)
        dcbuf[0:tm, :] = dc_ref[...]
        dcbuf[tm:tm + HALO, :] = jnp.where(i < nsteps - 1, dch_ref[...], 0.0)
        for r in range(tm // CONV_RB):
            rows = slice(r * CONV_RB, (r + 1) * CONV_RB)
            dh = _conv_taps(dcbuf, w_ref, r * CONV_RB, lambda j: (CK - 1) - j)
            a = u_ref[rows, 0:CW]
            sb = _sigmoid(u_ref[rows, CW:2 * CW])
            du_ref[rows, 0:CW] = (dh * sb).astype(du_ref.dtype)
            du_ref[rows, CW:2 * CW] = (dh * a * sb * (1.0 - sb)).astype(du_ref.dtype)

    return _pallas(
        body,
        out_shape=_out((T, CONV_COLS), BF16),
        grid=(nsteps,),
        in_specs=[pl.BlockSpec((tm, CONV_COLS), lambda i: (i, 0)),
                  pl.BlockSpec((tm, CW), lambda i: (i, 0)),
                  pl.BlockSpec((HALO, CW), lambda i: (jnp.minimum((i + 1) * (tm // HALO), T // HALO - 1), 0)),
                  pl.BlockSpec((HALO, CW), lambda i: (0, 0))],
        out_specs=pl.BlockSpec((tm, CONV_COLS), lambda i: (i, 0)),
        scratch_shapes=[pltpu.VMEM((tm + HALO, CW), F32)],
        compiler_params=_cparams(("arbitrary",), 8 * _nbytes((tm, CONV_COLS), F32)),
        name=name,
    )(u, dc, dc, w)


ATT_G = 4


def _att_load_kv(p_any, kbuf, vbuf, sems, T):
    kc = pltpu.make_async_copy(p_any.at[:, pl.ds(AW, AW)], kbuf.at[pl.ds(LEFT, T), :], sems.at[0])
    vc = pltpu.make_async_copy(p_any.at[:, pl.ds(2 * AW, AW)], vbuf.at[pl.ds(LEFT, T), :], sems.at[1])
    kc.start()
    vc.start()
    kbuf[0:LEFT, :] = jnp.zeros((LEFT, AW), BF16)
    vbuf[0:LEFT, :] = jnp.zeros((LEFT, AW), BF16)
    kc.wait()
    vc.wait()


ATT_QB = CHUNK * ATT_G
ATT_KB = LEFT + ATT_QB
REL_PAD = 384
TOEP = 1024


def _att_consts():
    m = np.arange(TOEP)
    d = ATT_KB - 1 - m
    idx = np.clip(d, -128, 128) + 128
    sel = (np.arange(REL_PAD)[:, None] == idx[None, :]) & (m[None, :] < ATT_QB + ATT_KB - 1)
    return jnp.asarray(sel.astype(np.float32))


def _att_build_bias(rel_ref, sel_ref, bias_scr):
    tr = jnp.dot(rel_ref[...], sel_ref[...], precision=HI, preferred_element_type=F32)
    qc = lax.broadcasted_iota(jnp.int32, (ATT_QB, ATT_KB), 0) // CHUNK
    kc = lax.broadcasted_iota(jnp.int32, (ATT_QB, ATT_KB), 1) // CHUNK
    band = (kc >= qc) & (kc <= qc + 8)
    for h in range(AH):
        rows = jnp.broadcast_to(tr[h:h + 1, :], (ATT_QB, TOEP))
        toep = pltpu.roll(rows, TOEP - (ATT_QB - 1), 1, stride=1, stride_axis=0)[:, 0:ATT_KB]
        bias_scr[h // 2, (h % 2) * ATT_QB:(h % 2 + 1) * ATT_QB, :] = jnp.where(band, toep, NEG)


def _att_probs(qst, kb, bias_p, n0):
    sc = lax.dot_general(qst, kb, _DIMS["nt"], preferred_element_type=F32) * (64 ** -0.5) + bias_p
    pos = lax.broadcasted_iota(jnp.int32, (2 * ATT_QB, ATT_KB), 1)
    sc = jnp.where(pos >= CHUNK * (8 - n0), sc, NEG)
    mx = jnp.max(sc, axis=-1, keepdims=True)
    ex = jnp.exp(sc - mx)
    return ex / jnp.sum(ex, axis=-1, keepdims=True)


def _head_stack(a2, lo):
    zero = jnp.zeros_like(a2)
    return jnp.concatenate([jnp.where(lo, a2, zero), jnp.where(lo, zero, a2)], axis=0)


def _att_fwd(p, rel, sel, name):
    T = p.shape[0]

    def body(q_ref, p_any, rel_ref, sel_ref, o_ref, kbuf, vbuf, bias_scr, sems):
        i = pl.program_id(0)

        @pl.when(i == 0)
        def _():
            _att_load_kv(p_any, kbuf, vbuf, sems, T)
            _att_build_bias(rel_ref, sel_ref, bias_scr)

        lo = lax.broadcasted_iota(jnp.int32, (ATT_QB, 128), 1) < 64
        n0 = i * ATT_G
        start = pl.multiple_of(i * ATT_QB, ATT_QB)
        for hp in range(AH // 2):
            cols = slice(hp * 128, (hp + 1) * 128)
            kb = kbuf[pl.ds(start, ATT_KB), cols]
            vb = vbuf[pl.ds(start, ATT_KB), cols]
            pr = _att_probs(_head_stack(q_ref[:, cols], lo), kb, bias_scr[hp], n0)
            pv = jnp.dot(pr.astype(BF16), vb, preferred_element_type=F32)
            o_ref[:, cols] = jnp.where(lo, pv[0:ATT_QB], pv[ATT_QB:2 * ATT_QB]).astype(o_ref.dtype)

    vm = 2 * _nbytes((T + LEFT, AW), BF16) + 8 * _nbytes((2 * ATT_QB, ATT_KB), F32) + (8 << 20)
    return _pallas(
        body,
        out_shape=_out((T, AW), BF16),
        grid=(T // ATT_QB,),
        in_specs=[pl.BlockSpec((ATT_QB, AW), lambda i: (i, 0)), pl.BlockSpec(memory_space=pl.ANY),
                  pl.BlockSpec((8, REL_PAD), lambda i: (0, 0)), pl.BlockSpec((REL_PAD, TOEP), lambda i: (0, 0))],
        out_specs=pl.BlockSpec((ATT_QB, AW), lambda i: (i, 0)),
        scratch_shapes=[pltpu.VMEM((T + LEFT, AW), BF16), pltpu.VMEM((T + LEFT, AW), BF16),
                        pltpu.VMEM((AH // 2, 2 * ATT_QB, ATT_KB), F32), pltpu.SemaphoreType.DMA((2,))],
        compiler_params=_cparams(("arbitrary",), vm),
        name=name,
    )(p, p, rel, sel)


def _att_bwd(p, do, rel, sel, name):
    T = p.shape[0]
    nsteps = T // ATT_QB

    def body(q_ref, p_any, do_ref, rel_ref, sel_ref, dp_any, drel_ref,
             kbuf, vbuf, dqbuf, dkbuf, dvbuf, bias_scr, dbias_scr, dtr_scr, sems):
        i = pl.program_id(0)

        @pl.when(i == 0)
        def _():
            _att_load_kv(p_any, kbuf, vbuf, sems, T)
            _att_build_bias(rel_ref, sel_ref, bias_scr)
            dkbuf[...] = jnp.zeros_like(dkbuf)
            dvbuf[...] = jnp.zeros_like(dvbuf)
            dbias_scr[...] = jnp.zeros_like(dbias_scr)

        lo = lax.broadcasted_iota(jnp.int32, (ATT_QB, 128), 1) < 64
        n0 = i * ATT_G
        start = pl.multiple_of(i * ATT_QB, ATT_QB)
        for hp in range(AH // 2):
            cols = slice(hp * 128, (hp + 1) * 128)
            kb = kbuf[pl.ds(start, ATT_KB), cols]
            vb = vbuf[pl.ds(start, ATT_KB), cols]
            qst = _head_stack(q_ref[:, cols], lo)
            dost = _head_stack(do_ref[:, cols].astype(BF16), lo)
            pr = _att_probs(qst, kb, bias_scr[hp], n0)
            dpr = lax.dot_general(dost, vb, _DIMS["nt"], preferred_element_type=F32)
            ds = pr * (dpr - jnp.sum(dpr * pr, axis=-1, keepdims=True))
            dbias_scr[hp] += ds
            dsb = (ds * (64 ** -0.5)).astype(BF16)
            dq = jnp.dot(dsb, kb, preferred_element_type=F32)
            dqbuf[pl.ds(start, ATT_QB), cols] = jnp.where(lo, dq[0:ATT_QB], dq[ATT_QB:2 * ATT_QB]).astype(BF16)
            dkbuf[pl.ds(start, ATT_KB), cols] += lax.dot_general(dsb, qst, _DIMS["tn"], preferred_element_type=F32)
            dvbuf[pl.ds(start, ATT_KB), cols] += lax.dot_general(pr.astype(BF16), dost, _DIMS["tn"], preferred_element_type=F32)

        @pl.when(i == nsteps - 1)
        def _():
            kbuf[pl.ds(LEFT, T), :] = dkbuf[pl.ds(LEFT, T), :].astype(BF16)
            vbuf[pl.ds(LEFT, T), :] = dvbuf[pl.ds(LEFT, T), :].astype(BF16)
            cps = [pltpu.make_async_copy(dqbuf, dp_any.at[:, pl.ds(0, AW)], sems.at[0]),
                   pltpu.make_async_copy(kbuf.at[pl.ds(LEFT, T), :], dp_any.at[:, pl.ds(AW, AW)], sems.at[1]),
                   pltpu.make_async_copy(vbuf.at[pl.ds(LEFT, T), :], dp_any.at[:, pl.ds(2 * AW, AW)], sems.at[2])]
            for cp in cps:
                cp.start()
            dtr_scr[...] = jnp.zeros_like(dtr_scr)
            ri = lax.broadcasted_iota(jnp.int32, (ATT_QB, ATT_QB), 0)
            ci = lax.broadcasted_iota(jnp.int32, (ATT_QB, ATT_QB), 1)
            flip = jnp.where(ri + ci == ATT_QB - 1, 1.0, 0.0)
            for h in range(AH):
                db = dbias_scr[h // 2, (h % 2) * ATT_QB:(h % 2 + 1) * ATT_QB, :]
                db = jnp.dot(flip, db, precision=HI, preferred_element_type=F32)
                wide = jnp.concatenate([db, jnp.zeros((ATT_QB, TOEP - ATT_KB), F32)], axis=1)
                diag = pltpu.roll(wide, 0, 1, stride=1, stride_axis=0)
                dtr_scr[h:h + 1, :] = jnp.sum(diag, axis=0, keepdims=True)
            drel_ref[...] = lax.dot_general(dtr_scr[...], sel_ref[...], _DIMS["nt"], precision=HI, preferred_element_type=F32)
            for cp in cps:
                cp.wait()

    vm = 3 * _nbytes((T + LEFT, AW), BF16) + 2 * _nbytes((T + LEFT, AW), F32) + 12 * _nbytes((2 * ATT_QB, ATT_KB), F32) + (8 << 20)
    return _pallas(
        body,
        out_shape=(_out((T, ATT_COLS), BF16), _out((8, REL_PAD), F32)),
        grid=(nsteps,),
        in_specs=[pl.BlockSpec((ATT_QB, AW), lambda i: (i, 0)), pl.BlockSpec(memory_space=pl.ANY),
                  pl.BlockSpec((ATT_QB, AW), lambda i: (i, 0)),
                  pl.BlockSpec((8, REL_PAD), lambda i: (0, 0)), pl.BlockSpec((REL_PAD, TOEP), lambda i: (0, 0))],
        out_specs=(pl.BlockSpec(memory_space=pl.ANY), pl.BlockSpec((8, REL_PAD), lambda i: (0, 0))),
        scratch_shapes=[pltpu.VMEM((T + LEFT, AW), BF16), pltpu.VMEM((T + LEFT, AW), BF16), pltpu.VMEM((T, AW), BF16),
                        pltpu.VMEM((T + LEFT, AW), F32), pltpu.VMEM((T + LEFT, AW), F32),
                        pltpu.VMEM((AH // 2, 2 * ATT_QB, ATT_KB), F32), pltpu.VMEM((AH // 2, 2 * ATT_QB, ATT_KB), F32),
                        pltpu.VMEM((8, TOEP), F32), pltpu.SemaphoreType.DMA((3,))],
        compiler_params=_cparams(("arbitrary",), vm),
        name=name,
    )(p, p, do, rel, sel)


def _layer_fwd(h, wl, consts, tag):
    p_gla, p_conv, p_att, xn = _mm_fan(h, [wl["w_gla"], wl["w_conv"], wl["w_att"]], mode="nt", out_dtypes=(F32, F32, BF16),
                                       norm_g=wl["norm_mix"], name=f"{tag}_proj")
    o_gla, states = _gla_fwd(p_gla, wl["wg"], wl["bg"], wl["gla_norm"], consts, f"{tag}_gla_fwd")
    o_conv, conv = _conv_fwd(p_conv, wl["w_dw"], wl["b_dw"], wl["ln_g"], wl["ln_b"], f"{tag}_conv_fwd")
    rel = jnp.pad(wl["rel_bias"], ((0, 8 - AH), (0, REL_PAD - N_REL)))
    o_att = _att_fwd(p_att, rel, consts[3], f"{tag}_att_fwd")
    h1 = _mm_sum([o_gla, o_conv, o_att], [wl["w_out_g"], wl["w_out_c"], wl["w_out_a"]], mode="nn", out_dtype=F32, extra=h,
                 name=f"{tag}_out")
    u, xn2 = _mm(h1, wl["w_up"], mode="nn", out_dtype=BF16, tm=1024, tn=1024, b_chips=True, norm_g=wl["norm_ffn"],
                 name=f"{tag}_mlp_up")
    h2 = _mm(u, wl["w_down"], mode="nn", out_dtype=F32, tm=1024, tk=2048, a_pro="relu2", epi="add", extra=h1,
             name=f"{tag}_mlp_down")
    saved = dict(h=h, xn=xn, p_gla=p_gla, p_conv=p_conv, p_att=p_att, states=states, o_gla=o_gla, o_conv=o_conv, conv=conv,
                 o_att=o_att, rel=rel, h1=h1, xn2=xn2, u=u)
    return h2, saved


def _layer_bwd(dh2, sv, wl, consts, tag, emit=lambda name, grad: None):
    g = {}
    du = _mm(dh2, wl["w_down"], mode="nt", out_dtype=BF16, tm=1024, tn=1024, epi="relu2grad", extra=sv["u"],
             name=f"{tag}_mlp_down_dx")
    g["w_down"] = _mm(sv["u"], dh2, mode="tn", out_dtype=F32, tm=2048, tn=1024, tk=512, a_pro="relu2", name=f"{tag}_mlp_down_dw")
    emit("w_down", g["w_down"].reshape(4, D_FF // 4, D))
    dh1, g["norm_ffn"] = _mm(du, wl["w_up"], mode="nt", out_dtype=F32, tm=1024, tk=1024, b_chips=True,
                             norm_bwd=(sv["h1"], wl["norm_ffn"], dh2), name=f"{tag}_mlp_up_dx")
    g["w_up"] = _mm(sv["xn2"], du, mode="tn", out_dtype=F32, tm=1024, tn=1024, tk=1024, out_chips=True, name=f"{tag}_mlp_up_dw")
    emit("w_up", g["w_up"])
    d_gla, d_conv, d_att = _mm_fan(dh1, [wl["w_out_g"], wl["w_out_c"], wl["w_out_a"]], mode="nt", out_dtypes=(F32, F32, F32),
                                   name=f"{tag}_out_dx")
    g["w_out_g"], g["w_out_c"], g["w_out_a"] = _mm_tn_multi([sv["o_gla"], sv["o_conv"], sv["o_att"], dh1],
                                                            [(0, 3), (1, 3), (2, 3)], name=f"{tag}_out_dw")
    emit("w_out", jnp.concatenate([g["w_out_g"], g["w_out_c"], g["w_out_a"]], axis=0).reshape(4, D // 4, D))
    dp_gla, g["wg"], g["bg"], g["gla_norm"] = _gla_bwd(sv["p_gla"], d_gla, sv["states"], wl["wg"], wl["bg"], wl["gla_norm"],
                                                       consts, f"{tag}_gla_bwd")
    dc, g["w_dw"], g["b_dw"], g["ln_g"], g["ln_b"] = _conv_bwd_dc(sv["p_conv"], sv["conv"], d_conv, wl["b_dw"], wl["ln_g"],
                                                                  wl["ln_b"], f"{tag}_conv_bwd_dc")
    dp_conv = _conv_bwd_du(sv["p_conv"], dc, wl["w_dw"], f"{tag}_conv_bwd_du")
    dp_att, drel = _att_bwd(sv["p_att"], d_att, sv["rel"], consts[3], f"{tag}_att_bwd")
    g["rel_bias"] = drel[0:AH, 0:N_REL]
    g["w_gla"], g["w_conv"], g["w_att"] = _mm_tn_multi([sv["xn"], dp_gla, dp_conv, dp_att], [(1, 0), (2, 0), (3, 0)],
                                                       name=f"{tag}_proj_dw")
    emit("w_in", _join_w_in_t(g))
    dh, g["norm_mix"] = _mm_sum([dp_gla, dp_conv, dp_att], [wl["w_gla"], wl["w_conv"], wl["w_att"]], mode="nn", out_dtype=F32,
                                norm_bwd=(sv["h"], wl["norm_mix"], dh1), name=f"{tag}_proj_dx")
    return dh, g


def _local_step(x, target, layers, norm_final, emit=lambda layer, name, grad: None):
    consts = _gla_consts() + (_att_consts(),)
    h = x
    saved = []
    for l, wl in enumerate(layers):
        h, sv = _layer_fwd(h, wl, consts, f"l{l}")
        saved.append(sv)
    loss, dh, g_final = _final_loss(h, norm_final, target, "final_loss")
    grads = [None] * len(layers)
    for l in reversed(range(len(layers))):
        dh, grads[l] = _layer_bwd(dh, saved[l], layers[l], consts, f"l{l}", functools.partial(emit, l))
    return loss, dh, grads, g_final


ANY = pl.BlockSpec(memory_space=pl.ANY)


def _place():
    x, y, c = lax.axis_index("x"), lax.axis_index("y"), lax.axis_index("c")
    chips = [(1 - x, y), (x, 1 - y), (1 - x, 1 - y)]
    return x, y, c, chips


def _shape(shape, dtype):
    return jax.ShapeDtypeStruct(tuple(shape), dtype)


def _remote(src, dst, send_sem, recv_sem, to):
    return pltpu.make_async_remote_copy(src_ref=src, dst_ref=dst, send_sem=send_sem, recv_sem=recv_sem,
                                        device_id=to, device_id_type=MESH)


class _Staged:
    def __init__(self, src, dst, buf, sems):
        self.load = pltpu.make_async_copy(src, buf, sems.at[0])
        self.store = pltpu.make_async_copy(buf, dst, sems.at[1])

    def start(self):
        self.load.start()

    def wait(self):
        self.load.wait()
        self.store.start()
        self.store.wait()


def _phase_gather_ici(src, then):
    R, C = src.shape
    rh = R // 2

    def copies(ins, outs, sems):
        x, y, c, chips = _place()
        me = 2 * x + y
        local = _Staged(ins[0], outs[0].at[me], sems[3], sems[2])
        sends = [_remote(ins[0].at[pl.ds(c * rh, rh), :], outs[0].at[me, pl.ds(c * rh, rh), :], sems[0].at[k], sems[1].at[k],
                         (px, py, c)) for k, (px, py) in enumerate(chips)]
        recvs = [_remote(outs[0].at[2 * px + py, pl.ds(c * rh, rh), :], outs[0].at[2 * px + py, pl.ds(c * rh, rh), :],
                         sems[0].at[k], sems[1].at[k], (px, py, c)) for k, (px, py) in enumerate(chips)]
        return local, sends, recvs

    def start(ins, outs, sems):
        local, sends, _ = copies(ins, outs, sems)
        local.start()
        for cp in sends:
            cp.start()

    def finish(ins, outs, sems):
        local, sends, recvs = copies(ins, outs, sems)
        for cp in recvs:
            cp.wait_recv()
        for cp in sends:
            cp.wait_send()
        local.wait()

    return _Comm([src], [_shape((4, R, C), src.dtype)], {}, [(3,), (3,), (2,), ((R, C), src.dtype)], start, finish, then)


def _phase_gather_d2d(part, then):
    _, R, C = part.shape
    rh = R // 2

    def copies(ins, outs, sems):
        x, y, c, chips = _place()
        sends = [_remote(ins[0].at[2 * px + py, pl.ds(c * rh, rh), :], outs[0].at[2 * px + py, pl.ds(c * rh, rh), :],
                         sems[0].at[k], sems[1].at[k], (x, y, 1 - c)) for k, (px, py) in enumerate(chips)]
        recvs = [_remote(outs[0].at[2 * px + py, pl.ds((1 - c) * rh, rh), :], outs[0].at[2 * px + py, pl.ds((1 - c) * rh, rh), :],
                         sems[0].at[k], sems[1].at[k], (x, y, 1 - c)) for k, (px, py) in enumerate(chips)]
        return sends, recvs

    def start(ins, outs, sems):
        for cp in copies(ins, outs, sems)[0]:
            cp.start()

    def finish(ins, outs, sems):
        sends, recvs = copies(ins, outs, sems)
        for cp in recvs:
            cp.wait_recv()
        for cp in sends:
            cp.wait_send()

    return _Comm([part], [_shape(part.shape, part.dtype)], {0: 0}, [(3,), (3,)], start, finish, then)


def _phase_pair_exchange(full, then):
    _, R, C = full.shape
    rh = R // 2

    def copy(ins, outs, sems):
        x, y, c, _ = _place()
        return _remote(ins[0].at[:, pl.ds((1 - c) * rh, rh), :], outs[0], sems[0].at[0], sems[1].at[0], (x, y, 1 - c))

    return _Comm([full], [_shape((4, rh, C), full.dtype)], {}, [(1,), (1,)],
                 lambda ins, outs, sems: copy(ins, outs, sems).start(),
                 lambda ins, outs, sems: copy(ins, outs, sems).wait(), then)


def _phase_chip_scatter(parts, then):
    def copies(ins, outs, sems):
        x, y, c, chips = _place()
        me = 2 * x + y
        local = _Staged(ins[0].at[me], outs[0].at[me], sems[3], sems[2])
        sends = [_remote(ins[0].at[2 * px + py], outs[0].at[me], sems[0].at[k], sems[1].at[k], (px, py, c))
                 for k, (px, py) in enumerate(chips)]
        recvs = [_remote(outs[0].at[2 * px + py], outs[0].at[2 * px + py], sems[0].at[k], sems[1].at[k], (px, py, c))
                 for k, (px, py) in enumerate(chips)]
        return local, sends, recvs

    def start(ins, outs, sems):
        local, sends, _ = copies(ins, outs, sems)
        local.start()
        for cp in sends:
            cp.start()

    def finish(ins, outs, sems):
        local, sends, recvs = copies(ins, outs, sems)
        for cp in recvs:
            cp.wait_recv()
        for cp in sends:
            cp.wait_send()
        local.wait()

    return _Comm([parts], [_shape(parts.shape, parts.dtype)], {}, [(3,), (3,), (2,), (parts.shape[1:], parts.dtype)],
                 start, finish, then)


def _phase_pair_allgather(half, layer, depth, into, then):
    rh, C = half.shape

    def copies(ins, outs, sems):
        x, y, c, _ = _place()
        mine = outs[0].at[layer, pl.ds(c * rh, rh), :]
        theirs = outs[0].at[layer, pl.ds((1 - c) * rh, rh), :]
        return (_Staged(ins[0], mine, sems[3], sems[2]),
                _remote(ins[0], mine, sems[0].at[0], sems[1].at[0], (x, y, 1 - c)),
                _remote(theirs, theirs, sems[0].at[0], sems[1].at[0], (x, y, 1 - c)))

    def start(ins, outs, sems):
        local, send, _ = copies(ins, outs, sems)
        local.start()
        send.start()

    def finish(ins, outs, sems):
        local, send, recv = copies(ins, outs, sems)
        recv.wait_recv()
        send.wait_send()
        local.wait()

    ins = [half] if into is None else [half, into]
    return _Comm(ins, [_shape((depth, 2 * rh, C), half.dtype)], {} if into is None else {1: 0},
                 [(1,), (1,), (2,), ((rh, C), half.dtype)], start, finish, then)


def _comm_only(comms, name):
    plan = _Plan()
    for c in comms:
        plan.at(name, c)
    saved, _PLAN[0] = _PLAN[0], plan
    try:
        def body(o_ref):
            o_ref[...] = jnp.zeros_like(o_ref)

        _pallas(body, out_shape=[jax.ShapeDtypeStruct((8, 128), F32)], in_specs=[],
                out_specs=[pl.BlockSpec(memory_space=pltpu.VMEM)], name=name)()
    finally:
        _PLAN[0] = saved


def _row_tile(rows, cols, itemsize=4, budget=1 << 20, mult=8):
    fits = [t for t in range(mult, rows + 1, mult) if rows % t == 0 and t * cols * itemsize <= budget]
    return max(fits) if fits else rows


def _pair_add(full, got, c, name):
    _, L, R, C = full.shape
    rh = R // 2
    tr = _row_tile(rh, C, budget=2 << 20, mult=16)
    nb = rh // tr

    def body(c_ref, a_ref, b_ref, o_ref):
        o_ref[...] = (a_ref[...] + b_ref[...]).astype(o_ref.dtype)

    grid_spec = pltpu.PrefetchScalarGridSpec(
        num_scalar_prefetch=1,
        grid=(4, L, nb),
        in_specs=[pl.BlockSpec((1, 1, tr, C), lambda j, l, i, c_ref: (j, l, c_ref[0] * nb + i, 0)),
                  pl.BlockSpec((1, 1, tr, C), lambda j, l, i, c_ref: (j, l, i, 0))],
        out_specs=pl.BlockSpec((1, 1, tr, C), lambda j, l, i, c_ref: (j, l, i, 0)),
    )
    return _pallas(
        body,
        out_shape=_out((4, L, rh, C), BF16),
        grid_spec=grid_spec,
        compiler_params=_cparams(("parallel", "parallel", "parallel"), 8 * tr * C * 4),
        name=name,
    )(jnp.reshape(c, (1,)).astype(jnp.int32), full, got)


def _sum_chips(parts, name):
    _, L, rh, C = parts.shape
    tr = _row_tile(rh, C, budget=2 << 20, mult=16)

    def body(p_ref, o_ref):
        acc = p_ref[0].astype(F32)
        for j in range(1, 4):
            acc = acc + p_ref[j].astype(F32)
        o_ref[...] = acc

    return _pallas(
        body,
        out_shape=_out((L, rh, C), F32),
        grid=(L, rh // tr),
        in_specs=[pl.BlockSpec((4, 1, tr, C), lambda l, i: (0, l, i, 0))],
        out_specs=pl.BlockSpec((1, tr, C), lambda l, i: (l, i, 0)),
        compiler_params=_cparams(("parallel", "parallel"), 16 * tr * C * 4),
        name=name,
    )(parts)


def _allreduce_small(v):
    R = v.shape[0]

    def body(v_ref, o_ref, slots, send_sems, recv_sems):
        x, y, c, _ = _place()
        me = 4 * x + 2 * y + c
        slots[me] = v_ref[...]
        cps = []
        for r in range(1, 8):
            px, py, pc = x ^ (r >> 2), y ^ ((r >> 1) & 1), c ^ (r & 1)
            cps.append(pltpu.make_async_remote_copy(src_ref=v_ref, dst_ref=slots.at[me], send_sem=send_sems.at[r - 1],
                                                    recv_sem=recv_sems.at[r - 1], device_id=(px, py, pc), device_id_type=MESH))
            cps[-1].start()
        for r in range(1, 8):
            px, py, pc = x ^ (r >> 2), y ^ ((r >> 1) & 1), c ^ (r & 1)
            theirs = slots.at[4 * px + 2 * py + pc]
            pltpu.make_async_remote_copy(src_ref=theirs, dst_ref=theirs, send_sem=send_sems.at[r - 1], recv_sem=recv_sems.at[r - 1],
                                         device_id=(px, py, pc), device_id_type=MESH).wait_recv()
        acc = slots[0]
        for j in range(1, 8):
            acc = acc + slots[j]
        o_ref[...] = acc
        for cp in cps:
            cp.wait_send()

    return pl.pallas_call(
        body,
        out_shape=jax.ShapeDtypeStruct(v.shape, F32),
        in_specs=[pl.BlockSpec(memory_space=pltpu.VMEM)],
        out_specs=pl.BlockSpec(memory_space=pltpu.VMEM),
        scratch_shapes=[pltpu.VMEM((8, R, 128), F32), pltpu.SemaphoreType.DMA((7,)), pltpu.SemaphoreType.DMA((7,))],
        name="allreduce_small",
    )(v)


def _adamw_math(w, g, m, v):
    m = ADAM_B1 * m + (1.0 - ADAM_B1) * g
    v = ADAM_B2 * v + (1.0 - ADAM_B2) * (g * g)
    m_hat = m / (1.0 - ADAM_B1 ** ADAM_STEP)
    v_hat = v / (1.0 - ADAM_B2 ** ADAM_STEP)
    delta = -ADAM_LR * (m_hat / (jnp.sqrt(v_hat) + ADAM_EPS) + ADAM_WD * w)
    return delta, m, v


def _adamw(ws, gs, ms, vs, name, budget=1 << 19):
    n = len(ws)
    tiles = [_row_tile(w.shape[1], w.shape[2], budget=budget) for w in ws]
    per_layer = [w.shape[1] // t for w, t in zip(ws, tiles)]
    steps = [w.shape[0] * p for w, p in zip(ws, per_layer)]
    starts = [sum(steps[:k]) for k in range(n)]

    def body(*refs):
        i = pl.program_id(0)
        for k in range(n):
            w_ref, g_ref, m_ref, v_ref = (refs[j * n + k] for j in range(4))
            outs = [refs[(4 + j) * n + k] for j in range(3)]

            @pl.when((i >= starts[k]) & (i < starts[k] + steps[k]))
            def _(w_ref=w_ref, g_ref=g_ref, m_ref=m_ref, v_ref=v_ref, outs=outs):
                outs[0][...], outs[1][...], outs[2][...] = _adamw_math(w_ref[...], g_ref[...], m_ref[...], v_ref[...])

    def spec(k):
        def index(i):
            local = jnp.clip(i - starts[k], 0, steps[k] - 1)
            return local // per_layer[k], local % per_layer[k], 0
        return pl.BlockSpec((None, tiles[k], ws[k].shape[2]), index)

    specs = [spec(k) for k in range(n)]
    outs = [_out(w.shape, F32) for w in ws]
    res = _pallas(
        body,
        out_shape=tuple(outs * 3),
        grid=(sum(steps),),
        in_specs=specs * 4,
        out_specs=tuple(specs * 3),
        compiler_params=_cparams(("arbitrary",), sum(16 * t * w.shape[2] * 4 for w, t in zip(ws, tiles))),
        name=name,
    )(*ws, *gs, *ms, *vs)
    return res[:n], res[n:2 * n], res[2 * n:]


def _adamw_slabs(w, g, m, v, name, slabs=59):
    n, L, C = w.shape
    assert n % slabs == 0

    def body(w_ref, g_ref, m_ref, v_ref, d_ref, nm_ref, nv_ref):
        d_ref[...], nm_ref[...], nv_ref[...] = _adamw_math(w_ref[...], g_ref[...], m_ref[...], v_ref[...])

    blk = pl.BlockSpec((slabs, L, C), lambda i: (i, 0, 0))
    out = _out(w.shape, F32)
    return _pallas(
        body,
        out_shape=(out, out, out),
        grid=(n // slabs,),
        in_specs=[blk] * 4,
        out_specs=(blk, blk, blk),
        compiler_params=_cparams(("parallel",), 16 * slabs * 8 * C * 4),
        name=name,
    )(w, g, m, v)


def _adamw_small(ws, gs, ms, vs):
    n = len(ws)

    def body(*refs):
        for t in range(n):
            w_ref, g_ref, m_ref, v_ref = (refs[k * n + t] for k in range(4))
            d_ref, nm_ref, nv_ref = (refs[(4 + k) * n + t] for k in range(3))
            d_ref[...], nm_ref[...], nv_ref[...] = _adamw_math(w_ref[...], g_ref[...], m_ref[...], v_ref[...])

    vmem = pl.BlockSpec(memory_space=pltpu.VMEM)
    outs = [jax.ShapeDtypeStruct(w.shape, F32) for w in ws]
    res = pl.pallas_call(
        body,
        out_shape=outs * 3,
        in_specs=[vmem] * (4 * n),
        out_specs=[vmem] * (3 * n),
        name="adamw_small",
    )(*ws, *gs, *ms, *vs)
    return res[:n], res[n:2 * n], res[2 * n:]


IN_SIZES = (192, 192, 384, 384, 16, 512, 384, 384, 384)
IN_OFFS = tuple(int(v) for v in np.cumsum((0,) + IN_SIZES))
SMALL = ("norm_mix", "w_gla_gate", "b_gla_gate", "gla_norm", "b_dw", "conv_ln_g", "conv_ln_b", "rel_bias", "norm_ffn")


def _pad_cols(a, n):
    return jnp.pad(a, ((0, 0), (0, n - a.shape[1])))


W_IN_SHARD = 708
W_IN_ROWS = 736


def _pad_rows(a, n):
    return jnp.pad(a, ((0, n - a.shape[0]), (0, 0)))


def _split_w_in_t(w):
    s = [w[IN_OFFS[i]:IN_OFFS[i + 1]] for i in range(9)]
    w_gla = jnp.concatenate([_pad_rows(s[0], KW), _pad_rows(s[1], KW), s[2], s[3], _pad_rows(s[4], LRW)], axis=0)
    return w_gla, s[5], jnp.concatenate(s[6:9], axis=0)


def _join_w_in_t(g):
    gg = g["w_gla"]
    full = jnp.concatenate([gg[0:192], gg[KW:KW + 192], gg[2 * KW:2 * KW + VW], gg[2 * KW + VW:2 * KW + 2 * VW],
                            gg[2 * KW + 2 * VW:2 * KW + 2 * VW + 16], g["w_conv"], g["w_att"]], axis=0)
    return jnp.pad(full.reshape(4, W_IN_SHARD, D), ((0, 0), (0, W_IN_ROWS - W_IN_SHARD), (0, 0)))


def _pack(arrs, rows):
    flat = jnp.concatenate([a.reshape(-1) for a in arrs])
    return jnp.pad(flat, (0, rows * 128 - flat.shape[0])).reshape(rows, 128)


def _unpack(packed, shapes):
    flat = packed.reshape(-1)
    out, off = [], 0
    for s in shapes:
        n = int(np.prod(s))
        out.append(flat[off:off + n].reshape(s))
        off += n
    return out


def kernel(x, norm_mix, w_in, w_gla_gate, b_gla_gate, gla_norm, w_dw, b_dw, conv_ln_g, conv_ln_b, rel_bias, w_out, norm_ffn, w_up, w_down, norm_final, loss_target, m_norm_mix, m_w_in, m_w_gla_gate, m_b_gla_gate, m_gla_norm, m_w_dw, m_b_dw, m_conv_ln_g, m_conv_ln_b, m_rel_bias, m_w_out, m_norm_ffn, m_w_up, m_w_down, m_norm_final, v_norm_mix, v_w_in, v_w_gla_gate, v_b_gla_gate, v_gla_norm, v_w_dw, v_b_dw, v_conv_ln_g, v_conv_ln_b, v_rel_bias, v_w_out, v_norm_ffn, v_w_up, v_w_down, v_norm_final):
    P = dict(norm_mix=norm_mix, w_in=w_in, w_gla_gate=w_gla_gate, b_gla_gate=b_gla_gate, gla_norm=gla_norm, w_dw=w_dw, b_dw=b_dw,
             conv_ln_g=conv_ln_g, conv_ln_b=conv_ln_b, rel_bias=rel_bias, w_out=w_out, norm_ffn=norm_ffn, w_up=w_up,
             w_down=w_down, norm_final=norm_final)
    Mo = dict(norm_mix=m_norm_mix, w_in=m_w_in, w_gla_gate=m_w_gla_gate, b_gla_gate=m_b_gla_gate, gla_norm=m_gla_norm, w_dw=m_w_dw,
              b_dw=m_b_dw, conv_ln_g=m_conv_ln_g, conv_ln_b=m_conv_ln_b, rel_bias=m_rel_bias, w_out=m_w_out, norm_ffn=m_norm_ffn,
              w_up=m_w_up, w_down=m_w_down, norm_final=m_norm_final)
    Vo = dict(norm_mix=v_norm_mix, w_in=v_w_in, w_gla_gate=v_w_gla_gate, b_gla_gate=v_b_gla_gate, gla_norm=v_gla_norm, w_dw=v_w_dw,
              b_dw=v_b_dw, conv_ln_g=v_conv_ln_g, conv_ln_b=v_conv_ln_b, rel_bias=v_rel_bias, w_out=v_w_out, norm_ffn=v_norm_ffn,
              w_up=v_w_up, w_down=v_w_down, norm_final=v_norm_final)
    depth = w_in.shape[0]
    xi, yi, ci = lax.axis_index("x"), lax.axis_index("y"), lax.axis_index("c")
    chip = 2 * xi + yi

    plan = _Plan()
    _PLAN[0] = plan
    layers = [dict(
        norm_mix=norm_mix[l][None], wg=jnp.pad(w_gla_gate[l], ((0, LRW - 16), (0, KW - 192))),
        bg=_pad_cols(b_gla_gate[l][None], KW), gla_norm=gla_norm[l][None], b_dw=b_dw[l][None], ln_g=conv_ln_g[l][None],
        ln_b=conv_ln_b[l][None], rel_bias=rel_bias[l], norm_ffn=norm_ffn[l][None]) for l in range(depth)]

    w_in_t, m_w_in_t, v_w_in_t = (jnp.transpose(a, (2, 0, 1)) for a in (w_in, m_w_in, v_w_in))

    def w_in_shard(l):
        return _pad_rows(w_in_t[:, l, :], W_IN_ROWS).astype(BF16)

    def have_w_in(l, full):
        rows = jnp.concatenate([full[j, 0:W_IN_SHARD] for j in range(4)], axis=0)
        layers[l]["w_gla"], layers[l]["w_conv"], layers[l]["w_att"] = _split_w_in_t(rows)

    def have_w_out(l, full):
        w = full.reshape(D, D)
        layers[l]["w_out_g"], layers[l]["w_out_c"], layers[l]["w_out_a"] = w[0:VW], w[VW:VW + CW], w[VW + CW:]

    def have_w_up(l, full):
        layers[l]["w_up"] = full

    def have_w_down(l, full):
        layers[l]["w_down"] = full.reshape(D_FF, D)

    def have_w_dw(full):
        taps = full.reshape(4, depth, HALO, CW // 4)
        for l in range(depth):
            layers[l]["w_dw"] = jnp.transpose(taps[:, l], (1, 0, 2)).reshape(HALO, CW)

    first_d2d = []

    def first_ici(shard, have):
        return _phase_gather_ici(shard, lambda outs: first_d2d.append(_phase_gather_d2d(outs[0], lambda done: have(done[0]))))

    w_dw_pad = jnp.pad(w_dw, ((0, 0), (0, HALO - CK), (0, 0))).reshape(depth * HALO, CW // 4)
    _comm_only([first_ici(w_in_shard(0), functools.partial(have_w_in, 0)), first_ici(w_dw_pad, have_w_dw)],
               "gather_first_ici")
    _comm_only(first_d2d, "gather_first_d2d")

    def gather_behind(shard, ici_call, d2d_call, have):
        plan.at(ici_call, _phase_gather_ici(
            shard, lambda outs: plan.at(d2d_call, _phase_gather_d2d(outs[0], lambda done: have(done[0])))))

    for l in range(depth):
        if l > 0:
            gather_behind(w_in_shard(l), f"l{l - 1}_mlp_up", f"l{l - 1}_mlp_down", functools.partial(have_w_in, l))
        gather_behind(w_out[l].astype(BF16), f"l{l - 1}_mlp_down" if l > 0 else "l0_proj", f"l{l}_gla_fwd",
                      functools.partial(have_w_out, l))
        if l > 0:
            gather_behind(w_up[l].astype(BF16), f"l{l}_proj", f"l{l}_gla_fwd", functools.partial(have_w_up, l))
            gather_behind(w_down[l].astype(BF16), f"l{l}_gla_fwd", f"l{l}_att_fwd", functools.partial(have_w_down, l))
        else:
            gather_behind(w_up[l].astype(BF16), f"l{l}_gla_fwd", f"l{l}_att_fwd", functools.partial(have_w_up, l))
            gather_behind(w_down[l].astype(BF16), f"l{l}_att_fwd", f"l{l}_mlp_up", functools.partial(have_w_down, l))

    reduced = {}
    last_swap = []

    def reduce_calls(name, l):
        if name == "w_down":
            return f"l{l}_mlp_up_dx", f"l{l}_gla_bwd", f"l{l}_conv_bwd_dc"
        if name == "w_up":
            return f"l{l}_out_dx", f"l{l}_att_bwd", f"l{l}_proj_dw"
        if name == "w_out":
            return f"l{l}_gla_bwd", f"l{l}_conv_bwd_dc", f"l{l}_att_bwd"
        if l > 0:
            return f"l{l}_proj_dx", f"l{l - 1}_mlp_down_dw", f"l{l - 1}_mlp_up_dx"
        return None, "l0_proj_dx", None

    def reduce_behind(l, name, full):
        calls = reduce_calls(name, l)

        def swapped(outs):
            pair = _pair_add(full[:, None], outs[0][:, None], ci, f"reduce_pair_add_{name}{l}")[:, 0]
            plan.at(calls[1], _phase_chip_scatter(pair, scattered))

        def scattered(outs):
            half = _sum_chips(outs[0][:, None], f"reduce_sum_chips_{name}{l}")[0]
            phase = _phase_pair_allgather(half, l, depth, reduced.get(name), gathered)
            if calls[2] is None:
                last_swap.append(phase)
            else:
                plan.at(calls[2], phase)

        def gathered(outs):
            reduced[name] = outs[0]

        if calls[0] is None:
            _comm_only([_phase_pair_exchange(full, swapped)], f"reduce_pair_exchange_{name}{l}")
        else:
            plan.at(calls[0], _phase_pair_exchange(full, swapped))

    loss_part, grad_x, grads, g_final = _local_step(x[0], loss_target[0], layers, norm_final[None], reduce_behind)

    G, delta, new_m, new_v = {}, {}, {}, {}
    early = ("w_down", "w_up", "w_out")
    for name in early:
        G[name] = reduced[name]
    ds, nms, nvs = _adamw([P[k] for k in early], [G[k] for k in early], [Mo[k] for k in early], [Vo[k] for k in early],
                          "adamw_early")
    for i, name in enumerate(early):
        delta[name], new_m[name], new_v[name] = ds[i], nms[i], nvs[i]
    _PLAN[0] = None
    assert not plan.by_call, sorted(plan.by_call)

    small_g = []
    for l in range(depth):
        g = grads[l]
        small_g += [g["norm_mix"], g["wg"][0:16, 0:192], g["bg"][:, 0:192], g["gla_norm"], g["b_dw"], g["ln_g"], g["ln_b"],
                    g["rel_bias"], g["norm_ffn"], g["w_dw"][0:CK]]
    small_g += [g_final, loss_part]
    small_shapes = [a.shape for a in small_g]
    n_small = sum(int(np.prod(s)) for s in small_shapes)
    rows = -(-n_small // 1024) * 8
    red = _unpack(_allreduce_small(_pack(small_g, rows)), small_shapes)
    per = len(SMALL) + 1
    for i, name in enumerate(SMALL):
        G[name] = jnp.stack([red[l * per + i].reshape(P[name].shape[1:]) for l in range(depth)])
    gw_dw_all = jnp.stack([red[l * per + len(SMALL)] for l in range(depth)])
    G["w_dw"] = lax.dynamic_slice_in_dim(gw_dw_all, chip * (CW // 4), CW // 4, axis=2)
    G["norm_final"] = red[-2].reshape(norm_final.shape)
    loss = red[-1][0, 0]

    _comm_only(last_swap, "reduce_pair_allgather_last")
    back = lambda a: jnp.transpose(a, (1, 2, 0))
    g_in_t = jnp.transpose(reduced["w_in"][:, 0:W_IN_SHARD, :], (1, 0, 2))
    d_in, nm_in, nv_in = _adamw_slabs(w_in_t, g_in_t, m_w_in_t, v_w_in_t, "adamw_w_in")
    G["w_in"], delta["w_in"], new_m["w_in"], new_v["w_in"] = back(g_in_t), back(d_in), back(nm_in), back(nv_in)

    small_names = list(SMALL) + ["w_dw", "norm_final"]
    two_d = lambda a: a.reshape(-1, a.shape[-1])
    ds, nms, nvs = _adamw_small([two_d(P[k]) for k in small_names], [two_d(G[k]) for k in small_names],
                                [two_d(Mo[k]) for k in small_names], [two_d(Vo[k]) for k in small_names])
    for i, name in enumerate(small_names):
        shp = P[name].shape
        delta[name], new_m[name], new_v[name] = ds[i].reshape(shp), nms[i].reshape(shp), nvs[i].reshape(shp)

    order = ["norm_mix", "w_in", "w_gla_gate", "b_gla_gate", "gla_norm", "w_dw", "b_dw", "conv_ln_g", "conv_ln_b", "rel_bias",
             "w_out", "norm_ffn", "w_up", "w_down", "norm_final"]
    return (loss, grad_x[None], *[G[k] for k in order], *[delta[k] for k in order], *[new_m[k] for k in order],
            *[new_v[k] for k in order])
```

```python
import functools

import numpy as np
import jax
import jax.numpy as jnp
from jax import lax
from jax.experimental import pallas as pl
from jax.experimental.pallas import tpu as pltpu

F32 = jnp.float32
BF16 = jnp.bfloat16
HI = lax.Precision.HIGHEST

D = 1024
CHUNK = 64
GLA_DK, GLA_DV, GLA_H = 48, 96, 4
KW = 256
VW = 384
LRW = 128
GLA_TAU = 16.0
CW = 256
CK = 31
AW = 384
AH = 6
BAND = 576
LEFT = 512
D_FF = 4096
EPS = 1e-6
NEG = -1e30
N_REL = 257

GLA_COLS = 2 * KW + 2 * VW + LRW
CONV_COLS = 2 * CW
ATT_COLS = 3 * AW

ADAM_LR, ADAM_B1, ADAM_B2, ADAM_EPS, ADAM_WD, ADAM_STEP = 0.001, 0.9, 0.999, 1e-08, 0.01, 10

VMEM_CAP = 56 * 1024 * 1024
MESH = pl.DeviceIdType.MESH


def _cparams(sem, vmem_bytes):
    limit = int(min(VMEM_CAP, max(vmem_bytes * 5 // 4 + (4 << 20), 16 << 20)))
    return pltpu.CompilerParams(dimension_semantics=sem, vmem_limit_bytes=limit)


def _out(shape, dtype):
    return pltpu.HBM(tuple(shape), dtype)


class _Comm:
    def __init__(self, ins, outs, aliases, sems, start, finish, then=None):
        self.ins, self.outs, self.aliases, self.sems = list(ins), list(outs), dict(aliases), list(sems)
        self.start, self.finish, self.then = start, finish, then


class _Plan:
    def __init__(self):
        self.by_call = {}

    def at(self, call, comm):
        self.by_call.setdefault(call, []).append(comm)

    def take(self, call):
        return self.by_call.pop(call, [])


_PLAN = [None]


def _pin(a):
    return pltpu.with_memory_space_constraint(a, pltpu.HBM) if jnp.issubdtype(a.dtype, jnp.floating) else a


def _pallas(body, **kw):
    comms = _PLAN[0].take(kw.get("name")) if _PLAN[0] is not None else []
    if not comms:
        call = pl.pallas_call(body, **kw)
        return lambda *args: call(*[_pin(a) for a in args])

    grid = tuple(kw.get("grid", ()))
    single = not isinstance(kw["out_shape"], (tuple, list))
    out_shape = [kw["out_shape"]] if single else list(kw["out_shape"])
    out_specs = [kw["out_specs"]] if single else list(kw["out_specs"])
    in_specs = list(kw["in_specs"])
    scratch = list(kw.get("scratch_shapes", ()))
    n_in, n_out, n_scr = len(in_specs), len(out_shape), len(scratch)
    c_in = sum(len(c.ins) for c in comms)
    c_out = sum(len(c.outs) for c in comms)
    aliases = dict(kw.get("input_output_aliases", {}))
    i0, o0 = n_in, n_out
    for c in comms:
        for i, o in c.aliases.items():
            aliases[i0 + i] = o0 + o
        i0 += len(c.ins)
        o0 += len(c.outs)

    def wrapped(*refs):
        ins, c_ins = refs[:n_in], refs[n_in:n_in + c_in]
        outs, c_outs = refs[n_in + c_in:n_in + c_in + n_out], refs[n_in + c_in + n_out:n_in + c_in + n_out + c_out]
        scr, c_sems = refs[n_in + c_in + n_out + c_out:][:n_scr], refs[n_in + c_in + n_out + c_out + n_scr:]

        def each(what):
            i0 = o0 = s0 = 0
            for c in comms:
                getattr(c, what)(c_ins[i0:i0 + len(c.ins)], c_outs[o0:o0 + len(c.outs)], c_sems[s0:s0 + len(c.sems)])
                i0, o0, s0 = i0 + len(c.ins), o0 + len(c.outs), s0 + len(c.sems)

        if grid:
            first = functools.reduce(jnp.logical_and, [pl.program_id(a) == 0 for a in range(len(grid))])
            last = functools.reduce(jnp.logical_and, [pl.program_id(a) == grid[a] - 1 for a in range(len(grid))])
            pl.when(first)(lambda: each("start"))
            body(*ins, *outs, *scr)
            pl.when(last)(lambda: each("finish"))
        else:
            each("start")
            body(*ins, *outs, *scr)
            each("finish")

    kw = dict(kw)
    kw["in_specs"] = in_specs + [ANY] * c_in
    kw["out_shape"] = out_shape + [_out(s.shape, s.dtype) for c in comms for s in c.outs]
    kw["out_specs"] = out_specs + [ANY] * c_out
    staging = [s for c in comms for s in c.sems if len(s) == 2 and not isinstance(s[1], int)]
    kw["scratch_shapes"] = scratch + [pltpu.VMEM(*s) if s in staging else pltpu.SemaphoreType.DMA(s) for c in comms for s in c.sems]
    kw["input_output_aliases"] = aliases
    extra = sum(_nbytes(*s) for s in staging)
    old = kw.get("compiler_params")
    limit = (old.vmem_limit_bytes if old is not None else 16 << 20) + extra
    kw["compiler_params"] = pltpu.CompilerParams(
        dimension_semantics=old.dimension_semantics if old is not None else None, vmem_limit_bytes=int(min(VMEM_CAP, limit)))
    call = pl.pallas_call(wrapped, **kw)

    def run(*args):
        res = call(*[_pin(a) for a in args], *[_pin(a) for c in comms for a in c.ins])
        o0 = n_out
        for c in comms:
            if c.then is not None:
                c.then(res[o0:o0 + len(c.outs)])
            o0 += len(c.outs)
        return res[0] if single else res[:n_out]

    return run


def _nbytes(shape, dtype):
    return int(np.prod(shape)) * jnp.dtype(dtype).itemsize


def _sigmoid(x):
    return 1.0 / (1.0 + jnp.exp(-x))


_DIMS = {"nn": (((1,), (0,)), ((), ())), "nt": (((1,), (1,)), ((), ())), "tn": (((0,), (0,)), ((), ()))}


def _mm(a, b, *, mode, out_dtype, name, tm=512, tn=None, tk=None, a_pro=None, epi=None, extra=None,
        b_chips=False, out_chips=False, norm_g=None, norm_bwd=None):
    b2 = (b.shape[1], 4 * b.shape[2]) if b_chips else b.shape
    if mode == "nn":
        (M, K), (K2, N) = a.shape, b2
    elif mode == "nt":
        (M, K), (N, K2) = a.shape, b2
    else:
        (K, M), (K2, N) = a.shape, b2
    assert K == K2, (a.shape, b.shape, mode)
    tm = min(tm, M)
    tn = N if tn is None else min(tn, N)
    tk = K if tk is None else min(tk, K)
    assert M % tm == 0 and N % tn == 0 and K % tk == 0, (M, N, K, tm, tn, tk)
    nk = K // tk
    a_blk = (tk, tm) if mode == "tn" else (tm, tk)
    a_map = (lambda i, j, k: (k, i)) if mode == "tn" else (lambda i, j, k: (i, k))
    b_blk = (tn, tk) if mode == "nt" else (tk, tn)
    b_map = (lambda i, j, k: (j, k)) if mode == "nt" else (lambda i, j, k: (k, j))
    if b_chips:
        per = b.shape[2] // b_blk[1]
        assert b.shape[2] % b_blk[1] == 0 and mode != "tn"
        flat_map = b_map
        b_map = lambda i, j, k: (flat_map(i, j, k)[1] // per, flat_map(i, j, k)[0], flat_map(i, j, k)[1] % per)
        b_blk = (None,) + b_blk
    in_specs = [pl.BlockSpec(a_blk, a_map), pl.BlockSpec(b_blk, b_map)]
    args = [a, b]
    if epi is not None:
        in_specs.append(pl.BlockSpec((tm, tn), lambda i, j, k: (i, j)))
        args.append(extra)
    if norm_g is not None:
        assert nk == 1 and mode != "tn"
        in_specs.append(pl.BlockSpec((1, K), lambda i, j, k: (0, 0)))
        args.append(norm_g)
    if norm_bwd is not None:
        assert tn == N
        row = pl.BlockSpec((tm, N), lambda i, j, k: (i, 0))
        in_specs += [row, pl.BlockSpec((1, N), lambda i, j, k: (0, 0)), row]
        args += list(norm_bwd)

    def body(*refs):
        it = iter(refs)
        a_ref, b_ref = next(it), next(it)
        e_ref = next(it) if epi is not None else None
        ng_ref = next(it) if norm_g is not None else None
        h_ref, g_ref, dres_ref = (next(it), next(it), next(it)) if norm_bwd is not None else (None, None, None)
        o_ref = next(it)
        xn_ref = next(it) if norm_g is not None else None
        dg_ref = next(it) if norm_bwd is not None else None
        if norm_bwd is not None:
            @pl.when((pl.program_id(0) == 0) & (pl.program_id(2) == 0))
            def _():
                dg_ref[...] = jnp.zeros_like(dg_ref)

        av = a_ref[...]
        if a_pro == "relu2":
            af = jnp.maximum(av.astype(F32), 0.0)
            av = af * af
        if norm_g is not None:
            av = (av * lax.rsqrt(jnp.mean(av * av, axis=-1, keepdims=True) + EPS) * ng_ref[...]).astype(BF16)
            xn_ref[...] = av
        p = lax.dot_general(av.astype(BF16), b_ref[...].astype(BF16), _DIMS[mode], preferred_element_type=F32)

        def finish(acc):
            if epi == "add":
                acc = acc + e_ref[...].astype(F32)
            elif epi == "relu2grad":
                acc = acc * (2.0 * jnp.maximum(e_ref[...].astype(F32), 0.0))
            if norm_bwd is None:
                o_ref[...] = acc.astype(o_ref.dtype)
                return

            x = h_ref[...]
            r = lax.rsqrt(jnp.mean(x * x, axis=-1, keepdims=True) + EPS)
            gy = acc * g_ref[...]
            dot = jnp.mean(x * gy, axis=-1, keepdims=True)
            o_ref[...] = dres_ref[...] + r * gy - x * (r * r * r * dot)
            dg_ref[...] += jnp.sum(acc * x * r, axis=0, keepdims=True)

        if nk == 1:
            finish(p)
        else:
            acc_ref = refs[-1]
            k = pl.program_id(2)

            @pl.when(k == 0)
            def _():
                acc_ref[...] = p

            @pl.when(k > 0)
            def _():
                acc_ref[...] += p

            @pl.when(k == nk - 1)
            def _():
                finish(acc_ref[...])

    vm = 2 * (_nbytes(a_blk, a.dtype) + _nbytes((tk, tn), b.dtype) + _nbytes((tm, tn), out_dtype))
    vm += 3 * _nbytes((tm, tn), F32)
    if epi is not None:
        vm += 2 * _nbytes((tm, tn), extra.dtype)
    if out_chips:
        per_out = N // 4 // tn
        assert N % (4 * tn) == 0
        out_shape = _out((4, M, N // 4), out_dtype)
        out_spec = pl.BlockSpec((None, tm, tn), lambda i, j, k: (j // per_out, i, j % per_out))
    else:
        out_shape = _out((M, N), out_dtype)
        out_spec = pl.BlockSpec((tm, tn), lambda i, j, k: (i, j))
    sem = ("parallel", "parallel", "arbitrary")
    if norm_g is not None:
        out_shape, out_spec = (out_shape, _out((M, K), BF16)), (out_spec, pl.BlockSpec((tm, K), lambda i, j, k: (i, 0)))
        vm += 4 * _nbytes((tm, K), F32)
    if norm_bwd is not None:
        out_shape, out_spec = (out_shape, _out((1, N), F32)), (out_spec, pl.BlockSpec((1, N), lambda i, j, k: (0, 0)))
        sem = ("arbitrary", "arbitrary", "arbitrary")
        vm += 8 * _nbytes((tm, N), F32)
    return _pallas(
        body,
        out_shape=out_shape,
        grid=(M // tm, N // tn, nk),
        in_specs=in_specs,
        out_specs=out_spec,
        scratch_shapes=[pltpu.VMEM((tm, tn), F32)] if nk > 1 else [],
        compiler_params=_cparams(sem, vm),
        name=name,
    )(*args)


def _mm_fan(a, bs, *, mode, out_dtypes, name, tm=512, norm_g=None):
    M, K = a.shape
    ns = [b.shape[1] if mode == "nn" else b.shape[0] for b in bs]
    n = len(bs)
    first_out = 1 + n + (norm_g is not None)

    def body(*refs):
        if norm_g is None:
            av = refs[0][...].astype(BF16)
        else:
            x = refs[0][...]
            av = (x * lax.rsqrt(jnp.mean(x * x, axis=-1, keepdims=True) + EPS) * refs[1 + n][...]).astype(BF16)
            refs[first_out + n][...] = av
        for i in range(n):
            refs[first_out + i][...] = lax.dot_general(av, refs[1 + i][...], _DIMS[mode],
                                                       preferred_element_type=F32).astype(refs[first_out + i].dtype)

    vm = 4 * _nbytes((tm, K), F32) + sum(2 * _nbytes(b.shape, b.dtype) + 3 * _nbytes((tm, nn), F32) for b, nn in zip(bs, ns))
    in_specs = [pl.BlockSpec((tm, K), lambda i: (i, 0))] + [pl.BlockSpec(b.shape, lambda i: (0, 0)) for b in bs]
    out_shape = [_out((M, nn), dt) for nn, dt in zip(ns, out_dtypes)]
    out_specs = [pl.BlockSpec((tm, nn), lambda i: (i, 0)) for nn in ns]
    args = [a, *bs]
    if norm_g is not None:
        in_specs.append(pl.BlockSpec((1, K), lambda i: (0, 0)))
        out_shape.append(_out((M, K), BF16))
        out_specs.append(pl.BlockSpec((tm, K), lambda i: (i, 0)))
        args.append(norm_g)
    return _pallas(
        body,
        out_shape=tuple(out_shape),
        grid=(M // tm,),
        in_specs=in_specs,
        out_specs=tuple(out_specs),
        compiler_params=_cparams(("parallel",), vm),
        name=name,
    )(*args)


def _mm_sum(as_, bs, *, mode, out_dtype, name, extra=None, norm_bwd=None, tm=512):
    M = as_[0].shape[0]
    N = bs[0].shape[1] if mode == "nn" else bs[0].shape[0]
    n = len(as_)

    def body(*refs):
        acc = None
        for i in range(n):
            p = lax.dot_general(refs[i][...].astype(BF16), refs[n + i][...], _DIMS[mode], preferred_element_type=F32)
            acc = p if acc is None else acc + p
        if extra is not None:
            acc = acc + refs[2 * n][...].astype(F32)
        if norm_bwd is None:
            refs[-1][...] = acc.astype(refs[-1].dtype)
        else:
            h_ref, g_ref, dres_ref, dh_ref, dg_ref = refs[-5:]

            @pl.when(pl.program_id(0) == 0)
            def _():
                dg_ref[...] = jnp.zeros_like(dg_ref)

            x = h_ref[...]
            r = lax.rsqrt(jnp.mean(x * x, axis=-1, keepdims=True) + EPS)
            gy = acc * g_ref[...]
            dot = jnp.mean(x * gy, axis=-1, keepdims=True)
            dh_ref[...] = dres_ref[...] + r * gy - x * (r * r * r * dot)
            dg_ref[...] += jnp.sum(acc * x * r, axis=0, keepdims=True)

    row = pl.BlockSpec((tm, N), lambda i: (i, 0))
    in_specs = [pl.BlockSpec((tm, a.shape[1]), lambda i: (i, 0)) for a in as_]
    in_specs += [pl.BlockSpec(b.shape, lambda i: (0, 0)) for b in bs]
    args = list(as_) + list(bs)
    if extra is not None:
        in_specs.append(row)
        args.append(extra)
    vm = sum(2 * _nbytes((tm, a.shape[1]), a.dtype) for a in as_) + sum(2 * _nbytes(b.shape, b.dtype) for b in bs)
    vm += 8 * _nbytes((tm, N), F32)
    if norm_bwd is None:
        out_shape, out_specs, sem = _out((M, N), out_dtype), row, "parallel"
    else:
        vec = pl.BlockSpec((1, N), lambda i: (0, 0))
        in_specs += [row, vec, row]
        args += list(norm_bwd)
        out_shape, out_specs, sem = (_out((M, N), F32), _out((1, N), F32)), (row, vec), "arbitrary"
        vm += 8 * _nbytes((tm, N), F32)
    return _pallas(
        body,
        out_shape=out_shape,
        grid=(M // tm,),
        in_specs=in_specs,
        out_specs=out_specs,
        compiler_params=_cparams((sem,), vm),
        name=name,
    )(*args)


def _mm_tn_multi(ops, pairs, *, name, tk=1024):
    T = ops[0].shape[0]
    tk = min(tk, T)
    n, m = len(ops), len(pairs)
    shapes = [(ops[a].shape[1], ops[b].shape[1]) for a, b in pairs]

    def body(*refs):
        vals = [refs[i][...].astype(BF16) for i in range(n)]
        first = pl.program_id(0) == 0
        for j, (a, b) in enumerate(pairs):
            p = lax.dot_general(vals[a], vals[b], _DIMS["tn"], preferred_element_type=F32)
            o_ref = refs[n + j]

            @pl.when(first)
            def _(o_ref=o_ref, p=p):
                o_ref[...] = p

            @pl.when(jnp.logical_not(first))
            def _(o_ref=o_ref, p=p):
                o_ref[...] += p

    vm = sum(2 * _nbytes((tk, o.shape[1]), o.dtype) for o in ops) + sum(3 * _nbytes(s, F32) for s in shapes)
    return _pallas(
        body,
        out_shape=tuple(_out(s, F32) for s in shapes),
        grid=(T // tk,),
        in_specs=[pl.BlockSpec((tk, o.shape[1]), lambda k: (k, 0)) for o in ops],
        out_specs=tuple(pl.BlockSpec(s, lambda k: (0, 0)) for s in shapes),
        compiler_params=_cparams(("arbitrary",), vm),
        name=name,
    )(*ops)


def _final_loss(h, g, target, name, tm=512):
    T = h.shape[0]

    def body(h_ref, g_ref, t_ref, loss_ref, dh_ref, dg_ref):
        @pl.when(pl.program_id(0) == 0)
        def _():
            dg_ref[...] = jnp.zeros_like(dg_ref)
            loss_ref[...] = jnp.zeros_like(loss_ref)

        x = h_ref[...]
        gg = g_ref[...]
        r = lax.rsqrt(jnp.mean(x * x, axis=-1, keepdims=True) + EPS)
        y = x * r * gg
        e = y - t_ref[...]
        loss_ref[...] += 0.5 * jnp.sum(jnp.mean(e * e, axis=-1, keepdims=True), axis=0, keepdims=True)
        dy = e * (1.0 / D)
        gy = dy * gg
        dot = jnp.mean(x * gy, axis=-1, keepdims=True)
        dh_ref[...] = r * gy - x * (r * r * r * dot)
        dg_ref[...] += jnp.sum(dy * x * r, axis=0, keepdims=True)

    row = pl.BlockSpec((tm, D), lambda i: (i, 0))
    vec = pl.BlockSpec((1, D), lambda i: (0, 0))
    one = pl.BlockSpec((1, 1), lambda i: (0, 0))
    return _pallas(
        body,
        out_shape=(_out((1, 1), F32), _out((T, D), F32), _out((1, D), F32)),
        grid=(T // tm,),
        in_specs=[row, vec, row],
        out_specs=(one, row, vec),
        compiler_params=_cparams(("arbitrary",), 12 * _nbytes((tm, D), F32)),
        name=name,
    )(h, g, target)


GLA_G = 8


def _gla_consts():
    i = np.arange(KW)[:, None]
    j = np.arange(VW)[None, :]
    mask = ((i // GLA_DK) == (j // GLA_DV)) & (i < GLA_H * GLA_DK)
    a = np.arange(VW)
    hm = ((a[:, None] // GLA_DV) == (a[None, :] // GLA_DV)).astype(np.float32)
    c = np.arange(CHUNK)
    low = (c[:, None] >= c[None, :]).astype(np.float32)
    return jnp.asarray(mask.astype(np.float32)), jnp.asarray(hm, BF16), jnp.asarray(low, BF16)


def _split(x):
    hi = x.astype(BF16)
    return hi, (x - hi.astype(F32)).astype(BF16)


def _dot_sel(a, b, dims, split):
    if split == "a":
        hi, lo = _split(a)
        return (lax.dot_general(hi, b, dims, preferred_element_type=F32) + lax.dot_general(lo, b, dims, preferred_element_type=F32))
    hi, lo = _split(b)
    return (lax.dot_general(a, hi, dims, preferred_element_type=F32) + lax.dot_general(a, lo, dims, preferred_element_type=F32))


def _dot3(a, b, dims):
    ah, al = _split(a)
    bh, bl = _split(b)
    return (lax.dot_general(ah, bh, dims, preferred_element_type=F32) + lax.dot_general(al, bh, dims, preferred_element_type=F32)
            + lax.dot_general(ah, bl, dims, preferred_element_type=F32))


def _dot3s(a_s, b_s, dims):
    (ah, al), (bh, bl) = a_s, b_s
    return (lax.dot_general(ah, bh, dims, preferred_element_type=F32) + lax.dot_general(al, bh, dims, preferred_element_type=F32)
            + lax.dot_general(ah, bl, dims, preferred_element_type=F32))


def _gla_group_common(p_ref, wg, bg):
    lr_s = _split(p_ref[:, 2 * KW + 2 * VW:GLA_COLS])
    wg_s = _split(wg)
    z = _dot3s(lr_s, wg_s, _DIMS["nn"]) + bg
    la = (jnp.minimum(z, 0.0) - jnp.log(1.0 + jnp.exp(-jnp.abs(z)))) * (1.0 / GLA_TAU)
    return lr_s, wg_s, z, _split(la)


def _gla_chunk_common(p_ref, rows, la_s, low, ones_v):
    q = p_ref[rows, 0:KW]
    k = p_ref[rows, KW:2 * KW]
    v = p_ref[rows, 2 * KW:2 * KW + VW]
    g = p_ref[rows, 2 * KW + VW:2 * KW + 2 * VW]
    la_h, la_l = la_s[0][rows], la_s[1][rows]
    cum = jnp.dot(low, la_h, preferred_element_type=F32) + jnp.dot(low, la_l, preferred_element_type=F32)
    endb = cum[CHUNK - 1:CHUNK, :]
    w = jnp.exp(endb - cum)
    a_full = jnp.exp(lax.dot_general(la_h, ones_v, _DIMS["tn"], preferred_element_type=F32)
                     + lax.dot_general(la_l, ones_v, _DIMS["tn"], preferred_element_type=F32))
    return q, k, v, g, w, endb, a_full


def _gla_fwd(p, wg, bg, gn, consts, name):
    T = p.shape[0]
    rb = CHUNK * GLA_G
    ng = T // rb
    mask, hm, low = consts[:3]
    scale = GLA_DK ** -0.5

    def body(p_ref, wg_ref, bg_ref, gn_ref, m_ref, hm_ref, l_ref, o_ref, st_ref, s_ref):
        @pl.when(pl.program_id(0) == 0)
        def _():
            s_ref[...] = jnp.zeros_like(s_ref)

        wg_v, bg_v, gn_v = wg_ref[...], bg_ref[...], gn_ref[...]
        ones_v = jnp.ones((CHUNK, VW), BF16)
        s_new = s_ref[...]
        _, _, _, la_s = _gla_group_common(p_ref, wg_v, bg_v)
        outs = []
        for c in range(GLA_G):
            rows = slice(c * CHUNK, (c + 1) * CHUNK)
            q, k, v, _, w, _, a_full = _gla_chunk_common(p_ref, rows, la_s, l_ref[...], ones_v)
            kd = (k * w).astype(BF16)
            kv = lax.dot_general(kd, v.astype(BF16), _DIMS["tn"], preferred_element_type=F32) * m_ref[...]
            s_new = a_full * s_new + kv
            st_ref[c] = s_new
            outs.append(jnp.dot((q * scale).astype(BF16), s_new.astype(BF16), preferred_element_type=F32))
        s_ref[...] = s_new
        o = jnp.concatenate(outs, axis=0)
        g = p_ref[:, 2 * KW + VW:2 * KW + 2 * VW]
        ms = _dot_sel(o * o, hm_ref[...], _DIMS["nn"], "a") * (1.0 / GLA_DV)
        o_ref[...] = (o * lax.rsqrt(ms + EPS) * gn_v * (g * _sigmoid(g))).astype(o_ref.dtype)

    full = lambda shape: pl.BlockSpec(shape, lambda i: tuple(0 for _ in shape))
    vm = 2 * _nbytes((rb, GLA_COLS), F32) + 2 * _nbytes((GLA_G, KW, VW), F32) + 12 * _nbytes((KW, VW), F32)
    return _pallas(
        body,
        out_shape=(_out((T, VW), BF16), _out((T // CHUNK, KW, VW), F32)),
        grid=(ng,),
        in_specs=[pl.BlockSpec((rb, GLA_COLS), lambda i: (i, 0)), full((LRW, KW)), full((1, KW)), full((1, VW)),
                  full((KW, VW)), full((VW, VW)), full((CHUNK, CHUNK))],
        out_specs=(pl.BlockSpec((rb, VW), lambda i: (i, 0)), pl.BlockSpec((GLA_G, KW, VW), lambda i: (i, 0, 0))),
        scratch_shapes=[pltpu.VMEM((KW, VW), F32)],
        compiler_params=_cparams(("arbitrary",), vm),
        name=name,
    )(p, wg, bg, gn, mask, hm, low)


def _gla_bwd(p, dy, states, wg, bg, gn, consts, name):
    T = p.shape[0]
    rb = CHUNK * GLA_G
    ng = T // rb
    mask, hm, low = consts[:3]
    scale = GLA_DK ** -0.5

    def body(p_ref, dy_ref, st_ref, sp_ref, wg_ref, bg_ref, gn_ref, m_ref, hm_ref, l_ref,
             dp_ref, dwg_ref, dbg_ref, dgn_ref, ga_ref):
        step = pl.program_id(0)

        @pl.when(step == 0)
        def _():
            ga_ref[...] = jnp.zeros_like(ga_ref)
            dwg_ref[...] = jnp.zeros_like(dwg_ref)
            dbg_ref[...] = jnp.zeros_like(dbg_ref)
            dgn_ref[...] = jnp.zeros_like(dgn_ref)

        first_group = step == ng - 1
        wg_v, bg_v, gn_v = wg_ref[...], bg_ref[...], gn_ref[...]
        ones_v = jnp.ones((CHUNK, VW), BF16)
        ones_8 = jnp.ones((8, VW), BF16)
        ga = ga_ref[...]
        lr_s, wg_s, z_all, la_s = _gla_group_common(p_ref, wg_v, bg_v)
        qss = [(p_ref[c * CHUNK:(c + 1) * CHUNK, 0:KW] * scale).astype(BF16) for c in range(GLA_G)]
        o = jnp.concatenate([jnp.dot(qss[c], st_ref[c].astype(BF16), preferred_element_type=F32) for c in range(GLA_G)], axis=0)
        g = p_ref[:, 2 * KW + VW:2 * KW + 2 * VW]
        dyv = dy_ref[...].astype(F32)
        r = lax.rsqrt(_dot_sel(o * o, hm_ref[...], _DIMS["nn"], "a") * (1.0 / GLA_DV) + EPS)
        on = o * r
        sg = _sigmoid(g)
        silu = g * sg
        d_on = dyv * gn_v * silu
        dp_ref[:, 2 * KW + VW:2 * KW + 2 * VW] = (dyv * on * gn_v * (sg * (1.0 + g * (1.0 - sg)))).astype(dp_ref.dtype)
        dgn_ref[...] += jnp.sum(dyv * on * silu, axis=0, keepdims=True)
        mo = _dot_sel(o * d_on, hm_ref[...], _DIMS["nn"], "a") * (1.0 / GLA_DV)
        dob_all = (r * d_on - o * (r * r * r) * mo).astype(BF16)
        dzs = [None] * GLA_G
        for c in reversed(range(GLA_G)):
            rows = slice(c * CHUNK, (c + 1) * CHUNK)
            _, k, v, _, w, endb, a_full = _gla_chunk_common(p_ref, rows, la_s, l_ref[...], ones_v)
            s_n = st_ref[c]
            if c > 0:
                s_prev = st_ref[c - 1]
            else:
                s_prev = jnp.where(first_group, 0.0, sp_ref[0])
            kd = k * w
            dob = dob_all[rows]
            dq = lax.dot_general(dob, s_n.astype(BF16), _DIMS["nt"], preferred_element_type=F32) * scale
            g_n = lax.dot_general(qss[c], dob, _DIMS["tn"], preferred_element_type=F32) * m_ref[...] + ga
            d_a = _dot_sel(ones_8, g_n * s_prev, _DIMS["nt"], "b")[0:1, :]
            g_nb = g_n.astype(BF16)
            dkd = lax.dot_general(v.astype(BF16), g_nb, _DIMS["nt"], preferred_element_type=F32)
            dv = jnp.dot(kd.astype(BF16), g_nb, preferred_element_type=F32)
            e = dkd * kd
            d_end = jnp.sum(e, axis=0, keepdims=True) + d_a * jnp.exp(endb)
            dla = _dot_sel(l_ref[...], -e, _DIMS["tn"], "b") + d_end
            dzs[c] = dla * (1.0 - _sigmoid(z_all[rows])) * (1.0 / GLA_TAU)
            ga = a_full * g_n
            dp_ref[rows, 0:KW] = dq.astype(dp_ref.dtype)
            dp_ref[rows, KW:2 * KW] = (dkd * w).astype(dp_ref.dtype)
            dp_ref[rows, 2 * KW:2 * KW + VW] = dv.astype(dp_ref.dtype)
        ga_ref[...] = ga
        dz = jnp.concatenate(dzs, axis=0)
        dz_s = _split(dz)
        dp_ref[:, 2 * KW + 2 * VW:GLA_COLS] = _dot3s(dz_s, wg_s, _DIMS["nt"]).astype(dp_ref.dtype)
        dwg_ref[...] += _dot3s(lr_s, dz_s, _DIMS["tn"])
        dbg_ref[...] += jnp.sum(dz, axis=0, keepdims=True)

    full = lambda shape: pl.BlockSpec(shape, lambda i: tuple(0 for _ in shape))
    rev = lambda i: (ng - 1 - i, 0)
    vm = 4 * _nbytes((rb, GLA_COLS), F32) + 2 * _nbytes((rb, VW), F32) + 2 * _nbytes((GLA_G + 1, KW, VW), F32)
    vm += 16 * _nbytes((KW, VW), F32)
    return _pallas(
        body,
        out_shape=(_out((T, GLA_COLS), BF16), _out((LRW, KW), F32),
                   _out((1, KW), F32), _out((1, VW), F32)),
        grid=(ng,),
        in_specs=[pl.BlockSpec((rb, GLA_COLS), rev), pl.BlockSpec((rb, VW), rev),
                  pl.BlockSpec((GLA_G, KW, VW), lambda i: (ng - 1 - i, 0, 0)),
                  pl.BlockSpec((1, KW, VW), lambda i: (jnp.maximum((ng - 1 - i) * GLA_G - 1, 0), 0, 0)),
                  full((LRW, KW)), full((1, KW)), full((1, VW)), full((KW, VW)), full((VW, VW)), full((CHUNK, CHUNK))],
        out_specs=(pl.BlockSpec((rb, GLA_COLS), rev), full((LRW, KW)), full((1, KW)), full((1, VW))),
        scratch_shapes=[pltpu.VMEM((KW, VW), F32)],
        compiler_params=_cparams(("arbitrary",), vm),
        name=name,
    )(p, dy, states, states, wg, bg, gn, mask, hm, low)


CONV_TM = 512
HALO = 32
CONV_RB = 64


def _glu(u):
    a = u[:, 0:CW]
    b = u[:, CW:2 * CW]
    return a * _sigmoid(b)


def _conv_taps(buf_ref, w_ref, rb0, first_tap):
    acc = jnp.zeros((CONV_RB, CW), F32)
    for j in range(CK):
        s = rb0 + first_tap(j)
        acc = acc + w_ref[j:j + 1, :] * buf_ref[s:s + CONV_RB, :]
    return acc


def _ln_fwd(c, lg, lb):
    mu = jnp.mean(c, axis=-1, keepdims=True)
    xc = c - mu
    rstd = lax.rsqrt(jnp.mean(xc * xc, axis=-1, keepdims=True) + EPS)
    n = xc * rstd
    return n, rstd, n * lg + lb


def _conv_fwd(u, w, b, lg, lb, name):
    T = u.shape[0]
    tm = CONV_TM

    def body(u_ref, uh_ref, w_ref, b_ref, lg_ref, lb_ref, o_ref, c_ref, hbuf):
        i = pl.program_id(0)
        hbuf[0:HALO, :] = jnp.where(i > 0, _glu(uh_ref[...]), 0.0)
        hbuf[HALO:HALO + tm, :] = _glu(u_ref[...])
        for r in range(tm // CONV_RB):
            rows = slice(r * CONV_RB, (r + 1) * CONV_RB)
            acc = _conv_taps(hbuf, w_ref, r * CONV_RB, lambda j: HALO - (CK - 1) + j)
            c_ref[rows, :] = acc
            _, _, zz = _ln_fwd(acc + b_ref[...], lg_ref[...], lb_ref[...])
            o_ref[rows, :] = (zz * _sigmoid(zz)).astype(o_ref.dtype)

    vec = pl.BlockSpec((1, CW), lambda i: (0, 0))
    return _pallas(
        body,
        out_shape=(_out((T, CW), BF16), _out((T, CW), F32)),
        grid=(T // tm,),
        in_specs=[pl.BlockSpec((tm, CONV_COLS), lambda i: (i, 0)),
                  pl.BlockSpec((HALO, CONV_COLS), lambda i: (jnp.maximum(i * (tm // HALO) - 1, 0), 0)),
                  pl.BlockSpec((HALO, CW), lambda i: (0, 0)), vec, vec, vec],
        out_specs=(pl.BlockSpec((tm, CW), lambda i: (i, 0)), pl.BlockSpec((tm, CW), lambda i: (i, 0))),
        scratch_shapes=[pltpu.VMEM((tm + HALO, CW), F32)],
        compiler_params=_cparams(("arbitrary",), 8 * _nbytes((tm, CONV_COLS), F32)),
        name=name,
    )(u, u, w, b, lg, lb)


def _conv_bwd_dc(u, conv, dout, b, lg, lb, name):
    T = u.shape[0]
    tm = CONV_TM
    nsteps = T // tm

    def body(u_ref, uh_ref, c_ref, do_ref, b_ref, lg_ref, lb_ref, dc_ref, dw_ref, db_ref, dlg_ref, dlb_ref, hbuf, dwacc):
        i = pl.program_id(0)

        @pl.when(i == 0)
        def _():
            dwacc[...] = jnp.zeros_like(dwacc)
            db_ref[...] = jnp.zeros_like(db_ref)
            dlg_ref[...] = jnp.zeros_like(dlg_ref)
            dlb_ref[...] = jnp.zeros_like(dlb_ref)

        hbuf[0:HALO, :] = jnp.where(i > 0, _glu(uh_ref[...]), 0.0)
        hbuf[HALO:HALO + tm, :] = _glu(u_ref[...])
        for r in range(tm // CONV_RB):
            rows = slice(r * CONV_RB, (r + 1) * CONV_RB)
            n, rstd, zz = _ln_fwd(c_ref[rows, :] + b_ref[...], lg_ref[...], lb_ref[...])
            sg = _sigmoid(zz)
            dz = do_ref[rows, :].astype(F32) * (sg * (1.0 + zz * (1.0 - sg)))
            dlg_ref[...] += jnp.sum(dz * n, axis=0, keepdims=True)
            dlb_ref[...] += jnp.sum(dz, axis=0, keepdims=True)
            dn = dz * lg_ref[...]
            dc = rstd * (dn - jnp.mean(dn, axis=-1, keepdims=True) - n * jnp.mean(dn * n, axis=-1, keepdims=True))
            dc_ref[rows, :] = dc
            db_ref[...] += jnp.sum(dc, axis=0, keepdims=True)
            for j in range(CK):
                s = r * CONV_RB + HALO - (CK - 1) + j
                prod = dc * hbuf[s:s + CONV_RB, :]
                dwacc[j] += jnp.sum(prod.reshape(CONV_RB // 8, 8, CW), axis=0)

        @pl.when(i == nsteps - 1)
        def _():
            dw_ref[...] = jnp.sum(dwacc[...], axis=1)

    vec = pl.BlockSpec((1, CW), lambda i: (0, 0))
    return _pallas(
        body,
        out_shape=(_out((T, CW), F32), _out((HALO, CW), F32),
                   _out((1, CW), F32), _out((1, CW), F32), _out((1, CW), F32)),
        grid=(nsteps,),
        in_specs=[pl.BlockSpec((tm, CONV_COLS), lambda i: (i, 0)),
                  pl.BlockSpec((HALO, CONV_COLS), lambda i: (jnp.maximum(i * (tm // HALO) - 1, 0), 0)),
                  pl.BlockSpec((tm, CW), lambda i: (i, 0)), pl.BlockSpec((tm, CW), lambda i: (i, 0)), vec, vec, vec],
        out_specs=(pl.BlockSpec((tm, CW), lambda i: (i, 0)), pl.BlockSpec((HALO, CW), lambda i: (0, 0)), vec, vec, vec),
        scratch_shapes=[pltpu.VMEM((tm + HALO, CW), F32), pltpu.VMEM((HALO, 8, CW), F32)],
        compiler_params=_cparams(("arbitrary",), 10 * _nbytes((tm, CONV_COLS), F32)),
        name=name,
    )(u, u, conv, dout, b, lg, lb)


def _conv_bwd_du(u, dc, w, name):
    T = u.shape[0]
    tm = CONV_TM
    nsteps = T // tm

    def body(u_ref, dc_ref, dch_ref, w_ref, du_ref, dcbuf):
        i = pl.program_id(0)
        dcbuf[0:tm, :] = dc_ref[...]
        dcbuf[tm:tm + HALO, :] = jnp.where(i < nsteps - 1, dch_ref[...], 0.0)
        for r in range(tm // CONV_RB):
            rows = slice(r * CONV_RB, (r + 1) * CONV_RB)
            dh = _conv_taps(dcbuf, w_ref, r * CONV_RB, lambda j: (CK - 1) - j)
            a = u_ref[rows, 0:CW]
            sb = _sigmoid(u_ref[rows, CW:2 * CW])
            du_ref[rows, 0:CW] = (dh * sb).astype(du_ref.dtype)
            du_ref[rows, CW:2 * CW] = (dh * a * sb * (1.0 - sb)).astype(du_ref.dtype)

    return _pallas(
        body,
        out_shape=_out((T, CONV_COLS), BF16),
        grid=(nsteps,),
        in_specs=[pl.BlockSpec((tm, CONV_COLS), lambda i: (i, 0)),
                  pl.BlockSpec((tm, CW), lambda i: (i, 0)),
                  pl.BlockSpec((HALO, CW), lambda i: (jnp.minimum((i + 1) * (tm // HALO), T // HALO - 1), 0)),
                  pl.BlockSpec((HALO, CW), lambda i: (0, 0))],
        out_specs=pl.BlockSpec((tm, CONV_COLS), lambda i: (i, 0)),
        scratch_shapes=[pltpu.VMEM((tm + HALO, CW), F32)],
        compiler_params=_cparams(("arbitrary",), 8 * _nbytes((tm, CONV_COLS), F32)),
        name=name,
    )(u, dc, dc, w)


ATT_G = 4


def _att_load_kv(p_any, kbuf, vbuf, sems, T):
    kc = pltpu.make_async_copy(p_any.at[:, pl.ds(AW, AW)], kbuf.at[pl.ds(LEFT, T), :], sems.at[0])
    vc = pltpu.make_async_copy(p_any.at[:, pl.ds(2 * AW, AW)], vbuf.at[pl.ds(LEFT, T), :], sems.at[1])
    kc.start()
    vc.start()
    kbuf[0:LEFT, :] = jnp.zeros((LEFT, AW), BF16)
    vbuf[0:LEFT, :] = jnp.zeros((LEFT, AW), BF16)
    kc.wait()
    vc.wait()


ATT_QB = CHUNK * ATT_G
ATT_KB = LEFT + ATT_QB
REL_PAD = 384
TOEP = 1024


def _att_consts():
    m = np.arange(TOEP)
    d = ATT_KB - 1 - m
    idx = np.clip(d, -128, 128) + 128
    sel = (np.arange(REL_PAD)[:, None] == idx[None, :]) & (m[None, :] < ATT_QB + ATT_KB - 1)
    return jnp.asarray(sel.astype(np.float32))


def _att_build_bias(rel_ref, sel_ref, bias_scr):
    tr = jnp.dot(rel_ref[...], sel_ref[...], precision=HI, preferred_element_type=F32)
    qc = lax.broadcasted_iota(jnp.int32, (ATT_QB, ATT_KB), 0) // CHUNK
    kc = lax.broadcasted_iota(jnp.int32, (ATT_QB, ATT_KB), 1) // CHUNK
    band = (kc >= qc) & (kc <= qc + 8)
    for h in range(AH):
        rows = jnp.broadcast_to(tr[h:h + 1, :], (ATT_QB, TOEP))
        toep = pltpu.roll(rows, TOEP - (ATT_QB - 1), 1, stride=1, stride_axis=0)[:, 0:ATT_KB]
        bias_scr[h // 2, (h % 2) * ATT_QB:(h % 2 + 1) * ATT_QB, :] = jnp.where(band, toep, NEG)


def _att_probs(qst, kb, bias_p, n0):
    sc = lax.dot_general(qst, kb, _DIMS["nt"], preferred_element_type=F32) * (64 ** -0.5) + bias_p
    pos = lax.broadcasted_iota(jnp.int32, (2 * ATT_QB, ATT_KB), 1)
    sc = jnp.where(pos >= CHUNK * (8 - n0), sc, NEG)
    mx = jnp.max(sc, axis=-1, keepdims=True)
    ex = jnp.exp(sc - mx)
    return ex / jnp.sum(ex, axis=-1, keepdims=True)


def _head_stack(a2, lo):
    zero = jnp.zeros_like(a2)
    return jnp.concatenate([jnp.where(lo, a2, zero), jnp.where(lo, zero, a2)], axis=0)


def _att_fwd(p, rel, sel, name):
    T = p.shape[0]

    def body(q_ref, p_any, rel_ref, sel_ref, o_ref, kbuf, vbuf, bias_scr, sems):
        i = pl.program_id(0)

        @pl.when(i == 0)
        def _():
            _att_load_kv(p_any, kbuf, vbuf, sems, T)
            _att_build_bias(rel_ref, sel_ref, bias_scr)

        lo = lax.broadcasted_iota(jnp.int32, (ATT_QB, 128), 1) < 64
        n0 = i * ATT_G
        start = pl.multiple_of(i * ATT_QB, ATT_QB)
        for hp in range(AH // 2):
            cols = slice(hp * 128, (hp + 1) * 128)
            kb = kbuf[pl.ds(start, ATT_KB), cols]
            vb = vbuf[pl.ds(start, ATT_KB), cols]
            pr = _att_probs(_head_stack(q_ref[:, cols], lo), kb, bias_scr[hp], n0)
            pv = jnp.dot(pr.astype(BF16), vb, preferred_element_type=F32)
            o_ref[:, cols] = jnp.where(lo, pv[0:ATT_QB], pv[ATT_QB:2 * ATT_QB]).astype(o_ref.dtype)

    vm = 2 * _nbytes((T + LEFT, AW), BF16) + 8 * _nbytes((2 * ATT_QB, ATT_KB), F32) + (8 << 20)
    return _pallas(
        body,
        out_shape=_out((T, AW), BF16),
        grid=(T // ATT_QB,),
        in_specs=[pl.BlockSpec((ATT_QB, AW), lambda i: (i, 0)), pl.BlockSpec(memory_space=pl.ANY),
                  pl.BlockSpec((8, REL_PAD), lambda i: (0, 0)), pl.BlockSpec((REL_PAD, TOEP), lambda i: (0, 0))],
        out_specs=pl.BlockSpec((ATT_QB, AW), lambda i: (i, 0)),
        scratch_shapes=[pltpu.VMEM((T + LEFT, AW), BF16), pltpu.VMEM((T + LEFT, AW), BF16),
                        pltpu.VMEM((AH // 2, 2 * ATT_QB, ATT_KB), F32), pltpu.SemaphoreType.DMA((2,))],
        compiler_params=_cparams(("arbitrary",), vm),
        name=name,
    )(p, p, rel, sel)


def _att_bwd(p, do, rel, sel, name):
    T = p.shape[0]
    nsteps = T // ATT_QB

    def body(q_ref, p_any, do_ref, rel_ref, sel_ref, dp_any, drel_ref,
             kbuf, vbuf, dqbuf, dkbuf, dvbuf, bias_scr, dbias_scr, dtr_scr, sems):
        i = pl.program_id(0)

        @pl.when(i == 0)
        def _():
            _att_load_kv(p_any, kbuf, vbuf, sems, T)
            _att_build_bias(rel_ref, sel_ref, bias_scr)
            dkbuf[...] = jnp.zeros_like(dkbuf)
            dvbuf[...] = jnp.zeros_like(dvbuf)
            dbias_scr[...] = jnp.zeros_like(dbias_scr)

        lo = lax.broadcasted_iota(jnp.int32, (ATT_QB, 128), 1) < 64
        n0 = i * ATT_G
        start = pl.multiple_of(i * ATT_QB, ATT_QB)
        for hp in range(AH // 2):
            cols = slice(hp * 128, (hp + 1) * 128)
            kb = kbuf[pl.ds(start, ATT_KB), cols]
            vb = vbuf[pl.ds(start, ATT_KB), cols]
            qst = _head_stack(q_ref[:, cols], lo)
            dost = _head_stack(do_ref[:, cols].astype(BF16), lo)
            pr = _att_probs(qst, kb, bias_scr[hp], n0)
            dpr = lax.dot_general(dost, vb, _DIMS["nt"], preferred_element_type=F32)
            ds = pr * (dpr - jnp.sum(dpr * pr, axis=-1, keepdims=True))
            dbias_scr[hp] += ds
            dsb = (ds * (64 ** -0.5)).astype(BF16)
            dq = jnp.dot(dsb, kb, preferred_element_type=F32)
            dqbuf[pl.ds(start, ATT_QB), cols] = jnp.where(lo, dq[0:ATT_QB], dq[ATT_QB:2 * ATT_QB]).astype(BF16)
            dkbuf[pl.ds(start, ATT_KB), cols] += lax.dot_general(dsb, qst, _DIMS["tn"], preferred_element_type=F32)
            dvbuf[pl.ds(start, ATT_KB), cols] += lax.dot_general(pr.astype(BF16), dost, _DIMS["tn"], preferred_element_type=F32)

        @pl.when(i == nsteps - 1)
        def _():
            kbuf[pl.ds(LEFT, T), :] = dkbuf[pl.ds(LEFT, T), :].astype(BF16)
            vbuf[pl.ds(LEFT, T), :] = dvbuf[pl.ds(LEFT, T), :].astype(BF16)
            cps = [pltpu.make_async_copy(dqbuf, dp_any.at[:, pl.ds(0, AW)], sems.at[0]),
                   pltpu.make_async_copy(kbuf.at[pl.ds(LEFT, T), :], dp_any.at[:, pl.ds(AW, AW)], sems.at[1]),
                   pltpu.make_async_copy(vbuf.at[pl.ds(LEFT, T), :], dp_any.at[:, pl.ds(2 * AW, AW)], sems.at[2])]
            for cp in cps:
                cp.start()
            dtr_scr[...] = jnp.zeros_like(dtr_scr)
            ri = lax.broadcasted_iota(jnp.int32, (ATT_QB, ATT_QB), 0)
            ci = lax.broadcasted_iota(jnp.int32, (ATT_QB, ATT_QB), 1)
            flip = jnp.where(ri + ci == ATT_QB - 1, 1.0, 0.0)
            for h in range(AH):
                db = dbias_scr[h // 2, (h % 2) * ATT_QB:(h % 2 + 1) * ATT_QB, :]
                db = jnp.dot(flip, db, precision=HI, preferred_element_type=F32)
                wide = jnp.concatenate([db, jnp.zeros((ATT_QB, TOEP - ATT_KB), F32)], axis=1)
                diag = pltpu.roll(wide, 0, 1, stride=1, stride_axis=0)
                dtr_scr[h:h + 1, :] = jnp.sum(diag, axis=0, keepdims=True)
            drel_ref[...] = lax.dot_general(dtr_scr[...], sel_ref[...], _DIMS["nt"], precision=HI, preferred_element_type=F32)
            for cp in cps:
                cp.wait()

    vm = 3 * _nbytes((T + LEFT, AW), BF16) + 2 * _nbytes((T + LEFT, AW), F32) + 12 * _nbytes((2 * ATT_QB, ATT_KB), F32) + (8 << 20)
    return _pallas(
        body,
        out_shape=(_out((T, ATT_COLS), BF16), _out((8, REL_PAD), F32)),
        grid=(nsteps,),
        in_specs=[pl.BlockSpec((ATT_QB, AW), lambda i: (i, 0)), pl.BlockSpec(memory_space=pl.ANY),
                  pl.BlockSpec((ATT_QB, AW), lambda i: (i, 0)),
                  pl.BlockSpec((8, REL_PAD), lambda i: (0, 0)), pl.BlockSpec((REL_PAD, TOEP), lambda i: (0, 0))],
        out_specs=(pl.BlockSpec(memory_space=pl.ANY), pl.BlockSpec((8, REL_PAD), lambda i: (0, 0))),
        scratch_shapes=[pltpu.VMEM((T + LEFT, AW), BF16), pltpu.VMEM((T + LEFT, AW), BF16), pltpu.VMEM((T, AW), BF16),
                        pltpu.VMEM((T + LEFT, AW), F32), pltpu.VMEM((T + LEFT, AW), F32),
                        pltpu.VMEM((AH // 2, 2 * ATT_QB, ATT_KB), F32), pltpu.VMEM((AH // 2, 2 * ATT_QB, ATT_KB), F32),
                        pltpu.VMEM((8, TOEP), F32), pltpu.SemaphoreType.DMA((3,))],
        compiler_params=_cparams(("arbitrary",), vm),
        name=name,
    )(p, p, do, rel, sel)


def _layer_fwd(h, wl, consts, tag):
    p_gla, p_conv, p_att, xn = _mm_fan(h, [wl["w_gla"], wl["w_conv"], wl["w_att"]], mode="nt", out_dtypes=(F32, F32, BF16),
                                       norm_g=wl["norm_mix"], name=f"{tag}_proj")
    o_gla, states = _gla_fwd(p_gla, wl["wg"], wl["bg"], wl["gla_norm"], consts, f"{tag}_gla_fwd")
    o_conv, conv = _conv_fwd(p_conv, wl["w_dw"], wl["b_dw"], wl["ln_g"], wl["ln_b"], f"{tag}_conv_fwd")
    rel = jnp.pad(wl["rel_bias"], ((0, 8 - AH), (0, REL_PAD - N_REL)))
    o_att = _att_fwd(p_att, rel, consts[3], f"{tag}_att_fwd")
    h1 = _mm_sum([o_gla, o_conv, o_att], [wl["w_out_g"], wl["w_out_c"], wl["w_out_a"]], mode="nn", out_dtype=F32, extra=h,
                 name=f"{tag}_out")
    u, xn2 = _mm(h1, wl["w_up"], mode="nn", out_dtype=BF16, tm=1024, tn=1024, b_chips=True, norm_g=wl["norm_ffn"],
                 name=f"{tag}_mlp_up")
    h2 = _mm(u, wl["w_down"], mode="nn", out_dtype=F32, tm=1024, tk=2048, a_pro="relu2", epi="add", extra=h1,
             name=f"{tag}_mlp_down")
    saved = dict(h=h, xn=xn, p_gla=p_gla, p_conv=p_conv, p_att=p_att, states=states, o_gla=o_gla, o_conv=o_conv, conv=conv,
                 o_att=o_att, rel=rel, h1=h1, xn2=xn2, u=u)
    return h2, saved


def _layer_bwd(dh2, sv, wl, consts, tag, emit=lambda name, grad: None):
    g = {}
    du = _mm(dh2, wl["w_down"], mode="nt", out_dtype=BF16, tm=1024, tn=1024, epi="relu2grad", extra=sv["u"],
             name=f"{tag}_mlp_down_dx")
    g["w_down"] = _mm(sv["u"], dh2, mode="tn", out_dtype=F32, tm=2048, tn=1024, tk=1024, a_pro="relu2", name=f"{tag}_mlp_down_dw")
    emit("w_down", g["w_down"].reshape(4, D_FF // 4, D))
    dh1, g["norm_ffn"] = _mm(du, wl["w_up"], mode="nt", out_dtype=F32, tm=1024, tk=1024, b_chips=True,
                             norm_bwd=(sv["h1"], wl["norm_ffn"], dh2), name=f"{tag}_mlp_up_dx")
    g["w_up"] = _mm(sv["xn2"], du, mode="tn", out_dtype=F32, tm=1024, tn=1024, tk=1024, out_chips=True, name=f"{tag}_mlp_up_dw")
    emit("w_up", g["w_up"])
    d_gla, d_conv, d_att = _mm_fan(dh1, [wl["w_out_g"], wl["w_out_c"], wl["w_out_a"]], mode="nt", out_dtypes=(F32, F32, F32),
                                   name=f"{tag}_out_dx")
    g["w_out_g"], g["w_out_c"], g["w_out_a"] = _mm_tn_multi([sv["o_gla"], sv["o_conv"], sv["o_att"], dh1],
                                                            [(0, 3), (1, 3), (2, 3)], name=f"{tag}_out_dw")
    emit("w_out", jnp.concatenate([g["w_out_g"], g["w_out_c"], g["w_out_a"]], axis=0).reshape(4, D // 4, D))
    dp_gla, g["wg"], g["bg"], g["gla_norm"] = _gla_bwd(sv["p_gla"], d_gla, sv["states"], wl["wg"], wl["bg"], wl["gla_norm"],
                                                       consts, f"{tag}_gla_bwd")
    dc, g["w_dw"], g["b_dw"], g["ln_g"], g["ln_b"] = _conv_bwd_dc(sv["p_conv"], sv["conv"], d_conv, wl["b_dw"], wl["ln_g"],
                                                                  wl["ln_b"], f"{tag}_conv_bwd_dc")
    dp_conv = _conv_bwd_du(sv["p_conv"], dc, wl["w_dw"], f"{tag}_conv_bwd_du")
    dp_att, drel = _att_bwd(sv["p_att"], d_att, sv["rel"], consts[3], f"{tag}_att_bwd")
    g["rel_bias"] = drel[0:AH, 0:N_REL]
    g["w_gla"], g["w_conv"], g["w_att"] = _mm_tn_multi([sv["xn"], dp_gla, dp_conv, dp_att], [(1, 0), (2, 0), (3, 0)],
                                                       name=f"{tag}_proj_dw")
    emit("w_in", _join_w_in_t(g))
    dh, g["norm_mix"] = _mm_sum([dp_gla, dp_conv, dp_att], [wl["w_gla"], wl["w_conv"], wl["w_att"]], mode="nn", out_dtype=F32,
                                norm_bwd=(sv["h"], wl["norm_mix"], dh1), name=f"{tag}_proj_dx")
    return dh, g


def _local_step(x, target, layers, norm_final, emit=lambda layer, name, grad: None):
    consts = _gla_consts() + (_att_consts(),)
    h = x
    saved = []
    for l, wl in enumerate(layers):
        h, sv = _layer_fwd(h, wl, consts, f"l{l}")
        saved.append(sv)
    loss, dh, g_final = _final_loss(h, norm_final, target, "final_loss")
    grads = [None] * len(layers)
    for l in reversed(range(len(layers))):
        dh, grads[l] = _layer_bwd(dh, saved[l], layers[l], consts, f"l{l}", functools.partial(emit, l))
    return loss, dh, grads, g_final


ANY = pl.BlockSpec(memory_space=pl.ANY)


def _place():
    x, y, c = lax.axis_index("x"), lax.axis_index("y"), lax.axis_index("c")
    chips = [(1 - x, y), (x, 1 - y), (1 - x, 1 - y)]
    return x, y, c, chips


def _shape(shape, dtype):
    return jax.ShapeDtypeStruct(tuple(shape), dtype)


def _remote(src, dst, send_sem, recv_sem, to):
    return pltpu.make_async_remote_copy(src_ref=src, dst_ref=dst, send_sem=send_sem, recv_sem=recv_sem,
                                        device_id=to, device_id_type=MESH)


class _Staged:
    def __init__(self, src, dst, buf, sems):
        self.load = pltpu.make_async_copy(src, buf, sems.at[0])
        self.store = pltpu.make_async_copy(buf, dst, sems.at[1])

    def start(self):
        self.load.start()

    def wait(self):
        self.load.wait()
        self.store.start()
        self.store.wait()


def _phase_gather_ici(src, then):
    R, C = src.shape
    rh = R // 2

    def copies(ins, outs, sems):
        x, y, c, chips = _place()
        me = 2 * x + y
        local = _Staged(ins[0], outs[0].at[me], sems[3], sems[2])
        sends = [_remote(ins[0].at[pl.ds(c * rh, rh), :], outs[0].at[me, pl.ds(c * rh, rh), :], sems[0].at[k], sems[1].at[k],
                         (px, py, c)) for k, (px, py) in enumerate(chips)]
        recvs = [_remote(outs[0].at[2 * px + py, pl.ds(c * rh, rh), :], outs[0].at[2 * px + py, pl.ds(c * rh, rh), :],
                         sems[0].at[k], sems[1].at[k], (px, py, c)) for k, (px, py) in enumerate(chips)]
        return local, sends, recvs

    def start(ins, outs, sems):
        local, sends, _ = copies(ins, outs, sems)
        local.start()
        for cp in sends:
            cp.start()

    def finish(ins, outs, sems):
        local, sends, recvs = copies(ins, outs, sems)
        for cp in recvs:
            cp.wait_recv()
        for cp in sends:
            cp.wait_send()
        local.wait()

    return _Comm([src], [_shape((4, R, C), src.dtype)], {}, [(3,), (3,), (2,), ((R, C), src.dtype)], start, finish, then)


def _phase_gather_d2d(part, then):
    _, R, C = part.shape
    rh = R // 2

    def copies(ins, outs, sems):
        x, y, c, chips = _place()
        sends = [_remote(ins[0].at[2 * px + py, pl.ds(c * rh, rh), :], outs[0].at[2 * px + py, pl.ds(c * rh, rh), :],
                         sems[0].at[k], sems[1].at[k], (x, y, 1 - c)) for k, (px, py) in enumerate(chips)]
        recvs = [_remote(outs[0].at[2 * px + py, pl.ds((1 - c) * rh, rh), :], outs[0].at[2 * px + py, pl.ds((1 - c) * rh, rh), :],
                         sems[0].at[k], sems[1].at[k], (x, y, 1 - c)) for k, (px, py) in enumerate(chips)]
        return sends, recvs

    def start(ins, outs, sems):
        for cp in copies(ins, outs, sems)[0]:
            cp.start()

    def finish(ins, outs, sems):
        sends, recvs = copies(ins, outs, sems)
        for cp in recvs:
            cp.wait_recv()
        for cp in sends:
            cp.wait_send()

    return _Comm([part], [_shape(part.shape, part.dtype)], {0: 0}, [(3,), (3,)], start, finish, then)


def _phase_pair_exchange(full, then):
    _, R, C = full.shape
    rh = R // 2

    def copy(ins, outs, sems):
        x, y, c, _ = _place()
        return _remote(ins[0].at[:, pl.ds((1 - c) * rh, rh), :], outs[0], sems[0].at[0], sems[1].at[0], (x, y, 1 - c))

    return _Comm([full], [_shape((4, rh, C), full.dtype)], {}, [(1,), (1,)],
                 lambda ins, outs, sems: copy(ins, outs, sems).start(),
                 lambda ins, outs, sems: copy(ins, outs, sems).wait(), then)


def _phase_chip_scatter(parts, then):
    def copies(ins, outs, sems):
        x, y, c, chips = _place()
        me = 2 * x + y
        local = _Staged(ins[0].at[me], outs[0].at[me], sems[3], sems[2])
        sends = [_remote(ins[0].at[2 * px + py], outs[0].at[me], sems[0].at[k], sems[1].at[k], (px, py, c))
                 for k, (px, py) in enumerate(chips)]
        recvs = [_remote(outs[0].at[2 * px + py], outs[0].at[2 * px + py], sems[0].at[k], sems[1].at[k], (px, py, c))
                 for k, (px, py) in enumerate(chips)]
        return local, sends, recvs

    def start(ins, outs, sems):
        local, sends, _ = copies(ins, outs, sems)
        local.start()
        for cp in sends:
            cp.start()

    def finish(ins, outs, sems):
        local, sends, recvs = copies(ins, outs, sems)
        for cp in recvs:
            cp.wait_recv()
        for cp in sends:
            cp.wait_send()
        local.wait()

    return _Comm([parts], [_shape(parts.shape, parts.dtype)], {}, [(3,), (3,), (2,), (parts.shape[1:], parts.dtype)],
                 start, finish, then)


def _phase_pair_allgather(half, layer, depth, into, then):
    rh, C = half.shape

    def copies(ins, outs, sems):
        x, y, c, _ = _place()
        mine = outs[0].at[layer, pl.ds(c * rh, rh), :]
        theirs = outs[0].at[layer, pl.ds((1 - c) * rh, rh), :]
        return (_Staged(ins[0], mine, sems[3], sems[2]),
                _remote(ins[0], mine, sems[0].at[0], sems[1].at[0], (x, y, 1 - c)),
                _remote(theirs, theirs, sems[0].at[0], sems[1].at[0], (x, y, 1 - c)))

    def start(ins, outs, sems):
        local, send, _ = copies(ins, outs, sems)
        local.start()
        send.start()

    def finish(ins, outs, sems):
        local, send, recv = copies(ins, outs, sems)
        recv.wait_recv()
        send.wait_send()
        local.wait()

    ins = [half] if into is None else [half, into]
    return _Comm(ins, [_shape((depth, 2 * rh, C), half.dtype)], {} if into is None else {1: 0},
                 [(1,), (1,), (2,), ((rh, C), half.dtype)], start, finish, then)


def _comm_only(comms, name):
    plan = _Plan()
    for c in comms:
        plan.at(name, c)
    saved, _PLAN[0] = _PLAN[0], plan
    try:
        def body(o_ref):
            o_ref[...] = jnp.zeros_like(o_ref)

        _pallas(body, out_shape=[jax.ShapeDtypeStruct((8, 128), F32)], in_specs=[],
                out_specs=[pl.BlockSpec(memory_space=pltpu.VMEM)], name=name)()
    finally:
        _PLAN[0] = saved


def _row_tile(rows, cols, itemsize=4, budget=1 << 20, mult=8):
    fits = [t for t in range(mult, rows + 1, mult) if rows % t == 0 and t * cols * itemsize <= budget]
    return max(fits) if fits else rows


def _pair_add(full, got, c, name):
    _, L, R, C = full.shape
    rh = R // 2
    tr = _row_tile(rh, C, budget=2 << 20, mult=16)
    nb = rh // tr

    def body(c_ref, a_ref, b_ref, o_ref):
        o_ref[...] = (a_ref[...] + b_ref[...]).astype(o_ref.dtype)

    grid_spec = pltpu.PrefetchScalarGridSpec(
        num_scalar_prefetch=1,
        grid=(4, L, nb),
        in_specs=[pl.BlockSpec((1, 1, tr, C), lambda j, l, i, c_ref: (j, l, c_ref[0] * nb + i, 0)),
                  pl.BlockSpec((1, 1, tr, C), lambda j, l, i, c_ref: (j, l, i, 0))],
        out_specs=pl.BlockSpec((1, 1, tr, C), lambda j, l, i, c_ref: (j, l, i, 0)),
    )
    return _pallas(
        body,
        out_shape=_out((4, L, rh, C), BF16),
        grid_spec=grid_spec,
        compiler_params=_cparams(("parallel", "parallel", "parallel"), 8 * tr * C * 4),
        name=name,
    )(jnp.reshape(c, (1,)).astype(jnp.int32), full, got)


def _sum_chips(parts, name):
    _, L, rh, C = parts.shape
    tr = _row_tile(rh, C, budget=2 << 20, mult=16)

    def body(p_ref, o_ref):
        acc = p_ref[0].astype(F32)
        for j in range(1, 4):
            acc = acc + p_ref[j].astype(F32)
        o_ref[...] = acc

    return _pallas(
        body,
        out_shape=_out((L, rh, C), F32),
        grid=(L, rh // tr),
        in_specs=[pl.BlockSpec((4, 1, tr, C), lambda l, i: (0, l, i, 0))],
        out_specs=pl.BlockSpec((1, tr, C), lambda l, i: (l, i, 0)),
        compiler_params=_cparams(("parallel", "parallel"), 16 * tr * C * 4),
        name=name,
    )(parts)


def _allreduce_small(v):
    R = v.shape[0]

    def body(v_ref, o_ref, slots, send_sems, recv_sems):
        x, y, c, _ = _place()
        me = 4 * x + 2 * y + c
        slots[me] = v_ref[...]
        cps = []
        for r in range(1, 8):
            px, py, pc = x ^ (r >> 2), y ^ ((r >> 1) & 1), c ^ (r & 1)
            cps.append(pltpu.make_async_remote_copy(src_ref=v_ref, dst_ref=slots.at[me], send_sem=send_sems.at[r - 1],
                                                    recv_sem=recv_sems.at[r - 1], device_id=(px, py, pc), device_id_type=MESH))
            cps[-1].start()
        for r in range(1, 8):
            px, py, pc = x ^ (r >> 2), y ^ ((r >> 1) & 1), c ^ (r & 1)
            theirs = slots.at[4 * px + 2 * py + pc]
            pltpu.make_async_remote_copy(src_ref=theirs, dst_ref=theirs, send_sem=send_sems.at[r - 1], recv_sem=recv_sems.at[r - 1],
                                         device_id=(px, py, pc), device_id_type=MESH).wait_recv()
        acc = slots[0]
        for j in range(1, 8):
            acc = acc + slots[j]
        o_ref[...] = acc
        for cp in cps:
            cp.wait_send()

    return pl.pallas_call(
        body,
        out_shape=jax.ShapeDtypeStruct(v.shape, F32),
        in_specs=[pl.BlockSpec(memory_space=pltpu.VMEM)],
        out_specs=pl.BlockSpec(memory_space=pltpu.VMEM),
        scratch_shapes=[pltpu.VMEM((8, R, 128), F32), pltpu.SemaphoreType.DMA((7,)), pltpu.SemaphoreType.DMA((7,))],
        name="allreduce_small",
    )(v)


def _adamw_math(w, g, m, v):
    m = ADAM_B1 * m + (1.0 - ADAM_B1) * g
    v = ADAM_B2 * v + (1.0 - ADAM_B2) * (g * g)
    m_hat = m / (1.0 - ADAM_B1 ** ADAM_STEP)
    v_hat = v / (1.0 - ADAM_B2 ** ADAM_STEP)
    delta = -ADAM_LR * (m_hat / (jnp.sqrt(v_hat) + ADAM_EPS) + ADAM_WD * w)
    return delta, m, v


def _adamw(ws, gs, ms, vs, name, budget=1 << 19):
    n = len(ws)
    tiles = [_row_tile(w.shape[1], w.shape[2], budget=budget) for w in ws]
    per_layer = [w.shape[1] // t for w, t in zip(ws, tiles)]
    steps = [w.shape[0] * p for w, p in zip(ws, per_layer)]
    starts = [sum(steps[:k]) for k in range(n)]

    def body(*refs):
        i = pl.program_id(0)
        for k in range(n):
            w_ref, g_ref, m_ref, v_ref = (refs[j * n + k] for j in range(4))
            outs = [refs[(4 + j) * n + k] for j in range(3)]

            @pl.when((i >= starts[k]) & (i < starts[k] + steps[k]))
            def _(w_ref=w_ref, g_ref=g_ref, m_ref=m_ref, v_ref=v_ref, outs=outs):
                outs[0][...], outs[1][...], outs[2][...] = _adamw_math(w_ref[...], g_ref[...], m_ref[...], v_ref[...])

    def spec(k):
        def index(i):
            local = jnp.clip(i - starts[k], 0, steps[k] - 1)
            return local // per_layer[k], local % per_layer[k], 0
        return pl.BlockSpec((None, tiles[k], ws[k].shape[2]), index)

    specs = [spec(k) for k in range(n)]
    outs = [_out(w.shape, F32) for w in ws]
    res = _pallas(
        body,
        out_shape=tuple(outs * 3),
        grid=(sum(steps),),
        in_specs=specs * 4,
        out_specs=tuple(specs * 3),
        compiler_params=_cparams(("arbitrary",), sum(16 * t * w.shape[2] * 4 for w, t in zip(ws, tiles))),
        name=name,
    )(*ws, *gs, *ms, *vs)
    return res[:n], res[n:2 * n], res[2 * n:]


def _adamw_slabs(w, g, m, v, name, slabs=59):
    n, L, C = w.shape
    assert n % slabs == 0

    def body(w_ref, g_ref, m_ref, v_ref, d_ref, nm_ref, nv_ref):
        d_ref[...], nm_ref[...], nv_ref[...] = _adamw_math(w_ref[...], g_ref[...], m_ref[...], v_ref[...])

    blk = pl.BlockSpec((slabs, L, C), lambda i: (i, 0, 0))
    out = _out(w.shape, F32)
    return _pallas(
        body,
        out_shape=(out, out, out),
        grid=(n // slabs,),
        in_specs=[blk] * 4,
        out_specs=(blk, blk, blk),
        compiler_params=_cparams(("parallel",), 16 * slabs * 8 * C * 4),
        name=name,
    )(w, g, m, v)


def _adamw_small(ws, gs, ms, vs):
    n = len(ws)

    def body(*refs):
        for t in range(n):
            w_ref, g_ref, m_ref, v_ref = (refs[k * n + t] for k in range(4))
            d_ref, nm_ref, nv_ref = (refs[(4 + k) * n + t] for k in range(3))
            d_ref[...], nm_ref[...], nv_ref[...] = _adamw_math(w_ref[...], g_ref[...], m_ref[...], v_ref[...])

    vmem = pl.BlockSpec(memory_space=pltpu.VMEM)
    outs = [jax.ShapeDtypeStruct(w.shape, F32) for w in ws]
    res = pl.pallas_call(
        body,
        out_shape=outs * 3,
        in_specs=[vmem] * (4 * n),
        out_specs=[vmem] * (3 * n),
        name="adamw_small",
    )(*ws, *gs, *ms, *vs)
    return res[:n], res[n:2 * n], res[2 * n:]


IN_SIZES = (192, 192, 384, 384, 16, 512, 384, 384, 384)
IN_OFFS = tuple(int(v) for v in np.cumsum((0,) + IN_SIZES))
SMALL = ("norm_mix", "w_gla_gate", "b_gla_gate", "gla_norm", "b_dw", "conv_ln_g", "conv_ln_b", "rel_bias", "norm_ffn")


def _pad_cols(a, n):
    return jnp.pad(a, ((0, 0), (0, n - a.shape[1])))


W_IN_SHARD = 708
W_IN_ROWS = 736


def _pad_rows(a, n):
    return jnp.pad(a, ((0, n - a.shape[0]), (0, 0)))


def _split_w_in_t(w):
    s = [w[IN_OFFS[i]:IN_OFFS[i + 1]] for i in range(9)]
    w_gla = jnp.concatenate([_pad_rows(s[0], KW), _pad_rows(s[1], KW), s[2], s[3], _pad_rows(s[4], LRW)], axis=0)
    return w_gla, s[5], jnp.concatenate(s[6:9], axis=0)


def _join_w_in_t(g):
    gg = g["w_gla"]
    full = jnp.concatenate([gg[0:192], gg[KW:KW + 192], gg[2 * KW:2 * KW + VW], gg[2 * KW + VW:2 * KW + 2 * VW],
                            gg[2 * KW + 2 * VW:2 * KW + 2 * VW + 16], g["w_conv"], g["w_att"]], axis=0)
    return jnp.pad(full.reshape(4, W_IN_SHARD, D), ((0, 0), (0, W_IN_ROWS - W_IN_SHARD), (0, 0)))


def _pack(arrs, rows):
    flat = jnp.concatenate([a.reshape(-1) for a in arrs])
    return jnp.pad(flat, (0, rows * 128 - flat.shape[0])).reshape(rows, 128)


def _unpack(packed, shapes):
    flat = packed.reshape(-1)
    out, off = [], 0
    for s in shapes:
        n = int(np.prod(s))
        out.append(flat[off:off + n].reshape(s))
        off += n
    return out


def kernel(x, norm_mix, w_in, w_gla_gate, b_gla_gate, gla_norm, w_dw, b_dw, conv_ln_g, conv_ln_b, rel_bias, w_out, norm_ffn, w_up, w_down, norm_final, loss_target, m_norm_mix, m_w_in, m_w_gla_gate, m_b_gla_gate, m_gla_norm, m_w_dw, m_b_dw, m_conv_ln_g, m_conv_ln_b, m_rel_bias, m_w_out, m_norm_ffn, m_w_up, m_w_down, m_norm_final, v_norm_mix, v_w_in, v_w_gla_gate, v_b_gla_gate, v_gla_norm, v_w_dw, v_b_dw, v_conv_ln_g, v_conv_ln_b, v_rel_bias, v_w_out, v_norm_ffn, v_w_up, v_w_down, v_norm_final):
    P = dict(norm_mix=norm_mix, w_in=w_in, w_gla_gate=w_gla_gate, b_gla_gate=b_gla_gate, gla_norm=gla_norm, w_dw=w_dw, b_dw=b_dw,
             conv_ln_g=conv_ln_g, conv_ln_b=conv_ln_b, rel_bias=rel_bias, w_out=w_out, norm_ffn=norm_ffn, w_up=w_up,
             w_down=w_down, norm_final=norm_final)
    Mo = dict(norm_mix=m_norm_mix, w_in=m_w_in, w_gla_gate=m_w_gla_gate, b_gla_gate=m_b_gla_gate, gla_norm=m_gla_norm, w_dw=m_w_dw,
              b_dw=m_b_dw, conv_ln_g=m_conv_ln_g, conv_ln_b=m_conv_ln_b, rel_bias=m_rel_bias, w_out=m_w_out, norm_ffn=m_norm_ffn,
              w_up=m_w_up, w_down=m_w_down, norm_final=m_norm_final)
    Vo = dict(norm_mix=v_norm_mix, w_in=v_w_in, w_gla_gate=v_w_gla_gate, b_gla_gate=v_b_gla_gate, gla_norm=v_gla_norm, w_dw=v_w_dw,
              b_dw=v_b_dw, conv_ln_g=v_conv_ln_g, conv_ln_b=v_conv_ln_b, rel_bias=v_rel_bias, w_out=v_w_out, norm_ffn=v_norm_ffn,
              w_up=v_w_up, w_down=v_w_down, norm_final=v_norm_final)
    depth = w_in.shape[0]
    xi, yi, ci = lax.axis_index("x"), lax.axis_index("y"), lax.axis_index("c")
    chip = 2 * xi + yi

    plan = _Plan()
    _PLAN[0] = plan
    layers = [dict(
        norm_mix=norm_mix[l][None], wg=jnp.pad(w_gla_gate[l], ((0, LRW - 16), (0, KW - 192))),
        bg=_pad_cols(b_gla_gate[l][None], KW), gla_norm=gla_norm[l][None], b_dw=b_dw[l][None], ln_g=conv_ln_g[l][None],
        ln_b=conv_ln_b[l][None], rel_bias=rel_bias[l], norm_ffn=norm_ffn[l][None]) for l in range(depth)]

    w_in_t, m_w_in_t, v_w_in_t = (jnp.transpose(a, (2, 0, 1)) for a in (w_in, m_w_in, v_w_in))

    def w_in_shard(l):
        return _pad_rows(w_in_t[:, l, :], W_IN_ROWS).astype(BF16)

    def have_w_in(l, full):
        rows = jnp.concatenate([full[j, 0:W_IN_SHARD] for j in range(4)], axis=0)
        layers[l]["w_gla"], layers[l]["w_conv"], layers[l]["w_att"] = _split_w_in_t(rows)

    def have_w_out(l, full):
        w = full.reshape(D, D)
        layers[l]["w_out_g"], layers[l]["w_out_c"], layers[l]["w_out_a"] = w[0:VW], w[VW:VW + CW], w[VW + CW:]

    def have_w_up(l, full):
        layers[l]["w_up"] = full

    def have_w_down(l, full):
        layers[l]["w_down"] = full.reshape(D_FF, D)

    def have_w_dw(full):
        taps = full.reshape(4, depth, HALO, CW // 4)
        for l in range(depth):
            layers[l]["w_dw"] = jnp.transpose(taps[:, l], (1, 0, 2)).reshape(HALO, CW)

    first_d2d = []

    def first_ici(shard, have):
        return _phase_gather_ici(shard, lambda outs: first_d2d.append(_phase_gather_d2d(outs[0], lambda done: have(done[0]))))

    w_dw_pad = jnp.pad(w_dw, ((0, 0), (0, HALO - CK), (0, 0))).reshape(depth * HALO, CW // 4)
    _comm_only([first_ici(w_in_shard(0), functools.partial(have_w_in, 0)), first_ici(w_dw_pad, have_w_dw)],
               "gather_first_ici")
    _comm_only(first_d2d, "gather_first_d2d")

    def gather_behind(shard, ici_call, d2d_call, have):
        plan.at(ici_call, _phase_gather_ici(
            shard, lambda outs: plan.at(d2d_call, _phase_gather_d2d(outs[0], lambda done: have(done[0])))))

    for l in range(depth):
        if l > 0:
            gather_behind(w_in_shard(l), f"l{l - 1}_mlp_up", f"l{l - 1}_mlp_down", functools.partial(have_w_in, l))
        gather_behind(w_out[l].astype(BF16), f"l{l - 1}_mlp_down" if l > 0 else "l0_proj", f"l{l}_gla_fwd",
                      functools.partial(have_w_out, l))
        if l > 0:
            gather_behind(w_up[l].astype(BF16), f"l{l}_proj", f"l{l}_gla_fwd", functools.partial(have_w_up, l))
            gather_behind(w_down[l].astype(BF16), f"l{l}_gla_fwd", f"l{l}_att_fwd", functools.partial(have_w_down, l))
        else:
            gather_behind(w_up[l].astype(BF16), f"l{l}_gla_fwd", f"l{l}_att_fwd", functools.partial(have_w_up, l))
            gather_behind(w_down[l].astype(BF16), f"l{l}_att_fwd", f"l{l}_mlp_up", functools.partial(have_w_down, l))

    reduced = {}
    last_swap = []

    def reduce_calls(name, l):
        if name == "w_down":
            return f"l{l}_mlp_up_dx", f"l{l}_gla_bwd", f"l{l}_conv_bwd_dc"
        if name == "w_up":
            return f"l{l}_out_dx", f"l{l}_att_bwd", f"l{l}_proj_dw"
        if name == "w_out":
            return f"l{l}_gla_bwd", f"l{l}_conv_bwd_dc", f"l{l}_att_bwd"
        if l > 0:
            return f"l{l}_proj_dx", f"l{l - 1}_mlp_down_dw", f"l{l - 1}_mlp_up_dx"
        return None, "l0_proj_dx", None

    def reduce_behind(l, name, full):
        calls = reduce_calls(name, l)

        def swapped(outs):
            pair = _pair_add(full[:, None], outs[0][:, None], ci, f"reduce_pair_add_{name}{l}")[:, 0]
            plan.at(calls[1], _phase_chip_scatter(pair, scattered))

        def scattered(outs):
            half = _sum_chips(outs[0][:, None], f"reduce_sum_chips_{name}{l}")[0]
            phase = _phase_pair_allgather(half, l, depth, reduced.get(name), gathered)
            if calls[2] is None:
                last_swap.append(phase)
            else:
                plan.at(calls[2], phase)

        def gathered(outs):
            reduced[name] = outs[0]

        if calls[0] is None:
            _comm_only([_phase_pair_exchange(full, swapped)], f"reduce_pair_exchange_{name}{l}")
        else:
            plan.at(calls[0], _phase_pair_exchange(full, swapped))

    loss_part, grad_x, grads, g_final = _local_step(x[0], loss_target[0], layers, norm_final[None], reduce_behind)

    G, delta, new_m, new_v = {}, {}, {}, {}
    early = ("w_down", "w_up", "w_out")
    for name in early:
        G[name] = reduced[name]
    ds, nms, nvs = _adamw([P[k] for k in early], [G[k] for k in early], [Mo[k] for k in early], [Vo[k] for k in early],
                          "adamw_early")
    for i, name in enumerate(early):
        delta[name], new_m[name], new_v[name] = ds[i], nms[i], nvs[i]
    _PLAN[0] = None
    assert not plan.by_call, sorted(plan.by_call)

    small_g = []
    for l in range(depth):
        g = grads[l]
        small_g += [g["norm_mix"], g["wg"][0:16, 0:192], g["bg"][:, 0:192], g["gla_norm"], g["b_dw"], g["ln_g"], g["ln_b"],
                    g["rel_bias"], g["norm_ffn"], g["w_dw"][0:CK]]
    small_g += [g_final, loss_part]
    small_shapes = [a.shape for a in small_g]
    n_small = sum(int(np.prod(s)) for s in small_shapes)
    rows = -(-n_small // 1024) * 8
    red = _unpack(_allreduce_small(_pack(small_g, rows)), small_shapes)
    per = len(SMALL) + 1
    for i, name in enumerate(SMALL):
        G[name] = jnp.stack([red[l * per + i].reshape(P[name].shape[1:]) for l in range(depth)])
    gw_dw_all = jnp.stack([red[l * per + len(SMALL)] for l in range(depth)])
    G["w_dw"] = lax.dynamic_slice_in_dim(gw_dw_all, chip * (CW // 4), CW // 4, axis=2)
    G["norm_final"] = red[-2].reshape(norm_final.shape)
    loss = red[-1][0, 0]

    _comm_only(last_swap, "reduce_pair_allgather_last")
    back = lambda a: jnp.transpose(a, (1, 2, 0))
    g_in_t = jnp.transpose(reduced["w_in"][:, 0:W_IN_SHARD, :], (1, 0, 2))
    d_in, nm_in, nv_in = _adamw_slabs(w_in_t, g_in_t, m_w_in_t, v_w_in_t, "adamw_w_in")
    G["w_in"], delta["w_in"], new_m["w_in"], new_v["w_in"] = back(g_in_t), back(d_in), back(nm_in), back(nv_in)

    small_names = list(SMALL) + ["w_dw", "norm_final"]
    two_d = lambda a: a.reshape(-1, a.shape[-1])
    ds, nms, nvs = _adamw_small([two_d(P[k]) for k in small_names], [two_d(G[k]) for k in small_names],
                                [two_d(Mo[k]) for k in small_names], [two_d(Vo[k]) for k in small_names])
    for i, name in enumerate(small_names):
        shp = P[name].shape
        delta[name], new_m[name], new_v[name] = ds[i].reshape(shp), nms[i].reshape(shp), nvs[i].reshape(shp)

    order = ["norm_mix", "w_in", "w_gla_gate", "b_gla_gate", "gla_norm", "w_dw", "b_dw", "conv_ln_g", "conv_ln_b", "rel_bias",
             "w_out", "norm_ffn", "w_up", "w_down", "norm_final"]
    return (loss, grad_x[None], *[G[k] for k in order], *[delta[k] for k in order], *[new_m[k] for k in order],
            *[new_v[k] for k in order])
```

```python
import functools

import numpy as np
import jax
import jax.numpy as jnp
from jax import lax
from jax.experimental import pallas as pl
from jax.experimental.pallas import tpu as pltpu

F32 = jnp.float32
BF16 = jnp.bfloat16
HI = lax.Precision.HIGHEST

D = 1024
CHUNK = 64
GLA_DK, GLA_DV, GLA_H = 48, 96, 4
KW = 256
VW = 384
LRW = 128
GLA_TAU = 16.0
CW = 256
CK = 31
AW = 384
AH = 6
BAND = 576
LEFT = 512
D_FF = 4096
EPS = 1e-6
NEG = -1e30
N_REL = 257

GLA_COLS = 2 * KW + 2 * VW + LRW
CONV_COLS = 2 * CW
ATT_COLS = 3 * AW

ADAM_LR, ADAM_B1, ADAM_B2, ADAM_EPS, ADAM_WD, ADAM_STEP = 0.001, 0.9, 0.999, 1e-08, 0.01, 10

VMEM_CAP = 56 * 1024 * 1024
MESH = pl.DeviceIdType.MESH


def _cparams(sem, vmem_bytes):
    limit = int(min(VMEM_CAP, max(vmem_bytes * 5 // 4 + (4 << 20), 16 << 20)))
    return pltpu.CompilerParams(dimension_semantics=sem, vmem_limit_bytes=limit)


def _out(shape, dtype):
    return pltpu.HBM(tuple(shape), dtype)


class _Comm:
    def __init__(self, ins, outs, aliases, sems, start, finish, then=None):
        self.ins, self.outs, self.aliases, self.sems = list(ins), list(outs), dict(aliases), list(sems)
        self.start, self.finish, self.then = start, finish, then


class _Plan:
    def __init__(self):
        self.by_call = {}

    def at(self, call, comm):
        self.by_call.setdefault(call, []).append(comm)

    def take(self, call):
        return self.by_call.pop(call, [])


_PLAN = [None]


def _pin(a):
    return pltpu.with_memory_space_constraint(a, pltpu.HBM) if jnp.issubdtype(a.dtype, jnp.floating) else a


def _pallas(body, **kw):
    comms = _PLAN[0].take(kw.get("name")) if _PLAN[0] is not None else []
    if not comms:
        call = pl.pallas_call(body, **kw)
        return lambda *args: call(*[_pin(a) for a in args])

    grid = tuple(kw.get("grid", ()))
    single = not isinstance(kw["out_shape"], (tuple, list))
    out_shape = [kw["out_shape"]] if single else list(kw["out_shape"])
    out_specs = [kw["out_specs"]] if single else list(kw["out_specs"])
    in_specs = list(kw["in_specs"])
    scratch = list(kw.get("scratch_shapes", ()))
    n_in, n_out, n_scr = len(in_specs), len(out_shape), len(scratch)
    c_in = sum(len(c.ins) for c in comms)
    c_out = sum(len(c.outs) for c in comms)
    aliases = dict(kw.get("input_output_aliases", {}))
    i0, o0 = n_in, n_out
    for c in comms:
        for i, o in c.aliases.items():
            aliases[i0 + i] = o0 + o
        i0 += len(c.ins)
        o0 += len(c.outs)

    def wrapped(*refs):
        ins, c_ins = refs[:n_in], refs[n_in:n_in + c_in]
        outs, c_outs = refs[n_in + c_in:n_in + c_in + n_out], refs[n_in + c_in + n_out:n_in + c_in + n_out + c_out]
        scr, c_sems = refs[n_in + c_in + n_out + c_out:][:n_scr], refs[n_in + c_in + n_out + c_out + n_scr:]

        def each(what):
            i0 = o0 = s0 = 0
            for c in comms:
                getattr(c, what)(c_ins[i0:i0 + len(c.ins)], c_outs[o0:o0 + len(c.outs)], c_sems[s0:s0 + len(c.sems)])
                i0, o0, s0 = i0 + len(c.ins), o0 + len(c.outs), s0 + len(c.sems)

        if grid:
            first = functools.reduce(jnp.logical_and, [pl.program_id(a) == 0 for a in range(len(grid))])
            last = functools.reduce(jnp.logical_and, [pl.program_id(a) == grid[a] - 1 for a in range(len(grid))])
            pl.when(first)(lambda: each("start"))
            body(*ins, *outs, *scr)
            pl.when(last)(lambda: each("finish"))
        else:
            each("start")
            body(*ins, *outs, *scr)
            each("finish")

    kw = dict(kw)
    kw["in_specs"] = in_specs + [ANY] * c_in
    kw["out_shape"] = out_shape + [_out(s.shape, s.dtype) for c in comms for s in c.outs]
    kw["out_specs"] = out_specs + [ANY] * c_out
    staging = [s for c in comms for s in c.sems if len(s) == 2 and not isinstance(s[1], int)]
    kw["scratch_shapes"] = scratch + [pltpu.VMEM(*s) if s in staging else pltpu.SemaphoreType.DMA(s) for c in comms for s in c.sems]
    kw["input_output_aliases"] = aliases
    extra = sum(_nbytes(*s) for s in staging)
    old = kw.get("compiler_params")
    limit = (old.vmem_limit_bytes if old is not None else 16 << 20) + extra
    kw["compiler_params"] = pltpu.CompilerParams(
        dimension_semantics=old.dimension_semantics if old is not None else None, vmem_limit_bytes=int(min(VMEM_CAP, limit)))
    call = pl.pallas_call(wrapped, **kw)

    def run(*args):
        res = call(*[_pin(a) for a in args], *[_pin(a) for c in comms for a in c.ins])
        o0 = n_out
        for c in comms:
            if c.then is not None:
                c.then(res[o0:o0 + len(c.outs)])
            o0 += len(c.outs)
        return res[0] if single else res[:n_out]

    return run


def _nbytes(shape, dtype):
    return int(np.prod(shape)) * jnp.dtype(dtype).itemsize


def _sigmoid(x):
    return 1.0 / (1.0 + jnp.exp(-x))


_DIMS = {"nn": (((1,), (0,)), ((), ())), "nt": (((1,), (1,)), ((), ())), "tn": (((0,), (0,)), ((), ()))}


def _mm(a, b, *, mode, out_dtype, name, tm=512, tn=None, tk=None, a_pro=None, epi=None, extra=None,
        b_chips=False, out_chips=False, norm_g=None, norm_bwd=None):
    b2 = (b.shape[1], 4 * b.shape[2]) if b_chips else b.shape
    if mode == "nn":
        (M, K), (K2, N) = a.shape, b2
    elif mode == "nt":
        (M, K), (N, K2) = a.shape, b2
    else:
        (K, M), (K2, N) = a.shape, b2
    assert K == K2, (a.shape, b.shape, mode)
    tm = min(tm, M)
    tn = N if tn is None else min(tn, N)
    tk = K if tk is None else min(tk, K)
    assert M % tm == 0 and N % tn == 0 and K % tk == 0, (M, N, K, tm, tn, tk)
    nk = K // tk
    a_blk = (tk, tm) if mode == "tn" else (tm, tk)
    a_map = (lambda i, j, k: (k, i)) if mode == "tn" else (lambda i, j, k: (i, k))
    b_blk = (tn, tk) if mode == "nt" else (tk, tn)
    b_map = (lambda i, j, k: (j, k)) if mode == "nt" else (lambda i, j, k: (k, j))
    if b_chips:
        per = b.shape[2] // b_blk[1]
        assert b.shape[2] % b_blk[1] == 0 and mode != "tn"
        flat_map = b_map
        b_map = lambda i, j, k: (flat_map(i, j, k)[1] // per, flat_map(i, j, k)[0], flat_map(i, j, k)[1] % per)
        b_blk = (None,) + b_blk
    in_specs = [pl.BlockSpec(a_blk, a_map), pl.BlockSpec(b_blk, b_map)]
    args = [a, b]
    if epi is not None:
        in_specs.append(pl.BlockSpec((tm, tn), lambda i, j, k: (i, j)))
        args.append(extra)
    if norm_g is not None:
        assert nk == 1 and mode != "tn"
        in_specs.append(pl.BlockSpec((1, K), lambda i, j, k: (0, 0)))
        args.append(norm_g)
    if norm_bwd is not None:
        assert tn == N
        row = pl.BlockSpec((tm, N), lambda i, j, k: (i, 0))
        in_specs += [row, pl.BlockSpec((1, N), lambda i, j, k: (0, 0)), row]
        args += list(norm_bwd)

    def body(*refs):
        it = iter(refs)
        a_ref, b_ref = next(it), next(it)
        e_ref = next(it) if epi is not None else None
        ng_ref = next(it) if norm_g is not None else None
        h_ref, g_ref, dres_ref = (next(it), next(it), next(it)) if norm_bwd is not None else (None, None, None)
        o_ref = next(it)
        xn_ref = next(it) if norm_g is not None else None
        dg_ref = next(it) if norm_bwd is not None else None
        if norm_bwd is not None:
            @pl.when((pl.program_id(0) == 0) & (pl.program_id(2) == 0))
            def _():
                dg_ref[...] = jnp.zeros_like(dg_ref)

        av = a_ref[...]
        if a_pro == "relu2":
            af = jnp.maximum(av.astype(F32), 0.0)
            av = af * af
        if norm_g is not None:
            av = (av * lax.rsqrt(jnp.mean(av * av, axis=-1, keepdims=True) + EPS) * ng_ref[...]).astype(BF16)
            xn_ref[...] = av
        p = lax.dot_general(av.astype(BF16), b_ref[...].astype(BF16), _DIMS[mode], preferred_element_type=F32)

        def finish(acc):
            if epi == "add":
                acc = acc + e_ref[...].astype(F32)
            elif epi == "relu2grad":
                acc = acc * (2.0 * jnp.maximum(e_ref[...].astype(F32), 0.0))
            if norm_bwd is None:
                o_ref[...] = acc.astype(o_ref.dtype)
                return

            x = h_ref[...]
            r = lax.rsqrt(jnp.mean(x * x, axis=-1, keepdims=True) + EPS)
            gy = acc * g_ref[...]
            dot = jnp.mean(x * gy, axis=-1, keepdims=True)
            o_ref[...] = dres_ref[...] + r * gy - x * (r * r * r * dot)
            dg_ref[...] += jnp.sum(acc * x * r, axis=0, keepdims=True)

        if nk == 1:
            finish(p)
        else:
            acc_ref = refs[-1]
            k = pl.program_id(2)

            @pl.when(k == 0)
            def _():
                acc_ref[...] = p

            @pl.when(k > 0)
            def _():
                acc_ref[...] += p

            @pl.when(k == nk - 1)
            def _():
                finish(acc_ref[...])

    vm = 2 * (_nbytes(a_blk, a.dtype) + _nbytes((tk, tn), b.dtype) + _nbytes((tm, tn), out_dtype))
    vm += 3 * _nbytes((tm, tn), F32)
    if epi is not None:
        vm += 2 * _nbytes((tm, tn), extra.dtype)
    if out_chips:
        per_out = N // 4 // tn
        assert N % (4 * tn) == 0
        out_shape = _out((4, M, N // 4), out_dtype)
        out_spec = pl.BlockSpec((None, tm, tn), lambda i, j, k: (j // per_out, i, j % per_out))
    else:
        out_shape = _out((M, N), out_dtype)
        out_spec = pl.BlockSpec((tm, tn), lambda i, j, k: (i, j))
    sem = ("parallel", "parallel", "arbitrary")
    if norm_g is not None:
        out_shape, out_spec = (out_shape, _out((M, K), BF16)), (out_spec, pl.BlockSpec((tm, K), lambda i, j, k: (i, 0)))
        vm += 4 * _nbytes((tm, K), F32)
    if norm_bwd is not None:
        out_shape, out_spec = (out_shape, _out((1, N), F32)), (out_spec, pl.BlockSpec((1, N), lambda i, j, k: (0, 0)))
        sem = ("arbitrary", "arbitrary", "arbitrary")
        vm += 8 * _nbytes((tm, N), F32)
    return _pallas(
        body,
        out_shape=out_shape,
        grid=(M // tm, N // tn, nk),
        in_specs=in_specs,
        out_specs=out_spec,
        scratch_shapes=[pltpu.VMEM((tm, tn), F32)] if nk > 1 else [],
        compiler_params=_cparams(sem, vm),
        name=name,
    )(*args)


def _mm_fan(a, bs, *, mode, out_dtypes, name, tm=512, norm_g=None):
    M, K = a.shape
    ns = [b.shape[1] if mode == "nn" else b.shape[0] for b in bs]
    n = len(bs)
    first_out = 1 + n + (norm_g is not None)

    def body(*refs):
        if norm_g is None:
            av = refs[0][...].astype(BF16)
        else:
            x = refs[0][...]
            av = (x * lax.rsqrt(jnp.mean(x * x, axis=-1, keepdims=True) + EPS) * refs[1 + n][...]).astype(BF16)
            refs[first_out + n][...] = av
        for i in range(n):
            refs[first_out + i][...] = lax.dot_general(av, refs[1 + i][...], _DIMS[mode],
                                                       preferred_element_type=F32).astype(refs[first_out + i].dtype)

    vm = 4 * _nbytes((tm, K), F32) + sum(2 * _nbytes(b.shape, b.dtype) + 3 * _nbytes((tm, nn), F32) for b, nn in zip(bs, ns))
    in_specs = [pl.BlockSpec((tm, K), lambda i: (i, 0))] + [pl.BlockSpec(b.shape, lambda i: (0, 0)) for b in bs]
    out_shape = [_out((M, nn), dt) for nn, dt in zip(ns, out_dtypes)]
    out_specs = [pl.BlockSpec((tm, nn), lambda i: (i, 0)) for nn in ns]
    args = [a, *bs]
    if norm_g is not None:
        in_specs.append(pl.BlockSpec((1, K), lambda i: (0, 0)))
        out_shape.append(_out((M, K), BF16))
        out_specs.append(pl.BlockSpec((tm, K), lambda i: (i, 0)))
        args.append(norm_g)
    return _pallas(
        body,
        out_shape=tuple(out_shape),
        grid=(M // tm,),
        in_specs=in_specs,
        out_specs=tuple(out_specs),
        compiler_params=_cparams(("parallel",), vm),
        name=name,
    )(*args)


def _mm_sum(as_, bs, *, mode, out_dtype, name, extra=None, norm_bwd=None, tm=512):
    M = as_[0].shape[0]
    N = bs[0].shape[1] if mode == "nn" else bs[0].shape[0]
    n = len(as_)

    def body(*refs):
        acc = None
        for i in range(n):
            p = lax.dot_general(refs[i][...].astype(BF16), refs[n + i][...], _DIMS[mode], preferred_element_type=F32)
            acc = p if acc is None else acc + p
        if extra is not None:
            acc = acc + refs[2 * n][...].astype(F32)
        if norm_bwd is None:
            refs[-1][...] = acc.astype(refs[-1].dtype)
        else:
            h_ref, g_ref, dres_ref, dh_ref, dg_ref = refs[-5:]

            @pl.when(pl.program_id(0) == 0)
            def _():
                dg_ref[...] = jnp.zeros_like(dg_ref)

            x = h_ref[...]
            r = lax.rsqrt(jnp.mean(x * x, axis=-1, keepdims=True) + EPS)
            gy = acc * g_ref[...]
            dot = jnp.mean(x * gy, axis=-1, keepdims=True)
            dh_ref[...] = dres_ref[...] + r * gy - x * (r * r * r * dot)
            dg_ref[...] += jnp.sum(acc * x * r, axis=0, keepdims=True)

    row = pl.BlockSpec((tm, N), lambda i: (i, 0))
    in_specs = [pl.BlockSpec((tm, a.shape[1]), lambda i: (i, 0)) for a in as_]
    in_specs += [pl.BlockSpec(b.shape, lambda i: (0, 0)) for b in bs]
    args = list(as_) + list(bs)
    if extra is not None:
        in_specs.append(row)
        args.append(extra)
    vm = sum(2 * _nbytes((tm, a.shape[1]), a.dtype) for a in as_) + sum(2 * _nbytes(b.shape, b.dtype) for b in bs)
    vm += 8 * _nbytes((tm, N), F32)
    if norm_bwd is None:
        out_shape, out_specs, sem = _out((M, N), out_dtype), row, "parallel"
    else:
        vec = pl.BlockSpec((1, N), lambda i: (0, 0))
        in_specs += [row, vec, row]
        args += list(norm_bwd)
        out_shape, out_specs, sem = (_out((M, N), F32), _out((1, N), F32)), (row, vec), "arbitrary"
        vm += 8 * _nbytes((tm, N), F32)
    return _pallas(
        body,
        out_shape=out_shape,
        grid=(M // tm,),
        in_specs=in_specs,
        out_specs=out_specs,
        compiler_params=_cparams((sem,), vm),
        name=name,
    )(*args)


def _mm_tn_multi(ops, pairs, *, name, tk=1024):
    T = ops[0].shape[0]
    tk = min(tk, T)
    n, m = len(ops), len(pairs)
    shapes = [(ops[a].shape[1], ops[b].shape[1]) for a, b in pairs]

    def body(*refs):
        vals = [refs[i][...].astype(BF16) for i in range(n)]
        first = pl.program_id(0) == 0
        for j, (a, b) in enumerate(pairs):
            p = lax.dot_general(vals[a], vals[b], _DIMS["tn"], preferred_element_type=F32)
            o_ref = refs[n + j]

            @pl.when(first)
            def _(o_ref=o_ref, p=p):
                o_ref[...] = p

            @pl.when(jnp.logical_not(first))
            def _(o_ref=o_ref, p=p):
                o_ref[...] += p

    vm = sum(2 * _nbytes((tk, o.shape[1]), o.dtype) for o in ops) + sum(3 * _nbytes(s, F32) for s in shapes)
    return _pallas(
        body,
        out_shape=tuple(_out(s, F32) for s in shapes),
        grid=(T // tk,),
        in_specs=[pl.BlockSpec((tk, o.shape[1]), lambda k: (k, 0)) for o in ops],
        out_specs=tuple(pl.BlockSpec(s, lambda k: (0, 0)) for s in shapes),
        compiler_params=_cparams(("arbitrary",), vm),
        name=name,
    )(*ops)


def _final_loss(h, g, target, name, tm=512):
    T = h.shape[0]

    def body(h_ref, g_ref, t_ref, loss_ref, dh_ref, dg_ref):
        @pl.when(pl.program_id(0) == 0)
        def _():
            dg_ref[...] = jnp.zeros_like(dg_ref)
            loss_ref[...] = jnp.zeros_like(loss_ref)

        x = h_ref[...]
        gg = g_ref[...]
        r = lax.rsqrt(jnp.mean(x * x, axis=-1, keepdims=True) + EPS)
        y = x * r * gg
        e = y - t_ref[...]
        loss_ref[...] += 0.5 * jnp.sum(jnp.mean(e * e, axis=-1, keepdims=True), axis=0, keepdims=True)
        dy = e * (1.0 / D)
        gy = dy * gg
        dot = jnp.mean(x * gy, axis=-1, keepdims=True)
        dh_ref[...] = r * gy - x * (r * r * r * dot)
        dg_ref[...] += jnp.sum(dy * x * r, axis=0, keepdims=True)

    row = pl.BlockSpec((tm, D), lambda i: (i, 0))
    vec = pl.BlockSpec((1, D), lambda i: (0, 0))
    one = pl.BlockSpec((1, 1), lambda i: (0, 0))
    return _pallas(
        body,
        out_shape=(_out((1, 1), F32), _out((T, D), F32), _out((1, D), F32)),
        grid=(T // tm,),
        in_specs=[row, vec, row],
        out_specs=(one, row, vec),
        compiler_params=_cparams(("arbitrary",), 12 * _nbytes((tm, D), F32)),
        name=name,
    )(h, g, target)


GLA_G = 8


def _gla_consts():
    i = np.arange(KW)[:, None]
    j = np.arange(VW)[None, :]
    mask = ((i // GLA_DK) == (j // GLA_DV)) & (i < GLA_H * GLA_DK)
    a = np.arange(VW)
    hm = ((a[:, None] // GLA_DV) == (a[None, :] // GLA_DV)).astype(np.float32)
    c = np.arange(CHUNK)
    low = (c[:, None] >= c[None, :]).astype(np.float32)
    return jnp.asarray(mask.astype(np.float32)), jnp.asarray(hm, BF16), jnp.asarray(low, BF16)


def _split(x):
    hi = x.astype(BF16)
    return hi, (x - hi.astype(F32)).astype(BF16)


def _dot_sel(a, b, dims, split):
    if split == "a":
        hi, lo = _split(a)
        return (lax.dot_general(hi, b, dims, preferred_element_type=F32) + lax.dot_general(lo, b, dims, preferred_element_type=F32))
    hi, lo = _split(b)
    return (lax.dot_general(a, hi, dims, preferred_element_type=F32) + lax.dot_general(a, lo, dims, preferred_element_type=F32))


def _dot3(a, b, dims):
    ah, al = _split(a)
    bh, bl = _split(b)
    return (lax.dot_general(ah, bh, dims, preferred_element_type=F32) + lax.dot_general(al, bh, dims, preferred_element_type=F32)
            + lax.dot_general(ah, bl, dims, preferred_element_type=F32))


def _dot3s(a_s, b_s, dims):
    (ah, al), (bh, bl) = a_s, b_s
    return (lax.dot_general(ah, bh, dims, preferred_element_type=F32) + lax.dot_general(al, bh, dims, preferred_element_type=F32)
            + lax.dot_general(ah, bl, dims, preferred_element_type=F32))


def _gla_group_common(p_ref, wg, bg):
    lr_s = _split(p_ref[:, 2 * KW + 2 * VW:GLA_COLS])
    wg_s = _split(wg)
    z = _dot3s(lr_s, wg_s, _DIMS["nn"]) + bg
    la = (jnp.minimum(z, 0.0) - jnp.log(1.0 + jnp.exp(-jnp.abs(z)))) * (1.0 / GLA_TAU)
    return lr_s, wg_s, z, _split(la)


def _gla_chunk_common(p_ref, rows, la_s, low, ones_v):
    q = p_ref[rows, 0:KW]
    k = p_ref[rows, KW:2 * KW]
    v = p_ref[rows, 2 * KW:2 * KW + VW]
    g = p_ref[rows, 2 * KW + VW:2 * KW + 2 * VW]
    la_h, la_l = la_s[0][rows], la_s[1][rows]
    cum = jnp.dot(low, la_h, preferred_element_type=F32) + jnp.dot(low, la_l, preferred_element_type=F32)
    endb = cum[CHUNK - 1:CHUNK, :]
    w = jnp.exp(endb - cum)
    a_full = jnp.exp(lax.dot_general(la_h, ones_v, _DIMS["tn"], preferred_element_type=F32)
                     + lax.dot_general(la_l, ones_v, _DIMS["tn"], preferred_element_type=F32))
    return q, k, v, g, w, endb, a_full


def _gla_fwd(p, wg, bg, gn, consts, name):
    T = p.shape[0]
    rb = CHUNK * GLA_G
    ng = T // rb
    mask, hm, low = consts[:3]
    scale = GLA_DK ** -0.5

    def body(p_ref, wg_ref, bg_ref, gn_ref, m_ref, hm_ref, l_ref, o_ref, st_ref, s_ref):
        @pl.when(pl.program_id(0) == 0)
        def _():
            s_ref[...] = jnp.zeros_like(s_ref)

        wg_v, bg_v, gn_v = wg_ref[...], bg_ref[...], gn_ref[...]
        ones_v = jnp.ones((CHUNK, VW), BF16)
        s_new = s_ref[...]
        _, _, _, la_s = _gla_group_common(p_ref, wg_v, bg_v)
        outs = []
        for c in range(GLA_G):
            rows = slice(c * CHUNK, (c + 1) * CHUNK)
            q, k, v, _, w, _, a_full = _gla_chunk_common(p_ref, rows, la_s, l_ref[...], ones_v)
            kd = (k * w).astype(BF16)
            kv = lax.dot_general(kd, v.astype(BF16), _DIMS["tn"], preferred_element_type=F32) * m_ref[...]
            s_new = a_full * s_new + kv
            st_ref[c] = s_new
            outs.append(jnp.dot((q * scale).astype(BF16), s_new.astype(BF16), preferred_element_type=F32))
        s_ref[...] = s_new
        o = jnp.concatenate(outs, axis=0)
        g = p_ref[:, 2 * KW + VW:2 * KW + 2 * VW]
        ms = _dot_sel(o * o, hm_ref[...], _DIMS["nn"], "a") * (1.0 / GLA_DV)
        o_ref[...] = (o * lax.rsqrt(ms + EPS) * gn_v * (g * _sigmoid(g))).astype(o_ref.dtype)

    full = lambda shape: pl.BlockSpec(shape, lambda i: tuple(0 for _ in shape))
    vm = 2 * _nbytes((rb, GLA_COLS), F32) + 2 * _nbytes((GLA_G, KW, VW), F32) + 12 * _nbytes((KW, VW), F32)
    return _pallas(
        body,
        out_shape=(_out((T, VW), BF16), _out((T // CHUNK, KW, VW), F32)),
        grid=(ng,),
        in_specs=[pl.BlockSpec((rb, GLA_COLS), lambda i: (i, 0)), full((LRW, KW)), full((1, KW)), full((1, VW)),
                  full((KW, VW)), full((VW, VW)), full((CHUNK, CHUNK))],
        out_specs=(pl.BlockSpec((rb, VW), lambda i: (i, 0)), pl.BlockSpec((GLA_G, KW, VW), lambda i: (i, 0, 0))),
        scratch_shapes=[pltpu.VMEM((KW, VW), F32)],
        compiler_params=_cparams(("arbitrary",), vm),
        name=name,
    )(p, wg, bg, gn, mask, hm, low)


def _gla_bwd(p, dy, states, wg, bg, gn, consts, name):
    T = p.shape[0]
    rb = CHUNK * GLA_G
    ng = T // rb
    mask, hm, low = consts[:3]
    scale = GLA_DK ** -0.5

    def body(p_ref, dy_ref, st_ref, sp_ref, wg_ref, bg_ref, gn_ref, m_ref, hm_ref, l_ref,
             dp_ref, dwg_ref, dbg_ref, dgn_ref, ga_ref):
        step = pl.program_id(0)

        @pl.when(step == 0)
        def _():
            ga_ref[...] = jnp.zeros_like(ga_ref)
            dwg_ref[...] = jnp.zeros_like(dwg_ref)
            dbg_ref[...] = jnp.zeros_like(dbg_ref)
            dgn_ref[...] = jnp.zeros_like(dgn_ref)

        first_group = step == ng - 1
        wg_v, bg_v, gn_v = wg_ref[...], bg_ref[...], gn_ref[...]
        ones_v = jnp.ones((CHUNK, VW), BF16)
        ones_8 = jnp.ones((8, VW), BF16)
        ga = ga_ref[...]
        lr_s, wg_s, z_all, la_s = _gla_group_common(p_ref, wg_v, bg_v)
        qss = [(p_ref[c * CHUNK:(c + 1) * CHUNK, 0:KW] * scale).astype(BF16) for c in range(GLA_G)]
        o = jnp.concatenate([jnp.dot(qss[c], st_ref[c].astype(BF16), preferred_element_type=F32) for c in range(GLA_G)], axis=0)
        g = p_ref[:, 2 * KW + VW:2 * KW + 2 * VW]
        dyv = dy_ref[...].astype(F32)
        r = lax.rsqrt(_dot_sel(o * o, hm_ref[...], _DIMS["nn"], "a") * (1.0 / GLA_DV) + EPS)
        on = o * r
        sg = _sigmoid(g)
        silu = g * sg
        d_on = dyv * gn_v * silu
        dp_ref[:, 2 * KW + VW:2 * KW + 2 * VW] = (dyv * on * gn_v * (sg * (1.0 + g * (1.0 - sg)))).astype(dp_ref.dtype)
        dgn_ref[...] += jnp.sum(dyv * on * silu, axis=0, keepdims=True)
        mo = _dot_sel(o * d_on, hm_ref[...], _DIMS["nn"], "a") * (1.0 / GLA_DV)
        dob_all = (r * d_on - o * (r * r * r) * mo).astype(BF16)
        dzs = [None] * GLA_G
        for c in reversed(range(GLA_G)):
            rows = slice(c * CHUNK, (c + 1) * CHUNK)
            _, k, v, _, w, endb, a_full = _gla_chunk_common(p_ref, rows, la_s, l_ref[...], ones_v)
            s_n = st_ref[c]
            if c > 0:
                s_prev = st_ref[c - 1]
            else:
                s_prev = jnp.where(first_group, 0.0, sp_ref[0])
            kd = k * w
            dob = dob_all[rows]
            dq = lax.dot_general(dob, s_n.astype(BF16), _DIMS["nt"], preferred_element_type=F32) * scale
            g_n = lax.dot_general(qss[c], dob, _DIMS["tn"], preferred_element_type=F32) * m_ref[...] + ga
            d_a = _dot_sel(ones_8, g_n * s_prev, _DIMS["nt"], "b")[0:1, :]
            g_nb = g_n.astype(BF16)
            dkd = lax.dot_general(v.astype(BF16), g_nb, _DIMS["nt"], preferred_element_type=F32)
            dv = jnp.dot(kd.astype(BF16), g_nb, preferred_element_type=F32)
            e = dkd * kd
            d_end = jnp.sum(e, axis=0, keepdims=True) + d_a * jnp.exp(endb)
            dla = _dot_sel(l_ref[...], -e, _DIMS["tn"], "b") + d_end
            dzs[c] = dla * (1.0 - _sigmoid(z_all[rows])) * (1.0 / GLA_TAU)
            ga = a_full * g_n
            dp_ref[rows, 0:KW] = dq.astype(dp_ref.dtype)
            dp_ref[rows, KW:2 * KW] = (dkd * w).astype(dp_ref.dtype)
            dp_ref[rows, 2 * KW:2 * KW + VW] = dv.astype(dp_ref.dtype)
        ga_ref[...] = ga
        dz = jnp.concatenate(dzs, axis=0)
        dz_s = _split(dz)
        dp_ref[:, 2 * KW + 2 * VW:GLA_COLS] = _dot3s(dz_s, wg_s, _DIMS["nt"]).astype(dp_ref.dtype)
        dwg_ref[...] += _dot3s(lr_s, dz_s, _DIMS["tn"])
        dbg_ref[...] += jnp.sum(dz, axis=0, keepdims=True)

    full = lambda shape: pl.BlockSpec(shape, lambda i: tuple(0 for _ in shape))
    rev = lambda i: (ng - 1 - i, 0)
    vm = 4 * _nbytes((rb, GLA_COLS), F32) + 2 * _nbytes((rb, VW), F32) + 2 * _nbytes((GLA_G + 1, KW, VW), F32)
    vm += 16 * _nbytes((KW, VW), F32)
    return _pallas(
        body,
        out_shape=(_out((T, GLA_COLS), BF16), _out((LRW, KW), F32),
                   _out((1, KW), F32), _out((1, VW), F32)),
        grid=(ng,),
        in_specs=[pl.BlockSpec((rb, GLA_COLS), rev), pl.BlockSpec((rb, VW), rev),
                  pl.BlockSpec((GLA_G, KW, VW), lambda i: (ng - 1 - i, 0, 0)),
                  pl.BlockSpec((1, KW, VW), lambda i: (jnp.maximum((ng - 1 - i) * GLA_G - 1, 0), 0, 0)),
                  full((LRW, KW)), full((1, KW)), full((1, VW)), full((KW, VW)), full((VW, VW)), full((CHUNK, CHUNK))],
        out_specs=(pl.BlockSpec((rb, GLA_COLS), rev), full((LRW, KW)), full((1, KW)), full((1, VW))),
        scratch_shapes=[pltpu.VMEM((KW, VW), F32)],
        compiler_params=_cparams(("arbitrary",), vm),
        name=name,
    )(p, dy, states, states, wg, bg, gn, mask, hm, low)


CONV_TM = 512
HALO = 32
CONV_RB = 64


def _glu(u):
    a = u[:, 0:CW]
    b = u[:, CW:2 * CW]
    return a * _sigmoid(b)


def _conv_taps(buf_ref, w_ref, rb0, first_tap):
    acc = jnp.zeros((CONV_RB, CW), F32)
    for j in range(CK):
        s = rb0 + first_tap(j)
        acc = acc + w_ref[j:j + 1, :] * buf_ref[s:s + CONV_RB, :]
    return acc


def _ln_fwd(c, lg, lb):
    mu = jnp.mean(c, axis=-1, keepdims=True)
    xc = c - mu
    rstd = lax.rsqrt(jnp.mean(xc * xc, axis=-1, keepdims=True) + EPS)
    n = xc * rstd
    return n, rstd, n * lg + lb


def _conv_fwd(u, w, b, lg, lb, name):
    T = u.shape[0]
    tm = CONV_TM

    def body(u_ref, uh_ref, w_ref, b_ref, lg_ref, lb_ref, o_ref, c_ref, hbuf):
        i = pl.program_id(0)
        hbuf[0:HALO, :] = jnp.where(i > 0, _glu(uh_ref[...]), 0.0)
        hbuf[HALO:HALO + tm, :] = _glu(u_ref[...])
        for r in range(tm // CONV_RB):
            rows = slice(r * CONV_RB, (r + 1) * CONV_RB)
            acc = _conv_taps(hbuf, w_ref, r * CONV_RB, lambda j: HALO - (CK - 1) + j)
            c_ref[rows, :] = acc
            _, _, zz = _ln_fwd(acc + b_ref[...], lg_ref[...], lb_ref[...])
            o_ref[rows, :] = (zz * _sigmoid(zz)).astype(o_ref.dtype)

    vec = pl.BlockSpec((1, CW), lambda i: (0, 0))
    return _pallas(
        body,
        out_shape=(_out((T, CW), BF16), _out((T, CW), F32)),
        grid=(T // tm,),
        in_specs=[pl.BlockSpec((tm, CONV_COLS), lambda i: (i, 0)),
                  pl.BlockSpec((HALO, CONV_COLS), lambda i: (jnp.maximum(i * (tm // HALO) - 1, 0), 0)),
                  pl.BlockSpec((HALO, CW), lambda i: (0, 0)), vec, vec, vec],
        out_specs=(pl.BlockSpec((tm, CW), lambda i: (i, 0)), pl.BlockSpec((tm, CW), lambda i: (i, 0))),
        scratch_shapes=[pltpu.VMEM((tm + HALO, CW), F32)],
        compiler_params=_cparams(("arbitrary",), 8 * _nbytes((tm, CONV_COLS), F32)),
        name=name,
    )(u, u, w, b, lg, lb)


def _conv_bwd_dc(u, conv, dout, b, lg, lb, name):
    T = u.shape[0]
    tm = CONV_TM
    nsteps = T // tm

    def body(u_ref, uh_ref, c_ref, do_ref, b_ref, lg_ref, lb_ref, dc_ref, dw_ref, db_ref, dlg_ref, dlb_ref, hbuf, dwacc):
        i = pl.program_id(0)

        @pl.when(i == 0)
        def _():
            dwacc[...] = jnp.zeros_like(dwacc)
            db_ref[...] = jnp.zeros_like(db_ref)
            dlg_ref[...] = jnp.zeros_like(dlg_ref)
            dlb_ref[...] = jnp.zeros_like(dlb_ref)

        hbuf[0:HALO, :] = jnp.where(i > 0, _glu(uh_ref[...]), 0.0)
        hbuf[HALO:HALO + tm, :] = _glu(u_ref[...])
        for r in range(tm // CONV_RB):
            rows = slice(r * CONV_RB, (r + 1) * CONV_RB)
            n, rstd, zz = _ln_fwd(c_ref[rows, :] + b_ref[...], lg_ref[...], lb_ref[...])
            sg = _sigmoid(zz)
            dz = do_ref[rows, :].astype(F32) * (sg * (1.0 + zz * (1.0 - sg)))
            dlg_ref[...] += jnp.sum(dz * n, axis=0, keepdims=True)
            dlb_ref[...] += jnp.sum(dz, axis=0, keepdims=True)
            dn = dz * lg_ref[...]
            dc = rstd * (dn - jnp.mean(dn, axis=-1, keepdims=True) - n * jnp.mean(dn * n, axis=-1, keepdims=True))
            dc_ref[rows, :] = dc
            db_ref[...] += jnp.sum(dc, axis=0, keepdims=True)
            for j in range(CK):
                s = r * CONV_RB + HALO - (CK - 1) + j
                prod = dc * hbuf[s:s + CONV_RB, :]
                dwacc[j] += jnp.sum(prod.reshape(CONV_RB // 8, 8, CW), axis=0)

        @pl.when(i == nsteps - 1)
        def _():
            dw_ref[...] = jnp.sum(dwacc[...], axis=1)

    vec = pl.BlockSpec((1, CW), lambda i: (0, 0))
    return _pallas(
        body,
        out_shape=(_out((T, CW), F32), _out((HALO, CW), F32),
                   _out((1, CW), F32), _out((1, CW), F32), _out((1, CW), F32)),
        grid=(nsteps,),
        in_specs=[pl.BlockSpec((tm, CONV_COLS), lambda i: (i, 0)),
                  pl.BlockSpec((HALO, CONV_COLS), lambda i: (jnp.maximum(i * (tm // HALO) - 1, 0), 0)),
                  pl.BlockSpec((tm, CW), lambda i: (i, 0)), pl.BlockSpec((tm, CW), lambda i: (i, 0)), vec, vec, vec],
        out_specs=(pl.BlockSpec((tm, CW), lambda i: (i, 0)), pl.BlockSpec((HALO, CW), lambda i: (0, 0)), vec, vec, vec),
        scratch_shapes=[pltpu.VMEM((tm + HALO, CW), F32), pltpu.VMEM((HALO, 8, CW), F32)],
        compiler_params=_cparams(("arbitrary",), 10 * _nbytes((tm, CONV_COLS), F32)),
        name=name,
    )(u, u, conv, dout, b, lg, lb)


def _conv_bwd_du(u, dc, w, name):
    T = u.shape[0]
    tm = CONV_TM
    nsteps = T // tm

    def body(u_ref, dc_ref, dch_ref, w_ref, du_ref, dcbuf):
        i = pl.program_id(0)
        dcbuf[0:tm, :] = dc_ref[...]
        dcbuf[tm:tm + HALO, :] = jnp.where(i < nsteps - 1, dch_ref[...], 0.0)
        for r in range(tm // CONV_RB):
            rows = slice(r * CONV_RB, (r + 1) * CONV_RB)
            dh = _conv_taps(dcbuf, w_ref, r * CONV_RB, lambda j: (CK - 1) - j)
            a = u_ref[rows, 0:CW]
            sb = _sigmoid(u_ref[rows, CW:2 * CW])
            du_ref[rows, 0:CW] = (dh * sb).astype(du_ref.dtype)
            du_ref[rows, CW:2 * CW] = (dh * a * sb * (1.0 - sb)).astype(du_ref.dtype)

    return _pallas(
        body,
        out_shape=_out((T, CONV_COLS), BF16),
        grid=(nsteps,),
        in_specs=[pl.BlockSpec((tm, CONV_COLS), lambda i: (i, 0)),
                  pl.BlockSpec((tm, CW), lambda i: (i, 0)),
                  pl.BlockSpec((HALO, CW), lambda i: (jnp.minimum((i + 1) * (tm // HALO), T // HALO - 1), 0)),
                  pl.BlockSpec((HALO, CW), lambda i: (0, 0))],
        out_specs=pl.BlockSpec((tm, CONV_COLS), lambda i: (i, 0)),
        scratch_shapes=[pltpu.VMEM((tm + HALO, CW), F32)],
        compiler_params=_cparams(("arbitrary",), 8 * _nbytes((tm, CONV_COLS), F32)),
        name=name,
    )(u, dc, dc, w)


ATT_G = 4


def _att_load_kv(p_any, kbuf, vbuf, sems, T):
    kc = pltpu.make_async_copy(p_any.at[:, pl.ds(AW, AW)], kbuf.at[pl.ds(LEFT, T), :], sems.at[0])
    vc = pltpu.make_async_copy(p_any.at[:, pl.ds(2 * AW, AW)], vbuf.at[pl.ds(LEFT, T), :], sems.at[1])
    kc.start()
    vc.start()
    kbuf[0:LEFT, :] = jnp.zeros((LEFT, AW), BF16)
    vbuf[0:LEFT, :] = jnp.zeros((LEFT, AW), BF16)
    kc.wait()
    vc.wait()


ATT_QB = CHUNK * ATT_G
ATT_KB = LEFT + ATT_QB
REL_PAD = 384
TOEP = 1024


def _att_consts():
    m = np.arange(TOEP)
    d = ATT_KB - 1 - m
    idx = np.clip(d, -128, 128) + 128
    sel = (np.arange(REL_PAD)[:, None] == idx[None, :]) & (m[None, :] < ATT_QB + ATT_KB - 1)
    return jnp.asarray(sel.astype(np.float32))


def _att_build_bias(rel_ref, sel_ref, bias_scr):
    tr = jnp.dot(rel_ref[...], sel_ref[...], precision=HI, preferred_element_type=F32)
    qc = lax.broadcasted_iota(jnp.int32, (ATT_QB, ATT_KB), 0) // CHUNK
    kc = lax.broadcasted_iota(jnp.int32, (ATT_QB, ATT_KB), 1) // CHUNK
    band = (kc >= qc) & (kc <= qc + 8)
    for h in range(AH):
        rows = jnp.broadcast_to(tr[h:h + 1, :], (ATT_QB, TOEP))
        toep = pltpu.roll(rows, TOEP - (ATT_QB - 1), 1, stride=1, stride_axis=0)[:, 0:ATT_KB]
        bias_scr[h // 2, (h % 2) * ATT_QB:(h % 2 + 1) * ATT_QB, :] = jnp.where(band, toep, NEG)


def _att_probs(qst, kb, bias_p, n0, lse=None):
    sc = lax.dot_general(qst, kb, _DIMS["nt"], preferred_element_type=F32) * (64 ** -0.5) + bias_p
    pos = lax.broadcasted_iota(jnp.int32, (2 * ATT_QB, ATT_KB), 1)
    sc = jnp.where(pos >= CHUNK * (8 - n0), sc, NEG)
    if lse is not None:
        return jnp.exp(sc - lse), lse
    mx = jnp.max(sc, axis=-1, keepdims=True)
    ex = jnp.exp(sc - mx)
    tot = jnp.sum(ex, axis=-1, keepdims=True)
    return ex / tot, mx + jnp.log(tot)


def _head_stack(a2, lo):
    zero = jnp.zeros_like(a2)
    return jnp.concatenate([jnp.where(lo, a2, zero), jnp.where(lo, zero, a2)], axis=0)


def _att_fwd(p, rel, sel, name):
    T = p.shape[0]

    def body(q_ref, p_any, rel_ref, sel_ref, o_ref, lse_ref, kbuf, vbuf, bias_scr, sems):
        i = pl.program_id(0)

        @pl.when(i == 0)
        def _():
            _att_load_kv(p_any, kbuf, vbuf, sems, T)
            _att_build_bias(rel_ref, sel_ref, bias_scr)

        lo = lax.broadcasted_iota(jnp.int32, (ATT_QB, 128), 1) < 64
        n0 = i * ATT_G
        start = pl.multiple_of(i * ATT_QB, ATT_QB)
        for hp in range(AH // 2):
            cols = slice(hp * 128, (hp + 1) * 128)
            kb = kbuf[pl.ds(start, ATT_KB), cols]
            vb = vbuf[pl.ds(start, ATT_KB), cols]
            pr, lse = _att_probs(_head_stack(q_ref[:, cols], lo), kb, bias_scr[hp], n0)
            lse_ref[hp] = jnp.broadcast_to(lse, (2 * ATT_QB, 128))
            pv = jnp.dot(pr.astype(BF16), vb, preferred_element_type=F32)
            o_ref[:, cols] = jnp.where(lo, pv[0:ATT_QB], pv[ATT_QB:2 * ATT_QB]).astype(o_ref.dtype)

    vm = 2 * _nbytes((T + LEFT, AW), BF16) + 8 * _nbytes((2 * ATT_QB, ATT_KB), F32) + (8 << 20)
    return _pallas(
        body,
        out_shape=(_out((T, AW), BF16), _out((T // ATT_QB, AH // 2, 2 * ATT_QB, 128), F32)),
        grid=(T // ATT_QB,),
        in_specs=[pl.BlockSpec((ATT_QB, AW), lambda i: (i, 0)), pl.BlockSpec(memory_space=pl.ANY),
                  pl.BlockSpec((8, REL_PAD), lambda i: (0, 0)), pl.BlockSpec((REL_PAD, TOEP), lambda i: (0, 0))],
        out_specs=(pl.BlockSpec((ATT_QB, AW), lambda i: (i, 0)),
                   pl.BlockSpec((None, AH // 2, 2 * ATT_QB, 128), lambda i: (i, 0, 0, 0))),
        scratch_shapes=[pltpu.VMEM((T + LEFT, AW), BF16), pltpu.VMEM((T + LEFT, AW), BF16),
                        pltpu.VMEM((AH // 2, 2 * ATT_QB, ATT_KB), F32), pltpu.SemaphoreType.DMA((2,))],
        compiler_params=_cparams(("arbitrary",), vm),
        name=name,
    )(p, p, rel, sel)


def _att_bwd(p, do, lse, rel, sel, name):
    T = p.shape[0]
    nsteps = T // ATT_QB

    def body(q_ref, p_any, do_ref, lse_ref, rel_ref, sel_ref, dp_any, drel_ref,
             kbuf, vbuf, dqbuf, dkbuf, dvbuf, bias_scr, dbias_scr, dtr_scr, sems):
        i = pl.program_id(0)

        @pl.when(i == 0)
        def _():
            _att_load_kv(p_any, kbuf, vbuf, sems, T)
            _att_build_bias(rel_ref, sel_ref, bias_scr)
            dkbuf[...] = jnp.zeros_like(dkbuf)
            dvbuf[...] = jnp.zeros_like(dvbuf)
            dbias_scr[...] = jnp.zeros_like(dbias_scr)

        lo = lax.broadcasted_iota(jnp.int32, (ATT_QB, 128), 1) < 64
        n0 = i * ATT_G
        start = pl.multiple_of(i * ATT_QB, ATT_QB)
        for hp in range(AH // 2):
            cols = slice(hp * 128, (hp + 1) * 128)
            kb = kbuf[pl.ds(start, ATT_KB), cols]
            vb = vbuf[pl.ds(start, ATT_KB), cols]
            qst = _head_stack(q_ref[:, cols], lo)
            dost = _head_stack(do_ref[:, cols].astype(BF16), lo)
            pr, _ = _att_probs(qst, kb, bias_scr[hp], n0, lse_ref[hp][:, 0:1])
            dpr = lax.dot_general(dost, vb, _DIMS["nt"], preferred_element_type=F32)
            ds = pr * (dpr - jnp.sum(dpr * pr, axis=-1, keepdims=True))
            dbias_scr[hp] += ds
            dsb = (ds * (64 ** -0.5)).astype(BF16)
            dq = jnp.dot(dsb, kb, preferred_element_type=F32)
            dqbuf[pl.ds(start, ATT_QB), cols] = jnp.where(lo, dq[0:ATT_QB], dq[ATT_QB:2 * ATT_QB]).astype(BF16)
            dkbuf[pl.ds(start, ATT_KB), cols] += lax.dot_general(dsb, qst, _DIMS["tn"], preferred_element_type=F32)
            dvbuf[pl.ds(start, ATT_KB), cols] += lax.dot_general(pr.astype(BF16), dost, _DIMS["tn"], preferred_element_type=F32)

        @pl.when(i == nsteps - 1)
        def _():
            kbuf[pl.ds(LEFT, T), :] = dkbuf[pl.ds(LEFT, T), :].astype(BF16)
            vbuf[pl.ds(LEFT, T), :] = dvbuf[pl.ds(LEFT, T), :].astype(BF16)
            cps = [pltpu.make_async_copy(dqbuf, dp_any.at[:, pl.ds(0, AW)], sems.at[0]),
                   pltpu.make_async_copy(kbuf.at[pl.ds(LEFT, T), :], dp_any.at[:, pl.ds(AW, AW)], sems.at[1]),
                   pltpu.make_async_copy(vbuf.at[pl.ds(LEFT, T), :], dp_any.at[:, pl.ds(2 * AW, AW)], sems.at[2])]
            for cp in cps:
                cp.start()
            dtr_scr[...] = jnp.zeros_like(dtr_scr)
            ri = lax.broadcasted_iota(jnp.int32, (ATT_QB, ATT_QB), 0)
            ci = lax.broadcasted_iota(jnp.int32, (ATT_QB, ATT_QB), 1)
            flip = jnp.where(ri + ci == ATT_QB - 1, 1.0, 0.0)
            for h in range(AH):
                db = dbias_scr[h // 2, (h % 2) * ATT_QB:(h % 2 + 1) * ATT_QB, :]
                db = jnp.dot(flip, db, precision=HI, preferred_element_type=F32)
                wide = jnp.concatenate([db, jnp.zeros((ATT_QB, TOEP - ATT_KB), F32)], axis=1)
                diag = pltpu.roll(wide, 0, 1, stride=1, stride_axis=0)
                dtr_scr[h:h + 1, :] = jnp.sum(diag, axis=0, keepdims=True)
            drel_ref[...] = lax.dot_general(dtr_scr[...], sel_ref[...], _DIMS["nt"], precision=HI, preferred_element_type=F32)
            for cp in cps:
                cp.wait()

    vm = 3 * _nbytes((T + LEFT, AW), BF16) + 2 * _nbytes((T + LEFT, AW), F32) + 12 * _nbytes((2 * ATT_QB, ATT_KB), F32) + (8 << 20)
    return _pallas(
        body,
        out_shape=(_out((T, ATT_COLS), BF16), _out((8, REL_PAD), F32)),
        grid=(nsteps,),
        in_specs=[pl.BlockSpec((ATT_QB, AW), lambda i: (i, 0)), pl.BlockSpec(memory_space=pl.ANY),
                  pl.BlockSpec((ATT_QB, AW), lambda i: (i, 0)),
                  pl.BlockSpec((None, AH // 2, 2 * ATT_QB, 128), lambda i: (i, 0, 0, 0)),
                  pl.BlockSpec((8, REL_PAD), lambda i: (0, 0)), pl.BlockSpec((REL_PAD, TOEP), lambda i: (0, 0))],
        out_specs=(pl.BlockSpec(memory_space=pl.ANY), pl.BlockSpec((8, REL_PAD), lambda i: (0, 0))),
        scratch_shapes=[pltpu.VMEM((T + LEFT, AW), BF16), pltpu.VMEM((T + LEFT, AW), BF16), pltpu.VMEM((T, AW), BF16),
                        pltpu.VMEM((T + LEFT, AW), F32), pltpu.VMEM((T + LEFT, AW), F32),
                        pltpu.VMEM((AH // 2, 2 * ATT_QB, ATT_KB), F32), pltpu.VMEM((AH // 2, 2 * ATT_QB, ATT_KB), F32),
                        pltpu.VMEM((8, TOEP), F32), pltpu.SemaphoreType.DMA((3,))],
        compiler_params=_cparams(("arbitrary",), vm),
        name=name,
    )(p, p, do, lse, rel, sel)


def _layer_fwd(h, wl, consts, tag):
    p_gla, p_conv, p_att, xn = _mm_fan(h, [wl["w_gla"], wl["w_conv"], wl["w_att"]], mode="nt", out_dtypes=(F32, F32, BF16),
                                       norm_g=wl["norm_mix"], name=f"{tag}_proj")
    o_gla, states = _gla_fwd(p_gla, wl["wg"], wl["bg"], wl["gla_norm"], consts, f"{tag}_gla_fwd")
    o_conv, conv = _conv_fwd(p_conv, wl["w_dw"], wl["b_dw"], wl["ln_g"], wl["ln_b"], f"{tag}_conv_fwd")
    rel = jnp.pad(wl["rel_bias"], ((0, 8 - AH), (0, REL_PAD - N_REL)))
    o_att, att_lse = _att_fwd(p_att, rel, consts[3], f"{tag}_att_fwd")
    h1 = _mm_sum([o_gla, o_conv, o_att], [wl["w_out_g"], wl["w_out_c"], wl["w_out_a"]], mode="nn", out_dtype=F32, extra=h,
                 name=f"{tag}_out")
    u, xn2 = _mm(h1, wl["w_up"], mode="nn", out_dtype=BF16, tm=1024, tn=1024, b_chips=True, norm_g=wl["norm_ffn"],
                 name=f"{tag}_mlp_up")
    h2 = _mm(u, wl["w_down"], mode="nn", out_dtype=F32, tm=1024, tk=2048, a_pro="relu2", epi="add", extra=h1,
             name=f"{tag}_mlp_down")
    saved = dict(h=h, xn=xn, p_gla=p_gla, p_conv=p_conv, p_att=p_att, states=states, o_gla=o_gla, o_conv=o_conv, conv=conv,
                 o_att=o_att, att_lse=att_lse, rel=rel, h1=h1, xn2=xn2, u=u)
    return h2, saved


def _layer_bwd(dh2, sv, wl, consts, tag, emit=lambda name, grad: None):
    g = {}
    du = _mm(dh2, wl["w_down"], mode="nt", out_dtype=BF16, tm=1024, tn=1024, epi="relu2grad", extra=sv["u"],
             name=f"{tag}_mlp_down_dx")
    g["w_down"] = _mm(sv["u"], dh2, mode="tn", out_dtype=F32, tm=2048, tn=1024, tk=1024, a_pro="relu2", name=f"{tag}_mlp_down_dw")
    emit("w_down", g["w_down"].reshape(4, D_FF // 4, D))
    dh1, g["norm_ffn"] = _mm(du, wl["w_up"], mode="nt", out_dtype=F32, tm=1024, tk=1024, b_chips=True,
                             norm_bwd=(sv["h1"], wl["norm_ffn"], dh2), name=f"{tag}_mlp_up_dx")
    g["w_up"] = _mm(sv["xn2"], du, mode="tn", out_dtype=F32, tm=1024, tn=1024, tk=1024, out_chips=True, name=f"{tag}_mlp_up_dw")
    emit("w_up", g["w_up"])
    d_gla, d_conv, d_att = _mm_fan(dh1, [wl["w_out_g"], wl["w_out_c"], wl["w_out_a"]], mode="nt", out_dtypes=(F32, F32, F32),
                                   name=f"{tag}_out_dx")
    g["w_out_g"], g["w_out_c"], g["w_out_a"] = _mm_tn_multi([sv["o_gla"], sv["o_conv"], sv["o_att"], dh1],
                                                            [(0, 3), (1, 3), (2, 3)], name=f"{tag}_out_dw")
    emit("w_out", jnp.concatenate([g["w_out_g"], g["w_out_c"], g["w_out_a"]], axis=0).reshape(4, D // 4, D))
    dp_gla, g["wg"], g["bg"], g["gla_norm"] = _gla_bwd(sv["p_gla"], d_gla, sv["states"], wl["wg"], wl["bg"], wl["gla_norm"],
                                                       consts, f"{tag}_gla_bwd")
    dc, g["w_dw"], g["b_dw"], g["ln_g"], g["ln_b"] = _conv_bwd_dc(sv["p_conv"], sv["conv"], d_conv, wl["b_dw"], wl["ln_g"],
                                                                  wl["ln_b"], f"{tag}_conv_bwd_dc")
    dp_conv = _conv_bwd_du(sv["p_conv"], dc, wl["w_dw"], f"{tag}_conv_bwd_du")
    dp_att, drel = _att_bwd(sv["p_att"], d_att, sv["att_lse"], sv["rel"], consts[3], f"{tag}_att_bwd")
    g["rel_bias"] = drel[0:AH, 0:N_REL]
    g["w_gla"], g["w_conv"], g["w_att"] = _mm_tn_multi([sv["xn"], dp_gla, dp_conv, dp_att], [(1, 0), (2, 0), (3, 0)],
                                                       name=f"{tag}_proj_dw")
    emit("w_in", _join_w_in_t(g))
    dh, g["norm_mix"] = _mm_sum([dp_gla, dp_conv, dp_att], [wl["w_gla"], wl["w_conv"], wl["w_att"]], mode="nn", out_dtype=F32,
                                norm_bwd=(sv["h"], wl["norm_mix"], dh1), name=f"{tag}_proj_dx")
    return dh, g


def _local_step(x, target, layers, norm_final, emit=lambda layer, name, grad: None):
    consts = _gla_consts() + (_att_consts(),)
    h = x
    saved = []
    for l, wl in enumerate(layers):
        h, sv = _layer_fwd(h, wl, consts, f"l{l}")
        saved.append(sv)
    loss, dh, g_final = _final_loss(h, norm_final, target, "final_loss")
    grads = [None] * len(layers)
    for l in reversed(range(len(layers))):
        dh, grads[l] = _layer_bwd(dh, saved[l], layers[l], consts, f"l{l}", functools.partial(emit, l))
    return loss, dh, grads, g_final


ANY = pl.BlockSpec(memory_space=pl.ANY)


def _place():
    x, y, c = lax.axis_index("x"), lax.axis_index("y"), lax.axis_index("c")
    chips = [(1 - x, y), (x, 1 - y), (1 - x, 1 - y)]
    return x, y, c, chips


def _shape(shape, dtype):
    return jax.ShapeDtypeStruct(tuple(shape), dtype)


def _remote(src, dst, send_sem, recv_sem, to):
    return pltpu.make_async_remote_copy(src_ref=src, dst_ref=dst, send_sem=send_sem, recv_sem=recv_sem,
                                        device_id=to, device_id_type=MESH)


class _Staged:
    def __init__(self, src, dst, buf, sems):
        self.load = pltpu.make_async_copy(src, buf, sems.at[0])
        self.store = pltpu.make_async_copy(buf, dst, sems.at[1])

    def start(self):
        self.load.start()

    def wait(self):
        self.load.wait()
        self.store.start()
        self.store.wait()


def _phase_gather_ici(src, then):
    R, C = src.shape
    rh = R // 2

    def copies(ins, outs, sems):
        x, y, c, chips = _place()
        me = 2 * x + y
        local = _Staged(ins[0], outs[0].at[me], sems[3], sems[2])
        sends = [_remote(ins[0].at[pl.ds(c * rh, rh), :], outs[0].at[me, pl.ds(c * rh, rh), :], sems[0].at[k], sems[1].at[k],
                         (px, py, c)) for k, (px, py) in enumerate(chips)]
        recvs = [_remote(outs[0].at[2 * px + py, pl.ds(c * rh, rh), :], outs[0].at[2 * px + py, pl.ds(c * rh, rh), :],
                         sems[0].at[k], sems[1].at[k], (px, py, c)) for k, (px, py) in enumerate(chips)]
        return local, sends, recvs

    def start(ins, outs, sems):
        local, sends, _ = copies(ins, outs, sems)
        local.start()
        for cp in sends:
            cp.start()

    def finish(ins, outs, sems):
        local, sends, recvs = copies(ins, outs, sems)
        for cp in recvs:
            cp.wait_recv()
        for cp in sends:
            cp.wait_send()
        local.wait()

    return _Comm([src], [_shape((4, R, C), src.dtype)], {}, [(3,), (3,), (2,), ((R, C), src.dtype)], start, finish, then)


def _phase_gather_d2d(part, then):
    _, R, C = part.shape
    rh = R // 2

    def copies(ins, outs, sems):
        x, y, c, chips = _place()
        sends = [_remote(ins[0].at[2 * px + py, pl.ds(c * rh, rh), :], outs[0].at[2 * px + py, pl.ds(c * rh, rh), :],
                         sems[0].at[k], sems[1].at[k], (x, y, 1 - c)) for k, (px, py) in enumerate(chips)]
        recvs = [_remote(outs[0].at[2 * px + py, pl.ds((1 - c) * rh, rh), :], outs[0].at[2 * px + py, pl.ds((1 - c) * rh, rh), :],
                         sems[0].at[k], sems[1].at[k], (x, y, 1 - c)) for k, (px, py) in enumerate(chips)]
        return sends, recvs

    def start(ins, outs, sems):
        for cp in copies(ins, outs, sems)[0]:
            cp.start()

    def finish(ins, outs, sems):
        sends, recvs = copies(ins, outs, sems)
        for cp in recvs:
            cp.wait_recv()
        for cp in sends:
            cp.wait_send()

    return _Comm([part], [_shape(part.shape, part.dtype)], {0: 0}, [(3,), (3,)], start, finish, then)


def _phase_pair_exchange(full, then):
    _, R, C = full.shape
    rh = R // 2

    def copy(ins, outs, sems):
        x, y, c, _ = _place()
        return _remote(ins[0].at[:, pl.ds((1 - c) * rh, rh), :], outs[0], sems[0].at[0], sems[1].at[0], (x, y, 1 - c))

    return _Comm([full], [_shape((4, rh, C), full.dtype)], {}, [(1,), (1,)],
                 lambda ins, outs, sems: copy(ins, outs, sems).start(),
                 lambda ins, outs, sems: copy(ins, outs, sems).wait(), then)


def _phase_chip_scatter(parts, then):
    def copies(ins, outs, sems):
        x, y, c, chips = _place()
        me = 2 * x + y
        local = _Staged(ins[0].at[me], outs[0].at[me], sems[3], sems[2])
        sends = [_remote(ins[0].at[2 * px + py], outs[0].at[me], sems[0].at[k], sems[1].at[k], (px, py, c))
                 for k, (px, py) in enumerate(chips)]
        recvs = [_remote(outs[0].at[2 * px + py], outs[0].at[2 * px + py], sems[0].at[k], sems[1].at[k], (px, py, c))
                 for k, (px, py) in enumerate(chips)]
        return local, sends, recvs

    def start(ins, outs, sems):
        local, sends, _ = copies(ins, outs, sems)
        local.start()
        for cp in sends:
            cp.start()

    def finish(ins, outs, sems):
        local, sends, recvs = copies(ins, outs, sems)
        for cp in recvs:
            cp.wait_recv()
        for cp in sends:
            cp.wait_send()
        local.wait()

    return _Comm([parts], [_shape(parts.shape, parts.dtype)], {}, [(3,), (3,), (2,), (parts.shape[1:], parts.dtype)],
                 start, finish, then)


def _phase_pair_allgather(half, layer, depth, into, then):
    rh, C = half.shape

    def copies(ins, outs, sems):
        x, y, c, _ = _place()
        mine = outs[0].at[layer, pl.ds(c * rh, rh), :]
        theirs = outs[0].at[layer, pl.ds((1 - c) * rh, rh), :]
        return (_Staged(ins[0], mine, sems[3], sems[2]),
                _remote(ins[0], mine, sems[0].at[0], sems[1].at[0], (x, y, 1 - c)),
                _remote(theirs, theirs, sems[0].at[0], sems[1].at[0], (x, y, 1 - c)))

    def start(ins, outs, sems):
        local, send, _ = copies(ins, outs, sems)
        local.start()
        send.start()

    def finish(ins, outs, sems):
        local, send, recv = copies(ins, outs, sems)
        recv.wait_recv()
        send.wait_send()
        local.wait()

    ins = [half] if into is None else [half, into]
    return _Comm(ins, [_shape((depth, 2 * rh, C), half.dtype)], {} if into is None else {1: 0},
                 [(1,), (1,), (2,), ((rh, C), half.dtype)], start, finish, then)


def _comm_only(comms, name):
    plan = _Plan()
    for c in comms:
        plan.at(name, c)
    saved, _PLAN[0] = _PLAN[0], plan
    try:
        def body(o_ref):
            o_ref[...] = jnp.zeros_like(o_ref)

        _pallas(body, out_shape=[jax.ShapeDtypeStruct((8, 128), F32)], in_specs=[],
                out_specs=[pl.BlockSpec(memory_space=pltpu.VMEM)], name=name)()
    finally:
        _PLAN[0] = saved


def _row_tile(rows, cols, itemsize=4, budget=1 << 20, mult=8):
    fits = [t for t in range(mult, rows + 1, mult) if rows % t == 0 and t * cols * itemsize <= budget]
    return max(fits) if fits else rows


def _pair_add(full, got, c, name):
    _, L, R, C = full.shape
    rh = R // 2
    tr = _row_tile(rh, C, budget=2 << 20, mult=16)
    nb = rh // tr

    def body(c_ref, a_ref, b_ref, o_ref):
        o_ref[...] = (a_ref[...] + b_ref[...]).astype(o_ref.dtype)

    grid_spec = pltpu.PrefetchScalarGridSpec(
        num_scalar_prefetch=1,
        grid=(4, L, nb),
        in_specs=[pl.BlockSpec((1, 1, tr, C), lambda j, l, i, c_ref: (j, l, c_ref[0] * nb + i, 0)),
                  pl.BlockSpec((1, 1, tr, C), lambda j, l, i, c_ref: (j, l, i, 0))],
        out_specs=pl.BlockSpec((1, 1, tr, C), lambda j, l, i, c_ref: (j, l, i, 0)),
    )
    return _pallas(
        body,
        out_shape=_out((4, L, rh, C), BF16),
        grid_spec=grid_spec,
        compiler_params=_cparams(("parallel", "parallel", "parallel"), 8 * tr * C * 4),
        name=name,
    )(jnp.reshape(c, (1,)).astype(jnp.int32), full, got)


def _sum_chips(parts, name):
    _, L, rh, C = parts.shape
    tr = _row_tile(rh, C, budget=2 << 20, mult=16)

    def body(p_ref, o_ref):
        acc = p_ref[0].astype(F32)
        for j in range(1, 4):
            acc = acc + p_ref[j].astype(F32)
        o_ref[...] = acc

    return _pallas(
        body,
        out_shape=_out((L, rh, C), F32),
        grid=(L, rh // tr),
        in_specs=[pl.BlockSpec((4, 1, tr, C), lambda l, i: (0, l, i, 0))],
        out_specs=pl.BlockSpec((1, tr, C), lambda l, i: (l, i, 0)),
        compiler_params=_cparams(("parallel", "parallel"), 16 * tr * C * 4),
        name=name,
    )(parts)


def _allreduce_small(v):
    R = v.shape[0]

    def body(v_ref, o_ref, slots, send_sems, recv_sems):
        x, y, c, _ = _place()
        me = 4 * x + 2 * y + c
        slots[me] = v_ref[...]
        cps = []
        for r in range(1, 8):
            px, py, pc = x ^ (r >> 2), y ^ ((r >> 1) & 1), c ^ (r & 1)
            cps.append(pltpu.make_async_remote_copy(src_ref=v_ref, dst_ref=slots.at[me], send_sem=send_sems.at[r - 1],
                                                    recv_sem=recv_sems.at[r - 1], device_id=(px, py, pc), device_id_type=MESH))
            cps[-1].start()
        for r in range(1, 8):
            px, py, pc = x ^ (r >> 2), y ^ ((r >> 1) & 1), c ^ (r & 1)
            theirs = slots.at[4 * px + 2 * py + pc]
            pltpu.make_async_remote_copy(src_ref=theirs, dst_ref=theirs, send_sem=send_sems.at[r - 1], recv_sem=recv_sems.at[r - 1],
                                         device_id=(px, py, pc), device_id_type=MESH).wait_recv()
        acc = slots[0]
        for j in range(1, 8):
            acc = acc + slots[j]
        o_ref[...] = acc
        for cp in cps:
            cp.wait_send()

    return pl.pallas_call(
        body,
        out_shape=jax.ShapeDtypeStruct(v.shape, F32),
        in_specs=[pl.BlockSpec(memory_space=pltpu.VMEM)],
        out_specs=pl.BlockSpec(memory_space=pltpu.VMEM),
        scratch_shapes=[pltpu.VMEM((8, R, 128), F32), pltpu.SemaphoreType.DMA((7,)), pltpu.SemaphoreType.DMA((7,))],
        name="allreduce_small",
    )(v)


def _adamw_math(w, g, m, v):
    m = ADAM_B1 * m + (1.0 - ADAM_B1) * g
    v = ADAM_B2 * v + (1.0 - ADAM_B2) * (g * g)
    m_hat = m / (1.0 - ADAM_B1 ** ADAM_STEP)
    v_hat = v / (1.0 - ADAM_B2 ** ADAM_STEP)
    delta = -ADAM_LR * (m_hat / (jnp.sqrt(v_hat) + ADAM_EPS) + ADAM_WD * w)
    return delta, m, v


def _adamw(ws, gs, ms, vs, name, budget=1 << 19):
    n = len(ws)
    tiles = [_row_tile(w.shape[1], w.shape[2], budget=budget) for w in ws]
    per_layer = [w.shape[1] // t for w, t in zip(ws, tiles)]
    steps = [w.shape[0] * p for w, p in zip(ws, per_layer)]
    starts = [sum(steps[:k]) for k in range(n)]

    def body(*refs):
        i = pl.program_id(0)
        for k in range(n):
            w_ref, g_ref, m_ref, v_ref = (refs[j * n + k] for j in range(4))
            outs = [refs[(4 + j) * n + k] for j in range(3)]

            @pl.when((i >= starts[k]) & (i < starts[k] + steps[k]))
            def _(w_ref=w_ref, g_ref=g_ref, m_ref=m_ref, v_ref=v_ref, outs=outs):
                outs[0][...], outs[1][...], outs[2][...] = _adamw_math(w_ref[...], g_ref[...], m_ref[...], v_ref[...])

    def spec(k):
        def index(i):
            local = jnp.clip(i - starts[k], 0, steps[k] - 1)
            return local // per_layer[k], local % per_layer[k], 0
        return pl.BlockSpec((None, tiles[k], ws[k].shape[2]), index)

    specs = [spec(k) for k in range(n)]
    outs = [_out(w.shape, F32) for w in ws]
    res = _pallas(
        body,
        out_shape=tuple(outs * 3),
        grid=(sum(steps),),
        in_specs=specs * 4,
        out_specs=tuple(specs * 3),
        compiler_params=_cparams(("arbitrary",), sum(16 * t * w.shape[2] * 4 for w, t in zip(ws, tiles))),
        name=name,
    )(*ws, *gs, *ms, *vs)
    return res[:n], res[n:2 * n], res[2 * n:]


def _adamw_slabs(w, g, m, v, name, slabs=59):
    n, L, C = w.shape
    assert n % slabs == 0

    def body(w_ref, g_ref, m_ref, v_ref, d_ref, nm_ref, nv_ref):
        d_ref[...], nm_ref[...], nv_ref[...] = _adamw_math(w_ref[...], g_ref[...], m_ref[...], v_ref[...])

    blk = pl.BlockSpec((slabs, L, C), lambda i: (i, 0, 0))
    out = _out(w.shape, F32)
    return _pallas(
        body,
        out_shape=(out, out, out),
        grid=(n // slabs,),
        in_specs=[blk] * 4,
        out_specs=(blk, blk, blk),
        compiler_params=_cparams(("parallel",), 16 * slabs * 8 * C * 4),
        name=name,
    )(w, g, m, v)


def _adamw_small(ws, gs, ms, vs):
    n = len(ws)

    def body(*refs):
        for t in range(n):
            w_ref, g_ref, m_ref, v_ref = (refs[k * n + t] for k in range(4))
            d_ref, nm_ref, nv_ref = (refs[(4 + k) * n + t] for k in range(3))
            d_ref[...], nm_ref[...], nv_ref[...] = _adamw_math(w_ref[...], g_ref[...], m_ref[...], v_ref[...])

    vmem = pl.BlockSpec(memory_space=pltpu.VMEM)
    outs = [jax.ShapeDtypeStruct(w.shape, F32) for w in ws]
    res = pl.pallas_call(
        body,
        out_shape=outs * 3,
        in_specs=[vmem] * (4 * n),
        out_specs=[vmem] * (3 * n),
        name="adamw_small",
    )(*ws, *gs, *ms, *vs)
    return res[:n], res[n:2 * n], res[2 * n:]


IN_SIZES = (192, 192, 384, 384, 16, 512, 384, 384, 384)
IN_OFFS = tuple(int(v) for v in np.cumsum((0,) + IN_SIZES))
SMALL = ("norm_mix", "w_gla_gate", "b_gla_gate", "gla_norm", "b_dw", "conv_ln_g", "conv_ln_b", "rel_bias", "norm_ffn")


def _pad_cols(a, n):
    return jnp.pad(a, ((0, 0), (0, n - a.shape[1])))


W_IN_SHARD = 708
W_IN_ROWS = 736


def _pad_rows(a, n):
    return jnp.pad(a, ((0, n - a.shape[0]), (0, 0)))


def _split_w_in_t(w):
    s = [w[IN_OFFS[i]:IN_OFFS[i + 1]] for i in range(9)]
    w_gla = jnp.concatenate([_pad_rows(s[0], KW), _pad_rows(s[1], KW), s[2], s[3], _pad_rows(s[4], LRW)], axis=0)
    return w_gla, s[5], jnp.concatenate(s[6:9], axis=0)


def _join_w_in_t(g):
    gg = g["w_gla"]
    full = jnp.concatenate([gg[0:192], gg[KW:KW + 192], gg[2 * KW:2 * KW + VW], gg[2 * KW + VW:2 * KW + 2 * VW],
                            gg[2 * KW + 2 * VW:2 * KW + 2 * VW + 16], g["w_conv"], g["w_att"]], axis=0)
    return jnp.pad(full.reshape(4, W_IN_SHARD, D), ((0, 0), (0, W_IN_ROWS - W_IN_SHARD), (0, 0)))


def _pack(arrs, rows):
    flat = jnp.concatenate([a.reshape(-1) for a in arrs])
    return jnp.pad(flat, (0, rows * 128 - flat.shape[0])).reshape(rows, 128)


def _unpack(packed, shapes):
    flat = packed.reshape(-1)
    out, off = [], 0
    for s in shapes:
        n = int(np.prod(s))
        out.append(flat[off:off + n].reshape(s))
        off += n
    return out


def kernel(x, norm_mix, w_in, w_gla_gate, b_gla_gate, gla_norm, w_dw, b_dw, conv_ln_g, conv_ln_b, rel_bias, w_out, norm_ffn, w_up, w_down, norm_final, loss_target, m_norm_mix, m_w_in, m_w_gla_gate, m_b_gla_gate, m_gla_norm, m_w_dw, m_b_dw, m_conv_ln_g, m_conv_ln_b, m_rel_bias, m_w_out, m_norm_ffn, m_w_up, m_w_down, m_norm_final, v_norm_mix, v_w_in, v_w_gla_gate, v_b_gla_gate, v_gla_norm, v_w_dw, v_b_dw, v_conv_ln_g, v_conv_ln_b, v_rel_bias, v_w_out, v_norm_ffn, v_w_up, v_w_down, v_norm_final):
    P = dict(norm_mix=norm_mix, w_in=w_in, w_gla_gate=w_gla_gate, b_gla_gate=b_gla_gate, gla_norm=gla_norm, w_dw=w_dw, b_dw=b_dw,
             conv_ln_g=conv_ln_g, conv_ln_b=conv_ln_b, rel_bias=rel_bias, w_out=w_out, norm_ffn=norm_ffn, w_up=w_up,
             w_down=w_down, norm_final=norm_final)
    Mo = dict(norm_mix=m_norm_mix, w_in=m_w_in, w_gla_gate=m_w_gla_gate, b_gla_gate=m_b_gla_gate, gla_norm=m_gla_norm, w_dw=m_w_dw,
              b_dw=m_b_dw, conv_ln_g=m_conv_ln_g, conv_ln_b=m_conv_ln_b, rel_bias=m_rel_bias, w_out=m_w_out, norm_ffn=m_norm_ffn,
              w_up=m_w_up, w_down=m_w_down, norm_final=m_norm_final)
    Vo = dict(norm_mix=v_norm_mix, w_in=v_w_in, w_gla_gate=v_w_gla_gate, b_gla_gate=v_b_gla_gate, gla_norm=v_gla_norm, w_dw=v_w_dw,
              b_dw=v_b_dw, conv_ln_g=v_conv_ln_g, conv_ln_b=v_conv_ln_b, rel_bias=v_rel_bias, w_out=v_w_out, norm_ffn=v_norm_ffn,
              w_up=v_w_up, w_down=v_w_down, norm_final=v_norm_final)
    depth = w_in.shape[0]
    xi, yi, ci = lax.axis_index("x"), lax.axis_index("y"), lax.axis_index("c")
    chip = 2 * xi + yi

    plan = _Plan()
    _PLAN[0] = plan
    layers = [dict(
        norm_mix=norm_mix[l][None], wg=jnp.pad(w_gla_gate[l], ((0, LRW - 16), (0, KW - 192))),
        bg=_pad_cols(b_gla_gate[l][None], KW), gla_norm=gla_norm[l][None], b_dw=b_dw[l][None], ln_g=conv_ln_g[l][None],
        ln_b=conv_ln_b[l][None], rel_bias=rel_bias[l], norm_ffn=norm_ffn[l][None]) for l in range(depth)]

    w_in_t, m_w_in_t, v_w_in_t = (jnp.transpose(a, (2, 0, 1)) for a in (w_in, m_w_in, v_w_in))

    def w_in_shard(l):
        return _pad_rows(w_in_t[:, l, :], W_IN_ROWS).astype(BF16)

    def have_w_in(l, full):
        rows = jnp.concatenate([full[j, 0:W_IN_SHARD] for j in range(4)], axis=0)
        layers[l]["w_gla"], layers[l]["w_conv"], layers[l]["w_att"] = _split_w_in_t(rows)

    def have_w_out(l, full):
        w = full.reshape(D, D)
        layers[l]["w_out_g"], layers[l]["w_out_c"], layers[l]["w_out_a"] = w[0:VW], w[VW:VW + CW], w[VW + CW:]

    def have_w_up(l, full):
        layers[l]["w_up"] = full

    def have_w_down(l, full):
        layers[l]["w_down"] = full.reshape(D_FF, D)

    def have_w_dw(full):
        taps = full.reshape(4, depth, HALO, CW // 4)
        for l in range(depth):
            layers[l]["w_dw"] = jnp.transpose(taps[:, l], (1, 0, 2)).reshape(HALO, CW)

    first_d2d = []

    def first_ici(shard, have):
        return _phase_gather_ici(shard, lambda outs: first_d2d.append(_phase_gather_d2d(outs[0], lambda done: have(done[0]))))

    w_dw_pad = jnp.pad(w_dw, ((0, 0), (0, HALO - CK), (0, 0))).reshape(depth * HALO, CW // 4)
    _comm_only([first_ici(w_in_shard(0), functools.partial(have_w_in, 0)), first_ici(w_dw_pad, have_w_dw)],
               "gather_first_ici")
    _comm_only(first_d2d, "gather_first_d2d")

    def gather_behind(shard, ici_call, d2d_call, have):
        plan.at(ici_call, _phase_gather_ici(
            shard, lambda outs: plan.at(d2d_call, _phase_gather_d2d(outs[0], lambda done: have(done[0])))))

    for l in range(depth):
        if l > 0:
            gather_behind(w_in_shard(l), f"l{l - 1}_mlp_up", f"l{l - 1}_mlp_down", functools.partial(have_w_in, l))
        gather_behind(w_out[l].astype(BF16), f"l{l - 1}_mlp_down" if l > 0 else "l0_proj", f"l{l}_gla_fwd",
                      functools.partial(have_w_out, l))
        if l > 0:
            gather_behind(w_up[l].astype(BF16), f"l{l}_proj", f"l{l}_gla_fwd", functools.partial(have_w_up, l))
            gather_behind(w_down[l].astype(BF16), f"l{l}_gla_fwd", f"l{l}_att_fwd", functools.partial(have_w_down, l))
        else:
            gather_behind(w_up[l].astype(BF16), f"l{l}_gla_fwd", f"l{l}_att_fwd", functools.partial(have_w_up, l))
            gather_behind(w_down[l].astype(BF16), f"l{l}_att_fwd", f"l{l}_mlp_up", functools.partial(have_w_down, l))

    reduced = {}
    last_swap = []

    def reduce_calls(name, l):
        if name == "w_down":
            return f"l{l}_mlp_up_dx", f"l{l}_gla_bwd", f"l{l}_conv_bwd_dc"
        if name == "w_up":
            return f"l{l}_out_dx", f"l{l}_att_bwd", f"l{l}_proj_dw"
        if name == "w_out":
            return f"l{l}_gla_bwd", f"l{l}_conv_bwd_dc", f"l{l}_att_bwd"
        if l > 0:
            return f"l{l}_proj_dx", f"l{l - 1}_mlp_down_dw", f"l{l - 1}_mlp_up_dx"
        return None, "l0_proj_dx", None

    def reduce_behind(l, name, full):
        calls = reduce_calls(name, l)

        def swapped(outs):
            pair = _pair_add(full[:, None], outs[0][:, None], ci, f"reduce_pair_add_{name}{l}")[:, 0]
            plan.at(calls[1], _phase_chip_scatter(pair, scattered))

        def scattered(outs):
            half = _sum_chips(outs[0][:, None], f"reduce_sum_chips_{name}{l}")[0]
            phase = _phase_pair_allgather(half, l, depth, reduced.get(name), gathered)
            if calls[2] is None:
                last_swap.append(phase)
            else:
                plan.at(calls[2], phase)

        def gathered(outs):
            reduced[name] = outs[0]

        if calls[0] is None:
            _comm_only([_phase_pair_exchange(full, swapped)], f"reduce_pair_exchange_{name}{l}")
        else:
            plan.at(calls[0], _phase_pair_exchange(full, swapped))

    loss_part, grad_x, grads, g_final = _local_step(x[0], loss_target[0], layers, norm_final[None], reduce_behind)

    G, delta, new_m, new_v = {}, {}, {}, {}
    early = ("w_down", "w_up", "w_out")
    for name in early:
        G[name] = reduced[name]
    ds, nms, nvs = _adamw([P[k] for k in early], [G[k] for k in early], [Mo[k] for k in early], [Vo[k] for k in early],
                          "adamw_early")
    for i, name in enumerate(early):
        delta[name], new_m[name], new_v[name] = ds[i], nms[i], nvs[i]
    _PLAN[0] = None
    assert not plan.by_call, sorted(plan.by_call)

    small_g = []
    for l in range(depth):
        g = grads[l]
        small_g += [g["norm_mix"], g["wg"][0:16, 0:192], g["bg"][:, 0:192], g["gla_norm"], g["b_dw"], g["ln_g"], g["ln_b"],
                    g["rel_bias"], g["norm_ffn"], g["w_dw"][0:CK]]
    small_g += [g_final, loss_part]
    small_shapes = [a.shape for a in small_g]
    n_small = sum(int(np.prod(s)) for s in small_shapes)
    rows = -(-n_small // 1024) * 8
    red = _unpack(_allreduce_small(_pack(small_g, rows)), small_shapes)
    per = len(SMALL) + 1
    for i, name in enumerate(SMALL):
        G[name] = jnp.stack([red[l * per + i].reshape(P[name].shape[1:]) for l in range(depth)])
    gw_dw_all = jnp.stack([red[l * per + len(SMALL)] for l in range(depth)])
    G["w_dw"] = lax.dynamic_slice_in_dim(gw_dw_all, chip * (CW // 4), CW // 4, axis=2)
    G["norm_final"] = red[-2].reshape(norm_final.shape)
    loss = red[-1][0, 0]

    _comm_only(last_swap, "reduce_pair_allgather_last")
    back = lambda a: jnp.transpose(a, (1, 2, 0))
    g_in_t = jnp.transpose(reduced["w_in"][:, 0:W_IN_SHARD, :], (1, 0, 2))
    d_in, nm_in, nv_in = _adamw_slabs(w_in_t, g_in_t, m_w_in_t, v_w_in_t, "adamw_w_in")
    G["w_in"], delta["w_in"], new_m["w_in"], new_v["w_in"] = back(g_in_t), back(d_in), back(nm_in), back(nv_in)

    small_names = list(SMALL) + ["w_dw", "norm_final"]
    two_d = lambda a: a.reshape(-1, a.shape[-1])
    ds, nms, nvs = _adamw_small([two_d(P[k]) for k in small_names], [two_d(G[k]) for k in small_names],
                                [two_d(Mo[k]) for k in small_names], [two_d(Vo[k]) for k in small_names])
    for i, name in enumerate(small_names):
        shp = P[name].shape
        delta[name], new_m[name], new_v[name] = ds[i].reshape(shp), nms[i].reshape(shp), nvs[i].reshape(shp)

    order = ["norm_mix", "w_in", "w_gla_gate", "b_gla_gate", "gla_norm", "w_dw", "b_dw", "conv_ln_g", "conv_ln_b", "rel_bias",
             "w_out", "norm_ffn", "w_up", "w_down", "norm_final"]
    return (loss, grad_x[None], *[G[k] for k in order], *[delta[k] for k in order], *[new_m[k] for k in order],
            *[new_v[k] for k in order])
```
